```python
import math
import jax, jax.numpy as jnp
from jax import lax
import numpy as np

D_MODEL = 2048
BATCH = 1
SEQ = 16384
DEPTH = 2

N_MIXERS = 2
N_GDN_LAYERS = (DEPTH + 1) // 2
N_DSA_LAYERS = DEPTH // 2

GDN_QK_HEADS = 16
GDN_V_HEADS = 32
GDN_DK = 128
GDN_DV = 128
GDN_CONV = 4
GDN_CHUNK = 64
GDN_QK_W = GDN_QK_HEADS * GDN_DK
GDN_V_W = GDN_V_HEADS * GDN_DV
GDN_IN_W = 2 * GDN_QK_W + 2 * GDN_V_W + 2 * GDN_V_HEADS

DSA_HEADS = 16
DSA_KV_HEADS = 4
DSA_GROUP = DSA_HEADS // DSA_KV_HEADS
DSA_DH = 128
IDX_HEADS = 16
IDX_DIM = 128
TOPK_MAX = 256
Q_BLOCK = 128
DSA_IN_W = DSA_HEADS * DSA_DH + 2 * DSA_KV_HEADS * DSA_DH + IDX_HEADS * IDX_DIM + IDX_DIM + IDX_HEADS

N_BUCKETS = 32
MAX_DISTANCE = 128

D_FF = 5120
FFN_CONV = 3

PLE_DIM = 256

DN_ALPHA = (2.0 * DEPTH) ** 0.25
DN_BETA = (8.0 * DEPTH) ** -0.25
LN_EPS = 1e-5
RMS_EPS = 1e-6

kernel_name = "hybrid_gdn_dsa_convglu_deepnorm"


def layer_norm(x, g, b):
    xf = x.astype(jnp.float32)
    mu = jnp.mean(xf, -1, keepdims=True)
    var = jnp.mean(jnp.square(xf - mu), -1, keepdims=True)
    y = (xf - mu) * lax.rsqrt(var + LN_EPS) * g.astype(jnp.float32) + b.astype(jnp.float32)
    return y.astype(x.dtype)


def causal_dwconv(x, w):
    K = w.shape[0]
    L = x.shape[1]
    xp = jnp.pad(x, ((0, 0), (K - 1, 0), (0, 0)))
    return sum(xp[:, j:j + L] * w[j] for j in range(K))


def l2norm(x):
    return x * lax.rsqrt(jnp.sum(jnp.square(x), -1, keepdims=True) + RMS_EPS)


def gated_delta_rule(q, k, v, g, beta):
    B, L, H, dk = q.shape
    dv = v.shape[-1]
    C = GDN_CHUNK
    N = L // C

    def chunk(a):
        a = a.reshape(B, N, C, H, *a.shape[3:])
        return jnp.moveaxis(a, 3, 1)

    q, k, v, g, beta = chunk(q), chunk(k), chunk(v), chunk(g), chunk(beta)
    gc = jnp.cumsum(g, axis=-1)
    kb = k * beta[..., None]
    vb = v * beta[..., None]
    tri = jnp.tril(jnp.ones((C, C), dtype=bool))
    strict = jnp.tril(jnp.ones((C, C), dtype=bool), -1)
    diff = gc[..., :, None] - gc[..., None, :]
    decay = jnp.where(tri, jnp.exp(jnp.where(tri, diff, 0.0)), 0.0)
    lmat = jnp.where(strict, jnp.einsum('bhncd,bhnsd->bhncs', kb, k) * decay, 0.0)
    amat = lmat + jnp.eye(C, dtype=jnp.float32)
    rhs = jnp.concatenate([vb, kb * jnp.exp(gc)[..., None]], axis=-1)
    sol = lax.linalg.triangular_solve(amat, rhs, left_side=True, lower=True, unit_diagonal=True)
    u, w = sol[..., :dv], sol[..., dv:]
    qk = jnp.where(tri, jnp.einsum('bhncd,bhnsd->bhncs', q, k) * decay, 0.0)
    q_dec = q * jnp.exp(gc)[..., None]
    k_dec = k * jnp.exp(gc[..., -1:] - gc)[..., None]
    g_last = jnp.exp(gc[..., -1])

    def step(S, inp):
        u_n, w_n, qd_n, kd_n, qk_n, gl_n = inp
        v_new = u_n - jnp.einsum('bhck,bhkv->bhcv', w_n, S)
        o_n = jnp.einsum('bhck,bhkv->bhcv', qd_n, S) + jnp.einsum('bhcs,bhsv->bhcv', qk_n, v_new)
        S = S * gl_n[..., None, None] + jnp.einsum('bhck,bhcv->bhkv', kd_n, v_new)
        return S, o_n

    xs = tuple(jnp.moveaxis(a, 2, 0) for a in (u, w, q_dec, k_dec, qk, g_last))
    S0 = jnp.zeros((B, H, dk, dv), jnp.float32)
    _, o = lax.scan(step, S0, xs)
    o = jnp.transpose(o, (1, 0, 3, 2, 4))
    return o.reshape(B, L, H, dv)


def gdn_mixer(x, w_in, conv_w, a_log, dt_bias, norm_g, w_o):
    B, L, _ = x.shape
    proj = x @ w_in
    qkv, z, a, b = jnp.split(proj, [2 * GDN_QK_W + GDN_V_W, 2 * GDN_QK_W + 2 * GDN_V_W,
                                    2 * GDN_QK_W + 2 * GDN_V_W + GDN_V_HEADS], axis=-1)
    qkv = jax.nn.silu(causal_dwconv(qkv, conv_w)).astype(jnp.float32)
    q, k, v = jnp.split(qkv, [GDN_QK_W, 2 * GDN_QK_W], axis=-1)
    rep = GDN_V_HEADS // GDN_QK_HEADS
    q = l2norm(q.reshape(B, L, GDN_QK_HEADS, GDN_DK)) * (GDN_DK ** -0.5)
    k = l2norm(k.reshape(B, L, GDN_QK_HEADS, GDN_DK))
    q = jnp.repeat(q, rep, axis=2)
    k = jnp.repeat(k, rep, axis=2)
    v = v.reshape(B, L, GDN_V_HEADS, GDN_DV)
    beta = jax.nn.sigmoid(b.astype(jnp.float32))
    g = -jnp.exp(a_log.astype(jnp.float32)) * jax.nn.softplus(a.astype(jnp.float32) + dt_bias.astype(jnp.float32))
    o = gated_delta_rule(q, k, v, g, beta)
    zf = z.astype(jnp.float32).reshape(B, L, GDN_V_HEADS, GDN_DV)
    o = o * lax.rsqrt(jnp.mean(jnp.square(o), -1, keepdims=True) + RMS_EPS) * norm_g.astype(jnp.float32) * jax.nn.silu(zf)
    return o.reshape(B, L, GDN_V_W).astype(x.dtype) @ w_o


def rel_bucket(dist):
    max_exact = N_BUCKETS // 2
    d = jnp.maximum(dist, 0)
    df = jnp.maximum(d, 1).astype(jnp.float32)
    large = max_exact + (jnp.log(df / max_exact) / math.log(MAX_DISTANCE / max_exact)
                         * (N_BUCKETS - max_exact)).astype(jnp.int32)
    large = jnp.minimum(large, N_BUCKETS - 1)
    return jnp.where(d < max_exact, d, large)


def dsa_mixer(x, w_in, kidx_ln_g, kidx_ln_b, rel_bias, w_o):
    B, L, _ = x.shape
    k_top = min(TOPK_MAX, L // 4)
    proj = x @ w_in
    sq = DSA_HEADS * DSA_DH
    skv = DSA_KV_HEADS * DSA_DH
    si = IDX_HEADS * IDX_DIM
    q, k, v, qi, ki, wi = jnp.split(proj, [sq, sq + skv, sq + 2 * skv, sq + 2 * skv + si,
                                           sq + 2 * skv + si + IDX_DIM], axis=-1)
    q = q.reshape(B, L, DSA_HEADS, DSA_DH)
    k = k.reshape(B, L, DSA_KV_HEADS, DSA_DH)
    v = v.reshape(B, L, DSA_KV_HEADS, DSA_DH)
    qi = qi.reshape(B, L, IDX_HEADS, IDX_DIM)
    ki = layer_norm(ki, kidx_ln_g, kidx_ln_b).astype(jnp.float32)
    wi = wi * ((IDX_HEADS ** -0.5) * (IDX_DIM ** -0.5))
    nb = L // Q_BLOCK
    scale = DSA_DH ** -0.5
    key_pos = jnp.arange(L)
    bias_tab = rel_bias.astype(jnp.float32)

    def blocks(a):
        return jnp.moveaxis(a.reshape(B, nb, Q_BLOCK, *a.shape[2:]), 1, 0)

    def attend(args):
        q_b, qi_b, wi_b, s0 = args
        t = s0 + jnp.arange(Q_BLOCK)
        sc = jnp.einsum('bqhd,bsd->bqhs', qi_b.astype(jnp.float32), ki)
        score = jnp.einsum('bqhs,bqh->bqs', jax.nn.relu(sc), wi_b.astype(jnp.float32))
        causal = key_pos[None, :] <= t[:, None]
        score = jnp.where(causal[None], score, -jnp.inf)
        _, idx = lax.top_k(score, k_top)
        k_sel = jax.vmap(lambda kk, ii: kk[ii])(k, idx)
        v_sel = jax.vmap(lambda vv, ii: vv[ii])(v, idx)
        qg = q_b.reshape(B, Q_BLOCK, DSA_KV_HEADS, DSA_GROUP, DSA_DH)
        logits = jnp.einsum('bqkgd,bqskd->bqkgs', qg, k_sel).astype(jnp.float32) * scale
        dist = t[None, :, None] - idx
        bias = bias_tab[rel_bucket(dist)]
        bias = bias.reshape(B, Q_BLOCK, k_top, DSA_KV_HEADS, DSA_GROUP).transpose(0, 1, 3, 4, 2)
        valid = (dist >= 0)[:, :, None, None, :]
        logits = jnp.where(valid, logits + bias, -jnp.inf)
        probs = jax.nn.softmax(logits, axis=-1).astype(v.dtype)
        o = jnp.einsum('bqkgs,bqskd->bqkgd', probs, v_sel)
        return o.reshape(B, Q_BLOCK, DSA_HEADS * DSA_DH)

    starts = jnp.arange(nb, dtype=jnp.int32) * Q_BLOCK
    o = lax.map(attend, (blocks(q), blocks(qi), blocks(wi), starts))
    o = jnp.moveaxis(o, 0, 1).reshape(B, L, DSA_HEADS * DSA_DH)
    return o @ w_o


def conv_glu(x, w_gate, w_up, conv_w, w_down):
    gate = causal_dwconv(x @ w_gate, conv_w)
    return (jax.nn.silu(gate) * (x @ w_up)) @ w_down


def setup_inputs(seed: int = 0) -> dict:
    key = jax.random.key(seed)
    ks = jax.random.split(key, 32)

    def nrm(k, shape, scale):
        return jax.random.normal(k, shape, jnp.float32) * scale

    NA, NB, L = N_GDN_LAYERS, N_DSA_LAYERS, DEPTH
    x = nrm(ks[0], (BATCH, SEQ, D_MODEL), 1.0)
    p = nrm(ks[1], (DEPTH, BATCH, SEQ, PLE_DIM), 1.0)
    gdn_w_in = nrm(ks[2], (NA, D_MODEL, GDN_IN_W), D_MODEL ** -0.5)
    gdn_conv_w = nrm(ks[3], (NA, GDN_CONV, 2 * GDN_QK_W + GDN_V_W), GDN_CONV ** -0.5)
    gdn_a_log = jnp.log(jax.random.uniform(ks[4], (NA, GDN_V_HEADS), jnp.float32, 1.0, 16.0))
    dt = jnp.exp(jax.random.uniform(ks[5], (NA, GDN_V_HEADS), jnp.float32, math.log(1e-3), math.log(1e-1)))
    gdn_dt_bias = dt + jnp.log(-jnp.expm1(-dt))
    gdn_norm_g = 1.0 + nrm(ks[6], (NA, GDN_DV), 0.02)
    gdn_w_o = nrm(ks[7], (NA, GDN_V_W, D_MODEL), DN_BETA * GDN_V_W ** -0.5)
    dsa_w_in = nrm(ks[8], (NB, D_MODEL, DSA_IN_W), D_MODEL ** -0.5)
    dsa_kidx_ln_g = 1.0 + nrm(ks[9], (NB, IDX_DIM), 0.02)
    dsa_kidx_ln_b = nrm(ks[10], (NB, IDX_DIM), 0.02)
    dsa_w_o = nrm(ks[11], (NB, DSA_HEADS * DSA_DH, D_MODEL), DN_BETA * (DSA_HEADS * DSA_DH) ** -0.5)
    rel_bias = nrm(ks[12], (N_BUCKETS, DSA_HEADS), 0.5)
    ln1_g = 1.0 + nrm(ks[13], (L, D_MODEL), 0.02)
    ln1_b = nrm(ks[14], (L, D_MODEL), 0.02)
    ffn_w_gate = nrm(ks[15], (L, D_MODEL, D_FF), D_MODEL ** -0.5)
    ffn_w_up = nrm(ks[16], (L, D_MODEL, D_FF), D_MODEL ** -0.5)
    ffn_conv_w = nrm(ks[17], (L, FFN_CONV, D_FF), FFN_CONV ** -0.5)
    ffn_w_down = nrm(ks[18], (L, D_FF, D_MODEL), DN_BETA * D_FF ** -0.5)
    ln2_g = 1.0 + nrm(ks[19], (L, D_MODEL), 0.02)
    ln2_b = nrm(ks[20], (L, D_MODEL), 0.02)
    ple_w_proj = nrm(ks[21], (L, PLE_DIM, D_MODEL), 0.5 * PLE_DIM ** -0.5)
    ple_w_gate = nrm(ks[22], (L, D_MODEL, D_MODEL), D_MODEL ** -0.5)
    return {"x": x, "p": p,
            "gdn_w_in": gdn_w_in, "gdn_conv_w": gdn_conv_w, "gdn_a_log": gdn_a_log,
            "gdn_dt_bias": gdn_dt_bias, "gdn_norm_g": gdn_norm_g, "gdn_w_o": gdn_w_o,
            "dsa_w_in": dsa_w_in, "dsa_kidx_ln_g": dsa_kidx_ln_g, "dsa_kidx_ln_b": dsa_kidx_ln_b,
            "dsa_w_o": dsa_w_o, "rel_bias": rel_bias,
            "ln1_g": ln1_g, "ln1_b": ln1_b,
            "ffn_w_gate": ffn_w_gate, "ffn_w_up": ffn_w_up, "ffn_conv_w": ffn_conv_w, "ffn_w_down": ffn_w_down,
            "ln2_g": ln2_g, "ln2_b": ln2_b,
            "ple_w_proj": ple_w_proj, "ple_w_gate": ple_w_gate}


def reference(x, p, gdn_w_in, gdn_conv_w, gdn_a_log, gdn_dt_bias, gdn_norm_g, gdn_w_o,
              dsa_w_in, dsa_kidx_ln_g, dsa_kidx_ln_b, dsa_w_o, rel_bias,
              ln1_g, ln1_b, ffn_w_gate, ffn_w_up, ffn_conv_w, ffn_w_down, ln2_g, ln2_b,
              ple_w_proj, ple_w_gate):
    ia = 0
    ib = 0
    for i in range(DEPTH):
        if i % N_MIXERS == 0:
            h = gdn_mixer(x, gdn_w_in[ia], gdn_conv_w[ia], gdn_a_log[ia], gdn_dt_bias[ia],
                          gdn_norm_g[ia], gdn_w_o[ia])
            ia += 1
        else:
            h = dsa_mixer(x, dsa_w_in[ib], dsa_kidx_ln_g[ib], dsa_kidx_ln_b[ib], rel_bias, dsa_w_o[ib])
            ib += 1
        x = layer_norm(DN_ALPHA * x + h, ln1_g[i], ln1_b[i])
        f = conv_glu(x, ffn_w_gate[i], ffn_w_up[i], ffn_conv_w[i], ffn_w_down[i])
        x = layer_norm(DN_ALPHA * x + f, ln2_g[i], ln2_b[i])
        x = x + jax.nn.sigmoid(x @ ple_w_gate[i]) * (p[i] @ ple_w_proj[i])
    return x
```

```python
import functools
import math

import jax
import jax.numpy as jnp
import numpy as np
from jax import lax
from jax.experimental import pallas as pl
from jax.experimental.pallas import tpu as pltpu

F32 = jnp.float32
BF16 = jnp.bfloat16
I32 = jnp.int32

D_MODEL = 2048
GDN_QK_HEADS = 16
GDN_V_HEADS = 32
GDN_DK = 128
GDN_DV = 128
GDN_CONV = 4
GDN_CHUNK = 64
GDN_QK_W = GDN_QK_HEADS * GDN_DK
GDN_V_W = GDN_V_HEADS * GDN_DV
DSA_HEADS = 16
DSA_KV_HEADS = 4
DSA_GROUP = DSA_HEADS // DSA_KV_HEADS
DSA_DH = 128
IDX_HEADS = 16
IDX_DIM = 128
TOPK_MAX = 256
N_BUCKETS = 32
MAX_DISTANCE = 128
D_FF = 5120
FFN_CONV = 3
PLE_DIM = 256
DEPTH = 2
DN_ALPHA = (2.0 * DEPTH) ** 0.25
LN_EPS = 1e-5
RMS_EPS = 1e-6

V7X_VMEM_BYTES = 64 * 1024 * 1024
V7X_VMEM_BUDGET = 56 * 1024 * 1024
LANES = 128
BF16_SUBLANES = 16

HALO = BF16_SUBLANES
Q_BLOCK = 128
IDX_KEY_BLOCK = 2048
IDX_SUB = 512
INT_MIN = -(2 ** 31)

_NT = (((1,), (1,)), ((), ()))
_TN = (((0,), (0,)), ((), ()))


def _cparams(semantics, vmem_bytes):
    return pltpu.CompilerParams(dimension_semantics=semantics,
                                vmem_limit_bytes=int(min(V7X_VMEM_BUDGET, vmem_bytes)))


def _silu(y):
    return y * jax.nn.sigmoid(y)


def _mm_scale_kernel(x_ref, w_ref, cs_ref, o_ref):
    acc = jnp.dot(x_ref[...], w_ref[...], preferred_element_type=F32)
    o_ref[...] = (acc * cs_ref[...]).astype(o_ref.dtype)


def _matmul_scaled(x, w, colscale, out_dtype, tm, tn):
    M, K = x.shape
    N = w.shape[1]
    tm, tn = min(tm, M), min(tn, N)
    osz = jnp.dtype(out_dtype).itemsize
    vmem = 2 * (tm * K * 2 + K * tn * 2 + tm * tn * osz) + 2 * tm * tn * 4
    return pl.pallas_call(
        _mm_scale_kernel,
        grid=(M // tm, N // tn),
        in_specs=[pl.BlockSpec((tm, K), lambda i, j: (i, 0)),
                  pl.BlockSpec((K, tn), lambda i, j: (0, j)),
                  pl.BlockSpec((1, tn), lambda i, j: (0, j))],
        out_specs=pl.BlockSpec((tm, tn), lambda i, j: (i, j)),
        out_shape=jax.ShapeDtypeStruct((M, N), out_dtype),
        compiler_params=_cparams(("parallel", "parallel"), vmem),
        name="matmul_scaled",
    )(x, w, colscale)


def _causal_conv(g, gh, cw_ref, g_scr, kc, tm):
    g_scr[0:HALO, :] = gh
    g_scr[HALO:HALO + tm, :] = g
    y = cw_ref[kc - 1:kc, :] * g
    for j in range(kc - 1):
        off = HALO - (kc - 1) + j
        y = y + cw_ref[j:j + 1, :] * g_scr[off:off + tm, :]
    return y


def _mm_conv_silu_kernel(x_ref, xh_ref, w_ref, cw_ref, o_ref, g_scr, *, kc, tm):
    w = w_ref[...]
    g = jnp.dot(x_ref[...], w, preferred_element_type=F32)
    gh = jnp.dot(xh_ref[...], w, preferred_element_type=F32)
    gh = jnp.where(pl.program_id(0) == 0, 0.0, gh)
    y = _causal_conv(g, gh, cw_ref, g_scr, kc, tm)
    o_ref[...] = _silu(y).astype(o_ref.dtype)


def _proj_conv_silu(x, w, conv_w, tm, tn):
    M, K = x.shape
    N = w.shape[1]
    kc = conv_w.shape[0]
    tm, tn = min(tm, M), min(tn, N)
    hb = tm // HALO
    vmem = 2 * (tm * K * 2 + HALO * K * 2 + K * tn * 2 + tm * tn * 2) + 4 * tm * tn * 4
    return pl.pallas_call(
        functools.partial(_mm_conv_silu_kernel, kc=kc, tm=tm),
        grid=(M // tm, N // tn),
        in_specs=[pl.BlockSpec((tm, K), lambda i, j: (i, 0)),
                  pl.BlockSpec((HALO, K), lambda i, j: (jnp.maximum(i * hb - 1, 0), 0)),
                  pl.BlockSpec((K, tn), lambda i, j: (0, j)),
                  pl.BlockSpec((kc, tn), lambda i, j: (0, j))],
        out_specs=pl.BlockSpec((tm, tn), lambda i, j: (i, j)),
        out_shape=jax.ShapeDtypeStruct((M, N), BF16),
        scratch_shapes=[pltpu.VMEM((tm + HALO, tn), F32)],
        compiler_params=_cparams(("parallel", "parallel"), vmem),
        name="proj_conv_silu",
    )(x, x, w, conv_w)


def _ffn_up_kernel(x_ref, xh_ref, wg_ref, wu_ref, cw_ref, o_ref, g_scr, *, kc, tm):
    wg = wg_ref[...]
    x = x_ref[...]
    g = jnp.dot(x, wg, preferred_element_type=F32)
    gh = jnp.dot(xh_ref[...], wg, preferred_element_type=F32)
    gh = jnp.where(pl.program_id(0) == 0, 0.0, gh)
    y = _causal_conv(g, gh, cw_ref, g_scr, kc, tm)
    u = jnp.dot(x, wu_ref[...], preferred_element_type=F32)
    o_ref[...] = (_silu(y) * u).astype(o_ref.dtype)


def _ffn_up(x, w_gate, w_up, conv_w, tm, tn):
    M, K = x.shape
    N = w_gate.shape[1]
    kc = conv_w.shape[0]
    tm, tn = min(tm, M), min(tn, N)
    hb = tm // HALO
    vmem = 2 * (tm * K * 2 + HALO * K * 2 + 2 * K * tn * 2 + tm * tn * 2) + 6 * tm * tn * 4
    return pl.pallas_call(
        functools.partial(_ffn_up_kernel, kc=kc, tm=tm),
        grid=(M // tm, N // tn),
        in_specs=[pl.BlockSpec((tm, K), lambda i, j: (i, 0)),
                  pl.BlockSpec((HALO, K), lambda i, j: (jnp.maximum(i * hb - 1, 0), 0)),
                  pl.BlockSpec((K, tn), lambda i, j: (0, j)),
                  pl.BlockSpec((K, tn), lambda i, j: (0, j)),
                  pl.BlockSpec((kc, tn), lambda i, j: (0, j))],
        out_specs=pl.BlockSpec((tm, tn), lambda i, j: (i, j)),
        out_shape=jax.ShapeDtypeStruct((M, N), BF16),
        scratch_shapes=[pltpu.VMEM((tm + HALO, tn), F32)],
        compiler_params=_cparams(("parallel", "parallel"), vmem),
        name="ffn_up",
    )(x, x, w_gate, w_up, conv_w)


def _mm_res_ln_kernel(a_ref, w_ref, res_ref, g_ref, b_ref, of_ref, ob_ref, acc_ref):
    k = pl.program_id(1)

    @pl.when(k == 0)
    def _():
        acc_ref[...] = jnp.zeros_like(acc_ref)

    acc_ref[...] += jnp.dot(a_ref[...], w_ref[...], preferred_element_type=F32)

    @pl.when(k == pl.num_programs(1) - 1)
    def _():
        y = DN_ALPHA * res_ref[...] + acc_ref[...]
        mu = jnp.mean(y, axis=-1, keepdims=True)
        yc = y - mu
        var = jnp.mean(yc * yc, axis=-1, keepdims=True)
        out = yc * lax.rsqrt(var + LN_EPS) * g_ref[...] + b_ref[...]
        of_ref[...] = out
        ob_ref[...] = out.astype(BF16)


def _proj_res_ln(a, w, res, g, b, tm, tk):
    M, K = a.shape
    N = w.shape[1]
    tm, tk = min(tm, M), min(tk, K)
    vmem = 2 * (tm * tk * 2 + tk * N * 2 + tm * N * 4 + tm * N * 4 + tm * N * 2) + 3 * tm * N * 4
    return pl.pallas_call(
        _mm_res_ln_kernel,
        grid=(M // tm, K // tk),
        in_specs=[pl.BlockSpec((tm, tk), lambda i, k: (i, k)),
                  pl.BlockSpec((tk, N), lambda i, k: (k, 0)),
                  pl.BlockSpec((tm, N), lambda i, k: (i, 0)),
                  pl.BlockSpec((1, N), lambda i, k: (0, 0)),
                  pl.BlockSpec((1, N), lambda i, k: (0, 0))],
        out_specs=[pl.BlockSpec((tm, N), lambda i, k: (i, 0)),
                   pl.BlockSpec((tm, N), lambda i, k: (i, 0))],
        out_shape=[jax.ShapeDtypeStruct((M, N), F32), jax.ShapeDtypeStruct((M, N), BF16)],
        scratch_shapes=[pltpu.VMEM((tm, N), F32)],
        compiler_params=_cparams(("parallel", "arbitrary"), vmem),
        name="proj_res_ln",
    )(a, w, res, g.reshape(1, N), b.reshape(1, N))


def _ple_kernel(xb_ref, wg_ref, p_ref, wp_ref, xr_ref, of_ref, ob_ref):
    gate = jax.nn.sigmoid(jnp.dot(xb_ref[...], wg_ref[...], preferred_element_type=F32))
    pe = jnp.dot(p_ref[...].astype(BF16), wp_ref[...], preferred_element_type=F32)
    out = xr_ref[...] + gate * pe
    of_ref[...] = out
    ob_ref[...] = out.astype(BF16)


def _ple(xb, xf, w_gate, p, w_proj, tm, tn):
    M, K = xb.shape
    N = w_gate.shape[1]
    P = p.shape[1]
    tm, tn = min(tm, M), min(tn, N)
    vmem = 2 * (tm * K * 2 + K * tn * 2 + tm * P * 4 + P * tn * 2 + tm * tn * 10) + 4 * tm * tn * 4
    return pl.pallas_call(
        _ple_kernel,
        grid=(M // tm, N // tn),
        in_specs=[pl.BlockSpec((tm, K), lambda i, j: (i, 0)),
                  pl.BlockSpec((K, tn), lambda i, j: (0, j)),
                  pl.BlockSpec((tm, P), lambda i, j: (i, 0)),
                  pl.BlockSpec((P, tn), lambda i, j: (0, j)),
                  pl.BlockSpec((tm, tn), lambda i, j: (i, j))],
        out_specs=[pl.BlockSpec((tm, tn), lambda i, j: (i, j)),
                   pl.BlockSpec((tm, tn), lambda i, j: (i, j))],
        out_shape=[jax.ShapeDtypeStruct((M, N), F32), jax.ShapeDtypeStruct((M, N), BF16)],
        compiler_params=_cparams(("parallel", "parallel"), vmem),
        name="ple",
    )(xb, w_gate, p, w_proj, xf)


GDN_HB = 4
GDN_NC = 4


def _gdn_kernel(q_ref, k_ref, v_ref, z_ref, sc_ref, hp_ref, ng_ref, o_ref, s_scr):
    C = GDN_CHUNK

    @pl.when(pl.program_id(1) == 0)
    def _():
        s_scr[...] = jnp.zeros_like(s_scr)

    row = lax.broadcasted_iota(I32, (C, C), 0)
    col = lax.broadcasted_iota(I32, (C, C), 1)
    tri = row >= col
    strict = row > col
    eye = row == col
    tri_f = tri.astype(F32)
    eye_f = eye.astype(F32)

    raw = sc_ref[...]
    a_log = hp_ref[0, 0:1, :]
    dt_b = hp_ref[0, 1:2, :]
    xs = raw + dt_b
    softplus = jnp.maximum(xs, 0.0) + jnp.log1p(jnp.exp(-jnp.abs(xs)))
    g_all = -jnp.exp(a_log) * softplus
    beta_all = jax.nn.sigmoid(raw)
    ng = ng_ref[...]

    for c in range(GDN_NC):
        r0 = c * C
        gc = jnp.dot(tri_f, g_all[r0:r0 + C, :], precision=lax.Precision.HIGHEST,
                     preferred_element_type=F32)
        g_last = gc[C - 1:C, :]
        e_gc = jnp.exp(gc)
        e_rest = jnp.exp(g_last - gc)
        beta_c = beta_all[r0:r0 + C, :]
        qn, kn = [], []
        for hq in range(GDN_HB // 2):
            qf = q_ref[r0:r0 + C, hq * GDN_DK:(hq + 1) * GDN_DK].astype(F32)
            kf = k_ref[r0:r0 + C, hq * GDN_DK:(hq + 1) * GDN_DK].astype(F32)
            qn.append(qf * lax.rsqrt(jnp.sum(qf * qf, axis=-1, keepdims=True) + RMS_EPS) * (GDN_DK ** -0.5))
            kn.append(kf * lax.rsqrt(jnp.sum(kf * kf, axis=-1, keepdims=True) + RMS_EPS))
        for j in range(GDN_HB):
            q_h, k_h = qn[j // 2], kn[j // 2]
            k_bf = k_h.astype(BF16)
            vf = v_ref[r0:r0 + C, j * GDN_DV:(j + 1) * GDN_DV].astype(F32)
            beta = beta_c[:, GDN_HB + j:GDN_HB + j + 1]
            gc_j = gc[:, j:j + 1]
            kb = k_h * beta
            vb = vf * beta
            gcb = jnp.broadcast_to(gc_j, (C, C))
            gcr = jnp.sum(jnp.where(eye, gcb, 0.0), axis=0, keepdims=True)
            decay = jnp.where(tri, jnp.exp(jnp.where(tri, gcb - gcr, 0.0)), 0.0)
            kk = lax.dot_general(kb.astype(BF16), k_bf, _NT, preferred_element_type=F32)
            x = -jnp.where(strict, kk * decay, 0.0)
            t_inv = eye_f + x
            for _ in range(5):
                xb16 = x.astype(BF16)
                x = jnp.dot(xb16, xb16, preferred_element_type=F32)
                t_inv = t_inv + jnp.dot(t_inv.astype(BF16), x.astype(BF16), preferred_element_type=F32)
            rhs = jnp.concatenate([vb, kb * e_gc[:, j:j + 1]], axis=-1).astype(BF16)
            sol = jnp.dot(t_inv.astype(BF16), rhs, preferred_element_type=F32)
            u, w = sol[:, :GDN_DV], sol[:, GDN_DV:]
            qk = jnp.where(tri, lax.dot_general(q_h.astype(BF16), k_bf, _NT,
                                                preferred_element_type=F32) * decay, 0.0)
            q_dec = (q_h * e_gc[:, j:j + 1]).astype(BF16)
            k_dec = (k_h * e_rest[:, j:j + 1]).astype(BF16)
            s_old = s_scr[j]
            s_bf = s_old.astype(BF16)
            v_new = u - jnp.dot(w.astype(BF16), s_bf, preferred_element_type=F32)
            v_new_bf = v_new.astype(BF16)
            o = (jnp.dot(q_dec, s_bf, preferred_element_type=F32)
                 + jnp.dot(qk.astype(BF16), v_new_bf, preferred_element_type=F32))
            s_scr[j] = (s_old * jnp.exp(g_last[:, j:j + 1])
                        + lax.dot_general(k_dec, v_new_bf, _TN, preferred_element_type=F32))
            zf = z_ref[r0:r0 + C, j * GDN_DV:(j + 1) * GDN_DV].astype(F32)
            o = o * lax.rsqrt(jnp.mean(o * o, axis=-1, keepdims=True) + RMS_EPS) * ng * _silu(zf)
            o_ref[r0:r0 + C, j * GDN_DV:(j + 1) * GDN_DV] = o.astype(o_ref.dtype)


def _gdn_core(qkv, z, scal, hparams, norm_g):
    L = qkv.shape[0]
    G = GDN_V_HEADS // GDN_HB
    R = GDN_NC * GDN_CHUNK
    qw = (GDN_HB // 2) * GDN_DK
    vw = GDN_HB * GDN_DV
    k_blk0 = GDN_QK_W // qw
    v_blk0 = 2 * GDN_QK_W // vw
    vmem = 2 * (2 * R * qw * 2 + 2 * R * vw * 2 + R * LANES * 4 + R * vw * 2) + (16 << 20)
    return pl.pallas_call(
        _gdn_kernel,
        grid=(G, L // R),
        in_specs=[pl.BlockSpec((R, qw), lambda g, s: (s, g)),
                  pl.BlockSpec((R, qw), lambda g, s: (s, k_blk0 + g)),
                  pl.BlockSpec((R, vw), lambda g, s: (s, v_blk0 + g)),
                  pl.BlockSpec((R, vw), lambda g, s: (s, g)),
                  pl.BlockSpec((R, LANES), lambda g, s: (s, g)),
                  pl.BlockSpec((1, 8, LANES), lambda g, s: (g, 0, 0)),
                  pl.BlockSpec((1, GDN_DV), lambda g, s: (0, 0))],
        out_specs=pl.BlockSpec((R, vw), lambda g, s: (s, g)),
        out_shape=jax.ShapeDtypeStruct((L, GDN_V_W), BF16),
        scratch_shapes=[pltpu.VMEM((GDN_HB, GDN_DK, GDN_DV), F32)],
        compiler_params=_cparams(("parallel", "arbitrary"), vmem),
        name="gdn_core",
    )(qkv, qkv, qkv, z, scal, hparams, norm_g.reshape(1, GDN_DV))


def _gdn_mixer(xb, w_in, conv_w, a_log, dt_bias, norm_g):
    L = xb.shape[0]
    nqkv = 2 * GDN_QK_W + GDN_V_W
    w_qkv = w_in[:, :nqkv].astype(BF16)
    w_z = w_in[:, nqkv:nqkv + GDN_V_W].astype(BF16)
    w_ab = w_in[:, nqkv + GDN_V_W:]
    G = GDN_V_HEADS // GDN_HB
    w_a = w_ab[:, :GDN_V_HEADS].reshape(D_MODEL, G, GDN_HB)
    w_b = w_ab[:, GDN_V_HEADS:].reshape(D_MODEL, G, GDN_HB)
    w_sc = jnp.concatenate([w_a, w_b, jnp.zeros((D_MODEL, G, LANES - 2 * GDN_HB), F32)], axis=-1)
    w_sc = w_sc.reshape(D_MODEL, G * LANES).astype(BF16)
    qkv = _proj_conv_silu(xb, w_qkv, conv_w, tm=1024, tn=1024)
    ones_z = jnp.ones((1, GDN_V_W), F32)
    z = _matmul_scaled(xb, w_z, ones_z, BF16, tm=1024, tn=1024)
    scal = _matmul_scaled(xb, w_sc, jnp.ones((1, G * LANES), F32), F32, tm=1024, tn=G * LANES)
    hp = jnp.zeros((G, 8, LANES), F32)
    hp = hp.at[:, 0, :GDN_HB].set(a_log.reshape(G, GDN_HB))
    hp = hp.at[:, 1, :GDN_HB].set(dt_bias.reshape(G, GDN_HB))
    return _gdn_core(qkv, z, scal, hp, norm_g)


def _dsa_small_kernel(x_ref, w_ref, g_ref, b_ref, ki_ref, wi_ref):
    acc = jnp.dot(x_ref[...], w_ref[...], preferred_element_type=F32)
    ki = acc[:, :IDX_DIM]
    mu = jnp.mean(ki, axis=-1, keepdims=True)
    kc = ki - mu
    var = jnp.mean(kc * kc, axis=-1, keepdims=True)
    ki_ref[...] = (kc * lax.rsqrt(var + LN_EPS) * g_ref[...] + b_ref[...]).astype(ki_ref.dtype)
    wi_ref[...] = acc[:, IDX_DIM:] * ((IDX_HEADS ** -0.5) * (IDX_DIM ** -0.5))


def _dsa_small(xb, w_small, ln_g, ln_b, tm):
    M, K = xb.shape
    tm = min(tm, M)
    N = 2 * LANES
    vmem = 2 * (tm * K * 2 + K * N * 2 + tm * LANES * 6) + 4 * tm * N * 4
    return pl.pallas_call(
        _dsa_small_kernel,
        grid=(M // tm,),
        in_specs=[pl.BlockSpec((tm, K), lambda i: (i, 0)),
                  pl.BlockSpec((K, N), lambda i: (0, 0)),
                  pl.BlockSpec((1, IDX_DIM), lambda i: (0, 0)),
                  pl.BlockSpec((1, IDX_DIM), lambda i: (0, 0))],
        out_specs=[pl.BlockSpec((tm, IDX_DIM), lambda i: (i, 0)),
                   pl.BlockSpec((tm, LANES), lambda i: (i, 0))],
        out_shape=[jax.ShapeDtypeStruct((M, IDX_DIM), BF16), jax.ShapeDtypeStruct((M, LANES), F32)],
        compiler_params=_cparams(("parallel",), vmem),
        name="dsa_idx_proj",
    )(xb, w_small, ln_g.reshape(1, IDX_DIM), ln_b.reshape(1, IDX_DIM))


def _sortable_key(score):
    bits = lax.bitcast_convert_type(score, I32)
    return jnp.where(bits >= 0, bits, bits ^ jnp.int32(0x7FFFFFFF))


def _idx_kernel(qi_tab, kj_tab, last_tab,
                qidx_ref, kidx_ref, wi_ref, far_ref, near_ref, key_scr, *, k_top, n_sub_total):
    s = pl.program_id(0)
    i = qi_tab[s]
    j = kj_tab[s]
    nsub = IDX_KEY_BLOCK // IDX_SUB
    t_col = i * Q_BLOCK + lax.broadcasted_iota(I32, (Q_BLOCK, 1), 0)
    lane_sub = lax.broadcasted_iota(I32, (Q_BLOCK, IDX_SUB), 1)
    wi = wi_ref[...]

    for sub in range(nsub):
        ki_sub = kidx_ref[sub * IDX_SUB:(sub + 1) * IDX_SUB, :]
        acc = jnp.zeros((Q_BLOCK, IDX_SUB), F32)
        for h in range(IDX_HEADS):
            sc = lax.dot_general(qidx_ref[:, h * IDX_DIM:(h + 1) * IDX_DIM], ki_sub, _NT,
                                 preferred_element_type=F32)
            acc = acc + jnp.maximum(sc, 0.0) * wi[:, h:h + 1]
        s_idx = j * IDX_KEY_BLOCK + sub * IDX_SUB + lane_sub
        key_scr[j * nsub + sub] = jnp.where(s_idx <= t_col, _sortable_key(acc), INT_MIN)

    @pl.when(last_tab[s] == 1)
    def _():
        n_chunks = (j + 1) * nsub

        def count(pred, ref_val):
            refb = jnp.broadcast_to(ref_val, (Q_BLOCK, LANES))

            def body(c, cnt):
                blk = key_scr[c]
                for l in range(IDX_SUB // LANES):
                    cnt = cnt + jnp.where(pred(blk[:, l * LANES:(l + 1) * LANES], refb), 1, 0)
                return cnt

            cnt = lax.fori_loop(0, n_chunks, body, jnp.zeros((Q_BLOCK, LANES), I32))
            return jnp.sum(cnt, axis=1, keepdims=True)

        def bit_body(b, thr):
            cand = thr + jnp.left_shift(jnp.int32(1), 31 - b)
            return jnp.where(count(lambda a, r: a >= r, cand) >= k_top, cand, thr)

        thr = lax.fori_loop(0, 32, bit_body, jnp.full((Q_BLOCK, 1), INT_MIN, I32))
        n_gt = count(lambda a, r: a > r, thr)
        need_eq = (k_top - n_gt).astype(F32)

        thr_b = jnp.broadcast_to(thr, (Q_BLOCK, IDX_SUB))
        incl = (lax.broadcasted_iota(I32, (IDX_SUB, IDX_SUB), 0)
                <= lax.broadcasted_iota(I32, (IDX_SUB, IDX_SUB), 1)).astype(BF16)

        def sel_body(c, carry):
            blk = key_scr[c]
            eq = blk == thr_b
            eq_f = jnp.where(eq, 1.0, 0.0)
            rank = carry + jnp.dot(eq_f.astype(BF16), incl, preferred_element_type=F32)
            s_idx = c * IDX_SUB + lane_sub
            sel = ((blk > thr_b) | (eq & (rank <= need_eq))) & (s_idx <= t_col)
            far = sel & (t_col - s_idx >= MAX_DISTANCE)
            far_ref[0, c] = jnp.where(far, 0.0, -jnp.inf).astype(far_ref.dtype)
            key_scr[c] = jnp.where(sel, 1, 0)
            return carry + jnp.sum(eq_f, axis=1, keepdims=True)

        lax.fori_loop(0, n_chunks, sel_body, jnp.zeros((Q_BLOCK, 1), F32))

        def fill_body(c, carry):
            far_ref[0, c] = jnp.full((Q_BLOCK, IDX_SUB), -jnp.inf, far_ref.dtype)
            return carry

        lax.fori_loop(n_chunks, n_sub_total, fill_body, 0)

        def window(blk_idx):
            per = IDX_SUB // Q_BLOCK
            chunk = key_scr[blk_idx // per]
            m = blk_idx % per
            out = chunk[:, 0:Q_BLOCK]
            for q in range(1, per):
                out = jnp.where(m == q, chunk[:, q * Q_BLOCK:(q + 1) * Q_BLOCK], out)
            return out

        r_i = lax.broadcasted_iota(I32, (Q_BLOCK, Q_BLOCK), 0)
        c_i = lax.broadcasted_iota(I32, (Q_BLOCK, Q_BLOCK), 1)
        d_lo = Q_BLOCK + r_i - c_i
        d_hi = r_i - c_i
        near_lo = (window(jnp.maximum(i - 1, 0)) != 0) & (d_lo < MAX_DISTANCE) & (i >= 1)
        near_hi = (window(i) != 0) & (d_hi >= 0) & (d_hi < MAX_DISTANCE)
        near_ref[:, 0:Q_BLOCK] = jnp.where(near_lo, 0.0, -jnp.inf).astype(near_ref.dtype)
        near_ref[:, Q_BLOCK:2 * Q_BLOCK] = jnp.where(near_hi, 0.0, -jnp.inf).astype(near_ref.dtype)


def _idx_select(proj, ki, wi, k_top):
    L = ki.shape[0]
    nq = L // Q_BLOCK
    n_sub_total = L // IDX_SUB
    qi_l, kj_l, last_l = [], [], []
    for i in range(nq):
        j_last = (i * Q_BLOCK + Q_BLOCK - 1) // IDX_KEY_BLOCK
        for j in range(j_last + 1):
            qi_l.append(i)
            kj_l.append(j)
            last_l.append(1 if j == j_last else 0)
    tabs = [jnp.asarray(np.asarray(t, np.int32)) for t in (qi_l, kj_l, last_l)]
    qcol = (DSA_HEADS * DSA_DH) // (IDX_HEADS * IDX_DIM)
    vmem = (L * Q_BLOCK * 4 + 2 * (Q_BLOCK * L * 2) + 2 * (Q_BLOCK * IDX_HEADS * IDX_DIM * 2)
            + 2 * IDX_KEY_BLOCK * IDX_DIM * 2 + (8 << 20))
    grid_spec = pltpu.PrefetchScalarGridSpec(
        num_scalar_prefetch=3,
        grid=(len(qi_l),),
        in_specs=[pl.BlockSpec((Q_BLOCK, IDX_HEADS * IDX_DIM), lambda s, qt, kt, lt: (qt[s], qcol)),
                  pl.BlockSpec((IDX_KEY_BLOCK, IDX_DIM), lambda s, qt, kt, lt: (kt[s], 0)),
                  pl.BlockSpec((Q_BLOCK, LANES), lambda s, qt, kt, lt: (qt[s], 0))],
        out_specs=[pl.BlockSpec((1, n_sub_total, Q_BLOCK, IDX_SUB), lambda s, qt, kt, lt: (qt[s], 0, 0, 0)),
                   pl.BlockSpec((Q_BLOCK, 2 * Q_BLOCK), lambda s, qt, kt, lt: (qt[s], 0))],
        scratch_shapes=[pltpu.VMEM((n_sub_total, Q_BLOCK, IDX_SUB), I32)],
    )
    return pl.pallas_call(
        functools.partial(_idx_kernel, k_top=k_top, n_sub_total=n_sub_total),
        grid_spec=grid_spec,
        out_shape=[jax.ShapeDtypeStruct((nq, n_sub_total, Q_BLOCK, IDX_SUB), BF16),
                   jax.ShapeDtypeStruct((L, 2 * Q_BLOCK), BF16)],
        compiler_params=_cparams(("arbitrary",), vmem),
        name="dsa_idx_select",
    )(*tabs, proj, ki, wi)


M_INIT = -1e30


def _attn_kernel(qi_tab, kj_tab, kind_tab, first_tab,
                 tab_ref, q_ref, kf_ref, vf_ref, klo_ref, khi_ref, vlo_ref, vhi_ref, far_ref, near_ref,
                 o_ref, m_scr, l_scr, acc_scr, b_scr):
    s = pl.program_id(0)

    @pl.when(s == 0)
    def _():
        r_i = lax.broadcasted_iota(I32, (Q_BLOCK, 2 * Q_BLOCK), 0)
        c_i = lax.broadcasted_iota(I32, (Q_BLOCK, 2 * Q_BLOCK), 1)
        d = jnp.maximum(Q_BLOCK + r_i - c_i, 0)
        max_exact = N_BUCKETS // 2
        df = jnp.maximum(d, 1).astype(F32)
        large = max_exact + (jnp.log(df / max_exact) / math.log(MAX_DISTANCE / max_exact)
                             * (N_BUCKETS - max_exact)).astype(I32)
        large = jnp.minimum(large, N_BUCKETS - 1)
        bkt = jnp.where(d < max_exact, d, large)
        for h in range(DSA_HEADS):
            acc = jnp.zeros((Q_BLOCK, 2 * Q_BLOCK), F32)
            for b in range(N_BUCKETS):
                acc = jnp.where(bkt == b, tab_ref[b, h], acc)
            b_scr[h] = acc

    @pl.when(first_tab[s] == 1)
    def _():
        m_scr[...] = jnp.full_like(m_scr, M_INIT)
        l_scr[...] = jnp.zeros_like(l_scr)
        acc_scr[...] = jnp.zeros_like(acc_scr)

    def update(h, logits, v):
        m_prev = m_scr[h]
        m_new = jnp.maximum(m_prev, jnp.max(logits, axis=-1, keepdims=True))
        alpha = jnp.exp(m_prev - m_new)
        p = jnp.exp(logits - m_new[:, 0:1])
        l_scr[h] = alpha * l_scr[h] + jnp.sum(p, axis=-1, keepdims=True)
        acc_scr[h] = alpha * acc_scr[h] + jnp.dot(p.astype(BF16), v, preferred_element_type=F32)
        m_scr[h] = m_new

    @pl.when(kind_tab[s] == 0)
    def _():
        mask = far_ref[0, 0].astype(F32)
        for g in range(DSA_KV_HEADS):
            kg = kf_ref[:, g * DSA_DH:(g + 1) * DSA_DH]
            vg = vf_ref[:, g * DSA_DH:(g + 1) * DSA_DH]
            for hh in range(DSA_GROUP):
                h = g * DSA_GROUP + hh
                logits = lax.dot_general(q_ref[:, h * DSA_DH:(h + 1) * DSA_DH], kg, _NT,
                                         preferred_element_type=F32)
                update(h, logits + mask + tab_ref[N_BUCKETS - 1, h], vg)

    @pl.when(kind_tab[s] == 1)
    def _():
        for half, (k_ref, v_ref) in enumerate(((klo_ref, vlo_ref), (khi_ref, vhi_ref))):
            mask = near_ref[:, half * Q_BLOCK:(half + 1) * Q_BLOCK].astype(F32)
            for g in range(DSA_KV_HEADS):
                kg = k_ref[:, g * DSA_DH:(g + 1) * DSA_DH]
                vg = v_ref[:, g * DSA_DH:(g + 1) * DSA_DH]
                for hh in range(DSA_GROUP):
                    h = g * DSA_GROUP + hh
                    logits = lax.dot_general(q_ref[:, h * DSA_DH:(h + 1) * DSA_DH], kg, _NT,
                                             preferred_element_type=F32)
                    bias = b_scr[h][:, half * Q_BLOCK:(half + 1) * Q_BLOCK]
                    update(h, logits + mask + bias, vg)
        for h in range(DSA_HEADS):
            o_ref[:, h * DSA_DH:(h + 1) * DSA_DH] = (acc_scr[h] / l_scr[h]).astype(o_ref.dtype)


def _masked_attention(proj, far, near, rel_bias):
    L = proj.shape[0]
    nq = L // Q_BLOCK
    per = IDX_SUB // Q_BLOCK
    qi_l, kj_l, kind_l, first_l = [], [], [], []
    for i in range(nq):
        n_far = -(-i // per)
        for j in range(n_far):
            qi_l.append(i); kj_l.append(j); kind_l.append(0); first_l.append(1 if j == 0 else 0)
        qi_l.append(i); kj_l.append(max(n_far - 1, 0)); kind_l.append(1); first_l.append(1 if n_far == 0 else 0)
    tabs = [jnp.asarray(np.asarray(t, np.int32)) for t in (qi_l, kj_l, kind_l, first_l)]
    qw = DSA_HEADS * DSA_DH
    kvw = DSA_KV_HEADS * DSA_DH
    k_col = (2 * qw) // kvw
    v_col = k_col + 1
    hw = DSA_HEADS
    vmem = (2 * (Q_BLOCK * qw * 2 * 2 + 2 * IDX_SUB * kvw * 2 + 4 * Q_BLOCK * kvw * 2
                 + Q_BLOCK * IDX_SUB * 2 + Q_BLOCK * 2 * Q_BLOCK * 2)
            + hw * Q_BLOCK * (3 * LANES + 2 * Q_BLOCK) * 4 + (16 << 20))
    idx = lambda f: (lambda s, qt, kt, kd, ft: f(qt[s], kt[s]))
    grid_spec = pltpu.PrefetchScalarGridSpec(
        num_scalar_prefetch=4,
        grid=(len(qi_l),),
        in_specs=[pl.BlockSpec(memory_space=pltpu.SMEM),
                  pl.BlockSpec((Q_BLOCK, qw), idx(lambda i, j: (i, 0))),
                  pl.BlockSpec((IDX_SUB, kvw), idx(lambda i, j: (j, k_col))),
                  pl.BlockSpec((IDX_SUB, kvw), idx(lambda i, j: (j, v_col))),
                  pl.BlockSpec((Q_BLOCK, kvw), idx(lambda i, j: (jnp.maximum(i - 1, 0), k_col))),
                  pl.BlockSpec((Q_BLOCK, kvw), idx(lambda i, j: (i, k_col))),
                  pl.BlockSpec((Q_BLOCK, kvw), idx(lambda i, j: (jnp.maximum(i - 1, 0), v_col))),
                  pl.BlockSpec((Q_BLOCK, kvw), idx(lambda i, j: (i, v_col))),
                  pl.BlockSpec((1, 1, Q_BLOCK, IDX_SUB), idx(lambda i, j: (i, j, 0, 0))),
                  pl.BlockSpec((Q_BLOCK, 2 * Q_BLOCK), idx(lambda i, j: (i, 0)))],
        out_specs=pl.BlockSpec((Q_BLOCK, qw), idx(lambda i, j: (i, 0))),
        scratch_shapes=[pltpu.VMEM((hw, Q_BLOCK, LANES), F32),
                        pltpu.VMEM((hw, Q_BLOCK, LANES), F32),
                        pltpu.VMEM((hw, Q_BLOCK, DSA_DH), F32),
                        pltpu.VMEM((hw, Q_BLOCK, 2 * Q_BLOCK), F32)],
    )
    return pl.pallas_call(
        _attn_kernel,
        grid_spec=grid_spec,
        out_shape=jax.ShapeDtypeStruct((L, qw), BF16),
        compiler_params=_cparams(("arbitrary",), vmem),
        name="dsa_attention",
    )(*tabs, rel_bias, proj, proj, proj, proj, proj, proj, proj, far, near)


def _dsa_mixer(xb, w_in, ln_g, ln_b, rel_bias):
    L = xb.shape[0]
    k_top = min(TOPK_MAX, L // 4)
    sq = DSA_HEADS * DSA_DH
    skv = DSA_KV_HEADS * DSA_DH
    si = IDX_HEADS * IDX_DIM
    w_q = w_in[:, :sq]
    w_k = w_in[:, sq:sq + skv]
    w_v = w_in[:, sq + skv:sq + 2 * skv]
    w_qi = w_in[:, sq + 2 * skv:sq + 2 * skv + si]
    w_ki = w_in[:, sq + 2 * skv + si:sq + 2 * skv + si + IDX_DIM]
    w_wi = w_in[:, sq + 2 * skv + si + IDX_DIM:]
    w_main = jnp.concatenate([w_q, w_qi, w_k, w_v], axis=1).astype(BF16)
    colscale = jnp.concatenate([jnp.full((1, sq), DSA_DH ** -0.5, F32),
                                jnp.ones((1, si + 2 * skv), F32)], axis=1)
    w_small = jnp.concatenate([w_ki, w_wi, jnp.zeros((D_MODEL, LANES - IDX_HEADS), F32)], axis=1).astype(BF16)
    proj = _matmul_scaled(xb, w_main, colscale, BF16, tm=1024, tn=1024)
    ki, wi = _dsa_small(xb, w_small, ln_g, ln_b, tm=1024)
    far, near = _idx_select(proj, ki, wi, k_top)
    return _masked_attention(proj, far, near, rel_bias)


def kernel(x, p, gdn_w_in, gdn_conv_w, gdn_a_log, gdn_dt_bias, gdn_norm_g, gdn_w_o, dsa_w_in, dsa_kidx_ln_g, dsa_kidx_ln_b, dsa_w_o, rel_bias, ln1_g, ln1_b, ffn_w_gate, ffn_w_up, ffn_conv_w, ffn_w_down, ln2_g, ln2_b, ple_w_proj, ple_w_gate):
    assert x.shape[0] == 1 and x.shape[2] == D_MODEL
    xf = x[0]
    xb = xf.astype(BF16)
    ia = ib = 0
    for i in range(DEPTH):
        if i % 2 == 0:
            mix = _gdn_mixer(xb, gdn_w_in[ia], gdn_conv_w[ia], gdn_a_log[ia], gdn_dt_bias[ia], gdn_norm_g[ia])
            w_o = gdn_w_o[ia]
            ia += 1
        else:
            mix = _dsa_mixer(xb, dsa_w_in[ib], dsa_kidx_ln_g[ib], dsa_kidx_ln_b[ib], rel_bias)
            w_o = dsa_w_o[ib]
            ib += 1
        xf, xb = _proj_res_ln(mix, w_o.astype(BF16), xf, ln1_g[i], ln1_b[i], tm=512, tk=1024)
        hmid = _ffn_up(xb, ffn_w_gate[i].astype(BF16), ffn_w_up[i].astype(BF16), ffn_conv_w[i], tm=1024, tn=512)
        xf, xb = _proj_res_ln(hmid, ffn_w_down[i].astype(BF16), xf, ln2_g[i], ln2_b[i], tm=512, tk=1024)
        xf, xb = _ple(xb, xf, ple_w_gate[i].astype(BF16), p[i, 0], ple_w_proj[i].astype(BF16), tm=1024, tn=1024)
    return xf[None]
```

```python
import functools
import math

import jax
import jax.numpy as jnp
import numpy as np
from jax import lax
from jax.experimental import pallas as pl
from jax.experimental.pallas import tpu as pltpu

F32 = jnp.float32
BF16 = jnp.bfloat16
I32 = jnp.int32

D_MODEL = 2048
GDN_QK_HEADS = 16
GDN_V_HEADS = 32
GDN_DK = 128
GDN_DV = 128
GDN_CONV = 4
GDN_CHUNK = 64
GDN_QK_W = GDN_QK_HEADS * GDN_DK
GDN_V_W = GDN_V_HEADS * GDN_DV
DSA_HEADS = 16
DSA_KV_HEADS = 4
DSA_GROUP = DSA_HEADS // DSA_KV_HEADS
DSA_DH = 128
IDX_HEADS = 16
IDX_DIM = 128
TOPK_MAX = 256
N_BUCKETS = 32
MAX_DISTANCE = 128
D_FF = 5120
FFN_CONV = 3
PLE_DIM = 256
DEPTH = 2
DN_ALPHA = (2.0 * DEPTH) ** 0.25
LN_EPS = 1e-5
RMS_EPS = 1e-6

V7X_VMEM_BYTES = 64 * 1024 * 1024
V7X_VMEM_BUDGET = 56 * 1024 * 1024
LANES = 128
BF16_SUBLANES = 16

HALO = BF16_SUBLANES
Q_BLOCK = 128
IDX_KEY_BLOCK = 2048
IDX_SUB = 512
INT_MIN = -(2 ** 31)

_NT = (((1,), (1,)), ((), ()))
_TN = (((0,), (0,)), ((), ()))


def _cparams(semantics, vmem_bytes):
    return pltpu.CompilerParams(dimension_semantics=semantics,
                                vmem_limit_bytes=int(min(V7X_VMEM_BUDGET, vmem_bytes)))


def _silu(y):
    return y * jax.nn.sigmoid(y)


def _mm_scale_kernel(x_ref, w_ref, cs_ref, o_ref):
    acc = jnp.dot(x_ref[...], w_ref[...], preferred_element_type=F32)
    o_ref[...] = (acc * cs_ref[...]).astype(o_ref.dtype)


def _matmul_scaled(x, w, colscale, out_dtype, tm, tn):
    M, K = x.shape
    N = w.shape[1]
    tm, tn = min(tm, M), min(tn, N)
    osz = jnp.dtype(out_dtype).itemsize
    vmem = 2 * (tm * K * 2 + K * tn * 2 + tm * tn * osz) + 2 * tm * tn * 4
    return pl.pallas_call(
        _mm_scale_kernel,
        grid=(M // tm, N // tn),
        in_specs=[pl.BlockSpec((tm, K), lambda i, j: (i, 0)),
                  pl.BlockSpec((K, tn), lambda i, j: (0, j)),
                  pl.BlockSpec((1, tn), lambda i, j: (0, j))],
        out_specs=pl.BlockSpec((tm, tn), lambda i, j: (i, j)),
        out_shape=jax.ShapeDtypeStruct((M, N), out_dtype),
        compiler_params=_cparams(("parallel", "parallel"), vmem),
        name="matmul_scaled",
    )(x, w, colscale)


def _causal_conv(g, gh, cw_ref, g_scr, kc, tm):
    g_scr[0:HALO, :] = gh
    g_scr[HALO:HALO + tm, :] = g
    y = cw_ref[kc - 1:kc, :] * g
    for j in range(kc - 1):
        off = HALO - (kc - 1) + j
        y = y + cw_ref[j:j + 1, :] * g_scr[off:off + tm, :]
    return y


def _mm_conv_silu_kernel(x_ref, xh_ref, w_ref, cw_ref, o_ref, g_scr, *, kc, tm):
    w = w_ref[...]
    g = jnp.dot(x_ref[...], w, preferred_element_type=F32)
    gh = jnp.dot(xh_ref[...], w, preferred_element_type=F32)
    gh = jnp.where(pl.program_id(0) == 0, 0.0, gh)
    y = _causal_conv(g, gh, cw_ref, g_scr, kc, tm)
    o_ref[...] = _silu(y).astype(o_ref.dtype)


def _proj_conv_silu(x, w, conv_w, tm, tn):
    M, K = x.shape
    N = w.shape[1]
    kc = conv_w.shape[0]
    tm, tn = min(tm, M), min(tn, N)
    hb = tm // HALO
    vmem = 2 * (tm * K * 2 + HALO * K * 2 + K * tn * 2 + tm * tn * 2) + 4 * tm * tn * 4
    return pl.pallas_call(
        functools.partial(_mm_conv_silu_kernel, kc=kc, tm=tm),
        grid=(M // tm, N // tn),
        in_specs=[pl.BlockSpec((tm, K), lambda i, j: (i, 0)),
                  pl.BlockSpec((HALO, K), lambda i, j: (jnp.maximum(i * hb - 1, 0), 0)),
                  pl.BlockSpec((K, tn), lambda i, j: (0, j)),
                  pl.BlockSpec((kc, tn), lambda i, j: (0, j))],
        out_specs=pl.BlockSpec((tm, tn), lambda i, j: (i, j)),
        out_shape=jax.ShapeDtypeStruct((M, N), BF16),
        scratch_shapes=[pltpu.VMEM((tm + HALO, tn), F32)],
        compiler_params=_cparams(("parallel", "parallel"), vmem),
        name="proj_conv_silu",
    )(x, x, w, conv_w)


def _ffn_up_kernel(x_ref, xh_ref, wg_ref, wu_ref, cw_ref, o_ref, g_scr, *, kc, tm):
    wg = wg_ref[...]
    x = x_ref[...]
    g = jnp.dot(x, wg, preferred_element_type=F32)
    gh = jnp.dot(xh_ref[...], wg, preferred_element_type=F32)
    gh = jnp.where(pl.program_id(0) == 0, 0.0, gh)
    y = _causal_conv(g, gh, cw_ref, g_scr, kc, tm)
    u = jnp.dot(x, wu_ref[...], preferred_element_type=F32)
    o_ref[...] = (_silu(y) * u).astype(o_ref.dtype)


def _ffn_up(x, w_gate, w_up, conv_w, tm, tn):
    M, K = x.shape
    N = w_gate.shape[1]
    kc = conv_w.shape[0]
    tm, tn = min(tm, M), min(tn, N)
    hb = tm // HALO
    vmem = 2 * (tm * K * 2 + HALO * K * 2 + 2 * K * tn * 2 + tm * tn * 2) + 6 * tm * tn * 4
    return pl.pallas_call(
        functools.partial(_ffn_up_kernel, kc=kc, tm=tm),
        grid=(M // tm, N // tn),
        in_specs=[pl.BlockSpec((tm, K), lambda i, j: (i, 0)),
                  pl.BlockSpec((HALO, K), lambda i, j: (jnp.maximum(i * hb - 1, 0), 0)),
                  pl.BlockSpec((K, tn), lambda i, j: (0, j)),
                  pl.BlockSpec((K, tn), lambda i, j: (0, j)),
                  pl.BlockSpec((kc, tn), lambda i, j: (0, j))],
        out_specs=pl.BlockSpec((tm, tn), lambda i, j: (i, j)),
        out_shape=jax.ShapeDtypeStruct((M, N), BF16),
        scratch_shapes=[pltpu.VMEM((tm + HALO, tn), F32)],
        compiler_params=_cparams(("parallel", "parallel"), vmem),
        name="ffn_up",
    )(x, x, w_gate, w_up, conv_w)


def _mm_res_ln_kernel(a_ref, w_ref, res_ref, g_ref, b_ref, of_ref, ob_ref, acc_ref):
    k = pl.program_id(1)

    @pl.when(k == 0)
    def _():
        acc_ref[...] = jnp.zeros_like(acc_ref)

    acc_ref[...] += jnp.dot(a_ref[...], w_ref[...], preferred_element_type=F32)

    @pl.when(k == pl.num_programs(1) - 1)
    def _():
        y = DN_ALPHA * res_ref[...] + acc_ref[...]
        mu = jnp.mean(y, axis=-1, keepdims=True)
        yc = y - mu
        var = jnp.mean(yc * yc, axis=-1, keepdims=True)
        out = yc * lax.rsqrt(var + LN_EPS) * g_ref[...] + b_ref[...]
        of_ref[...] = out
        ob_ref[...] = out.astype(BF16)


def _proj_res_ln(a, w, res, g, b, tm, tk):
    M, K = a.shape
    N = w.shape[1]
    tm, tk = min(tm, M), min(tk, K)
    vmem = 2 * (tm * tk * 2 + tk * N * 2 + tm * N * 4 + tm * N * 4 + tm * N * 2) + 3 * tm * N * 4
    return pl.pallas_call(
        _mm_res_ln_kernel,
        grid=(M // tm, K // tk),
        in_specs=[pl.BlockSpec((tm, tk), lambda i, k: (i, k)),
                  pl.BlockSpec((tk, N), lambda i, k: (k, 0)),
                  pl.BlockSpec((tm, N), lambda i, k: (i, 0)),
                  pl.BlockSpec((1, N), lambda i, k: (0, 0)),
                  pl.BlockSpec((1, N), lambda i, k: (0, 0))],
        out_specs=[pl.BlockSpec((tm, N), lambda i, k: (i, 0)),
                   pl.BlockSpec((tm, N), lambda i, k: (i, 0))],
        out_shape=[jax.ShapeDtypeStruct((M, N), F32), jax.ShapeDtypeStruct((M, N), BF16)],
        scratch_shapes=[pltpu.VMEM((tm, N), F32)],
        compiler_params=_cparams(("parallel", "arbitrary"), vmem),
        name="proj_res_ln",
    )(a, w, res, g.reshape(1, N), b.reshape(1, N))


def _ple_kernel(xb_ref, wg_ref, p_ref, wp_ref, xr_ref, of_ref, ob_ref):
    gate = jax.nn.sigmoid(jnp.dot(xb_ref[...], wg_ref[...], preferred_element_type=F32))
    pe = jnp.dot(p_ref[...].astype(BF16), wp_ref[...], preferred_element_type=F32)
    out = xr_ref[...] + gate * pe
    of_ref[...] = out
    ob_ref[...] = out.astype(BF16)


def _ple(xb, xf, w_gate, p, w_proj, tm, tn):
    M, K = xb.shape
    N = w_gate.shape[1]
    P = p.shape[1]
    tm, tn = min(tm, M), min(tn, N)
    vmem = 2 * (tm * K * 2 + K * tn * 2 + tm * P * 4 + P * tn * 2 + tm * tn * 10) + 4 * tm * tn * 4
    return pl.pallas_call(
        _ple_kernel,
        grid=(M // tm, N // tn),
        in_specs=[pl.BlockSpec((tm, K), lambda i, j: (i, 0)),
                  pl.BlockSpec((K, tn), lambda i, j: (0, j)),
                  pl.BlockSpec((tm, P), lambda i, j: (i, 0)),
                  pl.BlockSpec((P, tn), lambda i, j: (0, j)),
                  pl.BlockSpec((tm, tn), lambda i, j: (i, j))],
        out_specs=[pl.BlockSpec((tm, tn), lambda i, j: (i, j)),
                   pl.BlockSpec((tm, tn), lambda i, j: (i, j))],
        out_shape=[jax.ShapeDtypeStruct((M, N), F32), jax.ShapeDtypeStruct((M, N), BF16)],
        compiler_params=_cparams(("parallel", "parallel"), vmem),
        name="ple",
    )(xb, w_gate, p, w_proj, xf)


GDN_HB = 4
GDN_NC = 4


def _gdn_kernel(q_ref, k_ref, v_ref, z_ref, sc_ref, hp_ref, ng_ref, o_ref, s_scr):
    C = GDN_CHUNK

    @pl.when(pl.program_id(1) == 0)
    def _():
        s_scr[...] = jnp.zeros_like(s_scr)

    row = lax.broadcasted_iota(I32, (C, C), 0)
    col = lax.broadcasted_iota(I32, (C, C), 1)
    tri = row >= col
    strict = row > col
    eye = row == col
    tri_f = tri.astype(F32)
    eye_f = eye.astype(F32)

    raw = sc_ref[...]
    a_log = hp_ref[0, 0:1, :]
    dt_b = hp_ref[0, 1:2, :]
    xs = raw + dt_b
    softplus = jnp.maximum(xs, 0.0) + jnp.log1p(jnp.exp(-jnp.abs(xs)))
    g_all = -jnp.exp(a_log) * softplus
    beta_all = jax.nn.sigmoid(raw)
    ng = ng_ref[...]

    units = [(c, j) for c in range(GDN_NC) for j in range(GDN_HB)]
    kb_l, rhs_l, decay_l, qd_l, kd_l, kbf_l, qbf_l, gl_l = [], [], [], [], [], [], [], []
    for c in range(GDN_NC):
        r0 = c * C
        gc = jnp.dot(tri_f, g_all[r0:r0 + C, :], precision=lax.Precision.HIGHEST,
                     preferred_element_type=F32)
        g_last = gc[C - 1:C, :]
        e_gc = jnp.exp(gc)
        e_rest = jnp.exp(g_last - gc)
        e_last = jnp.exp(g_last)
        beta_c = beta_all[r0:r0 + C, :]
        qn, kn = [], []
        for hq in range(GDN_HB // 2):
            qf = q_ref[r0:r0 + C, hq * GDN_DK:(hq + 1) * GDN_DK].astype(F32)
            kf = k_ref[r0:r0 + C, hq * GDN_DK:(hq + 1) * GDN_DK].astype(F32)
            qn.append(qf * lax.rsqrt(jnp.sum(qf * qf, axis=-1, keepdims=True) + RMS_EPS) * (GDN_DK ** -0.5))
            kn.append(kf * lax.rsqrt(jnp.sum(kf * kf, axis=-1, keepdims=True) + RMS_EPS))
        for j in range(GDN_HB):
            q_h, k_h = qn[j // 2], kn[j // 2]
            vf = v_ref[r0:r0 + C, j * GDN_DV:(j + 1) * GDN_DV].astype(F32)
            beta = beta_c[:, GDN_HB + j:GDN_HB + j + 1]
            kb = k_h * beta
            gcb = jnp.broadcast_to(gc[:, j:j + 1], (C, C))
            gcr = jnp.sum(jnp.where(eye, gcb, 0.0), axis=0, keepdims=True)
            decay_l.append(jnp.where(tri, jnp.exp(jnp.where(tri, gcb - gcr, 0.0)), 0.0))
            kb_l.append(kb.astype(BF16))
            rhs_l.append(jnp.concatenate([vf * beta, kb * e_gc[:, j:j + 1]], axis=-1).astype(BF16))
            qd_l.append((q_h * e_gc[:, j:j + 1]).astype(BF16))
            kd_l.append((k_h * e_rest[:, j:j + 1]).astype(BF16))
            kbf_l.append(k_h.astype(BF16))
            qbf_l.append(q_h.astype(BF16))
            gl_l.append(e_last[:, j:j + 1])

    n_u = len(units)
    kk_l = [lax.dot_general(kb_l[u], kbf_l[u], _NT, preferred_element_type=F32) for u in range(n_u)]
    qk_l = [lax.dot_general(qbf_l[u], kbf_l[u], _NT, preferred_element_type=F32) for u in range(n_u)]
    qk_l = [jnp.where(tri, qk_l[u] * decay_l[u], 0.0).astype(BF16) for u in range(n_u)]
    x_l = [(-jnp.where(strict, kk_l[u] * decay_l[u], 0.0)) for u in range(n_u)]
    t_l = [eye_f + x_l[u] for u in range(n_u)]
    x_l = [x.astype(BF16) for x in x_l]
    for _ in range(5):
        x_l = [jnp.dot(x, x, preferred_element_type=F32).astype(BF16) for x in x_l]
        t_l = [t + jnp.dot(t.astype(BF16), x, preferred_element_type=F32) for t, x in zip(t_l, x_l)]
    sol_l = [jnp.dot(t_l[u].astype(BF16), rhs_l[u], preferred_element_type=F32) for u in range(n_u)]

    s_cur = [s_scr[j] for j in range(GDN_HB)]
    for c in range(GDN_NC):
        r0 = c * C
        us = [c * GDN_HB + j for j in range(GDN_HB)]
        s_bf = [s.astype(BF16) for s in s_cur]
        ws_l = [jnp.dot(sol_l[u][:, GDN_DV:].astype(BF16), s_bf[j], preferred_element_type=F32)
                for j, u in enumerate(us)]
        qs_l = [jnp.dot(qd_l[u], s_bf[j], preferred_element_type=F32) for j, u in enumerate(us)]
        vn_l = [(sol_l[u][:, :GDN_DV] - ws_l[j]).astype(BF16) for j, u in enumerate(us)]
        kv_l = [lax.dot_general(kd_l[u], vn_l[j], _TN, preferred_element_type=F32) for j, u in enumerate(us)]
        ov_l = [jnp.dot(qk_l[u], vn_l[j], preferred_element_type=F32) for j, u in enumerate(us)]
        s_cur = [s_cur[j] * gl_l[u] + kv_l[j] for j, u in enumerate(us)]
        for j in range(GDN_HB):
            o = qs_l[j] + ov_l[j]
            zf = z_ref[r0:r0 + C, j * GDN_DV:(j + 1) * GDN_DV].astype(F32)
            o = o * lax.rsqrt(jnp.mean(o * o, axis=-1, keepdims=True) + RMS_EPS) * ng * _silu(zf)
            o_ref[r0:r0 + C, j * GDN_DV:(j + 1) * GDN_DV] = o.astype(o_ref.dtype)
    for j in range(GDN_HB):
        s_scr[j] = s_cur[j]


def _gdn_core(qkv, z, scal, hparams, norm_g):
    L = qkv.shape[0]
    G = GDN_V_HEADS // GDN_HB
    R = GDN_NC * GDN_CHUNK
    qw = (GDN_HB // 2) * GDN_DK
    vw = GDN_HB * GDN_DV
    k_blk0 = GDN_QK_W // qw
    v_blk0 = 2 * GDN_QK_W // vw
    vmem = 2 * (2 * R * qw * 2 + 2 * R * vw * 2 + R * LANES * 4 + R * vw * 2) + (16 << 20)
    return pl.pallas_call(
        _gdn_kernel,
        grid=(G, L // R),
        in_specs=[pl.BlockSpec((R, qw), lambda g, s: (s, g)),
                  pl.BlockSpec((R, qw), lambda g, s: (s, k_blk0 + g)),
                  pl.BlockSpec((R, vw), lambda g, s: (s, v_blk0 + g)),
                  pl.BlockSpec((R, vw), lambda g, s: (s, g)),
                  pl.BlockSpec((R, LANES), lambda g, s: (s, g)),
                  pl.BlockSpec((1, 8, LANES), lambda g, s: (g, 0, 0)),
                  pl.BlockSpec((1, GDN_DV), lambda g, s: (0, 0))],
        out_specs=pl.BlockSpec((R, vw), lambda g, s: (s, g)),
        out_shape=jax.ShapeDtypeStruct((L, GDN_V_W), BF16),
        scratch_shapes=[pltpu.VMEM((GDN_HB, GDN_DK, GDN_DV), F32)],
        compiler_params=_cparams(("parallel", "arbitrary"), vmem),
        name="gdn_core",
    )(qkv, qkv, qkv, z, scal, hparams, norm_g.reshape(1, GDN_DV))


def _gdn_mixer(xb, w_in, conv_w, a_log, dt_bias, norm_g):
    L = xb.shape[0]
    nqkv = 2 * GDN_QK_W + GDN_V_W
    w_qkv = w_in[:, :nqkv].astype(BF16)
    w_z = w_in[:, nqkv:nqkv + GDN_V_W].astype(BF16)
    w_ab = w_in[:, nqkv + GDN_V_W:]
    G = GDN_V_HEADS // GDN_HB
    w_a = w_ab[:, :GDN_V_HEADS].reshape(D_MODEL, G, GDN_HB)
    w_b = w_ab[:, GDN_V_HEADS:].reshape(D_MODEL, G, GDN_HB)
    w_sc = jnp.concatenate([w_a, w_b, jnp.zeros((D_MODEL, G, LANES - 2 * GDN_HB), F32)], axis=-1)
    w_sc = w_sc.reshape(D_MODEL, G * LANES).astype(BF16)
    qkv = _proj_conv_silu(xb, w_qkv, conv_w, tm=1024, tn=1024)
    ones_z = jnp.ones((1, GDN_V_W), F32)
    z = _matmul_scaled(xb, w_z, ones_z, BF16, tm=1024, tn=1024)
    scal = _matmul_scaled(xb, w_sc, jnp.ones((1, G * LANES), F32), F32, tm=1024, tn=G * LANES)
    hp = jnp.zeros((G, 8, LANES), F32)
    hp = hp.at[:, 0, :GDN_HB].set(a_log.reshape(G, GDN_HB))
    hp = hp.at[:, 1, :GDN_HB].set(dt_bias.reshape(G, GDN_HB))
    return _gdn_core(qkv, z, scal, hp, norm_g)


def _dsa_small_kernel(x_ref, w_ref, g_ref, b_ref, ki_ref, wi_ref):
    acc = jnp.dot(x_ref[...], w_ref[...], preferred_element_type=F32)
    ki = acc[:, :IDX_DIM]
    mu = jnp.mean(ki, axis=-1, keepdims=True)
    kc = ki - mu
    var = jnp.mean(kc * kc, axis=-1, keepdims=True)
    ki_ref[...] = (kc * lax.rsqrt(var + LN_EPS) * g_ref[...] + b_ref[...]).astype(ki_ref.dtype)
    wi_ref[...] = acc[:, IDX_DIM:] * ((IDX_HEADS ** -0.5) * (IDX_DIM ** -0.5))


def _dsa_small(xb, w_small, ln_g, ln_b, tm):
    M, K = xb.shape
    tm = min(tm, M)
    N = 2 * LANES
    vmem = 2 * (tm * K * 2 + K * N * 2 + tm * LANES * 6) + 4 * tm * N * 4
    return pl.pallas_call(
        _dsa_small_kernel,
        grid=(M // tm,),
        in_specs=[pl.BlockSpec((tm, K), lambda i: (i, 0)),
                  pl.BlockSpec((K, N), lambda i: (0, 0)),
                  pl.BlockSpec((1, IDX_DIM), lambda i: (0, 0)),
                  pl.BlockSpec((1, IDX_DIM), lambda i: (0, 0))],
        out_specs=[pl.BlockSpec((tm, IDX_DIM), lambda i: (i, 0)),
                   pl.BlockSpec((tm, LANES), lambda i: (i, 0))],
        out_shape=[jax.ShapeDtypeStruct((M, IDX_DIM), BF16), jax.ShapeDtypeStruct((M, LANES), F32)],
        compiler_params=_cparams(("parallel",), vmem),
        name="dsa_idx_proj",
    )(xb, w_small, ln_g.reshape(1, IDX_DIM), ln_b.reshape(1, IDX_DIM))


def _sortable_key(score):
    bits = lax.bitcast_convert_type(score, I32)
    return jnp.where(bits >= 0, bits, bits ^ jnp.int32(0x7FFFFFFF))


def _idx_kernel(qi_tab, kj_tab, last_tab,
                qidx_ref, kidx_ref, wi_ref, far_ref, near_ref, key_scr, *, k_top, n_sub_total):
    s = pl.program_id(0)
    i = qi_tab[s]
    j = kj_tab[s]
    nsub = IDX_KEY_BLOCK // IDX_SUB
    t_col = i * Q_BLOCK + lax.broadcasted_iota(I32, (Q_BLOCK, 1), 0)
    lane_sub = lax.broadcasted_iota(I32, (Q_BLOCK, IDX_SUB), 1)
    wi = wi_ref[...]

    for sub in range(nsub):
        ki_sub = kidx_ref[sub * IDX_SUB:(sub + 1) * IDX_SUB, :]
        acc = jnp.zeros((Q_BLOCK, IDX_SUB), F32)
        for h in range(IDX_HEADS):
            sc = lax.dot_general(qidx_ref[:, h * IDX_DIM:(h + 1) * IDX_DIM], ki_sub, _NT,
                                 preferred_element_type=F32)
            acc = acc + jnp.maximum(sc, 0.0) * wi[:, h:h + 1]
        s_idx = j * IDX_KEY_BLOCK + sub * IDX_SUB + lane_sub
        key_scr[j * nsub + sub] = jnp.where(s_idx <= t_col, _sortable_key(acc), INT_MIN)

    @pl.when(last_tab[s] == 1)
    def _():
        n_chunks = (j + 1) * nsub

        def count(pred, ref_val):
            refb = jnp.broadcast_to(ref_val, (Q_BLOCK, LANES))

            def body(c, cnt):
                blk = key_scr[c]
                for l in range(IDX_SUB // LANES):
                    cnt = cnt + jnp.where(pred(blk[:, l * LANES:(l + 1) * LANES], refb), 1, 0)
                return cnt

            cnt = lax.fori_loop(0, n_chunks, body, jnp.zeros((Q_BLOCK, LANES), I32))
            return jnp.sum(cnt, axis=1, keepdims=True)

        def bit_body(b, thr):
            cand = thr + jnp.left_shift(jnp.int32(1), 31 - b)
            return jnp.where(count(lambda a, r: a >= r, cand) >= k_top, cand, thr)

        thr = lax.fori_loop(0, 32, bit_body, jnp.full((Q_BLOCK, 1), INT_MIN, I32))
        n_gt = count(lambda a, r: a > r, thr)
        need_eq = (k_top - n_gt).astype(F32)

        thr_b = jnp.broadcast_to(thr, (Q_BLOCK, IDX_SUB))
        incl = (lax.broadcasted_iota(I32, (IDX_SUB, IDX_SUB), 0)
                <= lax.broadcasted_iota(I32, (IDX_SUB, IDX_SUB), 1)).astype(BF16)

        def sel_body(c, carry):
            blk = key_scr[c]
            eq = blk == thr_b
            eq_f = jnp.where(eq, 1.0, 0.0)
            rank = carry + jnp.dot(eq_f.astype(BF16), incl, preferred_element_type=F32)
            s_idx = c * IDX_SUB + lane_sub
            sel = ((blk > thr_b) | (eq & (rank <= need_eq))) & (s_idx <= t_col)
            far = sel & (t_col - s_idx >= MAX_DISTANCE)
            far_ref[0, c] = jnp.where(far, 0.0, -jnp.inf).astype(far_ref.dtype)
            key_scr[c] = jnp.where(sel, 1, 0)
            return carry + jnp.sum(eq_f, axis=1, keepdims=True)

        lax.fori_loop(0, n_chunks, sel_body, jnp.zeros((Q_BLOCK, 1), F32))

        def fill_body(c, carry):
            far_ref[0, c] = jnp.full((Q_BLOCK, IDX_SUB), -jnp.inf, far_ref.dtype)
            return carry

        lax.fori_loop(n_chunks, n_sub_total, fill_body, 0)

        def window(blk_idx):
            per = IDX_SUB // Q_BLOCK
            chunk = key_scr[blk_idx // per]
            m = blk_idx % per
            out = chunk[:, 0:Q_BLOCK]
            for q in range(1, per):
                out = jnp.where(m == q, chunk[:, q * Q_BLOCK:(q + 1) * Q_BLOCK], out)
            return out

        r_i = lax.broadcasted_iota(I32, (Q_BLOCK, Q_BLOCK), 0)
        c_i = lax.broadcasted_iota(I32, (Q_BLOCK, Q_BLOCK), 1)
        d_lo = Q_BLOCK + r_i - c_i
        d_hi = r_i - c_i
        near_lo = (window(jnp.maximum(i - 1, 0)) != 0) & (d_lo < MAX_DISTANCE) & (i >= 1)
        near_hi = (window(i) != 0) & (d_hi >= 0) & (d_hi < MAX_DISTANCE)
        near_ref[:, 0:Q_BLOCK] = jnp.where(near_lo, 0.0, -jnp.inf).astype(near_ref.dtype)
        near_ref[:, Q_BLOCK:2 * Q_BLOCK] = jnp.where(near_hi, 0.0, -jnp.inf).astype(near_ref.dtype)


def _idx_select(proj, ki, wi, k_top):
    L = ki.shape[0]
    nq = L // Q_BLOCK
    n_sub_total = L // IDX_SUB
    qi_l, kj_l, last_l = [], [], []
    for i in range(nq):
        j_last = (i * Q_BLOCK + Q_BLOCK - 1) // IDX_KEY_BLOCK
        for j in range(j_last + 1):
            qi_l.append(i)
            kj_l.append(j)
            last_l.append(1 if j == j_last else 0)
    tabs = [jnp.asarray(np.asarray(t, np.int32)) for t in (qi_l, kj_l, last_l)]
    qcol = (DSA_HEADS * DSA_DH) // (IDX_HEADS * IDX_DIM)
    vmem = (L * Q_BLOCK * 4 + 2 * (Q_BLOCK * L * 2) + 2 * (Q_BLOCK * IDX_HEADS * IDX_DIM * 2)
            + 2 * IDX_KEY_BLOCK * IDX_DIM * 2 + (8 << 20))
    grid_spec = pltpu.PrefetchScalarGridSpec(
        num_scalar_prefetch=3,
        grid=(len(qi_l),),
        in_specs=[pl.BlockSpec((Q_BLOCK, IDX_HEADS * IDX_DIM), lambda s, qt, kt, lt: (qt[s], qcol)),
                  pl.BlockSpec((IDX_KEY_BLOCK, IDX_DIM), lambda s, qt, kt, lt: (kt[s], 0)),
                  pl.BlockSpec((Q_BLOCK, LANES), lambda s, qt, kt, lt: (qt[s], 0))],
        out_specs=[pl.BlockSpec((1, n_sub_total, Q_BLOCK, IDX_SUB), lambda s, qt, kt, lt: (qt[s], 0, 0, 0)),
                   pl.BlockSpec((Q_BLOCK, 2 * Q_BLOCK), lambda s, qt, kt, lt: (qt[s], 0))],
        scratch_shapes=[pltpu.VMEM((n_sub_total, Q_BLOCK, IDX_SUB), I32)],
    )
    return pl.pallas_call(
        functools.partial(_idx_kernel, k_top=k_top, n_sub_total=n_sub_total),
        grid_spec=grid_spec,
        out_shape=[jax.ShapeDtypeStruct((nq, n_sub_total, Q_BLOCK, IDX_SUB), BF16),
                   jax.ShapeDtypeStruct((L, 2 * Q_BLOCK), BF16)],
        compiler_params=_cparams(("arbitrary",), vmem),
        name="dsa_idx_select",
    )(*tabs, proj, ki, wi)


M_INIT = -1e30
LOG2E = math.log2(math.e)


def _attn_kernel(qi_tab, kj_tab, kind_tab, first_tab,
                 tab_ref, q_ref, kf_ref, vf_ref, klo_ref, khi_ref, vlo_ref, vhi_ref, far_ref, near_ref,
                 o_ref, m_scr, l_scr, acc_scr, b_scr, s_scr):
    s = pl.program_id(0)

    @pl.when(s == 0)
    def _():
        r_i = lax.broadcasted_iota(I32, (Q_BLOCK, 2 * Q_BLOCK), 0)
        c_i = lax.broadcasted_iota(I32, (Q_BLOCK, 2 * Q_BLOCK), 1)
        d = jnp.maximum(Q_BLOCK + r_i - c_i, 0)
        max_exact = N_BUCKETS // 2
        df = jnp.maximum(d, 1).astype(F32)
        large = max_exact + (jnp.log(df / max_exact) / math.log(MAX_DISTANCE / max_exact)
                             * (N_BUCKETS - max_exact)).astype(I32)
        large = jnp.minimum(large, N_BUCKETS - 1)
        bkt = jnp.where(d < max_exact, d, large)
        for h in range(DSA_HEADS):
            acc = jnp.zeros((Q_BLOCK, 2 * Q_BLOCK), F32)
            for b in range(N_BUCKETS):
                acc = jnp.where(bkt == b, (tab_ref[b, h] - tab_ref[N_BUCKETS - 1, h]) * LOG2E, acc)
            b_scr[h] = acc

    @pl.when(first_tab[s] == 1)
    def _():
        m_scr[...] = jnp.full_like(m_scr, M_INIT)
        l_scr[...] = jnp.zeros_like(l_scr)
        acc_scr[...] = jnp.zeros_like(acc_scr)

    def run_units(units, width):
        nt = width // LANES
        for n, (h, k_ref, _, add) in enumerate(units):
            g = h // DSA_GROUP
            logits = lax.dot_general(q_ref[:, h * DSA_DH:(h + 1) * DSA_DH],
                                     k_ref[:, g * DSA_DH:(g + 1) * DSA_DH], _NT, preferred_element_type=F32)
            s_scr[n, :, 0:width] = logits + add(h)
        for n, (h, _, v_ref, _) in enumerate(units):
            g = h // DSA_GROUP
            tiles = [s_scr[n, :, t * LANES:(t + 1) * LANES] for t in range(nt)]
            tmax = tiles[0]
            for t in range(1, nt):
                tmax = jnp.maximum(tmax, tiles[t])
            m_prev = m_scr[h]
            m_new = jnp.maximum(m_prev, jnp.max(tmax, axis=-1, keepdims=True))
            alpha = jnp.exp2(m_prev - m_new)
            p = [jnp.exp2(tiles[t] - m_new) for t in range(nt)]
            psum = p[0]
            for t in range(1, nt):
                psum = psum + p[t]
            l_scr[h] = alpha * l_scr[h] + psum
            pb = jnp.concatenate([x.astype(BF16) for x in p], axis=-1) if nt > 1 else p[0].astype(BF16)
            acc_scr[h] = alpha * acc_scr[h] + jnp.dot(pb, v_ref[:, g * DSA_DH:(g + 1) * DSA_DH],
                                                      preferred_element_type=F32)
            m_scr[h] = m_new

    @pl.when(kind_tab[s] == 0)
    def _():
        mask = far_ref[0, 0].astype(F32)
        run_units([(h, kf_ref, vf_ref, lambda h: mask) for h in range(DSA_HEADS)], IDX_SUB)

    @pl.when(kind_tab[s] == 1)
    def _():
        for half, (k_ref, v_ref) in enumerate(((klo_ref, vlo_ref), (khi_ref, vhi_ref))):
            cols = slice(half * Q_BLOCK, (half + 1) * Q_BLOCK)
            mask = near_ref[:, cols].astype(F32)
            run_units([(h, k_ref, v_ref, lambda h, mask=mask, cols=cols: mask + b_scr[h][:, cols])
                       for h in range(DSA_HEADS)], Q_BLOCK)
        for h in range(DSA_HEADS):
            l_row = jnp.sum(l_scr[h], axis=-1, keepdims=True)
            o_ref[:, h * DSA_DH:(h + 1) * DSA_DH] = (acc_scr[h] / l_row).astype(o_ref.dtype)


def _masked_attention(proj, far, near, rel_bias):
    L = proj.shape[0]
    nq = L // Q_BLOCK
    per = IDX_SUB // Q_BLOCK
    qi_l, kj_l, kind_l, first_l = [], [], [], []
    for i in range(nq):
        n_far = -(-i // per)
        for j in range(n_far):
            qi_l.append(i); kj_l.append(j); kind_l.append(0); first_l.append(1 if j == 0 else 0)
        qi_l.append(i); kj_l.append(max(n_far - 1, 0)); kind_l.append(1); first_l.append(1 if n_far == 0 else 0)
    tabs = [jnp.asarray(np.asarray(t, np.int32)) for t in (qi_l, kj_l, kind_l, first_l)]
    qw = DSA_HEADS * DSA_DH
    kvw = DSA_KV_HEADS * DSA_DH
    k_col = (2 * qw) // kvw
    v_col = k_col + 1
    hw = DSA_HEADS
    vmem = (2 * (Q_BLOCK * qw * 2 * 2 + 2 * IDX_SUB * kvw * 2 + 4 * Q_BLOCK * kvw * 2
                 + Q_BLOCK * IDX_SUB * 2 + Q_BLOCK * 2 * Q_BLOCK * 2)
            + hw * Q_BLOCK * (3 * LANES + 2 * Q_BLOCK) * 4 + (16 << 20))
    idx = lambda f: (lambda s, qt, kt, kd, ft: f(qt[s], kt[s]))
    grid_spec = pltpu.PrefetchScalarGridSpec(
        num_scalar_prefetch=4,
        grid=(len(qi_l),),
        in_specs=[pl.BlockSpec(memory_space=pltpu.SMEM),
                  pl.BlockSpec((Q_BLOCK, qw), idx(lambda i, j: (i, 0))),
                  pl.BlockSpec((IDX_SUB, kvw), idx(lambda i, j: (j, k_col))),
                  pl.BlockSpec((IDX_SUB, kvw), idx(lambda i, j: (j, v_col))),
                  pl.BlockSpec((Q_BLOCK, kvw), idx(lambda i, j: (jnp.maximum(i - 1, 0), k_col))),
                  pl.BlockSpec((Q_BLOCK, kvw), idx(lambda i, j: (i, k_col))),
                  pl.BlockSpec((Q_BLOCK, kvw), idx(lambda i, j: (jnp.maximum(i - 1, 0), v_col))),
                  pl.BlockSpec((Q_BLOCK, kvw), idx(lambda i, j: (i, v_col))),
                  pl.BlockSpec((1, 1, Q_BLOCK, IDX_SUB), idx(lambda i, j: (i, j, 0, 0))),
                  pl.BlockSpec((Q_BLOCK, 2 * Q_BLOCK), idx(lambda i, j: (i, 0)))],
        out_specs=pl.BlockSpec((Q_BLOCK, qw), idx(lambda i, j: (i, 0))),
        scratch_shapes=[pltpu.VMEM((hw, Q_BLOCK, LANES), F32),
                        pltpu.VMEM((hw, Q_BLOCK, LANES), F32),
                        pltpu.VMEM((hw, Q_BLOCK, DSA_DH), F32),
                        pltpu.VMEM((hw, Q_BLOCK, 2 * Q_BLOCK), F32),
                        pltpu.VMEM((hw, Q_BLOCK, IDX_SUB), F32)],
    )
    return pl.pallas_call(
        _attn_kernel,
        grid_spec=grid_spec,
        out_shape=jax.ShapeDtypeStruct((L, qw), BF16),
        compiler_params=_cparams(("arbitrary",), vmem),
        name="dsa_attention",
    )(*tabs, rel_bias, proj, proj, proj, proj, proj, proj, proj, far, near)


def _dsa_mixer(xb, w_in, ln_g, ln_b, rel_bias):
    L = xb.shape[0]
    k_top = min(TOPK_MAX, L // 4)
    sq = DSA_HEADS * DSA_DH
    skv = DSA_KV_HEADS * DSA_DH
    si = IDX_HEADS * IDX_DIM
    w_q = w_in[:, :sq]
    w_k = w_in[:, sq:sq + skv]
    w_v = w_in[:, sq + skv:sq + 2 * skv]
    w_qi = w_in[:, sq + 2 * skv:sq + 2 * skv + si]
    w_ki = w_in[:, sq + 2 * skv + si:sq + 2 * skv + si + IDX_DIM]
    w_wi = w_in[:, sq + 2 * skv + si + IDX_DIM:]
    w_main = jnp.concatenate([w_q, w_qi, w_k, w_v], axis=1).astype(BF16)
    colscale = jnp.concatenate([jnp.full((1, sq), DSA_DH ** -0.5 * LOG2E, F32),
                                jnp.ones((1, si + 2 * skv), F32)], axis=1)
    w_small = jnp.concatenate([w_ki, w_wi, jnp.zeros((D_MODEL, LANES - IDX_HEADS), F32)], axis=1).astype(BF16)
    proj = _matmul_scaled(xb, w_main, colscale, BF16, tm=1024, tn=1024)
    ki, wi = _dsa_small(xb, w_small, ln_g, ln_b, tm=1024)
    far, near = _idx_select(proj, ki, wi, k_top)
    return _masked_attention(proj, far, near, rel_bias)


def kernel(x, p, gdn_w_in, gdn_conv_w, gdn_a_log, gdn_dt_bias, gdn_norm_g, gdn_w_o, dsa_w_in, dsa_kidx_ln_g, dsa_kidx_ln_b, dsa_w_o, rel_bias, ln1_g, ln1_b, ffn_w_gate, ffn_w_up, ffn_conv_w, ffn_w_down, ln2_g, ln2_b, ple_w_proj, ple_w_gate):
    assert x.shape[0] == 1 and x.shape[2] == D_MODEL
    xf = x[0]
    xb = xf.astype(BF16)
    ia = ib = 0
    for i in range(DEPTH):
        if i % 2 == 0:
            mix = _gdn_mixer(xb, gdn_w_in[ia], gdn_conv_w[ia], gdn_a_log[ia], gdn_dt_bias[ia], gdn_norm_g[ia])
            w_o = gdn_w_o[ia]
            ia += 1
        else:
            mix = _dsa_mixer(xb, dsa_w_in[ib], dsa_kidx_ln_g[ib], dsa_kidx_ln_b[ib], rel_bias)
            w_o = dsa_w_o[ib]
            ib += 1
        xf, xb = _proj_res_ln(mix, w_o.astype(BF16), xf, ln1_g[i], ln1_b[i], tm=512, tk=1024)
        hmid = _ffn_up(xb, ffn_w_gate[i].astype(BF16), ffn_w_up[i].astype(BF16), ffn_conv_w[i], tm=1024, tn=512)
        xf, xb = _proj_res_ln(hmid, ffn_w_down[i].astype(BF16), xf, ln2_g[i], ln2_b[i], tm=512, tk=1024)
        xf, xb = _ple(xb, xf, ple_w_gate[i].astype(BF16), p[i, 0], ple_w_proj[i].astype(BF16), tm=1024, tn=1024)
    return xf[None]
```

```python
import functools
import math

import jax
import jax.numpy as jnp
import numpy as np
from jax import lax
from jax.experimental import pallas as pl
from jax.experimental.pallas import tpu as pltpu

F32 = jnp.float32
BF16 = jnp.bfloat16
I32 = jnp.int32

D_MODEL = 2048
GDN_QK_HEADS = 16
GDN_V_HEADS = 32
GDN_DK = 128
GDN_DV = 128
GDN_CONV = 4
GDN_CHUNK = 64
GDN_QK_W = GDN_QK_HEADS * GDN_DK
GDN_V_W = GDN_V_HEADS * GDN_DV
DSA_HEADS = 16
DSA_KV_HEADS = 4
DSA_GROUP = DSA_HEADS // DSA_KV_HEADS
DSA_DH = 128
IDX_HEADS = 16
IDX_DIM = 128
TOPK_MAX = 256
N_BUCKETS = 32
MAX_DISTANCE = 128
D_FF = 5120
FFN_CONV = 3
PLE_DIM = 256
DEPTH = 2
DN_ALPHA = (2.0 * DEPTH) ** 0.25
LN_EPS = 1e-5
RMS_EPS = 1e-6

V7X_VMEM_BYTES = 64 * 1024 * 1024
V7X_VMEM_BUDGET = 56 * 1024 * 1024
LANES = 128
BF16_SUBLANES = 16

HALO = BF16_SUBLANES
Q_BLOCK = 128
IDX_KEY_BLOCK = 2048
IDX_SUB = 512
INT_MIN = -(2 ** 31)

_NT = (((1,), (1,)), ((), ()))
_TN = (((0,), (0,)), ((), ()))


def _cparams(semantics, vmem_bytes):
    return pltpu.CompilerParams(dimension_semantics=semantics,
                                vmem_limit_bytes=int(min(V7X_VMEM_BUDGET, vmem_bytes)))


def _silu(y):
    return y * jax.nn.sigmoid(y)


def _mm_scale_kernel(x_ref, w_ref, cs_ref, o_ref):
    acc = jnp.dot(x_ref[...], w_ref[...], preferred_element_type=F32)
    o_ref[...] = (acc * cs_ref[...]).astype(o_ref.dtype)


def _matmul_scaled(x, w, colscale, out_dtype, tm, tn):
    M, K = x.shape
    N = w.shape[1]
    tm, tn = min(tm, M), min(tn, N)
    osz = jnp.dtype(out_dtype).itemsize
    vmem = 2 * (tm * K * 2 + K * tn * 2 + tm * tn * osz) + 2 * tm * tn * 4
    return pl.pallas_call(
        _mm_scale_kernel,
        grid=(M // tm, N // tn),
        in_specs=[pl.BlockSpec((tm, K), lambda i, j: (i, 0)),
                  pl.BlockSpec((K, tn), lambda i, j: (0, j)),
                  pl.BlockSpec((1, tn), lambda i, j: (0, j))],
        out_specs=pl.BlockSpec((tm, tn), lambda i, j: (i, j)),
        out_shape=jax.ShapeDtypeStruct((M, N), out_dtype),
        compiler_params=_cparams(("parallel", "parallel"), vmem),
        name="matmul_scaled",
    )(x, w, colscale)


CONV_SUB = 256
FFN_SUB = 512


CONV_ROWS = 1024


def _causal_conv(g, gh, cw_ref, g_scr, kc, tm, cols, r0=0):
    if gh is not None:
        g_scr[0:HALO, cols] = gh
    g_scr[HALO + r0:HALO + r0 + tm, cols] = g
    y = cw_ref[kc - 1:kc, cols] * g
    for j in range(kc - 1):
        off = HALO + r0 - (kc - 1) + j
        y = y + cw_ref[j:j + 1, cols] * g_scr[off:off + tm, cols]
    return y


def _mm_conv_silu_kernel(x_ref, xh_ref, w_ref, cw_ref, o_ref, g_scr, *, kc, tm):
    first = pl.program_id(0) == 0
    rs = min(CONV_ROWS, tm)
    for c0 in range(0, o_ref.shape[1], CONV_SUB):
        cols = slice(c0, c0 + CONV_SUB)
        w = w_ref[:, cols]
        gh = jnp.dot(xh_ref[...], w, preferred_element_type=F32)
        gh = jnp.where(first, 0.0, gh)
        for r0 in range(0, tm, rs):
            g = jnp.dot(x_ref[r0:r0 + rs, :], w, preferred_element_type=F32)
            y = _causal_conv(g, gh if r0 == 0 else None, cw_ref, g_scr, kc, rs, cols, r0)
            o_ref[r0:r0 + rs, cols] = _silu(y).astype(o_ref.dtype)


def _proj_conv_silu(x, w, conv_w, tm, tn):
    M, K = x.shape
    N = w.shape[1]
    kc = conv_w.shape[0]
    tm, tn = min(tm, M), min(tn, N)
    hb = tm // HALO
    vmem = 2 * (tm * K * 2 + HALO * K * 2 + K * tn * 2 + tm * tn * 2) + 4 * tm * tn * 4
    return pl.pallas_call(
        functools.partial(_mm_conv_silu_kernel, kc=kc, tm=tm),
        grid=(M // tm, N // tn),
        in_specs=[pl.BlockSpec((tm, K), lambda i, j: (i, 0)),
                  pl.BlockSpec((HALO, K), lambda i, j: (jnp.maximum(i * hb - 1, 0), 0)),
                  pl.BlockSpec((K, tn), lambda i, j: (0, j)),
                  pl.BlockSpec((kc, tn), lambda i, j: (0, j))],
        out_specs=pl.BlockSpec((tm, tn), lambda i, j: (i, j)),
        out_shape=jax.ShapeDtypeStruct((M, N), BF16),
        scratch_shapes=[pltpu.VMEM((tm + HALO, tn), F32)],
        compiler_params=_cparams(("parallel", "parallel"), vmem),
        name="proj_conv_silu",
    )(x, x, w, conv_w)


def _ffn_up_kernel(x_ref, xh_ref, wg_ref, wu_ref, cw_ref, o_ref, g_scr, *, kc, tm):
    first = pl.program_id(0) == 0
    for c0 in range(0, o_ref.shape[1], FFN_SUB):
        cols = slice(c0, c0 + FFN_SUB)
        wg = wg_ref[:, cols]
        g = jnp.dot(x_ref[...], wg, preferred_element_type=F32)
        gh = jnp.dot(xh_ref[...], wg, preferred_element_type=F32)
        gh = jnp.where(first, 0.0, gh)
        u = jnp.dot(x_ref[...], wu_ref[:, cols], preferred_element_type=F32)
        y = _causal_conv(g, gh, cw_ref, g_scr, kc, tm, cols)
        o_ref[:, cols] = (_silu(y) * u).astype(o_ref.dtype)


def _ffn_up(x, w_gate, w_up, conv_w, tm, tn):
    M, K = x.shape
    N = w_gate.shape[1]
    kc = conv_w.shape[0]
    tm, tn = min(tm, M), min(tn, N)
    hb = tm // HALO
    vmem = 2 * (tm * K * 2 + HALO * K * 2 + 2 * K * tn * 2 + tm * tn * 2) + 6 * tm * tn * 4
    return pl.pallas_call(
        functools.partial(_ffn_up_kernel, kc=kc, tm=tm),
        grid=(M // tm, N // tn),
        in_specs=[pl.BlockSpec((tm, K), lambda i, j: (i, 0)),
                  pl.BlockSpec((HALO, K), lambda i, j: (jnp.maximum(i * hb - 1, 0), 0)),
                  pl.BlockSpec((K, tn), lambda i, j: (0, j)),
                  pl.BlockSpec((K, tn), lambda i, j: (0, j)),
                  pl.BlockSpec((kc, tn), lambda i, j: (0, j))],
        out_specs=pl.BlockSpec((tm, tn), lambda i, j: (i, j)),
        out_shape=jax.ShapeDtypeStruct((M, N), BF16),
        scratch_shapes=[pltpu.VMEM((tm + HALO, tn), F32)],
        compiler_params=_cparams(("parallel", "parallel"), vmem),
        name="ffn_up",
    )(x, x, w_gate, w_up, conv_w)


def _mm_res_ln_kernel(a_ref, w_ref, res_ref, g_ref, b_ref, of_ref, ob_ref, *, sub):
    for r0 in range(0, a_ref.shape[0], sub):
        rows = slice(r0, r0 + sub)
        acc = jnp.dot(a_ref[rows, :], w_ref[...], preferred_element_type=F32)
        y = DN_ALPHA * res_ref[rows, :] + acc
        mu = jnp.mean(y, axis=-1, keepdims=True)
        yc = y - mu
        var = jnp.mean(yc * yc, axis=-1, keepdims=True)
        out = yc * lax.rsqrt(var + LN_EPS) * g_ref[...] + b_ref[...]
        of_ref[rows, :] = out
        ob_ref[rows, :] = out.astype(BF16)


def _proj_res_ln(a, w, res, g, b, tm, sub):
    M, K = a.shape
    N = w.shape[1]
    tm = min(tm, M)
    sub = min(sub, tm)
    vmem = K * N * 2 + 2 * (tm * K * 2 + tm * N * 4 + tm * N * 4 + tm * N * 2) + 4 * sub * N * 4
    return pl.pallas_call(
        functools.partial(_mm_res_ln_kernel, sub=sub),
        grid=(M // tm,),
        in_specs=[pl.BlockSpec((tm, K), lambda i: (i, 0)),
                  pl.BlockSpec((K, N), lambda i: (0, 0), pipeline_mode=pl.Buffered(1)),
                  pl.BlockSpec((tm, N), lambda i: (i, 0)),
                  pl.BlockSpec((1, N), lambda i: (0, 0)),
                  pl.BlockSpec((1, N), lambda i: (0, 0))],
        out_specs=[pl.BlockSpec((tm, N), lambda i: (i, 0)),
                   pl.BlockSpec((tm, N), lambda i: (i, 0))],
        out_shape=[jax.ShapeDtypeStruct((M, N), F32), jax.ShapeDtypeStruct((M, N), BF16)],
        compiler_params=_cparams(("parallel",), vmem),
        name="proj_res_ln",
    )(a, w, res, g.reshape(1, N), b.reshape(1, N))


def _ple_kernel(xb_ref, wg_ref, p_ref, wp_ref, xr_ref, of_ref, ob_ref):
    gate = jax.nn.sigmoid(jnp.dot(xb_ref[...], wg_ref[...], preferred_element_type=F32))
    pe = jnp.dot(p_ref[...].astype(BF16), wp_ref[...], preferred_element_type=F32)
    out = xr_ref[...] + gate * pe
    of_ref[...] = out
    ob_ref[...] = out.astype(BF16)


def _ple(xb, xf, w_gate, p, w_proj, tm, tn):
    M, K = xb.shape
    N = w_gate.shape[1]
    P = p.shape[1]
    tm, tn = min(tm, M), min(tn, N)
    vmem = 2 * (tm * K * 2 + K * tn * 2 + tm * P * 4 + P * tn * 2 + tm * tn * 10) + 4 * tm * tn * 4
    return pl.pallas_call(
        _ple_kernel,
        grid=(M // tm, N // tn),
        in_specs=[pl.BlockSpec((tm, K), lambda i, j: (i, 0)),
                  pl.BlockSpec((K, tn), lambda i, j: (0, j)),
                  pl.BlockSpec((tm, P), lambda i, j: (i, 0)),
                  pl.BlockSpec((P, tn), lambda i, j: (0, j)),
                  pl.BlockSpec((tm, tn), lambda i, j: (i, j))],
        out_specs=[pl.BlockSpec((tm, tn), lambda i, j: (i, j)),
                   pl.BlockSpec((tm, tn), lambda i, j: (i, j))],
        out_shape=[jax.ShapeDtypeStruct((M, N), F32), jax.ShapeDtypeStruct((M, N), BF16)],
        compiler_params=_cparams(("parallel", "parallel"), vmem),
        name="ple",
    )(xb, w_gate, p, w_proj, xf)


GDN_HB = 4
GDN_NC = 4


def _gdn_kernel(q_ref, k_ref, v_ref, z_ref, sc_ref, hp_ref, ng_ref, o_ref, s_scr):
    C = GDN_CHUNK

    @pl.when(pl.program_id(1) == 0)
    def _():
        s_scr[...] = jnp.zeros_like(s_scr)

    row = lax.broadcasted_iota(I32, (C, C), 0)
    col = lax.broadcasted_iota(I32, (C, C), 1)
    tri = row >= col
    strict = row > col
    eye = row == col
    tri_f = tri.astype(F32)
    eye_f = eye.astype(F32)

    raw = sc_ref[...]
    a_log = hp_ref[0, 0:1, :]
    dt_b = hp_ref[0, 1:2, :]
    xs = raw + dt_b
    softplus = jnp.maximum(xs, 0.0) + jnp.log1p(jnp.exp(-jnp.abs(xs)))
    g_all = -jnp.exp(a_log) * softplus
    beta_all = jax.nn.sigmoid(raw)
    ng = ng_ref[...]

    units = [(c, j) for c in range(GDN_NC) for j in range(GDN_HB)]
    kb_l, rhs_l, decay_l, qd_l, kd_l, kbf_l, qbf_l, gl_l = [], [], [], [], [], [], [], []
    for c in range(GDN_NC):
        r0 = c * C
        gc = jnp.dot(tri_f, g_all[r0:r0 + C, :], precision=lax.Precision.HIGHEST,
                     preferred_element_type=F32)
        g_last = gc[C - 1:C, :]
        e_gc = jnp.exp(gc)
        e_rest = jnp.exp(g_last - gc)
        e_last = jnp.exp(g_last)
        beta_c = beta_all[r0:r0 + C, :]
        qn, kn = [], []
        for hq in range(GDN_HB // 2):
            qf = q_ref[r0:r0 + C, hq * GDN_DK:(hq + 1) * GDN_DK].astype(F32)
            kf = k_ref[r0:r0 + C, hq * GDN_DK:(hq + 1) * GDN_DK].astype(F32)
            qn.append(qf * lax.rsqrt(jnp.sum(qf * qf, axis=-1, keepdims=True) + RMS_EPS) * (GDN_DK ** -0.5))
            kn.append(kf * lax.rsqrt(jnp.sum(kf * kf, axis=-1, keepdims=True) + RMS_EPS))
        for j in range(GDN_HB):
            q_h, k_h = qn[j // 2], kn[j // 2]
            vf = v_ref[r0:r0 + C, j * GDN_DV:(j + 1) * GDN_DV].astype(F32)
            beta = beta_c[:, GDN_HB + j:GDN_HB + j + 1]
            kb = k_h * beta
            gcb = jnp.broadcast_to(gc[:, j:j + 1], (C, C))
            gcr = jnp.sum(jnp.where(eye, gcb, 0.0), axis=0, keepdims=True)
            decay_l.append(jnp.where(tri, jnp.exp(jnp.where(tri, gcb - gcr, 0.0)), 0.0))
            kb_l.append(kb.astype(BF16))
            rhs_l.append(jnp.concatenate([vf * beta, kb * e_gc[:, j:j + 1]], axis=-1).astype(BF16))
            qd_l.append((q_h * e_gc[:, j:j + 1]).astype(BF16))
            kd_l.append((k_h * e_rest[:, j:j + 1]).astype(BF16))
            kbf_l.append(k_h.astype(BF16))
            qbf_l.append(q_h.astype(BF16))
            gl_l.append(e_last[:, j:j + 1])

    n_u = len(units)
    kk_l = [lax.dot_general(kb_l[u], kbf_l[u], _NT, preferred_element_type=F32) for u in range(n_u)]
    qk_l = [lax.dot_general(qbf_l[u], kbf_l[u], _NT, preferred_element_type=F32) for u in range(n_u)]
    qk_l = [jnp.where(tri, qk_l[u] * decay_l[u], 0.0).astype(BF16) for u in range(n_u)]
    x_l = [(-jnp.where(strict, kk_l[u] * decay_l[u], 0.0)) for u in range(n_u)]
    t_l = [eye_f + x_l[u] for u in range(n_u)]
    x_l = [x.astype(BF16) for x in x_l]
    for _ in range(5):
        x_l = [jnp.dot(x, x, preferred_element_type=F32).astype(BF16) for x in x_l]
        t_l = [t + jnp.dot(t.astype(BF16), x, preferred_element_type=F32) for t, x in zip(t_l, x_l)]
    sol_l = [jnp.dot(t_l[u].astype(BF16), rhs_l[u], preferred_element_type=F32) for u in range(n_u)]

    s_cur = [s_scr[j] for j in range(GDN_HB)]
    for c in range(GDN_NC):
        r0 = c * C
        us = [c * GDN_HB + j for j in range(GDN_HB)]
        s_bf = [s.astype(BF16) for s in s_cur]
        ws_l = [jnp.dot(sol_l[u][:, GDN_DV:].astype(BF16), s_bf[j], preferred_element_type=F32)
                for j, u in enumerate(us)]
        qs_l = [jnp.dot(qd_l[u], s_bf[j], preferred_element_type=F32) for j, u in enumerate(us)]
        vn_l = [(sol_l[u][:, :GDN_DV] - ws_l[j]).astype(BF16) for j, u in enumerate(us)]
        kv_l = [lax.dot_general(kd_l[u], vn_l[j], _TN, preferred_element_type=F32) for j, u in enumerate(us)]
        ov_l = [jnp.dot(qk_l[u], vn_l[j], preferred_element_type=F32) for j, u in enumerate(us)]
        s_cur = [s_cur[j] * gl_l[u] + kv_l[j] for j, u in enumerate(us)]
        for j in range(GDN_HB):
            o = qs_l[j] + ov_l[j]
            zf = z_ref[r0:r0 + C, j * GDN_DV:(j + 1) * GDN_DV].astype(F32)
            o = o * lax.rsqrt(jnp.mean(o * o, axis=-1, keepdims=True) + RMS_EPS) * ng * _silu(zf)
            o_ref[r0:r0 + C, j * GDN_DV:(j + 1) * GDN_DV] = o.astype(o_ref.dtype)
    for j in range(GDN_HB):
        s_scr[j] = s_cur[j]


def _gdn_core(qkv, z, scal, hparams, norm_g):
    L = qkv.shape[0]
    G = GDN_V_HEADS // GDN_HB
    R = GDN_NC * GDN_CHUNK
    qw = (GDN_HB // 2) * GDN_DK
    vw = GDN_HB * GDN_DV
    k_blk0 = GDN_QK_W // qw
    v_blk0 = 2 * GDN_QK_W // vw
    vmem = 2 * (2 * R * qw * 2 + 2 * R * vw * 2 + R * LANES * 4 + R * vw * 2) + (16 << 20)
    return pl.pallas_call(
        _gdn_kernel,
        grid=(G, L // R),
        in_specs=[pl.BlockSpec((R, qw), lambda g, s: (s, g)),
                  pl.BlockSpec((R, qw), lambda g, s: (s, k_blk0 + g)),
                  pl.BlockSpec((R, vw), lambda g, s: (s, v_blk0 + g)),
                  pl.BlockSpec((R, vw), lambda g, s: (s, g)),
                  pl.BlockSpec((R, LANES), lambda g, s: (s, g)),
                  pl.BlockSpec((1, 8, LANES), lambda g, s: (g, 0, 0)),
                  pl.BlockSpec((1, GDN_DV), lambda g, s: (0, 0))],
        out_specs=pl.BlockSpec((R, vw), lambda g, s: (s, g)),
        out_shape=jax.ShapeDtypeStruct((L, GDN_V_W), BF16),
        scratch_shapes=[pltpu.VMEM((GDN_HB, GDN_DK, GDN_DV), F32)],
        compiler_params=_cparams(("parallel", "arbitrary"), vmem),
        name="gdn_core",
    )(qkv, qkv, qkv, z, scal, hparams, norm_g.reshape(1, GDN_DV))


def _gdn_mixer(xb, w_in, conv_w, a_log, dt_bias, norm_g):
    L = xb.shape[0]
    nqkv = 2 * GDN_QK_W + GDN_V_W
    w_qkv = w_in[:, :nqkv].astype(BF16)
    w_z = w_in[:, nqkv:nqkv + GDN_V_W].astype(BF16)
    w_ab = w_in[:, nqkv + GDN_V_W:]
    G = GDN_V_HEADS // GDN_HB
    w_a = w_ab[:, :GDN_V_HEADS].reshape(D_MODEL, G, GDN_HB)
    w_b = w_ab[:, GDN_V_HEADS:].reshape(D_MODEL, G, GDN_HB)
    w_sc = jnp.concatenate([w_a, w_b, jnp.zeros((D_MODEL, G, LANES - 2 * GDN_HB), F32)], axis=-1)
    w_sc = w_sc.reshape(D_MODEL, G * LANES).astype(BF16)
    qkv = _proj_conv_silu(xb, w_qkv, conv_w, tm=1024, tn=1024)
    ones_z = jnp.ones((1, GDN_V_W), F32)
    z = _matmul_scaled(xb, w_z, ones_z, BF16, tm=1024, tn=1024)
    scal = _matmul_scaled(xb, w_sc, jnp.ones((1, G * LANES), F32), F32, tm=1024, tn=G * LANES)
    hp = jnp.zeros((G, 8, LANES), F32)
    hp = hp.at[:, 0, :GDN_HB].set(a_log.reshape(G, GDN_HB))
    hp = hp.at[:, 1, :GDN_HB].set(dt_bias.reshape(G, GDN_HB))
    return _gdn_core(qkv, z, scal, hp, norm_g)


def _dsa_small_kernel(x_ref, w_ref, g_ref, b_ref, ki_ref, wi_ref):
    acc = jnp.dot(x_ref[...], w_ref[...], preferred_element_type=F32)
    ki = acc[:, :IDX_DIM]
    mu = jnp.mean(ki, axis=-1, keepdims=True)
    kc = ki - mu
    var = jnp.mean(kc * kc, axis=-1, keepdims=True)
    ki_ref[...] = (kc * lax.rsqrt(var + LN_EPS) * g_ref[...] + b_ref[...]).astype(ki_ref.dtype)
    wi_ref[...] = acc[:, IDX_DIM:] * ((IDX_HEADS ** -0.5) * (IDX_DIM ** -0.5))


def _dsa_small(xb, w_small, ln_g, ln_b, tm):
    M, K = xb.shape
    tm = min(tm, M)
    N = 2 * LANES
    vmem = 2 * (tm * K * 2 + K * N * 2 + tm * LANES * 6) + 4 * tm * N * 4
    return pl.pallas_call(
        _dsa_small_kernel,
        grid=(M // tm,),
        in_specs=[pl.BlockSpec((tm, K), lambda i: (i, 0)),
                  pl.BlockSpec((K, N), lambda i: (0, 0)),
                  pl.BlockSpec((1, IDX_DIM), lambda i: (0, 0)),
                  pl.BlockSpec((1, IDX_DIM), lambda i: (0, 0))],
        out_specs=[pl.BlockSpec((tm, IDX_DIM), lambda i: (i, 0)),
                   pl.BlockSpec((tm, LANES), lambda i: (i, 0))],
        out_shape=[jax.ShapeDtypeStruct((M, IDX_DIM), BF16), jax.ShapeDtypeStruct((M, LANES), F32)],
        compiler_params=_cparams(("parallel",), vmem),
        name="dsa_idx_proj",
    )(xb, w_small, ln_g.reshape(1, IDX_DIM), ln_b.reshape(1, IDX_DIM))


def _sortable_key(score):
    bits = lax.bitcast_convert_type(score, I32)
    return jnp.where(bits >= 0, bits, bits ^ jnp.int32(0x7FFFFFFF))


def _idx_kernel(qi_tab, kj_tab, last_tab,
                qidx_ref, kidx_ref, wi_ref, far_ref, near_ref, key_scr, *, k_top, n_sub_total):
    s = pl.program_id(0)
    i = qi_tab[s]
    j = kj_tab[s]
    nsub = IDX_KEY_BLOCK // IDX_SUB
    t_col = i * Q_BLOCK + lax.broadcasted_iota(I32, (Q_BLOCK, 1), 0)
    lane_sub = lax.broadcasted_iota(I32, (Q_BLOCK, IDX_SUB), 1)
    wi = wi_ref[...]

    for sub in range(nsub):
        ki_sub = kidx_ref[sub * IDX_SUB:(sub + 1) * IDX_SUB, :]
        acc = jnp.zeros((Q_BLOCK, IDX_SUB), F32)
        for h in range(IDX_HEADS):
            sc = lax.dot_general(qidx_ref[:, h * IDX_DIM:(h + 1) * IDX_DIM], ki_sub, _NT,
                                 preferred_element_type=F32)
            acc = acc + jnp.maximum(sc, 0.0) * wi[:, h:h + 1]
        s_idx = j * IDX_KEY_BLOCK + sub * IDX_SUB + lane_sub
        key_scr[j * nsub + sub] = jnp.where(s_idx <= t_col, _sortable_key(acc), INT_MIN)

    @pl.when(last_tab[s] == 1)
    def _():
        n_chunks = (j + 1) * nsub

        def count(pred, ref_val):
            refb = jnp.broadcast_to(ref_val, (Q_BLOCK, LANES))

            def body(c, cnt):
                blk = key_scr[c]
                for l in range(IDX_SUB // LANES):
                    cnt = cnt + jnp.where(pred(blk[:, l * LANES:(l + 1) * LANES], refb), 1, 0)
                return cnt

            cnt = lax.fori_loop(0, n_chunks, body, jnp.zeros((Q_BLOCK, LANES), I32))
            return jnp.sum(cnt, axis=1, keepdims=True)

        def bit_body(b, thr):
            cand = thr + jnp.left_shift(jnp.int32(1), 31 - b)
            return jnp.where(count(lambda a, r: a >= r, cand) >= k_top, cand, thr)

        thr = lax.fori_loop(0, 32, bit_body, jnp.full((Q_BLOCK, 1), INT_MIN, I32))
        n_gt = count(lambda a, r: a > r, thr)
        need_eq = (k_top - n_gt).astype(F32)

        thr_b = jnp.broadcast_to(thr, (Q_BLOCK, IDX_SUB))
        incl = (lax.broadcasted_iota(I32, (IDX_SUB, IDX_SUB), 0)
                <= lax.broadcasted_iota(I32, (IDX_SUB, IDX_SUB), 1)).astype(BF16)

        def sel_body(c, carry):
            blk = key_scr[c]
            eq = blk == thr_b
            eq_f = jnp.where(eq, 1.0, 0.0)
            rank = carry + jnp.dot(eq_f.astype(BF16), incl, preferred_element_type=F32)
            s_idx = c * IDX_SUB + lane_sub
            sel = ((blk > thr_b) | (eq & (rank <= need_eq))) & (s_idx <= t_col)
            far = sel & (t_col - s_idx >= MAX_DISTANCE)
            far_ref[0, c] = jnp.where(far, 0.0, -jnp.inf).astype(far_ref.dtype)
            key_scr[c] = jnp.where(sel, 1, 0)
            return carry + jnp.sum(eq_f, axis=1, keepdims=True)

        lax.fori_loop(0, n_chunks, sel_body, jnp.zeros((Q_BLOCK, 1), F32))

        def fill_body(c, carry):
            far_ref[0, c] = jnp.full((Q_BLOCK, IDX_SUB), -jnp.inf, far_ref.dtype)
            return carry

        lax.fori_loop(n_chunks, n_sub_total, fill_body, 0)

        def window(blk_idx):
            per = IDX_SUB // Q_BLOCK
            chunk = key_scr[blk_idx // per]
            m = blk_idx % per
            out = chunk[:, 0:Q_BLOCK]
            for q in range(1, per):
                out = jnp.where(m == q, chunk[:, q * Q_BLOCK:(q + 1) * Q_BLOCK], out)
            return out

        r_i = lax.broadcasted_iota(I32, (Q_BLOCK, Q_BLOCK), 0)
        c_i = lax.broadcasted_iota(I32, (Q_BLOCK, Q_BLOCK), 1)
        d_lo = Q_BLOCK + r_i - c_i
        d_hi = r_i - c_i
        near_lo = (window(jnp.maximum(i - 1, 0)) != 0) & (d_lo < MAX_DISTANCE) & (i >= 1)
        near_hi = (window(i) != 0) & (d_hi >= 0) & (d_hi < MAX_DISTANCE)
        near_ref[:, 0:Q_BLOCK] = jnp.where(near_lo, 0.0, -jnp.inf).astype(near_ref.dtype)
        near_ref[:, Q_BLOCK:2 * Q_BLOCK] = jnp.where(near_hi, 0.0, -jnp.inf).astype(near_ref.dtype)


def _idx_select(proj, ki, wi, k_top):
    L = ki.shape[0]
    nq = L // Q_BLOCK
    n_sub_total = L // IDX_SUB
    qi_l, kj_l, last_l = [], [], []
    for i in range(nq):
        j_last = (i * Q_BLOCK + Q_BLOCK - 1) // IDX_KEY_BLOCK
        for j in range(j_last + 1):
            qi_l.append(i)
            kj_l.append(j)
            last_l.append(1 if j == j_last else 0)
    tabs = [jnp.asarray(np.asarray(t, np.int32)) for t in (qi_l, kj_l, last_l)]
    qcol = (DSA_HEADS * DSA_DH) // (IDX_HEADS * IDX_DIM)
    vmem = (L * Q_BLOCK * 4 + 2 * (Q_BLOCK * L * 2) + 2 * (Q_BLOCK * IDX_HEADS * IDX_DIM * 2)
            + 2 * IDX_KEY_BLOCK * IDX_DIM * 2 + (8 << 20))
    grid_spec = pltpu.PrefetchScalarGridSpec(
        num_scalar_prefetch=3,
        grid=(len(qi_l),),
        in_specs=[pl.BlockSpec((Q_BLOCK, IDX_HEADS * IDX_DIM), lambda s, qt, kt, lt: (qt[s], qcol)),
                  pl.BlockSpec((IDX_KEY_BLOCK, IDX_DIM), lambda s, qt, kt, lt: (kt[s], 0)),
                  pl.BlockSpec((Q_BLOCK, LANES), lambda s, qt, kt, lt: (qt[s], 0))],
        out_specs=[pl.BlockSpec((1, n_sub_total, Q_BLOCK, IDX_SUB), lambda s, qt, kt, lt: (qt[s], 0, 0, 0)),
                   pl.BlockSpec((Q_BLOCK, 2 * Q_BLOCK), lambda s, qt, kt, lt: (qt[s], 0))],
        scratch_shapes=[pltpu.VMEM((n_sub_total, Q_BLOCK, IDX_SUB), I32)],
    )
    return pl.pallas_call(
        functools.partial(_idx_kernel, k_top=k_top, n_sub_total=n_sub_total),
        grid_spec=grid_spec,
        out_shape=[jax.ShapeDtypeStruct((nq, n_sub_total, Q_BLOCK, IDX_SUB), BF16),
                   jax.ShapeDtypeStruct((L, 2 * Q_BLOCK), BF16)],
        compiler_params=_cparams(("arbitrary",), vmem),
        name="dsa_idx_select",
    )(*tabs, proj, ki, wi)


M_INIT = -1e30
FAR_SUBS = 2
LOG2E = math.log2(math.e)


def _attn_kernel(qi_tab, kj_tab, kind_tab, first_tab,
                 tab_ref, q_ref, kf_ref, vf_ref, klo_ref, khi_ref, vlo_ref, vhi_ref, far_ref, near_ref,
                 o_ref, m_scr, l_scr, acc_scr, b_scr, s_scr):
    s = pl.program_id(0)

    @pl.when(s == 0)
    def _():
        r_i = lax.broadcasted_iota(I32, (Q_BLOCK, 2 * Q_BLOCK), 0)
        c_i = lax.broadcasted_iota(I32, (Q_BLOCK, 2 * Q_BLOCK), 1)
        d = jnp.maximum(Q_BLOCK + r_i - c_i, 0)
        max_exact = N_BUCKETS // 2
        df = jnp.maximum(d, 1).astype(F32)
        large = max_exact + (jnp.log(df / max_exact) / math.log(MAX_DISTANCE / max_exact)
                             * (N_BUCKETS - max_exact)).astype(I32)
        large = jnp.minimum(large, N_BUCKETS - 1)
        bkt = jnp.where(d < max_exact, d, large)
        for h in range(DSA_HEADS):
            acc = jnp.zeros((Q_BLOCK, 2 * Q_BLOCK), F32)
            for b in range(N_BUCKETS):
                acc = jnp.where(bkt == b, (tab_ref[b, h] - tab_ref[N_BUCKETS - 1, h]) * LOG2E, acc)
            b_scr[h] = acc

    @pl.when(first_tab[s] == 1)
    def _():
        m_scr[...] = jnp.full_like(m_scr, M_INIT)
        l_scr[...] = jnp.zeros_like(l_scr)
        acc_scr[...] = jnp.zeros_like(acc_scr)

    def run_units(units, width, row0=0):
        nt = width // LANES
        rows = slice(row0, row0 + width)
        for n, (h, k_ref, _, add) in enumerate(units):
            g = h // DSA_GROUP
            logits = lax.dot_general(q_ref[:, h * DSA_DH:(h + 1) * DSA_DH],
                                     k_ref[rows, g * DSA_DH:(g + 1) * DSA_DH], _NT, preferred_element_type=F32)
            s_scr[n, :, 0:width] = logits + add(h)
        for n, (h, _, v_ref, _) in enumerate(units):
            g = h // DSA_GROUP
            tiles = [s_scr[n, :, t * LANES:(t + 1) * LANES] for t in range(nt)]
            tmax = tiles[0]
            for t in range(1, nt):
                tmax = jnp.maximum(tmax, tiles[t])
            m_prev = m_scr[h]
            m_new = jnp.maximum(m_prev, jnp.max(tmax, axis=-1, keepdims=True))
            alpha = jnp.exp2(m_prev - m_new)
            p = [jnp.exp2(tiles[t] - m_new) for t in range(nt)]
            psum = p[0]
            for t in range(1, nt):
                psum = psum + p[t]
            l_scr[h] = alpha * l_scr[h] + psum
            pb = jnp.concatenate([x.astype(BF16) for x in p], axis=-1) if nt > 1 else p[0].astype(BF16)
            acc_scr[h] = alpha * acc_scr[h] + jnp.dot(pb, v_ref[rows, g * DSA_DH:(g + 1) * DSA_DH],
                                                      preferred_element_type=F32)
            m_scr[h] = m_new

    @pl.when(kind_tab[s] == 0)
    def _():
        for sub in range(FAR_SUBS):
            mask = far_ref[0, sub].astype(F32)
            run_units([(h, kf_ref, vf_ref, lambda h, mask=mask: mask) for h in range(DSA_HEADS)],
                      IDX_SUB, sub * IDX_SUB)

    @pl.when(kind_tab[s] == 1)
    def _():
        for half, (k_ref, v_ref) in enumerate(((klo_ref, vlo_ref), (khi_ref, vhi_ref))):
            cols = slice(half * Q_BLOCK, (half + 1) * Q_BLOCK)
            mask = near_ref[:, cols].astype(F32)
            run_units([(h, k_ref, v_ref, lambda h, mask=mask, cols=cols: mask + b_scr[h][:, cols])
                       for h in range(DSA_HEADS)], Q_BLOCK)
        for h in range(DSA_HEADS):
            l_row = jnp.sum(l_scr[h], axis=-1, keepdims=True)
            o_ref[:, h * DSA_DH:(h + 1) * DSA_DH] = (acc_scr[h] / l_row).astype(o_ref.dtype)


def _masked_attention(proj, far, near, rel_bias):
    L = proj.shape[0]
    nq = L // Q_BLOCK
    far_keys = FAR_SUBS * IDX_SUB
    per = far_keys // Q_BLOCK
    qi_l, kj_l, kind_l, first_l = [], [], [], []
    for i in range(nq):
        n_far = -(-i // per)
        for j in range(n_far):
            qi_l.append(i); kj_l.append(j); kind_l.append(0); first_l.append(1 if j == 0 else 0)
        qi_l.append(i); kj_l.append(max(n_far - 1, 0)); kind_l.append(1); first_l.append(1 if n_far == 0 else 0)
    tabs = [jnp.asarray(np.asarray(t, np.int32)) for t in (qi_l, kj_l, kind_l, first_l)]
    qw = DSA_HEADS * DSA_DH
    kvw = DSA_KV_HEADS * DSA_DH
    k_col = (2 * qw) // kvw
    v_col = k_col + 1
    hw = DSA_HEADS
    vmem = (2 * (Q_BLOCK * qw * 2 * 2 + 2 * far_keys * kvw * 2 + 4 * Q_BLOCK * kvw * 2
                 + Q_BLOCK * far_keys * 2 + Q_BLOCK * 2 * Q_BLOCK * 2)
            + hw * Q_BLOCK * (3 * LANES + 2 * Q_BLOCK) * 4 + (16 << 20))
    idx = lambda f: (lambda s, qt, kt, kd, ft: f(qt[s], kt[s]))
    grid_spec = pltpu.PrefetchScalarGridSpec(
        num_scalar_prefetch=4,
        grid=(len(qi_l),),
        in_specs=[pl.BlockSpec(memory_space=pltpu.SMEM),
                  pl.BlockSpec((Q_BLOCK, qw), idx(lambda i, j: (i, 0))),
                  pl.BlockSpec((far_keys, kvw), idx(lambda i, j: (j, k_col))),
                  pl.BlockSpec((far_keys, kvw), idx(lambda i, j: (j, v_col))),
                  pl.BlockSpec((Q_BLOCK, kvw), idx(lambda i, j: (jnp.maximum(i - 1, 0), k_col))),
                  pl.BlockSpec((Q_BLOCK, kvw), idx(lambda i, j: (i, k_col))),
                  pl.BlockSpec((Q_BLOCK, kvw), idx(lambda i, j: (jnp.maximum(i - 1, 0), v_col))),
                  pl.BlockSpec((Q_BLOCK, kvw), idx(lambda i, j: (i, v_col))),
                  pl.BlockSpec((1, FAR_SUBS, Q_BLOCK, IDX_SUB), idx(lambda i, j: (i, j, 0, 0))),
                  pl.BlockSpec((Q_BLOCK, 2 * Q_BLOCK), idx(lambda i, j: (i, 0)))],
        out_specs=pl.BlockSpec((Q_BLOCK, qw), idx(lambda i, j: (i, 0))),
        scratch_shapes=[pltpu.VMEM((hw, Q_BLOCK, LANES), F32),
                        pltpu.VMEM((hw, Q_BLOCK, LANES), F32),
                        pltpu.VMEM((hw, Q_BLOCK, DSA_DH), F32),
                        pltpu.VMEM((hw, Q_BLOCK, 2 * Q_BLOCK), F32),
                        pltpu.VMEM((hw, Q_BLOCK, IDX_SUB), F32)],
    )
    return pl.pallas_call(
        _attn_kernel,
        grid_spec=grid_spec,
        out_shape=jax.ShapeDtypeStruct((L, qw), BF16),
        compiler_params=_cparams(("arbitrary",), vmem),
        name="dsa_attention",
    )(*tabs, rel_bias, proj, proj, proj, proj, proj, proj, proj, far, near)


def _dsa_mixer(xb, w_in, ln_g, ln_b, rel_bias):
    L = xb.shape[0]
    k_top = min(TOPK_MAX, L // 4)
    sq = DSA_HEADS * DSA_DH
    skv = DSA_KV_HEADS * DSA_DH
    si = IDX_HEADS * IDX_DIM
    w_q = w_in[:, :sq]
    w_k = w_in[:, sq:sq + skv]
    w_v = w_in[:, sq + skv:sq + 2 * skv]
    w_qi = w_in[:, sq + 2 * skv:sq + 2 * skv + si]
    w_ki = w_in[:, sq + 2 * skv + si:sq + 2 * skv + si + IDX_DIM]
    w_wi = w_in[:, sq + 2 * skv + si + IDX_DIM:]
    w_main = jnp.concatenate([w_q, w_qi, w_k, w_v], axis=1).astype(BF16)
    colscale = jnp.concatenate([jnp.full((1, sq), DSA_DH ** -0.5 * LOG2E, F32),
                                jnp.ones((1, si + 2 * skv), F32)], axis=1)
    w_small = jnp.concatenate([w_ki, w_wi, jnp.zeros((D_MODEL, LANES - IDX_HEADS), F32)], axis=1).astype(BF16)
    proj = _matmul_scaled(xb, w_main, colscale, BF16, tm=1024, tn=1024)
    ki, wi = _dsa_small(xb, w_small, ln_g, ln_b, tm=1024)
    far, near = _idx_select(proj, ki, wi, k_top)
    return _masked_attention(proj, far, near, rel_bias)


def kernel(x, p, gdn_w_in, gdn_conv_w, gdn_a_log, gdn_dt_bias, gdn_norm_g, gdn_w_o, dsa_w_in, dsa_kidx_ln_g, dsa_kidx_ln_b, dsa_w_o, rel_bias, ln1_g, ln1_b, ffn_w_gate, ffn_w_up, ffn_conv_w, ffn_w_down, ln2_g, ln2_b, ple_w_proj, ple_w_gate):
    assert x.shape[0] == 1 and x.shape[2] == D_MODEL
    xf = x[0]
    xb = xf.astype(BF16)
    ia = ib = 0
    for i in range(DEPTH):
        if i % 2 == 0:
            mix = _gdn_mixer(xb, gdn_w_in[ia], gdn_conv_w[ia], gdn_a_log[ia], gdn_dt_bias[ia], gdn_norm_g[ia])
            w_o = gdn_w_o[ia]
            ia += 1
        else:
            mix = _dsa_mixer(xb, dsa_w_in[ib], dsa_kidx_ln_g[ib], dsa_kidx_ln_b[ib], rel_bias)
            w_o = dsa_w_o[ib]
            ib += 1
        xf, xb = _proj_res_ln(mix, w_o.astype(BF16), xf, ln1_g[i], ln1_b[i], tm=512, sub=256)
        hmid = _ffn_up(xb, ffn_w_gate[i].astype(BF16), ffn_w_up[i].astype(BF16), ffn_conv_w[i], tm=1024, tn=512)
        xf, xb = _proj_res_ln(hmid, ffn_w_down[i].astype(BF16), xf, ln2_g[i], ln2_b[i], tm=512, sub=256)
        xf, xb = _ple(xb, xf, ple_w_gate[i].astype(BF16), p[i, 0], ple_w_proj[i].astype(BF16), tm=1024, tn=1024)
    return xf[None]
```

```python
import functools
import math

import jax
import jax.numpy as jnp
import numpy as np
from jax import lax
from jax.experimental import pallas as pl
from jax.experimental.pallas import tpu as pltpu

F32 = jnp.float32
BF16 = jnp.bfloat16
I32 = jnp.int32

D_MODEL = 2048
GDN_QK_HEADS = 16
GDN_V_HEADS = 32
GDN_DK = 128
GDN_DV = 128
GDN_CONV = 4
GDN_CHUNK = 64
GDN_QK_W = GDN_QK_HEADS * GDN_DK
GDN_V_W = GDN_V_HEADS * GDN_DV
DSA_HEADS = 16
DSA_KV_HEADS = 4
DSA_GROUP = DSA_HEADS // DSA_KV_HEADS
DSA_DH = 128
IDX_HEADS = 16
IDX_DIM = 128
TOPK_MAX = 256
N_BUCKETS = 32
MAX_DISTANCE = 128
D_FF = 5120
FFN_CONV = 3
PLE_DIM = 256
DEPTH = 2
DN_ALPHA = (2.0 * DEPTH) ** 0.25
LN_EPS = 1e-5
RMS_EPS = 1e-6

V7X_VMEM_BYTES = 64 * 1024 * 1024
V7X_VMEM_BUDGET = 56 * 1024 * 1024
LANES = 128
BF16_SUBLANES = 16

HALO = BF16_SUBLANES
Q_BLOCK = 128
IDX_KEY_BLOCK = 2048
IDX_SUB = 512
INT_MIN = -(2 ** 31)

_NT = (((1,), (1,)), ((), ()))
_TN = (((0,), (0,)), ((), ()))


def _cparams(semantics, vmem_bytes):
    return pltpu.CompilerParams(dimension_semantics=semantics,
                                vmem_limit_bytes=int(min(V7X_VMEM_BUDGET, vmem_bytes)))


def _silu(y):
    return y * jax.nn.sigmoid(y)


def _mm_scale_kernel(x_ref, w_ref, cs_ref, o_ref):
    acc = jnp.dot(x_ref[...], w_ref[...], preferred_element_type=F32)
    o_ref[...] = (acc * cs_ref[...]).astype(o_ref.dtype)


def _matmul_scaled(x, w, colscale, out_dtype, tm, tn):
    M, K = x.shape
    N = w.shape[1]
    tm, tn = min(tm, M), min(tn, N)
    osz = jnp.dtype(out_dtype).itemsize
    vmem = 2 * (tm * K * 2 + K * tn * 2 + tm * tn * osz) + 2 * tm * tn * 4
    return pl.pallas_call(
        _mm_scale_kernel,
        grid=(M // tm, N // tn),
        in_specs=[pl.BlockSpec((tm, K), lambda i, j: (i, 0)),
                  pl.BlockSpec((K, tn), lambda i, j: (0, j)),
                  pl.BlockSpec((1, tn), lambda i, j: (0, j))],
        out_specs=pl.BlockSpec((tm, tn), lambda i, j: (i, j)),
        out_shape=jax.ShapeDtypeStruct((M, N), out_dtype),
        compiler_params=_cparams(("parallel", "parallel"), vmem),
        name="matmul_scaled",
    )(x, w, colscale)


CONV_SUB = 256
FFN_SUB = 512


CONV_ROWS = 1024


def _causal_conv(g, gh, cw_ref, g_scr, kc, tm, cols, r0=0):
    if gh is not None:
        g_scr[0:HALO, cols] = gh
    g_scr[HALO + r0:HALO + r0 + tm, cols] = g
    y = cw_ref[kc - 1:kc, cols] * g
    for j in range(kc - 1):
        off = HALO + r0 - (kc - 1) + j
        y = y + cw_ref[j:j + 1, cols] * g_scr[off:off + tm, cols]
    return y


def _mm_conv_silu_kernel(x_ref, xh_ref, w_ref, cw_ref, o_ref, g_scr, *, kc, tm):
    first = pl.program_id(0) == 0
    rs = min(CONV_ROWS, tm)
    for c0 in range(0, o_ref.shape[1], CONV_SUB):
        cols = slice(c0, c0 + CONV_SUB)
        w = w_ref[:, cols]
        gh = jnp.dot(xh_ref[...], w, preferred_element_type=F32)
        gh = jnp.where(first, 0.0, gh)
        for r0 in range(0, tm, rs):
            g = jnp.dot(x_ref[r0:r0 + rs, :], w, preferred_element_type=F32)
            y = _causal_conv(g, gh if r0 == 0 else None, cw_ref, g_scr, kc, rs, cols, r0)
            o_ref[r0:r0 + rs, cols] = _silu(y).astype(o_ref.dtype)


def _proj_conv_silu(x, w, conv_w, tm, tn):
    M, K = x.shape
    N = w.shape[1]
    kc = conv_w.shape[0]
    tm, tn = min(tm, M), min(tn, N)
    hb = tm // HALO
    vmem = 2 * (tm * K * 2 + HALO * K * 2 + K * tn * 2 + tm * tn * 2) + 4 * tm * tn * 4
    return pl.pallas_call(
        functools.partial(_mm_conv_silu_kernel, kc=kc, tm=tm),
        grid=(M // tm, N // tn),
        in_specs=[pl.BlockSpec((tm, K), lambda i, j: (i, 0)),
                  pl.BlockSpec((HALO, K), lambda i, j: (jnp.maximum(i * hb - 1, 0), 0)),
                  pl.BlockSpec((K, tn), lambda i, j: (0, j)),
                  pl.BlockSpec((kc, tn), lambda i, j: (0, j))],
        out_specs=pl.BlockSpec((tm, tn), lambda i, j: (i, j)),
        out_shape=jax.ShapeDtypeStruct((M, N), BF16),
        scratch_shapes=[pltpu.VMEM((tm + HALO, tn), F32)],
        compiler_params=_cparams(("parallel", "parallel"), vmem),
        name="proj_conv_silu",
    )(x, x, w, conv_w)


def _ffn_up_kernel(x_ref, xh_ref, wg_ref, wu_ref, cw_ref, o_ref, g_scr, *, kc, tm):
    first = pl.program_id(0) == 0
    for c0 in range(0, o_ref.shape[1], FFN_SUB):
        cols = slice(c0, c0 + FFN_SUB)
        wg = wg_ref[:, cols]
        g = jnp.dot(x_ref[...], wg, preferred_element_type=F32)
        gh = jnp.dot(xh_ref[...], wg, preferred_element_type=F32)
        gh = jnp.where(first, 0.0, gh)
        u = jnp.dot(x_ref[...], wu_ref[:, cols], preferred_element_type=F32)
        y = _causal_conv(g, gh, cw_ref, g_scr, kc, tm, cols)
        o_ref[:, cols] = (_silu(y) * u).astype(o_ref.dtype)


def _ffn_up(x, w_gate, w_up, conv_w, tm, tn):
    M, K = x.shape
    N = w_gate.shape[1]
    kc = conv_w.shape[0]
    tm, tn = min(tm, M), min(tn, N)
    hb = tm // HALO
    vmem = 2 * (tm * K * 2 + HALO * K * 2 + 2 * K * tn * 2 + tm * tn * 2) + 6 * tm * tn * 4
    return pl.pallas_call(
        functools.partial(_ffn_up_kernel, kc=kc, tm=tm),
        grid=(M // tm, N // tn),
        in_specs=[pl.BlockSpec((tm, K), lambda i, j: (i, 0)),
                  pl.BlockSpec((HALO, K), lambda i, j: (jnp.maximum(i * hb - 1, 0), 0)),
                  pl.BlockSpec((K, tn), lambda i, j: (0, j)),
                  pl.BlockSpec((K, tn), lambda i, j: (0, j)),
                  pl.BlockSpec((kc, tn), lambda i, j: (0, j))],
        out_specs=pl.BlockSpec((tm, tn), lambda i, j: (i, j)),
        out_shape=jax.ShapeDtypeStruct((M, N), BF16),
        scratch_shapes=[pltpu.VMEM((tm + HALO, tn), F32)],
        compiler_params=_cparams(("parallel", "parallel"), vmem),
        name="ffn_up",
    )(x, x, w_gate, w_up, conv_w)


def _mm_res_ln_kernel(a_ref, w_ref, res_ref, g_ref, b_ref, of_ref, ob_ref, *, sub):
    for r0 in range(0, a_ref.shape[0], sub):
        rows = slice(r0, r0 + sub)
        acc = jnp.dot(a_ref[rows, :], w_ref[...], preferred_element_type=F32)
        y = DN_ALPHA * res_ref[rows, :] + acc
        mu = jnp.mean(y, axis=-1, keepdims=True)
        yc = y - mu
        var = jnp.mean(yc * yc, axis=-1, keepdims=True)
        out = yc * lax.rsqrt(var + LN_EPS) * g_ref[...] + b_ref[...]
        of_ref[rows, :] = out
        ob_ref[rows, :] = out.astype(BF16)


def _proj_res_ln(a, w, res, g, b, tm, sub):
    M, K = a.shape
    N = w.shape[1]
    tm = min(tm, M)
    sub = min(sub, tm)
    vmem = K * N * 2 + 2 * (tm * K * 2 + tm * N * 4 + tm * N * 4 + tm * N * 2) + 4 * sub * N * 4
    return pl.pallas_call(
        functools.partial(_mm_res_ln_kernel, sub=sub),
        grid=(M // tm,),
        in_specs=[pl.BlockSpec((tm, K), lambda i: (i, 0)),
                  pl.BlockSpec((K, N), lambda i: (0, 0), pipeline_mode=pl.Buffered(1)),
                  pl.BlockSpec((tm, N), lambda i: (i, 0)),
                  pl.BlockSpec((1, N), lambda i: (0, 0)),
                  pl.BlockSpec((1, N), lambda i: (0, 0))],
        out_specs=[pl.BlockSpec((tm, N), lambda i: (i, 0)),
                   pl.BlockSpec((tm, N), lambda i: (i, 0))],
        out_shape=[jax.ShapeDtypeStruct((M, N), F32), jax.ShapeDtypeStruct((M, N), BF16)],
        compiler_params=_cparams(("parallel",), vmem),
        name="proj_res_ln",
    )(a, w, res, g.reshape(1, N), b.reshape(1, N))


def _ple_kernel(xb_ref, wg_ref, p_ref, wp_ref, xr_ref, of_ref, ob_ref):
    gate = jax.nn.sigmoid(jnp.dot(xb_ref[...], wg_ref[...], preferred_element_type=F32))
    pe = jnp.dot(p_ref[...].astype(BF16), wp_ref[...], preferred_element_type=F32)
    out = xr_ref[...] + gate * pe
    of_ref[...] = out
    ob_ref[...] = out.astype(BF16)


def _ple(xb, xf, w_gate, p, w_proj, tm, tn):
    M, K = xb.shape
    N = w_gate.shape[1]
    P = p.shape[1]
    tm, tn = min(tm, M), min(tn, N)
    vmem = 2 * (tm * K * 2 + K * tn * 2 + tm * P * 4 + P * tn * 2 + tm * tn * 10) + 4 * tm * tn * 4
    return pl.pallas_call(
        _ple_kernel,
        grid=(M // tm, N // tn),
        in_specs=[pl.BlockSpec((tm, K), lambda i, j: (i, 0)),
                  pl.BlockSpec((K, tn), lambda i, j: (0, j)),
                  pl.BlockSpec((tm, P), lambda i, j: (i, 0)),
                  pl.BlockSpec((P, tn), lambda i, j: (0, j)),
                  pl.BlockSpec((tm, tn), lambda i, j: (i, j))],
        out_specs=[pl.BlockSpec((tm, tn), lambda i, j: (i, j)),
                   pl.BlockSpec((tm, tn), lambda i, j: (i, j))],
        out_shape=[jax.ShapeDtypeStruct((M, N), F32), jax.ShapeDtypeStruct((M, N), BF16)],
        compiler_params=_cparams(("parallel", "parallel"), vmem),
        name="ple",
    )(xb, w_gate, p, w_proj, xf)


GDN_HB = 4
GDN_NC = 4


def _gdn_kernel(q_ref, k_ref, v_ref, z_ref, sc_ref, hp_ref, ng_ref, o_ref, s_scr):
    C = GDN_CHUNK

    @pl.when(pl.program_id(1) == 0)
    def _():
        s_scr[...] = jnp.zeros_like(s_scr)

    row = lax.broadcasted_iota(I32, (C, C), 0)
    col = lax.broadcasted_iota(I32, (C, C), 1)
    tri = row >= col
    strict = row > col
    eye = row == col
    tri_f = tri.astype(F32)
    eye_f = eye.astype(F32)

    raw = sc_ref[...]
    a_log = hp_ref[0, 0:1, :]
    dt_b = hp_ref[0, 1:2, :]
    xs = raw + dt_b
    softplus = jnp.maximum(xs, 0.0) + jnp.log1p(jnp.exp(-jnp.abs(xs)))
    g_all = -jnp.exp(a_log) * softplus
    beta_all = jax.nn.sigmoid(raw)
    ng = ng_ref[...]

    units = [(c, j) for c in range(GDN_NC) for j in range(GDN_HB)]
    kb_l, rhs_l, decay_l, qd_l, kd_l, kbf_l, qbf_l, gl_l = [], [], [], [], [], [], [], []
    for c in range(GDN_NC):
        r0 = c * C
        gc = jnp.dot(tri_f, g_all[r0:r0 + C, :], precision=lax.Precision.HIGHEST,
                     preferred_element_type=F32)
        g_last = gc[C - 1:C, :]
        e_gc = jnp.exp(gc)
        e_rest = jnp.exp(g_last - gc)
        e_last = jnp.exp(g_last)
        beta_c = beta_all[r0:r0 + C, :]
        qn, kn = [], []
        for hq in range(GDN_HB // 2):
            qf = q_ref[r0:r0 + C, hq * GDN_DK:(hq + 1) * GDN_DK].astype(F32)
            kf = k_ref[r0:r0 + C, hq * GDN_DK:(hq + 1) * GDN_DK].astype(F32)
            qn.append(qf * lax.rsqrt(jnp.sum(qf * qf, axis=-1, keepdims=True) + RMS_EPS) * (GDN_DK ** -0.5))
            kn.append(kf * lax.rsqrt(jnp.sum(kf * kf, axis=-1, keepdims=True) + RMS_EPS))
        for j in range(GDN_HB):
            q_h, k_h = qn[j // 2], kn[j // 2]
            vf = v_ref[r0:r0 + C, j * GDN_DV:(j + 1) * GDN_DV].astype(F32)
            beta = beta_c[:, GDN_HB + j:GDN_HB + j + 1]
            kb = k_h * beta
            gcb = jnp.broadcast_to(gc[:, j:j + 1], (C, C))
            gcr = jnp.sum(jnp.where(eye, gcb, 0.0), axis=0, keepdims=True)
            decay_l.append(jnp.where(tri, jnp.exp(jnp.where(tri, gcb - gcr, 0.0)), 0.0))
            kb_l.append(kb.astype(BF16))
            rhs_l.append(jnp.concatenate([vf * beta, kb * e_gc[:, j:j + 1]], axis=-1).astype(BF16))
            qd_l.append((q_h * e_gc[:, j:j + 1]).astype(BF16))
            kd_l.append((k_h * e_rest[:, j:j + 1]).astype(BF16))
            kbf_l.append(k_h.astype(BF16))
            qbf_l.append(q_h.astype(BF16))
            gl_l.append(e_last[:, j:j + 1])

    n_u = len(units)
    kk_l = [lax.dot_general(kb_l[u], kbf_l[u], _NT, preferred_element_type=F32) for u in range(n_u)]
    qk_l = [lax.dot_general(qbf_l[u], kbf_l[u], _NT, preferred_element_type=F32) for u in range(n_u)]
    qk_l = [jnp.where(tri, qk_l[u] * decay_l[u], 0.0).astype(BF16) for u in range(n_u)]
    x_l = [(-jnp.where(strict, kk_l[u] * decay_l[u], 0.0)) for u in range(n_u)]
    t_l = [eye_f + x_l[u] for u in range(n_u)]
    x_l = [x.astype(BF16) for x in x_l]
    for _ in range(5):
        x_l = [jnp.dot(x, x, preferred_element_type=F32).astype(BF16) for x in x_l]
        t_l = [t + jnp.dot(t.astype(BF16), x, preferred_element_type=F32) for t, x in zip(t_l, x_l)]
    sol_l = [jnp.dot(t_l[u].astype(BF16), rhs_l[u], preferred_element_type=F32) for u in range(n_u)]

    s_cur = [s_scr[j] for j in range(GDN_HB)]
    for c in range(GDN_NC):
        r0 = c * C
        us = [c * GDN_HB + j for j in range(GDN_HB)]
        s_bf = [s.astype(BF16) for s in s_cur]
        ws_l = [jnp.dot(sol_l[u][:, GDN_DV:].astype(BF16), s_bf[j], preferred_element_type=F32)
                for j, u in enumerate(us)]
        qs_l = [jnp.dot(qd_l[u], s_bf[j], preferred_element_type=F32) for j, u in enumerate(us)]
        vn_l = [(sol_l[u][:, :GDN_DV] - ws_l[j]).astype(BF16) for j, u in enumerate(us)]
        kv_l = [lax.dot_general(kd_l[u], vn_l[j], _TN, preferred_element_type=F32) for j, u in enumerate(us)]
        ov_l = [jnp.dot(qk_l[u], vn_l[j], preferred_element_type=F32) for j, u in enumerate(us)]
        s_cur = [s_cur[j] * gl_l[u] + kv_l[j] for j, u in enumerate(us)]
        for j in range(GDN_HB):
            o = qs_l[j] + ov_l[j]
            zf = z_ref[r0:r0 + C, j * GDN_DV:(j + 1) * GDN_DV].astype(F32)
            o = o * lax.rsqrt(jnp.mean(o * o, axis=-1, keepdims=True) + RMS_EPS) * ng * _silu(zf)
            o_ref[r0:r0 + C, j * GDN_DV:(j + 1) * GDN_DV] = o.astype(o_ref.dtype)
    for j in range(GDN_HB):
        s_scr[j] = s_cur[j]


def _gdn_core(qkv, z, scal, hparams, norm_g):
    L = qkv.shape[0]
    G = GDN_V_HEADS // GDN_HB
    R = GDN_NC * GDN_CHUNK
    qw = (GDN_HB // 2) * GDN_DK
    vw = GDN_HB * GDN_DV
    k_blk0 = GDN_QK_W // qw
    v_blk0 = 2 * GDN_QK_W // vw
    vmem = 2 * (2 * R * qw * 2 + 2 * R * vw * 2 + R * LANES * 4 + R * vw * 2) + (16 << 20)
    return pl.pallas_call(
        _gdn_kernel,
        grid=(G, L // R),
        in_specs=[pl.BlockSpec((R, qw), lambda g, s: (s, g)),
                  pl.BlockSpec((R, qw), lambda g, s: (s, k_blk0 + g)),
                  pl.BlockSpec((R, vw), lambda g, s: (s, v_blk0 + g)),
                  pl.BlockSpec((R, vw), lambda g, s: (s, g)),
                  pl.BlockSpec((R, LANES), lambda g, s: (s, g)),
                  pl.BlockSpec((1, 8, LANES), lambda g, s: (g, 0, 0)),
                  pl.BlockSpec((1, GDN_DV), lambda g, s: (0, 0))],
        out_specs=pl.BlockSpec((R, vw), lambda g, s: (s, g)),
        out_shape=jax.ShapeDtypeStruct((L, GDN_V_W), BF16),
        scratch_shapes=[pltpu.VMEM((GDN_HB, GDN_DK, GDN_DV), F32)],
        compiler_params=_cparams(("parallel", "arbitrary"), vmem),
        name="gdn_core",
    )(qkv, qkv, qkv, z, scal, hparams, norm_g.reshape(1, GDN_DV))


def _gdn_mixer(xb, w_in, conv_w, a_log, dt_bias, norm_g):
    L = xb.shape[0]
    nqkv = 2 * GDN_QK_W + GDN_V_W
    w_qkv = w_in[:, :nqkv].astype(BF16)
    w_z = w_in[:, nqkv:nqkv + GDN_V_W].astype(BF16)
    w_ab = w_in[:, nqkv + GDN_V_W:]
    G = GDN_V_HEADS // GDN_HB
    w_a = w_ab[:, :GDN_V_HEADS].reshape(D_MODEL, G, GDN_HB)
    w_b = w_ab[:, GDN_V_HEADS:].reshape(D_MODEL, G, GDN_HB)
    w_sc = jnp.concatenate([w_a, w_b, jnp.zeros((D_MODEL, G, LANES - 2 * GDN_HB), F32)], axis=-1)
    w_sc = w_sc.reshape(D_MODEL, G * LANES).astype(BF16)
    qkv = _proj_conv_silu(xb, w_qkv, conv_w, tm=1024, tn=1024)
    ones_z = jnp.ones((1, GDN_V_W), F32)
    z = _matmul_scaled(xb, w_z, ones_z, BF16, tm=1024, tn=1024)
    scal = _matmul_scaled(xb, w_sc, jnp.ones((1, G * LANES), F32), F32, tm=1024, tn=G * LANES)
    hp = jnp.zeros((G, 8, LANES), F32)
    hp = hp.at[:, 0, :GDN_HB].set(a_log.reshape(G, GDN_HB))
    hp = hp.at[:, 1, :GDN_HB].set(dt_bias.reshape(G, GDN_HB))
    return _gdn_core(qkv, z, scal, hp, norm_g)


def _dsa_small_kernel(x_ref, w_ref, g_ref, b_ref, ki_ref, wi_ref):
    acc = jnp.dot(x_ref[...], w_ref[...], preferred_element_type=F32)
    ki = acc[:, :IDX_DIM]
    mu = jnp.mean(ki, axis=-1, keepdims=True)
    kc = ki - mu
    var = jnp.mean(kc * kc, axis=-1, keepdims=True)
    ki_ref[...] = (kc * lax.rsqrt(var + LN_EPS) * g_ref[...] + b_ref[...]).astype(ki_ref.dtype)
    wi_ref[...] = acc[:, IDX_DIM:] * ((IDX_HEADS ** -0.5) * (IDX_DIM ** -0.5))


def _dsa_small(xb, w_small, ln_g, ln_b, tm):
    M, K = xb.shape
    tm = min(tm, M)
    N = 2 * LANES
    vmem = 2 * (tm * K * 2 + K * N * 2 + tm * LANES * 6) + 4 * tm * N * 4
    return pl.pallas_call(
        _dsa_small_kernel,
        grid=(M // tm,),
        in_specs=[pl.BlockSpec((tm, K), lambda i: (i, 0)),
                  pl.BlockSpec((K, N), lambda i: (0, 0)),
                  pl.BlockSpec((1, IDX_DIM), lambda i: (0, 0)),
                  pl.BlockSpec((1, IDX_DIM), lambda i: (0, 0))],
        out_specs=[pl.BlockSpec((tm, IDX_DIM), lambda i: (i, 0)),
                   pl.BlockSpec((tm, LANES), lambda i: (i, 0))],
        out_shape=[jax.ShapeDtypeStruct((M, IDX_DIM), BF16), jax.ShapeDtypeStruct((M, LANES), F32)],
        compiler_params=_cparams(("parallel",), vmem),
        name="dsa_idx_proj",
    )(xb, w_small, ln_g.reshape(1, IDX_DIM), ln_b.reshape(1, IDX_DIM))


def _sortable_key(score):
    bits = lax.bitcast_convert_type(score, I32)
    return jnp.where(bits >= 0, bits, bits ^ jnp.int32(0x7FFFFFFF))


def _idx_kernel(qi_tab, kj_tab, last_tab,
                qidx_ref, kidx_ref, wi_ref, far_ref, near_ref, key_scr, *, k_top, n_sub_total):
    s = pl.program_id(0)
    i = qi_tab[s]
    j = kj_tab[s]
    nsub = IDX_KEY_BLOCK // IDX_SUB
    t_col = i * Q_BLOCK + lax.broadcasted_iota(I32, (Q_BLOCK, 1), 0)
    lane_sub = lax.broadcasted_iota(I32, (Q_BLOCK, IDX_SUB), 1)
    wi = wi_ref[...]

    for sub in range(nsub):
        ki_sub = kidx_ref[sub * IDX_SUB:(sub + 1) * IDX_SUB, :]
        acc = jnp.zeros((Q_BLOCK, IDX_SUB), F32)
        for h in range(IDX_HEADS):
            sc = lax.dot_general(qidx_ref[:, h * IDX_DIM:(h + 1) * IDX_DIM], ki_sub, _NT,
                                 preferred_element_type=F32)
            acc = acc + jnp.maximum(sc, 0.0) * wi[:, h:h + 1]
        s_idx = j * IDX_KEY_BLOCK + sub * IDX_SUB + lane_sub
        key_scr[j * nsub + sub] = jnp.where(s_idx <= t_col, _sortable_key(acc), INT_MIN)

    @pl.when(last_tab[s] == 1)
    def _():
        n_chunks = (j + 1) * nsub

        def count(pred, ref_val):
            refb = jnp.broadcast_to(ref_val, (Q_BLOCK, LANES))

            def body(c, cnt):
                blk = key_scr[c]
                for l in range(IDX_SUB // LANES):
                    cnt = cnt + jnp.where(pred(blk[:, l * LANES:(l + 1) * LANES], refb), 1, 0)
                return cnt

            cnt = lax.fori_loop(0, n_chunks, body, jnp.zeros((Q_BLOCK, LANES), I32))
            return jnp.sum(cnt, axis=1, keepdims=True)

        def bit_body(b, carry):
            thr, n_ge = carry
            cand = thr + jnp.left_shift(jnp.int32(1), 31 - b)
            n_cand = count(lambda a, r: a >= r, cand)
            ok = n_cand >= k_top
            return jnp.where(ok, cand, thr), jnp.where(ok, n_cand, n_ge)

        thr, n_ge = lax.fori_loop(0, 32, bit_body,
                                  (jnp.full((Q_BLOCK, 1), INT_MIN, I32),
                                   jnp.broadcast_to(n_chunks * IDX_SUB, (Q_BLOCK, 1)).astype(I32)))
        thr_b = jnp.broadcast_to(thr, (Q_BLOCK, IDX_SUB))
        n_tied_rows = jnp.max(jnp.where(n_ge != k_top, 1, 0))

        def emit(c, sel):
            s_idx = c * IDX_SUB + lane_sub
            far = sel & (t_col - s_idx >= MAX_DISTANCE)
            far_ref[0, c] = jnp.where(far, 0.0, -jnp.inf).astype(far_ref.dtype)
            key_scr[c] = jnp.where(sel, 1, 0)

        @pl.when(n_tied_rows == 0)
        def _():
            def sel_body(c, carry):
                emit(c, key_scr[c] >= thr_b)
                return carry

            lax.fori_loop(0, n_chunks, sel_body, 0)

        @pl.when(n_tied_rows != 0)
        def _():
            n_gt = count(lambda a, r: a > r, thr)
            need_eq = (k_top - n_gt).astype(F32)
            incl = (lax.broadcasted_iota(I32, (IDX_SUB, IDX_SUB), 0)
                    <= lax.broadcasted_iota(I32, (IDX_SUB, IDX_SUB), 1)).astype(BF16)

            def sel_body(c, carry):
                blk = key_scr[c]
                eq = blk == thr_b
                eq_f = jnp.where(eq, 1.0, 0.0)
                rank = carry + jnp.dot(eq_f.astype(BF16), incl, preferred_element_type=F32)
                s_idx = c * IDX_SUB + lane_sub
                emit(c, ((blk > thr_b) | (eq & (rank <= need_eq))) & (s_idx <= t_col))
                return carry + jnp.sum(eq_f, axis=1, keepdims=True)

            lax.fori_loop(0, n_chunks, sel_body, jnp.zeros((Q_BLOCK, 1), F32))

        def fill_body(c, carry):
            far_ref[0, c] = jnp.full((Q_BLOCK, IDX_SUB), -jnp.inf, far_ref.dtype)
            return carry

        lax.fori_loop(n_chunks, n_sub_total, fill_body, 0)

        def window(blk_idx):
            per = IDX_SUB // Q_BLOCK
            chunk = key_scr[blk_idx // per]
            m = blk_idx % per
            out = chunk[:, 0:Q_BLOCK]
            for q in range(1, per):
                out = jnp.where(m == q, chunk[:, q * Q_BLOCK:(q + 1) * Q_BLOCK], out)
            return out

        r_i = lax.broadcasted_iota(I32, (Q_BLOCK, Q_BLOCK), 0)
        c_i = lax.broadcasted_iota(I32, (Q_BLOCK, Q_BLOCK), 1)
        d_lo = Q_BLOCK + r_i - c_i
        d_hi = r_i - c_i
        near_lo = (window(jnp.maximum(i - 1, 0)) != 0) & (d_lo < MAX_DISTANCE) & (i >= 1)
        near_hi = (window(i) != 0) & (d_hi >= 0) & (d_hi < MAX_DISTANCE)
        near_ref[:, 0:Q_BLOCK] = jnp.where(near_lo, 0.0, -jnp.inf).astype(near_ref.dtype)
        near_ref[:, Q_BLOCK:2 * Q_BLOCK] = jnp.where(near_hi, 0.0, -jnp.inf).astype(near_ref.dtype)


def _idx_select(proj, ki, wi, k_top):
    L = ki.shape[0]
    nq = L // Q_BLOCK
    n_sub_total = L // IDX_SUB
    qi_l, kj_l, last_l = [], [], []
    for i in range(nq):
        j_last = (i * Q_BLOCK + Q_BLOCK - 1) // IDX_KEY_BLOCK
        for j in range(j_last + 1):
            qi_l.append(i)
            kj_l.append(j)
            last_l.append(1 if j == j_last else 0)
    tabs = [jnp.asarray(np.asarray(t, np.int32)) for t in (qi_l, kj_l, last_l)]
    qcol = (DSA_HEADS * DSA_DH) // (IDX_HEADS * IDX_DIM)
    vmem = (L * Q_BLOCK * 4 + 2 * (Q_BLOCK * L * 2) + 2 * (Q_BLOCK * IDX_HEADS * IDX_DIM * 2)
            + 2 * IDX_KEY_BLOCK * IDX_DIM * 2 + (8 << 20))
    grid_spec = pltpu.PrefetchScalarGridSpec(
        num_scalar_prefetch=3,
        grid=(len(qi_l),),
        in_specs=[pl.BlockSpec((Q_BLOCK, IDX_HEADS * IDX_DIM), lambda s, qt, kt, lt: (qt[s], qcol)),
                  pl.BlockSpec((IDX_KEY_BLOCK, IDX_DIM), lambda s, qt, kt, lt: (kt[s], 0)),
                  pl.BlockSpec((Q_BLOCK, LANES), lambda s, qt, kt, lt: (qt[s], 0))],
        out_specs=[pl.BlockSpec((1, n_sub_total, Q_BLOCK, IDX_SUB), lambda s, qt, kt, lt: (qt[s], 0, 0, 0)),
                   pl.BlockSpec((Q_BLOCK, 2 * Q_BLOCK), lambda s, qt, kt, lt: (qt[s], 0))],
        scratch_shapes=[pltpu.VMEM((n_sub_total, Q_BLOCK, IDX_SUB), I32)],
    )
    return pl.pallas_call(
        functools.partial(_idx_kernel, k_top=k_top, n_sub_total=n_sub_total),
        grid_spec=grid_spec,
        out_shape=[jax.ShapeDtypeStruct((nq, n_sub_total, Q_BLOCK, IDX_SUB), BF16),
                   jax.ShapeDtypeStruct((L, 2 * Q_BLOCK), BF16)],
        compiler_params=_cparams(("arbitrary",), vmem),
        name="dsa_idx_select",
    )(*tabs, proj, ki, wi)


M_INIT = -1e30
FAR_SUBS = 2
LOG2E = math.log2(math.e)


def _attn_kernel(qi_tab, kj_tab, kind_tab, first_tab,
                 tab_ref, q_ref, kf_ref, vf_ref, klo_ref, khi_ref, vlo_ref, vhi_ref, far_ref, near_ref,
                 o_ref, m_scr, l_scr, acc_scr, b_scr, s_scr):
    s = pl.program_id(0)

    @pl.when(s == 0)
    def _():
        r_i = lax.broadcasted_iota(I32, (Q_BLOCK, 2 * Q_BLOCK), 0)
        c_i = lax.broadcasted_iota(I32, (Q_BLOCK, 2 * Q_BLOCK), 1)
        d = jnp.maximum(Q_BLOCK + r_i - c_i, 0)
        max_exact = N_BUCKETS // 2
        df = jnp.maximum(d, 1).astype(F32)
        large = max_exact + (jnp.log(df / max_exact) / math.log(MAX_DISTANCE / max_exact)
                             * (N_BUCKETS - max_exact)).astype(I32)
        large = jnp.minimum(large, N_BUCKETS - 1)
        bkt = jnp.where(d < max_exact, d, large)
        for h in range(DSA_HEADS):
            acc = jnp.zeros((Q_BLOCK, 2 * Q_BLOCK), F32)
            for b in range(N_BUCKETS):
                acc = jnp.where(bkt == b, (tab_ref[b, h] - tab_ref[N_BUCKETS - 1, h]) * LOG2E, acc)
            b_scr[h] = acc

    @pl.when(first_tab[s] == 1)
    def _():
        m_scr[...] = jnp.full_like(m_scr, M_INIT)
        l_scr[...] = jnp.zeros_like(l_scr)
        acc_scr[...] = jnp.zeros_like(acc_scr)

    def run_units(k_ref, v_ref, mask_of, bias_of, width, row0=0):
        nt = width // LANES
        rows = slice(row0, row0 + width)
        for h in range(DSA_HEADS):
            g = h // DSA_GROUP
            logits = lax.dot_general(q_ref[:, h * DSA_DH:(h + 1) * DSA_DH],
                                     k_ref[rows, g * DSA_DH:(g + 1) * DSA_DH], _NT, preferred_element_type=F32)
            if bias_of is not None:
                logits = logits + bias_of(h)
            s_scr[h, :, 0:width] = logits.astype(BF16)
        ones = jnp.ones((width, DSA_DH), BF16)
        for h in range(DSA_HEADS):
            g = h // DSA_GROUP
            mask = mask_of()
            tiles = [s_scr[h, :, t * LANES:(t + 1) * LANES] + mask[:, t * LANES:(t + 1) * LANES]
                     for t in range(nt)]
            tmax = tiles[0]
            for t in range(1, nt):
                tmax = jnp.maximum(tmax, tiles[t])
            m_prev = m_scr[h]
            m_new = jnp.maximum(m_prev, jnp.max(tmax.astype(F32), axis=-1, keepdims=True))
            alpha = jnp.exp2(m_prev - m_new)
            m_bf = m_new.astype(BF16)
            p = [jnp.exp2(tiles[t] - m_bf) for t in range(nt)]
            pb = jnp.concatenate(p, axis=-1) if nt > 1 else p[0]
            v_aug = jnp.concatenate([v_ref[rows, g * DSA_DH:(g + 1) * DSA_DH], ones], axis=-1)
            pv = jnp.dot(pb, v_aug, preferred_element_type=F32)
            acc_scr[h] = alpha * acc_scr[h] + pv[:, :DSA_DH]
            l_scr[h] = alpha * l_scr[h] + pv[:, DSA_DH:]
            m_scr[h] = m_new

    @pl.when(kind_tab[s] == 0)
    def _():
        for sub in range(FAR_SUBS):
            run_units(kf_ref, vf_ref, lambda sub=sub: far_ref[0, sub], None, IDX_SUB, sub * IDX_SUB)

    @pl.when(kind_tab[s] == 1)
    def _():
        for half, (k_ref, v_ref) in enumerate(((klo_ref, vlo_ref), (khi_ref, vhi_ref))):
            cols = slice(half * Q_BLOCK, (half + 1) * Q_BLOCK)
            run_units(k_ref, v_ref, lambda cols=cols: near_ref[:, cols],
                      lambda h, cols=cols: b_scr[h][:, cols], Q_BLOCK)
        for h in range(DSA_HEADS):
            o_ref[:, h * DSA_DH:(h + 1) * DSA_DH] = (acc_scr[h] / l_scr[h]).astype(o_ref.dtype)


def _masked_attention(proj, far, near, rel_bias):
    L = proj.shape[0]
    nq = L // Q_BLOCK
    far_keys = FAR_SUBS * IDX_SUB
    per = far_keys // Q_BLOCK
    qi_l, kj_l, kind_l, first_l = [], [], [], []
    for i in range(nq):
        n_far = -(-i // per)
        for j in range(n_far):
            qi_l.append(i); kj_l.append(j); kind_l.append(0); first_l.append(1 if j == 0 else 0)
        qi_l.append(i); kj_l.append(max(n_far - 1, 0)); kind_l.append(1); first_l.append(1 if n_far == 0 else 0)
    tabs = [jnp.asarray(np.asarray(t, np.int32)) for t in (qi_l, kj_l, kind_l, first_l)]
    qw = DSA_HEADS * DSA_DH
    kvw = DSA_KV_HEADS * DSA_DH
    k_col = (2 * qw) // kvw
    v_col = k_col + 1
    hw = DSA_HEADS
    vmem = (2 * (Q_BLOCK * qw * 2 * 2 + 2 * far_keys * kvw * 2 + 4 * Q_BLOCK * kvw * 2
                 + Q_BLOCK * far_keys * 2 + Q_BLOCK * 2 * Q_BLOCK * 2)
            + hw * Q_BLOCK * (3 * LANES + 2 * Q_BLOCK) * 4 + (16 << 20))
    idx = lambda f: (lambda s, qt, kt, kd, ft: f(qt[s], kt[s]))
    grid_spec = pltpu.PrefetchScalarGridSpec(
        num_scalar_prefetch=4,
        grid=(len(qi_l),),
        in_specs=[pl.BlockSpec(memory_space=pltpu.SMEM),
                  pl.BlockSpec((Q_BLOCK, qw), idx(lambda i, j: (i, 0))),
                  pl.BlockSpec((far_keys, kvw), idx(lambda i, j: (j, k_col))),
                  pl.BlockSpec((far_keys, kvw), idx(lambda i, j: (j, v_col))),
                  pl.BlockSpec((Q_BLOCK, kvw), idx(lambda i, j: (jnp.maximum(i - 1, 0), k_col))),
                  pl.BlockSpec((Q_BLOCK, kvw), idx(lambda i, j: (i, k_col))),
                  pl.BlockSpec((Q_BLOCK, kvw), idx(lambda i, j: (jnp.maximum(i - 1, 0), v_col))),
                  pl.BlockSpec((Q_BLOCK, kvw), idx(lambda i, j: (i, v_col))),
                  pl.BlockSpec((1, FAR_SUBS, Q_BLOCK, IDX_SUB), idx(lambda i, j: (i, j, 0, 0))),
                  pl.BlockSpec((Q_BLOCK, 2 * Q_BLOCK), idx(lambda i, j: (i, 0)))],
        out_specs=pl.BlockSpec((Q_BLOCK, qw), idx(lambda i, j: (i, 0))),
        scratch_shapes=[pltpu.VMEM((hw, Q_BLOCK, LANES), F32),
                        pltpu.VMEM((hw, Q_BLOCK, LANES), F32),
                        pltpu.VMEM((hw, Q_BLOCK, DSA_DH), F32),
                        pltpu.VMEM((hw, Q_BLOCK, 2 * Q_BLOCK), F32),
                        pltpu.VMEM((hw, Q_BLOCK, IDX_SUB), BF16)],
    )
    return pl.pallas_call(
        _attn_kernel,
        grid_spec=grid_spec,
        out_shape=jax.ShapeDtypeStruct((L, qw), BF16),
        compiler_params=_cparams(("arbitrary",), vmem),
        name="dsa_attention",
    )(*tabs, rel_bias, proj, proj, proj, proj, proj, proj, proj, far, near)


def _dsa_mixer(xb, w_in, ln_g, ln_b, rel_bias):
    L = xb.shape[0]
    k_top = min(TOPK_MAX, L // 4)
    sq = DSA_HEADS * DSA_DH
    skv = DSA_KV_HEADS * DSA_DH
    si = IDX_HEADS * IDX_DIM
    w_q = w_in[:, :sq]
    w_k = w_in[:, sq:sq + skv]
    w_v = w_in[:, sq + skv:sq + 2 * skv]
    w_qi = w_in[:, sq + 2 * skv:sq + 2 * skv + si]
    w_ki = w_in[:, sq + 2 * skv + si:sq + 2 * skv + si + IDX_DIM]
    w_wi = w_in[:, sq + 2 * skv + si + IDX_DIM:]
    w_main = jnp.concatenate([w_q, w_qi, w_k, w_v], axis=1).astype(BF16)
    colscale = jnp.concatenate([jnp.full((1, sq), DSA_DH ** -0.5 * LOG2E, F32),
                                jnp.ones((1, si + 2 * skv), F32)], axis=1)
    w_small = jnp.concatenate([w_ki, w_wi, jnp.zeros((D_MODEL, LANES - IDX_HEADS), F32)], axis=1).astype(BF16)
    proj = _matmul_scaled(xb, w_main, colscale, BF16, tm=1024, tn=1024)
    ki, wi = _dsa_small(xb, w_small, ln_g, ln_b, tm=1024)
    far, near = _idx_select(proj, ki, wi, k_top)
    return _masked_attention(proj, far, near, rel_bias)


def kernel(x, p, gdn_w_in, gdn_conv_w, gdn_a_log, gdn_dt_bias, gdn_norm_g, gdn_w_o, dsa_w_in, dsa_kidx_ln_g, dsa_kidx_ln_b, dsa_w_o, rel_bias, ln1_g, ln1_b, ffn_w_gate, ffn_w_up, ffn_conv_w, ffn_w_down, ln2_g, ln2_b, ple_w_proj, ple_w_gate):
    assert x.shape[0] == 1 and x.shape[2] == D_MODEL
    xf = x[0]
    xb = xf.astype(BF16)
    ia = ib = 0
    for i in range(DEPTH):
        if i % 2 == 0:
            mix = _gdn_mixer(xb, gdn_w_in[ia], gdn_conv_w[ia], gdn_a_log[ia], gdn_dt_bias[ia], gdn_norm_g[ia])
            w_o = gdn_w_o[ia]
            ia += 1
        else:
            mix = _dsa_mixer(xb, dsa_w_in[ib], dsa_kidx_ln_g[ib], dsa_kidx_ln_b[ib], rel_bias)
            w_o = dsa_w_o[ib]
            ib += 1
        xf, xb = _proj_res_ln(mix, w_o.astype(BF16), xf, ln1_g[i], ln1_b[i], tm=512, sub=256)
        hmid = _ffn_up(xb, ffn_w_gate[i].astype(BF16), ffn_w_up[i].astype(BF16), ffn_conv_w[i], tm=1024, tn=512)
        xf, xb = _proj_res_ln(hmid, ffn_w_down[i].astype(BF16), xf, ln2_g[i], ln2_b[i], tm=512, sub=256)
        xf, xb = _ple(xb, xf, ple_w_gate[i].astype(BF16), p[i, 0], ple_w_proj[i].astype(BF16), tm=1024, tn=1024)
    return xf[None]
```

```python
import functools
import math

import jax
import jax.numpy as jnp
import numpy as np
from jax import lax
from jax.experimental import pallas as pl
from jax.experimental.pallas import tpu as pltpu

F32 = jnp.float32
BF16 = jnp.bfloat16
I32 = jnp.int32

D_MODEL = 2048
GDN_QK_HEADS = 16
GDN_V_HEADS = 32
GDN_DK = 128
GDN_DV = 128
GDN_CONV = 4
GDN_CHUNK = 64
GDN_QK_W = GDN_QK_HEADS * GDN_DK
GDN_V_W = GDN_V_HEADS * GDN_DV
DSA_HEADS = 16
DSA_KV_HEADS = 4
DSA_GROUP = DSA_HEADS // DSA_KV_HEADS
DSA_DH = 128
IDX_HEADS = 16
IDX_DIM = 128
TOPK_MAX = 256
N_BUCKETS = 32
MAX_DISTANCE = 128
D_FF = 5120
FFN_CONV = 3
PLE_DIM = 256
DEPTH = 2
DN_ALPHA = (2.0 * DEPTH) ** 0.25
LN_EPS = 1e-5
RMS_EPS = 1e-6

V7X_VMEM_BYTES = 64 * 1024 * 1024
V7X_VMEM_BUDGET = 56 * 1024 * 1024
LANES = 128
BF16_SUBLANES = 16

HALO = BF16_SUBLANES
Q_BLOCK = 128
IDX_KEY_BLOCK = 2048
IDX_SUB = 512
INT_MIN = -(2 ** 31)

_NT = (((1,), (1,)), ((), ()))
_TN = (((0,), (0,)), ((), ()))


def _cparams(semantics, vmem_bytes):
    return pltpu.CompilerParams(dimension_semantics=semantics,
                                vmem_limit_bytes=int(min(V7X_VMEM_BUDGET, vmem_bytes)))


def _silu(y):
    return y * jax.nn.sigmoid(y)


def _mm_scale_kernel(x_ref, w_ref, cs_ref, o_ref):
    acc = jnp.dot(x_ref[...], w_ref[...], preferred_element_type=F32)
    o_ref[...] = (acc * cs_ref[...]).astype(o_ref.dtype)


def _matmul_scaled(x, w, colscale, out_dtype, tm, tn):
    M, K = x.shape
    N = w.shape[1]
    tm, tn = min(tm, M), min(tn, N)
    osz = jnp.dtype(out_dtype).itemsize
    vmem = 2 * (tm * K * 2 + K * tn * 2 + tm * tn * osz) + 2 * tm * tn * 4
    return pl.pallas_call(
        _mm_scale_kernel,
        grid=(M // tm, N // tn),
        in_specs=[pl.BlockSpec((tm, K), lambda i, j: (i, 0)),
                  pl.BlockSpec((K, tn), lambda i, j: (0, j)),
                  pl.BlockSpec((1, tn), lambda i, j: (0, j))],
        out_specs=pl.BlockSpec((tm, tn), lambda i, j: (i, j)),
        out_shape=jax.ShapeDtypeStruct((M, N), out_dtype),
        compiler_params=_cparams(("parallel", "parallel"), vmem),
        name="matmul_scaled",
    )(x, w, colscale)


CONV_SUB = 256
FFN_SUB = 512


CONV_ROWS = 1024


def _causal_conv(g, gh, cw_ref, g_scr, kc, tm, cols, r0=0):
    if gh is not None:
        g_scr[0:HALO, cols] = gh
    g_scr[HALO + r0:HALO + r0 + tm, cols] = g
    y = cw_ref[kc - 1:kc, cols] * g
    for j in range(kc - 1):
        off = HALO + r0 - (kc - 1) + j
        y = y + cw_ref[j:j + 1, cols] * g_scr[off:off + tm, cols]
    return y


def _mm_conv_silu_kernel(x_ref, xh_ref, w_ref, cw_ref, o_ref, g_scr, *, kc, tm):
    first = pl.program_id(0) == 0
    rs = min(CONV_ROWS, tm)
    for c0 in range(0, o_ref.shape[1], CONV_SUB):
        cols = slice(c0, c0 + CONV_SUB)
        w = w_ref[:, cols]
        gh = jnp.dot(xh_ref[...], w, preferred_element_type=F32)
        gh = jnp.where(first, 0.0, gh)
        for r0 in range(0, tm, rs):
            g = jnp.dot(x_ref[r0:r0 + rs, :], w, preferred_element_type=F32)
            y = _causal_conv(g, gh if r0 == 0 else None, cw_ref, g_scr, kc, rs, cols, r0)
            o_ref[r0:r0 + rs, cols] = _silu(y).astype(o_ref.dtype)


def _proj_conv_silu(x, w, conv_w, tm, tn):
    M, K = x.shape
    N = w.shape[1]
    kc = conv_w.shape[0]
    tm, tn = min(tm, M), min(tn, N)
    hb = tm // HALO
    vmem = 2 * (tm * K * 2 + HALO * K * 2 + K * tn * 2 + tm * tn * 2) + 4 * tm * tn * 4
    return pl.pallas_call(
        functools.partial(_mm_conv_silu_kernel, kc=kc, tm=tm),
        grid=(M // tm, N // tn),
        in_specs=[pl.BlockSpec((tm, K), lambda i, j: (i, 0)),
                  pl.BlockSpec((HALO, K), lambda i, j: (jnp.maximum(i * hb - 1, 0), 0)),
                  pl.BlockSpec((K, tn), lambda i, j: (0, j)),
                  pl.BlockSpec((kc, tn), lambda i, j: (0, j))],
        out_specs=pl.BlockSpec((tm, tn), lambda i, j: (i, j)),
        out_shape=jax.ShapeDtypeStruct((M, N), BF16),
        scratch_shapes=[pltpu.VMEM((tm + HALO, tn), F32)],
        compiler_params=_cparams(("parallel", "parallel"), vmem),
        name="proj_conv_silu",
    )(x, x, w, conv_w)


def _ffn_up_kernel(x_ref, xh_ref, wg_ref, wu_ref, cw_ref, o_ref, g_scr, *, kc, tm):
    first = pl.program_id(0) == 0
    for c0 in range(0, o_ref.shape[1], FFN_SUB):
        cols = slice(c0, c0 + FFN_SUB)
        wg = wg_ref[:, cols]
        g = jnp.dot(x_ref[...], wg, preferred_element_type=F32)
        gh = jnp.dot(xh_ref[...], wg, preferred_element_type=F32)
        gh = jnp.where(first, 0.0, gh)
        u = jnp.dot(x_ref[...], wu_ref[:, cols], preferred_element_type=F32)
        y = _causal_conv(g, gh, cw_ref, g_scr, kc, tm, cols)
        o_ref[:, cols] = (_silu(y) * u).astype(o_ref.dtype)


def _ffn_up(x, w_gate, w_up, conv_w, tm, tn):
    M, K = x.shape
    N = w_gate.shape[1]
    kc = conv_w.shape[0]
    tm, tn = min(tm, M), min(tn, N)
    hb = tm // HALO
    vmem = 2 * (tm * K * 2 + HALO * K * 2 + 2 * K * tn * 2 + tm * tn * 2) + 6 * tm * tn * 4
    return pl.pallas_call(
        functools.partial(_ffn_up_kernel, kc=kc, tm=tm),
        grid=(M // tm, N // tn),
        in_specs=[pl.BlockSpec((tm, K), lambda i, j: (i, 0)),
                  pl.BlockSpec((HALO, K), lambda i, j: (jnp.maximum(i * hb - 1, 0), 0)),
                  pl.BlockSpec((K, tn), lambda i, j: (0, j)),
                  pl.BlockSpec((K, tn), lambda i, j: (0, j)),
                  pl.BlockSpec((kc, tn), lambda i, j: (0, j))],
        out_specs=pl.BlockSpec((tm, tn), lambda i, j: (i, j)),
        out_shape=jax.ShapeDtypeStruct((M, N), BF16),
        scratch_shapes=[pltpu.VMEM((tm + HALO, tn), F32)],
        compiler_params=_cparams(("parallel", "parallel"), vmem),
        name="ffn_up",
    )(x, x, w_gate, w_up, conv_w)


def _mm_res_ln_kernel(a_ref, w_ref, res_ref, g_ref, b_ref, of_ref, ob_ref, *, sub):
    for r0 in range(0, a_ref.shape[0], sub):
        rows = slice(r0, r0 + sub)
        acc = jnp.dot(a_ref[rows, :], w_ref[...], preferred_element_type=F32)
        y = DN_ALPHA * res_ref[rows, :] + acc
        mu = jnp.mean(y, axis=-1, keepdims=True)
        yc = y - mu
        var = jnp.mean(yc * yc, axis=-1, keepdims=True)
        out = yc * lax.rsqrt(var + LN_EPS) * g_ref[...] + b_ref[...]
        of_ref[rows, :] = out
        ob_ref[rows, :] = out.astype(BF16)


def _proj_res_ln(a, w, res, g, b, tm, sub):
    M, K = a.shape
    N = w.shape[1]
    tm = min(tm, M)
    sub = min(sub, tm)
    vmem = K * N * 2 + 2 * (tm * K * 2 + tm * N * 4 + tm * N * 4 + tm * N * 2) + 4 * sub * N * 4
    return pl.pallas_call(
        functools.partial(_mm_res_ln_kernel, sub=sub),
        grid=(M // tm,),
        in_specs=[pl.BlockSpec((tm, K), lambda i: (i, 0)),
                  pl.BlockSpec((K, N), lambda i: (0, 0), pipeline_mode=pl.Buffered(1)),
                  pl.BlockSpec((tm, N), lambda i: (i, 0)),
                  pl.BlockSpec((1, N), lambda i: (0, 0)),
                  pl.BlockSpec((1, N), lambda i: (0, 0))],
        out_specs=[pl.BlockSpec((tm, N), lambda i: (i, 0)),
                   pl.BlockSpec((tm, N), lambda i: (i, 0))],
        out_shape=[jax.ShapeDtypeStruct((M, N), F32), jax.ShapeDtypeStruct((M, N), BF16)],
        compiler_params=_cparams(("parallel",), vmem),
        name="proj_res_ln",
    )(a, w, res, g.reshape(1, N), b.reshape(1, N))


def _ple_kernel(xb_ref, wg_ref, p_ref, wp_ref, xr_ref, of_ref, ob_ref):
    gate = jax.nn.sigmoid(jnp.dot(xb_ref[...], wg_ref[...], preferred_element_type=F32))
    pe = jnp.dot(p_ref[...].astype(BF16), wp_ref[...], preferred_element_type=F32)
    out = xr_ref[...] + gate * pe
    of_ref[...] = out
    ob_ref[...] = out.astype(BF16)


def _ple(xb, xf, w_gate, p, w_proj, tm, tn):
    M, K = xb.shape
    N = w_gate.shape[1]
    P = p.shape[1]
    tm, tn = min(tm, M), min(tn, N)
    vmem = 2 * (tm * K * 2 + K * tn * 2 + tm * P * 4 + P * tn * 2 + tm * tn * 10) + 4 * tm * tn * 4
    return pl.pallas_call(
        _ple_kernel,
        grid=(M // tm, N // tn),
        in_specs=[pl.BlockSpec((tm, K), lambda i, j: (i, 0)),
                  pl.BlockSpec((K, tn), lambda i, j: (0, j)),
                  pl.BlockSpec((tm, P), lambda i, j: (i, 0)),
                  pl.BlockSpec((P, tn), lambda i, j: (0, j)),
                  pl.BlockSpec((tm, tn), lambda i, j: (i, j))],
        out_specs=[pl.BlockSpec((tm, tn), lambda i, j: (i, j)),
                   pl.BlockSpec((tm, tn), lambda i, j: (i, j))],
        out_shape=[jax.ShapeDtypeStruct((M, N), F32), jax.ShapeDtypeStruct((M, N), BF16)],
        compiler_params=_cparams(("parallel", "parallel"), vmem),
        name="ple",
    )(xb, w_gate, p, w_proj, xf)


GDN_HB = 4
GDN_NC = 4


def _gdn_kernel(q_ref, k_ref, v_ref, z_ref, sc_ref, hp_ref, ng_ref, o_ref, s_scr):
    C = GDN_CHUNK

    @pl.when(pl.program_id(1) == 0)
    def _():
        s_scr[...] = jnp.zeros_like(s_scr)

    row = lax.broadcasted_iota(I32, (C, C), 0)
    col = lax.broadcasted_iota(I32, (C, C), 1)
    tri = row >= col
    strict = row > col
    eye = row == col
    tri_f = tri.astype(F32)
    eye_f = eye.astype(F32)

    raw = sc_ref[...]
    a_log = hp_ref[0, 0:1, :]
    dt_b = hp_ref[0, 1:2, :]
    xs = raw + dt_b
    softplus = jnp.maximum(xs, 0.0) + jnp.log1p(jnp.exp(-jnp.abs(xs)))
    g_all = -jnp.exp(a_log) * softplus
    beta_all = jax.nn.sigmoid(raw)
    ng = ng_ref[...]

    units = [(c, j) for c in range(GDN_NC) for j in range(GDN_HB)]
    kb_l, rhs_l, decay_l, qd_l, kd_l, kbf_l, qbf_l, gl_l = [], [], [], [], [], [], [], []
    for c in range(GDN_NC):
        r0 = c * C
        gc = jnp.dot(tri_f, g_all[r0:r0 + C, :], precision=lax.Precision.HIGHEST,
                     preferred_element_type=F32)
        g_last = gc[C - 1:C, :]
        e_gc = jnp.exp(gc)
        e_rest = jnp.exp(g_last - gc)
        e_last = jnp.exp(g_last)
        beta_c = beta_all[r0:r0 + C, :]
        qn, kn = [], []
        for hq in range(GDN_HB // 2):
            qf = q_ref[r0:r0 + C, hq * GDN_DK:(hq + 1) * GDN_DK].astype(F32)
            kf = k_ref[r0:r0 + C, hq * GDN_DK:(hq + 1) * GDN_DK].astype(F32)
            qn.append(qf * lax.rsqrt(jnp.sum(qf * qf, axis=-1, keepdims=True) + RMS_EPS) * (GDN_DK ** -0.5))
            kn.append(kf * lax.rsqrt(jnp.sum(kf * kf, axis=-1, keepdims=True) + RMS_EPS))
        for j in range(GDN_HB):
            q_h, k_h = qn[j // 2], kn[j // 2]
            vf = v_ref[r0:r0 + C, j * GDN_DV:(j + 1) * GDN_DV].astype(F32)
            beta = beta_c[:, GDN_HB + j:GDN_HB + j + 1]
            kb = k_h * beta
            gcb = jnp.broadcast_to(gc[:, j:j + 1], (C, C))
            gcr = jnp.sum(jnp.where(eye, gcb, 0.0), axis=0, keepdims=True)
            decay_l.append(jnp.where(tri, jnp.exp(jnp.where(tri, gcb - gcr, 0.0)), 0.0))
            kb_l.append(kb.astype(BF16))
            rhs_l.append(jnp.concatenate([vf * beta, kb * e_gc[:, j:j + 1]], axis=-1).astype(BF16))
            qd_l.append((q_h * e_gc[:, j:j + 1]).astype(BF16))
            kd_l.append((k_h * e_rest[:, j:j + 1]).astype(BF16))
            kbf_l.append(k_h.astype(BF16))
            qbf_l.append(q_h.astype(BF16))
            gl_l.append(e_last[:, j:j + 1])

    n_u = len(units)
    kk_l = [lax.dot_general(kb_l[u], kbf_l[u], _NT, preferred_element_type=F32) for u in range(n_u)]
    qk_l = [lax.dot_general(qbf_l[u], kbf_l[u], _NT, preferred_element_type=F32) for u in range(n_u)]
    qk_l = [jnp.where(tri, qk_l[u] * decay_l[u], 0.0).astype(BF16) for u in range(n_u)]
    x_l = [(-jnp.where(strict, kk_l[u] * decay_l[u], 0.0)) for u in range(n_u)]
    t_l = [eye_f + x_l[u] for u in range(n_u)]
    x_l = [x.astype(BF16) for x in x_l]
    for _ in range(5):
        x_l = [jnp.dot(x, x, preferred_element_type=F32).astype(BF16) for x in x_l]
        t_l = [t + jnp.dot(t.astype(BF16), x, preferred_element_type=F32) for t, x in zip(t_l, x_l)]
    sol_l = [jnp.dot(t_l[u].astype(BF16), rhs_l[u], preferred_element_type=F32) for u in range(n_u)]

    s_cur = [s_scr[j] for j in range(GDN_HB)]
    for c in range(GDN_NC):
        r0 = c * C
        us = [c * GDN_HB + j for j in range(GDN_HB)]
        s_bf = [s.astype(BF16) for s in s_cur]
        ws_l = [jnp.dot(sol_l[u][:, GDN_DV:].astype(BF16), s_bf[j], preferred_element_type=F32)
                for j, u in enumerate(us)]
        qs_l = [jnp.dot(qd_l[u], s_bf[j], preferred_element_type=F32) for j, u in enumerate(us)]
        vn_l = [(sol_l[u][:, :GDN_DV] - ws_l[j]).astype(BF16) for j, u in enumerate(us)]
        kv_l = [lax.dot_general(kd_l[u], vn_l[j], _TN, preferred_element_type=F32) for j, u in enumerate(us)]
        ov_l = [jnp.dot(qk_l[u], vn_l[j], preferred_element_type=F32) for j, u in enumerate(us)]
        s_cur = [s_cur[j] * gl_l[u] + kv_l[j] for j, u in enumerate(us)]
        for j in range(GDN_HB):
            o = qs_l[j] + ov_l[j]
            zf = z_ref[r0:r0 + C, j * GDN_DV:(j + 1) * GDN_DV].astype(F32)
            o = o * lax.rsqrt(jnp.mean(o * o, axis=-1, keepdims=True) + RMS_EPS) * ng * _silu(zf)
            o_ref[r0:r0 + C, j * GDN_DV:(j + 1) * GDN_DV] = o.astype(o_ref.dtype)
    for j in range(GDN_HB):
        s_scr[j] = s_cur[j]


def _gdn_core(qkv, z, scal, hparams, norm_g):
    L = qkv.shape[0]
    G = GDN_V_HEADS // GDN_HB
    R = GDN_NC * GDN_CHUNK
    qw = (GDN_HB // 2) * GDN_DK
    vw = GDN_HB * GDN_DV
    k_blk0 = GDN_QK_W // qw
    v_blk0 = 2 * GDN_QK_W // vw
    vmem = 2 * (2 * R * qw * 2 + 2 * R * vw * 2 + R * LANES * 4 + R * vw * 2) + (16 << 20)
    return pl.pallas_call(
        _gdn_kernel,
        grid=(G, L // R),
        in_specs=[pl.BlockSpec((R, qw), lambda g, s: (s, g)),
                  pl.BlockSpec((R, qw), lambda g, s: (s, k_blk0 + g)),
                  pl.BlockSpec((R, vw), lambda g, s: (s, v_blk0 + g)),
                  pl.BlockSpec((R, vw), lambda g, s: (s, g)),
                  pl.BlockSpec((R, LANES), lambda g, s: (s, g)),
                  pl.BlockSpec((1, 8, LANES), lambda g, s: (g, 0, 0)),
                  pl.BlockSpec((1, GDN_DV), lambda g, s: (0, 0))],
        out_specs=pl.BlockSpec((R, vw), lambda g, s: (s, g)),
        out_shape=jax.ShapeDtypeStruct((L, GDN_V_W), BF16),
        scratch_shapes=[pltpu.VMEM((GDN_HB, GDN_DK, GDN_DV), F32)],
        compiler_params=_cparams(("parallel", "arbitrary"), vmem),
        name="gdn_core",
    )(qkv, qkv, qkv, z, scal, hparams, norm_g.reshape(1, GDN_DV))


def _gdn_mixer(xb, w_in, conv_w, a_log, dt_bias, norm_g):
    L = xb.shape[0]
    nqkv = 2 * GDN_QK_W + GDN_V_W
    w_qkv = w_in[:, :nqkv].astype(BF16)
    w_z = w_in[:, nqkv:nqkv + GDN_V_W].astype(BF16)
    w_ab = w_in[:, nqkv + GDN_V_W:]
    G = GDN_V_HEADS // GDN_HB
    w_a = w_ab[:, :GDN_V_HEADS].reshape(D_MODEL, G, GDN_HB)
    w_b = w_ab[:, GDN_V_HEADS:].reshape(D_MODEL, G, GDN_HB)
    w_sc = jnp.concatenate([w_a, w_b, jnp.zeros((D_MODEL, G, LANES - 2 * GDN_HB), F32)], axis=-1)
    w_sc = w_sc.reshape(D_MODEL, G * LANES).astype(BF16)
    qkv = _proj_conv_silu(xb, w_qkv, conv_w, tm=1024, tn=1024)
    ones_z = jnp.ones((1, GDN_V_W), F32)
    z = _matmul_scaled(xb, w_z, ones_z, BF16, tm=1024, tn=1024)
    scal = _matmul_scaled(xb, w_sc, jnp.ones((1, G * LANES), F32), F32, tm=1024, tn=G * LANES)
    hp = jnp.zeros((G, 8, LANES), F32)
    hp = hp.at[:, 0, :GDN_HB].set(a_log.reshape(G, GDN_HB))
    hp = hp.at[:, 1, :GDN_HB].set(dt_bias.reshape(G, GDN_HB))
    return _gdn_core(qkv, z, scal, hp, norm_g)


def _dsa_small_kernel(x_ref, w_ref, g_ref, b_ref, ki_ref, wi_ref):
    acc = jnp.dot(x_ref[...], w_ref[...], preferred_element_type=F32)
    ki = acc[:, :IDX_DIM]
    mu = jnp.mean(ki, axis=-1, keepdims=True)
    kc = ki - mu
    var = jnp.mean(kc * kc, axis=-1, keepdims=True)
    ki_ref[...] = (kc * lax.rsqrt(var + LN_EPS) * g_ref[...] + b_ref[...]).astype(ki_ref.dtype)
    wi_ref[...] = acc[:, IDX_DIM:] * ((IDX_HEADS ** -0.5) * (IDX_DIM ** -0.5))


def _dsa_small(xb, w_small, ln_g, ln_b, tm):
    M, K = xb.shape
    tm = min(tm, M)
    N = 2 * LANES
    vmem = 2 * (tm * K * 2 + K * N * 2 + tm * LANES * 6) + 4 * tm * N * 4
    return pl.pallas_call(
        _dsa_small_kernel,
        grid=(M // tm,),
        in_specs=[pl.BlockSpec((tm, K), lambda i: (i, 0)),
                  pl.BlockSpec((K, N), lambda i: (0, 0)),
                  pl.BlockSpec((1, IDX_DIM), lambda i: (0, 0)),
                  pl.BlockSpec((1, IDX_DIM), lambda i: (0, 0))],
        out_specs=[pl.BlockSpec((tm, IDX_DIM), lambda i: (i, 0)),
                   pl.BlockSpec((tm, LANES), lambda i: (i, 0))],
        out_shape=[jax.ShapeDtypeStruct((M, IDX_DIM), BF16), jax.ShapeDtypeStruct((M, LANES), F32)],
        compiler_params=_cparams(("parallel",), vmem),
        name="dsa_idx_proj",
    )(xb, w_small, ln_g.reshape(1, IDX_DIM), ln_b.reshape(1, IDX_DIM))


def _sortable_key(score):
    bits = lax.bitcast_convert_type(score, I32)
    return jnp.where(bits >= 0, bits, bits ^ jnp.int32(0x7FFFFFFF))


def _idx_kernel(qi_tab, kj_tab, last_tab,
                qidx_ref, kidx_ref, wi_ref, far_ref, near_ref, key_scr, *, k_top, n_sub_total):
    s = pl.program_id(0)
    i = qi_tab[s]
    j = kj_tab[s]
    nsub = IDX_KEY_BLOCK // IDX_SUB
    t_col = i * Q_BLOCK + lax.broadcasted_iota(I32, (Q_BLOCK, 1), 0)
    lane_sub = lax.broadcasted_iota(I32, (Q_BLOCK, IDX_SUB), 1)
    wi = wi_ref[...]

    for sub in range(nsub):
        ki_sub = kidx_ref[sub * IDX_SUB:(sub + 1) * IDX_SUB, :]
        acc = jnp.zeros((Q_BLOCK, IDX_SUB), F32)
        for h in range(IDX_HEADS):
            sc = lax.dot_general(qidx_ref[:, h * IDX_DIM:(h + 1) * IDX_DIM], ki_sub, _NT,
                                 preferred_element_type=F32)
            acc = acc + jnp.maximum(sc, 0.0) * wi[:, h:h + 1]
        s_idx = j * IDX_KEY_BLOCK + sub * IDX_SUB + lane_sub
        key_scr[j * nsub + sub] = jnp.where(s_idx <= t_col, _sortable_key(acc), INT_MIN)

    @pl.when(last_tab[s] == 1)
    def _():
        n_chunks = (j + 1) * nsub

        def count(pred, ref_val):
            refb = jnp.broadcast_to(ref_val, (Q_BLOCK, LANES))

            def body(c, cnt):
                blk = key_scr[c]
                for l in range(IDX_SUB // LANES):
                    cnt = cnt + jnp.where(pred(blk[:, l * LANES:(l + 1) * LANES], refb), 1, 0)
                return cnt

            cnt = lax.fori_loop(0, n_chunks, body, jnp.zeros((Q_BLOCK, LANES), I32))
            return jnp.sum(cnt, axis=1, keepdims=True)

        def bit_body(b, carry):
            thr, n_ge = carry
            cand = thr + jnp.left_shift(jnp.int32(1), 31 - b)
            n_cand = count(lambda a, r: a >= r, cand)
            ok = n_cand >= k_top
            return jnp.where(ok, cand, thr), jnp.where(ok, n_cand, n_ge)

        thr, n_ge = lax.fori_loop(0, 32, bit_body,
                                  (jnp.full((Q_BLOCK, 1), INT_MIN, I32),
                                   jnp.broadcast_to(n_chunks * IDX_SUB, (Q_BLOCK, 1)).astype(I32)))
        thr_b = jnp.broadcast_to(thr, (Q_BLOCK, IDX_SUB))
        n_tied_rows = jnp.max(jnp.where(n_ge != k_top, 1, 0))

        def emit(c, sel):
            s_idx = c * IDX_SUB + lane_sub
            far = sel & (t_col - s_idx >= MAX_DISTANCE)
            far_ref[0, c] = jnp.where(far, 0.0, -jnp.inf).astype(far_ref.dtype)
            key_scr[c] = jnp.where(sel, 1, 0)

        @pl.when(n_tied_rows == 0)
        def _():
            def sel_body(c, carry):
                emit(c, key_scr[c] >= thr_b)
                return carry

            lax.fori_loop(0, n_chunks, sel_body, 0)

        @pl.when(n_tied_rows != 0)
        def _():
            n_gt = count(lambda a, r: a > r, thr)
            need_eq = (k_top - n_gt).astype(F32)
            incl = (lax.broadcasted_iota(I32, (IDX_SUB, IDX_SUB), 0)
                    <= lax.broadcasted_iota(I32, (IDX_SUB, IDX_SUB), 1)).astype(BF16)

            def sel_body(c, carry):
                blk = key_scr[c]
                eq = blk == thr_b
                eq_f = jnp.where(eq, 1.0, 0.0)
                rank = carry + jnp.dot(eq_f.astype(BF16), incl, preferred_element_type=F32)
                s_idx = c * IDX_SUB + lane_sub
                emit(c, ((blk > thr_b) | (eq & (rank <= need_eq))) & (s_idx <= t_col))
                return carry + jnp.sum(eq_f, axis=1, keepdims=True)

            lax.fori_loop(0, n_chunks, sel_body, jnp.zeros((Q_BLOCK, 1), F32))

        def fill_body(c, carry):
            far_ref[0, c] = jnp.full((Q_BLOCK, IDX_SUB), -jnp.inf, far_ref.dtype)
            return carry

        lax.fori_loop(n_chunks, n_sub_total, fill_body, 0)

        def window(blk_idx):
            per = IDX_SUB // Q_BLOCK
            chunk = key_scr[blk_idx // per]
            m = blk_idx % per
            out = chunk[:, 0:Q_BLOCK]
            for q in range(1, per):
                out = jnp.where(m == q, chunk[:, q * Q_BLOCK:(q + 1) * Q_BLOCK], out)
            return out

        r_i = lax.broadcasted_iota(I32, (Q_BLOCK, Q_BLOCK), 0)
        c_i = lax.broadcasted_iota(I32, (Q_BLOCK, Q_BLOCK), 1)
        d_lo = Q_BLOCK + r_i - c_i
        d_hi = r_i - c_i
        near_lo = (window(jnp.maximum(i - 1, 0)) != 0) & (d_lo < MAX_DISTANCE) & (i >= 1)
        near_hi = (window(i) != 0) & (d_hi >= 0) & (d_hi < MAX_DISTANCE)
        near_ref[:, 0:Q_BLOCK] = jnp.where(near_lo, 0.0, -jnp.inf).astype(near_ref.dtype)
        near_ref[:, Q_BLOCK:2 * Q_BLOCK] = jnp.where(near_hi, 0.0, -jnp.inf).astype(near_ref.dtype)


def _idx_select(proj, ki, wi, k_top):
    L = ki.shape[0]
    nq = L // Q_BLOCK
    n_sub_total = L // IDX_SUB
    qi_l, kj_l, last_l = [], [], []
    for i in range(nq):
        j_last = (i * Q_BLOCK + Q_BLOCK - 1) // IDX_KEY_BLOCK
        for j in range(j_last + 1):
            qi_l.append(i)
            kj_l.append(j)
            last_l.append(1 if j == j_last else 0)
    tabs = [jnp.asarray(np.asarray(t, np.int32)) for t in (qi_l, kj_l, last_l)]
    qcol = (DSA_HEADS * DSA_DH) // (IDX_HEADS * IDX_DIM)
    vmem = (L * Q_BLOCK * 4 + 2 * (Q_BLOCK * L * 2) + 2 * (Q_BLOCK * IDX_HEADS * IDX_DIM * 2)
            + 2 * IDX_KEY_BLOCK * IDX_DIM * 2 + (8 << 20))
    grid_spec = pltpu.PrefetchScalarGridSpec(
        num_scalar_prefetch=3,
        grid=(len(qi_l),),
        in_specs=[pl.BlockSpec((Q_BLOCK, IDX_HEADS * IDX_DIM), lambda s, qt, kt, lt: (qt[s], qcol)),
                  pl.BlockSpec((IDX_KEY_BLOCK, IDX_DIM), lambda s, qt, kt, lt: (kt[s], 0)),
                  pl.BlockSpec((Q_BLOCK, LANES), lambda s, qt, kt, lt: (qt[s], 0))],
        out_specs=[pl.BlockSpec((1, n_sub_total, Q_BLOCK, IDX_SUB), lambda s, qt, kt, lt: (qt[s], 0, 0, 0)),
                   pl.BlockSpec((Q_BLOCK, 2 * Q_BLOCK), lambda s, qt, kt, lt: (qt[s], 0))],
        scratch_shapes=[pltpu.VMEM((n_sub_total, Q_BLOCK, IDX_SUB), I32)],
    )
    return pl.pallas_call(
        functools.partial(_idx_kernel, k_top=k_top, n_sub_total=n_sub_total),
        grid_spec=grid_spec,
        out_shape=[jax.ShapeDtypeStruct((nq, n_sub_total, Q_BLOCK, IDX_SUB), BF16),
                   jax.ShapeDtypeStruct((L, 2 * Q_BLOCK), BF16)],
        compiler_params=_cparams(("arbitrary",), vmem),
        name="dsa_idx_select",
    )(*tabs, proj, ki, wi)


M_INIT = -1e30
FAR_SUBS = 2
LOG2E = math.log2(math.e)


def _attn_kernel(qi_tab, kj_tab, kind_tab, first_tab,
                 tab_ref, q_ref, kf_ref, vf_ref, klo_ref, khi_ref, vlo_ref, vhi_ref, far_ref, near_ref,
                 o_ref, m_scr, l_scr, acc_scr, b_scr, s_scr):
    s = pl.program_id(0)

    @pl.when(s == 0)
    def _():
        r_i = lax.broadcasted_iota(I32, (Q_BLOCK, 2 * Q_BLOCK), 0)
        c_i = lax.broadcasted_iota(I32, (Q_BLOCK, 2 * Q_BLOCK), 1)
        d = jnp.maximum(Q_BLOCK + r_i - c_i, 0)
        max_exact = N_BUCKETS // 2
        df = jnp.maximum(d, 1).astype(F32)
        large = max_exact + (jnp.log(df / max_exact) / math.log(MAX_DISTANCE / max_exact)
                             * (N_BUCKETS - max_exact)).astype(I32)
        large = jnp.minimum(large, N_BUCKETS - 1)
        bkt = jnp.where(d < max_exact, d, large)
        for h in range(DSA_HEADS):
            acc = jnp.zeros((Q_BLOCK, 2 * Q_BLOCK), F32)
            for b in range(N_BUCKETS):
                acc = jnp.where(bkt == b, (tab_ref[b, h] - tab_ref[N_BUCKETS - 1, h]) * LOG2E, acc)
            b_scr[h] = acc

    @pl.when(first_tab[s] == 1)
    def _():
        m_scr[...] = jnp.full_like(m_scr, M_INIT)
        l_scr[...] = jnp.zeros_like(l_scr)
        acc_scr[...] = jnp.zeros_like(acc_scr)

    def run_units(k_ref, v_ref, mask_of, bias_of, width, row0=0):
        nt = width // LANES
        rows = slice(row0, row0 + width)
        for h in range(DSA_HEADS):
            g = h // DSA_GROUP
            logits = lax.dot_general(q_ref[:, h * DSA_DH:(h + 1) * DSA_DH],
                                     k_ref[rows, g * DSA_DH:(g + 1) * DSA_DH], _NT, preferred_element_type=F32)
            if bias_of is not None:
                logits = logits + bias_of(h)
            s_scr[h, :, 0:width] = logits.astype(BF16)
        ones = jnp.ones((width, DSA_DH), BF16)
        for h in range(DSA_HEADS):
            g = h // DSA_GROUP
            mask = mask_of()
            tiles = [s_scr[h, :, t * LANES:(t + 1) * LANES] + mask[:, t * LANES:(t + 1) * LANES]
                     for t in range(nt)]
            tmax = tiles[0]
            for t in range(1, nt):
                tmax = jnp.maximum(tmax, tiles[t])
            m_prev = m_scr[h]
            m_new = jnp.maximum(m_prev, jnp.max(tmax.astype(F32), axis=-1, keepdims=True))
            alpha = jnp.exp2(m_prev - m_new)
            m_bf = m_new.astype(BF16)
            p = [jnp.exp2((tiles[t] - m_bf).astype(F32)).astype(BF16) for t in range(nt)]
            pb = jnp.concatenate(p, axis=-1) if nt > 1 else p[0]
            v_aug = jnp.concatenate([v_ref[rows, g * DSA_DH:(g + 1) * DSA_DH], ones], axis=-1)
            pv = jnp.dot(pb, v_aug, preferred_element_type=F32)
            acc_scr[h] = alpha * acc_scr[h] + pv[:, :DSA_DH]
            l_scr[h] = alpha * l_scr[h] + pv[:, DSA_DH:]
            m_scr[h] = m_new

    @pl.when(kind_tab[s] == 0)
    def _():
        for sub in range(FAR_SUBS):
            run_units(kf_ref, vf_ref, lambda sub=sub: far_ref[0, sub], None, IDX_SUB, sub * IDX_SUB)

    @pl.when(kind_tab[s] == 1)
    def _():
        for half, (k_ref, v_ref) in enumerate(((klo_ref, vlo_ref), (khi_ref, vhi_ref))):
            cols = slice(half * Q_BLOCK, (half + 1) * Q_BLOCK)
            run_units(k_ref, v_ref, lambda cols=cols: near_ref[:, cols],
                      lambda h, cols=cols: b_scr[h][:, cols], Q_BLOCK)
        for h in range(DSA_HEADS):
            o_ref[:, h * DSA_DH:(h + 1) * DSA_DH] = (acc_scr[h] / l_scr[h]).astype(o_ref.dtype)


def _masked_attention(proj, far, near, rel_bias):
    L = proj.shape[0]
    nq = L // Q_BLOCK
    far_keys = FAR_SUBS * IDX_SUB
    per = far_keys // Q_BLOCK
    qi_l, kj_l, kind_l, first_l = [], [], [], []
    for i in range(nq):
        n_far = -(-i // per)
        for j in range(n_far):
            qi_l.append(i); kj_l.append(j); kind_l.append(0); first_l.append(1 if j == 0 else 0)
        qi_l.append(i); kj_l.append(max(n_far - 1, 0)); kind_l.append(1); first_l.append(1 if n_far == 0 else 0)
    tabs = [jnp.asarray(np.asarray(t, np.int32)) for t in (qi_l, kj_l, kind_l, first_l)]
    qw = DSA_HEADS * DSA_DH
    kvw = DSA_KV_HEADS * DSA_DH
    k_col = (2 * qw) // kvw
    v_col = k_col + 1
    hw = DSA_HEADS
    vmem = (2 * (Q_BLOCK * qw * 2 * 2 + 2 * far_keys * kvw * 2 + 4 * Q_BLOCK * kvw * 2
                 + Q_BLOCK * far_keys * 2 + Q_BLOCK * 2 * Q_BLOCK * 2)
            + hw * Q_BLOCK * (3 * LANES + 2 * Q_BLOCK) * 4 + (16 << 20))
    idx = lambda f: (lambda s, qt, kt, kd, ft: f(qt[s], kt[s]))
    grid_spec = pltpu.PrefetchScalarGridSpec(
        num_scalar_prefetch=4,
        grid=(len(qi_l),),
        in_specs=[pl.BlockSpec(memory_space=pltpu.SMEM),
                  pl.BlockSpec((Q_BLOCK, qw), idx(lambda i, j: (i, 0))),
                  pl.BlockSpec((far_keys, kvw), idx(lambda i, j: (j, k_col))),
                  pl.BlockSpec((far_keys, kvw), idx(lambda i, j: (j, v_col))),
                  pl.BlockSpec((Q_BLOCK, kvw), idx(lambda i, j: (jnp.maximum(i - 1, 0), k_col))),
                  pl.BlockSpec((Q_BLOCK, kvw), idx(lambda i, j: (i, k_col))),
                  pl.BlockSpec((Q_BLOCK, kvw), idx(lambda i, j: (jnp.maximum(i - 1, 0), v_col))),
                  pl.BlockSpec((Q_BLOCK, kvw), idx(lambda i, j: (i, v_col))),
                  pl.BlockSpec((1, FAR_SUBS, Q_BLOCK, IDX_SUB), idx(lambda i, j: (i, j, 0, 0))),
                  pl.BlockSpec((Q_BLOCK, 2 * Q_BLOCK), idx(lambda i, j: (i, 0)))],
        out_specs=pl.BlockSpec((Q_BLOCK, qw), idx(lambda i, j: (i, 0))),
        scratch_shapes=[pltpu.VMEM((hw, Q_BLOCK, LANES), F32),
                        pltpu.VMEM((hw, Q_BLOCK, LANES), F32),
                        pltpu.VMEM((hw, Q_BLOCK, DSA_DH), F32),
                        pltpu.VMEM((hw, Q_BLOCK, 2 * Q_BLOCK), F32),
                        pltpu.VMEM((hw, Q_BLOCK, IDX_SUB), BF16)],
    )
    return pl.pallas_call(
        _attn_kernel,
        grid_spec=grid_spec,
        out_shape=jax.ShapeDtypeStruct((L, qw), BF16),
        compiler_params=_cparams(("arbitrary",), vmem),
        name="dsa_attention",
    )(*tabs, rel_bias, proj, proj, proj, proj, proj, proj, proj, far, near)


def _dsa_mixer(xb, w_in, ln_g, ln_b, rel_bias):
    L = xb.shape[0]
    k_top = min(TOPK_MAX, L // 4)
    sq = DSA_HEADS * DSA_DH
    skv = DSA_KV_HEADS * DSA_DH
    si = IDX_HEADS * IDX_DIM
    w_q = w_in[:, :sq]
    w_k = w_in[:, sq:sq + skv]
    w_v = w_in[:, sq + skv:sq + 2 * skv]
    w_qi = w_in[:, sq + 2 * skv:sq + 2 * skv + si]
    w_ki = w_in[:, sq + 2 * skv + si:sq + 2 * skv + si + IDX_DIM]
    w_wi = w_in[:, sq + 2 * skv + si + IDX_DIM:]
    w_main = jnp.concatenate([w_q, w_qi, w_k, w_v], axis=1).astype(BF16)
    colscale = jnp.concatenate([jnp.full((1, sq), DSA_DH ** -0.5 * LOG2E, F32),
                                jnp.ones((1, si + 2 * skv), F32)], axis=1)
    w_small = jnp.concatenate([w_ki, w_wi, jnp.zeros((D_MODEL, LANES - IDX_HEADS), F32)], axis=1).astype(BF16)
    proj = _matmul_scaled(xb, w_main, colscale, BF16, tm=1024, tn=1024)
    ki, wi = _dsa_small(xb, w_small, ln_g, ln_b, tm=1024)
    far, near = _idx_select(proj, ki, wi, k_top)
    return _masked_attention(proj, far, near, rel_bias)


def kernel(x, p, gdn_w_in, gdn_conv_w, gdn_a_log, gdn_dt_bias, gdn_norm_g, gdn_w_o, dsa_w_in, dsa_kidx_ln_g, dsa_kidx_ln_b, dsa_w_o, rel_bias, ln1_g, ln1_b, ffn_w_gate, ffn_w_up, ffn_conv_w, ffn_w_down, ln2_g, ln2_b, ple_w_proj, ple_w_gate):
    assert x.shape[0] == 1 and x.shape[2] == D_MODEL
    xf = x[0]
    xb = xf.astype(BF16)
    ia = ib = 0
    for i in range(DEPTH):
        if i % 2 == 0:
            mix = _gdn_mixer(xb, gdn_w_in[ia], gdn_conv_w[ia], gdn_a_log[ia], gdn_dt_bias[ia], gdn_norm_g[ia])
            w_o = gdn_w_o[ia]
            ia += 1
        else:
            mix = _dsa_mixer(xb, dsa_w_in[ib], dsa_kidx_ln_g[ib], dsa_kidx_ln_b[ib], rel_bias)
            w_o = dsa_w_o[ib]
            ib += 1
        xf, xb = _proj_res_ln(mix, w_o.astype(BF16), xf, ln1_g[i], ln1_b[i], tm=512, sub=256)
        hmid = _ffn_up(xb, ffn_w_gate[i].astype(BF16), ffn_w_up[i].astype(BF16), ffn_conv_w[i], tm=1024, tn=512)
        xf, xb = _proj_res_ln(hmid, ffn_w_down[i].astype(BF16), xf, ln2_g[i], ln2_b[i], tm=512, sub=256)
        xf, xb = _ple(xb, xf, ple_w_gate[i].astype(BF16), p[i, 0], ple_w_proj[i].astype(BF16), tm=1024, tn=1024)
    return xf[None]
```

```python
import functools
import math

import jax
import jax.numpy as jnp
import numpy as np
from jax import lax
from jax.experimental import pallas as pl
from jax.experimental.pallas import tpu as pltpu

F32 = jnp.float32
BF16 = jnp.bfloat16
I32 = jnp.int32

D_MODEL = 2048
GDN_QK_HEADS = 16
GDN_V_HEADS = 32
GDN_DK = 128
GDN_DV = 128
GDN_CONV = 4
GDN_CHUNK = 64
GDN_QK_W = GDN_QK_HEADS * GDN_DK
GDN_V_W = GDN_V_HEADS * GDN_DV
DSA_HEADS = 16
DSA_KV_HEADS = 4
DSA_GROUP = DSA_HEADS // DSA_KV_HEADS
DSA_DH = 128
IDX_HEADS = 16
IDX_DIM = 128
TOPK_MAX = 256
N_BUCKETS = 32
MAX_DISTANCE = 128
D_FF = 5120
FFN_CONV = 3
PLE_DIM = 256
DEPTH = 2
DN_ALPHA = (2.0 * DEPTH) ** 0.25
LN_EPS = 1e-5
RMS_EPS = 1e-6

V7X_VMEM_BYTES = 64 * 1024 * 1024
V7X_VMEM_BUDGET = 56 * 1024 * 1024
LANES = 128
BF16_SUBLANES = 16

HALO = BF16_SUBLANES
Q_BLOCK = 128
IDX_KEY_BLOCK = 2048
IDX_SUB = 512
INT_MIN = -(2 ** 31)

_NT = (((1,), (1,)), ((), ()))
_TN = (((0,), (0,)), ((), ()))


def _cparams(semantics, vmem_bytes):
    return pltpu.CompilerParams(dimension_semantics=semantics,
                                vmem_limit_bytes=int(min(V7X_VMEM_BUDGET, vmem_bytes)))


def _silu(y):
    return y * jax.nn.sigmoid(y)


def _mm_scale_kernel(x_ref, w_ref, cs_ref, o_ref):
    acc = jnp.dot(x_ref[...], w_ref[...], preferred_element_type=F32)
    o_ref[...] = (acc * cs_ref[...]).astype(o_ref.dtype)


def _matmul_scaled(x, w, colscale, out_dtype, tm, tn):
    M, K = x.shape
    N = w.shape[1]
    tm, tn = min(tm, M), min(tn, N)
    osz = jnp.dtype(out_dtype).itemsize
    vmem = 2 * (tm * K * 2 + K * tn * 2 + tm * tn * osz) + 2 * tm * tn * 4
    return pl.pallas_call(
        _mm_scale_kernel,
        grid=(M // tm, N // tn),
        in_specs=[pl.BlockSpec((tm, K), lambda i, j: (i, 0)),
                  pl.BlockSpec((K, tn), lambda i, j: (0, j)),
                  pl.BlockSpec((1, tn), lambda i, j: (0, j))],
        out_specs=pl.BlockSpec((tm, tn), lambda i, j: (i, j)),
        out_shape=jax.ShapeDtypeStruct((M, N), out_dtype),
        compiler_params=_cparams(("parallel", "parallel"), vmem),
        name="matmul_scaled",
    )(x, w, colscale)


CONV_SUB = 256
FFN_SUB = 512


CONV_ROWS = 1024


def _causal_conv(g, gh, cw_ref, g_scr, kc, tm, cols, r0=0):
    if gh is not None:
        g_scr[0:HALO, cols] = gh
    g_scr[HALO + r0:HALO + r0 + tm, cols] = g
    y = cw_ref[kc - 1:kc, cols] * g
    for j in range(kc - 1):
        off = HALO + r0 - (kc - 1) + j
        y = y + cw_ref[j:j + 1, cols] * g_scr[off:off + tm, cols]
    return y


def _mm_conv_silu_kernel(x_ref, xh_ref, w_ref, cw_ref, o_ref, g_scr, *, kc, tm):
    first = pl.program_id(0) == 0
    rs = min(CONV_ROWS, tm)
    for c0 in range(0, o_ref.shape[1], CONV_SUB):
        cols = slice(c0, c0 + CONV_SUB)
        w = w_ref[:, cols]
        gh = jnp.dot(xh_ref[...], w, preferred_element_type=F32)
        gh = jnp.where(first, 0.0, gh)
        for r0 in range(0, tm, rs):
            g = jnp.dot(x_ref[r0:r0 + rs, :], w, preferred_element_type=F32)
            y = _causal_conv(g, gh if r0 == 0 else None, cw_ref, g_scr, kc, rs, cols, r0)
            o_ref[r0:r0 + rs, cols] = _silu(y).astype(o_ref.dtype)


def _proj_conv_silu(x, w, conv_w, tm, tn):
    M, K = x.shape
    N = w.shape[1]
    kc = conv_w.shape[0]
    tm, tn = min(tm, M), min(tn, N)
    hb = tm // HALO
    vmem = 2 * (tm * K * 2 + HALO * K * 2 + K * tn * 2 + tm * tn * 2) + 4 * tm * tn * 4
    return pl.pallas_call(
        functools.partial(_mm_conv_silu_kernel, kc=kc, tm=tm),
        grid=(M // tm, N // tn),
        in_specs=[pl.BlockSpec((tm, K), lambda i, j: (i, 0)),
                  pl.BlockSpec((HALO, K), lambda i, j: (jnp.maximum(i * hb - 1, 0), 0)),
                  pl.BlockSpec((K, tn), lambda i, j: (0, j)),
                  pl.BlockSpec((kc, tn), lambda i, j: (0, j))],
        out_specs=pl.BlockSpec((tm, tn), lambda i, j: (i, j)),
        out_shape=jax.ShapeDtypeStruct((M, N), BF16),
        scratch_shapes=[pltpu.VMEM((tm + HALO, tn), F32)],
        compiler_params=_cparams(("parallel", "parallel"), vmem),
        name="proj_conv_silu",
    )(x, x, w, conv_w)


def _ffn_up_kernel(x_ref, xh_ref, wg_ref, wu_ref, cw_ref, o_ref, g_scr, *, kc, tm):
    first = pl.program_id(0) == 0
    for c0 in range(0, o_ref.shape[1], FFN_SUB):
        cols = slice(c0, c0 + FFN_SUB)
        wg = wg_ref[:, cols]
        g = jnp.dot(x_ref[...], wg, preferred_element_type=F32)
        gh = jnp.dot(xh_ref[...], wg, preferred_element_type=F32)
        gh = jnp.where(first, 0.0, gh)
        u = jnp.dot(x_ref[...], wu_ref[:, cols], preferred_element_type=F32)
        y = _causal_conv(g, gh, cw_ref, g_scr, kc, tm, cols)
        o_ref[:, cols] = (_silu(y) * u).astype(o_ref.dtype)


def _ffn_up(x, w_gate, w_up, conv_w, tm, tn):
    M, K = x.shape
    N = w_gate.shape[1]
    kc = conv_w.shape[0]
    tm, tn = min(tm, M), min(tn, N)
    hb = tm // HALO
    vmem = 2 * (tm * K * 2 + HALO * K * 2 + 2 * K * tn * 2 + tm * tn * 2) + 6 * tm * tn * 4
    return pl.pallas_call(
        functools.partial(_ffn_up_kernel, kc=kc, tm=tm),
        grid=(M // tm, N // tn),
        in_specs=[pl.BlockSpec((tm, K), lambda i, j: (i, 0)),
                  pl.BlockSpec((HALO, K), lambda i, j: (jnp.maximum(i * hb - 1, 0), 0)),
                  pl.BlockSpec((K, tn), lambda i, j: (0, j)),
                  pl.BlockSpec((K, tn), lambda i, j: (0, j)),
                  pl.BlockSpec((kc, tn), lambda i, j: (0, j))],
        out_specs=pl.BlockSpec((tm, tn), lambda i, j: (i, j)),
        out_shape=jax.ShapeDtypeStruct((M, N), BF16),
        scratch_shapes=[pltpu.VMEM((tm + HALO, tn), F32)],
        compiler_params=_cparams(("parallel", "parallel"), vmem),
        name="ffn_up",
    )(x, x, w_gate, w_up, conv_w)


def _mm_res_ln_kernel(a_ref, w_ref, res_ref, g_ref, b_ref, of_ref, ob_ref, *, sub):
    for r0 in range(0, a_ref.shape[0], sub):
        rows = slice(r0, r0 + sub)
        acc = jnp.dot(a_ref[rows, :], w_ref[...], preferred_element_type=F32)
        y = DN_ALPHA * res_ref[rows, :] + acc
        mu = jnp.mean(y, axis=-1, keepdims=True)
        yc = y - mu
        var = jnp.mean(yc * yc, axis=-1, keepdims=True)
        out = yc * lax.rsqrt(var + LN_EPS) * g_ref[...] + b_ref[...]
        of_ref[rows, :] = out
        ob_ref[rows, :] = out.astype(BF16)


def _proj_res_ln(a, w, res, g, b, tm, sub):
    M, K = a.shape
    N = w.shape[1]
    tm = min(tm, M)
    sub = min(sub, tm)
    vmem = K * N * 2 + 2 * (tm * K * 2 + tm * N * 4 + tm * N * 4 + tm * N * 2) + 4 * sub * N * 4
    return pl.pallas_call(
        functools.partial(_mm_res_ln_kernel, sub=sub),
        grid=(M // tm,),
        in_specs=[pl.BlockSpec((tm, K), lambda i: (i, 0)),
                  pl.BlockSpec((K, N), lambda i: (0, 0), pipeline_mode=pl.Buffered(1)),
                  pl.BlockSpec((tm, N), lambda i: (i, 0)),
                  pl.BlockSpec((1, N), lambda i: (0, 0)),
                  pl.BlockSpec((1, N), lambda i: (0, 0))],
        out_specs=[pl.BlockSpec((tm, N), lambda i: (i, 0)),
                   pl.BlockSpec((tm, N), lambda i: (i, 0))],
        out_shape=[jax.ShapeDtypeStruct((M, N), F32), jax.ShapeDtypeStruct((M, N), BF16)],
        compiler_params=_cparams(("parallel",), vmem),
        name="proj_res_ln",
    )(a, w, res, g.reshape(1, N), b.reshape(1, N))


def _ple_kernel(xb_ref, wg_ref, p_ref, wp_ref, xr_ref, of_ref, ob_ref):
    gate = jax.nn.sigmoid(jnp.dot(xb_ref[...], wg_ref[...], preferred_element_type=F32))
    pe = jnp.dot(p_ref[...].astype(BF16), wp_ref[...], preferred_element_type=F32)
    out = xr_ref[...] + gate * pe
    of_ref[...] = out
    ob_ref[...] = out.astype(BF16)


def _ple(xb, xf, w_gate, p, w_proj, tm, tn):
    M, K = xb.shape
    N = w_gate.shape[1]
    P = p.shape[1]
    tm, tn = min(tm, M), min(tn, N)
    vmem = 2 * (tm * K * 2 + K * tn * 2 + tm * P * 4 + P * tn * 2 + tm * tn * 10) + 4 * tm * tn * 4
    return pl.pallas_call(
        _ple_kernel,
        grid=(M // tm, N // tn),
        in_specs=[pl.BlockSpec((tm, K), lambda i, j: (i, 0)),
                  pl.BlockSpec((K, tn), lambda i, j: (0, j)),
                  pl.BlockSpec((tm, P), lambda i, j: (i, 0)),
                  pl.BlockSpec((P, tn), lambda i, j: (0, j)),
                  pl.BlockSpec((tm, tn), lambda i, j: (i, j))],
        out_specs=[pl.BlockSpec((tm, tn), lambda i, j: (i, j)),
                   pl.BlockSpec((tm, tn), lambda i, j: (i, j))],
        out_shape=[jax.ShapeDtypeStruct((M, N), F32), jax.ShapeDtypeStruct((M, N), BF16)],
        compiler_params=_cparams(("parallel", "parallel"), vmem),
        name="ple",
    )(xb, w_gate, p, w_proj, xf)


GDN_HB = 4
GDN_NC = 4


def _gdn_kernel(q_ref, k_ref, v_ref, z_ref, sc_ref, hp_ref, ng_ref, o_ref, s_scr):
    C = GDN_CHUNK

    @pl.when(pl.program_id(1) == 0)
    def _():
        s_scr[...] = jnp.zeros_like(s_scr)

    row = lax.broadcasted_iota(I32, (C, C), 0)
    col = lax.broadcasted_iota(I32, (C, C), 1)
    tri = row >= col
    strict = row > col
    eye = row == col
    tri_f = tri.astype(F32)
    eye_f = eye.astype(F32)

    raw = sc_ref[...]
    a_log = hp_ref[0, 0:1, :]
    dt_b = hp_ref[0, 1:2, :]
    xs = raw + dt_b
    softplus = jnp.maximum(xs, 0.0) + jnp.log1p(jnp.exp(-jnp.abs(xs)))
    g_all = -jnp.exp(a_log) * softplus
    beta_all = jax.nn.sigmoid(raw)
    ng = ng_ref[...]

    units = [(c, j) for c in range(GDN_NC) for j in range(GDN_HB)]
    kb_l, rhs_l, decay_l, qd_l, kd_l, kbf_l, qbf_l, gl_l = [], [], [], [], [], [], [], []
    for c in range(GDN_NC):
        r0 = c * C
        gc = jnp.dot(tri_f, g_all[r0:r0 + C, :], precision=lax.Precision.HIGHEST,
                     preferred_element_type=F32)
        g_last = gc[C - 1:C, :]
        e_gc = jnp.exp(gc)
        e_rest = jnp.exp(g_last - gc)
        e_last = jnp.exp(g_last)
        beta_c = beta_all[r0:r0 + C, :]
        qn, kn = [], []
        for hq in range(GDN_HB // 2):
            qf = q_ref[r0:r0 + C, hq * GDN_DK:(hq + 1) * GDN_DK].astype(F32)
            kf = k_ref[r0:r0 + C, hq * GDN_DK:(hq + 1) * GDN_DK].astype(F32)
            qn.append(qf * lax.rsqrt(jnp.sum(qf * qf, axis=-1, keepdims=True) + RMS_EPS) * (GDN_DK ** -0.5))
            kn.append(kf * lax.rsqrt(jnp.sum(kf * kf, axis=-1, keepdims=True) + RMS_EPS))
        for j in range(GDN_HB):
            q_h, k_h = qn[j // 2], kn[j // 2]
            vf = v_ref[r0:r0 + C, j * GDN_DV:(j + 1) * GDN_DV].astype(F32)
            beta = beta_c[:, GDN_HB + j:GDN_HB + j + 1]
            kb = k_h * beta
            gcb = jnp.broadcast_to(gc[:, j:j + 1], (C, C))
            gcr = jnp.sum(jnp.where(eye, gcb, 0.0), axis=0, keepdims=True)
            decay_l.append(jnp.where(tri, jnp.exp(jnp.where(tri, gcb - gcr, 0.0)), 0.0))
            kb_l.append(kb.astype(BF16))
            rhs_l.append(jnp.concatenate([vf * beta, kb * e_gc[:, j:j + 1]], axis=-1).astype(BF16))
            qd_l.append((q_h * e_gc[:, j:j + 1]).astype(BF16))
            kd_l.append((k_h * e_rest[:, j:j + 1]).astype(BF16))
            kbf_l.append(k_h.astype(BF16))
            qbf_l.append(q_h.astype(BF16))
            gl_l.append(e_last[:, j:j + 1])

    n_u = len(units)
    kk_l = [lax.dot_general(kb_l[u], kbf_l[u], _NT, preferred_element_type=F32) for u in range(n_u)]
    qk_l = [lax.dot_general(qbf_l[u], kbf_l[u], _NT, preferred_element_type=F32) for u in range(n_u)]
    qk_l = [jnp.where(tri, qk_l[u] * decay_l[u], 0.0).astype(BF16) for u in range(n_u)]
    x_l = [(-jnp.where(strict, kk_l[u] * decay_l[u], 0.0)) for u in range(n_u)]
    t_l = [eye_f + x_l[u] for u in range(n_u)]
    x_l = [x.astype(BF16) for x in x_l]
    for _ in range(5):
        x_l = [jnp.dot(x, x, preferred_element_type=F32).astype(BF16) for x in x_l]
        t_l = [t + jnp.dot(t.astype(BF16), x, preferred_element_type=F32) for t, x in zip(t_l, x_l)]
    sol_l = [jnp.dot(t_l[u].astype(BF16), rhs_l[u], preferred_element_type=F32) for u in range(n_u)]

    s_cur = [s_scr[j] for j in range(GDN_HB)]
    for c in range(GDN_NC):
        r0 = c * C
        us = [c * GDN_HB + j for j in range(GDN_HB)]
        s_bf = [s.astype(BF16) for s in s_cur]
        ws_l = [jnp.dot(sol_l[u][:, GDN_DV:].astype(BF16), s_bf[j], preferred_element_type=F32)
                for j, u in enumerate(us)]
        qs_l = [jnp.dot(qd_l[u], s_bf[j], preferred_element_type=F32) for j, u in enumerate(us)]
        vn_l = [(sol_l[u][:, :GDN_DV] - ws_l[j]).astype(BF16) for j, u in enumerate(us)]
        kv_l = [lax.dot_general(kd_l[u], vn_l[j], _TN, preferred_element_type=F32) for j, u in enumerate(us)]
        ov_l = [jnp.dot(qk_l[u], vn_l[j], preferred_element_type=F32) for j, u in enumerate(us)]
        s_cur = [s_cur[j] * gl_l[u] + kv_l[j] for j, u in enumerate(us)]
        for j in range(GDN_HB):
            o = qs_l[j] + ov_l[j]
            zf = z_ref[r0:r0 + C, j * GDN_DV:(j + 1) * GDN_DV].astype(F32)
            o = o * lax.rsqrt(jnp.mean(o * o, axis=-1, keepdims=True) + RMS_EPS) * ng * _silu(zf)
            o_ref[r0:r0 + C, j * GDN_DV:(j + 1) * GDN_DV] = o.astype(o_ref.dtype)
    for j in range(GDN_HB):
        s_scr[j] = s_cur[j]


def _gdn_core(qkv, z, scal, hparams, norm_g):
    L = qkv.shape[0]
    G = GDN_V_HEADS // GDN_HB
    R = GDN_NC * GDN_CHUNK
    qw = (GDN_HB // 2) * GDN_DK
    vw = GDN_HB * GDN_DV
    k_blk0 = GDN_QK_W // qw
    v_blk0 = 2 * GDN_QK_W // vw
    vmem = 2 * (2 * R * qw * 2 + 2 * R * vw * 2 + R * LANES * 4 + R * vw * 2) + (16 << 20)
    return pl.pallas_call(
        _gdn_kernel,
        grid=(G, L // R),
        in_specs=[pl.BlockSpec((R, qw), lambda g, s: (s, g)),
                  pl.BlockSpec((R, qw), lambda g, s: (s, k_blk0 + g)),
                  pl.BlockSpec((R, vw), lambda g, s: (s, v_blk0 + g)),
                  pl.BlockSpec((R, vw), lambda g, s: (s, g)),
                  pl.BlockSpec((R, LANES), lambda g, s: (s, g)),
                  pl.BlockSpec((1, 8, LANES), lambda g, s: (g, 0, 0)),
                  pl.BlockSpec((1, GDN_DV), lambda g, s: (0, 0))],
        out_specs=pl.BlockSpec((R, vw), lambda g, s: (s, g)),
        out_shape=jax.ShapeDtypeStruct((L, GDN_V_W), BF16),
        scratch_shapes=[pltpu.VMEM((GDN_HB, GDN_DK, GDN_DV), F32)],
        compiler_params=_cparams(("parallel", "arbitrary"), vmem),
        name="gdn_core",
    )(qkv, qkv, qkv, z, scal, hparams, norm_g.reshape(1, GDN_DV))


def _gdn_mixer(xb, w_in, conv_w, a_log, dt_bias, norm_g):
    L = xb.shape[0]
    nqkv = 2 * GDN_QK_W + GDN_V_W
    w_qkv = w_in[:, :nqkv].astype(BF16)
    w_z = w_in[:, nqkv:nqkv + GDN_V_W].astype(BF16)
    w_ab = w_in[:, nqkv + GDN_V_W:]
    G = GDN_V_HEADS // GDN_HB
    w_a = w_ab[:, :GDN_V_HEADS].reshape(D_MODEL, G, GDN_HB)
    w_b = w_ab[:, GDN_V_HEADS:].reshape(D_MODEL, G, GDN_HB)
    w_sc = jnp.concatenate([w_a, w_b, jnp.zeros((D_MODEL, G, LANES - 2 * GDN_HB), F32)], axis=-1)
    w_sc = w_sc.reshape(D_MODEL, G * LANES).astype(BF16)
    qkv = _proj_conv_silu(xb, w_qkv, conv_w, tm=1024, tn=1024)
    ones_z = jnp.ones((1, GDN_V_W), F32)
    z = _matmul_scaled(xb, w_z, ones_z, BF16, tm=1024, tn=1024)
    scal = _matmul_scaled(xb, w_sc, jnp.ones((1, G * LANES), F32), F32, tm=1024, tn=G * LANES)
    hp = jnp.zeros((G, 8, LANES), F32)
    hp = hp.at[:, 0, :GDN_HB].set(a_log.reshape(G, GDN_HB))
    hp = hp.at[:, 1, :GDN_HB].set(dt_bias.reshape(G, GDN_HB))
    return _gdn_core(qkv, z, scal, hp, norm_g)


def _dsa_small_kernel(x_ref, w_ref, g_ref, b_ref, ki_ref, wi_ref):
    acc = jnp.dot(x_ref[...], w_ref[...], preferred_element_type=F32)
    ki = acc[:, :IDX_DIM]
    mu = jnp.mean(ki, axis=-1, keepdims=True)
    kc = ki - mu
    var = jnp.mean(kc * kc, axis=-1, keepdims=True)
    ki_ref[...] = (kc * lax.rsqrt(var + LN_EPS) * g_ref[...] + b_ref[...]).astype(ki_ref.dtype)
    wi_ref[...] = acc[:, IDX_DIM:] * ((IDX_HEADS ** -0.5) * (IDX_DIM ** -0.5))


def _dsa_small(xb, w_small, ln_g, ln_b, tm):
    M, K = xb.shape
    tm = min(tm, M)
    N = 2 * LANES
    vmem = 2 * (tm * K * 2 + K * N * 2 + tm * LANES * 6) + 4 * tm * N * 4
    return pl.pallas_call(
        _dsa_small_kernel,
        grid=(M // tm,),
        in_specs=[pl.BlockSpec((tm, K), lambda i: (i, 0)),
                  pl.BlockSpec((K, N), lambda i: (0, 0)),
                  pl.BlockSpec((1, IDX_DIM), lambda i: (0, 0)),
                  pl.BlockSpec((1, IDX_DIM), lambda i: (0, 0))],
        out_specs=[pl.BlockSpec((tm, IDX_DIM), lambda i: (i, 0)),
                   pl.BlockSpec((tm, LANES), lambda i: (i, 0))],
        out_shape=[jax.ShapeDtypeStruct((M, IDX_DIM), BF16), jax.ShapeDtypeStruct((M, LANES), F32)],
        compiler_params=_cparams(("parallel",), vmem),
        name="dsa_idx_proj",
    )(xb, w_small, ln_g.reshape(1, IDX_DIM), ln_b.reshape(1, IDX_DIM))


def _sortable_key(score):
    bits = lax.bitcast_convert_type(score, I32)
    return jnp.where(bits >= 0, bits, bits ^ jnp.int32(0x7FFFFFFF))


def _idx_kernel(qi_tab, kj_tab, last_tab,
                qidx_ref, kidx_ref, wi_ref, far_ref, near_ref, key_scr, *, k_top, n_sub_total):
    s = pl.program_id(0)
    i = qi_tab[s]
    j = kj_tab[s]
    nsub = IDX_KEY_BLOCK // IDX_SUB
    t_col = i * Q_BLOCK + lax.broadcasted_iota(I32, (Q_BLOCK, 1), 0)
    lane_sub = lax.broadcasted_iota(I32, (Q_BLOCK, IDX_SUB), 1)
    wi = wi_ref[...]

    for sub in range(nsub):
        ki_sub = kidx_ref[sub * IDX_SUB:(sub + 1) * IDX_SUB, :]
        acc = jnp.zeros((Q_BLOCK, IDX_SUB), F32)
        for h in range(IDX_HEADS):
            sc = lax.dot_general(qidx_ref[:, h * IDX_DIM:(h + 1) * IDX_DIM], ki_sub, _NT,
                                 preferred_element_type=F32)
            acc = acc + jnp.maximum(sc, 0.0) * wi[:, h:h + 1]
        s_idx = j * IDX_KEY_BLOCK + sub * IDX_SUB + lane_sub
        key_scr[j * nsub + sub] = jnp.where(s_idx <= t_col, _sortable_key(acc), INT_MIN)

    @pl.when(last_tab[s] == 1)
    def _():
        n_chunks = (j + 1) * nsub

        def count(pred, ref_val):
            refb = jnp.broadcast_to(ref_val, (Q_BLOCK, LANES))

            def body(c, cnt):
                blk = key_scr[c]
                for l in range(IDX_SUB // LANES):
                    cnt = jnp.where(pred(blk[:, l * LANES:(l + 1) * LANES], refb), cnt + 1, cnt)
                return cnt

            cnt = lax.fori_loop(0, n_chunks, body, jnp.zeros((Q_BLOCK, LANES), I32))
            return jnp.sum(cnt, axis=1, keepdims=True)

        def bit_body(b, carry):
            thr, n_ge = carry
            cand = thr + jnp.left_shift(jnp.int32(1), 31 - b)
            n_cand = count(lambda a, r: a >= r, cand)
            ok = n_cand >= k_top
            return jnp.where(ok, cand, thr), jnp.where(ok, n_cand, n_ge)

        thr, n_ge = lax.fori_loop(0, 32, bit_body,
                                  (jnp.full((Q_BLOCK, 1), INT_MIN, I32),
                                   jnp.broadcast_to(n_chunks * IDX_SUB, (Q_BLOCK, 1)).astype(I32)))
        thr_b = jnp.broadcast_to(thr, (Q_BLOCK, IDX_SUB))
        n_tied_rows = jnp.max(jnp.where(n_ge != k_top, 1, 0))

        def emit(c, sel):
            s_idx = c * IDX_SUB + lane_sub
            far = sel & (t_col - s_idx >= MAX_DISTANCE)
            far_ref[0, c] = jnp.where(far, 0.0, -jnp.inf).astype(far_ref.dtype)
            key_scr[c] = jnp.where(sel, 1, 0)

        @pl.when(n_tied_rows == 0)
        def _():
            def sel_body(c, carry):
                emit(c, key_scr[c] >= thr_b)
                return carry

            lax.fori_loop(0, n_chunks, sel_body, 0)

        @pl.when(n_tied_rows != 0)
        def _():
            n_gt = count(lambda a, r: a > r, thr)
            need_eq = (k_top - n_gt).astype(F32)
            incl = (lax.broadcasted_iota(I32, (IDX_SUB, IDX_SUB), 0)
                    <= lax.broadcasted_iota(I32, (IDX_SUB, IDX_SUB), 1)).astype(BF16)

            def sel_body(c, carry):
                blk = key_scr[c]
                eq = blk == thr_b
                eq_f = jnp.where(eq, 1.0, 0.0)
                rank = carry + jnp.dot(eq_f.astype(BF16), incl, preferred_element_type=F32)
                s_idx = c * IDX_SUB + lane_sub
                emit(c, ((blk > thr_b) | (eq & (rank <= need_eq))) & (s_idx <= t_col))
                return carry + jnp.sum(eq_f, axis=1, keepdims=True)

            lax.fori_loop(0, n_chunks, sel_body, jnp.zeros((Q_BLOCK, 1), F32))

        def fill_body(c, carry):
            far_ref[0, c] = jnp.full((Q_BLOCK, IDX_SUB), -jnp.inf, far_ref.dtype)
            return carry

        lax.fori_loop(n_chunks, n_sub_total, fill_body, 0)

        def window(blk_idx):
            per = IDX_SUB // Q_BLOCK
            chunk = key_scr[blk_idx // per]
            m = blk_idx % per
            out = chunk[:, 0:Q_BLOCK]
            for q in range(1, per):
                out = jnp.where(m == q, chunk[:, q * Q_BLOCK:(q + 1) * Q_BLOCK], out)
            return out

        r_i = lax.broadcasted_iota(I32, (Q_BLOCK, Q_BLOCK), 0)
        c_i = lax.broadcasted_iota(I32, (Q_BLOCK, Q_BLOCK), 1)
        d_lo = Q_BLOCK + r_i - c_i
        d_hi = r_i - c_i
        near_lo = (window(jnp.maximum(i - 1, 0)) != 0) & (d_lo < MAX_DISTANCE) & (i >= 1)
        near_hi = (window(i) != 0) & (d_hi >= 0) & (d_hi < MAX_DISTANCE)
        near_ref[:, 0:Q_BLOCK] = jnp.where(near_lo, 0.0, -jnp.inf).astype(near_ref.dtype)
        near_ref[:, Q_BLOCK:2 * Q_BLOCK] = jnp.where(near_hi, 0.0, -jnp.inf).astype(near_ref.dtype)


def _idx_select(proj, ki, wi, k_top):
    L = ki.shape[0]
    nq = L // Q_BLOCK
    n_sub_total = L // IDX_SUB
    qi_l, kj_l, last_l = [], [], []
    for i in range(nq):
        j_last = (i * Q_BLOCK + Q_BLOCK - 1) // IDX_KEY_BLOCK
        for j in range(j_last + 1):
            qi_l.append(i)
            kj_l.append(j)
            last_l.append(1 if j == j_last else 0)
    tabs = [jnp.asarray(np.asarray(t, np.int32)) for t in (qi_l, kj_l, last_l)]
    qcol = (DSA_HEADS * DSA_DH) // (IDX_HEADS * IDX_DIM)
    vmem = (L * Q_BLOCK * 4 + 2 * (Q_BLOCK * L * 2) + 2 * (Q_BLOCK * IDX_HEADS * IDX_DIM * 2)
            + 2 * IDX_KEY_BLOCK * IDX_DIM * 2 + (8 << 20))
    grid_spec = pltpu.PrefetchScalarGridSpec(
        num_scalar_prefetch=3,
        grid=(len(qi_l),),
        in_specs=[pl.BlockSpec((Q_BLOCK, IDX_HEADS * IDX_DIM), lambda s, qt, kt, lt: (qt[s], qcol)),
                  pl.BlockSpec((IDX_KEY_BLOCK, IDX_DIM), lambda s, qt, kt, lt: (kt[s], 0)),
                  pl.BlockSpec((Q_BLOCK, LANES), lambda s, qt, kt, lt: (qt[s], 0))],
        out_specs=[pl.BlockSpec((1, n_sub_total, Q_BLOCK, IDX_SUB), lambda s, qt, kt, lt: (qt[s], 0, 0, 0)),
                   pl.BlockSpec((Q_BLOCK, 2 * Q_BLOCK), lambda s, qt, kt, lt: (qt[s], 0))],
        scratch_shapes=[pltpu.VMEM((n_sub_total, Q_BLOCK, IDX_SUB), I32)],
    )
    return pl.pallas_call(
        functools.partial(_idx_kernel, k_top=k_top, n_sub_total=n_sub_total),
        grid_spec=grid_spec,
        out_shape=[jax.ShapeDtypeStruct((nq, n_sub_total, Q_BLOCK, IDX_SUB), BF16),
                   jax.ShapeDtypeStruct((L, 2 * Q_BLOCK), BF16)],
        compiler_params=_cparams(("arbitrary",), vmem),
        name="dsa_idx_select",
    )(*tabs, proj, ki, wi)


M_INIT = -1e30
FAR_SUBS = 2
LOG2E = math.log2(math.e)


def _attn_kernel(qi_tab, kj_tab, kind_tab, first_tab,
                 tab_ref, q_ref, kf_ref, vf_ref, klo_ref, khi_ref, vlo_ref, vhi_ref, far_ref, near_ref,
                 o_ref, m_scr, l_scr, acc_scr, b_scr, s_scr):
    s = pl.program_id(0)

    @pl.when(s == 0)
    def _():
        r_i = lax.broadcasted_iota(I32, (Q_BLOCK, 2 * Q_BLOCK), 0)
        c_i = lax.broadcasted_iota(I32, (Q_BLOCK, 2 * Q_BLOCK), 1)
        d = jnp.maximum(Q_BLOCK + r_i - c_i, 0)
        max_exact = N_BUCKETS // 2
        df = jnp.maximum(d, 1).astype(F32)
        large = max_exact + (jnp.log(df / max_exact) / math.log(MAX_DISTANCE / max_exact)
                             * (N_BUCKETS - max_exact)).astype(I32)
        large = jnp.minimum(large, N_BUCKETS - 1)
        bkt = jnp.where(d < max_exact, d, large)
        for h in range(DSA_HEADS):
            acc = jnp.zeros((Q_BLOCK, 2 * Q_BLOCK), F32)
            for b in range(N_BUCKETS):
                acc = jnp.where(bkt == b, (tab_ref[b, h] - tab_ref[N_BUCKETS - 1, h]) * LOG2E, acc)
            b_scr[h] = acc

    @pl.when(first_tab[s] == 1)
    def _():
        m_scr[...] = jnp.full_like(m_scr, M_INIT)
        l_scr[...] = jnp.zeros_like(l_scr)
        acc_scr[...] = jnp.zeros_like(acc_scr)

    def run_units(k_ref, v_ref, mask_of, bias_of, width, row0=0):
        nt = width // LANES
        rows = slice(row0, row0 + width)
        mask = mask_of().astype(F32)
        for h in range(DSA_HEADS):
            g = h // DSA_GROUP
            logits = lax.dot_general(q_ref[:, h * DSA_DH:(h + 1) * DSA_DH],
                                     k_ref[rows, g * DSA_DH:(g + 1) * DSA_DH], _NT, preferred_element_type=F32)
            if bias_of is not None:
                logits = logits + bias_of(h)
            s_scr[h, :, 0:width] = logits + mask
        ones = jnp.ones((width, DSA_DH), BF16)
        for h in range(DSA_HEADS):
            g = h // DSA_GROUP
            tiles = [s_scr[h, :, t * LANES:(t + 1) * LANES] for t in range(nt)]
            tmax = tiles[0]
            for t in range(1, nt):
                tmax = jnp.maximum(tmax, tiles[t])
            m_prev = m_scr[h]
            m_new = jnp.maximum(m_prev, jnp.max(tmax, axis=-1, keepdims=True))
            alpha = jnp.exp2(m_prev - m_new)
            p = [jnp.exp2(tiles[t] - m_new).astype(BF16) for t in range(nt)]
            pb = jnp.concatenate(p, axis=-1) if nt > 1 else p[0]
            v_aug = jnp.concatenate([v_ref[rows, g * DSA_DH:(g + 1) * DSA_DH], ones], axis=-1)
            pv = jnp.dot(pb, v_aug, preferred_element_type=F32)
            acc_scr[h] = alpha * acc_scr[h] + pv[:, :DSA_DH]
            l_scr[h] = alpha * l_scr[h] + pv[:, DSA_DH:]
            m_scr[h] = m_new

    @pl.when(kind_tab[s] == 0)
    def _():
        for sub in range(FAR_SUBS):
            run_units(kf_ref, vf_ref, lambda sub=sub: far_ref[0, sub], None, IDX_SUB, sub * IDX_SUB)

    @pl.when(kind_tab[s] == 1)
    def _():
        for half, (k_ref, v_ref) in enumerate(((klo_ref, vlo_ref), (khi_ref, vhi_ref))):
            cols = slice(half * Q_BLOCK, (half + 1) * Q_BLOCK)
            run_units(k_ref, v_ref, lambda cols=cols: near_ref[:, cols],
                      lambda h, cols=cols: b_scr[h][:, cols], Q_BLOCK)
        for h in range(DSA_HEADS):
            o_ref[:, h * DSA_DH:(h + 1) * DSA_DH] = (acc_scr[h] / l_scr[h]).astype(o_ref.dtype)


def _masked_attention(proj, far, near, rel_bias):
    L = proj.shape[0]
    nq = L // Q_BLOCK
    far_keys = FAR_SUBS * IDX_SUB
    per = far_keys // Q_BLOCK
    qi_l, kj_l, kind_l, first_l = [], [], [], []
    for i in range(nq):
        n_far = -(-i // per)
        for j in range(n_far):
            qi_l.append(i); kj_l.append(j); kind_l.append(0); first_l.append(1 if j == 0 else 0)
        qi_l.append(i); kj_l.append(max(n_far - 1, 0)); kind_l.append(1); first_l.append(1 if n_far == 0 else 0)
    tabs = [jnp.asarray(np.asarray(t, np.int32)) for t in (qi_l, kj_l, kind_l, first_l)]
    qw = DSA_HEADS * DSA_DH
    kvw = DSA_KV_HEADS * DSA_DH
    k_col = (2 * qw) // kvw
    v_col = k_col + 1
    hw = DSA_HEADS
    vmem = (2 * (Q_BLOCK * qw * 2 * 2 + 2 * far_keys * kvw * 2 + 4 * Q_BLOCK * kvw * 2
                 + Q_BLOCK * far_keys * 2 + Q_BLOCK * 2 * Q_BLOCK * 2)
            + hw * Q_BLOCK * (3 * LANES + 2 * Q_BLOCK) * 4 + (16 << 20))
    idx = lambda f: (lambda s, qt, kt, kd, ft: f(qt[s], kt[s]))
    grid_spec = pltpu.PrefetchScalarGridSpec(
        num_scalar_prefetch=4,
        grid=(len(qi_l),),
        in_specs=[pl.BlockSpec(memory_space=pltpu.SMEM),
                  pl.BlockSpec((Q_BLOCK, qw), idx(lambda i, j: (i, 0))),
                  pl.BlockSpec((far_keys, kvw), idx(lambda i, j: (j, k_col))),
                  pl.BlockSpec((far_keys, kvw), idx(lambda i, j: (j, v_col))),
                  pl.BlockSpec((Q_BLOCK, kvw), idx(lambda i, j: (jnp.maximum(i - 1, 0), k_col))),
                  pl.BlockSpec((Q_BLOCK, kvw), idx(lambda i, j: (i, k_col))),
                  pl.BlockSpec((Q_BLOCK, kvw), idx(lambda i, j: (jnp.maximum(i - 1, 0), v_col))),
                  pl.BlockSpec((Q_BLOCK, kvw), idx(lambda i, j: (i, v_col))),
                  pl.BlockSpec((1, FAR_SUBS, Q_BLOCK, IDX_SUB), idx(lambda i, j: (i, j, 0, 0))),
                  pl.BlockSpec((Q_BLOCK, 2 * Q_BLOCK), idx(lambda i, j: (i, 0)))],
        out_specs=pl.BlockSpec((Q_BLOCK, qw), idx(lambda i, j: (i, 0))),
        scratch_shapes=[pltpu.VMEM((hw, Q_BLOCK, LANES), F32),
                        pltpu.VMEM((hw, Q_BLOCK, LANES), F32),
                        pltpu.VMEM((hw, Q_BLOCK, DSA_DH), F32),
                        pltpu.VMEM((hw, Q_BLOCK, 2 * Q_BLOCK), F32),
                        pltpu.VMEM((hw, Q_BLOCK, IDX_SUB), F32)],
    )
    return pl.pallas_call(
        _attn_kernel,
        grid_spec=grid_spec,
        out_shape=jax.ShapeDtypeStruct((L, qw), BF16),
        compiler_params=_cparams(("arbitrary",), vmem),
        name="dsa_attention",
    )(*tabs, rel_bias, proj, proj, proj, proj, proj, proj, proj, far, near)


def _dsa_mixer(xb, w_in, ln_g, ln_b, rel_bias):
    L = xb.shape[0]
    k_top = min(TOPK_MAX, L // 4)
    sq = DSA_HEADS * DSA_DH
    skv = DSA_KV_HEADS * DSA_DH
    si = IDX_HEADS * IDX_DIM
    w_q = w_in[:, :sq]
    w_k = w_in[:, sq:sq + skv]
    w_v = w_in[:, sq + skv:sq + 2 * skv]
    w_qi = w_in[:, sq + 2 * skv:sq + 2 * skv + si]
    w_ki = w_in[:, sq + 2 * skv + si:sq + 2 * skv + si + IDX_DIM]
    w_wi = w_in[:, sq + 2 * skv + si + IDX_DIM:]
    w_main = jnp.concatenate([w_q, w_qi, w_k, w_v], axis=1).astype(BF16)
    colscale = jnp.concatenate([jnp.full((1, sq), DSA_DH ** -0.5 * LOG2E, F32),
                                jnp.ones((1, si + 2 * skv), F32)], axis=1)
    w_small = jnp.concatenate([w_ki, w_wi, jnp.zeros((D_MODEL, LANES - IDX_HEADS), F32)], axis=1).astype(BF16)
    proj = _matmul_scaled(xb, w_main, colscale, BF16, tm=1024, tn=1024)
    ki, wi = _dsa_small(xb, w_small, ln_g, ln_b, tm=1024)
    far, near = _idx_select(proj, ki, wi, k_top)
    return _masked_attention(proj, far, near, rel_bias)


def kernel(x, p, gdn_w_in, gdn_conv_w, gdn_a_log, gdn_dt_bias, gdn_norm_g, gdn_w_o, dsa_w_in, dsa_kidx_ln_g, dsa_kidx_ln_b, dsa_w_o, rel_bias, ln1_g, ln1_b, ffn_w_gate, ffn_w_up, ffn_conv_w, ffn_w_down, ln2_g, ln2_b, ple_w_proj, ple_w_gate):
    assert x.shape[0] == 1 and x.shape[2] == D_MODEL
    xf = x[0]
    xb = xf.astype(BF16)
    ia = ib = 0
    for i in range(DEPTH):
        if i % 2 == 0:
            mix = _gdn_mixer(xb, gdn_w_in[ia], gdn_conv_w[ia], gdn_a_log[ia], gdn_dt_bias[ia], gdn_norm_g[ia])
            w_o = gdn_w_o[ia]
            ia += 1
        else:
            mix = _dsa_mixer(xb, dsa_w_in[ib], dsa_kidx_ln_g[ib], dsa_kidx_ln_b[ib], rel_bias)
            w_o = dsa_w_o[ib]
            ib += 1
        xf, xb = _proj_res_ln(mix, w_o.astype(BF16), xf, ln1_g[i], ln1_b[i], tm=512, sub=256)
        hmid = _ffn_up(xb, ffn_w_gate[i].astype(BF16), ffn_w_up[i].astype(BF16), ffn_conv_w[i], tm=1024, tn=512)
        xf, xb = _proj_res_ln(hmid, ffn_w_down[i].astype(BF16), xf, ln2_g[i], ln2_b[i], tm=512, sub=256)
        xf, xb = _ple(xb, xf, ple_w_gate[i].astype(BF16), p[i, 0], ple_w_proj[i].astype(BF16), tm=1024, tn=1024)
    return xf[None]
```

```python
import functools
import math

import jax
import jax.numpy as jnp
import numpy as np
from jax import lax
from jax.experimental import pallas as pl
from jax.experimental.pallas import tpu as pltpu

F32 = jnp.float32
BF16 = jnp.bfloat16
I32 = jnp.int32

D_MODEL = 2048
GDN_QK_HEADS = 16
GDN_V_HEADS = 32
GDN_DK = 128
GDN_DV = 128
GDN_CONV = 4
GDN_CHUNK = 64
GDN_QK_W = GDN_QK_HEADS * GDN_DK
GDN_V_W = GDN_V_HEADS * GDN_DV
DSA_HEADS = 16
DSA_KV_HEADS = 4
DSA_GROUP = DSA_HEADS // DSA_KV_HEADS
DSA_DH = 128
IDX_HEADS = 16
IDX_DIM = 128
TOPK_MAX = 256
N_BUCKETS = 32
MAX_DISTANCE = 128
D_FF = 5120
FFN_CONV = 3
PLE_DIM = 256
DEPTH = 2
DN_ALPHA = (2.0 * DEPTH) ** 0.25
LN_EPS = 1e-5
RMS_EPS = 1e-6

V7X_VMEM_BYTES = 64 * 1024 * 1024
V7X_VMEM_BUDGET = 56 * 1024 * 1024
LANES = 128
BF16_SUBLANES = 16

HALO = BF16_SUBLANES
Q_BLOCK = 128
IDX_KEY_BLOCK = 2048
IDX_SUB = 512
INT_MIN = -(2 ** 31)
FAR_MASKED = -2e30

_NT = (((1,), (1,)), ((), ()))
_TN = (((0,), (0,)), ((), ()))


def _cparams(semantics, vmem_bytes):
    return pltpu.CompilerParams(dimension_semantics=semantics,
                                vmem_limit_bytes=int(min(V7X_VMEM_BUDGET, vmem_bytes)))


def _silu(y):
    return y * jax.nn.sigmoid(y)


def _mm_scale_kernel(x_ref, w_ref, cs_ref, o_ref):
    acc = jnp.dot(x_ref[...], w_ref[...], preferred_element_type=F32)
    o_ref[...] = (acc * cs_ref[...]).astype(o_ref.dtype)


def _matmul_scaled(x, w, colscale, out_dtype, tm, tn):
    M, K = x.shape
    N = w.shape[1]
    tm, tn = min(tm, M), min(tn, N)
    osz = jnp.dtype(out_dtype).itemsize
    vmem = 2 * (tm * K * 2 + K * tn * 2 + tm * tn * osz) + 2 * tm * tn * 4
    return pl.pallas_call(
        _mm_scale_kernel,
        grid=(M // tm, N // tn),
        in_specs=[pl.BlockSpec((tm, K), lambda i, j: (i, 0)),
                  pl.BlockSpec((K, tn), lambda i, j: (0, j)),
                  pl.BlockSpec((1, tn), lambda i, j: (0, j))],
        out_specs=pl.BlockSpec((tm, tn), lambda i, j: (i, j)),
        out_shape=jax.ShapeDtypeStruct((M, N), out_dtype),
        compiler_params=_cparams(("parallel", "parallel"), vmem),
        name="matmul_scaled",
    )(x, w, colscale)


CONV_SUB = 256
FFN_SUB = 512


CONV_ROWS = 1024


def _causal_conv(g, gh, cw_ref, g_scr, kc, tm, cols, r0=0):
    if gh is not None:
        g_scr[0:HALO, cols] = gh
    g_scr[HALO + r0:HALO + r0 + tm, cols] = g
    y = cw_ref[kc - 1:kc, cols] * g
    for j in range(kc - 1):
        off = HALO + r0 - (kc - 1) + j
        y = y + cw_ref[j:j + 1, cols] * g_scr[off:off + tm, cols]
    return y


def _mm_conv_silu_kernel(x_ref, xh_ref, w_ref, cw_ref, o_ref, g_scr, *, kc, tm):
    first = pl.program_id(0) == 0
    rs = min(CONV_ROWS, tm)
    for c0 in range(0, o_ref.shape[1], CONV_SUB):
        cols = slice(c0, c0 + CONV_SUB)
        w = w_ref[:, cols]
        gh = jnp.dot(xh_ref[...], w, preferred_element_type=F32)
        gh = jnp.where(first, 0.0, gh)
        for r0 in range(0, tm, rs):
            g = jnp.dot(x_ref[r0:r0 + rs, :], w, preferred_element_type=F32)
            y = _causal_conv(g, gh if r0 == 0 else None, cw_ref, g_scr, kc, rs, cols, r0)
            o_ref[r0:r0 + rs, cols] = _silu(y).astype(o_ref.dtype)


def _proj_conv_silu(x, w, conv_w, tm, tn):
    M, K = x.shape
    N = w.shape[1]
    kc = conv_w.shape[0]
    tm, tn = min(tm, M), min(tn, N)
    hb = tm // HALO
    vmem = 2 * (tm * K * 2 + HALO * K * 2 + K * tn * 2 + tm * tn * 2) + 4 * tm * tn * 4
    return pl.pallas_call(
        functools.partial(_mm_conv_silu_kernel, kc=kc, tm=tm),
        grid=(M // tm, N // tn),
        in_specs=[pl.BlockSpec((tm, K), lambda i, j: (i, 0)),
                  pl.BlockSpec((HALO, K), lambda i, j: (jnp.maximum(i * hb - 1, 0), 0)),
                  pl.BlockSpec((K, tn), lambda i, j: (0, j)),
                  pl.BlockSpec((kc, tn), lambda i, j: (0, j))],
        out_specs=pl.BlockSpec((tm, tn), lambda i, j: (i, j)),
        out_shape=jax.ShapeDtypeStruct((M, N), BF16),
        scratch_shapes=[pltpu.VMEM((tm + HALO, tn), F32)],
        compiler_params=_cparams(("parallel", "parallel"), vmem),
        name="proj_conv_silu",
    )(x, x, w, conv_w)


def _ffn_up_kernel(x_ref, xh_ref, wg_ref, wu_ref, cw_ref, o_ref, g_scr, *, kc, tm):
    first = pl.program_id(0) == 0
    for c0 in range(0, o_ref.shape[1], FFN_SUB):
        cols = slice(c0, c0 + FFN_SUB)
        wg = wg_ref[:, cols]
        g = jnp.dot(x_ref[...], wg, preferred_element_type=F32)
        gh = jnp.dot(xh_ref[...], wg, preferred_element_type=F32)
        gh = jnp.where(first, 0.0, gh)
        u = jnp.dot(x_ref[...], wu_ref[:, cols], preferred_element_type=F32)
        y = _causal_conv(g, gh, cw_ref, g_scr, kc, tm, cols)
        o_ref[:, cols] = (_silu(y) * u).astype(o_ref.dtype)


def _ffn_up(x, w_gate, w_up, conv_w, tm, tn):
    M, K = x.shape
    N = w_gate.shape[1]
    kc = conv_w.shape[0]
    tm, tn = min(tm, M), min(tn, N)
    hb = tm // HALO
    vmem = 2 * (tm * K * 2 + HALO * K * 2 + 2 * K * tn * 2 + tm * tn * 2) + 6 * tm * tn * 4
    return pl.pallas_call(
        functools.partial(_ffn_up_kernel, kc=kc, tm=tm),
        grid=(M // tm, N // tn),
        in_specs=[pl.BlockSpec((tm, K), lambda i, j: (i, 0)),
                  pl.BlockSpec((HALO, K), lambda i, j: (jnp.maximum(i * hb - 1, 0), 0)),
                  pl.BlockSpec((K, tn), lambda i, j: (0, j)),
                  pl.BlockSpec((K, tn), lambda i, j: (0, j)),
                  pl.BlockSpec((kc, tn), lambda i, j: (0, j))],
        out_specs=pl.BlockSpec((tm, tn), lambda i, j: (i, j)),
        out_shape=jax.ShapeDtypeStruct((M, N), BF16),
        scratch_shapes=[pltpu.VMEM((tm + HALO, tn), F32)],
        compiler_params=_cparams(("parallel", "parallel"), vmem),
        name="ffn_up",
    )(x, x, w_gate, w_up, conv_w)


def _mm_res_ln_kernel(a_ref, w_ref, res_ref, g_ref, b_ref, of_ref, ob_ref, *, sub):
    for r0 in range(0, a_ref.shape[0], sub):
        rows = slice(r0, r0 + sub)
        acc = jnp.dot(a_ref[rows, :], w_ref[...], preferred_element_type=F32)
        y = DN_ALPHA * res_ref[rows, :] + acc
        mu = jnp.mean(y, axis=-1, keepdims=True)
        yc = y - mu
        var = jnp.mean(yc * yc, axis=-1, keepdims=True)
        out = yc * lax.rsqrt(var + LN_EPS) * g_ref[...] + b_ref[...]
        of_ref[rows, :] = out
        ob_ref[rows, :] = out.astype(BF16)


def _proj_res_ln(a, w, res, g, b, tm, sub):
    M, K = a.shape
    N = w.shape[1]
    tm = min(tm, M)
    sub = min(sub, tm)
    vmem = K * N * 2 + 2 * (tm * K * 2 + tm * N * 4 + tm * N * 4 + tm * N * 2) + 4 * sub * N * 4
    return pl.pallas_call(
        functools.partial(_mm_res_ln_kernel, sub=sub),
        grid=(M // tm,),
        in_specs=[pl.BlockSpec((tm, K), lambda i: (i, 0)),
                  pl.BlockSpec((K, N), lambda i: (0, 0), pipeline_mode=pl.Buffered(1)),
                  pl.BlockSpec((tm, N), lambda i: (i, 0)),
                  pl.BlockSpec((1, N), lambda i: (0, 0)),
                  pl.BlockSpec((1, N), lambda i: (0, 0))],
        out_specs=[pl.BlockSpec((tm, N), lambda i: (i, 0)),
                   pl.BlockSpec((tm, N), lambda i: (i, 0))],
        out_shape=[jax.ShapeDtypeStruct((M, N), F32), jax.ShapeDtypeStruct((M, N), BF16)],
        compiler_params=_cparams(("parallel",), vmem),
        name="proj_res_ln",
    )(a, w, res, g.reshape(1, N), b.reshape(1, N))


def _ple_kernel(xb_ref, wg_ref, p_ref, wp_ref, xr_ref, of_ref, ob_ref):
    gate = jax.nn.sigmoid(jnp.dot(xb_ref[...], wg_ref[...], preferred_element_type=F32))
    pe = jnp.dot(p_ref[...].astype(BF16), wp_ref[...], preferred_element_type=F32)
    out = xr_ref[...] + gate * pe
    of_ref[...] = out
    ob_ref[...] = out.astype(BF16)


def _ple(xb, xf, w_gate, p, w_proj, tm, tn):
    M, K = xb.shape
    N = w_gate.shape[1]
    P = p.shape[1]
    tm, tn = min(tm, M), min(tn, N)
    vmem = 2 * (tm * K * 2 + K * tn * 2 + tm * P * 4 + P * tn * 2 + tm * tn * 10) + 4 * tm * tn * 4
    return pl.pallas_call(
        _ple_kernel,
        grid=(M // tm, N // tn),
        in_specs=[pl.BlockSpec((tm, K), lambda i, j: (i, 0)),
                  pl.BlockSpec((K, tn), lambda i, j: (0, j)),
                  pl.BlockSpec((tm, P), lambda i, j: (i, 0)),
                  pl.BlockSpec((P, tn), lambda i, j: (0, j)),
                  pl.BlockSpec((tm, tn), lambda i, j: (i, j))],
        out_specs=[pl.BlockSpec((tm, tn), lambda i, j: (i, j)),
                   pl.BlockSpec((tm, tn), lambda i, j: (i, j))],
        out_shape=[jax.ShapeDtypeStruct((M, N), F32), jax.ShapeDtypeStruct((M, N), BF16)],
        compiler_params=_cparams(("parallel", "parallel"), vmem),
        name="ple",
    )(xb, w_gate, p, w_proj, xf)


GDN_HB = 4
GDN_NC = 4


def _gdn_kernel(q_ref, k_ref, v_ref, z_ref, sc_ref, hp_ref, ng_ref, o_ref, s_scr):
    C = GDN_CHUNK

    @pl.when(pl.program_id(1) == 0)
    def _():
        s_scr[...] = jnp.zeros_like(s_scr)

    row = lax.broadcasted_iota(I32, (C, C), 0)
    col = lax.broadcasted_iota(I32, (C, C), 1)
    tri = row >= col
    strict = row > col
    eye = row == col
    tri_f = tri.astype(F32)
    eye_f = eye.astype(F32)

    raw = sc_ref[...]
    a_log = hp_ref[0, 0:1, :]
    dt_b = hp_ref[0, 1:2, :]
    xs = raw + dt_b
    softplus = jnp.maximum(xs, 0.0) + jnp.log1p(jnp.exp(-jnp.abs(xs)))
    g_all = -jnp.exp(a_log) * softplus
    beta_all = jax.nn.sigmoid(raw)
    ng = ng_ref[...]

    units = [(c, j) for c in range(GDN_NC) for j in range(GDN_HB)]
    kb_l, rhs_l, decay_l, qd_l, kd_l, kbf_l, qbf_l, gl_l = [], [], [], [], [], [], [], []
    for c in range(GDN_NC):
        r0 = c * C
        gc = jnp.dot(tri_f, g_all[r0:r0 + C, :], precision=lax.Precision.HIGHEST,
                     preferred_element_type=F32)
        g_last = gc[C - 1:C, :]
        e_gc = jnp.exp(gc)
        e_rest = jnp.exp(g_last - gc)
        e_last = jnp.exp(g_last)
        beta_c = beta_all[r0:r0 + C, :]
        qn, kn = [], []
        for hq in range(GDN_HB // 2):
            qf = q_ref[r0:r0 + C, hq * GDN_DK:(hq + 1) * GDN_DK].astype(F32)
            kf = k_ref[r0:r0 + C, hq * GDN_DK:(hq + 1) * GDN_DK].astype(F32)
            qn.append(qf * lax.rsqrt(jnp.sum(qf * qf, axis=-1, keepdims=True) + RMS_EPS) * (GDN_DK ** -0.5))
            kn.append(kf * lax.rsqrt(jnp.sum(kf * kf, axis=-1, keepdims=True) + RMS_EPS))
        for j in range(GDN_HB):
            q_h, k_h = qn[j // 2], kn[j // 2]
            vf = v_ref[r0:r0 + C, j * GDN_DV:(j + 1) * GDN_DV].astype(F32)
            beta = beta_c[:, GDN_HB + j:GDN_HB + j + 1]
            kb = k_h * beta
            gcb = jnp.broadcast_to(gc[:, j:j + 1], (C, C))
            gcr = jnp.sum(jnp.where(eye, gcb, 0.0), axis=0, keepdims=True)
            decay_l.append(jnp.where(tri, jnp.exp(jnp.where(tri, gcb - gcr, 0.0)), 0.0))
            kb_l.append(kb.astype(BF16))
            rhs_l.append(jnp.concatenate([vf * beta, kb * e_gc[:, j:j + 1]], axis=-1).astype(BF16))
            qd_l.append((q_h * e_gc[:, j:j + 1]).astype(BF16))
            kd_l.append((k_h * e_rest[:, j:j + 1]).astype(BF16))
            kbf_l.append(k_h.astype(BF16))
            qbf_l.append(q_h.astype(BF16))
            gl_l.append(e_last[:, j:j + 1])

    n_u = len(units)
    kk_l = [lax.dot_general(kb_l[u], kbf_l[u], _NT, preferred_element_type=F32) for u in range(n_u)]
    qk_l = [lax.dot_general(qbf_l[u], kbf_l[u], _NT, preferred_element_type=F32) for u in range(n_u)]
    qk_l = [jnp.where(tri, qk_l[u] * decay_l[u], 0.0).astype(BF16) for u in range(n_u)]
    x_l = [(-jnp.where(strict, kk_l[u] * decay_l[u], 0.0)) for u in range(n_u)]
    t_l = [eye_f + x_l[u] for u in range(n_u)]
    x_l = [x.astype(BF16) for x in x_l]
    for _ in range(5):
        x_l = [jnp.dot(x, x, preferred_element_type=F32).astype(BF16) for x in x_l]
        t_l = [t + jnp.dot(t.astype(BF16), x, preferred_element_type=F32) for t, x in zip(t_l, x_l)]
    sol_l = [jnp.dot(t_l[u].astype(BF16), rhs_l[u], preferred_element_type=F32) for u in range(n_u)]

    s_cur = [s_scr[j] for j in range(GDN_HB)]
    for c in range(GDN_NC):
        r0 = c * C
        us = [c * GDN_HB + j for j in range(GDN_HB)]
        s_bf = [s.astype(BF16) for s in s_cur]
        ws_l = [jnp.dot(sol_l[u][:, GDN_DV:].astype(BF16), s_bf[j], preferred_element_type=F32)
                for j, u in enumerate(us)]
        qs_l = [jnp.dot(qd_l[u], s_bf[j], preferred_element_type=F32) for j, u in enumerate(us)]
        vn_l = [(sol_l[u][:, :GDN_DV] - ws_l[j]).astype(BF16) for j, u in enumerate(us)]
        kv_l = [lax.dot_general(kd_l[u], vn_l[j], _TN, preferred_element_type=F32) for j, u in enumerate(us)]
        ov_l = [jnp.dot(qk_l[u], vn_l[j], preferred_element_type=F32) for j, u in enumerate(us)]
        s_cur = [s_cur[j] * gl_l[u] + kv_l[j] for j, u in enumerate(us)]
        for j in range(GDN_HB):
            o = qs_l[j] + ov_l[j]
            zf = z_ref[r0:r0 + C, j * GDN_DV:(j + 1) * GDN_DV].astype(F32)
            o = o * lax.rsqrt(jnp.mean(o * o, axis=-1, keepdims=True) + RMS_EPS) * ng * _silu(zf)
            o_ref[r0:r0 + C, j * GDN_DV:(j + 1) * GDN_DV] = o.astype(o_ref.dtype)
    for j in range(GDN_HB):
        s_scr[j] = s_cur[j]


def _gdn_core(qkv, z, scal, hparams, norm_g):
    L = qkv.shape[0]
    G = GDN_V_HEADS // GDN_HB
    R = GDN_NC * GDN_CHUNK
    qw = (GDN_HB // 2) * GDN_DK
    vw = GDN_HB * GDN_DV
    k_blk0 = GDN_QK_W // qw
    v_blk0 = 2 * GDN_QK_W // vw
    vmem = 2 * (2 * R * qw * 2 + 2 * R * vw * 2 + R * LANES * 4 + R * vw * 2) + (16 << 20)
    return pl.pallas_call(
        _gdn_kernel,
        grid=(G, L // R),
        in_specs=[pl.BlockSpec((R, qw), lambda g, s: (s, g)),
                  pl.BlockSpec((R, qw), lambda g, s: (s, k_blk0 + g)),
                  pl.BlockSpec((R, vw), lambda g, s: (s, v_blk0 + g)),
                  pl.BlockSpec((R, vw), lambda g, s: (s, g)),
                  pl.BlockSpec((R, LANES), lambda g, s: (s, g)),
                  pl.BlockSpec((1, 8, LANES), lambda g, s: (g, 0, 0)),
                  pl.BlockSpec((1, GDN_DV), lambda g, s: (0, 0))],
        out_specs=pl.BlockSpec((R, vw), lambda g, s: (s, g)),
        out_shape=jax.ShapeDtypeStruct((L, GDN_V_W), BF16),
        scratch_shapes=[pltpu.VMEM((GDN_HB, GDN_DK, GDN_DV), F32)],
        compiler_params=_cparams(("parallel", "arbitrary"), vmem),
        name="gdn_core",
    )(qkv, qkv, qkv, z, scal, hparams, norm_g.reshape(1, GDN_DV))


def _gdn_mixer(xb, w_in, conv_w, a_log, dt_bias, norm_g):
    L = xb.shape[0]
    nqkv = 2 * GDN_QK_W + GDN_V_W
    w_qkv = w_in[:, :nqkv].astype(BF16)
    w_z = w_in[:, nqkv:nqkv + GDN_V_W].astype(BF16)
    w_ab = w_in[:, nqkv + GDN_V_W:]
    G = GDN_V_HEADS // GDN_HB
    w_a = w_ab[:, :GDN_V_HEADS].reshape(D_MODEL, G, GDN_HB)
    w_b = w_ab[:, GDN_V_HEADS:].reshape(D_MODEL, G, GDN_HB)
    w_sc = jnp.concatenate([w_a, w_b, jnp.zeros((D_MODEL, G, LANES - 2 * GDN_HB), F32)], axis=-1)
    w_sc = w_sc.reshape(D_MODEL, G * LANES).astype(BF16)
    qkv = _proj_conv_silu(xb, w_qkv, conv_w, tm=1024, tn=1024)
    ones_z = jnp.ones((1, GDN_V_W), F32)
    z = _matmul_scaled(xb, w_z, ones_z, BF16, tm=1024, tn=1024)
    scal = _matmul_scaled(xb, w_sc, jnp.ones((1, G * LANES), F32), F32, tm=1024, tn=G * LANES)
    hp = jnp.zeros((G, 8, LANES), F32)
    hp = hp.at[:, 0, :GDN_HB].set(a_log.reshape(G, GDN_HB))
    hp = hp.at[:, 1, :GDN_HB].set(dt_bias.reshape(G, GDN_HB))
    return _gdn_core(qkv, z, scal, hp, norm_g)


def _dsa_small_kernel(x_ref, w_ref, g_ref, b_ref, ki_ref, wi_ref):
    acc = jnp.dot(x_ref[...], w_ref[...], preferred_element_type=F32)
    ki = acc[:, :IDX_DIM]
    mu = jnp.mean(ki, axis=-1, keepdims=True)
    kc = ki - mu
    var = jnp.mean(kc * kc, axis=-1, keepdims=True)
    ki_ref[...] = (kc * lax.rsqrt(var + LN_EPS) * g_ref[...] + b_ref[...]).astype(ki_ref.dtype)
    wi_ref[...] = acc[:, IDX_DIM:] * ((IDX_HEADS ** -0.5) * (IDX_DIM ** -0.5))


def _dsa_small(xb, w_small, ln_g, ln_b, tm):
    M, K = xb.shape
    tm = min(tm, M)
    N = 2 * LANES
    vmem = 2 * (tm * K * 2 + K * N * 2 + tm * LANES * 6) + 4 * tm * N * 4
    return pl.pallas_call(
        _dsa_small_kernel,
        grid=(M // tm,),
        in_specs=[pl.BlockSpec((tm, K), lambda i: (i, 0)),
                  pl.BlockSpec((K, N), lambda i: (0, 0)),
                  pl.BlockSpec((1, IDX_DIM), lambda i: (0, 0)),
                  pl.BlockSpec((1, IDX_DIM), lambda i: (0, 0))],
        out_specs=[pl.BlockSpec((tm, IDX_DIM), lambda i: (i, 0)),
                   pl.BlockSpec((tm, LANES), lambda i: (i, 0))],
        out_shape=[jax.ShapeDtypeStruct((M, IDX_DIM), BF16), jax.ShapeDtypeStruct((M, LANES), F32)],
        compiler_params=_cparams(("parallel",), vmem),
        name="dsa_idx_proj",
    )(xb, w_small, ln_g.reshape(1, IDX_DIM), ln_b.reshape(1, IDX_DIM))


def _sortable_key(score):
    bits = lax.bitcast_convert_type(score, I32)
    return jnp.where(bits >= 0, bits, bits ^ jnp.int32(0x7FFFFFFF))


def _idx_kernel(qi_tab, kj_tab, last_tab,
                qidx_ref, kidx_ref, wi_ref, far_ref, near_ref, key_scr, *, k_top, n_sub_total):
    s = pl.program_id(0)
    i = qi_tab[s]
    j = kj_tab[s]
    nsub = IDX_KEY_BLOCK // IDX_SUB
    t_col = i * Q_BLOCK + lax.broadcasted_iota(I32, (Q_BLOCK, 1), 0)
    lane_sub = lax.broadcasted_iota(I32, (Q_BLOCK, IDX_SUB), 1)
    wi = wi_ref[...]

    for sub in range(nsub):
        ki_sub = kidx_ref[sub * IDX_SUB:(sub + 1) * IDX_SUB, :]
        acc = jnp.zeros((Q_BLOCK, IDX_SUB), F32)
        for h in range(IDX_HEADS):
            sc = lax.dot_general(qidx_ref[:, h * IDX_DIM:(h + 1) * IDX_DIM], ki_sub, _NT,
                                 preferred_element_type=F32)
            acc = acc + jnp.maximum(sc, 0.0) * wi[:, h:h + 1]
        s_idx = j * IDX_KEY_BLOCK + sub * IDX_SUB + lane_sub
        key_scr[j * nsub + sub] = jnp.where(s_idx <= t_col, _sortable_key(acc), INT_MIN)

    @pl.when(last_tab[s] == 1)
    def _():
        n_chunks = (i * Q_BLOCK + Q_BLOCK - 1) // IDX_SUB + 1

        def count(pred, ref_val):
            refb = jnp.broadcast_to(ref_val, (Q_BLOCK, LANES))

            def body(c, cnt):
                blk = key_scr[c]
                for l in range(IDX_SUB // LANES):
                    cnt = cnt + jnp.where(pred(blk[:, l * LANES:(l + 1) * LANES], refb), 1, 0)
                return cnt

            cnt = lax.fori_loop(0, n_chunks, body, jnp.zeros((Q_BLOCK, LANES), I32))
            return jnp.sum(cnt, axis=1, keepdims=True)

        def bit_body(b, carry):
            thr, n_ge = carry
            cand = thr + jnp.left_shift(jnp.int32(1), 31 - b)
            n_cand = count(lambda a, r: a >= r, cand)
            ok = n_cand >= k_top
            return jnp.where(ok, cand, thr), jnp.where(ok, n_cand, n_ge)

        thr, n_ge = lax.fori_loop(0, 32, bit_body,
                                  (jnp.full((Q_BLOCK, 1), INT_MIN, I32),
                                   jnp.broadcast_to(n_chunks * IDX_SUB, (Q_BLOCK, 1)).astype(I32)))
        thr_b = jnp.broadcast_to(thr, (Q_BLOCK, IDX_SUB))
        n_tied_rows = jnp.max(jnp.where(n_ge != k_top, 1, 0))

        def emit(c, sel):
            s_idx = c * IDX_SUB + lane_sub
            far = sel & (t_col - s_idx >= MAX_DISTANCE)
            far_ref[0, c] = jnp.where(far, 0.0, FAR_MASKED).astype(far_ref.dtype)
            key_scr[c] = jnp.where(sel, 1, 0)

        @pl.when(n_tied_rows == 0)
        def _():
            def sel_body(c, carry):
                emit(c, key_scr[c] >= thr_b)
                return carry

            lax.fori_loop(0, n_chunks, sel_body, 0)

        @pl.when(n_tied_rows != 0)
        def _():
            n_gt = count(lambda a, r: a > r, thr)
            need_eq = (k_top - n_gt).astype(F32)
            incl = (lax.broadcasted_iota(I32, (IDX_SUB, IDX_SUB), 0)
                    <= lax.broadcasted_iota(I32, (IDX_SUB, IDX_SUB), 1)).astype(BF16)

            def sel_body(c, carry):
                blk = key_scr[c]
                eq = blk == thr_b
                eq_f = jnp.where(eq, 1.0, 0.0)
                rank = carry + jnp.dot(eq_f.astype(BF16), incl, preferred_element_type=F32)
                s_idx = c * IDX_SUB + lane_sub
                emit(c, ((blk > thr_b) | (eq & (rank <= need_eq))) & (s_idx <= t_col))
                return carry + jnp.sum(eq_f, axis=1, keepdims=True)

            lax.fori_loop(0, n_chunks, sel_body, jnp.zeros((Q_BLOCK, 1), F32))

        def fill_body(c, carry):
            far_ref[0, c] = jnp.full((Q_BLOCK, IDX_SUB), FAR_MASKED, far_ref.dtype)
            return carry

        lax.fori_loop(n_chunks, n_sub_total, fill_body, 0)

        def window(blk_idx):
            per = IDX_SUB // Q_BLOCK
            chunk = key_scr[blk_idx // per]
            m = blk_idx % per
            out = chunk[:, 0:Q_BLOCK]
            for q in range(1, per):
                out = jnp.where(m == q, chunk[:, q * Q_BLOCK:(q + 1) * Q_BLOCK], out)
            return out

        r_i = lax.broadcasted_iota(I32, (Q_BLOCK, Q_BLOCK), 0)
        c_i = lax.broadcasted_iota(I32, (Q_BLOCK, Q_BLOCK), 1)
        d_lo = Q_BLOCK + r_i - c_i
        d_hi = r_i - c_i
        near_lo = (window(jnp.maximum(i - 1, 0)) != 0) & (d_lo < MAX_DISTANCE) & (i >= 1)
        near_hi = (window(i) != 0) & (d_hi >= 0) & (d_hi < MAX_DISTANCE)
        near_ref[:, 0:Q_BLOCK] = jnp.where(near_lo, 0.0, -jnp.inf).astype(near_ref.dtype)
        near_ref[:, Q_BLOCK:2 * Q_BLOCK] = jnp.where(near_hi, 0.0, -jnp.inf).astype(near_ref.dtype)


def _idx_select(proj, ki, wi, k_top):
    L = ki.shape[0]
    nq = L // Q_BLOCK
    n_sub_total = L // IDX_SUB
    qi_l, kj_l, last_l = [], [], []
    for i in range(nq):
        j_last = (i * Q_BLOCK + Q_BLOCK - 1) // IDX_KEY_BLOCK
        for j in range(j_last + 1):
            qi_l.append(i)
            kj_l.append(j)
            last_l.append(1 if j == j_last else 0)
    tabs = [jnp.asarray(np.asarray(t, np.int32)) for t in (qi_l, kj_l, last_l)]
    qcol = (DSA_HEADS * DSA_DH) // (IDX_HEADS * IDX_DIM)
    vmem = (L * Q_BLOCK * 4 + 2 * (Q_BLOCK * L * 2) + 2 * (Q_BLOCK * IDX_HEADS * IDX_DIM * 2)
            + 2 * IDX_KEY_BLOCK * IDX_DIM * 2 + (8 << 20))
    grid_spec = pltpu.PrefetchScalarGridSpec(
        num_scalar_prefetch=3,
        grid=(len(qi_l),),
        in_specs=[pl.BlockSpec((Q_BLOCK, IDX_HEADS * IDX_DIM), lambda s, qt, kt, lt: (qt[s], qcol)),
                  pl.BlockSpec((IDX_KEY_BLOCK, IDX_DIM), lambda s, qt, kt, lt: (kt[s], 0)),
                  pl.BlockSpec((Q_BLOCK, LANES), lambda s, qt, kt, lt: (qt[s], 0))],
        out_specs=[pl.BlockSpec((1, n_sub_total, Q_BLOCK, IDX_SUB), lambda s, qt, kt, lt: (qt[s], 0, 0, 0)),
                   pl.BlockSpec((Q_BLOCK, 2 * Q_BLOCK), lambda s, qt, kt, lt: (qt[s], 0))],
        scratch_shapes=[pltpu.VMEM((n_sub_total, Q_BLOCK, IDX_SUB), I32)],
    )
    return pl.pallas_call(
        functools.partial(_idx_kernel, k_top=k_top, n_sub_total=n_sub_total),
        grid_spec=grid_spec,
        out_shape=[jax.ShapeDtypeStruct((nq, n_sub_total, Q_BLOCK, IDX_SUB), BF16),
                   jax.ShapeDtypeStruct((L, 2 * Q_BLOCK), BF16)],
        compiler_params=_cparams(("arbitrary",), vmem),
        name="dsa_idx_select",
    )(*tabs, proj, ki, wi)


M_INIT = -1e30
FAR_SUBS = 2
LOG2E = math.log2(math.e)


def _attn_kernel(qi_tab, kj_tab, kind_tab, first_tab,
                 tab_ref, q_ref, kt_ref, vf_ref, klo_ref, khi_ref, vlo_ref, vhi_ref, far_ref, near_ref,
                 o_ref, m_scr, l_scr, acc_scr, b_scr, s_scr):
    s = pl.program_id(0)

    @pl.when(s == 0)
    def _():
        r_i = lax.broadcasted_iota(I32, (Q_BLOCK, 2 * Q_BLOCK), 0)
        c_i = lax.broadcasted_iota(I32, (Q_BLOCK, 2 * Q_BLOCK), 1)
        d = jnp.maximum(Q_BLOCK + r_i - c_i, 0)
        max_exact = N_BUCKETS // 2
        df = jnp.maximum(d, 1).astype(F32)
        large = max_exact + (jnp.log(df / max_exact) / math.log(MAX_DISTANCE / max_exact)
                             * (N_BUCKETS - max_exact)).astype(I32)
        large = jnp.minimum(large, N_BUCKETS - 1)
        bkt = jnp.where(d < max_exact, d, large)
        for h in range(DSA_HEADS):
            acc = jnp.zeros((Q_BLOCK, 2 * Q_BLOCK), F32)
            for b in range(N_BUCKETS):
                acc = jnp.where(bkt == b, (tab_ref[b, h] - tab_ref[N_BUCKETS - 1, h]) * LOG2E, acc)
            b_scr[h] = acc

    @pl.when(first_tab[s] == 1)
    def _():
        m_scr[...] = jnp.full_like(m_scr, M_INIT)
        l_scr[...] = jnp.zeros_like(l_scr)
        acc_scr[...] = jnp.zeros_like(acc_scr)

    def softmax_pv(v_ref, width, row0=0):
        nt = width // LANES
        rows = slice(row0, row0 + width)
        ones = jnp.ones((width, DSA_DH), BF16)
        for h in range(DSA_HEADS):
            g = h // DSA_GROUP
            tiles = [s_scr[h, :, t * LANES:(t + 1) * LANES] for t in range(nt)]
            tmax = tiles[0]
            for t in range(1, nt):
                tmax = jnp.maximum(tmax, tiles[t])
            m_prev = m_scr[h]
            m_new = jnp.maximum(m_prev, jnp.max(tmax, axis=-1, keepdims=True))
            alpha = jnp.exp2(m_prev - m_new)
            p = [jnp.exp2(tiles[t] - m_new).astype(BF16) for t in range(nt)]
            pb = jnp.concatenate(p, axis=-1) if nt > 1 else p[0]
            v_aug = jnp.concatenate([v_ref[rows, g * DSA_DH:(g + 1) * DSA_DH], ones], axis=-1)
            pv = jnp.dot(pb, v_aug, preferred_element_type=F32)
            acc_scr[h] = alpha * acc_scr[h] + pv[:, :DSA_DH]
            l_scr[h] = alpha * l_scr[h] + pv[:, DSA_DH:]
            m_scr[h] = m_new

    @pl.when(kind_tab[s] == 0)
    def _():
        eye = (lax.broadcasted_iota(I32, (Q_BLOCK, Q_BLOCK), 0)
               == lax.broadcasted_iota(I32, (Q_BLOCK, Q_BLOCK), 1)).astype(BF16)
        for sub in range(FAR_SUBS):
            cols = slice(sub * IDX_SUB, (sub + 1) * IDX_SUB)
            mask = far_ref[0, sub]
            for g in range(DSA_KV_HEADS):
                lhs = jnp.concatenate(
                    [jnp.concatenate([q_ref[:, h * DSA_DH:(h + 1) * DSA_DH], eye], axis=1)
                     for h in range(g * DSA_GROUP, (g + 1) * DSA_GROUP)], axis=0)
                rhs = jnp.concatenate([kt_ref[g * DSA_DH:(g + 1) * DSA_DH, cols], mask], axis=0)
                logits = jnp.dot(lhs, rhs, preferred_element_type=F32)
                for hh in range(DSA_GROUP):
                    s_scr[g * DSA_GROUP + hh] = logits[hh * Q_BLOCK:(hh + 1) * Q_BLOCK, :]
            softmax_pv(vf_ref, IDX_SUB, sub * IDX_SUB)

    @pl.when(kind_tab[s] == 1)
    def _():
        for half, (k_ref, v_ref) in enumerate(((klo_ref, vlo_ref), (khi_ref, vhi_ref))):
            cols = slice(half * Q_BLOCK, (half + 1) * Q_BLOCK)
            mask = near_ref[:, cols].astype(F32)
            for h in range(DSA_HEADS):
                g = h // DSA_GROUP
                logits = lax.dot_general(q_ref[:, h * DSA_DH:(h + 1) * DSA_DH],
                                         k_ref[:, g * DSA_DH:(g + 1) * DSA_DH], _NT, preferred_element_type=F32)
                s_scr[h, :, 0:Q_BLOCK] = logits + b_scr[h][:, cols] + mask
            softmax_pv(v_ref, Q_BLOCK)
        for h in range(DSA_HEADS):
            o_ref[:, h * DSA_DH:(h + 1) * DSA_DH] = (acc_scr[h] / l_scr[h]).astype(o_ref.dtype)


def _masked_attention(proj, far, near, rel_bias):
    L = proj.shape[0]
    nq = L // Q_BLOCK
    far_keys = FAR_SUBS * IDX_SUB
    per = far_keys // Q_BLOCK
    qi_l, kj_l, kind_l, first_l = [], [], [], []
    for i in range(nq):
        n_far = -(-i // per)
        for j in range(n_far):
            qi_l.append(i); kj_l.append(j); kind_l.append(0); first_l.append(1 if j == 0 else 0)
        qi_l.append(i); kj_l.append(max(n_far - 1, 0)); kind_l.append(1); first_l.append(1 if n_far == 0 else 0)
    tabs = [jnp.asarray(np.asarray(t, np.int32)) for t in (qi_l, kj_l, kind_l, first_l)]
    qw = DSA_HEADS * DSA_DH
    kvw = DSA_KV_HEADS * DSA_DH
    k_col = (2 * qw) // kvw
    v_col = k_col + 1
    hw = DSA_HEADS
    vmem = (2 * (Q_BLOCK * qw * 2 * 2 + 2 * far_keys * kvw * 2 + 4 * Q_BLOCK * kvw * 2
                 + Q_BLOCK * far_keys * 2 + Q_BLOCK * 2 * Q_BLOCK * 2)
            + hw * Q_BLOCK * (3 * LANES + 2 * Q_BLOCK) * 4 + (16 << 20))
    idx = lambda f: (lambda s, qt, kt, kd, ft: f(qt[s], kt[s]))
    k_t = proj[:, k_col * kvw:(k_col + 1) * kvw].T
    grid_spec = pltpu.PrefetchScalarGridSpec(
        num_scalar_prefetch=4,
        grid=(len(qi_l),),
        in_specs=[pl.BlockSpec(memory_space=pltpu.SMEM),
                  pl.BlockSpec((Q_BLOCK, qw), idx(lambda i, j: (i, 0))),
                  pl.BlockSpec((kvw, far_keys), idx(lambda i, j: (0, j))),
                  pl.BlockSpec((far_keys, kvw), idx(lambda i, j: (j, v_col))),
                  pl.BlockSpec((Q_BLOCK, kvw), idx(lambda i, j: (jnp.maximum(i - 1, 0), k_col))),
                  pl.BlockSpec((Q_BLOCK, kvw), idx(lambda i, j: (i, k_col))),
                  pl.BlockSpec((Q_BLOCK, kvw), idx(lambda i, j: (jnp.maximum(i - 1, 0), v_col))),
                  pl.BlockSpec((Q_BLOCK, kvw), idx(lambda i, j: (i, v_col))),
                  pl.BlockSpec((1, FAR_SUBS, Q_BLOCK, IDX_SUB), idx(lambda i, j: (i, j, 0, 0))),
                  pl.BlockSpec((Q_BLOCK, 2 * Q_BLOCK), idx(lambda i, j: (i, 0)))],
        out_specs=pl.BlockSpec((Q_BLOCK, qw), idx(lambda i, j: (i, 0))),
        scratch_shapes=[pltpu.VMEM((hw, Q_BLOCK, LANES), F32),
                        pltpu.VMEM((hw, Q_BLOCK, LANES), F32),
                        pltpu.VMEM((hw, Q_BLOCK, DSA_DH), F32),
                        pltpu.VMEM((hw, Q_BLOCK, 2 * Q_BLOCK), F32),
                        pltpu.VMEM((hw, Q_BLOCK, IDX_SUB), F32)],
    )
    return pl.pallas_call(
        _attn_kernel,
        grid_spec=grid_spec,
        out_shape=jax.ShapeDtypeStruct((L, qw), BF16),
        compiler_params=_cparams(("arbitrary",), vmem),
        name="dsa_attention",
    )(*tabs, rel_bias, proj, k_t, proj, proj, proj, proj, proj, far, near)


def _dsa_mixer(xb, w_in, ln_g, ln_b, rel_bias):
    L = xb.shape[0]
    k_top = min(TOPK_MAX, L // 4)
    sq = DSA_HEADS * DSA_DH
    skv = DSA_KV_HEADS * DSA_DH
    si = IDX_HEADS * IDX_DIM
    w_q = w_in[:, :sq]
    w_k = w_in[:, sq:sq + skv]
    w_v = w_in[:, sq + skv:sq + 2 * skv]
    w_qi = w_in[:, sq + 2 * skv:sq + 2 * skv + si]
    w_ki = w_in[:, sq + 2 * skv + si:sq + 2 * skv + si + IDX_DIM]
    w_wi = w_in[:, sq + 2 * skv + si + IDX_DIM:]
    w_main = jnp.concatenate([w_q, w_qi, w_k, w_v], axis=1).astype(BF16)
    colscale = jnp.concatenate([jnp.full((1, sq), DSA_DH ** -0.5 * LOG2E, F32),
                                jnp.ones((1, si + 2 * skv), F32)], axis=1)
    w_small = jnp.concatenate([w_ki, w_wi, jnp.zeros((D_MODEL, LANES - IDX_HEADS), F32)], axis=1).astype(BF16)
    proj = _matmul_scaled(xb, w_main, colscale, BF16, tm=1024, tn=1024)
    ki, wi = _dsa_small(xb, w_small, ln_g, ln_b, tm=1024)
    far, near = _idx_select(proj, ki, wi, k_top)
    return _masked_attention(proj, far, near, rel_bias)


def kernel(x, p, gdn_w_in, gdn_conv_w, gdn_a_log, gdn_dt_bias, gdn_norm_g, gdn_w_o, dsa_w_in, dsa_kidx_ln_g, dsa_kidx_ln_b, dsa_w_o, rel_bias, ln1_g, ln1_b, ffn_w_gate, ffn_w_up, ffn_conv_w, ffn_w_down, ln2_g, ln2_b, ple_w_proj, ple_w_gate):
    assert x.shape[0] == 1 and x.shape[2] == D_MODEL
    xf = x[0]
    xb = xf.astype(BF16)
    ia = ib = 0
    for i in range(DEPTH):
        if i % 2 == 0:
            mix = _gdn_mixer(xb, gdn_w_in[ia], gdn_conv_w[ia], gdn_a_log[ia], gdn_dt_bias[ia], gdn_norm_g[ia])
            w_o = gdn_w_o[ia]
            ia += 1
        else:
            mix = _dsa_mixer(xb, dsa_w_in[ib], dsa_kidx_ln_g[ib], dsa_kidx_ln_b[ib], rel_bias)
            w_o = dsa_w_o[ib]
            ib += 1
        xf, xb = _proj_res_ln(mix, w_o.astype(BF16), xf, ln1_g[i], ln1_b[i], tm=512, sub=256)
        hmid = _ffn_up(xb, ffn_w_gate[i].astype(BF16), ffn_w_up[i].astype(BF16), ffn_conv_w[i], tm=1024, tn=512)
        xf, xb = _proj_res_ln(hmid, ffn_w_down[i].astype(BF16), xf, ln2_g[i], ln2_b[i], tm=512, sub=256)
        xf, xb = _ple(xb, xf, ple_w_gate[i].astype(BF16), p[i, 0], ple_w_proj[i].astype(BF16), tm=1024, tn=1024)
    return xf[None]
```

```python
import functools
import math

import jax
import jax.numpy as jnp
import numpy as np
from jax import lax
from jax.experimental import pallas as pl
from jax.experimental.pallas import tpu as pltpu

F32 = jnp.float32
BF16 = jnp.bfloat16
I32 = jnp.int32

D_MODEL = 2048
GDN_QK_HEADS = 16
GDN_V_HEADS = 32
GDN_DK = 128
GDN_DV = 128
GDN_CONV = 4
GDN_CHUNK = 64
GDN_QK_W = GDN_QK_HEADS * GDN_DK
GDN_V_W = GDN_V_HEADS * GDN_DV
DSA_HEADS = 16
DSA_KV_HEADS = 4
DSA_GROUP = DSA_HEADS // DSA_KV_HEADS
DSA_DH = 128
IDX_HEADS = 16
IDX_DIM = 128
TOPK_MAX = 256
N_BUCKETS = 32
MAX_DISTANCE = 128
D_FF = 5120
FFN_CONV = 3
PLE_DIM = 256
DEPTH = 2
DN_ALPHA = (2.0 * DEPTH) ** 0.25
LN_EPS = 1e-5
RMS_EPS = 1e-6

V7X_VMEM_BYTES = 64 * 1024 * 1024
V7X_VMEM_BUDGET = 56 * 1024 * 1024
LANES = 128
BF16_SUBLANES = 16

HALO = BF16_SUBLANES
Q_BLOCK = 128
IDX_KEY_BLOCK = 2048
IDX_SUB = 512
INT_MIN = -(2 ** 31)
BITS_PER_CHECK = 4
FAR_MASKED = -2e30

_NT = (((1,), (1,)), ((), ()))
_TN = (((0,), (0,)), ((), ()))


def _cparams(semantics, vmem_bytes):
    return pltpu.CompilerParams(dimension_semantics=semantics,
                                vmem_limit_bytes=int(min(V7X_VMEM_BUDGET, vmem_bytes)))


def _silu(y):
    return y * jax.nn.sigmoid(y)


def _mm_scale_kernel(x_ref, w_ref, cs_ref, o_ref):
    acc = jnp.dot(x_ref[...], w_ref[...], preferred_element_type=F32)
    o_ref[...] = (acc * cs_ref[...]).astype(o_ref.dtype)


def _matmul_scaled(x, w, colscale, out_dtype, tm, tn):
    M, K = x.shape
    N = w.shape[1]
    tm, tn = min(tm, M), min(tn, N)
    osz = jnp.dtype(out_dtype).itemsize
    vmem = 2 * (tm * K * 2 + K * tn * 2 + tm * tn * osz) + 2 * tm * tn * 4
    return pl.pallas_call(
        _mm_scale_kernel,
        grid=(M // tm, N // tn),
        in_specs=[pl.BlockSpec((tm, K), lambda i, j: (i, 0)),
                  pl.BlockSpec((K, tn), lambda i, j: (0, j)),
                  pl.BlockSpec((1, tn), lambda i, j: (0, j))],
        out_specs=pl.BlockSpec((tm, tn), lambda i, j: (i, j)),
        out_shape=jax.ShapeDtypeStruct((M, N), out_dtype),
        compiler_params=_cparams(("parallel", "parallel"), vmem),
        name="matmul_scaled",
    )(x, w, colscale)


CONV_SUB = 256
FFN_SUB = 512


CONV_ROWS = 1024


def _causal_conv(g, gh, cw_ref, g_scr, kc, tm, cols, r0=0):
    if gh is not None:
        g_scr[0:HALO, cols] = gh
    g_scr[HALO + r0:HALO + r0 + tm, cols] = g
    y = cw_ref[kc - 1:kc, cols] * g
    for j in range(kc - 1):
        off = HALO + r0 - (kc - 1) + j
        y = y + cw_ref[j:j + 1, cols] * g_scr[off:off + tm, cols]
    return y


def _mm_conv_silu_kernel(x_ref, xh_ref, w_ref, cw_ref, o_ref, g_scr, *, kc, tm):
    first = pl.program_id(0) == 0
    rs = min(CONV_ROWS, tm)
    for c0 in range(0, o_ref.shape[1], CONV_SUB):
        cols = slice(c0, c0 + CONV_SUB)
        w = w_ref[:, cols]
        gh = jnp.dot(xh_ref[...], w, preferred_element_type=F32)
        gh = jnp.where(first, 0.0, gh)
        for r0 in range(0, tm, rs):
            g = jnp.dot(x_ref[r0:r0 + rs, :], w, preferred_element_type=F32)
            y = _causal_conv(g, gh if r0 == 0 else None, cw_ref, g_scr, kc, rs, cols, r0)
            o_ref[r0:r0 + rs, cols] = _silu(y).astype(o_ref.dtype)


def _proj_conv_silu(x, w, conv_w, tm, tn):
    M, K = x.shape
    N = w.shape[1]
    kc = conv_w.shape[0]
    tm, tn = min(tm, M), min(tn, N)
    hb = tm // HALO
    vmem = 2 * (tm * K * 2 + HALO * K * 2 + K * tn * 2 + tm * tn * 2) + 4 * tm * tn * 4
    return pl.pallas_call(
        functools.partial(_mm_conv_silu_kernel, kc=kc, tm=tm),
        grid=(M // tm, N // tn),
        in_specs=[pl.BlockSpec((tm, K), lambda i, j: (i, 0)),
                  pl.BlockSpec((HALO, K), lambda i, j: (jnp.maximum(i * hb - 1, 0), 0)),
                  pl.BlockSpec((K, tn), lambda i, j: (0, j)),
                  pl.BlockSpec((kc, tn), lambda i, j: (0, j))],
        out_specs=pl.BlockSpec((tm, tn), lambda i, j: (i, j)),
        out_shape=jax.ShapeDtypeStruct((M, N), BF16),
        scratch_shapes=[pltpu.VMEM((tm + HALO, tn), F32)],
        compiler_params=_cparams(("parallel", "parallel"), vmem),
        name="proj_conv_silu",
    )(x, x, w, conv_w)


def _ffn_up_kernel(x_ref, xh_ref, wg_ref, wu_ref, cw_ref, o_ref, g_scr, *, kc, tm):
    first = pl.program_id(0) == 0
    for c0 in range(0, o_ref.shape[1], FFN_SUB):
        cols = slice(c0, c0 + FFN_SUB)
        wg = wg_ref[:, cols]
        g = jnp.dot(x_ref[...], wg, preferred_element_type=F32)
        gh = jnp.dot(xh_ref[...], wg, preferred_element_type=F32)
        gh = jnp.where(first, 0.0, gh)
        u = jnp.dot(x_ref[...], wu_ref[:, cols], preferred_element_type=F32)
        y = _causal_conv(g, gh, cw_ref, g_scr, kc, tm, cols)
        o_ref[:, cols] = (_silu(y) * u).astype(o_ref.dtype)


def _ffn_up(x, w_gate, w_up, conv_w, tm, tn):
    M, K = x.shape
    N = w_gate.shape[1]
    kc = conv_w.shape[0]
    tm, tn = min(tm, M), min(tn, N)
    hb = tm // HALO
    vmem = 2 * (tm * K * 2 + HALO * K * 2 + 2 * K * tn * 2 + tm * tn * 2) + 6 * tm * tn * 4
    return pl.pallas_call(
        functools.partial(_ffn_up_kernel, kc=kc, tm=tm),
        grid=(M // tm, N // tn),
        in_specs=[pl.BlockSpec((tm, K), lambda i, j: (i, 0)),
                  pl.BlockSpec((HALO, K), lambda i, j: (jnp.maximum(i * hb - 1, 0), 0)),
                  pl.BlockSpec((K, tn), lambda i, j: (0, j)),
                  pl.BlockSpec((K, tn), lambda i, j: (0, j)),
                  pl.BlockSpec((kc, tn), lambda i, j: (0, j))],
        out_specs=pl.BlockSpec((tm, tn), lambda i, j: (i, j)),
        out_shape=jax.ShapeDtypeStruct((M, N), BF16),
        scratch_shapes=[pltpu.VMEM((tm + HALO, tn), F32)],
        compiler_params=_cparams(("parallel", "parallel"), vmem),
        name="ffn_up",
    )(x, x, w_gate, w_up, conv_w)


def _mm_res_ln_kernel(a_ref, w_ref, res_ref, g_ref, b_ref, of_ref, ob_ref, *, sub):
    for r0 in range(0, a_ref.shape[0], sub):
        rows = slice(r0, r0 + sub)
        acc = jnp.dot(a_ref[rows, :], w_ref[...], preferred_element_type=F32)
        y = DN_ALPHA * res_ref[rows, :] + acc
        mu = jnp.mean(y, axis=-1, keepdims=True)
        yc = y - mu
        var = jnp.mean(yc * yc, axis=-1, keepdims=True)
        out = yc * lax.rsqrt(var + LN_EPS) * g_ref[...] + b_ref[...]
        of_ref[rows, :] = out
        ob_ref[rows, :] = out.astype(BF16)


def _proj_res_ln(a, w, res, g, b, tm, sub):
    M, K = a.shape
    N = w.shape[1]
    tm = min(tm, M)
    sub = min(sub, tm)
    vmem = K * N * 2 + 2 * (tm * K * 2 + tm * N * 4 + tm * N * 4 + tm * N * 2) + 4 * sub * N * 4
    return pl.pallas_call(
        functools.partial(_mm_res_ln_kernel, sub=sub),
        grid=(M // tm,),
        in_specs=[pl.BlockSpec((tm, K), lambda i: (i, 0)),
                  pl.BlockSpec((K, N), lambda i: (0, 0), pipeline_mode=pl.Buffered(1)),
                  pl.BlockSpec((tm, N), lambda i: (i, 0)),
                  pl.BlockSpec((1, N), lambda i: (0, 0)),
                  pl.BlockSpec((1, N), lambda i: (0, 0))],
        out_specs=[pl.BlockSpec((tm, N), lambda i: (i, 0)),
                   pl.BlockSpec((tm, N), lambda i: (i, 0))],
        out_shape=[jax.ShapeDtypeStruct((M, N), F32), jax.ShapeDtypeStruct((M, N), BF16)],
        compiler_params=_cparams(("parallel",), vmem),
        name="proj_res_ln",
    )(a, w, res, g.reshape(1, N), b.reshape(1, N))


def _ple_kernel(xb_ref, wg_ref, p_ref, wp_ref, xr_ref, of_ref, ob_ref):
    gate = jax.nn.sigmoid(jnp.dot(xb_ref[...], wg_ref[...], preferred_element_type=F32))
    pe = jnp.dot(p_ref[...].astype(BF16), wp_ref[...], preferred_element_type=F32)
    out = xr_ref[...] + gate * pe
    of_ref[...] = out
    ob_ref[...] = out.astype(BF16)


def _ple(xb, xf, w_gate, p, w_proj, tm, tn):
    M, K = xb.shape
    N = w_gate.shape[1]
    P = p.shape[1]
    tm, tn = min(tm, M), min(tn, N)
    vmem = 2 * (tm * K * 2 + K * tn * 2 + tm * P * 4 + P * tn * 2 + tm * tn * 10) + 4 * tm * tn * 4
    return pl.pallas_call(
        _ple_kernel,
        grid=(M // tm, N // tn),
        in_specs=[pl.BlockSpec((tm, K), lambda i, j: (i, 0)),
                  pl.BlockSpec((K, tn), lambda i, j: (0, j)),
                  pl.BlockSpec((tm, P), lambda i, j: (i, 0)),
                  pl.BlockSpec((P, tn), lambda i, j: (0, j)),
                  pl.BlockSpec((tm, tn), lambda i, j: (i, j))],
        out_specs=[pl.BlockSpec((tm, tn), lambda i, j: (i, j)),
                   pl.BlockSpec((tm, tn), lambda i, j: (i, j))],
        out_shape=[jax.ShapeDtypeStruct((M, N), F32), jax.ShapeDtypeStruct((M, N), BF16)],
        compiler_params=_cparams(("parallel", "parallel"), vmem),
        name="ple",
    )(xb, w_gate, p, w_proj, xf)


GDN_HB = 8
GDN_NC = 2


def _gdn_kernel(q_ref, k_ref, v_ref, z_ref, sc_ref, hp_ref, ng_ref, o_ref, s_scr):
    C = GDN_CHUNK

    @pl.when(pl.program_id(1) == 0)
    def _():
        s_scr[...] = jnp.zeros_like(s_scr)

    row = lax.broadcasted_iota(I32, (C, C), 0)
    col = lax.broadcasted_iota(I32, (C, C), 1)
    tri = row >= col
    strict = row > col
    eye = row == col
    tri_f = tri.astype(F32)
    eye_f = eye.astype(F32)

    raw = sc_ref[...]
    a_log = hp_ref[0, 0:1, :]
    dt_b = hp_ref[0, 1:2, :]
    xs = raw + dt_b
    softplus = jnp.maximum(xs, 0.0) + jnp.log1p(jnp.exp(-jnp.abs(xs)))
    g_all = -jnp.exp(a_log) * softplus
    beta_all = jax.nn.sigmoid(raw)
    ng = ng_ref[...]

    units = [(c, j) for c in range(GDN_NC) for j in range(GDN_HB)]
    kb_l, rhs_l, decay_l, qd_l, kd_l, kbf_l, qbf_l, gl_l = [], [], [], [], [], [], [], []
    for c in range(GDN_NC):
        r0 = c * C
        gc = jnp.dot(tri_f, g_all[r0:r0 + C, :], precision=lax.Precision.HIGHEST,
                     preferred_element_type=F32)
        g_last = gc[C - 1:C, :]
        e_gc = jnp.exp(gc)
        e_rest = jnp.exp(g_last - gc)
        e_last = jnp.exp(g_last)
        beta_c = beta_all[r0:r0 + C, :]
        qn, kn = [], []
        for hq in range(GDN_HB // 2):
            qf = q_ref[r0:r0 + C, hq * GDN_DK:(hq + 1) * GDN_DK].astype(F32)
            kf = k_ref[r0:r0 + C, hq * GDN_DK:(hq + 1) * GDN_DK].astype(F32)
            qn.append(qf * lax.rsqrt(jnp.sum(qf * qf, axis=-1, keepdims=True) + RMS_EPS) * (GDN_DK ** -0.5))
            kn.append(kf * lax.rsqrt(jnp.sum(kf * kf, axis=-1, keepdims=True) + RMS_EPS))
        for j in range(GDN_HB):
            q_h, k_h = qn[j // 2], kn[j // 2]
            vf = v_ref[r0:r0 + C, j * GDN_DV:(j + 1) * GDN_DV].astype(F32)
            beta = beta_c[:, GDN_HB + j:GDN_HB + j + 1]
            kb = k_h * beta
            gcb = jnp.broadcast_to(gc[:, j:j + 1], (C, C))
            gcr = jnp.sum(jnp.where(eye, gcb, 0.0), axis=0, keepdims=True)
            decay_l.append(jnp.where(tri, jnp.exp(jnp.where(tri, gcb - gcr, 0.0)), 0.0))
            kb_l.append(kb.astype(BF16))
            rhs_l.append(jnp.concatenate([vf * beta, kb * e_gc[:, j:j + 1]], axis=-1).astype(BF16))
            qd_l.append((q_h * e_gc[:, j:j + 1]).astype(BF16))
            kd_l.append((k_h * e_rest[:, j:j + 1]).astype(BF16))
            kbf_l.append(k_h.astype(BF16))
            qbf_l.append(q_h.astype(BF16))
            gl_l.append(e_last[:, j:j + 1])

    n_u = len(units)
    kk_l = [lax.dot_general(kb_l[u], kbf_l[u], _NT, preferred_element_type=F32) for u in range(n_u)]
    qk_l = [lax.dot_general(qbf_l[u], kbf_l[u], _NT, preferred_element_type=F32) for u in range(n_u)]
    qk_l = [jnp.where(tri, qk_l[u] * decay_l[u], 0.0).astype(BF16) for u in range(n_u)]
    x_l = [(-jnp.where(strict, kk_l[u] * decay_l[u], 0.0)) for u in range(n_u)]
    t_l = [eye_f + x_l[u] for u in range(n_u)]
    x_l = [x.astype(BF16) for x in x_l]
    for _ in range(5):
        x_l = [jnp.dot(x, x, preferred_element_type=F32).astype(BF16) for x in x_l]
        t_l = [t + jnp.dot(t.astype(BF16), x, preferred_element_type=F32) for t, x in zip(t_l, x_l)]
    sol_l = [jnp.dot(t_l[u].astype(BF16), rhs_l[u], preferred_element_type=F32) for u in range(n_u)]

    s_cur = [s_scr[j] for j in range(GDN_HB)]
    for c in range(GDN_NC):
        r0 = c * C
        us = [c * GDN_HB + j for j in range(GDN_HB)]
        s_bf = [s.astype(BF16) for s in s_cur]
        ws_l = [jnp.dot(sol_l[u][:, GDN_DV:].astype(BF16), s_bf[j], preferred_element_type=F32)
                for j, u in enumerate(us)]
        qs_l = [jnp.dot(qd_l[u], s_bf[j], preferred_element_type=F32) for j, u in enumerate(us)]
        vn_l = [(sol_l[u][:, :GDN_DV] - ws_l[j]).astype(BF16) for j, u in enumerate(us)]
        kv_l = [lax.dot_general(kd_l[u], vn_l[j], _TN, preferred_element_type=F32) for j, u in enumerate(us)]
        ov_l = [jnp.dot(qk_l[u], vn_l[j], preferred_element_type=F32) for j, u in enumerate(us)]
        s_cur = [s_cur[j] * gl_l[u] + kv_l[j] for j, u in enumerate(us)]
        for j in range(GDN_HB):
            o = qs_l[j] + ov_l[j]
            zf = z_ref[r0:r0 + C, j * GDN_DV:(j + 1) * GDN_DV].astype(F32)
            o = o * lax.rsqrt(jnp.mean(o * o, axis=-1, keepdims=True) + RMS_EPS) * ng * _silu(zf)
            o_ref[r0:r0 + C, j * GDN_DV:(j + 1) * GDN_DV] = o.astype(o_ref.dtype)
    for j in range(GDN_HB):
        s_scr[j] = s_cur[j]


def _gdn_core(qkv, z, scal, hparams, norm_g):
    L = qkv.shape[0]
    G = GDN_V_HEADS // GDN_HB
    R = GDN_NC * GDN_CHUNK
    qw = (GDN_HB // 2) * GDN_DK
    vw = GDN_HB * GDN_DV
    k_blk0 = GDN_QK_W // qw
    v_blk0 = 2 * GDN_QK_W // vw
    vmem = 2 * (2 * R * qw * 2 + 2 * R * vw * 2 + R * LANES * 4 + R * vw * 2) + (16 << 20)
    return pl.pallas_call(
        _gdn_kernel,
        grid=(G, L // R),
        in_specs=[pl.BlockSpec((R, qw), lambda g, s: (s, g)),
                  pl.BlockSpec((R, qw), lambda g, s: (s, k_blk0 + g)),
                  pl.BlockSpec((R, vw), lambda g, s: (s, v_blk0 + g)),
                  pl.BlockSpec((R, vw), lambda g, s: (s, g)),
                  pl.BlockSpec((R, LANES), lambda g, s: (s, g)),
                  pl.BlockSpec((1, 8, LANES), lambda g, s: (g, 0, 0)),
                  pl.BlockSpec((1, GDN_DV), lambda g, s: (0, 0))],
        out_specs=pl.BlockSpec((R, vw), lambda g, s: (s, g)),
        out_shape=jax.ShapeDtypeStruct((L, GDN_V_W), BF16),
        scratch_shapes=[pltpu.VMEM((GDN_HB, GDN_DK, GDN_DV), F32)],
        compiler_params=_cparams(("parallel", "arbitrary"), vmem),
        name="gdn_core",
    )(qkv, qkv, qkv, z, scal, hparams, norm_g.reshape(1, GDN_DV))


def _gdn_mixer(xb, w_in, conv_w, a_log, dt_bias, norm_g):
    L = xb.shape[0]
    nqkv = 2 * GDN_QK_W + GDN_V_W
    w_qkv = w_in[:, :nqkv].astype(BF16)
    w_z = w_in[:, nqkv:nqkv + GDN_V_W].astype(BF16)
    w_ab = w_in[:, nqkv + GDN_V_W:]
    G = GDN_V_HEADS // GDN_HB
    w_a = w_ab[:, :GDN_V_HEADS].reshape(D_MODEL, G, GDN_HB)
    w_b = w_ab[:, GDN_V_HEADS:].reshape(D_MODEL, G, GDN_HB)
    w_sc = jnp.concatenate([w_a, w_b, jnp.zeros((D_MODEL, G, LANES - 2 * GDN_HB), F32)], axis=-1)
    w_sc = w_sc.reshape(D_MODEL, G * LANES).astype(BF16)
    qkv = _proj_conv_silu(xb, w_qkv, conv_w, tm=1024, tn=1024)
    ones_z = jnp.ones((1, GDN_V_W), F32)
    z = _matmul_scaled(xb, w_z, ones_z, BF16, tm=1024, tn=1024)
    scal = _matmul_scaled(xb, w_sc, jnp.ones((1, G * LANES), F32), F32, tm=1024, tn=G * LANES)
    hp = jnp.zeros((G, 8, LANES), F32)
    hp = hp.at[:, 0, :GDN_HB].set(a_log.reshape(G, GDN_HB))
    hp = hp.at[:, 1, :GDN_HB].set(dt_bias.reshape(G, GDN_HB))
    return _gdn_core(qkv, z, scal, hp, norm_g)


def _dsa_small_kernel(x_ref, w_ref, g_ref, b_ref, ki_ref, wi_ref):
    acc = jnp.dot(x_ref[...], w_ref[...], preferred_element_type=F32)
    ki = acc[:, :IDX_DIM]
    mu = jnp.mean(ki, axis=-1, keepdims=True)
    kc = ki - mu
    var = jnp.mean(kc * kc, axis=-1, keepdims=True)
    ki_ref[...] = (kc * lax.rsqrt(var + LN_EPS) * g_ref[...] + b_ref[...]).astype(ki_ref.dtype)
    wi_ref[...] = acc[:, IDX_DIM:] * ((IDX_HEADS ** -0.5) * (IDX_DIM ** -0.5))


def _dsa_small(xb, w_small, ln_g, ln_b, tm):
    M, K = xb.shape
    tm = min(tm, M)
    N = 2 * LANES
    vmem = 2 * (tm * K * 2 + K * N * 2 + tm * LANES * 6) + 4 * tm * N * 4
    return pl.pallas_call(
        _dsa_small_kernel,
        grid=(M // tm,),
        in_specs=[pl.BlockSpec((tm, K), lambda i: (i, 0)),
                  pl.BlockSpec((K, N), lambda i: (0, 0)),
                  pl.BlockSpec((1, IDX_DIM), lambda i: (0, 0)),
                  pl.BlockSpec((1, IDX_DIM), lambda i: (0, 0))],
        out_specs=[pl.BlockSpec((tm, IDX_DIM), lambda i: (i, 0)),
                   pl.BlockSpec((tm, LANES), lambda i: (i, 0))],
        out_shape=[jax.ShapeDtypeStruct((M, IDX_DIM), BF16), jax.ShapeDtypeStruct((M, LANES), F32)],
        compiler_params=_cparams(("parallel",), vmem),
        name="dsa_idx_proj",
    )(xb, w_small, ln_g.reshape(1, IDX_DIM), ln_b.reshape(1, IDX_DIM))


def _sortable_key(score):
    bits = lax.bitcast_convert_type(score, I32)
    return jnp.where(bits >= 0, bits, bits ^ jnp.int32(0x7FFFFFFF))


def _idx_kernel(qi_tab, kj_tab, last_tab,
                qidx_ref, kidx_ref, wi_ref, far_ref, near_ref, key_scr, *, k_top, n_sub_total):
    s = pl.program_id(0)
    i = qi_tab[s]
    j = kj_tab[s]
    nsub = IDX_KEY_BLOCK // IDX_SUB
    t_col = i * Q_BLOCK + lax.broadcasted_iota(I32, (Q_BLOCK, 1), 0)
    lane_sub = lax.broadcasted_iota(I32, (Q_BLOCK, IDX_SUB), 1)
    wi = wi_ref[...]

    for sub in range(nsub):
        ki_sub = kidx_ref[sub * IDX_SUB:(sub + 1) * IDX_SUB, :]
        acc = jnp.zeros((Q_BLOCK, IDX_SUB), F32)
        for h in range(IDX_HEADS):
            sc = lax.dot_general(qidx_ref[:, h * IDX_DIM:(h + 1) * IDX_DIM], ki_sub, _NT,
                                 preferred_element_type=F32)
            acc = acc + jnp.maximum(sc, 0.0) * wi[:, h:h + 1]
        s_idx = j * IDX_KEY_BLOCK + sub * IDX_SUB + lane_sub
        key_scr[j * nsub + sub] = jnp.where(s_idx <= t_col, _sortable_key(acc), INT_MIN)

    @pl.when(last_tab[s] == 1)
    def _():
        n_chunks = (i * Q_BLOCK + Q_BLOCK - 1) // IDX_SUB + 1

        def count(pred, ref_val):
            refb = jnp.broadcast_to(ref_val, (Q_BLOCK, LANES))

            def body(c, cnt):
                blk = key_scr[c]
                for l in range(IDX_SUB // LANES):
                    cnt = cnt + jnp.where(pred(blk[:, l * LANES:(l + 1) * LANES], refb), 1, 0)
                return cnt

            cnt = lax.fori_loop(0, n_chunks, body, jnp.zeros((Q_BLOCK, LANES), I32))
            return jnp.sum(cnt, axis=1, keepdims=True)

        def group_body(carry):
            grp, thr, n_ge, _ = carry
            for bb in range(BITS_PER_CHECK):
                cand = thr + jnp.left_shift(jnp.int32(1), 31 - (grp * BITS_PER_CHECK + bb))
                n_cand = count(lambda a, r: a >= r, cand)
                ok = n_cand >= k_top
                thr, n_ge = jnp.where(ok, cand, thr), jnp.where(ok, n_cand, n_ge)
            return grp + 1, thr, n_ge, jnp.max(jnp.where(n_ge != k_top, 1, 0))

        _, thr, n_ge, n_tied_rows = lax.while_loop(
            lambda carry: (carry[0] < 32 // BITS_PER_CHECK) & (carry[3] != 0), group_body,
            (jnp.int32(0), jnp.full((Q_BLOCK, 1), INT_MIN, I32),
             jnp.broadcast_to(n_chunks * IDX_SUB, (Q_BLOCK, 1)).astype(I32), jnp.int32(1)))
        thr_b = jnp.broadcast_to(thr, (Q_BLOCK, IDX_SUB))

        def emit(c, sel):
            s_idx = c * IDX_SUB + lane_sub
            far = sel & (t_col - s_idx >= MAX_DISTANCE)
            far_ref[0, c] = jnp.where(far, 0.0, FAR_MASKED).astype(far_ref.dtype)
            key_scr[c] = jnp.where(sel, 1, 0)

        @pl.when(n_tied_rows == 0)
        def _():
            def sel_body(c, carry):
                emit(c, key_scr[c] >= thr_b)
                return carry

            lax.fori_loop(0, n_chunks, sel_body, 0)

        @pl.when(n_tied_rows != 0)
        def _():
            n_gt = count(lambda a, r: a > r, thr)
            need_eq = (k_top - n_gt).astype(F32)
            incl = (lax.broadcasted_iota(I32, (IDX_SUB, IDX_SUB), 0)
                    <= lax.broadcasted_iota(I32, (IDX_SUB, IDX_SUB), 1)).astype(BF16)

            def sel_body(c, carry):
                blk = key_scr[c]
                eq = blk == thr_b
                eq_f = jnp.where(eq, 1.0, 0.0)
                rank = carry + jnp.dot(eq_f.astype(BF16), incl, preferred_element_type=F32)
                s_idx = c * IDX_SUB + lane_sub
                emit(c, ((blk > thr_b) | (eq & (rank <= need_eq))) & (s_idx <= t_col))
                return carry + jnp.sum(eq_f, axis=1, keepdims=True)

            lax.fori_loop(0, n_chunks, sel_body, jnp.zeros((Q_BLOCK, 1), F32))

        def fill_body(c, carry):
            far_ref[0, c] = jnp.full((Q_BLOCK, IDX_SUB), FAR_MASKED, far_ref.dtype)
            return carry

        lax.fori_loop(n_chunks, n_sub_total, fill_body, 0)

        def window(blk_idx):
            per = IDX_SUB // Q_BLOCK
            chunk = key_scr[blk_idx // per]
            m = blk_idx % per
            out = chunk[:, 0:Q_BLOCK]
            for q in range(1, per):
                out = jnp.where(m == q, chunk[:, q * Q_BLOCK:(q + 1) * Q_BLOCK], out)
            return out

        r_i = lax.broadcasted_iota(I32, (Q_BLOCK, Q_BLOCK), 0)
        c_i = lax.broadcasted_iota(I32, (Q_BLOCK, Q_BLOCK), 1)
        d_lo = Q_BLOCK + r_i - c_i
        d_hi = r_i - c_i
        near_lo = (window(jnp.maximum(i - 1, 0)) != 0) & (d_lo < MAX_DISTANCE) & (i >= 1)
        near_hi = (window(i) != 0) & (d_hi >= 0) & (d_hi < MAX_DISTANCE)
        near_ref[:, 0:Q_BLOCK] = jnp.where(near_lo, 0.0, -jnp.inf).astype(near_ref.dtype)
        near_ref[:, Q_BLOCK:2 * Q_BLOCK] = jnp.where(near_hi, 0.0, -jnp.inf).astype(near_ref.dtype)


def _idx_select(proj, ki, wi, k_top):
    L = ki.shape[0]
    nq = L // Q_BLOCK
    n_sub_total = L // IDX_SUB
    qi_l, kj_l, last_l = [], [], []
    for i in range(nq):
        j_last = (i * Q_BLOCK + Q_BLOCK - 1) // IDX_KEY_BLOCK
        for j in range(j_last + 1):
            qi_l.append(i)
            kj_l.append(j)
            last_l.append(1 if j == j_last else 0)
    tabs = [jnp.asarray(np.asarray(t, np.int32)) for t in (qi_l, kj_l, last_l)]
    qcol = (DSA_HEADS * DSA_DH) // (IDX_HEADS * IDX_DIM)
    vmem = (L * Q_BLOCK * 4 + 2 * (Q_BLOCK * L * 2) + 2 * (Q_BLOCK * IDX_HEADS * IDX_DIM * 2)
            + 2 * IDX_KEY_BLOCK * IDX_DIM * 2 + (8 << 20))
    grid_spec = pltpu.PrefetchScalarGridSpec(
        num_scalar_prefetch=3,
        grid=(len(qi_l),),
        in_specs=[pl.BlockSpec((Q_BLOCK, IDX_HEADS * IDX_DIM), lambda s, qt, kt, lt: (qt[s], qcol)),
                  pl.BlockSpec((IDX_KEY_BLOCK, IDX_DIM), lambda s, qt, kt, lt: (kt[s], 0)),
                  pl.BlockSpec((Q_BLOCK, LANES), lambda s, qt, kt, lt: (qt[s], 0))],
        out_specs=[pl.BlockSpec((1, n_sub_total, Q_BLOCK, IDX_SUB), lambda s, qt, kt, lt: (qt[s], 0, 0, 0)),
                   pl.BlockSpec((Q_BLOCK, 2 * Q_BLOCK), lambda s, qt, kt, lt: (qt[s], 0))],
        scratch_shapes=[pltpu.VMEM((n_sub_total, Q_BLOCK, IDX_SUB), I32)],
    )
    return pl.pallas_call(
        functools.partial(_idx_kernel, k_top=k_top, n_sub_total=n_sub_total),
        grid_spec=grid_spec,
        out_shape=[jax.ShapeDtypeStruct((nq, n_sub_total, Q_BLOCK, IDX_SUB), BF16),
                   jax.ShapeDtypeStruct((L, 2 * Q_BLOCK), BF16)],
        compiler_params=_cparams(("arbitrary",), vmem),
        name="dsa_idx_select",
    )(*tabs, proj, ki, wi)


M_INIT = -1e30
FAR_SUBS = 2
LOG2E = math.log2(math.e)


def _attn_kernel(qi_tab, kj_tab, kind_tab, first_tab,
                 tab_ref, q_ref, kt_ref, vf_ref, klo_ref, khi_ref, vlo_ref, vhi_ref, far_ref, near_ref,
                 o_ref, m_scr, l_scr, acc_scr, b_scr, s_scr):
    s = pl.program_id(0)

    @pl.when(s == 0)
    def _():
        r_i = lax.broadcasted_iota(I32, (Q_BLOCK, 2 * Q_BLOCK), 0)
        c_i = lax.broadcasted_iota(I32, (Q_BLOCK, 2 * Q_BLOCK), 1)
        d = jnp.maximum(Q_BLOCK + r_i - c_i, 0)
        max_exact = N_BUCKETS // 2
        df = jnp.maximum(d, 1).astype(F32)
        large = max_exact + (jnp.log(df / max_exact) / math.log(MAX_DISTANCE / max_exact)
                             * (N_BUCKETS - max_exact)).astype(I32)
        large = jnp.minimum(large, N_BUCKETS - 1)
        bkt = jnp.where(d < max_exact, d, large)
        for h in range(DSA_HEADS):
            acc = jnp.zeros((Q_BLOCK, 2 * Q_BLOCK), F32)
            for b in range(N_BUCKETS):
                acc = jnp.where(bkt == b, (tab_ref[b, h] - tab_ref[N_BUCKETS - 1, h]) * LOG2E, acc)
            b_scr[h] = acc

    @pl.when(first_tab[s] == 1)
    def _():
        m_scr[...] = jnp.full_like(m_scr, M_INIT)
        l_scr[...] = jnp.zeros_like(l_scr)
        acc_scr[...] = jnp.zeros_like(acc_scr)

    def softmax_pv(v_ref, width, row0=0):
        nt = width // LANES
        rows = slice(row0, row0 + width)
        ones = jnp.ones((width, DSA_DH), BF16)
        for h in range(DSA_HEADS):
            g = h // DSA_GROUP
            tiles = [s_scr[h, :, t * LANES:(t + 1) * LANES] for t in range(nt)]
            tmax = tiles[0]
            for t in range(1, nt):
                tmax = jnp.maximum(tmax, tiles[t])
            m_prev = m_scr[h]
            m_new = jnp.maximum(m_prev, jnp.max(tmax, axis=-1, keepdims=True))
            alpha = jnp.exp2(m_prev - m_new)
            p = [jnp.exp2(tiles[t] - m_new).astype(BF16) for t in range(nt)]
            pb = jnp.concatenate(p, axis=-1) if nt > 1 else p[0]
            v_aug = jnp.concatenate([v_ref[rows, g * DSA_DH:(g + 1) * DSA_DH], ones], axis=-1)
            pv = jnp.dot(pb, v_aug, preferred_element_type=F32)
            acc_scr[h] = alpha * acc_scr[h] + pv[:, :DSA_DH]
            l_scr[h] = alpha * l_scr[h] + pv[:, DSA_DH:]
            m_scr[h] = m_new

    @pl.when(kind_tab[s] == 0)
    def _():
        eye = (lax.broadcasted_iota(I32, (Q_BLOCK, Q_BLOCK), 0)
               == lax.broadcasted_iota(I32, (Q_BLOCK, Q_BLOCK), 1)).astype(BF16)
        for sub in range(FAR_SUBS):
            cols = slice(sub * IDX_SUB, (sub + 1) * IDX_SUB)
            mask = far_ref[0, sub]
            for g in range(DSA_KV_HEADS):
                lhs = jnp.concatenate(
                    [jnp.concatenate([q_ref[:, h * DSA_DH:(h + 1) * DSA_DH], eye], axis=1)
                     for h in range(g * DSA_GROUP, (g + 1) * DSA_GROUP)], axis=0)
                rhs = jnp.concatenate([kt_ref[g * DSA_DH:(g + 1) * DSA_DH, cols], mask], axis=0)
                logits = jnp.dot(lhs, rhs, preferred_element_type=F32)
                for hh in range(DSA_GROUP):
                    s_scr[g * DSA_GROUP + hh] = logits[hh * Q_BLOCK:(hh + 1) * Q_BLOCK, :]
            softmax_pv(vf_ref, IDX_SUB, sub * IDX_SUB)

    @pl.when(kind_tab[s] == 1)
    def _():
        for half, (k_ref, v_ref) in enumerate(((klo_ref, vlo_ref), (khi_ref, vhi_ref))):
            cols = slice(half * Q_BLOCK, (half + 1) * Q_BLOCK)
            mask = near_ref[:, cols].astype(F32)
            for h in range(DSA_HEADS):
                g = h // DSA_GROUP
                logits = lax.dot_general(q_ref[:, h * DSA_DH:(h + 1) * DSA_DH],
                                         k_ref[:, g * DSA_DH:(g + 1) * DSA_DH], _NT, preferred_element_type=F32)
                s_scr[h, :, 0:Q_BLOCK] = logits + b_scr[h][:, cols] + mask
            softmax_pv(v_ref, Q_BLOCK)
        for h in range(DSA_HEADS):
            o_ref[:, h * DSA_DH:(h + 1) * DSA_DH] = (acc_scr[h] / l_scr[h]).astype(o_ref.dtype)


def _masked_attention(proj, far, near, rel_bias):
    L = proj.shape[0]
    nq = L // Q_BLOCK
    far_keys = FAR_SUBS * IDX_SUB
    per = far_keys // Q_BLOCK
    qi_l, kj_l, kind_l, first_l = [], [], [], []
    for i in range(nq):
        n_far = -(-i // per)
        for j in range(n_far):
            qi_l.append(i); kj_l.append(j); kind_l.append(0); first_l.append(1 if j == 0 else 0)
        qi_l.append(i); kj_l.append(max(n_far - 1, 0)); kind_l.append(1); first_l.append(1 if n_far == 0 else 0)
    tabs = [jnp.asarray(np.asarray(t, np.int32)) for t in (qi_l, kj_l, kind_l, first_l)]
    qw = DSA_HEADS * DSA_DH
    kvw = DSA_KV_HEADS * DSA_DH
    k_col = (2 * qw) // kvw
    v_col = k_col + 1
    hw = DSA_HEADS
    vmem = (2 * (Q_BLOCK * qw * 2 * 2 + 2 * far_keys * kvw * 2 + 4 * Q_BLOCK * kvw * 2
                 + Q_BLOCK * far_keys * 2 + Q_BLOCK * 2 * Q_BLOCK * 2)
            + hw * Q_BLOCK * (3 * LANES + 2 * Q_BLOCK) * 4 + (16 << 20))
    idx = lambda f: (lambda s, qt, kt, kd, ft: f(qt[s], kt[s]))
    k_t = proj[:, k_col * kvw:(k_col + 1) * kvw].T
    grid_spec = pltpu.PrefetchScalarGridSpec(
        num_scalar_prefetch=4,
        grid=(len(qi_l),),
        in_specs=[pl.BlockSpec(memory_space=pltpu.SMEM),
                  pl.BlockSpec((Q_BLOCK, qw), idx(lambda i, j: (i, 0))),
                  pl.BlockSpec((kvw, far_keys), idx(lambda i, j: (0, j))),
                  pl.BlockSpec((far_keys, kvw), idx(lambda i, j: (j, v_col))),
                  pl.BlockSpec((Q_BLOCK, kvw), idx(lambda i, j: (jnp.maximum(i - 1, 0), k_col))),
                  pl.BlockSpec((Q_BLOCK, kvw), idx(lambda i, j: (i, k_col))),
                  pl.BlockSpec((Q_BLOCK, kvw), idx(lambda i, j: (jnp.maximum(i - 1, 0), v_col))),
                  pl.BlockSpec((Q_BLOCK, kvw), idx(lambda i, j: (i, v_col))),
                  pl.BlockSpec((1, FAR_SUBS, Q_BLOCK, IDX_SUB), idx(lambda i, j: (i, j, 0, 0))),
                  pl.BlockSpec((Q_BLOCK, 2 * Q_BLOCK), idx(lambda i, j: (i, 0)))],
        out_specs=pl.BlockSpec((Q_BLOCK, qw), idx(lambda i, j: (i, 0))),
        scratch_shapes=[pltpu.VMEM((hw, Q_BLOCK, LANES), F32),
                        pltpu.VMEM((hw, Q_BLOCK, LANES), F32),
                        pltpu.VMEM((hw, Q_BLOCK, DSA_DH), F32),
                        pltpu.VMEM((hw, Q_BLOCK, 2 * Q_BLOCK), F32),
                        pltpu.VMEM((hw, Q_BLOCK, IDX_SUB), F32)],
    )
    return pl.pallas_call(
        _attn_kernel,
        grid_spec=grid_spec,
        out_shape=jax.ShapeDtypeStruct((L, qw), BF16),
        compiler_params=_cparams(("arbitrary",), vmem),
        name="dsa_attention",
    )(*tabs, rel_bias, proj, k_t, proj, proj, proj, proj, proj, far, near)


def _dsa_mixer(xb, w_in, ln_g, ln_b, rel_bias):
    L = xb.shape[0]
    k_top = min(TOPK_MAX, L // 4)
    sq = DSA_HEADS * DSA_DH
    skv = DSA_KV_HEADS * DSA_DH
    si = IDX_HEADS * IDX_DIM
    w_q = w_in[:, :sq]
    w_k = w_in[:, sq:sq + skv]
    w_v = w_in[:, sq + skv:sq + 2 * skv]
    w_qi = w_in[:, sq + 2 * skv:sq + 2 * skv + si]
    w_ki = w_in[:, sq + 2 * skv + si:sq + 2 * skv + si + IDX_DIM]
    w_wi = w_in[:, sq + 2 * skv + si + IDX_DIM:]
    w_main = jnp.concatenate([w_q, w_qi, w_k, w_v], axis=1).astype(BF16)
    colscale = jnp.concatenate([jnp.full((1, sq), DSA_DH ** -0.5 * LOG2E, F32),
                                jnp.ones((1, si + 2 * skv), F32)], axis=1)
    w_small = jnp.concatenate([w_ki, w_wi, jnp.zeros((D_MODEL, LANES - IDX_HEADS), F32)], axis=1).astype(BF16)
    proj = _matmul_scaled(xb, w_main, colscale, BF16, tm=1024, tn=1024)
    ki, wi = _dsa_small(xb, w_small, ln_g, ln_b, tm=1024)
    far, near = _idx_select(proj, ki, wi, k_top)
    return _masked_attention(proj, far, near, rel_bias)


def kernel(x, p, gdn_w_in, gdn_conv_w, gdn_a_log, gdn_dt_bias, gdn_norm_g, gdn_w_o, dsa_w_in, dsa_kidx_ln_g, dsa_kidx_ln_b, dsa_w_o, rel_bias, ln1_g, ln1_b, ffn_w_gate, ffn_w_up, ffn_conv_w, ffn_w_down, ln2_g, ln2_b, ple_w_proj, ple_w_gate):
    assert x.shape[0] == 1 and x.shape[2] == D_MODEL
    xf = x[0]
    xb = xf.astype(BF16)
    ia = ib = 0
    for i in range(DEPTH):
        if i % 2 == 0:
            mix = _gdn_mixer(xb, gdn_w_in[ia], gdn_conv_w[ia], gdn_a_log[ia], gdn_dt_bias[ia], gdn_norm_g[ia])
            w_o = gdn_w_o[ia]
            ia += 1
        else:
            mix = _dsa_mixer(xb, dsa_w_in[ib], dsa_kidx_ln_g[ib], dsa_kidx_ln_b[ib], rel_bias)
            w_o = dsa_w_o[ib]
            ib += 1
        xf, xb = _proj_res_ln(mix, w_o.astype(BF16), xf, ln1_g[i], ln1_b[i], tm=512, sub=256)
        hmid = _ffn_up(xb, ffn_w_gate[i].astype(BF16), ffn_w_up[i].astype(BF16), ffn_conv_w[i], tm=1024, tn=512)
        xf, xb = _proj_res_ln(hmid, ffn_w_down[i].astype(BF16), xf, ln2_g[i], ln2_b[i], tm=512, sub=256)
        xf, xb = _ple(xb, xf, ple_w_gate[i].astype(BF16), p[i, 0], ple_w_proj[i].astype(BF16), tm=1024, tn=1024)
    return xf[None]
```

```python
import functools
import math

import jax
import jax.numpy as jnp
import numpy as np
from jax import lax
from jax.experimental import pallas as pl
from jax.experimental.pallas import tpu as pltpu

F32 = jnp.float32
BF16 = jnp.bfloat16
I32 = jnp.int32

D_MODEL = 2048
GDN_QK_HEADS = 16
GDN_V_HEADS = 32
GDN_DK = 128
GDN_DV = 128
GDN_CONV = 4
GDN_CHUNK = 64
GDN_QK_W = GDN_QK_HEADS * GDN_DK
GDN_V_W = GDN_V_HEADS * GDN_DV
DSA_HEADS = 16
DSA_KV_HEADS = 4
DSA_GROUP = DSA_HEADS // DSA_KV_HEADS
DSA_DH = 128
IDX_HEADS = 16
IDX_DIM = 128
TOPK_MAX = 256
N_BUCKETS = 32
MAX_DISTANCE = 128
D_FF = 5120
FFN_CONV = 3
PLE_DIM = 256
DEPTH = 2
DN_ALPHA = (2.0 * DEPTH) ** 0.25
LN_EPS = 1e-5
RMS_EPS = 1e-6

V7X_VMEM_BYTES = 64 * 1024 * 1024
V7X_VMEM_BUDGET = 56 * 1024 * 1024
LANES = 128
BF16_SUBLANES = 16

HALO = BF16_SUBLANES
Q_BLOCK = 128
IDX_KEY_BLOCK = 2048
IDX_SUB = 512
INT_MIN = -(2 ** 31)
BITS_PER_CHECK = 4
EXP_PASSES = 8
FAR_MASKED = -2e30

_NT = (((1,), (1,)), ((), ()))
_TN = (((0,), (0,)), ((), ()))


def _cparams(semantics, vmem_bytes):
    return pltpu.CompilerParams(dimension_semantics=semantics,
                                vmem_limit_bytes=int(min(V7X_VMEM_BUDGET, vmem_bytes)))


def _silu(y):
    return y * jax.nn.sigmoid(y)


def _mm_scale_kernel(x_ref, w_ref, cs_ref, o_ref):
    acc = jnp.dot(x_ref[...], w_ref[...], preferred_element_type=F32)
    o_ref[...] = (acc * cs_ref[...]).astype(o_ref.dtype)


def _matmul_scaled(x, w, colscale, out_dtype, tm, tn):
    M, K = x.shape
    N = w.shape[1]
    tm, tn = min(tm, M), min(tn, N)
    osz = jnp.dtype(out_dtype).itemsize
    vmem = 2 * (tm * K * 2 + K * tn * 2 + tm * tn * osz) + 2 * tm * tn * 4
    return pl.pallas_call(
        _mm_scale_kernel,
        grid=(M // tm, N // tn),
        in_specs=[pl.BlockSpec((tm, K), lambda i, j: (i, 0)),
                  pl.BlockSpec((K, tn), lambda i, j: (0, j)),
                  pl.BlockSpec((1, tn), lambda i, j: (0, j))],
        out_specs=pl.BlockSpec((tm, tn), lambda i, j: (i, j)),
        out_shape=jax.ShapeDtypeStruct((M, N), out_dtype),
        compiler_params=_cparams(("parallel", "parallel"), vmem),
        name="matmul_scaled",
    )(x, w, colscale)


CONV_SUB = 256
FFN_SUB = 512


def _causal_conv(g, gh, cw_ref, g_scr, kc, tm, cols):
    g_scr[0:HALO, cols] = gh
    g_scr[HALO:HALO + tm, cols] = g
    y = cw_ref[kc - 1:kc, cols] * g
    for j in range(kc - 1):
        off = HALO - (kc - 1) + j
        y = y + cw_ref[j:j + 1, cols] * g_scr[off:off + tm, cols]
    return y


def _mm_conv_silu_kernel(x_ref, xh_ref, w_ref, cw_ref, o_ref, g_scr, *, kc, tm):
    first = pl.program_id(0) == 0
    for c0 in range(0, o_ref.shape[1], CONV_SUB):
        cols = slice(c0, c0 + CONV_SUB)
        w = w_ref[:, cols]
        gh = jnp.dot(xh_ref[...], w, preferred_element_type=F32)
        gh = jnp.where(first, 0.0, gh)
        g = jnp.dot(x_ref[...], w, preferred_element_type=F32)
        y = _causal_conv(g, gh, cw_ref, g_scr, kc, tm, cols)
        o_ref[:, cols] = _silu(y).astype(o_ref.dtype)


def _proj_conv_silu(x, w, conv_w, tm, tn):
    M, K = x.shape
    N = w.shape[1]
    kc = conv_w.shape[0]
    tm, tn = min(tm, M), min(tn, N)
    hb = tm // HALO
    vmem = 2 * (tm * K * 2 + HALO * K * 2 + K * tn * 2 + tm * tn * 2) + 4 * tm * tn * 4
    return pl.pallas_call(
        functools.partial(_mm_conv_silu_kernel, kc=kc, tm=tm),
        grid=(M // tm, N // tn),
        in_specs=[pl.BlockSpec((tm, K), lambda i, j: (i, 0)),
                  pl.BlockSpec((HALO, K), lambda i, j: (jnp.maximum(i * hb - 1, 0), 0)),
                  pl.BlockSpec((K, tn), lambda i, j: (0, j)),
                  pl.BlockSpec((kc, tn), lambda i, j: (0, j))],
        out_specs=pl.BlockSpec((tm, tn), lambda i, j: (i, j)),
        out_shape=jax.ShapeDtypeStruct((M, N), BF16),
        scratch_shapes=[pltpu.VMEM((tm + HALO, tn), F32)],
        compiler_params=_cparams(("parallel", "parallel"), vmem),
        name="proj_conv_silu",
    )(x, x, w, conv_w)


def _ffn_up_kernel(x_ref, xh_ref, wg_ref, wu_ref, cw_ref, o_ref, g_scr, *, kc, tm):
    first = pl.program_id(0) == 0
    for c0 in range(0, o_ref.shape[1], FFN_SUB):
        cols = slice(c0, c0 + FFN_SUB)
        wg = wg_ref[:, cols]
        g = jnp.dot(x_ref[...], wg, preferred_element_type=F32)
        gh = jnp.dot(xh_ref[...], wg, preferred_element_type=F32)
        gh = jnp.where(first, 0.0, gh)
        u = jnp.dot(x_ref[...], wu_ref[:, cols], preferred_element_type=F32)
        y = _causal_conv(g, gh, cw_ref, g_scr, kc, tm, cols)
        o_ref[:, cols] = (_silu(y) * u).astype(o_ref.dtype)


def _ffn_up(x, w_gate, w_up, conv_w, tm, tn):
    M, K = x.shape
    N = w_gate.shape[1]
    kc = conv_w.shape[0]
    tm, tn = min(tm, M), min(tn, N)
    hb = tm // HALO
    vmem = 2 * (tm * K * 2 + HALO * K * 2 + 2 * K * tn * 2 + tm * tn * 2) + 6 * tm * tn * 4
    return pl.pallas_call(
        functools.partial(_ffn_up_kernel, kc=kc, tm=tm),
        grid=(M // tm, N // tn),
        in_specs=[pl.BlockSpec((tm, K), lambda i, j: (i, 0)),
                  pl.BlockSpec((HALO, K), lambda i, j: (jnp.maximum(i * hb - 1, 0), 0)),
                  pl.BlockSpec((K, tn), lambda i, j: (0, j)),
                  pl.BlockSpec((K, tn), lambda i, j: (0, j)),
                  pl.BlockSpec((kc, tn), lambda i, j: (0, j))],
        out_specs=pl.BlockSpec((tm, tn), lambda i, j: (i, j)),
        out_shape=jax.ShapeDtypeStruct((M, N), BF16),
        scratch_shapes=[pltpu.VMEM((tm + HALO, tn), F32)],
        compiler_params=_cparams(("parallel", "parallel"), vmem),
        name="ffn_up",
    )(x, x, w_gate, w_up, conv_w)


def _mm_res_ln_kernel(a_ref, w_ref, res_ref, g_ref, b_ref, of_ref, ob_ref, *, sub):
    for r0 in range(0, a_ref.shape[0], sub):
        rows = slice(r0, r0 + sub)
        acc = jnp.dot(a_ref[rows, :], w_ref[...], preferred_element_type=F32)
        y = DN_ALPHA * res_ref[rows, :] + acc
        mu = jnp.mean(y, axis=-1, keepdims=True)
        yc = y - mu
        var = jnp.mean(yc * yc, axis=-1, keepdims=True)
        out = yc * lax.rsqrt(var + LN_EPS) * g_ref[...] + b_ref[...]
        of_ref[rows, :] = out
        ob_ref[rows, :] = out.astype(BF16)


def _proj_res_ln(a, w, res, g, b, tm, sub):
    M, K = a.shape
    N = w.shape[1]
    tm = min(tm, M)
    sub = min(sub, tm)
    vmem = K * N * 2 + 2 * (tm * K * 2 + tm * N * 4 + tm * N * 4 + tm * N * 2) + 4 * sub * N * 4
    return pl.pallas_call(
        functools.partial(_mm_res_ln_kernel, sub=sub),
        grid=(M // tm,),
        in_specs=[pl.BlockSpec((tm, K), lambda i: (i, 0)),
                  pl.BlockSpec((K, N), lambda i: (0, 0), pipeline_mode=pl.Buffered(1)),
                  pl.BlockSpec((tm, N), lambda i: (i, 0)),
                  pl.BlockSpec((1, N), lambda i: (0, 0)),
                  pl.BlockSpec((1, N), lambda i: (0, 0))],
        out_specs=[pl.BlockSpec((tm, N), lambda i: (i, 0)),
                   pl.BlockSpec((tm, N), lambda i: (i, 0))],
        out_shape=[jax.ShapeDtypeStruct((M, N), F32), jax.ShapeDtypeStruct((M, N), BF16)],
        compiler_params=_cparams(("parallel",), vmem),
        name="proj_res_ln",
    )(a, w, res, g.reshape(1, N), b.reshape(1, N))


def _ple_kernel(xb_ref, wg_ref, p_ref, wp_ref, xr_ref, of_ref, ob_ref):
    gate = jax.nn.sigmoid(jnp.dot(xb_ref[...], wg_ref[...], preferred_element_type=F32))
    pe = jnp.dot(p_ref[...].astype(BF16), wp_ref[...], preferred_element_type=F32)
    out = xr_ref[...] + gate * pe
    of_ref[...] = out
    ob_ref[...] = out.astype(BF16)


def _ple(xb, xf, w_gate, p, w_proj, tm, tn):
    M, K = xb.shape
    N = w_gate.shape[1]
    P = p.shape[1]
    tm, tn = min(tm, M), min(tn, N)
    vmem = 2 * (tm * K * 2 + K * tn * 2 + tm * P * 4 + P * tn * 2 + tm * tn * 10) + 4 * tm * tn * 4
    return pl.pallas_call(
        _ple_kernel,
        grid=(M // tm, N // tn),
        in_specs=[pl.BlockSpec((tm, K), lambda i, j: (i, 0)),
                  pl.BlockSpec((K, tn), lambda i, j: (0, j)),
                  pl.BlockSpec((tm, P), lambda i, j: (i, 0)),
                  pl.BlockSpec((P, tn), lambda i, j: (0, j)),
                  pl.BlockSpec((tm, tn), lambda i, j: (i, j))],
        out_specs=[pl.BlockSpec((tm, tn), lambda i, j: (i, j)),
                   pl.BlockSpec((tm, tn), lambda i, j: (i, j))],
        out_shape=[jax.ShapeDtypeStruct((M, N), F32), jax.ShapeDtypeStruct((M, N), BF16)],
        compiler_params=_cparams(("parallel", "parallel"), vmem),
        name="ple",
    )(xb, w_gate, p, w_proj, xf)


GDN_HB = 8
GDN_NC = 2


def _gdn_kernel(q_ref, k_ref, v_ref, z_ref, sc_ref, hp_ref, ng_ref, o_ref, s_scr):
    C = GDN_CHUNK

    @pl.when(pl.program_id(1) == 0)
    def _():
        s_scr[...] = jnp.zeros_like(s_scr)

    row = lax.broadcasted_iota(I32, (C, C), 0)
    col = lax.broadcasted_iota(I32, (C, C), 1)
    tri = row >= col
    strict = row > col
    eye = row == col
    tri_f = tri.astype(F32)
    eye_f = eye.astype(F32)

    raw = sc_ref[...]
    a_log = hp_ref[0, 0:1, :]
    dt_b = hp_ref[0, 1:2, :]
    xs = raw + dt_b
    softplus = jnp.maximum(xs, 0.0) + jnp.log1p(jnp.exp(-jnp.abs(xs)))
    g_all = -jnp.exp(a_log) * softplus
    beta_all = jax.nn.sigmoid(raw)
    ng = ng_ref[...]

    units = [(c, j) for c in range(GDN_NC) for j in range(GDN_HB)]
    kb_l, rhs_l, decay_l, qd_l, kd_l, kbf_l, qbf_l, gl_l = [], [], [], [], [], [], [], []
    for c in range(GDN_NC):
        r0 = c * C
        gc = jnp.dot(tri_f, g_all[r0:r0 + C, :], precision=lax.Precision.HIGHEST,
                     preferred_element_type=F32)
        g_last = gc[C - 1:C, :]
        e_gc = jnp.exp(gc)
        e_rest = jnp.exp(g_last - gc)
        e_last = jnp.exp(g_last)
        beta_c = beta_all[r0:r0 + C, :]
        qn, kn = [], []
        for hq in range(GDN_HB // 2):
            qf = q_ref[r0:r0 + C, hq * GDN_DK:(hq + 1) * GDN_DK].astype(F32)
            kf = k_ref[r0:r0 + C, hq * GDN_DK:(hq + 1) * GDN_DK].astype(F32)
            qn.append(qf * lax.rsqrt(jnp.sum(qf * qf, axis=-1, keepdims=True) + RMS_EPS) * (GDN_DK ** -0.5))
            kn.append(kf * lax.rsqrt(jnp.sum(kf * kf, axis=-1, keepdims=True) + RMS_EPS))
        for j in range(GDN_HB):
            q_h, k_h = qn[j // 2], kn[j // 2]
            vf = v_ref[r0:r0 + C, j * GDN_DV:(j + 1) * GDN_DV].astype(F32)
            beta = beta_c[:, GDN_HB + j:GDN_HB + j + 1]
            kb = k_h * beta
            gcb = jnp.broadcast_to(gc[:, j:j + 1], (C, C))
            gcr = jnp.sum(jnp.where(eye, gcb, 0.0), axis=0, keepdims=True)
            decay_l.append(jnp.where(tri, jnp.exp(jnp.where(tri, gcb - gcr, 0.0)), 0.0))
            kb_l.append(kb.astype(BF16))
            rhs_l.append(jnp.concatenate([vf * beta, kb * e_gc[:, j:j + 1]], axis=-1).astype(BF16))
            qd_l.append((q_h * e_gc[:, j:j + 1]).astype(BF16))
            kd_l.append((k_h * e_rest[:, j:j + 1]).astype(BF16))
            kbf_l.append(k_h.astype(BF16))
            qbf_l.append(q_h.astype(BF16))
            gl_l.append(e_last[:, j:j + 1])

    n_u = len(units)
    kk_l = [lax.dot_general(kb_l[u], kbf_l[u], _NT, preferred_element_type=F32) for u in range(n_u)]
    qk_l = [lax.dot_general(qbf_l[u], kbf_l[u], _NT, preferred_element_type=F32) for u in range(n_u)]
    qk_l = [jnp.where(tri, qk_l[u] * decay_l[u], 0.0).astype(BF16) for u in range(n_u)]
    x_l = [(-jnp.where(strict, kk_l[u] * decay_l[u], 0.0)) for u in range(n_u)]
    t_l = [eye_f + x_l[u] for u in range(n_u)]
    x_l = [x.astype(BF16) for x in x_l]
    for _ in range(5):
        x_l = [jnp.dot(x, x, preferred_element_type=F32).astype(BF16) for x in x_l]
        t_l = [t + jnp.dot(t.astype(BF16), x, preferred_element_type=F32) for t, x in zip(t_l, x_l)]
    sol_l = [jnp.dot(t_l[u].astype(BF16), rhs_l[u], preferred_element_type=F32) for u in range(n_u)]

    s_cur = [s_scr[j] for j in range(GDN_HB)]
    for c in range(GDN_NC):
        r0 = c * C
        us = [c * GDN_HB + j for j in range(GDN_HB)]
        s_bf = [s.astype(BF16) for s in s_cur]
        ws_l = [jnp.dot(sol_l[u][:, GDN_DV:].astype(BF16), s_bf[j], preferred_element_type=F32)
                for j, u in enumerate(us)]
        qs_l = [jnp.dot(qd_l[u], s_bf[j], preferred_element_type=F32) for j, u in enumerate(us)]
        vn_l = [(sol_l[u][:, :GDN_DV] - ws_l[j]).astype(BF16) for j, u in enumerate(us)]
        kv_l = [lax.dot_general(kd_l[u], vn_l[j], _TN, preferred_element_type=F32) for j, u in enumerate(us)]
        ov_l = [jnp.dot(qk_l[u], vn_l[j], preferred_element_type=F32) for j, u in enumerate(us)]
        s_cur = [s_cur[j] * gl_l[u] + kv_l[j] for j, u in enumerate(us)]
        for j in range(GDN_HB):
            o = qs_l[j] + ov_l[j]
            zf = z_ref[r0:r0 + C, j * GDN_DV:(j + 1) * GDN_DV].astype(F32)
            o = o * lax.rsqrt(jnp.mean(o * o, axis=-1, keepdims=True) + RMS_EPS) * ng * _silu(zf)
            o_ref[r0:r0 + C, j * GDN_DV:(j + 1) * GDN_DV] = o.astype(o_ref.dtype)
    for j in range(GDN_HB):
        s_scr[j] = s_cur[j]


def _gdn_core(qkv, z, scal, hparams, norm_g):
    L = qkv.shape[0]
    G = GDN_V_HEADS // GDN_HB
    R = GDN_NC * GDN_CHUNK
    qw = (GDN_HB // 2) * GDN_DK
    vw = GDN_HB * GDN_DV
    k_blk0 = GDN_QK_W // qw
    v_blk0 = 2 * GDN_QK_W // vw
    vmem = 2 * (2 * R * qw * 2 + 2 * R * vw * 2 + R * LANES * 4 + R * vw * 2) + (16 << 20)
    return pl.pallas_call(
        _gdn_kernel,
        grid=(G, L // R),
        in_specs=[pl.BlockSpec((R, qw), lambda g, s: (s, g)),
                  pl.BlockSpec((R, qw), lambda g, s: (s, k_blk0 + g)),
                  pl.BlockSpec((R, vw), lambda g, s: (s, v_blk0 + g)),
                  pl.BlockSpec((R, vw), lambda g, s: (s, g)),
                  pl.BlockSpec((R, LANES), lambda g, s: (s, g)),
                  pl.BlockSpec((1, 8, LANES), lambda g, s: (g, 0, 0)),
                  pl.BlockSpec((1, GDN_DV), lambda g, s: (0, 0))],
        out_specs=pl.BlockSpec((R, vw), lambda g, s: (s, g)),
        out_shape=jax.ShapeDtypeStruct((L, GDN_V_W), BF16),
        scratch_shapes=[pltpu.VMEM((GDN_HB, GDN_DK, GDN_DV), F32)],
        compiler_params=_cparams(("parallel", "arbitrary"), vmem),
        name="gdn_core",
    )(qkv, qkv, qkv, z, scal, hparams, norm_g.reshape(1, GDN_DV))


def _gdn_mixer(xb, w_in, conv_w, a_log, dt_bias, norm_g):
    L = xb.shape[0]
    nqkv = 2 * GDN_QK_W + GDN_V_W
    w_qkv = w_in[:, :nqkv].astype(BF16)
    w_z = w_in[:, nqkv:nqkv + GDN_V_W].astype(BF16)
    w_ab = w_in[:, nqkv + GDN_V_W:]
    G = GDN_V_HEADS // GDN_HB
    w_a = w_ab[:, :GDN_V_HEADS].reshape(D_MODEL, G, GDN_HB)
    w_b = w_ab[:, GDN_V_HEADS:].reshape(D_MODEL, G, GDN_HB)
    w_sc = jnp.concatenate([w_a, w_b, jnp.zeros((D_MODEL, G, LANES - 2 * GDN_HB), F32)], axis=-1)
    w_sc = w_sc.reshape(D_MODEL, G * LANES).astype(BF16)
    qkv = _proj_conv_silu(xb, w_qkv, conv_w, tm=1024, tn=1024)
    ones_z = jnp.ones((1, GDN_V_W), F32)
    z = _matmul_scaled(xb, w_z, ones_z, BF16, tm=1024, tn=1024)
    scal = _matmul_scaled(xb, w_sc, jnp.ones((1, G * LANES), F32), F32, tm=1024, tn=G * LANES)
    hp = jnp.zeros((G, 8, LANES), F32)
    hp = hp.at[:, 0, :GDN_HB].set(a_log.reshape(G, GDN_HB))
    hp = hp.at[:, 1, :GDN_HB].set(dt_bias.reshape(G, GDN_HB))
    return _gdn_core(qkv, z, scal, hp, norm_g)


def _dsa_small_kernel(x_ref, w_ref, g_ref, b_ref, ki_ref, wi_ref):
    acc = jnp.dot(x_ref[...], w_ref[...], preferred_element_type=F32)
    ki = acc[:, :IDX_DIM]
    mu = jnp.mean(ki, axis=-1, keepdims=True)
    kc = ki - mu
    var = jnp.mean(kc * kc, axis=-1, keepdims=True)
    ki_ref[...] = (kc * lax.rsqrt(var + LN_EPS) * g_ref[...] + b_ref[...]).astype(ki_ref.dtype)
    wi_ref[...] = acc[:, IDX_DIM:] * ((IDX_HEADS ** -0.5) * (IDX_DIM ** -0.5))


def _dsa_small(xb, w_small, ln_g, ln_b, tm):
    M, K = xb.shape
    tm = min(tm, M)
    N = 2 * LANES
    vmem = 2 * (tm * K * 2 + K * N * 2 + tm * LANES * 6) + 4 * tm * N * 4
    return pl.pallas_call(
        _dsa_small_kernel,
        grid=(M // tm,),
        in_specs=[pl.BlockSpec((tm, K), lambda i: (i, 0)),
                  pl.BlockSpec((K, N), lambda i: (0, 0)),
                  pl.BlockSpec((1, IDX_DIM), lambda i: (0, 0)),
                  pl.BlockSpec((1, IDX_DIM), lambda i: (0, 0))],
        out_specs=[pl.BlockSpec((tm, IDX_DIM), lambda i: (i, 0)),
                   pl.BlockSpec((tm, LANES), lambda i: (i, 0))],
        out_shape=[jax.ShapeDtypeStruct((M, IDX_DIM), BF16), jax.ShapeDtypeStruct((M, LANES), F32)],
        compiler_params=_cparams(("parallel",), vmem),
        name="dsa_idx_proj",
    )(xb, w_small, ln_g.reshape(1, IDX_DIM), ln_b.reshape(1, IDX_DIM))


def _sortable_key(score):
    bits = lax.bitcast_convert_type(score, I32)
    return jnp.where(bits >= 0, bits, bits ^ jnp.int32(0x7FFFFFFF))


def _idx_kernel(qi_tab, kj_tab, last_tab,
                qidx_ref, kidx_ref, wi_ref, far_ref, near_ref, key_scr, *, k_top, n_sub_total):
    s = pl.program_id(0)
    i = qi_tab[s]
    j = kj_tab[s]
    nsub = IDX_KEY_BLOCK // IDX_SUB
    t_col = i * Q_BLOCK + lax.broadcasted_iota(I32, (Q_BLOCK, 1), 0)
    lane_sub = lax.broadcasted_iota(I32, (Q_BLOCK, IDX_SUB), 1)
    wi = wi_ref[...]

    for sub in range(nsub):
        ki_sub = kidx_ref[sub * IDX_SUB:(sub + 1) * IDX_SUB, :]
        acc = jnp.zeros((Q_BLOCK, IDX_SUB), F32)
        for h in range(IDX_HEADS):
            sc = lax.dot_general(qidx_ref[:, h * IDX_DIM:(h + 1) * IDX_DIM], ki_sub, _NT,
                                 preferred_element_type=F32)
            acc = acc + jnp.maximum(sc, 0.0) * wi[:, h:h + 1]
        s_idx = j * IDX_KEY_BLOCK + sub * IDX_SUB + lane_sub
        key_scr[j * nsub + sub] = jnp.where(s_idx <= t_col, _sortable_key(acc), INT_MIN)

    @pl.when(last_tab[s] == 1)
    def _():
        n_chunks = (i * Q_BLOCK + Q_BLOCK - 1) // IDX_SUB + 1

        def count(pred, ref_val):
            refb = jnp.broadcast_to(ref_val, (Q_BLOCK, LANES))

            def body(c, cnt):
                blk = key_scr[c]
                for l in range(IDX_SUB // LANES):
                    cnt = cnt + jnp.where(pred(blk[:, l * LANES:(l + 1) * LANES], refb), 1, 0)
                return cnt

            cnt = lax.fori_loop(0, n_chunks, body, jnp.zeros((Q_BLOCK, LANES), I32))
            return jnp.sum(cnt, axis=1, keepdims=True)

        def group_body(carry):
            grp, thr, n_ge, _ = carry
            for bb in range(BITS_PER_CHECK):
                cand = thr + jnp.left_shift(jnp.int32(1), 31 - (grp * BITS_PER_CHECK + bb))
                n_cand = count(lambda a, r: a >= r, cand)
                ok = n_cand >= k_top
                thr, n_ge = jnp.where(ok, cand, thr), jnp.where(ok, n_cand, n_ge)
            return grp + 1, thr, n_ge, jnp.max(jnp.where(n_ge != k_top, 1, 0))

        def max_body(c, m):
            blk = key_scr[c]
            for l in range(IDX_SUB // LANES):
                m = jnp.maximum(m, blk[:, l * LANES:(l + 1) * LANES])
            return m

        row_max = jnp.max(lax.fori_loop(0, n_chunks, max_body, jnp.full((Q_BLOCK, LANES), INT_MIN, I32)),
                          axis=1, keepdims=True)
        thr = jnp.full((Q_BLOCK, 1), INT_MIN, I32)
        n_ge = jnp.broadcast_to(n_chunks * IDX_SUB, (Q_BLOCK, 1)).astype(I32)
        for b in range(EXP_PASSES):
            cand = thr + jnp.int32(-(2 ** 31) if b == 0 else 2 ** (31 - b))
            live = jnp.max(jnp.where(cand <= row_max, 1, 0))
            n_cand = lax.cond(live != 0, lambda cand=cand: count(lambda a, r: a >= r, cand),
                              lambda: jnp.zeros((Q_BLOCK, 1), I32))
            ok = n_cand >= k_top
            thr, n_ge = jnp.where(ok, cand, thr), jnp.where(ok, n_cand, n_ge)

        _, thr, n_ge, n_tied_rows = lax.while_loop(
            lambda carry: (carry[0] < 32 // BITS_PER_CHECK) & (carry[3] != 0), group_body,
            (jnp.int32(EXP_PASSES // BITS_PER_CHECK), thr, n_ge, jnp.int32(1)))
        thr_b = jnp.broadcast_to(thr, (Q_BLOCK, IDX_SUB))

        def emit(c, sel):
            s_idx = c * IDX_SUB + lane_sub
            far = sel & (t_col - s_idx >= MAX_DISTANCE)
            far_ref[0, c] = jnp.where(far, 0.0, FAR_MASKED).astype(far_ref.dtype)
            key_scr[c] = jnp.where(sel, 1, 0)

        @pl.when(n_tied_rows == 0)
        def _():
            def sel_body(c, carry):
                emit(c, key_scr[c] >= thr_b)
                return carry

            lax.fori_loop(0, n_chunks, sel_body, 0)

        @pl.when(n_tied_rows != 0)
        def _():
            n_gt = count(lambda a, r: a > r, thr)
            need_eq = (k_top - n_gt).astype(F32)
            incl = (lax.broadcasted_iota(I32, (IDX_SUB, IDX_SUB), 0)
                    <= lax.broadcasted_iota(I32, (IDX_SUB, IDX_SUB), 1)).astype(BF16)

            def sel_body(c, carry):
                blk = key_scr[c]
                eq = blk == thr_b
                eq_f = jnp.where(eq, 1.0, 0.0)
                rank = carry + jnp.dot(eq_f.astype(BF16), incl, preferred_element_type=F32)
                s_idx = c * IDX_SUB + lane_sub
                emit(c, ((blk > thr_b) | (eq & (rank <= need_eq))) & (s_idx <= t_col))
                return carry + jnp.sum(eq_f, axis=1, keepdims=True)

            lax.fori_loop(0, n_chunks, sel_body, jnp.zeros((Q_BLOCK, 1), F32))

        def fill_body(c, carry):
            far_ref[0, c] = jnp.full((Q_BLOCK, IDX_SUB), FAR_MASKED, far_ref.dtype)
            return carry

        lax.fori_loop(n_chunks, n_sub_total, fill_body, 0)

        def window(blk_idx):
            per = IDX_SUB // Q_BLOCK
            chunk = key_scr[blk_idx // per]
            m = blk_idx % per
            out = chunk[:, 0:Q_BLOCK]
            for q in range(1, per):
                out = jnp.where(m == q, chunk[:, q * Q_BLOCK:(q + 1) * Q_BLOCK], out)
            return out

        r_i = lax.broadcasted_iota(I32, (Q_BLOCK, Q_BLOCK), 0)
        c_i = lax.broadcasted_iota(I32, (Q_BLOCK, Q_BLOCK), 1)
        d_lo = Q_BLOCK + r_i - c_i
        d_hi = r_i - c_i
        near_lo = (window(jnp.maximum(i - 1, 0)) != 0) & (d_lo < MAX_DISTANCE) & (i >= 1)
        near_hi = (window(i) != 0) & (d_hi >= 0) & (d_hi < MAX_DISTANCE)
        near_ref[:, 0:Q_BLOCK] = jnp.where(near_lo, 0.0, -jnp.inf).astype(near_ref.dtype)
        near_ref[:, Q_BLOCK:2 * Q_BLOCK] = jnp.where(near_hi, 0.0, -jnp.inf).astype(near_ref.dtype)


def _idx_select(proj, ki, wi, k_top):
    L = ki.shape[0]
    nq = L // Q_BLOCK
    n_sub_total = L // IDX_SUB
    qi_l, kj_l, last_l = [], [], []
    for i in range(nq):
        j_last = (i * Q_BLOCK + Q_BLOCK - 1) // IDX_KEY_BLOCK
        for j in range(j_last + 1):
            qi_l.append(i)
            kj_l.append(j)
            last_l.append(1 if j == j_last else 0)
    tabs = [jnp.asarray(np.asarray(t, np.int32)) for t in (qi_l, kj_l, last_l)]
    qcol = (DSA_HEADS * DSA_DH) // (IDX_HEADS * IDX_DIM)
    vmem = (L * Q_BLOCK * 4 + 2 * (Q_BLOCK * L * 2) + 2 * (Q_BLOCK * IDX_HEADS * IDX_DIM * 2)
            + 2 * IDX_KEY_BLOCK * IDX_DIM * 2 + (8 << 20))
    grid_spec = pltpu.PrefetchScalarGridSpec(
        num_scalar_prefetch=3,
        grid=(len(qi_l),),
        in_specs=[pl.BlockSpec((Q_BLOCK, IDX_HEADS * IDX_DIM), lambda s, qt, kt, lt: (qt[s], qcol)),
                  pl.BlockSpec((IDX_KEY_BLOCK, IDX_DIM), lambda s, qt, kt, lt: (kt[s], 0)),
                  pl.BlockSpec((Q_BLOCK, LANES), lambda s, qt, kt, lt: (qt[s], 0))],
        out_specs=[pl.BlockSpec((1, n_sub_total, Q_BLOCK, IDX_SUB), lambda s, qt, kt, lt: (qt[s], 0, 0, 0)),
                   pl.BlockSpec((Q_BLOCK, 2 * Q_BLOCK), lambda s, qt, kt, lt: (qt[s], 0))],
        scratch_shapes=[pltpu.VMEM((n_sub_total, Q_BLOCK, IDX_SUB), I32)],
    )
    return pl.pallas_call(
        functools.partial(_idx_kernel, k_top=k_top, n_sub_total=n_sub_total),
        grid_spec=grid_spec,
        out_shape=[jax.ShapeDtypeStruct((nq, n_sub_total, Q_BLOCK, IDX_SUB), BF16),
                   jax.ShapeDtypeStruct((L, 2 * Q_BLOCK), BF16)],
        compiler_params=_cparams(("arbitrary",), vmem),
        name="dsa_idx_select",
    )(*tabs, proj, ki, wi)


M_INIT = -1e30
FAR_SUBS = 2
LOG2E = math.log2(math.e)


def _attn_kernel(qi_tab, kj_tab, kind_tab, first_tab,
                 tab_ref, q_ref, kt_ref, vf_ref, klo_ref, khi_ref, vlo_ref, vhi_ref, far_ref, near_ref,
                 o_ref, m_scr, l_scr, acc_scr, b_scr, s_scr):
    s = pl.program_id(0)

    @pl.when(s == 0)
    def _():
        r_i = lax.broadcasted_iota(I32, (Q_BLOCK, 2 * Q_BLOCK), 0)
        c_i = lax.broadcasted_iota(I32, (Q_BLOCK, 2 * Q_BLOCK), 1)
        d = jnp.maximum(Q_BLOCK + r_i - c_i, 0)
        max_exact = N_BUCKETS // 2
        df = jnp.maximum(d, 1).astype(F32)
        large = max_exact + (jnp.log(df / max_exact) / math.log(MAX_DISTANCE / max_exact)
                             * (N_BUCKETS - max_exact)).astype(I32)
        large = jnp.minimum(large, N_BUCKETS - 1)
        bkt = jnp.where(d < max_exact, d, large)
        for h in range(DSA_HEADS):
            acc = jnp.zeros((Q_BLOCK, 2 * Q_BLOCK), F32)
            for b in range(N_BUCKETS):
                acc = jnp.where(bkt == b, (tab_ref[b, h] - tab_ref[N_BUCKETS - 1, h]) * LOG2E, acc)
            b_scr[h] = acc

    @pl.when(first_tab[s] == 1)
    def _():
        m_scr[...] = jnp.full_like(m_scr, M_INIT)
        l_scr[...] = jnp.zeros_like(l_scr)
        acc_scr[...] = jnp.zeros_like(acc_scr)

    def softmax_pv(v_ref, width, row0=0):
        nt = width // LANES
        rows = slice(row0, row0 + width)
        ones = jnp.ones((width, DSA_DH), BF16)
        for h in range(DSA_HEADS):
            g = h // DSA_GROUP
            tiles = [s_scr[h, :, t * LANES:(t + 1) * LANES] for t in range(nt)]
            tmax = tiles[0]
            for t in range(1, nt):
                tmax = jnp.maximum(tmax, tiles[t])
            m_prev = m_scr[h]
            m_new = jnp.maximum(m_prev, jnp.max(tmax, axis=-1, keepdims=True))
            alpha = jnp.exp2(m_prev - m_new)
            p = [jnp.exp2(tiles[t] - m_new).astype(BF16) for t in range(nt)]
            pb = jnp.concatenate(p, axis=-1) if nt > 1 else p[0]
            v_aug = jnp.concatenate([v_ref[rows, g * DSA_DH:(g + 1) * DSA_DH], ones], axis=-1)
            pv = jnp.dot(pb, v_aug, preferred_element_type=F32)
            acc_scr[h] = alpha * acc_scr[h] + pv[:, :DSA_DH]
            l_scr[h] = alpha * l_scr[h] + pv[:, DSA_DH:]
            m_scr[h] = m_new

    @pl.when(kind_tab[s] == 0)
    def _():
        eye = (lax.broadcasted_iota(I32, (Q_BLOCK, Q_BLOCK), 0)
               == lax.broadcasted_iota(I32, (Q_BLOCK, Q_BLOCK), 1)).astype(BF16)
        for sub in range(FAR_SUBS):
            cols = slice(sub * IDX_SUB, (sub + 1) * IDX_SUB)
            mask = far_ref[0, sub]
            for g in range(DSA_KV_HEADS):
                lhs = jnp.concatenate(
                    [jnp.concatenate([q_ref[:, h * DSA_DH:(h + 1) * DSA_DH], eye], axis=1)
                     for h in range(g * DSA_GROUP, (g + 1) * DSA_GROUP)], axis=0)
                rhs = jnp.concatenate([kt_ref[g * DSA_DH:(g + 1) * DSA_DH, cols], mask], axis=0)
                logits = jnp.dot(lhs, rhs, preferred_element_type=F32)
                for hh in range(DSA_GROUP):
                    s_scr[g * DSA_GROUP + hh] = logits[hh * Q_BLOCK:(hh + 1) * Q_BLOCK, :]
            softmax_pv(vf_ref, IDX_SUB, sub * IDX_SUB)

    @pl.when(kind_tab[s] == 1)
    def _():
        for half, (k_ref, v_ref) in enumerate(((klo_ref, vlo_ref), (khi_ref, vhi_ref))):
            cols = slice(half * Q_BLOCK, (half + 1) * Q_BLOCK)
            mask = near_ref[:, cols].astype(F32)
            for h in range(DSA_HEADS):
                g = h // DSA_GROUP
                logits = lax.dot_general(q_ref[:, h * DSA_DH:(h + 1) * DSA_DH],
                                         k_ref[:, g * DSA_DH:(g + 1) * DSA_DH], _NT, preferred_element_type=F32)
                s_scr[h, :, 0:Q_BLOCK] = logits + b_scr[h][:, cols] + mask
            softmax_pv(v_ref, Q_BLOCK)
        for h in range(DSA_HEADS):
            o_ref[:, h * DSA_DH:(h + 1) * DSA_DH] = (acc_scr[h] / l_scr[h]).astype(o_ref.dtype)


def _masked_attention(proj, far, near, rel_bias):
    L = proj.shape[0]
    nq = L // Q_BLOCK
    far_keys = FAR_SUBS * IDX_SUB
    per = far_keys // Q_BLOCK
    qi_l, kj_l, kind_l, first_l = [], [], [], []
    for i in range(nq):
        n_far = -(-i // per)
        for j in range(n_far):
            qi_l.append(i); kj_l.append(j); kind_l.append(0); first_l.append(1 if j == 0 else 0)
        qi_l.append(i); kj_l.append(max(n_far - 1, 0)); kind_l.append(1); first_l.append(1 if n_far == 0 else 0)
    tabs = [jnp.asarray(np.asarray(t, np.int32)) for t in (qi_l, kj_l, kind_l, first_l)]
    qw = DSA_HEADS * DSA_DH
    kvw = DSA_KV_HEADS * DSA_DH
    k_col = (2 * qw) // kvw
    v_col = k_col + 1
    hw = DSA_HEADS
    vmem = (2 * (Q_BLOCK * qw * 2 * 2 + 2 * far_keys * kvw * 2 + 4 * Q_BLOCK * kvw * 2
                 + Q_BLOCK * far_keys * 2 + Q_BLOCK * 2 * Q_BLOCK * 2)
            + hw * Q_BLOCK * (3 * LANES + 2 * Q_BLOCK) * 4 + (16 << 20))
    idx = lambda f: (lambda s, qt, kt, kd, ft: f(qt[s], kt[s]))
    k_t = proj[:, k_col * kvw:(k_col + 1) * kvw].T
    grid_spec = pltpu.PrefetchScalarGridSpec(
        num_scalar_prefetch=4,
        grid=(len(qi_l),),
        in_specs=[pl.BlockSpec(memory_space=pltpu.SMEM),
                  pl.BlockSpec((Q_BLOCK, qw), idx(lambda i, j: (i, 0))),
                  pl.BlockSpec((kvw, far_keys), idx(lambda i, j: (0, j))),
                  pl.BlockSpec((far_keys, kvw), idx(lambda i, j: (j, v_col))),
                  pl.BlockSpec((Q_BLOCK, kvw), idx(lambda i, j: (jnp.maximum(i - 1, 0), k_col))),
                  pl.BlockSpec((Q_BLOCK, kvw), idx(lambda i, j: (i, k_col))),
                  pl.BlockSpec((Q_BLOCK, kvw), idx(lambda i, j: (jnp.maximum(i - 1, 0), v_col))),
                  pl.BlockSpec((Q_BLOCK, kvw), idx(lambda i, j: (i, v_col))),
                  pl.BlockSpec((1, FAR_SUBS, Q_BLOCK, IDX_SUB), idx(lambda i, j: (i, j, 0, 0))),
                  pl.BlockSpec((Q_BLOCK, 2 * Q_BLOCK), idx(lambda i, j: (i, 0)))],
        out_specs=pl.BlockSpec((Q_BLOCK, qw), idx(lambda i, j: (i, 0))),
        scratch_shapes=[pltpu.VMEM((hw, Q_BLOCK, LANES), F32),
                        pltpu.VMEM((hw, Q_BLOCK, LANES), F32),
                        pltpu.VMEM((hw, Q_BLOCK, DSA_DH), F32),
                        pltpu.VMEM((hw, Q_BLOCK, 2 * Q_BLOCK), F32),
                        pltpu.VMEM((hw, Q_BLOCK, IDX_SUB), F32)],
    )
    return pl.pallas_call(
        _attn_kernel,
        grid_spec=grid_spec,
        out_shape=jax.ShapeDtypeStruct((L, qw), BF16),
        compiler_params=_cparams(("arbitrary",), vmem),
        name="dsa_attention",
    )(*tabs, rel_bias, proj, k_t, proj, proj, proj, proj, proj, far, near)


def _dsa_mixer(xb, w_in, ln_g, ln_b, rel_bias):
    L = xb.shape[0]
    k_top = min(TOPK_MAX, L // 4)
    sq = DSA_HEADS * DSA_DH
    skv = DSA_KV_HEADS * DSA_DH
    si = IDX_HEADS * IDX_DIM
    w_q = w_in[:, :sq]
    w_k = w_in[:, sq:sq + skv]
    w_v = w_in[:, sq + skv:sq + 2 * skv]
    w_qi = w_in[:, sq + 2 * skv:sq + 2 * skv + si]
    w_ki = w_in[:, sq + 2 * skv + si:sq + 2 * skv + si + IDX_DIM]
    w_wi = w_in[:, sq + 2 * skv + si + IDX_DIM:]
    w_main = jnp.concatenate([w_q, w_qi, w_k, w_v], axis=1).astype(BF16)
    colscale = jnp.concatenate([jnp.full((1, sq), DSA_DH ** -0.5 * LOG2E, F32),
                                jnp.ones((1, si + 2 * skv), F32)], axis=1)
    w_small = jnp.concatenate([w_ki, w_wi, jnp.zeros((D_MODEL, LANES - IDX_HEADS), F32)], axis=1).astype(BF16)
    proj = _matmul_scaled(xb, w_main, colscale, BF16, tm=1024, tn=1024)
    ki, wi = _dsa_small(xb, w_small, ln_g, ln_b, tm=1024)
    far, near = _idx_select(proj, ki, wi, k_top)
    return _masked_attention(proj, far, near, rel_bias)


def kernel(x, p, gdn_w_in, gdn_conv_w, gdn_a_log, gdn_dt_bias, gdn_norm_g, gdn_w_o, dsa_w_in, dsa_kidx_ln_g, dsa_kidx_ln_b, dsa_w_o, rel_bias, ln1_g, ln1_b, ffn_w_gate, ffn_w_up, ffn_conv_w, ffn_w_down, ln2_g, ln2_b, ple_w_proj, ple_w_gate):
    assert x.shape[0] == 1 and x.shape[2] == D_MODEL
    xf = x[0]
    xb = xf.astype(BF16)
    ia = ib = 0
    for i in range(DEPTH):
        if i % 2 == 0:
            mix = _gdn_mixer(xb, gdn_w_in[ia], gdn_conv_w[ia], gdn_a_log[ia], gdn_dt_bias[ia], gdn_norm_g[ia])
            w_o = gdn_w_o[ia]
            ia += 1
        else:
            mix = _dsa_mixer(xb, dsa_w_in[ib], dsa_kidx_ln_g[ib], dsa_kidx_ln_b[ib], rel_bias)
            w_o = dsa_w_o[ib]
            ib += 1
        xf, xb = _proj_res_ln(mix, w_o.astype(BF16), xf, ln1_g[i], ln1_b[i], tm=512, sub=256)
        hmid = _ffn_up(xb, ffn_w_gate[i].astype(BF16), ffn_w_up[i].astype(BF16), ffn_conv_w[i], tm=1024, tn=512)
        xf, xb = _proj_res_ln(hmid, ffn_w_down[i].astype(BF16), xf, ln2_g[i], ln2_b[i], tm=512, sub=256)
        xf, xb = _ple(xb, xf, ple_w_gate[i].astype(BF16), p[i, 0], ple_w_proj[i].astype(BF16), tm=1024, tn=1024)
    return xf[None]
```

```python
import functools
import math

import jax
import jax.numpy as jnp
import numpy as np
from jax import lax
from jax.experimental import pallas as pl
from jax.experimental.pallas import tpu as pltpu

F32 = jnp.float32
BF16 = jnp.bfloat16
I32 = jnp.int32

D_MODEL = 2048
GDN_QK_HEADS = 16
GDN_V_HEADS = 32
GDN_DK = 128
GDN_DV = 128
GDN_CONV = 4
GDN_CHUNK = 64
GDN_QK_W = GDN_QK_HEADS * GDN_DK
GDN_V_W = GDN_V_HEADS * GDN_DV
DSA_HEADS = 16
DSA_KV_HEADS = 4
DSA_GROUP = DSA_HEADS // DSA_KV_HEADS
DSA_DH = 128
IDX_HEADS = 16
IDX_DIM = 128
TOPK_MAX = 256
N_BUCKETS = 32
MAX_DISTANCE = 128
D_FF = 5120
FFN_CONV = 3
PLE_DIM = 256
DEPTH = 2
DN_ALPHA = (2.0 * DEPTH) ** 0.25
LN_EPS = 1e-5
RMS_EPS = 1e-6

V7X_VMEM_BYTES = 64 * 1024 * 1024
V7X_VMEM_BUDGET = 56 * 1024 * 1024
LANES = 128
BF16_SUBLANES = 16

HALO = BF16_SUBLANES
Q_BLOCK = 128
IDX_KEY_BLOCK = 2048
IDX_SUB = 512
INT_MIN = -(2 ** 31)
BITS_PER_CHECK = 4
FAR_MASKED = -2e30

_NT = (((1,), (1,)), ((), ()))
_TN = (((0,), (0,)), ((), ()))


def _cparams(semantics, vmem_bytes):
    return pltpu.CompilerParams(dimension_semantics=semantics,
                                vmem_limit_bytes=int(min(V7X_VMEM_BUDGET, vmem_bytes)))


def _silu(y):
    return y * jax.nn.sigmoid(y)


def _mm_scale_kernel(x_ref, w_ref, cs_ref, o_ref):
    acc = jnp.dot(x_ref[...], w_ref[...], preferred_element_type=F32)
    o_ref[...] = (acc * cs_ref[...]).astype(o_ref.dtype)


def _matmul_scaled(x, w, colscale, out_dtype, tm, tn):
    M, K = x.shape
    N = w.shape[1]
    tm, tn = min(tm, M), min(tn, N)
    osz = jnp.dtype(out_dtype).itemsize
    vmem = 2 * (tm * K * 2 + K * tn * 2 + tm * tn * osz) + 2 * tm * tn * 4
    return pl.pallas_call(
        _mm_scale_kernel,
        grid=(M // tm, N // tn),
        in_specs=[pl.BlockSpec((tm, K), lambda i, j: (i, 0)),
                  pl.BlockSpec((K, tn), lambda i, j: (0, j)),
                  pl.BlockSpec((1, tn), lambda i, j: (0, j))],
        out_specs=pl.BlockSpec((tm, tn), lambda i, j: (i, j)),
        out_shape=jax.ShapeDtypeStruct((M, N), out_dtype),
        compiler_params=_cparams(("parallel", "parallel"), vmem),
        name="matmul_scaled",
    )(x, w, colscale)


CONV_SUB = 256
FFN_SUB = 512


def _causal_conv(g, gh, cw_ref, g_scr, kc, tm, cols):
    g_scr[0:HALO, cols] = gh
    g_scr[HALO:HALO + tm, cols] = g
    y = cw_ref[kc - 1:kc, cols] * g
    for j in range(kc - 1):
        off = HALO - (kc - 1) + j
        y = y + cw_ref[j:j + 1, cols] * g_scr[off:off + tm, cols]
    return y


def _mm_conv_silu_kernel(x_ref, xh_ref, w_ref, cw_ref, o_ref, g_scr, *, kc, tm):
    first = pl.program_id(0) == 0
    for c0 in range(0, o_ref.shape[1], CONV_SUB):
        cols = slice(c0, c0 + CONV_SUB)
        w = w_ref[:, cols]
        gh = jnp.dot(xh_ref[...], w, preferred_element_type=F32)
        gh = jnp.where(first, 0.0, gh)
        g = jnp.dot(x_ref[...], w, preferred_element_type=F32)
        y = _causal_conv(g, gh, cw_ref, g_scr, kc, tm, cols)
        o_ref[:, cols] = _silu(y).astype(o_ref.dtype)


def _proj_conv_silu(x, w, conv_w, tm, tn):
    M, K = x.shape
    N = w.shape[1]
    kc = conv_w.shape[0]
    tm, tn = min(tm, M), min(tn, N)
    hb = tm // HALO
    vmem = 2 * (tm * K * 2 + HALO * K * 2 + K * tn * 2 + tm * tn * 2) + 4 * tm * tn * 4
    return pl.pallas_call(
        functools.partial(_mm_conv_silu_kernel, kc=kc, tm=tm),
        grid=(M // tm, N // tn),
        in_specs=[pl.BlockSpec((tm, K), lambda i, j: (i, 0)),
                  pl.BlockSpec((HALO, K), lambda i, j: (jnp.maximum(i * hb - 1, 0), 0)),
                  pl.BlockSpec((K, tn), lambda i, j: (0, j)),
                  pl.BlockSpec((kc, tn), lambda i, j: (0, j))],
        out_specs=pl.BlockSpec((tm, tn), lambda i, j: (i, j)),
        out_shape=jax.ShapeDtypeStruct((M, N), BF16),
        scratch_shapes=[pltpu.VMEM((tm + HALO, tn), F32)],
        compiler_params=_cparams(("parallel", "parallel"), vmem),
        name="proj_conv_silu",
    )(x, x, w, conv_w)


def _ffn_up_kernel(x_ref, xh_ref, wg_ref, wu_ref, cw_ref, o_ref, g_scr, *, kc, tm):
    first = pl.program_id(0) == 0
    for c0 in range(0, o_ref.shape[1], FFN_SUB):
        cols = slice(c0, c0 + FFN_SUB)
        wg = wg_ref[:, cols]
        g = jnp.dot(x_ref[...], wg, preferred_element_type=F32)
        gh = jnp.dot(xh_ref[...], wg, preferred_element_type=F32)
        gh = jnp.where(first, 0.0, gh)
        u = jnp.dot(x_ref[...], wu_ref[:, cols], preferred_element_type=F32)
        y = _causal_conv(g, gh, cw_ref, g_scr, kc, tm, cols)
        o_ref[:, cols] = (_silu(y) * u).astype(o_ref.dtype)


def _ffn_up(x, w_gate, w_up, conv_w, tm, tn):
    M, K = x.shape
    N = w_gate.shape[1]
    kc = conv_w.shape[0]
    tm, tn = min(tm, M), min(tn, N)
    hb = tm // HALO
    vmem = 2 * (tm * K * 2 + HALO * K * 2 + 2 * K * tn * 2 + tm * tn * 2) + 6 * tm * tn * 4
    return pl.pallas_call(
        functools.partial(_ffn_up_kernel, kc=kc, tm=tm),
        grid=(M // tm, N // tn),
        in_specs=[pl.BlockSpec((tm, K), lambda i, j: (i, 0)),
                  pl.BlockSpec((HALO, K), lambda i, j: (jnp.maximum(i * hb - 1, 0), 0)),
                  pl.BlockSpec((K, tn), lambda i, j: (0, j)),
                  pl.BlockSpec((K, tn), lambda i, j: (0, j)),
                  pl.BlockSpec((kc, tn), lambda i, j: (0, j))],
        out_specs=pl.BlockSpec((tm, tn), lambda i, j: (i, j)),
        out_shape=jax.ShapeDtypeStruct((M, N), BF16),
        scratch_shapes=[pltpu.VMEM((tm + HALO, tn), F32)],
        compiler_params=_cparams(("parallel", "parallel"), vmem),
        name="ffn_up",
    )(x, x, w_gate, w_up, conv_w)


def _mm_res_ln_kernel(a_ref, w_ref, res_ref, g_ref, b_ref, of_ref, ob_ref, *, sub):
    for r0 in range(0, a_ref.shape[0], sub):
        rows = slice(r0, r0 + sub)
        acc = jnp.dot(a_ref[rows, :], w_ref[...], preferred_element_type=F32)
        y = DN_ALPHA * res_ref[rows, :] + acc
        mu = jnp.mean(y, axis=-1, keepdims=True)
        yc = y - mu
        var = jnp.mean(yc * yc, axis=-1, keepdims=True)
        out = yc * lax.rsqrt(var + LN_EPS) * g_ref[...] + b_ref[...]
        of_ref[rows, :] = out
        ob_ref[rows, :] = out.astype(BF16)


def _proj_res_ln(a, w, res, g, b, tm, sub):
    M, K = a.shape
    N = w.shape[1]
    tm = min(tm, M)
    sub = min(sub, tm)
    vmem = K * N * 2 + 2 * (tm * K * 2 + tm * N * 4 + tm * N * 4 + tm * N * 2) + 4 * sub * N * 4
    return pl.pallas_call(
        functools.partial(_mm_res_ln_kernel, sub=sub),
        grid=(M // tm,),
        in_specs=[pl.BlockSpec((tm, K), lambda i: (i, 0)),
                  pl.BlockSpec((K, N), lambda i: (0, 0), pipeline_mode=pl.Buffered(1)),
                  pl.BlockSpec((tm, N), lambda i: (i, 0)),
                  pl.BlockSpec((1, N), lambda i: (0, 0)),
                  pl.BlockSpec((1, N), lambda i: (0, 0))],
        out_specs=[pl.BlockSpec((tm, N), lambda i: (i, 0)),
                   pl.BlockSpec((tm, N), lambda i: (i, 0))],
        out_shape=[jax.ShapeDtypeStruct((M, N), F32), jax.ShapeDtypeStruct((M, N), BF16)],
        compiler_params=_cparams(("parallel",), vmem),
        name="proj_res_ln",
    )(a, w, res, g.reshape(1, N), b.reshape(1, N))


def _ple_kernel(xb_ref, wg_ref, p_ref, wp_ref, xr_ref, of_ref, ob_ref):
    gate = jax.nn.sigmoid(jnp.dot(xb_ref[...], wg_ref[...], preferred_element_type=F32))
    pe = jnp.dot(p_ref[...].astype(BF16), wp_ref[...], preferred_element_type=F32)
    out = xr_ref[...] + gate * pe
    of_ref[...] = out
    ob_ref[...] = out.astype(BF16)


def _ple(xb, xf, w_gate, p, w_proj, tm, tn):
    M, K = xb.shape
    N = w_gate.shape[1]
    P = p.shape[1]
    tm, tn = min(tm, M), min(tn, N)
    vmem = 2 * (tm * K * 2 + K * tn * 2 + tm * P * 4 + P * tn * 2 + tm * tn * 10) + 4 * tm * tn * 4
    return pl.pallas_call(
        _ple_kernel,
        grid=(M // tm, N // tn),
        in_specs=[pl.BlockSpec((tm, K), lambda i, j: (i, 0)),
                  pl.BlockSpec((K, tn), lambda i, j: (0, j)),
                  pl.BlockSpec((tm, P), lambda i, j: (i, 0)),
                  pl.BlockSpec((P, tn), lambda i, j: (0, j)),
                  pl.BlockSpec((tm, tn), lambda i, j: (i, j))],
        out_specs=[pl.BlockSpec((tm, tn), lambda i, j: (i, j)),
                   pl.BlockSpec((tm, tn), lambda i, j: (i, j))],
        out_shape=[jax.ShapeDtypeStruct((M, N), F32), jax.ShapeDtypeStruct((M, N), BF16)],
        compiler_params=_cparams(("parallel", "parallel"), vmem),
        name="ple",
    )(xb, w_gate, p, w_proj, xf)


GDN_HB = 8
GDN_NC = 4


def _gdn_kernel(q_ref, k_ref, v_ref, z_ref, sc_ref, hp_ref, ng_ref, o_ref, s_scr):
    C = GDN_CHUNK

    @pl.when(pl.program_id(1) == 0)
    def _():
        s_scr[...] = jnp.zeros_like(s_scr)

    row = lax.broadcasted_iota(I32, (C, C), 0)
    col = lax.broadcasted_iota(I32, (C, C), 1)
    tri = row >= col
    strict = row > col
    eye = row == col
    tri_f = tri.astype(F32)
    eye_f = eye.astype(F32)

    raw = sc_ref[...]
    a_log = hp_ref[0, 0:1, :]
    dt_b = hp_ref[0, 1:2, :]
    xs = raw + dt_b
    softplus = jnp.maximum(xs, 0.0) + jnp.log1p(jnp.exp(-jnp.abs(xs)))
    g_all = -jnp.exp(a_log) * softplus
    beta_all = jax.nn.sigmoid(raw)
    ng = ng_ref[...]

    units = [(c, j) for c in range(GDN_NC) for j in range(GDN_HB)]
    kb_l, rhs_l, decay_l, qd_l, kd_l, kbf_l, qbf_l, gl_l = [], [], [], [], [], [], [], []
    for c in range(GDN_NC):
        r0 = c * C
        gc = jnp.dot(tri_f, g_all[r0:r0 + C, :], precision=lax.Precision.HIGHEST,
                     preferred_element_type=F32)
        g_last = gc[C - 1:C, :]
        e_gc = jnp.exp(gc)
        e_rest = jnp.exp(g_last - gc)
        e_last = jnp.exp(g_last)
        beta_c = beta_all[r0:r0 + C, :]
        qn, kn = [], []
        for hq in range(GDN_HB // 2):
            qf = q_ref[r0:r0 + C, hq * GDN_DK:(hq + 1) * GDN_DK].astype(F32)
            kf = k_ref[r0:r0 + C, hq * GDN_DK:(hq + 1) * GDN_DK].astype(F32)
            qn.append(qf * lax.rsqrt(jnp.sum(qf * qf, axis=-1, keepdims=True) + RMS_EPS) * (GDN_DK ** -0.5))
            kn.append(kf * lax.rsqrt(jnp.sum(kf * kf, axis=-1, keepdims=True) + RMS_EPS))
        for j in range(GDN_HB):
            q_h, k_h = qn[j // 2], kn[j // 2]
            vf = v_ref[r0:r0 + C, j * GDN_DV:(j + 1) * GDN_DV].astype(F32)
            beta = beta_c[:, GDN_HB + j:GDN_HB + j + 1]
            kb = k_h * beta
            gcb = jnp.broadcast_to(gc[:, j:j + 1], (C, C))
            gcr = jnp.sum(jnp.where(eye, gcb, 0.0), axis=0, keepdims=True)
            decay_l.append(jnp.where(tri, jnp.exp(jnp.where(tri, gcb - gcr, 0.0)), 0.0))
            kb_l.append(kb.astype(BF16))
            rhs_l.append(jnp.concatenate([vf * beta, kb * e_gc[:, j:j + 1]], axis=-1).astype(BF16))
            qd_l.append((q_h * e_gc[:, j:j + 1]).astype(BF16))
            kd_l.append((k_h * e_rest[:, j:j + 1]).astype(BF16))
            kbf_l.append(k_h.astype(BF16))
            qbf_l.append(q_h.astype(BF16))
            gl_l.append(e_last[:, j:j + 1])

    n_u = len(units)
    kk_l = [lax.dot_general(kb_l[u], kbf_l[u], _NT, preferred_element_type=F32) for u in range(n_u)]
    qk_l = [lax.dot_general(qbf_l[u], kbf_l[u], _NT, preferred_element_type=F32) for u in range(n_u)]
    qk_l = [jnp.where(tri, qk_l[u] * decay_l[u], 0.0).astype(BF16) for u in range(n_u)]
    x_l = [(-jnp.where(strict, kk_l[u] * decay_l[u], 0.0)) for u in range(n_u)]
    t_l = [eye_f + x_l[u] for u in range(n_u)]
    x_l = [x.astype(BF16) for x in x_l]
    for _ in range(5):
        x_l = [jnp.dot(x, x, preferred_element_type=F32).astype(BF16) for x in x_l]
        t_l = [t + jnp.dot(t.astype(BF16), x, preferred_element_type=F32) for t, x in zip(t_l, x_l)]
    sol_l = [jnp.dot(t_l[u].astype(BF16), rhs_l[u], preferred_element_type=F32) for u in range(n_u)]

    s_cur = [s_scr[j] for j in range(GDN_HB)]
    for c in range(GDN_NC):
        r0 = c * C
        us = [c * GDN_HB + j for j in range(GDN_HB)]
        s_bf = [s.astype(BF16) for s in s_cur]
        ws_l = [jnp.dot(sol_l[u][:, GDN_DV:].astype(BF16), s_bf[j], preferred_element_type=F32)
                for j, u in enumerate(us)]
        qs_l = [jnp.dot(qd_l[u], s_bf[j], preferred_element_type=F32) for j, u in enumerate(us)]
        vn_l = [(sol_l[u][:, :GDN_DV] - ws_l[j]).astype(BF16) for j, u in enumerate(us)]
        kv_l = [lax.dot_general(kd_l[u], vn_l[j], _TN, preferred_element_type=F32) for j, u in enumerate(us)]
        ov_l = [jnp.dot(qk_l[u], vn_l[j], preferred_element_type=F32) for j, u in enumerate(us)]
        s_cur = [s_cur[j] * gl_l[u] + kv_l[j] for j, u in enumerate(us)]
        for j in range(GDN_HB):
            o = qs_l[j] + ov_l[j]
            zf = z_ref[r0:r0 + C, j * GDN_DV:(j + 1) * GDN_DV].astype(F32)
            o = o * lax.rsqrt(jnp.mean(o * o, axis=-1, keepdims=True) + RMS_EPS) * ng * _silu(zf)
            o_ref[r0:r0 + C, j * GDN_DV:(j + 1) * GDN_DV] = o.astype(o_ref.dtype)
    for j in range(GDN_HB):
        s_scr[j] = s_cur[j]


def _gdn_core(qkv, z, scal, hparams, norm_g):
    L = qkv.shape[0]
    G = GDN_V_HEADS // GDN_HB
    R = GDN_NC * GDN_CHUNK
    qw = (GDN_HB // 2) * GDN_DK
    vw = GDN_HB * GDN_DV
    k_blk0 = GDN_QK_W // qw
    v_blk0 = 2 * GDN_QK_W // vw
    vmem = 2 * (2 * R * qw * 2 + 2 * R * vw * 2 + R * LANES * 4 + R * vw * 2) + (16 << 20)
    return pl.pallas_call(
        _gdn_kernel,
        grid=(G, L // R),
        in_specs=[pl.BlockSpec((R, qw), lambda g, s: (s, g)),
                  pl.BlockSpec((R, qw), lambda g, s: (s, k_blk0 + g)),
                  pl.BlockSpec((R, vw), lambda g, s: (s, v_blk0 + g)),
                  pl.BlockSpec((R, vw), lambda g, s: (s, g)),
                  pl.BlockSpec((R, LANES), lambda g, s: (s, g)),
                  pl.BlockSpec((1, 8, LANES), lambda g, s: (g, 0, 0)),
                  pl.BlockSpec((1, GDN_DV), lambda g, s: (0, 0))],
        out_specs=pl.BlockSpec((R, vw), lambda g, s: (s, g)),
        out_shape=jax.ShapeDtypeStruct((L, GDN_V_W), BF16),
        scratch_shapes=[pltpu.VMEM((GDN_HB, GDN_DK, GDN_DV), F32)],
        compiler_params=_cparams(("parallel", "arbitrary"), vmem),
        name="gdn_core",
    )(qkv, qkv, qkv, z, scal, hparams, norm_g.reshape(1, GDN_DV))


def _gdn_mixer(xb, w_in, conv_w, a_log, dt_bias, norm_g):
    L = xb.shape[0]
    nqkv = 2 * GDN_QK_W + GDN_V_W
    w_qkv = w_in[:, :nqkv].astype(BF16)
    w_z = w_in[:, nqkv:nqkv + GDN_V_W].astype(BF16)
    w_ab = w_in[:, nqkv + GDN_V_W:]
    G = GDN_V_HEADS // GDN_HB
    w_a = w_ab[:, :GDN_V_HEADS].reshape(D_MODEL, G, GDN_HB)
    w_b = w_ab[:, GDN_V_HEADS:].reshape(D_MODEL, G, GDN_HB)
    w_sc = jnp.concatenate([w_a, w_b, jnp.zeros((D_MODEL, G, LANES - 2 * GDN_HB), F32)], axis=-1)
    w_sc = w_sc.reshape(D_MODEL, G * LANES).astype(BF16)
    qkv = _proj_conv_silu(xb, w_qkv, conv_w, tm=1024, tn=1024)
    ones_z = jnp.ones((1, GDN_V_W), F32)
    z = _matmul_scaled(xb, w_z, ones_z, BF16, tm=1024, tn=1024)
    scal = _matmul_scaled(xb, w_sc, jnp.ones((1, G * LANES), F32), F32, tm=1024, tn=G * LANES)
    hp = jnp.zeros((G, 8, LANES), F32)
    hp = hp.at[:, 0, :GDN_HB].set(a_log.reshape(G, GDN_HB))
    hp = hp.at[:, 1, :GDN_HB].set(dt_bias.reshape(G, GDN_HB))
    return _gdn_core(qkv, z, scal, hp, norm_g)


def _dsa_small_kernel(x_ref, w_ref, g_ref, b_ref, ki_ref, wi_ref):
    acc = jnp.dot(x_ref[...], w_ref[...], preferred_element_type=F32)
    ki = acc[:, :IDX_DIM]
    mu = jnp.mean(ki, axis=-1, keepdims=True)
    kc = ki - mu
    var = jnp.mean(kc * kc, axis=-1, keepdims=True)
    ki_ref[...] = (kc * lax.rsqrt(var + LN_EPS) * g_ref[...] + b_ref[...]).astype(ki_ref.dtype)
    wi_ref[...] = acc[:, IDX_DIM:] * ((IDX_HEADS ** -0.5) * (IDX_DIM ** -0.5))


def _dsa_small(xb, w_small, ln_g, ln_b, tm):
    M, K = xb.shape
    tm = min(tm, M)
    N = 2 * LANES
    vmem = 2 * (tm * K * 2 + K * N * 2 + tm * LANES * 6) + 4 * tm * N * 4
    return pl.pallas_call(
        _dsa_small_kernel,
        grid=(M // tm,),
        in_specs=[pl.BlockSpec((tm, K), lambda i: (i, 0)),
                  pl.BlockSpec((K, N), lambda i: (0, 0)),
                  pl.BlockSpec((1, IDX_DIM), lambda i: (0, 0)),
                  pl.BlockSpec((1, IDX_DIM), lambda i: (0, 0))],
        out_specs=[pl.BlockSpec((tm, IDX_DIM), lambda i: (i, 0)),
                   pl.BlockSpec((tm, LANES), lambda i: (i, 0))],
        out_shape=[jax.ShapeDtypeStruct((M, IDX_DIM), BF16), jax.ShapeDtypeStruct((M, LANES), F32)],
        compiler_params=_cparams(("parallel",), vmem),
        name="dsa_idx_proj",
    )(xb, w_small, ln_g.reshape(1, IDX_DIM), ln_b.reshape(1, IDX_DIM))


def _sortable_key(score):
    bits = lax.bitcast_convert_type(score, I32)
    return jnp.where(bits >= 0, bits, bits ^ jnp.int32(0x7FFFFFFF))


def _idx_kernel(qi_tab, kj_tab, last_tab,
                qidx_ref, kidx_ref, wi_ref, far_ref, near_ref, key_scr, *, k_top, n_sub_total):
    s = pl.program_id(0)
    i = qi_tab[s]
    j = kj_tab[s]
    nsub = IDX_KEY_BLOCK // IDX_SUB
    t_col = i * Q_BLOCK + lax.broadcasted_iota(I32, (Q_BLOCK, 1), 0)
    lane_sub = lax.broadcasted_iota(I32, (Q_BLOCK, IDX_SUB), 1)
    wi = wi_ref[...]

    for sub in range(nsub):
        ki_sub = kidx_ref[sub * IDX_SUB:(sub + 1) * IDX_SUB, :]
        acc = jnp.zeros((Q_BLOCK, IDX_SUB), F32)
        for h in range(IDX_HEADS):
            sc = lax.dot_general(qidx_ref[:, h * IDX_DIM:(h + 1) * IDX_DIM], ki_sub, _NT,
                                 preferred_element_type=F32)
            acc = acc + jnp.maximum(sc, 0.0) * wi[:, h:h + 1]
        s_idx = j * IDX_KEY_BLOCK + sub * IDX_SUB + lane_sub
        key_scr[j * nsub + sub] = jnp.where(s_idx <= t_col, _sortable_key(acc), INT_MIN)

    @pl.when(last_tab[s] == 1)
    def _():
        n_chunks = (i * Q_BLOCK + Q_BLOCK - 1) // IDX_SUB + 1

        def count(pred, ref_val):
            refb = jnp.broadcast_to(ref_val, (Q_BLOCK, LANES))

            def body(c, cnt):
                blk = key_scr[c]
                for l in range(IDX_SUB // LANES):
                    cnt = cnt + jnp.where(pred(blk[:, l * LANES:(l + 1) * LANES], refb), 1, 0)
                return cnt

            cnt = lax.fori_loop(0, n_chunks, body, jnp.zeros((Q_BLOCK, LANES), I32))
            return jnp.sum(cnt, axis=1, keepdims=True)

        def group_body(carry):
            grp, thr, n_ge, _ = carry
            for bb in range(BITS_PER_CHECK):
                cand = thr + jnp.left_shift(jnp.int32(1), 31 - (grp * BITS_PER_CHECK + bb))
                n_cand = count(lambda a, r: a >= r, cand)
                ok = n_cand >= k_top
                thr, n_ge = jnp.where(ok, cand, thr), jnp.where(ok, n_cand, n_ge)
            return grp + 1, thr, n_ge, jnp.max(jnp.where(n_ge != k_top, 1, 0))

        _, thr, n_ge, n_tied_rows = lax.while_loop(
            lambda carry: (carry[0] < 32 // BITS_PER_CHECK) & (carry[3] != 0), group_body,
            (jnp.int32(0), jnp.full((Q_BLOCK, 1), INT_MIN, I32),
             jnp.broadcast_to(n_chunks * IDX_SUB, (Q_BLOCK, 1)).astype(I32), jnp.int32(1)))
        thr_b = jnp.broadcast_to(thr, (Q_BLOCK, IDX_SUB))

        def emit(c, sel):
            s_idx = c * IDX_SUB + lane_sub
            far = sel & (t_col - s_idx >= MAX_DISTANCE)
            far_ref[0, c] = jnp.where(far, 0.0, FAR_MASKED).astype(far_ref.dtype)
            key_scr[c] = jnp.where(sel, 1, 0)

        @pl.when(n_tied_rows == 0)
        def _():
            def sel_body(c, carry):
                emit(c, key_scr[c] >= thr_b)
                return carry

            lax.fori_loop(0, n_chunks, sel_body, 0)

        @pl.when(n_tied_rows != 0)
        def _():
            n_gt = count(lambda a, r: a > r, thr)
            need_eq = (k_top - n_gt).astype(F32)
            incl = (lax.broadcasted_iota(I32, (IDX_SUB, IDX_SUB), 0)
                    <= lax.broadcasted_iota(I32, (IDX_SUB, IDX_SUB), 1)).astype(BF16)

            def sel_body(c, carry):
                blk = key_scr[c]
                eq = blk == thr_b
                eq_f = jnp.where(eq, 1.0, 0.0)
                rank = carry + jnp.dot(eq_f.astype(BF16), incl, preferred_element_type=F32)
                s_idx = c * IDX_SUB + lane_sub
                emit(c, ((blk > thr_b) | (eq & (rank <= need_eq))) & (s_idx <= t_col))
                return carry + jnp.sum(eq_f, axis=1, keepdims=True)

            lax.fori_loop(0, n_chunks, sel_body, jnp.zeros((Q_BLOCK, 1), F32))

        def fill_body(c, carry):
            far_ref[0, c] = jnp.full((Q_BLOCK, IDX_SUB), FAR_MASKED, far_ref.dtype)
            return carry

        lax.fori_loop(n_chunks, n_sub_total, fill_body, 0)

        def window(blk_idx):
            per = IDX_SUB // Q_BLOCK
            chunk = key_scr[blk_idx // per]
            m = blk_idx % per
            out = chunk[:, 0:Q_BLOCK]
            for q in range(1, per):
                out = jnp.where(m == q, chunk[:, q * Q_BLOCK:(q + 1) * Q_BLOCK], out)
            return out

        r_i = lax.broadcasted_iota(I32, (Q_BLOCK, Q_BLOCK), 0)
        c_i = lax.broadcasted_iota(I32, (Q_BLOCK, Q_BLOCK), 1)
        d_lo = Q_BLOCK + r_i - c_i
        d_hi = r_i - c_i
        near_lo = (window(jnp.maximum(i - 1, 0)) != 0) & (d_lo < MAX_DISTANCE) & (i >= 1)
        near_hi = (window(i) != 0) & (d_hi >= 0) & (d_hi < MAX_DISTANCE)
        near_ref[:, 0:Q_BLOCK] = jnp.where(near_lo, 0.0, -jnp.inf).astype(near_ref.dtype)
        near_ref[:, Q_BLOCK:2 * Q_BLOCK] = jnp.where(near_hi, 0.0, -jnp.inf).astype(near_ref.dtype)


def _idx_select(proj, ki, wi, k_top):
    L = ki.shape[0]
    nq = L // Q_BLOCK
    n_sub_total = L // IDX_SUB
    qi_l, kj_l, last_l = [], [], []
    for i in range(nq):
        j_last = (i * Q_BLOCK + Q_BLOCK - 1) // IDX_KEY_BLOCK
        for j in range(j_last + 1):
            qi_l.append(i)
            kj_l.append(j)
            last_l.append(1 if j == j_last else 0)
    tabs = [jnp.asarray(np.asarray(t, np.int32)) for t in (qi_l, kj_l, last_l)]
    qcol = (DSA_HEADS * DSA_DH) // (IDX_HEADS * IDX_DIM)
    vmem = (L * Q_BLOCK * 4 + 2 * (Q_BLOCK * L * 2) + 2 * (Q_BLOCK * IDX_HEADS * IDX_DIM * 2)
            + 2 * IDX_KEY_BLOCK * IDX_DIM * 2 + (8 << 20))
    grid_spec = pltpu.PrefetchScalarGridSpec(
        num_scalar_prefetch=3,
        grid=(len(qi_l),),
        in_specs=[pl.BlockSpec((Q_BLOCK, IDX_HEADS * IDX_DIM), lambda s, qt, kt, lt: (qt[s], qcol)),
                  pl.BlockSpec((IDX_KEY_BLOCK, IDX_DIM), lambda s, qt, kt, lt: (kt[s], 0)),
                  pl.BlockSpec((Q_BLOCK, LANES), lambda s, qt, kt, lt: (qt[s], 0))],
        out_specs=[pl.BlockSpec((1, n_sub_total, Q_BLOCK, IDX_SUB), lambda s, qt, kt, lt: (qt[s], 0, 0, 0)),
                   pl.BlockSpec((Q_BLOCK, 2 * Q_BLOCK), lambda s, qt, kt, lt: (qt[s], 0))],
        scratch_shapes=[pltpu.VMEM((n_sub_total, Q_BLOCK, IDX_SUB), I32)],
    )
    return pl.pallas_call(
        functools.partial(_idx_kernel, k_top=k_top, n_sub_total=n_sub_total),
        grid_spec=grid_spec,
        out_shape=[jax.ShapeDtypeStruct((nq, n_sub_total, Q_BLOCK, IDX_SUB), BF16),
                   jax.ShapeDtypeStruct((L, 2 * Q_BLOCK), BF16)],
        compiler_params=_cparams(("arbitrary",), vmem),
        name="dsa_idx_select",
    )(*tabs, proj, ki, wi)


M_INIT = -1e30
FAR_SUBS = 2
LOG2E = math.log2(math.e)


def _attn_kernel(qi_tab, kj_tab, kind_tab, first_tab,
                 tab_ref, q_ref, kt_ref, vf_ref, klo_ref, khi_ref, vlo_ref, vhi_ref, far_ref, near_ref,
                 o_ref, m_scr, l_scr, acc_scr, b_scr, s_scr):
    s = pl.program_id(0)

    @pl.when(s == 0)
    def _():
        r_i = lax.broadcasted_iota(I32, (Q_BLOCK, 2 * Q_BLOCK), 0)
        c_i = lax.broadcasted_iota(I32, (Q_BLOCK, 2 * Q_BLOCK), 1)
        d = jnp.maximum(Q_BLOCK + r_i - c_i, 0)
        max_exact = N_BUCKETS // 2
        df = jnp.maximum(d, 1).astype(F32)
        large = max_exact + (jnp.log(df / max_exact) / math.log(MAX_DISTANCE / max_exact)
                             * (N_BUCKETS - max_exact)).astype(I32)
        large = jnp.minimum(large, N_BUCKETS - 1)
        bkt = jnp.where(d < max_exact, d, large)
        for h in range(DSA_HEADS):
            acc = jnp.zeros((Q_BLOCK, 2 * Q_BLOCK), F32)
            for b in range(N_BUCKETS):
                acc = jnp.where(bkt == b, (tab_ref[b, h] - tab_ref[N_BUCKETS - 1, h]) * LOG2E, acc)
            b_scr[h] = acc

    @pl.when(first_tab[s] == 1)
    def _():
        m_scr[...] = jnp.full_like(m_scr, M_INIT)
        l_scr[...] = jnp.zeros_like(l_scr)
        acc_scr[...] = jnp.zeros_like(acc_scr)

    def softmax_pv(v_ref, width, row0=0):
        nt = width // LANES
        rows = slice(row0, row0 + width)
        ones = jnp.ones((width, DSA_DH), BF16)
        for h in range(DSA_HEADS):
            g = h // DSA_GROUP
            tiles = [s_scr[h, :, t * LANES:(t + 1) * LANES] for t in range(nt)]
            tmax = tiles[0]
            for t in range(1, nt):
                tmax = jnp.maximum(tmax, tiles[t])
            m_prev = m_scr[h]
            m_new = jnp.maximum(m_prev, jnp.max(tmax, axis=-1, keepdims=True))
            alpha = jnp.exp2(m_prev - m_new)
            p = [jnp.exp2(tiles[t] - m_new).astype(BF16) for t in range(nt)]
            pb = jnp.concatenate(p, axis=-1) if nt > 1 else p[0]
            v_aug = jnp.concatenate([v_ref[rows, g * DSA_DH:(g + 1) * DSA_DH], ones], axis=-1)
            pv = jnp.dot(pb, v_aug, preferred_element_type=F32)
            acc_scr[h] = alpha * acc_scr[h] + pv[:, :DSA_DH]
            l_scr[h] = alpha * l_scr[h] + pv[:, DSA_DH:]
            m_scr[h] = m_new

    @pl.when(kind_tab[s] == 0)
    def _():
        eye = (lax.broadcasted_iota(I32, (Q_BLOCK, Q_BLOCK), 0)
               == lax.broadcasted_iota(I32, (Q_BLOCK, Q_BLOCK), 1)).astype(BF16)
        for sub in range(FAR_SUBS):
            cols = slice(sub * IDX_SUB, (sub + 1) * IDX_SUB)
            mask = far_ref[0, sub]
            for g in range(DSA_KV_HEADS):
                lhs = jnp.concatenate(
                    [jnp.concatenate([q_ref[:, h * DSA_DH:(h + 1) * DSA_DH], eye], axis=1)
                     for h in range(g * DSA_GROUP, (g + 1) * DSA_GROUP)], axis=0)
                rhs = jnp.concatenate([kt_ref[g * DSA_DH:(g + 1) * DSA_DH, cols], mask], axis=0)
                logits = jnp.dot(lhs, rhs, preferred_element_type=F32)
                for hh in range(DSA_GROUP):
                    s_scr[g * DSA_GROUP + hh] = logits[hh * Q_BLOCK:(hh + 1) * Q_BLOCK, :]
            softmax_pv(vf_ref, IDX_SUB, sub * IDX_SUB)

    @pl.when(kind_tab[s] == 1)
    def _():
        for half, (k_ref, v_ref) in enumerate(((klo_ref, vlo_ref), (khi_ref, vhi_ref))):
            cols = slice(half * Q_BLOCK, (half + 1) * Q_BLOCK)
            mask = near_ref[:, cols].astype(F32)
            for h in range(DSA_HEADS):
                g = h // DSA_GROUP
                logits = lax.dot_general(q_ref[:, h * DSA_DH:(h + 1) * DSA_DH],
                                         k_ref[:, g * DSA_DH:(g + 1) * DSA_DH], _NT, preferred_element_type=F32)
                s_scr[h, :, 0:Q_BLOCK] = logits + b_scr[h][:, cols] + mask
            softmax_pv(v_ref, Q_BLOCK)
        for h in range(DSA_HEADS):
            o_ref[:, h * DSA_DH:(h + 1) * DSA_DH] = (acc_scr[h] / l_scr[h]).astype(o_ref.dtype)


def _masked_attention(proj, far, near, rel_bias):
    L = proj.shape[0]
    nq = L // Q_BLOCK
    far_keys = FAR_SUBS * IDX_SUB
    per = far_keys // Q_BLOCK
    qi_l, kj_l, kind_l, first_l = [], [], [], []
    for i in range(nq):
        n_far = -(-i // per)
        for j in range(n_far):
            qi_l.append(i); kj_l.append(j); kind_l.append(0); first_l.append(1 if j == 0 else 0)
        qi_l.append(i); kj_l.append(max(n_far - 1, 0)); kind_l.append(1); first_l.append(1 if n_far == 0 else 0)
    tabs = [jnp.asarray(np.asarray(t, np.int32)) for t in (qi_l, kj_l, kind_l, first_l)]
    qw = DSA_HEADS * DSA_DH
    kvw = DSA_KV_HEADS * DSA_DH
    k_col = (2 * qw) // kvw
    v_col = k_col + 1
    hw = DSA_HEADS
    vmem = (2 * (Q_BLOCK * qw * 2 * 2 + 2 * far_keys * kvw * 2 + 4 * Q_BLOCK * kvw * 2
                 + Q_BLOCK * far_keys * 2 + Q_BLOCK * 2 * Q_BLOCK * 2)
            + hw * Q_BLOCK * (3 * LANES + 2 * Q_BLOCK) * 4 + (16 << 20))
    idx = lambda f: (lambda s, qt, kt, kd, ft: f(qt[s], kt[s]))
    k_t = proj[:, k_col * kvw:(k_col + 1) * kvw].T
    grid_spec = pltpu.PrefetchScalarGridSpec(
        num_scalar_prefetch=4,
        grid=(len(qi_l),),
        in_specs=[pl.BlockSpec(memory_space=pltpu.SMEM),
                  pl.BlockSpec((Q_BLOCK, qw), idx(lambda i, j: (i, 0))),
                  pl.BlockSpec((kvw, far_keys), idx(lambda i, j: (0, j))),
                  pl.BlockSpec((far_keys, kvw), idx(lambda i, j: (j, v_col))),
                  pl.BlockSpec((Q_BLOCK, kvw), idx(lambda i, j: (jnp.maximum(i - 1, 0), k_col))),
                  pl.BlockSpec((Q_BLOCK, kvw), idx(lambda i, j: (i, k_col))),
                  pl.BlockSpec((Q_BLOCK, kvw), idx(lambda i, j: (jnp.maximum(i - 1, 0), v_col))),
                  pl.BlockSpec((Q_BLOCK, kvw), idx(lambda i, j: (i, v_col))),
                  pl.BlockSpec((1, FAR_SUBS, Q_BLOCK, IDX_SUB), idx(lambda i, j: (i, j, 0, 0))),
                  pl.BlockSpec((Q_BLOCK, 2 * Q_BLOCK), idx(lambda i, j: (i, 0)))],
        out_specs=pl.BlockSpec((Q_BLOCK, qw), idx(lambda i, j: (i, 0))),
        scratch_shapes=[pltpu.VMEM((hw, Q_BLOCK, LANES), F32),
                        pltpu.VMEM((hw, Q_BLOCK, LANES), F32),
                        pltpu.VMEM((hw, Q_BLOCK, DSA_DH), F32),
                        pltpu.VMEM((hw, Q_BLOCK, 2 * Q_BLOCK), F32),
                        pltpu.VMEM((hw, Q_BLOCK, IDX_SUB), F32)],
    )
    return pl.pallas_call(
        _attn_kernel,
        grid_spec=grid_spec,
        out_shape=jax.ShapeDtypeStruct((L, qw), BF16),
        compiler_params=_cparams(("arbitrary",), vmem),
        name="dsa_attention",
    )(*tabs, rel_bias, proj, k_t, proj, proj, proj, proj, proj, far, near)


def _dsa_mixer(xb, w_in, ln_g, ln_b, rel_bias):
    L = xb.shape[0]
    k_top = min(TOPK_MAX, L // 4)
    sq = DSA_HEADS * DSA_DH
    skv = DSA_KV_HEADS * DSA_DH
    si = IDX_HEADS * IDX_DIM
    w_q = w_in[:, :sq]
    w_k = w_in[:, sq:sq + skv]
    w_v = w_in[:, sq + skv:sq + 2 * skv]
    w_qi = w_in[:, sq + 2 * skv:sq + 2 * skv + si]
    w_ki = w_in[:, sq + 2 * skv + si:sq + 2 * skv + si + IDX_DIM]
    w_wi = w_in[:, sq + 2 * skv + si + IDX_DIM:]
    w_main = jnp.concatenate([w_q, w_qi, w_k, w_v], axis=1).astype(BF16)
    colscale = jnp.concatenate([jnp.full((1, sq), DSA_DH ** -0.5 * LOG2E, F32),
                                jnp.ones((1, si + 2 * skv), F32)], axis=1)
    w_small = jnp.concatenate([w_ki, w_wi, jnp.zeros((D_MODEL, LANES - IDX_HEADS), F32)], axis=1).astype(BF16)
    proj = _matmul_scaled(xb, w_main, colscale, BF16, tm=1024, tn=1024)
    ki, wi = _dsa_small(xb, w_small, ln_g, ln_b, tm=1024)
    far, near = _idx_select(proj, ki, wi, k_top)
    return _masked_attention(proj, far, near, rel_bias)


def kernel(x, p, gdn_w_in, gdn_conv_w, gdn_a_log, gdn_dt_bias, gdn_norm_g, gdn_w_o, dsa_w_in, dsa_kidx_ln_g, dsa_kidx_ln_b, dsa_w_o, rel_bias, ln1_g, ln1_b, ffn_w_gate, ffn_w_up, ffn_conv_w, ffn_w_down, ln2_g, ln2_b, ple_w_proj, ple_w_gate):
    assert x.shape[0] == 1 and x.shape[2] == D_MODEL
    xf = x[0]
    xb = xf.astype(BF16)
    ia = ib = 0
    for i in range(DEPTH):
        if i % 2 == 0:
            mix = _gdn_mixer(xb, gdn_w_in[ia], gdn_conv_w[ia], gdn_a_log[ia], gdn_dt_bias[ia], gdn_norm_g[ia])
            w_o = gdn_w_o[ia]
            ia += 1
        else:
            mix = _dsa_mixer(xb, dsa_w_in[ib], dsa_kidx_ln_g[ib], dsa_kidx_ln_b[ib], rel_bias)
            w_o = dsa_w_o[ib]
            ib += 1
        xf, xb = _proj_res_ln(mix, w_o.astype(BF16), xf, ln1_g[i], ln1_b[i], tm=512, sub=256)
        hmid = _ffn_up(xb, ffn_w_gate[i].astype(BF16), ffn_w_up[i].astype(BF16), ffn_conv_w[i], tm=1024, tn=512)
        xf, xb = _proj_res_ln(hmid, ffn_w_down[i].astype(BF16), xf, ln2_g[i], ln2_b[i], tm=512, sub=256)
        xf, xb = _ple(xb, xf, ple_w_gate[i].astype(BF16), p[i, 0], ple_w_proj[i].astype(BF16), tm=1024, tn=1024)
    return xf[None]
```

```python
import functools
import math

import jax
import jax.numpy as jnp
import numpy as np
from jax import lax
from jax.experimental import pallas as pl
from jax.experimental.pallas import tpu as pltpu

F32 = jnp.float32
BF16 = jnp.bfloat16
I32 = jnp.int32

D_MODEL = 2048
GDN_QK_HEADS = 16
GDN_V_HEADS = 32
GDN_DK = 128
GDN_DV = 128
GDN_CONV = 4
GDN_CHUNK = 64
GDN_QK_W = GDN_QK_HEADS * GDN_DK
GDN_V_W = GDN_V_HEADS * GDN_DV
DSA_HEADS = 16
DSA_KV_HEADS = 4
DSA_GROUP = DSA_HEADS // DSA_KV_HEADS
DSA_DH = 128
IDX_HEADS = 16
IDX_DIM = 128
TOPK_MAX = 256
N_BUCKETS = 32
MAX_DISTANCE = 128
D_FF = 5120
FFN_CONV = 3
PLE_DIM = 256
DEPTH = 2
DN_ALPHA = (2.0 * DEPTH) ** 0.25
LN_EPS = 1e-5
RMS_EPS = 1e-6

V7X_VMEM_BYTES = 64 * 1024 * 1024
V7X_VMEM_BUDGET = 56 * 1024 * 1024
LANES = 128
BF16_SUBLANES = 16

HALO = BF16_SUBLANES
Q_BLOCK = 128
IDX_KEY_BLOCK = 2048
IDX_SUB = 512
INT_MIN = -(2 ** 31)
BITS_PER_CHECK = 4
WORD_BITS = 32
FAR_MASKED = -2e30

_NT = (((1,), (1,)), ((), ()))
_TN = (((0,), (0,)), ((), ()))


def _cparams(semantics, vmem_bytes):
    return pltpu.CompilerParams(dimension_semantics=semantics,
                                vmem_limit_bytes=int(min(V7X_VMEM_BUDGET, vmem_bytes)))


def _silu(y):
    return y * jax.nn.sigmoid(y)


def _mm_scale_kernel(x_ref, w_ref, cs_ref, o_ref):
    acc = jnp.dot(x_ref[...], w_ref[...], preferred_element_type=F32)
    o_ref[...] = (acc * cs_ref[...]).astype(o_ref.dtype)


def _matmul_scaled(x, w, colscale, out_dtype, tm, tn):
    M, K = x.shape
    N = w.shape[1]
    tm, tn = min(tm, M), min(tn, N)
    osz = jnp.dtype(out_dtype).itemsize
    vmem = 2 * (tm * K * 2 + K * tn * 2 + tm * tn * osz) + 2 * tm * tn * 4
    return pl.pallas_call(
        _mm_scale_kernel,
        grid=(M // tm, N // tn),
        in_specs=[pl.BlockSpec((tm, K), lambda i, j: (i, 0)),
                  pl.BlockSpec((K, tn), lambda i, j: (0, j)),
                  pl.BlockSpec((1, tn), lambda i, j: (0, j))],
        out_specs=pl.BlockSpec((tm, tn), lambda i, j: (i, j)),
        out_shape=jax.ShapeDtypeStruct((M, N), out_dtype),
        compiler_params=_cparams(("parallel", "parallel"), vmem),
        name="matmul_scaled",
    )(x, w, colscale)


CONV_SUB = 256
FFN_SUB = 512


def _causal_conv(g, gh, cw_ref, g_scr, kc, tm, cols):
    g_scr[0:HALO, cols] = gh
    g_scr[HALO:HALO + tm, cols] = g
    y = cw_ref[kc - 1:kc, cols] * g
    for j in range(kc - 1):
        off = HALO - (kc - 1) + j
        y = y + cw_ref[j:j + 1, cols] * g_scr[off:off + tm, cols]
    return y


def _mm_conv_silu_kernel(x_ref, xh_ref, w_ref, cw_ref, o_ref, g_scr, *, kc, tm):
    first = pl.program_id(0) == 0
    for c0 in range(0, o_ref.shape[1], CONV_SUB):
        cols = slice(c0, c0 + CONV_SUB)
        w = w_ref[:, cols]
        gh = jnp.dot(xh_ref[...], w, preferred_element_type=F32)
        gh = jnp.where(first, 0.0, gh)
        g = jnp.dot(x_ref[...], w, preferred_element_type=F32)
        y = _causal_conv(g, gh, cw_ref, g_scr, kc, tm, cols)
        o_ref[:, cols] = _silu(y).astype(o_ref.dtype)


def _proj_conv_silu(x, w, conv_w, tm, tn):
    M, K = x.shape
    N = w.shape[1]
    kc = conv_w.shape[0]
    tm, tn = min(tm, M), min(tn, N)
    hb = tm // HALO
    vmem = 2 * (tm * K * 2 + HALO * K * 2 + K * tn * 2 + tm * tn * 2) + 4 * tm * tn * 4
    return pl.pallas_call(
        functools.partial(_mm_conv_silu_kernel, kc=kc, tm=tm),
        grid=(M // tm, N // tn),
        in_specs=[pl.BlockSpec((tm, K), lambda i, j: (i, 0)),
                  pl.BlockSpec((HALO, K), lambda i, j: (jnp.maximum(i * hb - 1, 0), 0)),
                  pl.BlockSpec((K, tn), lambda i, j: (0, j)),
                  pl.BlockSpec((kc, tn), lambda i, j: (0, j))],
        out_specs=pl.BlockSpec((tm, tn), lambda i, j: (i, j)),
        out_shape=jax.ShapeDtypeStruct((M, N), BF16),
        scratch_shapes=[pltpu.VMEM((tm + HALO, tn), F32)],
        compiler_params=_cparams(("parallel", "parallel"), vmem),
        name="proj_conv_silu",
    )(x, x, w, conv_w)


def _ffn_up_kernel(x_ref, xh_ref, wg_ref, wu_ref, cw_ref, o_ref, g_scr, *, kc, tm):
    first = pl.program_id(0) == 0
    for c0 in range(0, o_ref.shape[1], FFN_SUB):
        cols = slice(c0, c0 + FFN_SUB)
        wg = wg_ref[:, cols]
        g = jnp.dot(x_ref[...], wg, preferred_element_type=F32)
        gh = jnp.dot(xh_ref[...], wg, preferred_element_type=F32)
        gh = jnp.where(first, 0.0, gh)
        u = jnp.dot(x_ref[...], wu_ref[:, cols], preferred_element_type=F32)
        y = _causal_conv(g, gh, cw_ref, g_scr, kc, tm, cols)
        o_ref[:, cols] = (_silu(y) * u).astype(o_ref.dtype)


def _ffn_up(x, w_gate, w_up, conv_w, tm, tn):
    M, K = x.shape
    N = w_gate.shape[1]
    kc = conv_w.shape[0]
    tm, tn = min(tm, M), min(tn, N)
    hb = tm // HALO
    vmem = 2 * (tm * K * 2 + HALO * K * 2 + 2 * K * tn * 2 + tm * tn * 2) + 6 * tm * tn * 4
    return pl.pallas_call(
        functools.partial(_ffn_up_kernel, kc=kc, tm=tm),
        grid=(M // tm, N // tn),
        in_specs=[pl.BlockSpec((tm, K), lambda i, j: (i, 0)),
                  pl.BlockSpec((HALO, K), lambda i, j: (jnp.maximum(i * hb - 1, 0), 0)),
                  pl.BlockSpec((K, tn), lambda i, j: (0, j)),
                  pl.BlockSpec((K, tn), lambda i, j: (0, j)),
                  pl.BlockSpec((kc, tn), lambda i, j: (0, j))],
        out_specs=pl.BlockSpec((tm, tn), lambda i, j: (i, j)),
        out_shape=jax.ShapeDtypeStruct((M, N), BF16),
        scratch_shapes=[pltpu.VMEM((tm + HALO, tn), F32)],
        compiler_params=_cparams(("parallel", "parallel"), vmem),
        name="ffn_up",
    )(x, x, w_gate, w_up, conv_w)


def _mm_res_ln_kernel(a_ref, w_ref, res_ref, g_ref, b_ref, of_ref, ob_ref, *, sub):
    for r0 in range(0, a_ref.shape[0], sub):
        rows = slice(r0, r0 + sub)
        acc = jnp.dot(a_ref[rows, :], w_ref[...], preferred_element_type=F32)
        y = DN_ALPHA * res_ref[rows, :] + acc
        mu = jnp.mean(y, axis=-1, keepdims=True)
        yc = y - mu
        var = jnp.mean(yc * yc, axis=-1, keepdims=True)
        out = yc * lax.rsqrt(var + LN_EPS) * g_ref[...] + b_ref[...]
        of_ref[rows, :] = out
        ob_ref[rows, :] = out.astype(BF16)


def _proj_res_ln(a, w, res, g, b, tm, sub):
    M, K = a.shape
    N = w.shape[1]
    tm = min(tm, M)
    sub = min(sub, tm)
    vmem = K * N * 2 + 2 * (tm * K * 2 + tm * N * 4 + tm * N * 4 + tm * N * 2) + 4 * sub * N * 4
    return pl.pallas_call(
        functools.partial(_mm_res_ln_kernel, sub=sub),
        grid=(M // tm,),
        in_specs=[pl.BlockSpec((tm, K), lambda i: (i, 0)),
                  pl.BlockSpec((K, N), lambda i: (0, 0), pipeline_mode=pl.Buffered(1)),
                  pl.BlockSpec((tm, N), lambda i: (i, 0)),
                  pl.BlockSpec((1, N), lambda i: (0, 0)),
                  pl.BlockSpec((1, N), lambda i: (0, 0))],
        out_specs=[pl.BlockSpec((tm, N), lambda i: (i, 0)),
                   pl.BlockSpec((tm, N), lambda i: (i, 0))],
        out_shape=[jax.ShapeDtypeStruct((M, N), F32), jax.ShapeDtypeStruct((M, N), BF16)],
        compiler_params=_cparams(("parallel",), vmem),
        name="proj_res_ln",
    )(a, w, res, g.reshape(1, N), b.reshape(1, N))


def _ple_kernel(xb_ref, wg_ref, p_ref, wp_ref, xr_ref, of_ref, ob_ref):
    gate = jax.nn.sigmoid(jnp.dot(xb_ref[...], wg_ref[...], preferred_element_type=F32))
    pe = jnp.dot(p_ref[...].astype(BF16), wp_ref[...], preferred_element_type=F32)
    out = xr_ref[...] + gate * pe
    of_ref[...] = out
    ob_ref[...] = out.astype(BF16)


def _ple(xb, xf, w_gate, p, w_proj, tm, tn):
    M, K = xb.shape
    N = w_gate.shape[1]
    P = p.shape[1]
    tm, tn = min(tm, M), min(tn, N)
    vmem = 2 * (tm * K * 2 + K * tn * 2 + tm * P * 4 + P * tn * 2 + tm * tn * 10) + 4 * tm * tn * 4
    return pl.pallas_call(
        _ple_kernel,
        grid=(M // tm, N // tn),
        in_specs=[pl.BlockSpec((tm, K), lambda i, j: (i, 0)),
                  pl.BlockSpec((K, tn), lambda i, j: (0, j)),
                  pl.BlockSpec((tm, P), lambda i, j: (i, 0)),
                  pl.BlockSpec((P, tn), lambda i, j: (0, j)),
                  pl.BlockSpec((tm, tn), lambda i, j: (i, j))],
        out_specs=[pl.BlockSpec((tm, tn), lambda i, j: (i, j)),
                   pl.BlockSpec((tm, tn), lambda i, j: (i, j))],
        out_shape=[jax.ShapeDtypeStruct((M, N), F32), jax.ShapeDtypeStruct((M, N), BF16)],
        compiler_params=_cparams(("parallel", "parallel"), vmem),
        name="ple",
    )(xb, w_gate, p, w_proj, xf)


GDN_HB = 8
GDN_NC = 4


def _gdn_kernel(q_ref, k_ref, v_ref, z_ref, sc_ref, hp_ref, ng_ref, o_ref, s_scr):
    C = GDN_CHUNK

    @pl.when(pl.program_id(1) == 0)
    def _():
        s_scr[...] = jnp.zeros_like(s_scr)

    row = lax.broadcasted_iota(I32, (C, C), 0)
    col = lax.broadcasted_iota(I32, (C, C), 1)
    tri = row >= col
    strict = row > col
    eye = row == col
    tri_f = tri.astype(F32)
    eye_f = eye.astype(F32)

    raw = sc_ref[...]
    a_log = hp_ref[0, 0:1, :]
    dt_b = hp_ref[0, 1:2, :]
    xs = raw + dt_b
    softplus = jnp.maximum(xs, 0.0) + jnp.log1p(jnp.exp(-jnp.abs(xs)))
    g_all = -jnp.exp(a_log) * softplus
    beta_all = jax.nn.sigmoid(raw)
    ng = ng_ref[...]

    units = [(c, j) for c in range(GDN_NC) for j in range(GDN_HB)]
    kb_l, rhs_l, decay_l, qd_l, kd_l, kbf_l, qbf_l, gl_l = [], [], [], [], [], [], [], []
    for c in range(GDN_NC):
        r0 = c * C
        gc = jnp.dot(tri_f, g_all[r0:r0 + C, :], precision=lax.Precision.HIGHEST,
                     preferred_element_type=F32)
        g_last = gc[C - 1:C, :]
        e_gc = jnp.exp(gc)
        e_rest = jnp.exp(g_last - gc)
        e_last = jnp.exp(g_last)
        beta_c = beta_all[r0:r0 + C, :]
        qn, kn = [], []
        for hq in range(GDN_HB // 2):
            qf = q_ref[r0:r0 + C, hq * GDN_DK:(hq + 1) * GDN_DK].astype(F32)
            kf = k_ref[r0:r0 + C, hq * GDN_DK:(hq + 1) * GDN_DK].astype(F32)
            qn.append(qf * lax.rsqrt(jnp.sum(qf * qf, axis=-1, keepdims=True) + RMS_EPS) * (GDN_DK ** -0.5))
            kn.append(kf * lax.rsqrt(jnp.sum(kf * kf, axis=-1, keepdims=True) + RMS_EPS))
        for j in range(GDN_HB):
            q_h, k_h = qn[j // 2], kn[j // 2]
            vf = v_ref[r0:r0 + C, j * GDN_DV:(j + 1) * GDN_DV].astype(F32)
            beta = beta_c[:, GDN_HB + j:GDN_HB + j + 1]
            kb = k_h * beta
            gcb = jnp.broadcast_to(gc[:, j:j + 1], (C, C))
            gcr = jnp.sum(jnp.where(eye, gcb, 0.0), axis=0, keepdims=True)
            decay_l.append(jnp.where(tri, jnp.exp(jnp.where(tri, gcb - gcr, 0.0)), 0.0))
            kb_l.append(kb.astype(BF16))
            rhs_l.append(jnp.concatenate([vf * beta, kb * e_gc[:, j:j + 1]], axis=-1).astype(BF16))
            qd_l.append((q_h * e_gc[:, j:j + 1]).astype(BF16))
            kd_l.append((k_h * e_rest[:, j:j + 1]).astype(BF16))
            kbf_l.append(k_h.astype(BF16))
            qbf_l.append(q_h.astype(BF16))
            gl_l.append(e_last[:, j:j + 1])

    n_u = len(units)
    kk_l = [lax.dot_general(kb_l[u], kbf_l[u], _NT, preferred_element_type=F32) for u in range(n_u)]
    qk_l = [lax.dot_general(qbf_l[u], kbf_l[u], _NT, preferred_element_type=F32) for u in range(n_u)]
    qk_l = [jnp.where(tri, qk_l[u] * decay_l[u], 0.0).astype(BF16) for u in range(n_u)]
    x_l = [(-jnp.where(strict, kk_l[u] * decay_l[u], 0.0)) for u in range(n_u)]
    t_l = [eye_f + x_l[u] for u in range(n_u)]
    x_l = [x.astype(BF16) for x in x_l]
    for _ in range(5):
        x_l = [jnp.dot(x, x, preferred_element_type=F32).astype(BF16) for x in x_l]
        t_l = [t + jnp.dot(t.astype(BF16), x, preferred_element_type=F32) for t, x in zip(t_l, x_l)]
    sol_l = [jnp.dot(t_l[u].astype(BF16), rhs_l[u], preferred_element_type=F32) for u in range(n_u)]

    s_cur = [s_scr[j] for j in range(GDN_HB)]
    for c in range(GDN_NC):
        r0 = c * C
        us = [c * GDN_HB + j for j in range(GDN_HB)]
        s_bf = [s.astype(BF16) for s in s_cur]
        ws_l = [jnp.dot(sol_l[u][:, GDN_DV:].astype(BF16), s_bf[j], preferred_element_type=F32)
                for j, u in enumerate(us)]
        qs_l = [jnp.dot(qd_l[u], s_bf[j], preferred_element_type=F32) for j, u in enumerate(us)]
        vn_l = [(sol_l[u][:, :GDN_DV] - ws_l[j]).astype(BF16) for j, u in enumerate(us)]
        kv_l = [lax.dot_general(kd_l[u], vn_l[j], _TN, preferred_element_type=F32) for j, u in enumerate(us)]
        ov_l = [jnp.dot(qk_l[u], vn_l[j], preferred_element_type=F32) for j, u in enumerate(us)]
        s_cur = [s_cur[j] * gl_l[u] + kv_l[j] for j, u in enumerate(us)]
        for j in range(GDN_HB):
            o = qs_l[j] + ov_l[j]
            zf = z_ref[r0:r0 + C, j * GDN_DV:(j + 1) * GDN_DV].astype(F32)
            o = o * lax.rsqrt(jnp.mean(o * o, axis=-1, keepdims=True) + RMS_EPS) * ng * _silu(zf)
            o_ref[r0:r0 + C, j * GDN_DV:(j + 1) * GDN_DV] = o.astype(o_ref.dtype)
    for j in range(GDN_HB):
        s_scr[j] = s_cur[j]


def _gdn_core(qkv, z, scal, hparams, norm_g):
    L = qkv.shape[0]
    G = GDN_V_HEADS // GDN_HB
    R = GDN_NC * GDN_CHUNK
    qw = (GDN_HB // 2) * GDN_DK
    vw = GDN_HB * GDN_DV
    k_blk0 = GDN_QK_W // qw
    v_blk0 = 2 * GDN_QK_W // vw
    vmem = 2 * (2 * R * qw * 2 + 2 * R * vw * 2 + R * LANES * 4 + R * vw * 2) + (16 << 20)
    return pl.pallas_call(
        _gdn_kernel,
        grid=(G, L // R),
        in_specs=[pl.BlockSpec((R, qw), lambda g, s: (s, g)),
                  pl.BlockSpec((R, qw), lambda g, s: (s, k_blk0 + g)),
                  pl.BlockSpec((R, vw), lambda g, s: (s, v_blk0 + g)),
                  pl.BlockSpec((R, vw), lambda g, s: (s, g)),
                  pl.BlockSpec((R, LANES), lambda g, s: (s, g)),
                  pl.BlockSpec((1, 8, LANES), lambda g, s: (g, 0, 0)),
                  pl.BlockSpec((1, GDN_DV), lambda g, s: (0, 0))],
        out_specs=pl.BlockSpec((R, vw), lambda g, s: (s, g)),
        out_shape=jax.ShapeDtypeStruct((L, GDN_V_W), BF16),
        scratch_shapes=[pltpu.VMEM((GDN_HB, GDN_DK, GDN_DV), F32)],
        compiler_params=_cparams(("parallel", "arbitrary"), vmem),
        name="gdn_core",
    )(qkv, qkv, qkv, z, scal, hparams, norm_g.reshape(1, GDN_DV))


def _gdn_mixer(xb, w_in, conv_w, a_log, dt_bias, norm_g):
    L = xb.shape[0]
    nqkv = 2 * GDN_QK_W + GDN_V_W
    w_qkv = w_in[:, :nqkv].astype(BF16)
    w_z = w_in[:, nqkv:nqkv + GDN_V_W].astype(BF16)
    w_ab = w_in[:, nqkv + GDN_V_W:]
    G = GDN_V_HEADS // GDN_HB
    w_a = w_ab[:, :GDN_V_HEADS].reshape(D_MODEL, G, GDN_HB)
    w_b = w_ab[:, GDN_V_HEADS:].reshape(D_MODEL, G, GDN_HB)
    w_sc = jnp.concatenate([w_a, w_b, jnp.zeros((D_MODEL, G, LANES - 2 * GDN_HB), F32)], axis=-1)
    w_sc = w_sc.reshape(D_MODEL, G * LANES).astype(BF16)
    qkv = _proj_conv_silu(xb, w_qkv, conv_w, tm=1024, tn=1024)
    ones_z = jnp.ones((1, GDN_V_W), F32)
    z = _matmul_scaled(xb, w_z, ones_z, BF16, tm=1024, tn=1024)
    scal = _matmul_scaled(xb, w_sc, jnp.ones((1, G * LANES), F32), F32, tm=1024, tn=G * LANES)
    hp = jnp.zeros((G, 8, LANES), F32)
    hp = hp.at[:, 0, :GDN_HB].set(a_log.reshape(G, GDN_HB))
    hp = hp.at[:, 1, :GDN_HB].set(dt_bias.reshape(G, GDN_HB))
    return _gdn_core(qkv, z, scal, hp, norm_g)


def _dsa_small_kernel(x_ref, w_ref, g_ref, b_ref, ki_ref, wi_ref):
    acc = jnp.dot(x_ref[...], w_ref[...], preferred_element_type=F32)
    ki = acc[:, :IDX_DIM]
    mu = jnp.mean(ki, axis=-1, keepdims=True)
    kc = ki - mu
    var = jnp.mean(kc * kc, axis=-1, keepdims=True)
    ki_ref[...] = (kc * lax.rsqrt(var + LN_EPS) * g_ref[...] + b_ref[...]).astype(ki_ref.dtype)
    wi_ref[...] = acc[:, IDX_DIM:] * ((IDX_HEADS ** -0.5) * (IDX_DIM ** -0.5))


def _dsa_small(xb, w_small, ln_g, ln_b, tm):
    M, K = xb.shape
    tm = min(tm, M)
    N = 2 * LANES
    vmem = 2 * (tm * K * 2 + K * N * 2 + tm * LANES * 6) + 4 * tm * N * 4
    return pl.pallas_call(
        _dsa_small_kernel,
        grid=(M // tm,),
        in_specs=[pl.BlockSpec((tm, K), lambda i: (i, 0)),
                  pl.BlockSpec((K, N), lambda i: (0, 0)),
                  pl.BlockSpec((1, IDX_DIM), lambda i: (0, 0)),
                  pl.BlockSpec((1, IDX_DIM), lambda i: (0, 0))],
        out_specs=[pl.BlockSpec((tm, IDX_DIM), lambda i: (i, 0)),
                   pl.BlockSpec((tm, LANES), lambda i: (i, 0))],
        out_shape=[jax.ShapeDtypeStruct((M, IDX_DIM), BF16), jax.ShapeDtypeStruct((M, LANES), F32)],
        compiler_params=_cparams(("parallel",), vmem),
        name="dsa_idx_proj",
    )(xb, w_small, ln_g.reshape(1, IDX_DIM), ln_b.reshape(1, IDX_DIM))


def _sortable_key(score):
    bits = lax.bitcast_convert_type(score, I32)
    return jnp.where(bits >= 0, bits, bits ^ jnp.int32(0x7FFFFFFF))


def _idx_kernel(qi_tab, kj_tab, last_tab,
                qidx_ref, kidx_ref, wi_ref, far_ref, near_ref, key_scr, plane_scr, cand_scr, w_scr,
                *, k_top, n_sub_total):
    s = pl.program_id(0)
    i = qi_tab[s]
    j = kj_tab[s]

    @pl.when(s == 0)
    def _():
        key_scr[...] = jnp.full_like(key_scr, INT_MIN)
    nsub = IDX_KEY_BLOCK // IDX_SUB
    t_col = i * Q_BLOCK + lax.broadcasted_iota(I32, (Q_BLOCK, 1), 0)
    lane_sub = lax.broadcasted_iota(I32, (Q_BLOCK, IDX_SUB), 1)
    wi = wi_ref[...]

    for sub in range(nsub):
        ki_sub = kidx_ref[sub * IDX_SUB:(sub + 1) * IDX_SUB, :]
        acc = jnp.zeros((Q_BLOCK, IDX_SUB), F32)
        for h in range(IDX_HEADS):
            sc = lax.dot_general(qidx_ref[:, h * IDX_DIM:(h + 1) * IDX_DIM], ki_sub, _NT,
                                 preferred_element_type=F32)
            acc = acc + jnp.maximum(sc, 0.0) * wi[:, h:h + 1]
        s_idx = j * IDX_KEY_BLOCK + sub * IDX_SUB + lane_sub
        key_scr[j * nsub + sub] = jnp.where(s_idx <= t_col, _sortable_key(acc), INT_MIN)

    @pl.when(last_tab[s] == 1)
    def _():
        n_chunks = (i * Q_BLOCK + Q_BLOCK - 1) // IDX_SUB + 1

        def count(pred, ref_val):
            refb = jnp.broadcast_to(ref_val, (Q_BLOCK, LANES))

            def body(c, cnt):
                blk = key_scr[c]
                for l in range(IDX_SUB // LANES):
                    cnt = cnt + jnp.where(pred(blk[:, l * LANES:(l + 1) * LANES], refb), 1, 0)
                return cnt

            cnt = lax.fori_loop(0, n_chunks, body, jnp.zeros((Q_BLOCK, LANES), I32))
            return jnp.sum(cnt, axis=1, keepdims=True)

        tiles_per_chunk = IDX_SUB // LANES
        chunks_per_group = WORD_BITS // tiles_per_chunk
        n_tiles = n_chunks * tiles_per_chunk
        n_groups = (n_tiles + WORD_BITS - 1) // WORD_BITS

        def build_group(g, carry):
            def build_rows(r, carry_r):
                r8 = pl.multiple_of(r * 8, 8)
                a = [key_scr[g * chunks_per_group + t // tiles_per_chunk, pl.ds(r8, 8),
                             (t % tiles_per_chunk) * LANES:(t % tiles_per_chunk + 1) * LANES]
                     for t in range(WORD_BITS)]
                m, sh = 0x0000FFFF, 16
                while sh:
                    k = 0
                    while k < WORD_BITS:
                        x = (a[k] ^ lax.shift_right_logical(a[k + sh], jnp.int32(sh))) & jnp.int32(m)
                        a[k] = a[k] ^ x
                        a[k + sh] = a[k + sh] ^ jnp.left_shift(x, jnp.int32(sh))
                        k = (k + sh + 1) & ~sh
                    sh >>= 1
                    m = (m ^ (m << sh)) & 0xFFFFFFFF if sh else m
                a[0] = ~a[0]
                for p in range(WORD_BITS):
                    plane_scr[p, g, pl.ds(r8, 8), :] = a[p]
                return carry_r

            lax.fori_loop(0, Q_BLOCK // 8, build_rows, 0)
            n_valid = jnp.minimum(n_tiles - g * WORD_BITS, WORD_BITS)
            word = jnp.where(n_valid >= WORD_BITS, jnp.int32(-1),
                             jnp.left_shift(jnp.int32(-1), WORD_BITS - n_valid))
            cand_scr[g] = jnp.broadcast_to(word, (Q_BLOCK, LANES))
            return carry

        lax.fori_loop(0, n_groups, build_group, 0)

        def plane_pass(p, thr_u, n_above, n_cand):
            def acc_body(g, cnt):
                w = plane_scr[p, g] & cand_scr[g]
                w_scr[g] = w
                return cnt + lax.population_count(w)

            n_set = jnp.sum(lax.fori_loop(0, n_groups, acc_body, jnp.zeros((Q_BLOCK, LANES), I32)),
                            axis=1, keepdims=True)
            take = n_above + n_set >= k_top
            take_b = jnp.broadcast_to(take, (Q_BLOCK, LANES))

            def upd_body(g, carry):
                w = w_scr[g]
                cand_scr[g] = jnp.where(take_b, w, cand_scr[g] ^ w)
                return carry

            lax.fori_loop(0, n_groups, upd_body, 0)
            bit = jnp.left_shift(jnp.int32(1), WORD_BITS - 1 - p)
            return (jnp.where(take, thr_u | bit, thr_u), jnp.where(take, n_above, n_above + n_set),
                    jnp.where(take, n_set, n_cand - n_set))

        def group_body(carry):
            grp, thr_u, n_above, n_cand, _ = carry
            for bb in range(BITS_PER_CHECK):
                thr_u, n_above, n_cand = plane_pass(grp * BITS_PER_CHECK + bb, thr_u, n_above, n_cand)
            return grp + 1, thr_u, n_above, n_cand, jnp.max(jnp.where(n_above + n_cand != k_top, 1, 0))

        _, thr_u, n_above, n_cand, n_tied_rows = lax.while_loop(
            lambda carry: (carry[0] < WORD_BITS // BITS_PER_CHECK) & (carry[4] != 0), group_body,
            (jnp.int32(0), jnp.zeros((Q_BLOCK, 1), I32), jnp.zeros((Q_BLOCK, 1), I32),
             jnp.broadcast_to(n_tiles * LANES, (Q_BLOCK, 1)).astype(I32), jnp.int32(1)))
        thr = thr_u ^ jnp.int32(INT_MIN)
        thr_b = jnp.broadcast_to(thr, (Q_BLOCK, IDX_SUB))

        def emit(c, sel):
            s_idx = c * IDX_SUB + lane_sub
            far = sel & (t_col - s_idx >= MAX_DISTANCE)
            far_ref[0, c] = jnp.where(far, 0.0, FAR_MASKED).astype(far_ref.dtype)
            key_scr[c] = jnp.where(sel, 1, 0)

        @pl.when(n_tied_rows == 0)
        def _():
            def sel_body(c, carry):
                emit(c, key_scr[c] >= thr_b)
                return carry

            lax.fori_loop(0, n_chunks, sel_body, 0)

        @pl.when(n_tied_rows != 0)
        def _():
            n_gt = count(lambda a, r: a > r, thr)
            need_eq = (k_top - n_gt).astype(F32)
            incl = (lax.broadcasted_iota(I32, (IDX_SUB, IDX_SUB), 0)
                    <= lax.broadcasted_iota(I32, (IDX_SUB, IDX_SUB), 1)).astype(BF16)

            def sel_body(c, carry):
                blk = key_scr[c]
                eq = blk == thr_b
                eq_f = jnp.where(eq, 1.0, 0.0)
                rank = carry + jnp.dot(eq_f.astype(BF16), incl, preferred_element_type=F32)
                s_idx = c * IDX_SUB + lane_sub
                emit(c, ((blk > thr_b) | (eq & (rank <= need_eq))) & (s_idx <= t_col))
                return carry + jnp.sum(eq_f, axis=1, keepdims=True)

            lax.fori_loop(0, n_chunks, sel_body, jnp.zeros((Q_BLOCK, 1), F32))

        def fill_body(c, carry):
            far_ref[0, c] = jnp.full((Q_BLOCK, IDX_SUB), FAR_MASKED, far_ref.dtype)
            return carry

        lax.fori_loop(n_chunks, n_sub_total, fill_body, 0)

        def window(blk_idx):
            per = IDX_SUB // Q_BLOCK
            chunk = key_scr[blk_idx // per]
            m = blk_idx % per
            out = chunk[:, 0:Q_BLOCK]
            for q in range(1, per):
                out = jnp.where(m == q, chunk[:, q * Q_BLOCK:(q + 1) * Q_BLOCK], out)
            return out

        r_i = lax.broadcasted_iota(I32, (Q_BLOCK, Q_BLOCK), 0)
        c_i = lax.broadcasted_iota(I32, (Q_BLOCK, Q_BLOCK), 1)
        d_lo = Q_BLOCK + r_i - c_i
        d_hi = r_i - c_i
        near_lo = (window(jnp.maximum(i - 1, 0)) != 0) & (d_lo < MAX_DISTANCE) & (i >= 1)
        near_hi = (window(i) != 0) & (d_hi >= 0) & (d_hi < MAX_DISTANCE)
        near_ref[:, 0:Q_BLOCK] = jnp.where(near_lo, 0.0, -jnp.inf).astype(near_ref.dtype)
        near_ref[:, Q_BLOCK:2 * Q_BLOCK] = jnp.where(near_hi, 0.0, -jnp.inf).astype(near_ref.dtype)


def _idx_select(proj, ki, wi, k_top):
    L = ki.shape[0]
    nq = L // Q_BLOCK
    n_sub_total = L // IDX_SUB
    qi_l, kj_l, last_l = [], [], []
    for i in range(nq):
        j_last = (i * Q_BLOCK + Q_BLOCK - 1) // IDX_KEY_BLOCK
        for j in range(j_last + 1):
            qi_l.append(i)
            kj_l.append(j)
            last_l.append(1 if j == j_last else 0)
    tabs = [jnp.asarray(np.asarray(t, np.int32)) for t in (qi_l, kj_l, last_l)]
    qcol = (DSA_HEADS * DSA_DH) // (IDX_HEADS * IDX_DIM)
    group_keys = WORD_BITS * LANES
    n_groups_max = -(-L // group_keys)
    key_chunks = n_groups_max * (group_keys // IDX_SUB)
    vmem = ((key_chunks * IDX_SUB + (WORD_BITS + 2) * n_groups_max * LANES) * Q_BLOCK * 4
            + 2 * (Q_BLOCK * L * 2) + 2 * (Q_BLOCK * IDX_HEADS * IDX_DIM * 2)
            + 2 * IDX_KEY_BLOCK * IDX_DIM * 2 + (8 << 20))
    grid_spec = pltpu.PrefetchScalarGridSpec(
        num_scalar_prefetch=3,
        grid=(len(qi_l),),
        in_specs=[pl.BlockSpec((Q_BLOCK, IDX_HEADS * IDX_DIM), lambda s, qt, kt, lt: (qt[s], qcol)),
                  pl.BlockSpec((IDX_KEY_BLOCK, IDX_DIM), lambda s, qt, kt, lt: (kt[s], 0)),
                  pl.BlockSpec((Q_BLOCK, LANES), lambda s, qt, kt, lt: (qt[s], 0))],
        out_specs=[pl.BlockSpec((1, n_sub_total, Q_BLOCK, IDX_SUB), lambda s, qt, kt, lt: (qt[s], 0, 0, 0)),
                   pl.BlockSpec((Q_BLOCK, 2 * Q_BLOCK), lambda s, qt, kt, lt: (qt[s], 0))],
        scratch_shapes=[pltpu.VMEM((key_chunks, Q_BLOCK, IDX_SUB), I32),
                        pltpu.VMEM((WORD_BITS, n_groups_max, Q_BLOCK, LANES), I32),
                        pltpu.VMEM((n_groups_max, Q_BLOCK, LANES), I32),
                        pltpu.VMEM((n_groups_max, Q_BLOCK, LANES), I32)],
    )
    return pl.pallas_call(
        functools.partial(_idx_kernel, k_top=k_top, n_sub_total=n_sub_total),
        grid_spec=grid_spec,
        out_shape=[jax.ShapeDtypeStruct((nq, n_sub_total, Q_BLOCK, IDX_SUB), BF16),
                   jax.ShapeDtypeStruct((L, 2 * Q_BLOCK), BF16)],
        compiler_params=_cparams(("arbitrary",), vmem),
        name="dsa_idx_select",
    )(*tabs, proj, ki, wi)


M_INIT = -1e30
FAR_SUBS = 2
LOG2E = math.log2(math.e)


def _attn_kernel(qi_tab, kj_tab, kind_tab, first_tab,
                 tab_ref, q_ref, kt_ref, vf_ref, klo_ref, khi_ref, vlo_ref, vhi_ref, far_ref, near_ref,
                 o_ref, m_scr, l_scr, acc_scr, b_scr, s_scr):
    s = pl.program_id(0)

    @pl.when(s == 0)
    def _():
        r_i = lax.broadcasted_iota(I32, (Q_BLOCK, 2 * Q_BLOCK), 0)
        c_i = lax.broadcasted_iota(I32, (Q_BLOCK, 2 * Q_BLOCK), 1)
        d = jnp.maximum(Q_BLOCK + r_i - c_i, 0)
        max_exact = N_BUCKETS // 2
        df = jnp.maximum(d, 1).astype(F32)
        large = max_exact + (jnp.log(df / max_exact) / math.log(MAX_DISTANCE / max_exact)
                             * (N_BUCKETS - max_exact)).astype(I32)
        large = jnp.minimum(large, N_BUCKETS - 1)
        bkt = jnp.where(d < max_exact, d, large)
        for h in range(DSA_HEADS):
            acc = jnp.zeros((Q_BLOCK, 2 * Q_BLOCK), F32)
            for b in range(N_BUCKETS):
                acc = jnp.where(bkt == b, (tab_ref[b, h] - tab_ref[N_BUCKETS - 1, h]) * LOG2E, acc)
            b_scr[h] = acc

    @pl.when(first_tab[s] == 1)
    def _():
        m_scr[...] = jnp.full_like(m_scr, M_INIT)
        l_scr[...] = jnp.zeros_like(l_scr)
        acc_scr[...] = jnp.zeros_like(acc_scr)

    def softmax_pv(v_ref, width, row0=0):
        nt = width // LANES
        rows = slice(row0, row0 + width)
        ones = jnp.ones((width, DSA_DH), BF16)
        for h in range(DSA_HEADS):
            g = h // DSA_GROUP
            tiles = [s_scr[h, :, t * LANES:(t + 1) * LANES] for t in range(nt)]
            tmax = tiles[0]
            for t in range(1, nt):
                tmax = jnp.maximum(tmax, tiles[t])
            m_prev = m_scr[h]
            m_new = jnp.maximum(m_prev, jnp.max(tmax, axis=-1, keepdims=True))
            alpha = jnp.exp2(m_prev - m_new)
            p = [jnp.exp2(tiles[t] - m_new).astype(BF16) for t in range(nt)]
            pb = jnp.concatenate(p, axis=-1) if nt > 1 else p[0]
            v_aug = jnp.concatenate([v_ref[rows, g * DSA_DH:(g + 1) * DSA_DH], ones], axis=-1)
            pv = jnp.dot(pb, v_aug, preferred_element_type=F32)
            acc_scr[h] = alpha * acc_scr[h] + pv[:, :DSA_DH]
            l_scr[h] = alpha * l_scr[h] + pv[:, DSA_DH:]
            m_scr[h] = m_new

    @pl.when(kind_tab[s] == 0)
    def _():
        eye = (lax.broadcasted_iota(I32, (Q_BLOCK, Q_BLOCK), 0)
               == lax.broadcasted_iota(I32, (Q_BLOCK, Q_BLOCK), 1)).astype(BF16)
        for sub in range(FAR_SUBS):
            cols = slice(sub * IDX_SUB, (sub + 1) * IDX_SUB)
            mask = far_ref[0, sub]
            for g in range(DSA_KV_HEADS):
                lhs = jnp.concatenate(
                    [jnp.concatenate([q_ref[:, h * DSA_DH:(h + 1) * DSA_DH], eye], axis=1)
                     for h in range(g * DSA_GROUP, (g + 1) * DSA_GROUP)], axis=0)
                rhs = jnp.concatenate([kt_ref[g * DSA_DH:(g + 1) * DSA_DH, cols], mask], axis=0)
                logits = jnp.dot(lhs, rhs, preferred_element_type=F32)
                for hh in range(DSA_GROUP):
                    s_scr[g * DSA_GROUP + hh] = logits[hh * Q_BLOCK:(hh + 1) * Q_BLOCK, :]
            softmax_pv(vf_ref, IDX_SUB, sub * IDX_SUB)

    @pl.when(kind_tab[s] == 1)
    def _():
        for half, (k_ref, v_ref) in enumerate(((klo_ref, vlo_ref), (khi_ref, vhi_ref))):
            cols = slice(half * Q_BLOCK, (half + 1) * Q_BLOCK)
            mask = near_ref[:, cols].astype(F32)
            for h in range(DSA_HEADS):
                g = h // DSA_GROUP
                logits = lax.dot_general(q_ref[:, h * DSA_DH:(h + 1) * DSA_DH],
                                         k_ref[:, g * DSA_DH:(g + 1) * DSA_DH], _NT, preferred_element_type=F32)
                s_scr[h, :, 0:Q_BLOCK] = logits + b_scr[h][:, cols] + mask
            softmax_pv(v_ref, Q_BLOCK)
        for h in range(DSA_HEADS):
            o_ref[:, h * DSA_DH:(h + 1) * DSA_DH] = (acc_scr[h] / l_scr[h]).astype(o_ref.dtype)


def _masked_attention(proj, far, near, rel_bias):
    L = proj.shape[0]
    nq = L // Q_BLOCK
    far_keys = FAR_SUBS * IDX_SUB
    per = far_keys // Q_BLOCK
    qi_l, kj_l, kind_l, first_l = [], [], [], []
    for i in range(nq):
        n_far = -(-i // per)
        for j in range(n_far):
            qi_l.append(i); kj_l.append(j); kind_l.append(0); first_l.append(1 if j == 0 else 0)
        qi_l.append(i); kj_l.append(max(n_far - 1, 0)); kind_l.append(1); first_l.append(1 if n_far == 0 else 0)
    tabs = [jnp.asarray(np.asarray(t, np.int32)) for t in (qi_l, kj_l, kind_l, first_l)]
    qw = DSA_HEADS * DSA_DH
    kvw = DSA_KV_HEADS * DSA_DH
    k_col = (2 * qw) // kvw
    v_col = k_col + 1
    hw = DSA_HEADS
    vmem = (2 * (Q_BLOCK * qw * 2 * 2 + 2 * far_keys * kvw * 2 + 4 * Q_BLOCK * kvw * 2
                 + Q_BLOCK * far_keys * 2 + Q_BLOCK * 2 * Q_BLOCK * 2)
            + hw * Q_BLOCK * (3 * LANES + 2 * Q_BLOCK) * 4 + (16 << 20))
    idx = lambda f: (lambda s, qt, kt, kd, ft: f(qt[s], kt[s]))
    k_t = proj[:, k_col * kvw:(k_col + 1) * kvw].T
    grid_spec = pltpu.PrefetchScalarGridSpec(
        num_scalar_prefetch=4,
        grid=(len(qi_l),),
        in_specs=[pl.BlockSpec(memory_space=pltpu.SMEM),
                  pl.BlockSpec((Q_BLOCK, qw), idx(lambda i, j: (i, 0))),
                  pl.BlockSpec((kvw, far_keys), idx(lambda i, j: (0, j))),
                  pl.BlockSpec((far_keys, kvw), idx(lambda i, j: (j, v_col))),
                  pl.BlockSpec((Q_BLOCK, kvw), idx(lambda i, j: (jnp.maximum(i - 1, 0), k_col))),
                  pl.BlockSpec((Q_BLOCK, kvw), idx(lambda i, j: (i, k_col))),
                  pl.BlockSpec((Q_BLOCK, kvw), idx(lambda i, j: (jnp.maximum(i - 1, 0), v_col))),
                  pl.BlockSpec((Q_BLOCK, kvw), idx(lambda i, j: (i, v_col))),
                  pl.BlockSpec((1, FAR_SUBS, Q_BLOCK, IDX_SUB), idx(lambda i, j: (i, j, 0, 0))),
                  pl.BlockSpec((Q_BLOCK, 2 * Q_BLOCK), idx(lambda i, j: (i, 0)))],
        out_specs=pl.BlockSpec((Q_BLOCK, qw), idx(lambda i, j: (i, 0))),
        scratch_shapes=[pltpu.VMEM((hw, Q_BLOCK, LANES), F32),
                        pltpu.VMEM((hw, Q_BLOCK, LANES), F32),
                        pltpu.VMEM((hw, Q_BLOCK, DSA_DH), F32),
                        pltpu.VMEM((hw, Q_BLOCK, 2 * Q_BLOCK), F32),
                        pltpu.VMEM((hw, Q_BLOCK, IDX_SUB), F32)],
    )
    return pl.pallas_call(
        _attn_kernel,
        grid_spec=grid_spec,
        out_shape=jax.ShapeDtypeStruct((L, qw), BF16),
        compiler_params=_cparams(("arbitrary",), vmem),
        name="dsa_attention",
    )(*tabs, rel_bias, proj, k_t, proj, proj, proj, proj, proj, far, near)


def _dsa_mixer(xb, w_in, ln_g, ln_b, rel_bias):
    L = xb.shape[0]
    k_top = min(TOPK_MAX, L // 4)
    sq = DSA_HEADS * DSA_DH
    skv = DSA_KV_HEADS * DSA_DH
    si = IDX_HEADS * IDX_DIM
    w_q = w_in[:, :sq]
    w_k = w_in[:, sq:sq + skv]
    w_v = w_in[:, sq + skv:sq + 2 * skv]
    w_qi = w_in[:, sq + 2 * skv:sq + 2 * skv + si]
    w_ki = w_in[:, sq + 2 * skv + si:sq + 2 * skv + si + IDX_DIM]
    w_wi = w_in[:, sq + 2 * skv + si + IDX_DIM:]
    w_main = jnp.concatenate([w_q, w_qi, w_k, w_v], axis=1).astype(BF16)
    colscale = jnp.concatenate([jnp.full((1, sq), DSA_DH ** -0.5 * LOG2E, F32),
                                jnp.ones((1, si + 2 * skv), F32)], axis=1)
    w_small = jnp.concatenate([w_ki, w_wi, jnp.zeros((D_MODEL, LANES - IDX_HEADS), F32)], axis=1).astype(BF16)
    proj = _matmul_scaled(xb, w_main, colscale, BF16, tm=1024, tn=1024)
    ki, wi = _dsa_small(xb, w_small, ln_g, ln_b, tm=1024)
    far, near = _idx_select(proj, ki, wi, k_top)
    return _masked_attention(proj, far, near, rel_bias)


def kernel(x, p, gdn_w_in, gdn_conv_w, gdn_a_log, gdn_dt_bias, gdn_norm_g, gdn_w_o, dsa_w_in, dsa_kidx_ln_g, dsa_kidx_ln_b, dsa_w_o, rel_bias, ln1_g, ln1_b, ffn_w_gate, ffn_w_up, ffn_conv_w, ffn_w_down, ln2_g, ln2_b, ple_w_proj, ple_w_gate):
    assert x.shape[0] == 1 and x.shape[2] == D_MODEL
    xf = x[0]
    xb = xf.astype(BF16)
    ia = ib = 0
    for i in range(DEPTH):
        if i % 2 == 0:
            mix = _gdn_mixer(xb, gdn_w_in[ia], gdn_conv_w[ia], gdn_a_log[ia], gdn_dt_bias[ia], gdn_norm_g[ia])
            w_o = gdn_w_o[ia]
            ia += 1
        else:
            mix = _dsa_mixer(xb, dsa_w_in[ib], dsa_kidx_ln_g[ib], dsa_kidx_ln_b[ib], rel_bias)
            w_o = dsa_w_o[ib]
            ib += 1
        xf, xb = _proj_res_ln(mix, w_o.astype(BF16), xf, ln1_g[i], ln1_b[i], tm=512, sub=256)
        hmid = _ffn_up(xb, ffn_w_gate[i].astype(BF16), ffn_w_up[i].astype(BF16), ffn_conv_w[i], tm=1024, tn=512)
        xf, xb = _proj_res_ln(hmid, ffn_w_down[i].astype(BF16), xf, ln2_g[i], ln2_b[i], tm=512, sub=256)
        xf, xb = _ple(xb, xf, ple_w_gate[i].astype(BF16), p[i, 0], ple_w_proj[i].astype(BF16), tm=1024, tn=1024)
    return xf[None]
```

```python
import functools
import math

import jax
import jax.numpy as jnp
import numpy as np
from jax import lax
from jax.experimental import pallas as pl
from jax.experimental.pallas import tpu as pltpu

F32 = jnp.float32
BF16 = jnp.bfloat16
I32 = jnp.int32

D_MODEL = 2048
GDN_QK_HEADS = 16
GDN_V_HEADS = 32
GDN_DK = 128
GDN_DV = 128
GDN_CONV = 4
GDN_CHUNK = 64
GDN_QK_W = GDN_QK_HEADS * GDN_DK
GDN_V_W = GDN_V_HEADS * GDN_DV
DSA_HEADS = 16
DSA_KV_HEADS = 4
DSA_GROUP = DSA_HEADS // DSA_KV_HEADS
DSA_DH = 128
IDX_HEADS = 16
IDX_DIM = 128
TOPK_MAX = 256
N_BUCKETS = 32
MAX_DISTANCE = 128
D_FF = 5120
FFN_CONV = 3
PLE_DIM = 256
DEPTH = 2
DN_ALPHA = (2.0 * DEPTH) ** 0.25
LN_EPS = 1e-5
RMS_EPS = 1e-6

V7X_VMEM_BYTES = 64 * 1024 * 1024
V7X_VMEM_BUDGET = 56 * 1024 * 1024
LANES = 128
BF16_SUBLANES = 16

HALO = BF16_SUBLANES
Q_BLOCK = 128
IDX_KEY_BLOCK = 2048
IDX_SUB = 512
INT_MIN = -(2 ** 31)
BITS_PER_CHECK = 4
WORD_BITS = 32
FAR_MASKED = -2e30

_NT = (((1,), (1,)), ((), ()))
_TN = (((0,), (0,)), ((), ()))


def _cparams(semantics, vmem_bytes):
    return pltpu.CompilerParams(dimension_semantics=semantics,
                                vmem_limit_bytes=int(min(V7X_VMEM_BUDGET, vmem_bytes)))


def _silu(y):
    return y * jax.nn.sigmoid(y)


def _mm_scale_kernel(x_ref, w_ref, cs_ref, o_ref):
    acc = jnp.dot(x_ref[...], w_ref[...], preferred_element_type=F32)
    o_ref[...] = (acc * cs_ref[...]).astype(o_ref.dtype)


def _matmul_scaled(x, w, colscale, out_dtype, tm, tn):
    M, K = x.shape
    N = w.shape[1]
    tm, tn = min(tm, M), min(tn, N)
    osz = jnp.dtype(out_dtype).itemsize
    vmem = 2 * (tm * K * 2 + K * tn * 2 + tm * tn * osz) + 2 * tm * tn * 4
    return pl.pallas_call(
        _mm_scale_kernel,
        grid=(M // tm, N // tn),
        in_specs=[pl.BlockSpec((tm, K), lambda i, j: (i, 0)),
                  pl.BlockSpec((K, tn), lambda i, j: (0, j)),
                  pl.BlockSpec((1, tn), lambda i, j: (0, j))],
        out_specs=pl.BlockSpec((tm, tn), lambda i, j: (i, j)),
        out_shape=jax.ShapeDtypeStruct((M, N), out_dtype),
        compiler_params=_cparams(("parallel", "parallel"), vmem),
        name="matmul_scaled",
    )(x, w, colscale)


CONV_SUB = 256
FFN_SUB = 512


def _causal_conv(g, gh, cw_ref, g_scr, kc, tm, cols):
    g_scr[0:HALO, cols] = gh
    g_scr[HALO:HALO + tm, cols] = g
    y = cw_ref[kc - 1:kc, cols] * g
    for j in range(kc - 1):
        off = HALO - (kc - 1) + j
        y = y + cw_ref[j:j + 1, cols] * g_scr[off:off + tm, cols]
    return y


def _mm_conv_silu_kernel(x_ref, xh_ref, w_ref, cw_ref, o_ref, g_scr, *, kc, tm):
    first = pl.program_id(0) == 0
    for c0 in range(0, o_ref.shape[1], CONV_SUB):
        cols = slice(c0, c0 + CONV_SUB)
        w = w_ref[:, cols]
        gh = jnp.dot(xh_ref[...], w, preferred_element_type=F32)
        gh = jnp.where(first, 0.0, gh)
        g = jnp.dot(x_ref[...], w, preferred_element_type=F32)
        y = _causal_conv(g, gh, cw_ref, g_scr, kc, tm, cols)
        o_ref[:, cols] = _silu(y).astype(o_ref.dtype)


def _proj_conv_silu(x, w, conv_w, tm, tn):
    M, K = x.shape
    N = w.shape[1]
    kc = conv_w.shape[0]
    tm, tn = min(tm, M), min(tn, N)
    hb = tm // HALO
    vmem = 2 * (tm * K * 2 + HALO * K * 2 + K * tn * 2 + tm * tn * 2) + 4 * tm * tn * 4
    return pl.pallas_call(
        functools.partial(_mm_conv_silu_kernel, kc=kc, tm=tm),
        grid=(M // tm, N // tn),
        in_specs=[pl.BlockSpec((tm, K), lambda i, j: (i, 0)),
                  pl.BlockSpec((HALO, K), lambda i, j: (jnp.maximum(i * hb - 1, 0), 0)),
                  pl.BlockSpec((K, tn), lambda i, j: (0, j)),
                  pl.BlockSpec((kc, tn), lambda i, j: (0, j))],
        out_specs=pl.BlockSpec((tm, tn), lambda i, j: (i, j)),
        out_shape=jax.ShapeDtypeStruct((M, N), BF16),
        scratch_shapes=[pltpu.VMEM((tm + HALO, tn), F32)],
        compiler_params=_cparams(("parallel", "parallel"), vmem),
        name="proj_conv_silu",
    )(x, x, w, conv_w)


def _ffn_up_kernel(x_ref, xh_ref, wg_ref, wu_ref, cw_ref, o_ref, g_scr, *, kc, tm):
    first = pl.program_id(0) == 0
    for c0 in range(0, o_ref.shape[1], FFN_SUB):
        cols = slice(c0, c0 + FFN_SUB)
        wg = wg_ref[:, cols]
        g = jnp.dot(x_ref[...], wg, preferred_element_type=F32)
        gh = jnp.dot(xh_ref[...], wg, preferred_element_type=F32)
        gh = jnp.where(first, 0.0, gh)
        u = jnp.dot(x_ref[...], wu_ref[:, cols], preferred_element_type=F32)
        y = _causal_conv(g, gh, cw_ref, g_scr, kc, tm, cols)
        o_ref[:, cols] = (_silu(y) * u).astype(o_ref.dtype)


def _ffn_up(x, w_gate, w_up, conv_w, tm, tn):
    M, K = x.shape
    N = w_gate.shape[1]
    kc = conv_w.shape[0]
    tm, tn = min(tm, M), min(tn, N)
    hb = tm // HALO
    vmem = 2 * (tm * K * 2 + HALO * K * 2 + 2 * K * tn * 2 + tm * tn * 2) + 6 * tm * tn * 4
    return pl.pallas_call(
        functools.partial(_ffn_up_kernel, kc=kc, tm=tm),
        grid=(M // tm, N // tn),
        in_specs=[pl.BlockSpec((tm, K), lambda i, j: (i, 0)),
                  pl.BlockSpec((HALO, K), lambda i, j: (jnp.maximum(i * hb - 1, 0), 0)),
                  pl.BlockSpec((K, tn), lambda i, j: (0, j)),
                  pl.BlockSpec((K, tn), lambda i, j: (0, j)),
                  pl.BlockSpec((kc, tn), lambda i, j: (0, j))],
        out_specs=pl.BlockSpec((tm, tn), lambda i, j: (i, j)),
        out_shape=jax.ShapeDtypeStruct((M, N), BF16),
        scratch_shapes=[pltpu.VMEM((tm + HALO, tn), F32)],
        compiler_params=_cparams(("parallel", "parallel"), vmem),
        name="ffn_up",
    )(x, x, w_gate, w_up, conv_w)


def _mm_res_ln_kernel(a_ref, w_ref, res_ref, g_ref, b_ref, of_ref, ob_ref, *, sub):
    for r0 in range(0, a_ref.shape[0], sub):
        rows = slice(r0, r0 + sub)
        acc = jnp.dot(a_ref[rows, :], w_ref[...], preferred_element_type=F32)
        y = DN_ALPHA * res_ref[rows, :] + acc
        mu = jnp.mean(y, axis=-1, keepdims=True)
        yc = y - mu
        var = jnp.mean(yc * yc, axis=-1, keepdims=True)
        out = yc * lax.rsqrt(var + LN_EPS) * g_ref[...] + b_ref[...]
        of_ref[rows, :] = out
        ob_ref[rows, :] = out.astype(BF16)


def _proj_res_ln(a, w, res, g, b, tm, sub):
    M, K = a.shape
    N = w.shape[1]
    tm = min(tm, M)
    sub = min(sub, tm)
    vmem = K * N * 2 + 2 * (tm * K * 2 + tm * N * 4 + tm * N * 4 + tm * N * 2) + 4 * sub * N * 4
    return pl.pallas_call(
        functools.partial(_mm_res_ln_kernel, sub=sub),
        grid=(M // tm,),
        in_specs=[pl.BlockSpec((tm, K), lambda i: (i, 0)),
                  pl.BlockSpec((K, N), lambda i: (0, 0), pipeline_mode=pl.Buffered(1)),
                  pl.BlockSpec((tm, N), lambda i: (i, 0)),
                  pl.BlockSpec((1, N), lambda i: (0, 0)),
                  pl.BlockSpec((1, N), lambda i: (0, 0))],
        out_specs=[pl.BlockSpec((tm, N), lambda i: (i, 0)),
                   pl.BlockSpec((tm, N), lambda i: (i, 0))],
        out_shape=[jax.ShapeDtypeStruct((M, N), F32), jax.ShapeDtypeStruct((M, N), BF16)],
        compiler_params=_cparams(("parallel",), vmem),
        name="proj_res_ln",
    )(a, w, res, g.reshape(1, N), b.reshape(1, N))


def _ple_kernel(xb_ref, wg_ref, p_ref, wp_ref, xr_ref, of_ref, ob_ref):
    gate = jax.nn.sigmoid(jnp.dot(xb_ref[...], wg_ref[...], preferred_element_type=F32))
    pe = jnp.dot(p_ref[...].astype(BF16), wp_ref[...], preferred_element_type=F32)
    out = xr_ref[...] + gate * pe
    of_ref[...] = out
    ob_ref[...] = out.astype(BF16)


def _ple(xb, xf, w_gate, p, w_proj, tm, tn):
    M, K = xb.shape
    N = w_gate.shape[1]
    P = p.shape[1]
    tm, tn = min(tm, M), min(tn, N)
    vmem = 2 * (tm * K * 2 + K * tn * 2 + tm * P * 4 + P * tn * 2 + tm * tn * 10) + 4 * tm * tn * 4
    return pl.pallas_call(
        _ple_kernel,
        grid=(M // tm, N // tn),
        in_specs=[pl.BlockSpec((tm, K), lambda i, j: (i, 0)),
                  pl.BlockSpec((K, tn), lambda i, j: (0, j)),
                  pl.BlockSpec((tm, P), lambda i, j: (i, 0)),
                  pl.BlockSpec((P, tn), lambda i, j: (0, j)),
                  pl.BlockSpec((tm, tn), lambda i, j: (i, j))],
        out_specs=[pl.BlockSpec((tm, tn), lambda i, j: (i, j)),
                   pl.BlockSpec((tm, tn), lambda i, j: (i, j))],
        out_shape=[jax.ShapeDtypeStruct((M, N), F32), jax.ShapeDtypeStruct((M, N), BF16)],
        compiler_params=_cparams(("parallel", "parallel"), vmem),
        name="ple",
    )(xb, w_gate, p, w_proj, xf)


GDN_HB = 8
GDN_NC = 4


def _gdn_kernel(q_ref, k_ref, v_ref, z_ref, sc_ref, hp_ref, ng_ref, o_ref, s_scr):
    C = GDN_CHUNK

    @pl.when(pl.program_id(1) == 0)
    def _():
        s_scr[...] = jnp.zeros_like(s_scr)

    row = lax.broadcasted_iota(I32, (C, C), 0)
    col = lax.broadcasted_iota(I32, (C, C), 1)
    tri = row >= col
    strict = row > col
    eye = row == col
    tri_f = tri.astype(F32)
    eye_f = eye.astype(F32)

    raw = sc_ref[...]
    a_log = hp_ref[0, 0:1, :]
    dt_b = hp_ref[0, 1:2, :]
    xs = raw + dt_b
    softplus = jnp.maximum(xs, 0.0) + jnp.log1p(jnp.exp(-jnp.abs(xs)))
    g_all = -jnp.exp(a_log) * softplus
    beta_all = jax.nn.sigmoid(raw)
    ng = ng_ref[...]

    units = [(c, j) for c in range(GDN_NC) for j in range(GDN_HB)]
    kb_l, rhs_l, decay_l, qd_l, kd_l, kbf_l, qbf_l, gl_l = [], [], [], [], [], [], [], []
    for c in range(GDN_NC):
        r0 = c * C
        gc = jnp.dot(tri_f, g_all[r0:r0 + C, :], precision=lax.Precision.HIGHEST,
                     preferred_element_type=F32)
        g_last = gc[C - 1:C, :]
        e_gc = jnp.exp(gc)
        e_rest = jnp.exp(g_last - gc)
        e_last = jnp.exp(g_last)
        beta_c = beta_all[r0:r0 + C, :]
        qn, kn = [], []
        for hq in range(GDN_HB // 2):
            qf = q_ref[r0:r0 + C, hq * GDN_DK:(hq + 1) * GDN_DK].astype(F32)
            kf = k_ref[r0:r0 + C, hq * GDN_DK:(hq + 1) * GDN_DK].astype(F32)
            qn.append(qf * lax.rsqrt(jnp.sum(qf * qf, axis=-1, keepdims=True) + RMS_EPS) * (GDN_DK ** -0.5))
            kn.append(kf * lax.rsqrt(jnp.sum(kf * kf, axis=-1, keepdims=True) + RMS_EPS))
        for j in range(GDN_HB):
            q_h, k_h = qn[j // 2], kn[j // 2]
            vf = v_ref[r0:r0 + C, j * GDN_DV:(j + 1) * GDN_DV].astype(F32)
            beta = beta_c[:, GDN_HB + j:GDN_HB + j + 1]
            kb = k_h * beta
            gcb = jnp.broadcast_to(gc[:, j:j + 1], (C, C))
            gcr = jnp.sum(jnp.where(eye, gcb, 0.0), axis=0, keepdims=True)
            decay_l.append(jnp.where(tri, jnp.exp(jnp.where(tri, gcb - gcr, 0.0)), 0.0))
            kb_l.append(kb.astype(BF16))
            rhs_l.append(jnp.concatenate([vf * beta, kb * e_gc[:, j:j + 1]], axis=-1).astype(BF16))
            qd_l.append((q_h * e_gc[:, j:j + 1]).astype(BF16))
            kd_l.append((k_h * e_rest[:, j:j + 1]).astype(BF16))
            kbf_l.append(k_h.astype(BF16))
            qbf_l.append(q_h.astype(BF16))
            gl_l.append(e_last[:, j:j + 1])

    n_u = len(units)
    kk_l = [lax.dot_general(kb_l[u], kbf_l[u], _NT, preferred_element_type=F32) for u in range(n_u)]
    qk_l = [lax.dot_general(qbf_l[u], kbf_l[u], _NT, preferred_element_type=F32) for u in range(n_u)]
    qk_l = [jnp.where(tri, qk_l[u] * decay_l[u], 0.0).astype(BF16) for u in range(n_u)]
    x_l = [(-jnp.where(strict, kk_l[u] * decay_l[u], 0.0)) for u in range(n_u)]
    t_l = [eye_f + x_l[u] for u in range(n_u)]
    x_l = [x.astype(BF16) for x in x_l]
    for _ in range(5):
        x_l = [jnp.dot(x, x, preferred_element_type=F32).astype(BF16) for x in x_l]
        t_l = [t + jnp.dot(t.astype(BF16), x, preferred_element_type=F32) for t, x in zip(t_l, x_l)]
    sol_l = [jnp.dot(t_l[u].astype(BF16), rhs_l[u], preferred_element_type=F32) for u in range(n_u)]

    s_cur = [s_scr[j] for j in range(GDN_HB)]
    for c in range(GDN_NC):
        r0 = c * C
        us = [c * GDN_HB + j for j in range(GDN_HB)]
        s_bf = [s.astype(BF16) for s in s_cur]
        ws_l = [jnp.dot(sol_l[u][:, GDN_DV:].astype(BF16), s_bf[j], preferred_element_type=F32)
                for j, u in enumerate(us)]
        qs_l = [jnp.dot(qd_l[u], s_bf[j], preferred_element_type=F32) for j, u in enumerate(us)]
        vn_l = [(sol_l[u][:, :GDN_DV] - ws_l[j]).astype(BF16) for j, u in enumerate(us)]
        kv_l = [lax.dot_general(kd_l[u], vn_l[j], _TN, preferred_element_type=F32) for j, u in enumerate(us)]
        ov_l = [jnp.dot(qk_l[u], vn_l[j], preferred_element_type=F32) for j, u in enumerate(us)]
        s_cur = [s_cur[j] * gl_l[u] + kv_l[j] for j, u in enumerate(us)]
        for j in range(GDN_HB):
            o = qs_l[j] + ov_l[j]
            zf = z_ref[r0:r0 + C, j * GDN_DV:(j + 1) * GDN_DV].astype(F32)
            o = o * lax.rsqrt(jnp.mean(o * o, axis=-1, keepdims=True) + RMS_EPS) * ng * _silu(zf)
            o_ref[r0:r0 + C, j * GDN_DV:(j + 1) * GDN_DV] = o.astype(o_ref.dtype)
    for j in range(GDN_HB):
        s_scr[j] = s_cur[j]


def _gdn_core(qkv, z, scal, hparams, norm_g):
    L = qkv.shape[0]
    G = GDN_V_HEADS // GDN_HB
    R = GDN_NC * GDN_CHUNK
    qw = (GDN_HB // 2) * GDN_DK
    vw = GDN_HB * GDN_DV
    k_blk0 = GDN_QK_W // qw
    v_blk0 = 2 * GDN_QK_W // vw
    vmem = 2 * (2 * R * qw * 2 + 2 * R * vw * 2 + R * LANES * 4 + R * vw * 2) + (16 << 20)
    return pl.pallas_call(
        _gdn_kernel,
        grid=(G, L // R),
        in_specs=[pl.BlockSpec((R, qw), lambda g, s: (s, g)),
                  pl.BlockSpec((R, qw), lambda g, s: (s, k_blk0 + g)),
                  pl.BlockSpec((R, vw), lambda g, s: (s, v_blk0 + g)),
                  pl.BlockSpec((R, vw), lambda g, s: (s, g)),
                  pl.BlockSpec((R, LANES), lambda g, s: (s, g)),
                  pl.BlockSpec((1, 8, LANES), lambda g, s: (g, 0, 0)),
                  pl.BlockSpec((1, GDN_DV), lambda g, s: (0, 0))],
        out_specs=pl.BlockSpec((R, vw), lambda g, s: (s, g)),
        out_shape=jax.ShapeDtypeStruct((L, GDN_V_W), BF16),
        scratch_shapes=[pltpu.VMEM((GDN_HB, GDN_DK, GDN_DV), F32)],
        compiler_params=_cparams(("parallel", "arbitrary"), vmem),
        name="gdn_core",
    )(qkv, qkv, qkv, z, scal, hparams, norm_g.reshape(1, GDN_DV))


def _gdn_mixer(xb, w_in, conv_w, a_log, dt_bias, norm_g):
    L = xb.shape[0]
    nqkv = 2 * GDN_QK_W + GDN_V_W
    w_qkv = w_in[:, :nqkv].astype(BF16)
    w_z = w_in[:, nqkv:nqkv + GDN_V_W].astype(BF16)
    w_ab = w_in[:, nqkv + GDN_V_W:]
    G = GDN_V_HEADS // GDN_HB
    w_a = w_ab[:, :GDN_V_HEADS].reshape(D_MODEL, G, GDN_HB)
    w_b = w_ab[:, GDN_V_HEADS:].reshape(D_MODEL, G, GDN_HB)
    w_sc = jnp.concatenate([w_a, w_b, jnp.zeros((D_MODEL, G, LANES - 2 * GDN_HB), F32)], axis=-1)
    w_sc = w_sc.reshape(D_MODEL, G * LANES).astype(BF16)
    qkv = _proj_conv_silu(xb, w_qkv, conv_w, tm=1024, tn=1024)
    ones_z = jnp.ones((1, GDN_V_W), F32)
    z = _matmul_scaled(xb, w_z, ones_z, BF16, tm=1024, tn=1024)
    scal = _matmul_scaled(xb, w_sc, jnp.ones((1, G * LANES), F32), F32, tm=1024, tn=G * LANES)
    hp = jnp.zeros((G, 8, LANES), F32)
    hp = hp.at[:, 0, :GDN_HB].set(a_log.reshape(G, GDN_HB))
    hp = hp.at[:, 1, :GDN_HB].set(dt_bias.reshape(G, GDN_HB))
    return _gdn_core(qkv, z, scal, hp, norm_g)


def _dsa_small_kernel(x_ref, w_ref, g_ref, b_ref, ki_ref, wi_ref):
    acc = jnp.dot(x_ref[...], w_ref[...], preferred_element_type=F32)
    ki = acc[:, :IDX_DIM]
    mu = jnp.mean(ki, axis=-1, keepdims=True)
    kc = ki - mu
    var = jnp.mean(kc * kc, axis=-1, keepdims=True)
    ki_ref[...] = (kc * lax.rsqrt(var + LN_EPS) * g_ref[...] + b_ref[...]).astype(ki_ref.dtype)
    wi_ref[...] = acc[:, IDX_DIM:] * ((IDX_HEADS ** -0.5) * (IDX_DIM ** -0.5))


def _dsa_small(xb, w_small, ln_g, ln_b, tm):
    M, K = xb.shape
    tm = min(tm, M)
    N = 2 * LANES
    vmem = 2 * (tm * K * 2 + K * N * 2 + tm * LANES * 6) + 4 * tm * N * 4
    return pl.pallas_call(
        _dsa_small_kernel,
        grid=(M // tm,),
        in_specs=[pl.BlockSpec((tm, K), lambda i: (i, 0)),
                  pl.BlockSpec((K, N), lambda i: (0, 0)),
                  pl.BlockSpec((1, IDX_DIM), lambda i: (0, 0)),
                  pl.BlockSpec((1, IDX_DIM), lambda i: (0, 0))],
        out_specs=[pl.BlockSpec((tm, IDX_DIM), lambda i: (i, 0)),
                   pl.BlockSpec((tm, LANES), lambda i: (i, 0))],
        out_shape=[jax.ShapeDtypeStruct((M, IDX_DIM), BF16), jax.ShapeDtypeStruct((M, LANES), F32)],
        compiler_params=_cparams(("parallel",), vmem),
        name="dsa_idx_proj",
    )(xb, w_small, ln_g.reshape(1, IDX_DIM), ln_b.reshape(1, IDX_DIM))


def _sortable_key(score):
    bits = lax.bitcast_convert_type(score, I32)
    return jnp.where(bits >= 0, bits, bits ^ jnp.int32(0x7FFFFFFF))


def _idx_kernel(qi_tab, kj_tab, last_tab,
                qidx_ref, kidx_ref, wi_ref, far_ref, near_ref, key_scr, plane_scr, cand_scr, w_scr,
                *, k_top, n_sub_total):
    s = pl.program_id(0)
    i = qi_tab[s]
    j = kj_tab[s]

    @pl.when(s == 0)
    def _():
        key_scr[...] = jnp.full_like(key_scr, INT_MIN)
        plane_scr[...] = jnp.zeros_like(plane_scr)
    nsub = IDX_KEY_BLOCK // IDX_SUB
    t_col = i * Q_BLOCK + lax.broadcasted_iota(I32, (Q_BLOCK, 1), 0)
    lane_sub = lax.broadcasted_iota(I32, (Q_BLOCK, IDX_SUB), 1)
    wi = wi_ref[...]

    for sub in range(nsub):
        ki_sub = kidx_ref[sub * IDX_SUB:(sub + 1) * IDX_SUB, :]
        acc = jnp.zeros((Q_BLOCK, IDX_SUB), F32)
        for h in range(IDX_HEADS):
            sc = lax.dot_general(qidx_ref[:, h * IDX_DIM:(h + 1) * IDX_DIM], ki_sub, _NT,
                                 preferred_element_type=F32)
            acc = acc + jnp.maximum(sc, 0.0) * wi[:, h:h + 1]
        s_idx = j * IDX_KEY_BLOCK + sub * IDX_SUB + lane_sub
        key_scr[j * nsub + sub] = jnp.where(s_idx <= t_col, _sortable_key(acc), INT_MIN)

    @pl.when(last_tab[s] == 1)
    def _():
        n_chunks = (i * Q_BLOCK + Q_BLOCK - 1) // IDX_SUB + 1

        def count(pred, ref_val):
            refb = jnp.broadcast_to(ref_val, (Q_BLOCK, LANES))

            def body(c, cnt):
                blk = key_scr[c]
                for l in range(IDX_SUB // LANES):
                    cnt = cnt + jnp.where(pred(blk[:, l * LANES:(l + 1) * LANES], refb), 1, 0)
                return cnt

            cnt = lax.fori_loop(0, n_chunks, body, jnp.zeros((Q_BLOCK, LANES), I32))
            return jnp.sum(cnt, axis=1, keepdims=True)

        tiles_per_chunk = IDX_SUB // LANES
        chunks_per_group = WORD_BITS // tiles_per_chunk
        n_tiles = n_chunks * tiles_per_chunk
        n_groups = (n_tiles + WORD_BITS - 1) // WORD_BITS

        def build_group(g, carry):
            def build_rows(r, carry_r):
                r8 = pl.multiple_of(r * 8, 8)
                a = [key_scr[g * chunks_per_group + t // tiles_per_chunk, pl.ds(r8, 8),
                             (t % tiles_per_chunk) * LANES:(t % tiles_per_chunk + 1) * LANES]
                     for t in range(WORD_BITS)]
                m, sh = 0x0000FFFF, 16
                while sh:
                    k = 0
                    while k < WORD_BITS:
                        x = (a[k] ^ lax.shift_right_logical(a[k + sh], jnp.int32(sh))) & jnp.int32(m)
                        a[k] = a[k] ^ x
                        a[k + sh] = a[k + sh] ^ jnp.left_shift(x, jnp.int32(sh))
                        k = (k + sh + 1) & ~sh
                    sh >>= 1
                    m = (m ^ (m << sh)) & 0xFFFFFFFF if sh else m
                a[0] = ~a[0]
                for p in range(WORD_BITS):
                    plane_scr[p, g, pl.ds(r8, 8), :] = a[p]
                return carry_r

            lax.fori_loop(0, Q_BLOCK // 8, build_rows, 0)
            return carry

        lax.fori_loop(0, n_groups, build_group, 0)
        n_groups_max = plane_scr.shape[1]
        for g in range(n_groups_max):
            n_valid = jnp.clip(n_tiles - g * WORD_BITS, 0, WORD_BITS)
            word = jnp.where(n_valid >= WORD_BITS, jnp.int32(-1),
                             jnp.where(n_valid <= 0, jnp.int32(0),
                                       jnp.left_shift(jnp.int32(-1), WORD_BITS - n_valid)))
            cand_scr[g] = jnp.broadcast_to(word, (Q_BLOCK, LANES))

        def plane_pass(p, thr_u, n_above, n_cand):
            cnt = jnp.zeros((Q_BLOCK, LANES), I32)
            for g in range(n_groups_max):
                w = plane_scr[p, g] & cand_scr[g]
                w_scr[g] = w
                cnt = cnt + lax.population_count(w)
            n_set = jnp.sum(cnt, axis=1, keepdims=True)
            take = n_above + n_set >= k_top
            take_b = jnp.broadcast_to(take, (Q_BLOCK, LANES))
            for g in range(n_groups_max):
                w = w_scr[g]
                cand_scr[g] = jnp.where(take_b, w, cand_scr[g] ^ w)
            bit = jnp.left_shift(jnp.int32(1), WORD_BITS - 1 - p)
            return (jnp.where(take, thr_u | bit, thr_u), jnp.where(take, n_above, n_above + n_set),
                    jnp.where(take, n_set, n_cand - n_set))

        def group_body(carry):
            grp, thr_u, n_above, n_cand, _ = carry
            for bb in range(BITS_PER_CHECK):
                thr_u, n_above, n_cand = plane_pass(grp * BITS_PER_CHECK + bb, thr_u, n_above, n_cand)
            return grp + 1, thr_u, n_above, n_cand, jnp.max(jnp.where(n_above + n_cand != k_top, 1, 0))

        _, thr_u, n_above, n_cand, n_tied_rows = lax.while_loop(
            lambda carry: (carry[0] < WORD_BITS // BITS_PER_CHECK) & (carry[4] != 0), group_body,
            (jnp.int32(0), jnp.zeros((Q_BLOCK, 1), I32), jnp.zeros((Q_BLOCK, 1), I32),
             jnp.broadcast_to(n_tiles * LANES, (Q_BLOCK, 1)).astype(I32), jnp.int32(1)))
        thr = thr_u ^ jnp.int32(INT_MIN)
        thr_b = jnp.broadcast_to(thr, (Q_BLOCK, IDX_SUB))

        def emit(c, sel):
            s_idx = c * IDX_SUB + lane_sub
            far = sel & (t_col - s_idx >= MAX_DISTANCE)
            far_ref[0, c] = jnp.where(far, 0.0, FAR_MASKED).astype(far_ref.dtype)
            key_scr[c] = jnp.where(sel, 1, 0)

        @pl.when(n_tied_rows == 0)
        def _():
            def sel_body(c, carry):
                emit(c, key_scr[c] >= thr_b)
                return carry

            lax.fori_loop(0, n_chunks, sel_body, 0)

        @pl.when(n_tied_rows != 0)
        def _():
            n_gt = count(lambda a, r: a > r, thr)
            need_eq = (k_top - n_gt).astype(F32)
            incl = (lax.broadcasted_iota(I32, (IDX_SUB, IDX_SUB), 0)
                    <= lax.broadcasted_iota(I32, (IDX_SUB, IDX_SUB), 1)).astype(BF16)

            def sel_body(c, carry):
                blk = key_scr[c]
                eq = blk == thr_b
                eq_f = jnp.where(eq, 1.0, 0.0)
                rank = carry + jnp.dot(eq_f.astype(BF16), incl, preferred_element_type=F32)
                s_idx = c * IDX_SUB + lane_sub
                emit(c, ((blk > thr_b) | (eq & (rank <= need_eq))) & (s_idx <= t_col))
                return carry + jnp.sum(eq_f, axis=1, keepdims=True)

            lax.fori_loop(0, n_chunks, sel_body, jnp.zeros((Q_BLOCK, 1), F32))

        def fill_body(c, carry):
            far_ref[0, c] = jnp.full((Q_BLOCK, IDX_SUB), FAR_MASKED, far_ref.dtype)
            return carry

        lax.fori_loop(n_chunks, n_sub_total, fill_body, 0)

        def window(blk_idx):
            per = IDX_SUB // Q_BLOCK
            chunk = key_scr[blk_idx // per]
            m = blk_idx % per
            out = chunk[:, 0:Q_BLOCK]
            for q in range(1, per):
                out = jnp.where(m == q, chunk[:, q * Q_BLOCK:(q + 1) * Q_BLOCK], out)
            return out

        r_i = lax.broadcasted_iota(I32, (Q_BLOCK, Q_BLOCK), 0)
        c_i = lax.broadcasted_iota(I32, (Q_BLOCK, Q_BLOCK), 1)
        d_lo = Q_BLOCK + r_i - c_i
        d_hi = r_i - c_i
        near_lo = (window(jnp.maximum(i - 1, 0)) != 0) & (d_lo < MAX_DISTANCE) & (i >= 1)
        near_hi = (window(i) != 0) & (d_hi >= 0) & (d_hi < MAX_DISTANCE)
        near_ref[:, 0:Q_BLOCK] = jnp.where(near_lo, 0.0, -jnp.inf).astype(near_ref.dtype)
        near_ref[:, Q_BLOCK:2 * Q_BLOCK] = jnp.where(near_hi, 0.0, -jnp.inf).astype(near_ref.dtype)


def _idx_select(proj, ki, wi, k_top):
    L = ki.shape[0]
    nq = L // Q_BLOCK
    n_sub_total = L // IDX_SUB
    qi_l, kj_l, last_l = [], [], []
    for i in range(nq):
        j_last = (i * Q_BLOCK + Q_BLOCK - 1) // IDX_KEY_BLOCK
        for j in range(j_last + 1):
            qi_l.append(i)
            kj_l.append(j)
            last_l.append(1 if j == j_last else 0)
    tabs = [jnp.asarray(np.asarray(t, np.int32)) for t in (qi_l, kj_l, last_l)]
    qcol = (DSA_HEADS * DSA_DH) // (IDX_HEADS * IDX_DIM)
    group_keys = WORD_BITS * LANES
    n_groups_max = -(-L // group_keys)
    key_chunks = n_groups_max * (group_keys // IDX_SUB)
    vmem = ((key_chunks * IDX_SUB + (WORD_BITS + 2) * n_groups_max * LANES) * Q_BLOCK * 4
            + 2 * (Q_BLOCK * L * 2) + 2 * (Q_BLOCK * IDX_HEADS * IDX_DIM * 2)
            + 2 * IDX_KEY_BLOCK * IDX_DIM * 2 + (8 << 20))
    grid_spec = pltpu.PrefetchScalarGridSpec(
        num_scalar_prefetch=3,
        grid=(len(qi_l),),
        in_specs=[pl.BlockSpec((Q_BLOCK, IDX_HEADS * IDX_DIM), lambda s, qt, kt, lt: (qt[s], qcol)),
                  pl.BlockSpec((IDX_KEY_BLOCK, IDX_DIM), lambda s, qt, kt, lt: (kt[s], 0)),
                  pl.BlockSpec((Q_BLOCK, LANES), lambda s, qt, kt, lt: (qt[s], 0))],
        out_specs=[pl.BlockSpec((1, n_sub_total, Q_BLOCK, IDX_SUB), lambda s, qt, kt, lt: (qt[s], 0, 0, 0)),
                   pl.BlockSpec((Q_BLOCK, 2 * Q_BLOCK), lambda s, qt, kt, lt: (qt[s], 0))],
        scratch_shapes=[pltpu.VMEM((key_chunks, Q_BLOCK, IDX_SUB), I32),
                        pltpu.VMEM((WORD_BITS, n_groups_max, Q_BLOCK, LANES), I32),
                        pltpu.VMEM((n_groups_max, Q_BLOCK, LANES), I32),
                        pltpu.VMEM((n_groups_max, Q_BLOCK, LANES), I32)],
    )
    return pl.pallas_call(
        functools.partial(_idx_kernel, k_top=k_top, n_sub_total=n_sub_total),
        grid_spec=grid_spec,
        out_shape=[jax.ShapeDtypeStruct((nq, n_sub_total, Q_BLOCK, IDX_SUB), BF16),
                   jax.ShapeDtypeStruct((L, 2 * Q_BLOCK), BF16)],
        compiler_params=_cparams(("arbitrary",), vmem),
        name="dsa_idx_select",
    )(*tabs, proj, ki, wi)


M_INIT = -1e30
FAR_SUBS = 2
LOG2E = math.log2(math.e)


def _attn_kernel(qi_tab, kj_tab, kind_tab, first_tab,
                 tab_ref, q_ref, kt_ref, vf_ref, klo_ref, khi_ref, vlo_ref, vhi_ref, far_ref, near_ref,
                 o_ref, m_scr, l_scr, acc_scr, b_scr, s_scr):
    s = pl.program_id(0)

    @pl.when(s == 0)
    def _():
        r_i = lax.broadcasted_iota(I32, (Q_BLOCK, 2 * Q_BLOCK), 0)
        c_i = lax.broadcasted_iota(I32, (Q_BLOCK, 2 * Q_BLOCK), 1)
        d = jnp.maximum(Q_BLOCK + r_i - c_i, 0)
        max_exact = N_BUCKETS // 2
        df = jnp.maximum(d, 1).astype(F32)
        large = max_exact + (jnp.log(df / max_exact) / math.log(MAX_DISTANCE / max_exact)
                             * (N_BUCKETS - max_exact)).astype(I32)
        large = jnp.minimum(large, N_BUCKETS - 1)
        bkt = jnp.where(d < max_exact, d, large)
        for h in range(DSA_HEADS):
            acc = jnp.zeros((Q_BLOCK, 2 * Q_BLOCK), F32)
            for b in range(N_BUCKETS):
                acc = jnp.where(bkt == b, (tab_ref[b, h] - tab_ref[N_BUCKETS - 1, h]) * LOG2E, acc)
            b_scr[h] = acc

    @pl.when(first_tab[s] == 1)
    def _():
        m_scr[...] = jnp.full_like(m_scr, M_INIT)
        l_scr[...] = jnp.zeros_like(l_scr)
        acc_scr[...] = jnp.zeros_like(acc_scr)

    def softmax_pv(v_ref, width, row0=0):
        nt = width // LANES
        rows = slice(row0, row0 + width)
        ones = jnp.ones((width, DSA_DH), BF16)
        for h in range(DSA_HEADS):
            g = h // DSA_GROUP
            tiles = [s_scr[h, :, t * LANES:(t + 1) * LANES] for t in range(nt)]
            tmax = tiles[0]
            for t in range(1, nt):
                tmax = jnp.maximum(tmax, tiles[t])
            m_prev = m_scr[h]
            m_new = jnp.maximum(m_prev, jnp.max(tmax, axis=-1, keepdims=True))
            alpha = jnp.exp2(m_prev - m_new)
            p = [jnp.exp2(tiles[t] - m_new).astype(BF16) for t in range(nt)]
            pb = jnp.concatenate(p, axis=-1) if nt > 1 else p[0]
            v_aug = jnp.concatenate([v_ref[rows, g * DSA_DH:(g + 1) * DSA_DH], ones], axis=-1)
            pv = jnp.dot(pb, v_aug, preferred_element_type=F32)
            acc_scr[h] = alpha * acc_scr[h] + pv[:, :DSA_DH]
            l_scr[h] = alpha * l_scr[h] + pv[:, DSA_DH:]
            m_scr[h] = m_new

    @pl.when(kind_tab[s] == 0)
    def _():
        eye = (lax.broadcasted_iota(I32, (Q_BLOCK, Q_BLOCK), 0)
               == lax.broadcasted_iota(I32, (Q_BLOCK, Q_BLOCK), 1)).astype(BF16)
        for sub in range(FAR_SUBS):
            cols = slice(sub * IDX_SUB, (sub + 1) * IDX_SUB)
            mask = far_ref[0, sub]
            for g in range(DSA_KV_HEADS):
                lhs = jnp.concatenate(
                    [jnp.concatenate([q_ref[:, h * DSA_DH:(h + 1) * DSA_DH], eye], axis=1)
                     for h in range(g * DSA_GROUP, (g + 1) * DSA_GROUP)], axis=0)
                rhs = jnp.concatenate([kt_ref[g * DSA_DH:(g + 1) * DSA_DH, cols], mask], axis=0)
                logits = jnp.dot(lhs, rhs, preferred_element_type=F32)
                for hh in range(DSA_GROUP):
                    s_scr[g * DSA_GROUP + hh] = logits[hh * Q_BLOCK:(hh + 1) * Q_BLOCK, :]
            softmax_pv(vf_ref, IDX_SUB, sub * IDX_SUB)

    @pl.when(kind_tab[s] == 1)
    def _():
        for half, (k_ref, v_ref) in enumerate(((klo_ref, vlo_ref), (khi_ref, vhi_ref))):
            cols = slice(half * Q_BLOCK, (half + 1) * Q_BLOCK)
            mask = near_ref[:, cols].astype(F32)
            for h in range(DSA_HEADS):
                g = h // DSA_GROUP
                logits = lax.dot_general(q_ref[:, h * DSA_DH:(h + 1) * DSA_DH],
                                         k_ref[:, g * DSA_DH:(g + 1) * DSA_DH], _NT, preferred_element_type=F32)
                s_scr[h, :, 0:Q_BLOCK] = logits + b_scr[h][:, cols] + mask
            softmax_pv(v_ref, Q_BLOCK)
        for h in range(DSA_HEADS):
            o_ref[:, h * DSA_DH:(h + 1) * DSA_DH] = (acc_scr[h] / l_scr[h]).astype(o_ref.dtype)


def _masked_attention(proj, far, near, rel_bias):
    L = proj.shape[0]
    nq = L // Q_BLOCK
    far_keys = FAR_SUBS * IDX_SUB
    per = far_keys // Q_BLOCK
    qi_l, kj_l, kind_l, first_l = [], [], [], []
    for i in range(nq):
        n_far = -(-i // per)
        for j in range(n_far):
            qi_l.append(i); kj_l.append(j); kind_l.append(0); first_l.append(1 if j == 0 else 0)
        qi_l.append(i); kj_l.append(max(n_far - 1, 0)); kind_l.append(1); first_l.append(1 if n_far == 0 else 0)
    tabs = [jnp.asarray(np.asarray(t, np.int32)) for t in (qi_l, kj_l, kind_l, first_l)]
    qw = DSA_HEADS * DSA_DH
    kvw = DSA_KV_HEADS * DSA_DH
    k_col = (2 * qw) // kvw
    v_col = k_col + 1
    hw = DSA_HEADS
    vmem = (2 * (Q_BLOCK * qw * 2 * 2 + 2 * far_keys * kvw * 2 + 4 * Q_BLOCK * kvw * 2
                 + Q_BLOCK * far_keys * 2 + Q_BLOCK * 2 * Q_BLOCK * 2)
            + hw * Q_BLOCK * (3 * LANES + 2 * Q_BLOCK) * 4 + (16 << 20))
    idx = lambda f: (lambda s, qt, kt, kd, ft: f(qt[s], kt[s]))
    k_t = proj[:, k_col * kvw:(k_col + 1) * kvw].T
    grid_spec = pltpu.PrefetchScalarGridSpec(
        num_scalar_prefetch=4,
        grid=(len(qi_l),),
        in_specs=[pl.BlockSpec(memory_space=pltpu.SMEM),
                  pl.BlockSpec((Q_BLOCK, qw), idx(lambda i, j: (i, 0))),
                  pl.BlockSpec((kvw, far_keys), idx(lambda i, j: (0, j))),
                  pl.BlockSpec((far_keys, kvw), idx(lambda i, j: (j, v_col))),
                  pl.BlockSpec((Q_BLOCK, kvw), idx(lambda i, j: (jnp.maximum(i - 1, 0), k_col))),
                  pl.BlockSpec((Q_BLOCK, kvw), idx(lambda i, j: (i, k_col))),
                  pl.BlockSpec((Q_BLOCK, kvw), idx(lambda i, j: (jnp.maximum(i - 1, 0), v_col))),
                  pl.BlockSpec((Q_BLOCK, kvw), idx(lambda i, j: (i, v_col))),
                  pl.BlockSpec((1, FAR_SUBS, Q_BLOCK, IDX_SUB), idx(lambda i, j: (i, j, 0, 0))),
                  pl.BlockSpec((Q_BLOCK, 2 * Q_BLOCK), idx(lambda i, j: (i, 0)))],
        out_specs=pl.BlockSpec((Q_BLOCK, qw), idx(lambda i, j: (i, 0))),
        scratch_shapes=[pltpu.VMEM((hw, Q_BLOCK, LANES), F32),
                        pltpu.VMEM((hw, Q_BLOCK, LANES), F32),
                        pltpu.VMEM((hw, Q_BLOCK, DSA_DH), F32),
                        pltpu.VMEM((hw, Q_BLOCK, 2 * Q_BLOCK), F32),
                        pltpu.VMEM((hw, Q_BLOCK, IDX_SUB), F32)],
    )
    return pl.pallas_call(
        _attn_kernel,
        grid_spec=grid_spec,
        out_shape=jax.ShapeDtypeStruct((L, qw), BF16),
        compiler_params=_cparams(("arbitrary",), vmem),
        name="dsa_attention",
    )(*tabs, rel_bias, proj, k_t, proj, proj, proj, proj, proj, far, near)


def _dsa_mixer(xb, w_in, ln_g, ln_b, rel_bias):
    L = xb.shape[0]
    k_top = min(TOPK_MAX, L // 4)
    sq = DSA_HEADS * DSA_DH
    skv = DSA_KV_HEADS * DSA_DH
    si = IDX_HEADS * IDX_DIM
    w_q = w_in[:, :sq]
    w_k = w_in[:, sq:sq + skv]
    w_v = w_in[:, sq + skv:sq + 2 * skv]
    w_qi = w_in[:, sq + 2 * skv:sq + 2 * skv + si]
    w_ki = w_in[:, sq + 2 * skv + si:sq + 2 * skv + si + IDX_DIM]
    w_wi = w_in[:, sq + 2 * skv + si + IDX_DIM:]
    w_main = jnp.concatenate([w_q, w_qi, w_k, w_v], axis=1).astype(BF16)
    colscale = jnp.concatenate([jnp.full((1, sq), DSA_DH ** -0.5 * LOG2E, F32),
                                jnp.ones((1, si + 2 * skv), F32)], axis=1)
    w_small = jnp.concatenate([w_ki, w_wi, jnp.zeros((D_MODEL, LANES - IDX_HEADS), F32)], axis=1).astype(BF16)
    proj = _matmul_scaled(xb, w_main, colscale, BF16, tm=1024, tn=1024)
    ki, wi = _dsa_small(xb, w_small, ln_g, ln_b, tm=1024)
    far, near = _idx_select(proj, ki, wi, k_top)
    return _masked_attention(proj, far, near, rel_bias)


def kernel(x, p, gdn_w_in, gdn_conv_w, gdn_a_log, gdn_dt_bias, gdn_norm_g, gdn_w_o, dsa_w_in, dsa_kidx_ln_g, dsa_kidx_ln_b, dsa_w_o, rel_bias, ln1_g, ln1_b, ffn_w_gate, ffn_w_up, ffn_conv_w, ffn_w_down, ln2_g, ln2_b, ple_w_proj, ple_w_gate):
    assert x.shape[0] == 1 and x.shape[2] == D_MODEL
    xf = x[0]
    xb = xf.astype(BF16)
    ia = ib = 0
    for i in range(DEPTH):
        if i % 2 == 0:
            mix = _gdn_mixer(xb, gdn_w_in[ia], gdn_conv_w[ia], gdn_a_log[ia], gdn_dt_bias[ia], gdn_norm_g[ia])
            w_o = gdn_w_o[ia]
            ia += 1
        else:
            mix = _dsa_mixer(xb, dsa_w_in[ib], dsa_kidx_ln_g[ib], dsa_kidx_ln_b[ib], rel_bias)
            w_o = dsa_w_o[ib]
            ib += 1
        xf, xb = _proj_res_ln(mix, w_o.astype(BF16), xf, ln1_g[i], ln1_b[i], tm=512, sub=256)
        hmid = _ffn_up(xb, ffn_w_gate[i].astype(BF16), ffn_w_up[i].astype(BF16), ffn_conv_w[i], tm=1024, tn=512)
        xf, xb = _proj_res_ln(hmid, ffn_w_down[i].astype(BF16), xf, ln2_g[i], ln2_b[i], tm=512, sub=256)
        xf, xb = _ple(xb, xf, ple_w_gate[i].astype(BF16), p[i, 0], ple_w_proj[i].astype(BF16), tm=1024, tn=1024)
    return xf[None]
```

```python
import functools
import math

import jax
import jax.numpy as jnp
import numpy as np
from jax import lax
from jax.experimental import pallas as pl
from jax.experimental.pallas import tpu as pltpu

F32 = jnp.float32
BF16 = jnp.bfloat16
I32 = jnp.int32

D_MODEL = 2048
GDN_QK_HEADS = 16
GDN_V_HEADS = 32
GDN_DK = 128
GDN_DV = 128
GDN_CONV = 4
GDN_CHUNK = 64
GDN_QK_W = GDN_QK_HEADS * GDN_DK
GDN_V_W = GDN_V_HEADS * GDN_DV
DSA_HEADS = 16
DSA_KV_HEADS = 4
DSA_GROUP = DSA_HEADS // DSA_KV_HEADS
DSA_DH = 128
IDX_HEADS = 16
IDX_DIM = 128
TOPK_MAX = 256
N_BUCKETS = 32
MAX_DISTANCE = 128
D_FF = 5120
FFN_CONV = 3
PLE_DIM = 256
DEPTH = 2
DN_ALPHA = (2.0 * DEPTH) ** 0.25
LN_EPS = 1e-5
RMS_EPS = 1e-6

V7X_VMEM_BYTES = 64 * 1024 * 1024
V7X_VMEM_BUDGET = 56 * 1024 * 1024
LANES = 128
BF16_SUBLANES = 16

HALO = BF16_SUBLANES
Q_BLOCK = 128
IDX_KEY_BLOCK = 2048
IDX_SUB = 512
INT_MIN = -(2 ** 31)
BITS_PER_CHECK = 4
WORD_BITS = 32
FAR_MASKED = -2e30

_NT = (((1,), (1,)), ((), ()))
_TN = (((0,), (0,)), ((), ()))


def _cparams(semantics, vmem_bytes):
    return pltpu.CompilerParams(dimension_semantics=semantics,
                                vmem_limit_bytes=int(min(V7X_VMEM_BUDGET, vmem_bytes)))


def _silu(y):
    return y * jax.nn.sigmoid(y)


def _mm_scale_kernel(x_ref, w_ref, cs_ref, o_ref):
    acc = jnp.dot(x_ref[...], w_ref[...].astype(BF16), preferred_element_type=F32)
    o_ref[...] = (acc * cs_ref[...]).astype(o_ref.dtype)


def _weight_spec(w, K, tn, w_block):
    if w_block is None:
        return pl.BlockSpec((K, tn), lambda i, j: (0, j))
    return pl.BlockSpec((None, K, tn), lambda i, j: w_block(j))


def _matmul_scaled(x, w, colscale, out_dtype, tm, tn, w_block=None):
    M, K = x.shape
    N = colscale.shape[1]
    tm, tn = min(tm, M), min(tn, N)
    osz = jnp.dtype(out_dtype).itemsize
    wsz = jnp.dtype(w.dtype).itemsize
    vmem = 2 * (tm * K * 2 + K * tn * wsz + tm * tn * osz) + K * tn * 2 + 2 * tm * tn * 4
    return pl.pallas_call(
        _mm_scale_kernel,
        grid=(M // tm, N // tn),
        in_specs=[pl.BlockSpec((tm, K), lambda i, j: (i, 0)),
                  _weight_spec(w, K, tn, w_block),
                  pl.BlockSpec((1, tn), lambda i, j: (0, j))],
        out_specs=pl.BlockSpec((tm, tn), lambda i, j: (i, j)),
        out_shape=jax.ShapeDtypeStruct((M, N), out_dtype),
        compiler_params=_cparams(("parallel", "parallel"), vmem),
        name="matmul_scaled",
    )(x, w, colscale)


CONV_SUB = 256
FFN_SUB = 512


def _causal_conv(g, gh, cw_ref, g_scr, kc, tm, cols):
    g_scr[0:HALO, cols] = gh
    g_scr[HALO:HALO + tm, cols] = g
    y = cw_ref[kc - 1:kc, cols] * g
    for j in range(kc - 1):
        off = HALO - (kc - 1) + j
        y = y + cw_ref[j:j + 1, cols] * g_scr[off:off + tm, cols]
    return y


def _mm_conv_silu_kernel(x_ref, xh_ref, w_ref, cw_ref, o_ref, g_scr, *, kc, tm):
    first = pl.program_id(0) == 0
    for c0 in range(0, o_ref.shape[1], CONV_SUB):
        cols = slice(c0, c0 + CONV_SUB)
        w = w_ref[:, cols].astype(BF16)
        gh = jnp.dot(xh_ref[...], w, preferred_element_type=F32)
        gh = jnp.where(first, 0.0, gh)
        g = jnp.dot(x_ref[...], w, preferred_element_type=F32)
        y = _causal_conv(g, gh, cw_ref, g_scr, kc, tm, cols)
        o_ref[:, cols] = _silu(y).astype(o_ref.dtype)


def _proj_conv_silu(x, w, conv_w, tm, tn, w_block=None):
    M, K = x.shape
    kc, N = conv_w.shape
    tm, tn = min(tm, M), min(tn, N)
    hb = tm // HALO
    wsz = jnp.dtype(w.dtype).itemsize
    vmem = 2 * (tm * K * 2 + HALO * K * 2 + K * tn * wsz + tm * tn * 2) + K * tn * 2 + 4 * tm * tn * 4
    return pl.pallas_call(
        functools.partial(_mm_conv_silu_kernel, kc=kc, tm=tm),
        grid=(M // tm, N // tn),
        in_specs=[pl.BlockSpec((tm, K), lambda i, j: (i, 0)),
                  pl.BlockSpec((HALO, K), lambda i, j: (jnp.maximum(i * hb - 1, 0), 0)),
                  _weight_spec(w, K, tn, w_block),
                  pl.BlockSpec((kc, tn), lambda i, j: (0, j))],
        out_specs=pl.BlockSpec((tm, tn), lambda i, j: (i, j)),
        out_shape=jax.ShapeDtypeStruct((M, N), BF16),
        scratch_shapes=[pltpu.VMEM((tm + HALO, tn), F32)],
        compiler_params=_cparams(("parallel", "parallel"), vmem),
        name="proj_conv_silu",
    )(x, x, w, conv_w)


def _ffn_up_kernel(x_ref, xh_ref, wg_ref, wu_ref, cw_ref, o_ref, g_scr, *, kc, tm):
    first = pl.program_id(0) == 0
    for c0 in range(0, o_ref.shape[1], FFN_SUB):
        cols = slice(c0, c0 + FFN_SUB)
        wg = wg_ref[:, cols].astype(BF16)
        g = jnp.dot(x_ref[...], wg, preferred_element_type=F32)
        gh = jnp.dot(xh_ref[...], wg, preferred_element_type=F32)
        gh = jnp.where(first, 0.0, gh)
        u = jnp.dot(x_ref[...], wu_ref[:, cols].astype(BF16), preferred_element_type=F32)
        y = _causal_conv(g, gh, cw_ref, g_scr, kc, tm, cols)
        o_ref[:, cols] = (_silu(y) * u).astype(o_ref.dtype)


def _ffn_up(x, w_gate, w_up, conv_w, layer, tm, tn):
    M, K = x.shape
    N = w_gate.shape[2]
    kc = conv_w.shape[0]
    tm, tn = min(tm, M), min(tn, N)
    hb = tm // HALO
    wsz = jnp.dtype(w_gate.dtype).itemsize
    vmem = 2 * (tm * K * 2 + HALO * K * 2 + 2 * K * tn * wsz + tm * tn * 2) + 2 * K * tn * 2 + 6 * tm * tn * 4
    return pl.pallas_call(
        functools.partial(_ffn_up_kernel, kc=kc, tm=tm),
        grid=(M // tm, N // tn),
        in_specs=[pl.BlockSpec((tm, K), lambda i, j: (i, 0)),
                  pl.BlockSpec((HALO, K), lambda i, j: (jnp.maximum(i * hb - 1, 0), 0)),
                  pl.BlockSpec((None, K, tn), lambda i, j: (layer, 0, j)),
                  pl.BlockSpec((None, K, tn), lambda i, j: (layer, 0, j)),
                  pl.BlockSpec((kc, tn), lambda i, j: (0, j))],
        out_specs=pl.BlockSpec((tm, tn), lambda i, j: (i, j)),
        out_shape=jax.ShapeDtypeStruct((M, N), BF16),
        scratch_shapes=[pltpu.VMEM((tm + HALO, tn), F32)],
        compiler_params=_cparams(("parallel", "parallel"), vmem),
        name="ffn_up",
    )(x, x, w_gate, w_up, conv_w)


def _mm_res_ln_kernel(a_ref, w_ref, res_ref, g_ref, b_ref, of_ref, ob_ref, *, sub):
    for r0 in range(0, a_ref.shape[0], sub):
        rows = slice(r0, r0 + sub)
        acc = jnp.dot(a_ref[rows, :], w_ref[...], preferred_element_type=F32)
        y = DN_ALPHA * res_ref[rows, :] + acc
        mu = jnp.mean(y, axis=-1, keepdims=True)
        yc = y - mu
        var = jnp.mean(yc * yc, axis=-1, keepdims=True)
        out = yc * lax.rsqrt(var + LN_EPS) * g_ref[...] + b_ref[...]
        of_ref[rows, :] = out
        ob_ref[rows, :] = out.astype(BF16)


def _proj_res_ln(a, w, res, g, b, tm, sub):
    M, K = a.shape
    N = w.shape[1]
    tm = min(tm, M)
    sub = min(sub, tm)
    vmem = K * N * 2 + 2 * (tm * K * 2 + tm * N * 4 + tm * N * 4 + tm * N * 2) + 4 * sub * N * 4
    return pl.pallas_call(
        functools.partial(_mm_res_ln_kernel, sub=sub),
        grid=(M // tm,),
        in_specs=[pl.BlockSpec((tm, K), lambda i: (i, 0)),
                  pl.BlockSpec((K, N), lambda i: (0, 0), pipeline_mode=pl.Buffered(1)),
                  pl.BlockSpec((tm, N), lambda i: (i, 0)),
                  pl.BlockSpec((1, N), lambda i: (0, 0)),
                  pl.BlockSpec((1, N), lambda i: (0, 0))],
        out_specs=[pl.BlockSpec((tm, N), lambda i: (i, 0)),
                   pl.BlockSpec((tm, N), lambda i: (i, 0))],
        out_shape=[jax.ShapeDtypeStruct((M, N), F32), jax.ShapeDtypeStruct((M, N), BF16)],
        compiler_params=_cparams(("parallel",), vmem),
        name="proj_res_ln",
    )(a, w, res, g.reshape(1, N), b.reshape(1, N))


def _ple_kernel(xb_ref, wg_ref, p_ref, wp_ref, xr_ref, of_ref, ob_ref):
    gate = jax.nn.sigmoid(jnp.dot(xb_ref[...], wg_ref[...].astype(BF16), preferred_element_type=F32))
    pe = jnp.dot(p_ref[...].astype(BF16), wp_ref[...].astype(BF16), preferred_element_type=F32)
    out = xr_ref[...] + gate * pe
    of_ref[...] = out
    ob_ref[...] = out.astype(BF16)


def _ple(xb, xf, w_gate, p, w_proj, layer, tm, tn):
    M, K = xb.shape
    N = w_gate.shape[2]
    P = p.shape[3]
    tm, tn = min(tm, M), min(tn, N)
    vmem = 2 * (tm * K * 2 + K * tn * 4 + tm * P * 4 + P * tn * 4 + tm * tn * 10) + K * tn * 2 + 4 * tm * tn * 4
    return pl.pallas_call(
        _ple_kernel,
        grid=(M // tm, N // tn),
        in_specs=[pl.BlockSpec((tm, K), lambda i, j: (i, 0)),
                  pl.BlockSpec((None, K, tn), lambda i, j: (layer, 0, j)),
                  pl.BlockSpec((None, None, tm, P), lambda i, j: (layer, 0, i, 0)),
                  pl.BlockSpec((None, P, tn), lambda i, j: (layer, 0, j)),
                  pl.BlockSpec((tm, tn), lambda i, j: (i, j))],
        out_specs=[pl.BlockSpec((tm, tn), lambda i, j: (i, j)),
                   pl.BlockSpec((tm, tn), lambda i, j: (i, j))],
        out_shape=[jax.ShapeDtypeStruct((M, N), F32), jax.ShapeDtypeStruct((M, N), BF16)],
        compiler_params=_cparams(("parallel", "parallel"), vmem),
        name="ple",
    )(xb, w_gate, p, w_proj, xf)


GDN_HB = 8
GDN_NC = 4


def _gdn_kernel(q_ref, k_ref, v_ref, z_ref, sc_ref, hp_ref, ng_ref, o_ref, s_scr):
    C = GDN_CHUNK

    @pl.when(pl.program_id(1) == 0)
    def _():
        s_scr[...] = jnp.zeros_like(s_scr)

    row = lax.broadcasted_iota(I32, (C, C), 0)
    col = lax.broadcasted_iota(I32, (C, C), 1)
    tri = row >= col
    strict = row > col
    eye = row == col
    tri_f = tri.astype(F32)
    eye_f = eye.astype(F32)

    raw = sc_ref[...]
    a_log = hp_ref[0, 0:1, :]
    dt_b = hp_ref[0, 1:2, :]
    xs = raw + dt_b
    softplus = jnp.maximum(xs, 0.0) + jnp.log1p(jnp.exp(-jnp.abs(xs)))
    g_all = -jnp.exp(a_log) * softplus
    beta_all = jax.nn.sigmoid(raw)
    ng = ng_ref[...]

    units = [(c, j) for c in range(GDN_NC) for j in range(GDN_HB)]
    kb_l, rhs_l, decay_l, qd_l, kd_l, kbf_l, qbf_l, gl_l = [], [], [], [], [], [], [], []
    for c in range(GDN_NC):
        r0 = c * C
        gc = jnp.dot(tri_f, g_all[r0:r0 + C, :], precision=lax.Precision.HIGHEST,
                     preferred_element_type=F32)
        g_last = gc[C - 1:C, :]
        e_gc = jnp.exp(gc)
        e_rest = jnp.exp(g_last - gc)
        e_last = jnp.exp(g_last)
        beta_c = beta_all[r0:r0 + C, :]
        qn, kn = [], []
        for hq in range(GDN_HB // 2):
            qf = q_ref[r0:r0 + C, hq * GDN_DK:(hq + 1) * GDN_DK].astype(F32)
            kf = k_ref[r0:r0 + C, hq * GDN_DK:(hq + 1) * GDN_DK].astype(F32)
            qn.append(qf * lax.rsqrt(jnp.sum(qf * qf, axis=-1, keepdims=True) + RMS_EPS) * (GDN_DK ** -0.5))
            kn.append(kf * lax.rsqrt(jnp.sum(kf * kf, axis=-1, keepdims=True) + RMS_EPS))
        for j in range(GDN_HB):
            q_h, k_h = qn[j // 2], kn[j // 2]
            vf = v_ref[r0:r0 + C, j * GDN_DV:(j + 1) * GDN_DV].astype(F32)
            beta = beta_c[:, GDN_HB + j:GDN_HB + j + 1]
            kb = k_h * beta
            gcb = jnp.broadcast_to(gc[:, j:j + 1], (C, C))
            gcr = jnp.sum(jnp.where(eye, gcb, 0.0), axis=0, keepdims=True)
            decay_l.append(jnp.where(tri, jnp.exp(jnp.where(tri, gcb - gcr, 0.0)), 0.0))
            kb_l.append(kb.astype(BF16))
            rhs_l.append(jnp.concatenate([vf * beta, kb * e_gc[:, j:j + 1]], axis=-1).astype(BF16))
            qd_l.append((q_h * e_gc[:, j:j + 1]).astype(BF16))
            kd_l.append((k_h * e_rest[:, j:j + 1]).astype(BF16))
            kbf_l.append(k_h.astype(BF16))
            qbf_l.append(q_h.astype(BF16))
            gl_l.append(e_last[:, j:j + 1])

    n_u = len(units)
    kk_l = [lax.dot_general(kb_l[u], kbf_l[u], _NT, preferred_element_type=F32) for u in range(n_u)]
    qk_l = [lax.dot_general(qbf_l[u], kbf_l[u], _NT, preferred_element_type=F32) for u in range(n_u)]
    qk_l = [jnp.where(tri, qk_l[u] * decay_l[u], 0.0).astype(BF16) for u in range(n_u)]
    x_l = [(-jnp.where(strict, kk_l[u] * decay_l[u], 0.0)) for u in range(n_u)]
    t_l = [eye_f + x_l[u] for u in range(n_u)]
    x_l = [x.astype(BF16) for x in x_l]
    for _ in range(5):
        x_l = [jnp.dot(x, x, preferred_element_type=F32).astype(BF16) for x in x_l]
        t_l = [t + jnp.dot(t.astype(BF16), x, preferred_element_type=F32) for t, x in zip(t_l, x_l)]
    sol_l = [jnp.dot(t_l[u].astype(BF16), rhs_l[u], preferred_element_type=F32) for u in range(n_u)]

    s_cur = [s_scr[j] for j in range(GDN_HB)]
    for c in range(GDN_NC):
        r0 = c * C
        us = [c * GDN_HB + j for j in range(GDN_HB)]
        s_bf = [s.astype(BF16) for s in s_cur]
        ws_l = [jnp.dot(sol_l[u][:, GDN_DV:].astype(BF16), s_bf[j], preferred_element_type=F32)
                for j, u in enumerate(us)]
        qs_l = [jnp.dot(qd_l[u], s_bf[j], preferred_element_type=F32) for j, u in enumerate(us)]
        vn_l = [(sol_l[u][:, :GDN_DV] - ws_l[j]).astype(BF16) for j, u in enumerate(us)]
        kv_l = [lax.dot_general(kd_l[u], vn_l[j], _TN, preferred_element_type=F32) for j, u in enumerate(us)]
        ov_l = [jnp.dot(qk_l[u], vn_l[j], preferred_element_type=F32) for j, u in enumerate(us)]
        s_cur = [s_cur[j] * gl_l[u] + kv_l[j] for j, u in enumerate(us)]
        for j in range(GDN_HB):
            o = qs_l[j] + ov_l[j]
            zf = z_ref[r0:r0 + C, j * GDN_DV:(j + 1) * GDN_DV].astype(F32)
            o = o * lax.rsqrt(jnp.mean(o * o, axis=-1, keepdims=True) + RMS_EPS) * ng * _silu(zf)
            o_ref[r0:r0 + C, j * GDN_DV:(j + 1) * GDN_DV] = o.astype(o_ref.dtype)
    for j in range(GDN_HB):
        s_scr[j] = s_cur[j]


def _gdn_core(qkv, z, scal, hparams, norm_g):
    L = qkv.shape[0]
    G = GDN_V_HEADS // GDN_HB
    R = GDN_NC * GDN_CHUNK
    qw = (GDN_HB // 2) * GDN_DK
    vw = GDN_HB * GDN_DV
    k_blk0 = GDN_QK_W // qw
    v_blk0 = 2 * GDN_QK_W // vw
    vmem = 2 * (2 * R * qw * 2 + 2 * R * vw * 2 + R * LANES * 4 + R * vw * 2) + (16 << 20)
    return pl.pallas_call(
        _gdn_kernel,
        grid=(G, L // R),
        in_specs=[pl.BlockSpec((R, qw), lambda g, s: (s, g)),
                  pl.BlockSpec((R, qw), lambda g, s: (s, k_blk0 + g)),
                  pl.BlockSpec((R, vw), lambda g, s: (s, v_blk0 + g)),
                  pl.BlockSpec((R, vw), lambda g, s: (s, g)),
                  pl.BlockSpec((R, LANES), lambda g, s: (s, g)),
                  pl.BlockSpec((1, 8, LANES), lambda g, s: (g, 0, 0)),
                  pl.BlockSpec((1, GDN_DV), lambda g, s: (0, 0))],
        out_specs=pl.BlockSpec((R, vw), lambda g, s: (s, g)),
        out_shape=jax.ShapeDtypeStruct((L, GDN_V_W), BF16),
        scratch_shapes=[pltpu.VMEM((GDN_HB, GDN_DK, GDN_DV), F32)],
        compiler_params=_cparams(("parallel", "arbitrary"), vmem),
        name="gdn_core",
    )(qkv, qkv, qkv, z, scal, hparams, norm_g.reshape(1, GDN_DV))


def _gdn_mixer(xb, w_in_all, layer, conv_w, a_log, dt_bias, norm_g):
    nqkv = 2 * GDN_QK_W + GDN_V_W
    tn = 1024
    w_ab = w_in_all[layer, :, nqkv + GDN_V_W:]
    G = GDN_V_HEADS // GDN_HB
    w_a = w_ab[:, :GDN_V_HEADS].reshape(D_MODEL, G, GDN_HB)
    w_b = w_ab[:, GDN_V_HEADS:].reshape(D_MODEL, G, GDN_HB)
    w_sc = jnp.concatenate([w_a, w_b, jnp.zeros((D_MODEL, G, LANES - 2 * GDN_HB), F32)], axis=-1)
    w_sc = w_sc.reshape(D_MODEL, G * LANES).astype(BF16)
    qkv = _proj_conv_silu(xb, w_in_all, conv_w, tm=1024, tn=tn, w_block=lambda j: (layer, 0, j))
    ones_z = jnp.ones((1, GDN_V_W), F32)
    z = _matmul_scaled(xb, w_in_all, ones_z, BF16, tm=1024, tn=tn,
                       w_block=lambda j: (layer, 0, nqkv // tn + j))
    scal = _matmul_scaled(xb, w_sc, jnp.ones((1, G * LANES), F32), F32, tm=1024, tn=G * LANES)
    hp = jnp.zeros((G, 8, LANES), F32)
    hp = hp.at[:, 0, :GDN_HB].set(a_log.reshape(G, GDN_HB))
    hp = hp.at[:, 1, :GDN_HB].set(dt_bias.reshape(G, GDN_HB))
    return _gdn_core(qkv, z, scal, hp, norm_g)


def _dsa_small_kernel(x_ref, w_ref, g_ref, b_ref, ki_ref, wi_ref):
    acc = jnp.dot(x_ref[...], w_ref[...], preferred_element_type=F32)
    ki = acc[:, :IDX_DIM]
    mu = jnp.mean(ki, axis=-1, keepdims=True)
    kc = ki - mu
    var = jnp.mean(kc * kc, axis=-1, keepdims=True)
    ki_ref[...] = (kc * lax.rsqrt(var + LN_EPS) * g_ref[...] + b_ref[...]).astype(ki_ref.dtype)
    wi_ref[...] = acc[:, IDX_DIM:] * ((IDX_HEADS ** -0.5) * (IDX_DIM ** -0.5))


def _dsa_small(xb, w_small, ln_g, ln_b, tm):
    M, K = xb.shape
    tm = min(tm, M)
    N = 2 * LANES
    vmem = 2 * (tm * K * 2 + K * N * 2 + tm * LANES * 6) + 4 * tm * N * 4
    return pl.pallas_call(
        _dsa_small_kernel,
        grid=(M // tm,),
        in_specs=[pl.BlockSpec((tm, K), lambda i: (i, 0)),
                  pl.BlockSpec((K, N), lambda i: (0, 0)),
                  pl.BlockSpec((1, IDX_DIM), lambda i: (0, 0)),
                  pl.BlockSpec((1, IDX_DIM), lambda i: (0, 0))],
        out_specs=[pl.BlockSpec((tm, IDX_DIM), lambda i: (i, 0)),
                   pl.BlockSpec((tm, LANES), lambda i: (i, 0))],
        out_shape=[jax.ShapeDtypeStruct((M, IDX_DIM), BF16), jax.ShapeDtypeStruct((M, LANES), F32)],
        compiler_params=_cparams(("parallel",), vmem),
        name="dsa_idx_proj",
    )(xb, w_small, ln_g.reshape(1, IDX_DIM), ln_b.reshape(1, IDX_DIM))


def _sortable_key(score):
    bits = lax.bitcast_convert_type(score, I32)
    return jnp.where(bits >= 0, bits, bits ^ jnp.int32(0x7FFFFFFF))


def _idx_kernel(qi_tab, kj_tab, last_tab,
                qidx_ref, kidx_ref, wi_ref, far_ref, near_ref, key_scr, plane_scr, cand_scr, w_scr,
                *, k_top, n_sub_total):
    s = pl.program_id(0)
    i = qi_tab[s]
    j = kj_tab[s]

    @pl.when(s == 0)
    def _():
        key_scr[...] = jnp.full_like(key_scr, INT_MIN)
        plane_scr[...] = jnp.zeros_like(plane_scr)
    nsub = IDX_KEY_BLOCK // IDX_SUB
    t_col = i * Q_BLOCK + lax.broadcasted_iota(I32, (Q_BLOCK, 1), 0)
    lane_sub = lax.broadcasted_iota(I32, (Q_BLOCK, IDX_SUB), 1)
    wi = wi_ref[...]

    for sub in range(nsub):
        ki_sub = kidx_ref[sub * IDX_SUB:(sub + 1) * IDX_SUB, :]
        acc = jnp.zeros((Q_BLOCK, IDX_SUB), F32)
        for h in range(IDX_HEADS):
            sc = lax.dot_general(qidx_ref[:, h * IDX_DIM:(h + 1) * IDX_DIM], ki_sub, _NT,
                                 preferred_element_type=F32)
            acc = acc + jnp.maximum(sc, 0.0) * wi[:, h:h + 1]
        s_idx = j * IDX_KEY_BLOCK + sub * IDX_SUB + lane_sub
        key_scr[j * nsub + sub] = jnp.where(s_idx <= t_col, _sortable_key(acc), INT_MIN)

    @pl.when(last_tab[s] == 1)
    def _():
        n_chunks = (i * Q_BLOCK + Q_BLOCK - 1) // IDX_SUB + 1

        def count(pred, ref_val):
            refb = jnp.broadcast_to(ref_val, (Q_BLOCK, LANES))

            def body(c, cnt):
                blk = key_scr[c]
                for l in range(IDX_SUB // LANES):
                    cnt = cnt + jnp.where(pred(blk[:, l * LANES:(l + 1) * LANES], refb), 1, 0)
                return cnt

            cnt = lax.fori_loop(0, n_chunks, body, jnp.zeros((Q_BLOCK, LANES), I32))
            return jnp.sum(cnt, axis=1, keepdims=True)

        tiles_per_chunk = IDX_SUB // LANES
        chunks_per_group = WORD_BITS // tiles_per_chunk
        n_tiles = n_chunks * tiles_per_chunk
        n_groups = (n_tiles + WORD_BITS - 1) // WORD_BITS

        def build_group(g, carry):
            def build_rows(r, carry_r):
                r8 = pl.multiple_of(r * 8, 8)
                a = [key_scr[g * chunks_per_group + t // tiles_per_chunk, pl.ds(r8, 8),
                             (t % tiles_per_chunk) * LANES:(t % tiles_per_chunk + 1) * LANES]
                     for t in range(WORD_BITS)]
                m, sh = 0x0000FFFF, 16
                while sh:
                    k = 0
                    while k < WORD_BITS:
                        x = (a[k] ^ lax.shift_right_logical(a[k + sh], jnp.int32(sh))) & jnp.int32(m)
                        a[k] = a[k] ^ x
                        a[k + sh] = a[k + sh] ^ jnp.left_shift(x, jnp.int32(sh))
                        k = (k + sh + 1) & ~sh
                    sh >>= 1
                    m = (m ^ (m << sh)) & 0xFFFFFFFF if sh else m
                a[0] = ~a[0]
                for p in range(WORD_BITS):
                    plane_scr[p, g, pl.ds(r8, 8), :] = a[p]
                return carry_r

            lax.fori_loop(0, Q_BLOCK // 8, build_rows, 0)
            return carry

        lax.fori_loop(0, n_groups, build_group, 0)
        n_groups_max = plane_scr.shape[1]
        for g in range(n_groups_max):
            n_valid = jnp.clip(n_tiles - g * WORD_BITS, 0, WORD_BITS)
            word = jnp.where(n_valid >= WORD_BITS, jnp.int32(-1),
                             jnp.where(n_valid <= 0, jnp.int32(0),
                                       jnp.left_shift(jnp.int32(-1), WORD_BITS - n_valid)))
            cand_scr[g] = jnp.broadcast_to(word, (Q_BLOCK, LANES))

        def plane_pass(p, thr_u, n_above, n_cand):
            cnt = jnp.zeros((Q_BLOCK, LANES), I32)
            for g in range(n_groups_max):
                w = plane_scr[p, g] & cand_scr[g]
                w_scr[g] = w
                cnt = cnt + lax.population_count(w)
            n_set = jnp.sum(cnt, axis=1, keepdims=True)
            take = n_above + n_set >= k_top
            take_b = jnp.broadcast_to(take, (Q_BLOCK, LANES))
            for g in range(n_groups_max):
                w = w_scr[g]
                cand_scr[g] = jnp.where(take_b, w, cand_scr[g] ^ w)
            bit = jnp.left_shift(jnp.int32(1), WORD_BITS - 1 - p)
            return (jnp.where(take, thr_u | bit, thr_u), jnp.where(take, n_above, n_above + n_set),
                    jnp.where(take, n_set, n_cand - n_set))

        def group_body(carry):
            grp, thr_u, n_above, n_cand, _ = carry
            for bb in range(BITS_PER_CHECK):
                thr_u, n_above, n_cand = plane_pass(grp * BITS_PER_CHECK + bb, thr_u, n_above, n_cand)
            return grp + 1, thr_u, n_above, n_cand, jnp.max(jnp.where(n_above + n_cand != k_top, 1, 0))

        _, thr_u, n_above, n_cand, n_tied_rows = lax.while_loop(
            lambda carry: (carry[0] < WORD_BITS // BITS_PER_CHECK) & (carry[4] != 0), group_body,
            (jnp.int32(0), jnp.zeros((Q_BLOCK, 1), I32), jnp.zeros((Q_BLOCK, 1), I32),
             jnp.broadcast_to(n_tiles * LANES, (Q_BLOCK, 1)).astype(I32), jnp.int32(1)))
        thr = thr_u ^ jnp.int32(INT_MIN)
        thr_b = jnp.broadcast_to(thr, (Q_BLOCK, IDX_SUB))

        def emit(c, sel):
            s_idx = c * IDX_SUB + lane_sub
            far = sel & (t_col - s_idx >= MAX_DISTANCE)
            far_ref[0, c] = jnp.where(far, 0.0, FAR_MASKED).astype(far_ref.dtype)
            key_scr[c] = jnp.where(sel, 1, 0)

        @pl.when(n_tied_rows == 0)
        def _():
            def sel_body(c, carry):
                emit(c, key_scr[c] >= thr_b)
                return carry

            lax.fori_loop(0, n_chunks, sel_body, 0)

        @pl.when(n_tied_rows != 0)
        def _():
            n_gt = count(lambda a, r: a > r, thr)
            need_eq = (k_top - n_gt).astype(F32)
            incl = (lax.broadcasted_iota(I32, (IDX_SUB, IDX_SUB), 0)
                    <= lax.broadcasted_iota(I32, (IDX_SUB, IDX_SUB), 1)).astype(BF16)

            def sel_body(c, carry):
                blk = key_scr[c]
                eq = blk == thr_b
                eq_f = jnp.where(eq, 1.0, 0.0)
                rank = carry + jnp.dot(eq_f.astype(BF16), incl, preferred_element_type=F32)
                s_idx = c * IDX_SUB + lane_sub
                emit(c, ((blk > thr_b) | (eq & (rank <= need_eq))) & (s_idx <= t_col))
                return carry + jnp.sum(eq_f, axis=1, keepdims=True)

            lax.fori_loop(0, n_chunks, sel_body, jnp.zeros((Q_BLOCK, 1), F32))

        def fill_body(c, carry):
            far_ref[0, c] = jnp.full((Q_BLOCK, IDX_SUB), FAR_MASKED, far_ref.dtype)
            return carry

        lax.fori_loop(n_chunks, n_sub_total, fill_body, 0)

        def window(blk_idx):
            per = IDX_SUB // Q_BLOCK
            chunk = key_scr[blk_idx // per]
            m = blk_idx % per
            out = chunk[:, 0:Q_BLOCK]
            for q in range(1, per):
                out = jnp.where(m == q, chunk[:, q * Q_BLOCK:(q + 1) * Q_BLOCK], out)
            return out

        r_i = lax.broadcasted_iota(I32, (Q_BLOCK, Q_BLOCK), 0)
        c_i = lax.broadcasted_iota(I32, (Q_BLOCK, Q_BLOCK), 1)
        d_lo = Q_BLOCK + r_i - c_i
        d_hi = r_i - c_i
        near_lo = (window(jnp.maximum(i - 1, 0)) != 0) & (d_lo < MAX_DISTANCE) & (i >= 1)
        near_hi = (window(i) != 0) & (d_hi >= 0) & (d_hi < MAX_DISTANCE)
        near_ref[:, 0:Q_BLOCK] = jnp.where(near_lo, 0.0, -jnp.inf).astype(near_ref.dtype)
        near_ref[:, Q_BLOCK:2 * Q_BLOCK] = jnp.where(near_hi, 0.0, -jnp.inf).astype(near_ref.dtype)


def _idx_select(proj, ki, wi, k_top):
    L = ki.shape[0]
    nq = L // Q_BLOCK
    n_sub_total = L // IDX_SUB
    qi_l, kj_l, last_l = [], [], []
    for i in range(nq):
        j_last = (i * Q_BLOCK + Q_BLOCK - 1) // IDX_KEY_BLOCK
        for j in range(j_last + 1):
            qi_l.append(i)
            kj_l.append(j)
            last_l.append(1 if j == j_last else 0)
    tabs = [jnp.asarray(np.asarray(t, np.int32)) for t in (qi_l, kj_l, last_l)]
    qcol = (DSA_HEADS * DSA_DH) // (IDX_HEADS * IDX_DIM)
    group_keys = WORD_BITS * LANES
    n_groups_max = -(-L // group_keys)
    key_chunks = n_groups_max * (group_keys // IDX_SUB)
    vmem = ((key_chunks * IDX_SUB + (WORD_BITS + 2) * n_groups_max * LANES) * Q_BLOCK * 4
            + 2 * (Q_BLOCK * L * 2) + 2 * (Q_BLOCK * IDX_HEADS * IDX_DIM * 2)
            + 2 * IDX_KEY_BLOCK * IDX_DIM * 2 + (8 << 20))
    grid_spec = pltpu.PrefetchScalarGridSpec(
        num_scalar_prefetch=3,
        grid=(len(qi_l),),
        in_specs=[pl.BlockSpec((Q_BLOCK, IDX_HEADS * IDX_DIM), lambda s, qt, kt, lt: (qt[s], qcol)),
                  pl.BlockSpec((IDX_KEY_BLOCK, IDX_DIM), lambda s, qt, kt, lt: (kt[s], 0)),
                  pl.BlockSpec((Q_BLOCK, LANES), lambda s, qt, kt, lt: (qt[s], 0))],
        out_specs=[pl.BlockSpec((1, n_sub_total, Q_BLOCK, IDX_SUB), lambda s, qt, kt, lt: (qt[s], 0, 0, 0)),
                   pl.BlockSpec((Q_BLOCK, 2 * Q_BLOCK), lambda s, qt, kt, lt: (qt[s], 0))],
        scratch_shapes=[pltpu.VMEM((key_chunks, Q_BLOCK, IDX_SUB), I32),
                        pltpu.VMEM((WORD_BITS, n_groups_max, Q_BLOCK, LANES), I32),
                        pltpu.VMEM((n_groups_max, Q_BLOCK, LANES), I32),
                        pltpu.VMEM((n_groups_max, Q_BLOCK, LANES), I32)],
    )
    return pl.pallas_call(
        functools.partial(_idx_kernel, k_top=k_top, n_sub_total=n_sub_total),
        grid_spec=grid_spec,
        out_shape=[jax.ShapeDtypeStruct((nq, n_sub_total, Q_BLOCK, IDX_SUB), BF16),
                   jax.ShapeDtypeStruct((L, 2 * Q_BLOCK), BF16)],
        compiler_params=_cparams(("arbitrary",), vmem),
        name="dsa_idx_select",
    )(*tabs, proj, ki, wi)


M_INIT = -1e30
FAR_SUBS = 2
LOG2E = math.log2(math.e)


def _attn_kernel(qi_tab, kj_tab, kind_tab, first_tab,
                 tab_ref, q_ref, kt_ref, vf_ref, klo_ref, khi_ref, vlo_ref, vhi_ref, far_ref, near_ref,
                 o_ref, m_scr, l_scr, acc_scr, b_scr, s_scr):
    s = pl.program_id(0)

    @pl.when(s == 0)
    def _():
        r_i = lax.broadcasted_iota(I32, (Q_BLOCK, 2 * Q_BLOCK), 0)
        c_i = lax.broadcasted_iota(I32, (Q_BLOCK, 2 * Q_BLOCK), 1)
        d = jnp.maximum(Q_BLOCK + r_i - c_i, 0)
        max_exact = N_BUCKETS // 2
        df = jnp.maximum(d, 1).astype(F32)
        large = max_exact + (jnp.log(df / max_exact) / math.log(MAX_DISTANCE / max_exact)
                             * (N_BUCKETS - max_exact)).astype(I32)
        large = jnp.minimum(large, N_BUCKETS - 1)
        bkt = jnp.where(d < max_exact, d, large)
        for h in range(DSA_HEADS):
            acc = jnp.zeros((Q_BLOCK, 2 * Q_BLOCK), F32)
            for b in range(N_BUCKETS):
                acc = jnp.where(bkt == b, (tab_ref[b, h] - tab_ref[N_BUCKETS - 1, h]) * LOG2E, acc)
            b_scr[h] = acc

    @pl.when(first_tab[s] == 1)
    def _():
        m_scr[...] = jnp.full_like(m_scr, M_INIT)
        l_scr[...] = jnp.zeros_like(l_scr)
        acc_scr[...] = jnp.zeros_like(acc_scr)

    def softmax_pv(v_ref, width, row0=0):
        nt = width // LANES
        rows = slice(row0, row0 + width)
        ones = jnp.ones((width, DSA_DH), BF16)
        for h in range(DSA_HEADS):
            g = h // DSA_GROUP
            tiles = [s_scr[h, :, t * LANES:(t + 1) * LANES] for t in range(nt)]
            tmax = tiles[0]
            for t in range(1, nt):
                tmax = jnp.maximum(tmax, tiles[t])
            m_prev = m_scr[h]
            m_new = jnp.maximum(m_prev, jnp.max(tmax, axis=-1, keepdims=True))
            alpha = jnp.exp2(m_prev - m_new)
            p = [jnp.exp2(tiles[t] - m_new).astype(BF16) for t in range(nt)]
            pb = jnp.concatenate(p, axis=-1) if nt > 1 else p[0]
            v_aug = jnp.concatenate([v_ref[rows, g * DSA_DH:(g + 1) * DSA_DH], ones], axis=-1)
            pv = jnp.dot(pb, v_aug, preferred_element_type=F32)
            acc_scr[h] = alpha * acc_scr[h] + pv[:, :DSA_DH]
            l_scr[h] = alpha * l_scr[h] + pv[:, DSA_DH:]
            m_scr[h] = m_new

    @pl.when(kind_tab[s] == 0)
    def _():
        eye = (lax.broadcasted_iota(I32, (Q_BLOCK, Q_BLOCK), 0)
               == lax.broadcasted_iota(I32, (Q_BLOCK, Q_BLOCK), 1)).astype(BF16)
        for sub in range(FAR_SUBS):
            cols = slice(sub * IDX_SUB, (sub + 1) * IDX_SUB)
            mask = far_ref[0, sub]
            for g in range(DSA_KV_HEADS):
                lhs = jnp.concatenate(
                    [jnp.concatenate([q_ref[:, h * DSA_DH:(h + 1) * DSA_DH], eye], axis=1)
                     for h in range(g * DSA_GROUP, (g + 1) * DSA_GROUP)], axis=0)
                rhs = jnp.concatenate([kt_ref[g * DSA_DH:(g + 1) * DSA_DH, cols], mask], axis=0)
                logits = jnp.dot(lhs, rhs, preferred_element_type=F32)
                for hh in range(DSA_GROUP):
                    s_scr[g * DSA_GROUP + hh] = logits[hh * Q_BLOCK:(hh + 1) * Q_BLOCK, :]
            softmax_pv(vf_ref, IDX_SUB, sub * IDX_SUB)

    @pl.when(kind_tab[s] == 1)
    def _():
        for half, (k_ref, v_ref) in enumerate(((klo_ref, vlo_ref), (khi_ref, vhi_ref))):
            cols = slice(half * Q_BLOCK, (half + 1) * Q_BLOCK)
            mask = near_ref[:, cols].astype(F32)
            for h in range(DSA_HEADS):
                g = h // DSA_GROUP
                logits = lax.dot_general(q_ref[:, h * DSA_DH:(h + 1) * DSA_DH],
                                         k_ref[:, g * DSA_DH:(g + 1) * DSA_DH], _NT, preferred_element_type=F32)
                s_scr[h, :, 0:Q_BLOCK] = logits + b_scr[h][:, cols] + mask
            softmax_pv(v_ref, Q_BLOCK)
        for h in range(DSA_HEADS):
            o_ref[:, h * DSA_DH:(h + 1) * DSA_DH] = (acc_scr[h] / l_scr[h]).astype(o_ref.dtype)


def _masked_attention(proj, far, near, rel_bias):
    L = proj.shape[0]
    nq = L // Q_BLOCK
    far_keys = FAR_SUBS * IDX_SUB
    per = far_keys // Q_BLOCK
    qi_l, kj_l, kind_l, first_l = [], [], [], []
    for i in range(nq):
        n_far = -(-i // per)
        for j in range(n_far):
            qi_l.append(i); kj_l.append(j); kind_l.append(0); first_l.append(1 if j == 0 else 0)
        qi_l.append(i); kj_l.append(max(n_far - 1, 0)); kind_l.append(1); first_l.append(1 if n_far == 0 else 0)
    tabs = [jnp.asarray(np.asarray(t, np.int32)) for t in (qi_l, kj_l, kind_l, first_l)]
    qw = DSA_HEADS * DSA_DH
    kvw = DSA_KV_HEADS * DSA_DH
    k_col = (2 * qw) // kvw
    v_col = k_col + 1
    hw = DSA_HEADS
    vmem = (2 * (Q_BLOCK * qw * 2 * 2 + 2 * far_keys * kvw * 2 + 4 * Q_BLOCK * kvw * 2
                 + Q_BLOCK * far_keys * 2 + Q_BLOCK * 2 * Q_BLOCK * 2)
            + hw * Q_BLOCK * (3 * LANES + 2 * Q_BLOCK) * 4 + (16 << 20))
    idx = lambda f: (lambda s, qt, kt, kd, ft: f(qt[s], kt[s]))
    k_t = proj[:, k_col * kvw:(k_col + 1) * kvw].T
    grid_spec = pltpu.PrefetchScalarGridSpec(
        num_scalar_prefetch=4,
        grid=(len(qi_l),),
        in_specs=[pl.BlockSpec(memory_space=pltpu.SMEM),
                  pl.BlockSpec((Q_BLOCK, qw), idx(lambda i, j: (i, 0))),
                  pl.BlockSpec((kvw, far_keys), idx(lambda i, j: (0, j))),
                  pl.BlockSpec((far_keys, kvw), idx(lambda i, j: (j, v_col))),
                  pl.BlockSpec((Q_BLOCK, kvw), idx(lambda i, j: (jnp.maximum(i - 1, 0), k_col))),
                  pl.BlockSpec((Q_BLOCK, kvw), idx(lambda i, j: (i, k_col))),
                  pl.BlockSpec((Q_BLOCK, kvw), idx(lambda i, j: (jnp.maximum(i - 1, 0), v_col))),
                  pl.BlockSpec((Q_BLOCK, kvw), idx(lambda i, j: (i, v_col))),
                  pl.BlockSpec((1, FAR_SUBS, Q_BLOCK, IDX_SUB), idx(lambda i, j: (i, j, 0, 0))),
                  pl.BlockSpec((Q_BLOCK, 2 * Q_BLOCK), idx(lambda i, j: (i, 0)))],
        out_specs=pl.BlockSpec((Q_BLOCK, qw), idx(lambda i, j: (i, 0))),
        scratch_shapes=[pltpu.VMEM((hw, Q_BLOCK, LANES), F32),
                        pltpu.VMEM((hw, Q_BLOCK, LANES), F32),
                        pltpu.VMEM((hw, Q_BLOCK, DSA_DH), F32),
                        pltpu.VMEM((hw, Q_BLOCK, 2 * Q_BLOCK), F32),
                        pltpu.VMEM((hw, Q_BLOCK, IDX_SUB), F32)],
    )
    return pl.pallas_call(
        _attn_kernel,
        grid_spec=grid_spec,
        out_shape=jax.ShapeDtypeStruct((L, qw), BF16),
        compiler_params=_cparams(("arbitrary",), vmem),
        name="dsa_attention",
    )(*tabs, rel_bias, proj, k_t, proj, proj, proj, proj, proj, far, near)


def _dsa_mixer(xb, w_in_all, layer, ln_g, ln_b, rel_bias):
    L = xb.shape[0]
    k_top = min(TOPK_MAX, L // 4)
    sq = DSA_HEADS * DSA_DH
    skv = DSA_KV_HEADS * DSA_DH
    si = IDX_HEADS * IDX_DIM
    w_ki = w_in_all[layer, :, sq + 2 * skv + si:sq + 2 * skv + si + IDX_DIM]
    w_wi = w_in_all[layer, :, sq + 2 * skv + si + IDX_DIM:]
    colscale = jnp.concatenate([jnp.full((1, sq), DSA_DH ** -0.5 * LOG2E, F32),
                                jnp.ones((1, si + 2 * skv), F32)], axis=1)
    w_small = jnp.concatenate([w_ki, w_wi, jnp.zeros((D_MODEL, LANES - IDX_HEADS), F32)], axis=1).astype(BF16)
    nq, nkv = sq // skv, 1
    src_block = lambda j: jnp.where(j < nq, j, jnp.where(j < nq + si // skv, j + 2 * nkv, j - si // skv))
    proj = _matmul_scaled(xb, w_in_all, colscale, BF16, tm=1024, tn=skv,
                          w_block=lambda j: (layer, 0, src_block(j)))
    ki, wi = _dsa_small(xb, w_small, ln_g, ln_b, tm=1024)
    far, near = _idx_select(proj, ki, wi, k_top)
    return _masked_attention(proj, far, near, rel_bias)


def kernel(x, p, gdn_w_in, gdn_conv_w, gdn_a_log, gdn_dt_bias, gdn_norm_g, gdn_w_o, dsa_w_in, dsa_kidx_ln_g, dsa_kidx_ln_b, dsa_w_o, rel_bias, ln1_g, ln1_b, ffn_w_gate, ffn_w_up, ffn_conv_w, ffn_w_down, ln2_g, ln2_b, ple_w_proj, ple_w_gate):
    assert x.shape[0] == 1 and x.shape[2] == D_MODEL
    xf = x[0]
    xb = xf.astype(BF16)
    ia = ib = 0
    for i in range(DEPTH):
        if i % 2 == 0:
            mix = _gdn_mixer(xb, gdn_w_in, ia, gdn_conv_w[ia], gdn_a_log[ia], gdn_dt_bias[ia], gdn_norm_g[ia])
            w_o = gdn_w_o[ia]
            ia += 1
        else:
            mix = _dsa_mixer(xb, dsa_w_in, ib, dsa_kidx_ln_g[ib], dsa_kidx_ln_b[ib], rel_bias)
            w_o = dsa_w_o[ib]
            ib += 1
        xf, xb = _proj_res_ln(mix, w_o.astype(BF16), xf, ln1_g[i], ln1_b[i], tm=512, sub=256)
        hmid = _ffn_up(xb, ffn_w_gate, ffn_w_up, ffn_conv_w[i], i, tm=1024, tn=512)
        xf, xb = _proj_res_ln(hmid, ffn_w_down[i].astype(BF16), xf, ln2_g[i], ln2_b[i], tm=512, sub=256)
        xf, xb = _ple(xb, xf, ple_w_gate, p, ple_w_proj, i, tm=1024, tn=1024)
    return xf[None]
```

```python
import functools
import math

import jax
import jax.numpy as jnp
import numpy as np
from jax import lax
from jax.experimental import pallas as pl
from jax.experimental.pallas import tpu as pltpu

F32 = jnp.float32
BF16 = jnp.bfloat16
I32 = jnp.int32

D_MODEL = 2048
GDN_QK_HEADS = 16
GDN_V_HEADS = 32
GDN_DK = 128
GDN_DV = 128
GDN_CONV = 4
GDN_CHUNK = 64
GDN_QK_W = GDN_QK_HEADS * GDN_DK
GDN_V_W = GDN_V_HEADS * GDN_DV
DSA_HEADS = 16
DSA_KV_HEADS = 4
DSA_GROUP = DSA_HEADS // DSA_KV_HEADS
DSA_DH = 128
IDX_HEADS = 16
IDX_DIM = 128
TOPK_MAX = 256
N_BUCKETS = 32
MAX_DISTANCE = 128
D_FF = 5120
FFN_CONV = 3
PLE_DIM = 256
DEPTH = 2
DN_ALPHA = (2.0 * DEPTH) ** 0.25
LN_EPS = 1e-5
RMS_EPS = 1e-6

V7X_VMEM_BYTES = 64 * 1024 * 1024
V7X_VMEM_BUDGET = 56 * 1024 * 1024
LANES = 128
BF16_SUBLANES = 16

HALO = BF16_SUBLANES
Q_BLOCK = 128
IDX_KEY_BLOCK = 2048
IDX_SUB = 512
INT_MIN = -(2 ** 31)
BITS_PER_CHECK = 4
WORD_BITS = 32
FAR_MASKED = -2e30

_NT = (((1,), (1,)), ((), ()))
_TN = (((0,), (0,)), ((), ()))


def _cparams(semantics, vmem_bytes):
    return pltpu.CompilerParams(dimension_semantics=semantics,
                                vmem_limit_bytes=int(min(V7X_VMEM_BUDGET, vmem_bytes)))


def _silu(y):
    return y * jax.nn.sigmoid(y)


def _mm_scale_kernel(x_ref, w_ref, cs_ref, o_ref):
    acc = jnp.dot(x_ref[...], w_ref[...].astype(BF16), preferred_element_type=F32)
    o_ref[...] = (acc * cs_ref[...]).astype(o_ref.dtype)


def _weight_spec(w, K, tn, w_block):
    if w_block is None:
        return pl.BlockSpec((K, tn), lambda i, j: (0, j))
    return pl.BlockSpec((None, K, tn), lambda i, j: w_block(j))


def _matmul_scaled(x, w, colscale, out_dtype, tm, tn, w_block=None):
    M, K = x.shape
    N = colscale.shape[1]
    tm, tn = min(tm, M), min(tn, N)
    osz = jnp.dtype(out_dtype).itemsize
    wsz = jnp.dtype(w.dtype).itemsize
    vmem = 2 * (tm * K * 2 + K * tn * wsz + tm * tn * osz) + K * tn * 2 + 2 * tm * tn * 4
    return pl.pallas_call(
        _mm_scale_kernel,
        grid=(M // tm, N // tn),
        in_specs=[pl.BlockSpec((tm, K), lambda i, j: (i, 0)),
                  _weight_spec(w, K, tn, w_block),
                  pl.BlockSpec((1, tn), lambda i, j: (0, j))],
        out_specs=pl.BlockSpec((tm, tn), lambda i, j: (i, j)),
        out_shape=jax.ShapeDtypeStruct((M, N), out_dtype),
        compiler_params=_cparams(("parallel", "parallel"), vmem),
        name="matmul_scaled",
    )(x, w, colscale)


CONV_SUB = 256
FFN_SUB = 512


def _causal_conv(g, gh, cw_ref, g_scr, kc, tm, cols):
    g_scr[0:HALO, cols] = gh
    g_scr[HALO:HALO + tm, cols] = g
    y = cw_ref[kc - 1:kc, cols] * g
    for j in range(kc - 1):
        off = HALO - (kc - 1) + j
        y = y + cw_ref[j:j + 1, cols] * g_scr[off:off + tm, cols]
    return y


def _mm_conv_silu_kernel(x_ref, xh_ref, w_ref, cw_ref, o_ref, g_scr, *, kc, tm):
    first = pl.program_id(0) == 0
    for c0 in range(0, o_ref.shape[1], CONV_SUB):
        cols = slice(c0, c0 + CONV_SUB)
        w = w_ref[:, cols].astype(BF16)
        gh = jnp.dot(xh_ref[...], w, preferred_element_type=F32)
        gh = jnp.where(first, 0.0, gh)
        g = jnp.dot(x_ref[...], w, preferred_element_type=F32)
        y = _causal_conv(g, gh, cw_ref, g_scr, kc, tm, cols)
        o_ref[:, cols] = _silu(y).astype(o_ref.dtype)


def _proj_conv_silu(x, w, conv_w, tm, tn, w_block=None):
    M, K = x.shape
    kc, N = conv_w.shape
    tm, tn = min(tm, M), min(tn, N)
    hb = tm // HALO
    wsz = jnp.dtype(w.dtype).itemsize
    vmem = 2 * (tm * K * 2 + HALO * K * 2 + K * tn * wsz + tm * tn * 2) + K * tn * 2 + 4 * tm * tn * 4
    return pl.pallas_call(
        functools.partial(_mm_conv_silu_kernel, kc=kc, tm=tm),
        grid=(M // tm, N // tn),
        in_specs=[pl.BlockSpec((tm, K), lambda i, j: (i, 0)),
                  pl.BlockSpec((HALO, K), lambda i, j: (jnp.maximum(i * hb - 1, 0), 0)),
                  _weight_spec(w, K, tn, w_block),
                  pl.BlockSpec((kc, tn), lambda i, j: (0, j))],
        out_specs=pl.BlockSpec((tm, tn), lambda i, j: (i, j)),
        out_shape=jax.ShapeDtypeStruct((M, N), BF16),
        scratch_shapes=[pltpu.VMEM((tm + HALO, tn), F32)],
        compiler_params=_cparams(("parallel", "parallel"), vmem),
        name="proj_conv_silu",
    )(x, x, w, conv_w)


def _ffn_up_kernel(x_ref, xh_ref, wg_ref, wu_ref, cw_ref, o_ref, g_scr, *, kc, tm):
    first = pl.program_id(0) == 0
    for c0 in range(0, o_ref.shape[1], FFN_SUB):
        cols = slice(c0, c0 + FFN_SUB)
        wg = wg_ref[:, cols].astype(BF16)
        g = jnp.dot(x_ref[...], wg, preferred_element_type=F32)
        gh = jnp.dot(xh_ref[...], wg, preferred_element_type=F32)
        gh = jnp.where(first, 0.0, gh)
        u = jnp.dot(x_ref[...], wu_ref[:, cols].astype(BF16), preferred_element_type=F32)
        y = _causal_conv(g, gh, cw_ref, g_scr, kc, tm, cols)
        o_ref[:, cols] = (_silu(y) * u).astype(o_ref.dtype)


def _ffn_up(x, w_gate, w_up, conv_w, layer, tm, tn):
    M, K = x.shape
    N = w_gate.shape[2]
    kc = conv_w.shape[0]
    tm, tn = min(tm, M), min(tn, N)
    hb = tm // HALO
    wsz = jnp.dtype(w_gate.dtype).itemsize
    vmem = 2 * (tm * K * 2 + HALO * K * 2 + 2 * K * tn * wsz + tm * tn * 2) + 2 * K * tn * 2 + 6 * tm * tn * 4
    return pl.pallas_call(
        functools.partial(_ffn_up_kernel, kc=kc, tm=tm),
        grid=(M // tm, N // tn),
        in_specs=[pl.BlockSpec((tm, K), lambda i, j: (i, 0)),
                  pl.BlockSpec((HALO, K), lambda i, j: (jnp.maximum(i * hb - 1, 0), 0)),
                  pl.BlockSpec((None, K, tn), lambda i, j: (layer, 0, j)),
                  pl.BlockSpec((None, K, tn), lambda i, j: (layer, 0, j)),
                  pl.BlockSpec((kc, tn), lambda i, j: (0, j))],
        out_specs=pl.BlockSpec((tm, tn), lambda i, j: (i, j)),
        out_shape=jax.ShapeDtypeStruct((M, N), BF16),
        scratch_shapes=[pltpu.VMEM((tm + HALO, tn), F32)],
        compiler_params=_cparams(("parallel", "parallel"), vmem),
        name="ffn_up",
    )(x, x, w_gate, w_up, conv_w)


def _mm_res_ln_kernel(a_ref, w_ref, res_ref, g_ref, b_ref, of_ref, ob_ref, *, sub):
    for r0 in range(0, a_ref.shape[0], sub):
        rows = slice(r0, r0 + sub)
        acc = jnp.dot(a_ref[rows, :], w_ref[...], preferred_element_type=F32)
        y = DN_ALPHA * res_ref[rows, :] + acc
        mu = jnp.mean(y, axis=-1, keepdims=True)
        yc = y - mu
        var = jnp.mean(yc * yc, axis=-1, keepdims=True)
        out = yc * lax.rsqrt(var + LN_EPS) * g_ref[...] + b_ref[...]
        of_ref[rows, :] = out
        ob_ref[rows, :] = out.astype(BF16)


def _proj_res_ln(a, w, res, g, b, tm, sub):
    M, K = a.shape
    N = w.shape[1]
    tm = min(tm, M)
    sub = min(sub, tm)
    vmem = K * N * 2 + 2 * (tm * K * 2 + tm * N * 4 + tm * N * 4 + tm * N * 2) + 4 * sub * N * 4
    return pl.pallas_call(
        functools.partial(_mm_res_ln_kernel, sub=sub),
        grid=(M // tm,),
        in_specs=[pl.BlockSpec((tm, K), lambda i: (i, 0)),
                  pl.BlockSpec((K, N), lambda i: (0, 0), pipeline_mode=pl.Buffered(1)),
                  pl.BlockSpec((tm, N), lambda i: (i, 0)),
                  pl.BlockSpec((1, N), lambda i: (0, 0)),
                  pl.BlockSpec((1, N), lambda i: (0, 0))],
        out_specs=[pl.BlockSpec((tm, N), lambda i: (i, 0)),
                   pl.BlockSpec((tm, N), lambda i: (i, 0))],
        out_shape=[jax.ShapeDtypeStruct((M, N), F32), jax.ShapeDtypeStruct((M, N), BF16)],
        compiler_params=_cparams(("parallel",), vmem),
        name="proj_res_ln",
    )(a, w, res, g.reshape(1, N), b.reshape(1, N))


def _ple_kernel(xb_ref, wg_ref, p_ref, wp_ref, xr_ref, of_ref, ob_ref):
    gate = jax.nn.sigmoid(jnp.dot(xb_ref[...], wg_ref[...].astype(BF16), preferred_element_type=F32))
    pe = jnp.dot(p_ref[...].astype(BF16), wp_ref[...].astype(BF16), preferred_element_type=F32)
    out = xr_ref[...] + gate * pe
    of_ref[...] = out
    ob_ref[...] = out.astype(BF16)


def _ple(xb, xf, w_gate, p, w_proj, layer, tm, tn):
    M, K = xb.shape
    N = w_gate.shape[2]
    P = p.shape[3]
    tm, tn = min(tm, M), min(tn, N)
    wsz = jnp.dtype(w_gate.dtype).itemsize
    vmem = 2 * (tm * K * 2 + K * tn * wsz + tm * P * 4 + P * tn * 4 + tm * tn * 10) + K * tn * 2 + 4 * tm * tn * 4
    return pl.pallas_call(
        _ple_kernel,
        grid=(M // tm, N // tn),
        in_specs=[pl.BlockSpec((tm, K), lambda i, j: (i, 0)),
                  pl.BlockSpec((None, K, tn), lambda i, j: (layer, 0, j)),
                  pl.BlockSpec((None, None, tm, P), lambda i, j: (layer, 0, i, 0)),
                  pl.BlockSpec((None, P, tn), lambda i, j: (layer, 0, j)),
                  pl.BlockSpec((tm, tn), lambda i, j: (i, j))],
        out_specs=[pl.BlockSpec((tm, tn), lambda i, j: (i, j)),
                   pl.BlockSpec((tm, tn), lambda i, j: (i, j))],
        out_shape=[jax.ShapeDtypeStruct((M, N), F32), jax.ShapeDtypeStruct((M, N), BF16)],
        compiler_params=_cparams(("parallel", "parallel"), vmem),
        name="ple",
    )(xb, w_gate, p, w_proj, xf)


GDN_HB = 8
GDN_NC = 4


def _gdn_kernel(q_ref, k_ref, v_ref, z_ref, sc_ref, hp_ref, ng_ref, o_ref, s_scr):
    C = GDN_CHUNK

    @pl.when(pl.program_id(1) == 0)
    def _():
        s_scr[...] = jnp.zeros_like(s_scr)

    row = lax.broadcasted_iota(I32, (C, C), 0)
    col = lax.broadcasted_iota(I32, (C, C), 1)
    tri = row >= col
    strict = row > col
    eye = row == col
    tri_f = tri.astype(F32)
    eye_f = eye.astype(F32)

    raw = sc_ref[...]
    a_log = hp_ref[0, 0:1, :]
    dt_b = hp_ref[0, 1:2, :]
    xs = raw + dt_b
    softplus = jnp.maximum(xs, 0.0) + jnp.log1p(jnp.exp(-jnp.abs(xs)))
    g_all = -jnp.exp(a_log) * softplus
    beta_all = jax.nn.sigmoid(raw)
    ng = ng_ref[...]

    units = [(c, j) for c in range(GDN_NC) for j in range(GDN_HB)]
    kb_l, rhs_l, decay_l, qd_l, kd_l, kbf_l, qbf_l, gl_l = [], [], [], [], [], [], [], []
    for c in range(GDN_NC):
        r0 = c * C
        gc = jnp.dot(tri_f, g_all[r0:r0 + C, :], precision=lax.Precision.HIGHEST,
                     preferred_element_type=F32)
        g_last = gc[C - 1:C, :]
        e_gc = jnp.exp(gc)
        e_rest = jnp.exp(g_last - gc)
        e_last = jnp.exp(g_last)
        beta_c = beta_all[r0:r0 + C, :]
        qn, kn = [], []
        for hq in range(GDN_HB // 2):
            qf = q_ref[r0:r0 + C, hq * GDN_DK:(hq + 1) * GDN_DK].astype(F32)
            kf = k_ref[r0:r0 + C, hq * GDN_DK:(hq + 1) * GDN_DK].astype(F32)
            qn.append(qf * lax.rsqrt(jnp.sum(qf * qf, axis=-1, keepdims=True) + RMS_EPS) * (GDN_DK ** -0.5))
            kn.append(kf * lax.rsqrt(jnp.sum(kf * kf, axis=-1, keepdims=True) + RMS_EPS))
        for j in range(GDN_HB):
            q_h, k_h = qn[j // 2], kn[j // 2]
            vf = v_ref[r0:r0 + C, j * GDN_DV:(j + 1) * GDN_DV].astype(F32)
            beta = beta_c[:, GDN_HB + j:GDN_HB + j + 1]
            kb = k_h * beta
            gcb = jnp.broadcast_to(gc[:, j:j + 1], (C, C))
            gcr = jnp.sum(jnp.where(eye, gcb, 0.0), axis=0, keepdims=True)
            decay_l.append(jnp.where(tri, jnp.exp(jnp.where(tri, gcb - gcr, 0.0)), 0.0))
            kb_l.append(kb.astype(BF16))
            rhs_l.append(jnp.concatenate([vf * beta, kb * e_gc[:, j:j + 1]], axis=-1).astype(BF16))
            qd_l.append((q_h * e_gc[:, j:j + 1]).astype(BF16))
            kd_l.append((k_h * e_rest[:, j:j + 1]).astype(BF16))
            kbf_l.append(k_h.astype(BF16))
            qbf_l.append(q_h.astype(BF16))
            gl_l.append(e_last[:, j:j + 1])

    n_u = len(units)
    kk_l = [lax.dot_general(kb_l[u], kbf_l[u], _NT, preferred_element_type=F32) for u in range(n_u)]
    qk_l = [lax.dot_general(qbf_l[u], kbf_l[u], _NT, preferred_element_type=F32) for u in range(n_u)]
    qk_l = [jnp.where(tri, qk_l[u] * decay_l[u], 0.0).astype(BF16) for u in range(n_u)]
    x_l = [(-jnp.where(strict, kk_l[u] * decay_l[u], 0.0)) for u in range(n_u)]
    t_l = [eye_f + x_l[u] for u in range(n_u)]
    x_l = [x.astype(BF16) for x in x_l]
    for _ in range(5):
        x_l = [jnp.dot(x, x, preferred_element_type=F32).astype(BF16) for x in x_l]
        t_l = [t + jnp.dot(t.astype(BF16), x, preferred_element_type=F32) for t, x in zip(t_l, x_l)]
    sol_l = [jnp.dot(t_l[u].astype(BF16), rhs_l[u], preferred_element_type=F32) for u in range(n_u)]

    s_cur = [s_scr[j] for j in range(GDN_HB)]
    for c in range(GDN_NC):
        r0 = c * C
        us = [c * GDN_HB + j for j in range(GDN_HB)]
        s_bf = [s.astype(BF16) for s in s_cur]
        ws_l = [jnp.dot(sol_l[u][:, GDN_DV:].astype(BF16), s_bf[j], preferred_element_type=F32)
                for j, u in enumerate(us)]
        qs_l = [jnp.dot(qd_l[u], s_bf[j], preferred_element_type=F32) for j, u in enumerate(us)]
        vn_l = [(sol_l[u][:, :GDN_DV] - ws_l[j]).astype(BF16) for j, u in enumerate(us)]
        kv_l = [lax.dot_general(kd_l[u], vn_l[j], _TN, preferred_element_type=F32) for j, u in enumerate(us)]
        ov_l = [jnp.dot(qk_l[u], vn_l[j], preferred_element_type=F32) for j, u in enumerate(us)]
        s_cur = [s_cur[j] * gl_l[u] + kv_l[j] for j, u in enumerate(us)]
        for j in range(GDN_HB):
            o = qs_l[j] + ov_l[j]
            zf = z_ref[r0:r0 + C, j * GDN_DV:(j + 1) * GDN_DV].astype(F32)
            o = o * lax.rsqrt(jnp.mean(o * o, axis=-1, keepdims=True) + RMS_EPS) * ng * _silu(zf)
            o_ref[r0:r0 + C, j * GDN_DV:(j + 1) * GDN_DV] = o.astype(o_ref.dtype)
    for j in range(GDN_HB):
        s_scr[j] = s_cur[j]


def _gdn_core(qkv, z, scal, hparams, norm_g):
    L = qkv.shape[0]
    G = GDN_V_HEADS // GDN_HB
    R = GDN_NC * GDN_CHUNK
    qw = (GDN_HB // 2) * GDN_DK
    vw = GDN_HB * GDN_DV
    k_blk0 = GDN_QK_W // qw
    v_blk0 = 2 * GDN_QK_W // vw
    vmem = 2 * (2 * R * qw * 2 + 2 * R * vw * 2 + R * LANES * 4 + R * vw * 2) + (16 << 20)
    return pl.pallas_call(
        _gdn_kernel,
        grid=(G, L // R),
        in_specs=[pl.BlockSpec((R, qw), lambda g, s: (s, g)),
                  pl.BlockSpec((R, qw), lambda g, s: (s, k_blk0 + g)),
                  pl.BlockSpec((R, vw), lambda g, s: (s, v_blk0 + g)),
                  pl.BlockSpec((R, vw), lambda g, s: (s, g)),
                  pl.BlockSpec((R, LANES), lambda g, s: (s, g)),
                  pl.BlockSpec((1, 8, LANES), lambda g, s: (g, 0, 0)),
                  pl.BlockSpec((1, GDN_DV), lambda g, s: (0, 0))],
        out_specs=pl.BlockSpec((R, vw), lambda g, s: (s, g)),
        out_shape=jax.ShapeDtypeStruct((L, GDN_V_W), BF16),
        scratch_shapes=[pltpu.VMEM((GDN_HB, GDN_DK, GDN_DV), F32)],
        compiler_params=_cparams(("parallel", "arbitrary"), vmem),
        name="gdn_core",
    )(qkv, qkv, qkv, z, scal, hparams, norm_g.reshape(1, GDN_DV))


def _gdn_mixer(xb, w_in_all, layer, conv_w, a_log, dt_bias, norm_g):
    nqkv = 2 * GDN_QK_W + GDN_V_W
    tn = 1024
    w_ab = w_in_all[layer, :, nqkv + GDN_V_W:]
    G = GDN_V_HEADS // GDN_HB
    w_a = w_ab[:, :GDN_V_HEADS].reshape(D_MODEL, G, GDN_HB)
    w_b = w_ab[:, GDN_V_HEADS:].reshape(D_MODEL, G, GDN_HB)
    w_sc = jnp.concatenate([w_a, w_b, jnp.zeros((D_MODEL, G, LANES - 2 * GDN_HB), F32)], axis=-1)
    w_sc = w_sc.reshape(D_MODEL, G * LANES).astype(BF16)
    qkv = _proj_conv_silu(xb, w_in_all, conv_w, tm=1024, tn=tn, w_block=lambda j: (layer, 0, j))
    ones_z = jnp.ones((1, GDN_V_W), F32)
    z = _matmul_scaled(xb, w_in_all, ones_z, BF16, tm=1024, tn=tn,
                       w_block=lambda j: (layer, 0, nqkv // tn + j))
    scal = _matmul_scaled(xb, w_sc, jnp.ones((1, G * LANES), F32), F32, tm=1024, tn=G * LANES)
    hp = jnp.zeros((G, 8, LANES), F32)
    hp = hp.at[:, 0, :GDN_HB].set(a_log.reshape(G, GDN_HB))
    hp = hp.at[:, 1, :GDN_HB].set(dt_bias.reshape(G, GDN_HB))
    return _gdn_core(qkv, z, scal, hp, norm_g)


def _dsa_small_kernel(x_ref, w_ref, g_ref, b_ref, ki_ref, wi_ref):
    acc = jnp.dot(x_ref[...], w_ref[...], preferred_element_type=F32)
    ki = acc[:, :IDX_DIM]
    mu = jnp.mean(ki, axis=-1, keepdims=True)
    kc = ki - mu
    var = jnp.mean(kc * kc, axis=-1, keepdims=True)
    ki_ref[...] = (kc * lax.rsqrt(var + LN_EPS) * g_ref[...] + b_ref[...]).astype(ki_ref.dtype)
    wi_ref[...] = acc[:, IDX_DIM:] * ((IDX_HEADS ** -0.5) * (IDX_DIM ** -0.5))


def _dsa_small(xb, w_small, ln_g, ln_b, tm):
    M, K = xb.shape
    tm = min(tm, M)
    N = 2 * LANES
    vmem = 2 * (tm * K * 2 + K * N * 2 + tm * LANES * 6) + 4 * tm * N * 4
    return pl.pallas_call(
        _dsa_small_kernel,
        grid=(M // tm,),
        in_specs=[pl.BlockSpec((tm, K), lambda i: (i, 0)),
                  pl.BlockSpec((K, N), lambda i: (0, 0)),
                  pl.BlockSpec((1, IDX_DIM), lambda i: (0, 0)),
                  pl.BlockSpec((1, IDX_DIM), lambda i: (0, 0))],
        out_specs=[pl.BlockSpec((tm, IDX_DIM), lambda i: (i, 0)),
                   pl.BlockSpec((tm, LANES), lambda i: (i, 0))],
        out_shape=[jax.ShapeDtypeStruct((M, IDX_DIM), BF16), jax.ShapeDtypeStruct((M, LANES), F32)],
        compiler_params=_cparams(("parallel",), vmem),
        name="dsa_idx_proj",
    )(xb, w_small, ln_g.reshape(1, IDX_DIM), ln_b.reshape(1, IDX_DIM))


def _sortable_key(score):
    bits = lax.bitcast_convert_type(score, I32)
    return jnp.where(bits >= 0, bits, bits ^ jnp.int32(0x7FFFFFFF))


def _idx_kernel(qi_tab, kj_tab, last_tab,
                qidx_ref, kidx_ref, wi_ref, far_ref, near_ref, key_scr, plane_scr, cand_scr, w_scr,
                *, k_top, n_sub_total):
    s = pl.program_id(0)
    i = qi_tab[s]
    j = kj_tab[s]

    @pl.when(s == 0)
    def _():
        key_scr[...] = jnp.full_like(key_scr, INT_MIN)
        plane_scr[...] = jnp.zeros_like(plane_scr)
    nsub = IDX_KEY_BLOCK // IDX_SUB
    t_col = i * Q_BLOCK + lax.broadcasted_iota(I32, (Q_BLOCK, 1), 0)
    lane_sub = lax.broadcasted_iota(I32, (Q_BLOCK, IDX_SUB), 1)
    wi = wi_ref[...]

    for sub in range(nsub):
        ki_sub = kidx_ref[sub * IDX_SUB:(sub + 1) * IDX_SUB, :]
        acc = jnp.zeros((Q_BLOCK, IDX_SUB), F32)
        for h in range(IDX_HEADS):
            sc = lax.dot_general(qidx_ref[:, h * IDX_DIM:(h + 1) * IDX_DIM], ki_sub, _NT,
                                 preferred_element_type=F32)
            acc = acc + jnp.maximum(sc, 0.0) * wi[:, h:h + 1]
        s_idx = j * IDX_KEY_BLOCK + sub * IDX_SUB + lane_sub
        key_scr[j * nsub + sub] = jnp.where(s_idx <= t_col, _sortable_key(acc), INT_MIN)

    @pl.when(last_tab[s] == 1)
    def _():
        n_chunks = (i * Q_BLOCK + Q_BLOCK - 1) // IDX_SUB + 1

        def count(pred, ref_val):
            refb = jnp.broadcast_to(ref_val, (Q_BLOCK, LANES))

            def body(c, cnt):
                blk = key_scr[c]
                for l in range(IDX_SUB // LANES):
                    cnt = cnt + jnp.where(pred(blk[:, l * LANES:(l + 1) * LANES], refb), 1, 0)
                return cnt

            cnt = lax.fori_loop(0, n_chunks, body, jnp.zeros((Q_BLOCK, LANES), I32))
            return jnp.sum(cnt, axis=1, keepdims=True)

        tiles_per_chunk = IDX_SUB // LANES
        chunks_per_group = WORD_BITS // tiles_per_chunk
        n_tiles = n_chunks * tiles_per_chunk
        n_groups = (n_tiles + WORD_BITS - 1) // WORD_BITS

        def build_group(g, carry):
            def build_rows(r, carry_r):
                r8 = pl.multiple_of(r * 8, 8)
                a = [key_scr[g * chunks_per_group + t // tiles_per_chunk, pl.ds(r8, 8),
                             (t % tiles_per_chunk) * LANES:(t % tiles_per_chunk + 1) * LANES]
                     for t in range(WORD_BITS)]
                m, sh = 0x0000FFFF, 16
                while sh:
                    k = 0
                    while k < WORD_BITS:
                        x = (a[k] ^ lax.shift_right_logical(a[k + sh], jnp.int32(sh))) & jnp.int32(m)
                        a[k] = a[k] ^ x
                        a[k + sh] = a[k + sh] ^ jnp.left_shift(x, jnp.int32(sh))
                        k = (k + sh + 1) & ~sh
                    sh >>= 1
                    m = (m ^ (m << sh)) & 0xFFFFFFFF if sh else m
                a[0] = ~a[0]
                for p in range(WORD_BITS):
                    plane_scr[p, g, pl.ds(r8, 8), :] = a[p]
                return carry_r

            lax.fori_loop(0, Q_BLOCK // 8, build_rows, 0)
            return carry

        lax.fori_loop(0, n_groups, build_group, 0)
        n_groups_max = plane_scr.shape[1]
        for g in range(n_groups_max):
            n_valid = jnp.clip(n_tiles - g * WORD_BITS, 0, WORD_BITS)
            word = jnp.where(n_valid >= WORD_BITS, jnp.int32(-1),
                             jnp.where(n_valid <= 0, jnp.int32(0),
                                       jnp.left_shift(jnp.int32(-1), WORD_BITS - n_valid)))
            cand_scr[g] = jnp.broadcast_to(word, (Q_BLOCK, LANES))

        def plane_pass(p, thr_u, n_above, n_cand):
            cnt = jnp.zeros((Q_BLOCK, LANES), I32)
            for g in range(n_groups_max):
                w = plane_scr[p, g] & cand_scr[g]
                w_scr[g] = w
                cnt = cnt + lax.population_count(w)
            n_set = jnp.sum(cnt, axis=1, keepdims=True)
            take = n_above + n_set >= k_top
            take_b = jnp.broadcast_to(take, (Q_BLOCK, LANES))
            for g in range(n_groups_max):
                w = w_scr[g]
                cand_scr[g] = jnp.where(take_b, w, cand_scr[g] ^ w)
            bit = jnp.left_shift(jnp.int32(1), WORD_BITS - 1 - p)
            return (jnp.where(take, thr_u | bit, thr_u), jnp.where(take, n_above, n_above + n_set),
                    jnp.where(take, n_set, n_cand - n_set))

        def group_body(carry):
            grp, thr_u, n_above, n_cand, _ = carry
            for bb in range(BITS_PER_CHECK):
                thr_u, n_above, n_cand = plane_pass(grp * BITS_PER_CHECK + bb, thr_u, n_above, n_cand)
            return grp + 1, thr_u, n_above, n_cand, jnp.max(jnp.where(n_above + n_cand != k_top, 1, 0))

        _, thr_u, n_above, n_cand, n_tied_rows = lax.while_loop(
            lambda carry: (carry[0] < WORD_BITS // BITS_PER_CHECK) & (carry[4] != 0), group_body,
            (jnp.int32(0), jnp.zeros((Q_BLOCK, 1), I32), jnp.zeros((Q_BLOCK, 1), I32),
             jnp.broadcast_to(n_tiles * LANES, (Q_BLOCK, 1)).astype(I32), jnp.int32(1)))
        thr = thr_u ^ jnp.int32(INT_MIN)
        thr_b = jnp.broadcast_to(thr, (Q_BLOCK, IDX_SUB))

        def emit(c, sel):
            s_idx = c * IDX_SUB + lane_sub
            far = sel & (t_col - s_idx >= MAX_DISTANCE)
            far_ref[0, c] = jnp.where(far, 0.0, FAR_MASKED).astype(far_ref.dtype)
            key_scr[c] = jnp.where(sel, 1, 0)

        @pl.when(n_tied_rows == 0)
        def _():
            def sel_body(c, carry):
                emit(c, key_scr[c] >= thr_b)
                return carry

            lax.fori_loop(0, n_chunks, sel_body, 0)

        @pl.when(n_tied_rows != 0)
        def _():
            n_gt = count(lambda a, r: a > r, thr)
            need_eq = (k_top - n_gt).astype(F32)
            incl = (lax.broadcasted_iota(I32, (IDX_SUB, IDX_SUB), 0)
                    <= lax.broadcasted_iota(I32, (IDX_SUB, IDX_SUB), 1)).astype(BF16)

            def sel_body(c, carry):
                blk = key_scr[c]
                eq = blk == thr_b
                eq_f = jnp.where(eq, 1.0, 0.0)
                rank = carry + jnp.dot(eq_f.astype(BF16), incl, preferred_element_type=F32)
                s_idx = c * IDX_SUB + lane_sub
                emit(c, ((blk > thr_b) | (eq & (rank <= need_eq))) & (s_idx <= t_col))
                return carry + jnp.sum(eq_f, axis=1, keepdims=True)

            lax.fori_loop(0, n_chunks, sel_body, jnp.zeros((Q_BLOCK, 1), F32))

        def fill_body(c, carry):
            far_ref[0, c] = jnp.full((Q_BLOCK, IDX_SUB), FAR_MASKED, far_ref.dtype)
            return carry

        lax.fori_loop(n_chunks, n_sub_total, fill_body, 0)

        def window(blk_idx):
            per = IDX_SUB // Q_BLOCK
            chunk = key_scr[blk_idx // per]
            m = blk_idx % per
            out = chunk[:, 0:Q_BLOCK]
            for q in range(1, per):
                out = jnp.where(m == q, chunk[:, q * Q_BLOCK:(q + 1) * Q_BLOCK], out)
            return out

        r_i = lax.broadcasted_iota(I32, (Q_BLOCK, Q_BLOCK), 0)
        c_i = lax.broadcasted_iota(I32, (Q_BLOCK, Q_BLOCK), 1)
        d_lo = Q_BLOCK + r_i - c_i
        d_hi = r_i - c_i
        near_lo = (window(jnp.maximum(i - 1, 0)) != 0) & (d_lo < MAX_DISTANCE) & (i >= 1)
        near_hi = (window(i) != 0) & (d_hi >= 0) & (d_hi < MAX_DISTANCE)
        near_ref[:, 0:Q_BLOCK] = jnp.where(near_lo, 0.0, -jnp.inf).astype(near_ref.dtype)
        near_ref[:, Q_BLOCK:2 * Q_BLOCK] = jnp.where(near_hi, 0.0, -jnp.inf).astype(near_ref.dtype)


def _idx_select(proj, ki, wi, k_top):
    L = ki.shape[0]
    nq = L // Q_BLOCK
    n_sub_total = L // IDX_SUB
    qi_l, kj_l, last_l = [], [], []
    for i in range(nq):
        j_last = (i * Q_BLOCK + Q_BLOCK - 1) // IDX_KEY_BLOCK
        for j in range(j_last + 1):
            qi_l.append(i)
            kj_l.append(j)
            last_l.append(1 if j == j_last else 0)
    tabs = [jnp.asarray(np.asarray(t, np.int32)) for t in (qi_l, kj_l, last_l)]
    qcol = (DSA_HEADS * DSA_DH) // (IDX_HEADS * IDX_DIM)
    group_keys = WORD_BITS * LANES
    n_groups_max = -(-L // group_keys)
    key_chunks = n_groups_max * (group_keys // IDX_SUB)
    vmem = ((key_chunks * IDX_SUB + (WORD_BITS + 2) * n_groups_max * LANES) * Q_BLOCK * 4
            + 2 * (Q_BLOCK * L * 2) + 2 * (Q_BLOCK * IDX_HEADS * IDX_DIM * 2)
            + 2 * IDX_KEY_BLOCK * IDX_DIM * 2 + (8 << 20))
    grid_spec = pltpu.PrefetchScalarGridSpec(
        num_scalar_prefetch=3,
        grid=(len(qi_l),),
        in_specs=[pl.BlockSpec((Q_BLOCK, IDX_HEADS * IDX_DIM), lambda s, qt, kt, lt: (qt[s], qcol)),
                  pl.BlockSpec((IDX_KEY_BLOCK, IDX_DIM), lambda s, qt, kt, lt: (kt[s], 0)),
                  pl.BlockSpec((Q_BLOCK, LANES), lambda s, qt, kt, lt: (qt[s], 0))],
        out_specs=[pl.BlockSpec((1, n_sub_total, Q_BLOCK, IDX_SUB), lambda s, qt, kt, lt: (qt[s], 0, 0, 0)),
                   pl.BlockSpec((Q_BLOCK, 2 * Q_BLOCK), lambda s, qt, kt, lt: (qt[s], 0))],
        scratch_shapes=[pltpu.VMEM((key_chunks, Q_BLOCK, IDX_SUB), I32),
                        pltpu.VMEM((WORD_BITS, n_groups_max, Q_BLOCK, LANES), I32),
                        pltpu.VMEM((n_groups_max, Q_BLOCK, LANES), I32),
                        pltpu.VMEM((n_groups_max, Q_BLOCK, LANES), I32)],
    )
    return pl.pallas_call(
        functools.partial(_idx_kernel, k_top=k_top, n_sub_total=n_sub_total),
        grid_spec=grid_spec,
        out_shape=[jax.ShapeDtypeStruct((nq, n_sub_total, Q_BLOCK, IDX_SUB), BF16),
                   jax.ShapeDtypeStruct((L, 2 * Q_BLOCK), BF16)],
        compiler_params=_cparams(("arbitrary",), vmem),
        name="dsa_idx_select",
    )(*tabs, proj, ki, wi)


M_INIT = -1e30
FAR_SUBS = 2
LOG2E = math.log2(math.e)


def _attn_kernel(qi_tab, kj_tab, kind_tab, first_tab,
                 tab_ref, q_ref, kt_ref, vf_ref, klo_ref, khi_ref, vlo_ref, vhi_ref, far_ref, near_ref,
                 o_ref, m_scr, l_scr, acc_scr, b_scr, s_scr):
    s = pl.program_id(0)

    @pl.when(s == 0)
    def _():
        r_i = lax.broadcasted_iota(I32, (Q_BLOCK, 2 * Q_BLOCK), 0)
        c_i = lax.broadcasted_iota(I32, (Q_BLOCK, 2 * Q_BLOCK), 1)
        d = jnp.maximum(Q_BLOCK + r_i - c_i, 0)
        max_exact = N_BUCKETS // 2
        df = jnp.maximum(d, 1).astype(F32)
        large = max_exact + (jnp.log(df / max_exact) / math.log(MAX_DISTANCE / max_exact)
                             * (N_BUCKETS - max_exact)).astype(I32)
        large = jnp.minimum(large, N_BUCKETS - 1)
        bkt = jnp.where(d < max_exact, d, large)
        for h in range(DSA_HEADS):
            acc = jnp.zeros((Q_BLOCK, 2 * Q_BLOCK), F32)
            for b in range(N_BUCKETS):
                acc = jnp.where(bkt == b, (tab_ref[b, h] - tab_ref[N_BUCKETS - 1, h]) * LOG2E, acc)
            b_scr[h] = acc

    @pl.when(first_tab[s] == 1)
    def _():
        m_scr[...] = jnp.full_like(m_scr, M_INIT)
        l_scr[...] = jnp.zeros_like(l_scr)
        acc_scr[...] = jnp.zeros_like(acc_scr)

    def softmax_pv(v_ref, width, row0=0):
        nt = width // LANES
        rows = slice(row0, row0 + width)
        ones = jnp.ones((width, DSA_DH), BF16)
        for h in range(DSA_HEADS):
            g = h // DSA_GROUP
            tiles = [s_scr[h, :, t * LANES:(t + 1) * LANES] for t in range(nt)]
            tmax = tiles[0]
            for t in range(1, nt):
                tmax = jnp.maximum(tmax, tiles[t])
            m_prev = m_scr[h]
            m_new = jnp.maximum(m_prev, jnp.max(tmax, axis=-1, keepdims=True))
            alpha = jnp.exp2(m_prev - m_new)
            p = [jnp.exp2(tiles[t] - m_new).astype(BF16) for t in range(nt)]
            pb = jnp.concatenate(p, axis=-1) if nt > 1 else p[0]
            v_aug = jnp.concatenate([v_ref[rows, g * DSA_DH:(g + 1) * DSA_DH], ones], axis=-1)
            pv = jnp.dot(pb, v_aug, preferred_element_type=F32)
            acc_scr[h] = alpha * acc_scr[h] + pv[:, :DSA_DH]
            l_scr[h] = alpha * l_scr[h] + pv[:, DSA_DH:]
            m_scr[h] = m_new

    @pl.when(kind_tab[s] == 0)
    def _():
        eye = (lax.broadcasted_iota(I32, (Q_BLOCK, Q_BLOCK), 0)
               == lax.broadcasted_iota(I32, (Q_BLOCK, Q_BLOCK), 1)).astype(BF16)
        for sub in range(FAR_SUBS):
            cols = slice(sub * IDX_SUB, (sub + 1) * IDX_SUB)
            mask = far_ref[0, sub]
            for g in range(DSA_KV_HEADS):
                lhs = jnp.concatenate(
                    [jnp.concatenate([q_ref[:, h * DSA_DH:(h + 1) * DSA_DH], eye], axis=1)
                     for h in range(g * DSA_GROUP, (g + 1) * DSA_GROUP)], axis=0)
                rhs = jnp.concatenate([kt_ref[g * DSA_DH:(g + 1) * DSA_DH, cols], mask], axis=0)
                logits = jnp.dot(lhs, rhs, preferred_element_type=F32)
                for hh in range(DSA_GROUP):
                    s_scr[g * DSA_GROUP + hh] = logits[hh * Q_BLOCK:(hh + 1) * Q_BLOCK, :]
            softmax_pv(vf_ref, IDX_SUB, sub * IDX_SUB)

    @pl.when(kind_tab[s] == 1)
    def _():
        for half, (k_ref, v_ref) in enumerate(((klo_ref, vlo_ref), (khi_ref, vhi_ref))):
            cols = slice(half * Q_BLOCK, (half + 1) * Q_BLOCK)
            mask = near_ref[:, cols].astype(F32)
            for h in range(DSA_HEADS):
                g = h // DSA_GROUP
                logits = lax.dot_general(q_ref[:, h * DSA_DH:(h + 1) * DSA_DH],
                                         k_ref[:, g * DSA_DH:(g + 1) * DSA_DH], _NT, preferred_element_type=F32)
                s_scr[h, :, 0:Q_BLOCK] = logits + b_scr[h][:, cols] + mask
            softmax_pv(v_ref, Q_BLOCK)
        for h in range(DSA_HEADS):
            o_ref[:, h * DSA_DH:(h + 1) * DSA_DH] = (acc_scr[h] / l_scr[h]).astype(o_ref.dtype)


def _masked_attention(proj, far, near, rel_bias):
    L = proj.shape[0]
    nq = L // Q_BLOCK
    far_keys = FAR_SUBS * IDX_SUB
    per = far_keys // Q_BLOCK
    qi_l, kj_l, kind_l, first_l = [], [], [], []
    for i in range(nq):
        n_far = -(-i // per)
        for j in range(n_far):
            qi_l.append(i); kj_l.append(j); kind_l.append(0); first_l.append(1 if j == 0 else 0)
        qi_l.append(i); kj_l.append(max(n_far - 1, 0)); kind_l.append(1); first_l.append(1 if n_far == 0 else 0)
    tabs = [jnp.asarray(np.asarray(t, np.int32)) for t in (qi_l, kj_l, kind_l, first_l)]
    qw = DSA_HEADS * DSA_DH
    kvw = DSA_KV_HEADS * DSA_DH
    k_col = (2 * qw) // kvw
    v_col = k_col + 1
    hw = DSA_HEADS
    vmem = (2 * (Q_BLOCK * qw * 2 * 2 + 2 * far_keys * kvw * 2 + 4 * Q_BLOCK * kvw * 2
                 + Q_BLOCK * far_keys * 2 + Q_BLOCK * 2 * Q_BLOCK * 2)
            + hw * Q_BLOCK * (3 * LANES + 2 * Q_BLOCK) * 4 + (16 << 20))
    idx = lambda f: (lambda s, qt, kt, kd, ft: f(qt[s], kt[s]))
    k_t = proj[:, k_col * kvw:(k_col + 1) * kvw].T
    grid_spec = pltpu.PrefetchScalarGridSpec(
        num_scalar_prefetch=4,
        grid=(len(qi_l),),
        in_specs=[pl.BlockSpec(memory_space=pltpu.SMEM),
                  pl.BlockSpec((Q_BLOCK, qw), idx(lambda i, j: (i, 0))),
                  pl.BlockSpec((kvw, far_keys), idx(lambda i, j: (0, j))),
                  pl.BlockSpec((far_keys, kvw), idx(lambda i, j: (j, v_col))),
                  pl.BlockSpec((Q_BLOCK, kvw), idx(lambda i, j: (jnp.maximum(i - 1, 0), k_col))),
                  pl.BlockSpec((Q_BLOCK, kvw), idx(lambda i, j: (i, k_col))),
                  pl.BlockSpec((Q_BLOCK, kvw), idx(lambda i, j: (jnp.maximum(i - 1, 0), v_col))),
                  pl.BlockSpec((Q_BLOCK, kvw), idx(lambda i, j: (i, v_col))),
                  pl.BlockSpec((1, FAR_SUBS, Q_BLOCK, IDX_SUB), idx(lambda i, j: (i, j, 0, 0))),
                  pl.BlockSpec((Q_BLOCK, 2 * Q_BLOCK), idx(lambda i, j: (i, 0)))],
        out_specs=pl.BlockSpec((Q_BLOCK, qw), idx(lambda i, j: (i, 0))),
        scratch_shapes=[pltpu.VMEM((hw, Q_BLOCK, LANES), F32),
                        pltpu.VMEM((hw, Q_BLOCK, LANES), F32),
                        pltpu.VMEM((hw, Q_BLOCK, DSA_DH), F32),
                        pltpu.VMEM((hw, Q_BLOCK, 2 * Q_BLOCK), F32),
                        pltpu.VMEM((hw, Q_BLOCK, IDX_SUB), F32)],
    )
    return pl.pallas_call(
        _attn_kernel,
        grid_spec=grid_spec,
        out_shape=jax.ShapeDtypeStruct((L, qw), BF16),
        compiler_params=_cparams(("arbitrary",), vmem),
        name="dsa_attention",
    )(*tabs, rel_bias, proj, k_t, proj, proj, proj, proj, proj, far, near)


def _dsa_mixer(xb, w_in_all, layer, ln_g, ln_b, rel_bias):
    L = xb.shape[0]
    k_top = min(TOPK_MAX, L // 4)
    sq = DSA_HEADS * DSA_DH
    skv = DSA_KV_HEADS * DSA_DH
    si = IDX_HEADS * IDX_DIM
    w_ki = w_in_all[layer, :, sq + 2 * skv + si:sq + 2 * skv + si + IDX_DIM]
    w_wi = w_in_all[layer, :, sq + 2 * skv + si + IDX_DIM:]
    colscale = jnp.concatenate([jnp.full((1, sq), DSA_DH ** -0.5 * LOG2E, F32),
                                jnp.ones((1, si + 2 * skv), F32)], axis=1)
    w_small = jnp.concatenate([w_ki, w_wi, jnp.zeros((D_MODEL, LANES - IDX_HEADS), F32)], axis=1).astype(BF16)
    tn = 2 * skv
    nq, ni = sq // tn, si // tn
    src_block = lambda j: jnp.where(j < nq, j, jnp.where(j < nq + ni, j + 1, nq))
    proj = _matmul_scaled(xb, w_in_all, colscale, BF16, tm=1024, tn=tn,
                          w_block=lambda j: (layer, 0, src_block(j)))
    ki, wi = _dsa_small(xb, w_small, ln_g, ln_b, tm=1024)
    far, near = _idx_select(proj, ki, wi, k_top)
    return _masked_attention(proj, far, near, rel_bias)


def kernel(x, p, gdn_w_in, gdn_conv_w, gdn_a_log, gdn_dt_bias, gdn_norm_g, gdn_w_o, dsa_w_in, dsa_kidx_ln_g, dsa_kidx_ln_b, dsa_w_o, rel_bias, ln1_g, ln1_b, ffn_w_gate, ffn_w_up, ffn_conv_w, ffn_w_down, ln2_g, ln2_b, ple_w_proj, ple_w_gate):
    assert x.shape[0] == 1 and x.shape[2] == D_MODEL
    xf = x[0]
    xb = xf.astype(BF16)
    ple_gate_bf = ple_w_gate.astype(BF16)
    ia = ib = 0
    for i in range(DEPTH):
        if i % 2 == 0:
            mix = _gdn_mixer(xb, gdn_w_in, ia, gdn_conv_w[ia], gdn_a_log[ia], gdn_dt_bias[ia], gdn_norm_g[ia])
            w_o = gdn_w_o[ia]
            ia += 1
        else:
            mix = _dsa_mixer(xb, dsa_w_in, ib, dsa_kidx_ln_g[ib], dsa_kidx_ln_b[ib], rel_bias)
            w_o = dsa_w_o[ib]
            ib += 1
        xf, xb = _proj_res_ln(mix, w_o.astype(BF16), xf, ln1_g[i], ln1_b[i], tm=512, sub=256)
        hmid = _ffn_up(xb, ffn_w_gate, ffn_w_up, ffn_conv_w[i], i, tm=1024, tn=512)
        xf, xb = _proj_res_ln(hmid, ffn_w_down[i].astype(BF16), xf, ln2_g[i], ln2_b[i], tm=512, sub=256)
        xf, xb = _ple(xb, xf, ple_gate_bf, p, ple_w_proj, i, tm=1024, tn=1024)
    return xf[None]
```

```python
import functools
import math

import jax
import jax.numpy as jnp
import numpy as np
from jax import lax
from jax.experimental import pallas as pl
from jax.experimental.pallas import tpu as pltpu

F32 = jnp.float32
BF16 = jnp.bfloat16
I32 = jnp.int32

D_MODEL = 2048
GDN_QK_HEADS = 16
GDN_V_HEADS = 32
GDN_DK = 128
GDN_DV = 128
GDN_CONV = 4
GDN_CHUNK = 64
GDN_QK_W = GDN_QK_HEADS * GDN_DK
GDN_V_W = GDN_V_HEADS * GDN_DV
DSA_HEADS = 16
DSA_KV_HEADS = 4
DSA_GROUP = DSA_HEADS // DSA_KV_HEADS
DSA_DH = 128
IDX_HEADS = 16
IDX_DIM = 128
TOPK_MAX = 256
N_BUCKETS = 32
MAX_DISTANCE = 128
D_FF = 5120
FFN_CONV = 3
PLE_DIM = 256
DEPTH = 2
DN_ALPHA = (2.0 * DEPTH) ** 0.25
LN_EPS = 1e-5
RMS_EPS = 1e-6

V7X_VMEM_BYTES = 64 * 1024 * 1024
V7X_VMEM_BUDGET = 56 * 1024 * 1024
LANES = 128
BF16_SUBLANES = 16

TILE_M = 1024
TILE_N = 1024
FFN_TILE_N = 512
LN_TILE_M = 512
LN_SUB_M = 256

HALO = BF16_SUBLANES
Q_BLOCK = 128
IDX_KEY_BLOCK = 2048
IDX_SUB = 512
INT_MIN = -(2 ** 31)
BITS_PER_CHECK = 4
WORD_BITS = 32
FAR_MASKED = -2e30

_NT = (((1,), (1,)), ((), ()))
_TN = (((0,), (0,)), ((), ()))


def _cparams(semantics, vmem_bytes):
    return pltpu.CompilerParams(dimension_semantics=semantics,
                                vmem_limit_bytes=int(min(V7X_VMEM_BUDGET, vmem_bytes)))


def _silu(y):
    return y * jax.nn.sigmoid(y)


def _mm_scale_kernel(x_ref, w_ref, cs_ref, o_ref):
    acc = jnp.dot(x_ref[...], w_ref[...].astype(BF16), preferred_element_type=F32)
    o_ref[...] = (acc * cs_ref[...]).astype(o_ref.dtype)


def _weight_spec(w, K, tn, w_block):
    if w_block is None:
        return pl.BlockSpec((K, tn), lambda i, j: (0, j))
    return pl.BlockSpec((None, K, tn), lambda i, j: w_block(j))


def _matmul_scaled(x, w, colscale, out_dtype, tm, tn, w_block=None):
    M, K = x.shape
    N = colscale.shape[1]
    tm, tn = min(tm, M), min(tn, N)
    osz = jnp.dtype(out_dtype).itemsize
    wsz = jnp.dtype(w.dtype).itemsize
    vmem = 2 * (tm * K * 2 + K * tn * wsz + tm * tn * osz) + K * tn * 2 + 2 * tm * tn * 4
    return pl.pallas_call(
        _mm_scale_kernel,
        grid=(M // tm, N // tn),
        in_specs=[pl.BlockSpec((tm, K), lambda i, j: (i, 0)),
                  _weight_spec(w, K, tn, w_block),
                  pl.BlockSpec((1, tn), lambda i, j: (0, j))],
        out_specs=pl.BlockSpec((tm, tn), lambda i, j: (i, j)),
        out_shape=jax.ShapeDtypeStruct((M, N), out_dtype),
        compiler_params=_cparams(("parallel", "parallel"), vmem),
        name="matmul_scaled",
    )(x, w, colscale)


CONV_SUB = 256
FFN_SUB = 512


def _causal_conv(g, gh, cw_ref, g_scr, kc, tm, cols):
    g_scr[0:HALO, cols] = gh
    g_scr[HALO:HALO + tm, cols] = g
    y = cw_ref[kc - 1:kc, cols] * g
    for j in range(kc - 1):
        off = HALO - (kc - 1) + j
        y = y + cw_ref[j:j + 1, cols] * g_scr[off:off + tm, cols]
    return y


def _mm_conv_silu_kernel(x_ref, xh_ref, w_ref, cw_ref, o_ref, g_scr, *, kc, tm):
    first = pl.program_id(0) == 0
    for c0 in range(0, o_ref.shape[1], CONV_SUB):
        cols = slice(c0, c0 + CONV_SUB)
        w = w_ref[:, cols].astype(BF16)
        gh = jnp.dot(xh_ref[...], w, preferred_element_type=F32)
        gh = jnp.where(first, 0.0, gh)
        g = jnp.dot(x_ref[...], w, preferred_element_type=F32)
        y = _causal_conv(g, gh, cw_ref, g_scr, kc, tm, cols)
        o_ref[:, cols] = _silu(y).astype(o_ref.dtype)


def _proj_conv_silu(x, w, conv_w, tm, tn, w_block=None):
    M, K = x.shape
    kc, N = conv_w.shape
    tm, tn = min(tm, M), min(tn, N)
    hb = tm // HALO
    wsz = jnp.dtype(w.dtype).itemsize
    vmem = 2 * (tm * K * 2 + HALO * K * 2 + K * tn * wsz + tm * tn * 2) + K * tn * 2 + 4 * tm * tn * 4
    return pl.pallas_call(
        functools.partial(_mm_conv_silu_kernel, kc=kc, tm=tm),
        grid=(M // tm, N // tn),
        in_specs=[pl.BlockSpec((tm, K), lambda i, j: (i, 0)),
                  pl.BlockSpec((HALO, K), lambda i, j: (jnp.maximum(i * hb - 1, 0), 0)),
                  _weight_spec(w, K, tn, w_block),
                  pl.BlockSpec((kc, tn), lambda i, j: (0, j))],
        out_specs=pl.BlockSpec((tm, tn), lambda i, j: (i, j)),
        out_shape=jax.ShapeDtypeStruct((M, N), BF16),
        scratch_shapes=[pltpu.VMEM((tm + HALO, tn), F32)],
        compiler_params=_cparams(("parallel", "parallel"), vmem),
        name="proj_conv_silu",
    )(x, x, w, conv_w)


def _ffn_up_kernel(x_ref, xh_ref, wg_ref, wu_ref, cw_ref, o_ref, g_scr, *, kc, tm):
    first = pl.program_id(0) == 0
    for c0 in range(0, o_ref.shape[1], FFN_SUB):
        cols = slice(c0, c0 + FFN_SUB)
        wg = wg_ref[:, cols].astype(BF16)
        g = jnp.dot(x_ref[...], wg, preferred_element_type=F32)
        gh = jnp.dot(xh_ref[...], wg, preferred_element_type=F32)
        gh = jnp.where(first, 0.0, gh)
        u = jnp.dot(x_ref[...], wu_ref[:, cols].astype(BF16), preferred_element_type=F32)
        y = _causal_conv(g, gh, cw_ref, g_scr, kc, tm, cols)
        o_ref[:, cols] = (_silu(y) * u).astype(o_ref.dtype)


def _ffn_up(x, w_gate, w_up, conv_w, layer, tm, tn):
    M, K = x.shape
    N = w_gate.shape[2]
    kc = conv_w.shape[0]
    tm, tn = min(tm, M), min(tn, N)
    hb = tm // HALO
    wsz = jnp.dtype(w_gate.dtype).itemsize
    vmem = 2 * (tm * K * 2 + HALO * K * 2 + 2 * K * tn * wsz + tm * tn * 2) + 2 * K * tn * 2 + 6 * tm * tn * 4
    return pl.pallas_call(
        functools.partial(_ffn_up_kernel, kc=kc, tm=tm),
        grid=(M // tm, N // tn),
        in_specs=[pl.BlockSpec((tm, K), lambda i, j: (i, 0)),
                  pl.BlockSpec((HALO, K), lambda i, j: (jnp.maximum(i * hb - 1, 0), 0)),
                  pl.BlockSpec((None, K, tn), lambda i, j: (layer, 0, j)),
                  pl.BlockSpec((None, K, tn), lambda i, j: (layer, 0, j)),
                  pl.BlockSpec((kc, tn), lambda i, j: (0, j))],
        out_specs=pl.BlockSpec((tm, tn), lambda i, j: (i, j)),
        out_shape=jax.ShapeDtypeStruct((M, N), BF16),
        scratch_shapes=[pltpu.VMEM((tm + HALO, tn), F32)],
        compiler_params=_cparams(("parallel", "parallel"), vmem),
        name="ffn_up",
    )(x, x, w_gate, w_up, conv_w)


def _mm_res_ln_kernel(a_ref, w_ref, res_ref, g_ref, b_ref, of_ref, ob_ref, *, sub):
    for r0 in range(0, a_ref.shape[0], sub):
        rows = slice(r0, r0 + sub)
        acc = jnp.dot(a_ref[rows, :], w_ref[...], preferred_element_type=F32)
        y = DN_ALPHA * res_ref[rows, :] + acc
        mu = jnp.mean(y, axis=-1, keepdims=True)
        yc = y - mu
        var = jnp.mean(yc * yc, axis=-1, keepdims=True)
        out = yc * lax.rsqrt(var + LN_EPS) * g_ref[...] + b_ref[...]
        of_ref[rows, :] = out
        ob_ref[rows, :] = out.astype(BF16)


def _proj_res_ln(a, w, res, g, b, tm, sub):
    M, K = a.shape
    N = w.shape[1]
    tm = min(tm, M)
    sub = min(sub, tm)
    vmem = K * N * 2 + 2 * (tm * K * 2 + tm * N * 4 + tm * N * 4 + tm * N * 2) + 4 * sub * N * 4
    return pl.pallas_call(
        functools.partial(_mm_res_ln_kernel, sub=sub),
        grid=(M // tm,),
        in_specs=[pl.BlockSpec((tm, K), lambda i: (i, 0)),
                  pl.BlockSpec((K, N), lambda i: (0, 0), pipeline_mode=pl.Buffered(1)),
                  pl.BlockSpec((tm, N), lambda i: (i, 0)),
                  pl.BlockSpec((1, N), lambda i: (0, 0)),
                  pl.BlockSpec((1, N), lambda i: (0, 0))],
        out_specs=[pl.BlockSpec((tm, N), lambda i: (i, 0)),
                   pl.BlockSpec((tm, N), lambda i: (i, 0))],
        out_shape=[jax.ShapeDtypeStruct((M, N), F32), jax.ShapeDtypeStruct((M, N), BF16)],
        compiler_params=_cparams(("parallel",), vmem),
        name="proj_res_ln",
    )(a, w, res, g.reshape(1, N), b.reshape(1, N))


def _ple_kernel(xb_ref, wg_ref, p_ref, wp_ref, xr_ref, of_ref, ob_ref):
    gate = jax.nn.sigmoid(jnp.dot(xb_ref[...], wg_ref[...].astype(BF16), preferred_element_type=F32))
    pe = jnp.dot(p_ref[...].astype(BF16), wp_ref[...].astype(BF16), preferred_element_type=F32)
    out = xr_ref[...] + gate * pe
    of_ref[...] = out
    ob_ref[...] = out.astype(BF16)


def _ple(xb, xf, w_gate, p, w_proj, layer, tm, tn):
    M, K = xb.shape
    N = w_gate.shape[2]
    P = p.shape[3]
    tm, tn = min(tm, M), min(tn, N)
    wsz = jnp.dtype(w_gate.dtype).itemsize
    vmem = 2 * (tm * K * 2 + K * tn * wsz + tm * P * 4 + P * tn * 4 + tm * tn * 10) + K * tn * 2 + 4 * tm * tn * 4
    return pl.pallas_call(
        _ple_kernel,
        grid=(M // tm, N // tn),
        in_specs=[pl.BlockSpec((tm, K), lambda i, j: (i, 0)),
                  pl.BlockSpec((None, K, tn), lambda i, j: (layer, 0, j)),
                  pl.BlockSpec((None, None, tm, P), lambda i, j: (layer, 0, i, 0)),
                  pl.BlockSpec((None, P, tn), lambda i, j: (layer, 0, j)),
                  pl.BlockSpec((tm, tn), lambda i, j: (i, j))],
        out_specs=[pl.BlockSpec((tm, tn), lambda i, j: (i, j)),
                   pl.BlockSpec((tm, tn), lambda i, j: (i, j))],
        out_shape=[jax.ShapeDtypeStruct((M, N), F32), jax.ShapeDtypeStruct((M, N), BF16)],
        compiler_params=_cparams(("parallel", "parallel"), vmem),
        name="ple",
    )(xb, w_gate, p, w_proj, xf)


GDN_HB = 8
GDN_NC = 4
GDN_STAGE_UNITS = 32


def _gdn_kernel(q_ref, k_ref, v_ref, z_ref, sc_ref, hp_ref, ng_ref, o_ref, s_scr):
    C = GDN_CHUNK

    @pl.when(pl.program_id(1) == 0)
    def _():
        s_scr[...] = jnp.zeros_like(s_scr)

    row = lax.broadcasted_iota(I32, (C, C), 0)
    col = lax.broadcasted_iota(I32, (C, C), 1)
    tri = row >= col
    strict = row > col
    eye = row == col
    tri_f = tri.astype(F32)
    eye_f = eye.astype(F32)

    raw = sc_ref[...]
    a_log = hp_ref[0, 0:1, :]
    dt_b = hp_ref[0, 1:2, :]
    xs = raw + dt_b
    softplus = jnp.maximum(xs, 0.0) + jnp.log1p(jnp.exp(-jnp.abs(xs)))
    g_all = -jnp.exp(a_log) * softplus
    beta_all = jax.nn.sigmoid(raw)
    ng = ng_ref[...]

    units = [(c, j) for c in range(GDN_NC) for j in range(GDN_HB)]
    kb_l, rhs_l, decay_l, qd_l, kd_l, kbf_l, qbf_l, gl_l = [], [], [], [], [], [], [], []
    for c in range(GDN_NC):
        r0 = c * C
        gc = jnp.dot(tri_f, g_all[r0:r0 + C, :], precision=lax.Precision.HIGHEST,
                     preferred_element_type=F32)
        g_last = gc[C - 1:C, :]
        e_gc = jnp.exp(gc)
        e_rest = jnp.exp(g_last - gc)
        e_last = jnp.exp(g_last)
        beta_c = beta_all[r0:r0 + C, :]
        qn, kn = [], []
        for hq in range(GDN_HB // 2):
            qf = q_ref[r0:r0 + C, hq * GDN_DK:(hq + 1) * GDN_DK].astype(F32)
            kf = k_ref[r0:r0 + C, hq * GDN_DK:(hq + 1) * GDN_DK].astype(F32)
            qn.append(qf * lax.rsqrt(jnp.sum(qf * qf, axis=-1, keepdims=True) + RMS_EPS) * (GDN_DK ** -0.5))
            kn.append(kf * lax.rsqrt(jnp.sum(kf * kf, axis=-1, keepdims=True) + RMS_EPS))
        for j in range(GDN_HB):
            q_h, k_h = qn[j // 2], kn[j // 2]
            vf = v_ref[r0:r0 + C, j * GDN_DV:(j + 1) * GDN_DV].astype(F32)
            beta = beta_c[:, GDN_HB + j:GDN_HB + j + 1]
            kb = k_h * beta
            gcb = jnp.broadcast_to(gc[:, j:j + 1], (C, C))
            gcr = jnp.sum(jnp.where(eye, gcb, 0.0), axis=0, keepdims=True)
            decay_l.append(jnp.where(tri, jnp.exp(jnp.where(tri, gcb - gcr, 0.0)), 0.0))
            kb_l.append(kb.astype(BF16))
            rhs_l.append(jnp.concatenate([vf * beta, kb * e_gc[:, j:j + 1]], axis=-1).astype(BF16))
            qd_l.append((q_h * e_gc[:, j:j + 1]).astype(BF16))
            kd_l.append((k_h * e_rest[:, j:j + 1]).astype(BF16))
            kbf_l.append(k_h.astype(BF16))
            qbf_l.append(q_h.astype(BF16))
            gl_l.append(e_last[:, j:j + 1])

    n_u = len(units)
    qk_l, sol_l = [], []
    for b0 in range(0, n_u, GDN_STAGE_UNITS):
        us = range(b0, min(b0 + GDN_STAGE_UNITS, n_u))
        kk_b = [lax.dot_general(kb_l[u], kbf_l[u], _NT, preferred_element_type=F32) for u in us]
        qk_b = [lax.dot_general(qbf_l[u], kbf_l[u], _NT, preferred_element_type=F32) for u in us]
        qk_l += [jnp.where(tri, qk * decay_l[u], 0.0).astype(BF16) for qk, u in zip(qk_b, us)]
        x_b = [(-jnp.where(strict, kk * decay_l[u], 0.0)) for kk, u in zip(kk_b, us)]
        t_b = [eye_f + x for x in x_b]
        x_b = [x.astype(BF16) for x in x_b]
        for _ in range(5):
            x_b = [jnp.dot(x, x, preferred_element_type=F32).astype(BF16) for x in x_b]
            t_b = [t + jnp.dot(t.astype(BF16), x, preferred_element_type=F32) for t, x in zip(t_b, x_b)]
        sol_l += [jnp.dot(t.astype(BF16), rhs_l[u], preferred_element_type=F32) for t, u in zip(t_b, us)]

    s_cur = [s_scr[j] for j in range(GDN_HB)]
    for c in range(GDN_NC):
        r0 = c * C
        us = [c * GDN_HB + j for j in range(GDN_HB)]
        s_bf = [s.astype(BF16) for s in s_cur]
        ws_l = [jnp.dot(sol_l[u][:, GDN_DV:].astype(BF16), s_bf[j], preferred_element_type=F32)
                for j, u in enumerate(us)]
        qs_l = [jnp.dot(qd_l[u], s_bf[j], preferred_element_type=F32) for j, u in enumerate(us)]
        vn_l = [(sol_l[u][:, :GDN_DV] - ws_l[j]).astype(BF16) for j, u in enumerate(us)]
        kv_l = [lax.dot_general(kd_l[u], vn_l[j], _TN, preferred_element_type=F32) for j, u in enumerate(us)]
        ov_l = [jnp.dot(qk_l[u], vn_l[j], preferred_element_type=F32) for j, u in enumerate(us)]
        s_cur = [s_cur[j] * gl_l[u] + kv_l[j] for j, u in enumerate(us)]
        for j in range(GDN_HB):
            o = qs_l[j] + ov_l[j]
            zf = z_ref[r0:r0 + C, j * GDN_DV:(j + 1) * GDN_DV].astype(F32)
            o = o * lax.rsqrt(jnp.mean(o * o, axis=-1, keepdims=True) + RMS_EPS) * ng * _silu(zf)
            o_ref[r0:r0 + C, j * GDN_DV:(j + 1) * GDN_DV] = o.astype(o_ref.dtype)
    for j in range(GDN_HB):
        s_scr[j] = s_cur[j]


def _gdn_core(qkv, z, scal, hparams, norm_g):
    L = qkv.shape[0]
    G = GDN_V_HEADS // GDN_HB
    R = GDN_NC * GDN_CHUNK
    qw = (GDN_HB // 2) * GDN_DK
    vw = GDN_HB * GDN_DV
    k_blk0 = GDN_QK_W // qw
    v_blk0 = 2 * GDN_QK_W // vw
    vmem = 2 * (2 * R * qw * 2 + 2 * R * vw * 2 + R * LANES * 4 + R * vw * 2) + (16 << 20)
    return pl.pallas_call(
        _gdn_kernel,
        grid=(G, L // R),
        in_specs=[pl.BlockSpec((R, qw), lambda g, s: (s, g)),
                  pl.BlockSpec((R, qw), lambda g, s: (s, k_blk0 + g)),
                  pl.BlockSpec((R, vw), lambda g, s: (s, v_blk0 + g)),
                  pl.BlockSpec((R, vw), lambda g, s: (s, g)),
                  pl.BlockSpec((R, LANES), lambda g, s: (s, g)),
                  pl.BlockSpec((1, 8, LANES), lambda g, s: (g, 0, 0)),
                  pl.BlockSpec((1, GDN_DV), lambda g, s: (0, 0))],
        out_specs=pl.BlockSpec((R, vw), lambda g, s: (s, g)),
        out_shape=jax.ShapeDtypeStruct((L, GDN_V_W), BF16),
        scratch_shapes=[pltpu.VMEM((GDN_HB, GDN_DK, GDN_DV), F32)],
        compiler_params=_cparams(("parallel", "arbitrary"), vmem),
        name="gdn_core",
    )(qkv, qkv, qkv, z, scal, hparams, norm_g.reshape(1, GDN_DV))


def _gdn_mixer(xb, w_in_all, layer, conv_w, a_log, dt_bias, norm_g):
    nqkv = 2 * GDN_QK_W + GDN_V_W
    tn = TILE_N
    w_ab = w_in_all[layer, :, nqkv + GDN_V_W:]
    G = GDN_V_HEADS // GDN_HB
    w_a = w_ab[:, :GDN_V_HEADS].reshape(D_MODEL, G, GDN_HB)
    w_b = w_ab[:, GDN_V_HEADS:].reshape(D_MODEL, G, GDN_HB)
    w_sc = jnp.concatenate([w_a, w_b, jnp.zeros((D_MODEL, G, LANES - 2 * GDN_HB), F32)], axis=-1)
    w_sc = w_sc.reshape(D_MODEL, G * LANES).astype(BF16)
    qkv = _proj_conv_silu(xb, w_in_all, conv_w, tm=TILE_M, tn=tn, w_block=lambda j: (layer, 0, j))
    ones_z = jnp.ones((1, GDN_V_W), F32)
    z = _matmul_scaled(xb, w_in_all, ones_z, BF16, tm=TILE_M, tn=tn,
                       w_block=lambda j: (layer, 0, nqkv // tn + j))
    scal = _matmul_scaled(xb, w_sc, jnp.ones((1, G * LANES), F32), F32, tm=TILE_M, tn=G * LANES)
    hp = jnp.zeros((G, 8, LANES), F32)
    hp = hp.at[:, 0, :GDN_HB].set(a_log.reshape(G, GDN_HB))
    hp = hp.at[:, 1, :GDN_HB].set(dt_bias.reshape(G, GDN_HB))
    return _gdn_core(qkv, z, scal, hp, norm_g)


def _dsa_small_kernel(x_ref, w_ref, g_ref, b_ref, ki_ref, wi_ref):
    acc = jnp.dot(x_ref[...], w_ref[...], preferred_element_type=F32)
    ki = acc[:, :IDX_DIM]
    mu = jnp.mean(ki, axis=-1, keepdims=True)
    kc = ki - mu
    var = jnp.mean(kc * kc, axis=-1, keepdims=True)
    ki_ref[...] = (kc * lax.rsqrt(var + LN_EPS) * g_ref[...] + b_ref[...]).astype(ki_ref.dtype)
    wi_ref[...] = acc[:, IDX_DIM:] * ((IDX_HEADS ** -0.5) * (IDX_DIM ** -0.5))


def _dsa_small(xb, w_small, ln_g, ln_b, tm):
    M, K = xb.shape
    tm = min(tm, M)
    N = 2 * LANES
    vmem = 2 * (tm * K * 2 + K * N * 2 + tm * LANES * 6) + 4 * tm * N * 4
    return pl.pallas_call(
        _dsa_small_kernel,
        grid=(M // tm,),
        in_specs=[pl.BlockSpec((tm, K), lambda i: (i, 0)),
                  pl.BlockSpec((K, N), lambda i: (0, 0)),
                  pl.BlockSpec((1, IDX_DIM), lambda i: (0, 0)),
                  pl.BlockSpec((1, IDX_DIM), lambda i: (0, 0))],
        out_specs=[pl.BlockSpec((tm, IDX_DIM), lambda i: (i, 0)),
                   pl.BlockSpec((tm, LANES), lambda i: (i, 0))],
        out_shape=[jax.ShapeDtypeStruct((M, IDX_DIM), BF16), jax.ShapeDtypeStruct((M, LANES), F32)],
        compiler_params=_cparams(("parallel",), vmem),
        name="dsa_idx_proj",
    )(xb, w_small, ln_g.reshape(1, IDX_DIM), ln_b.reshape(1, IDX_DIM))


def _sortable_key(score):
    bits = lax.bitcast_convert_type(score, I32)
    return jnp.where(bits >= 0, bits, bits ^ jnp.int32(0x7FFFFFFF))


def _idx_kernel(qi_tab, kj_tab, last_tab,
                qidx_ref, kidx_ref, wi_ref, far_ref, near_ref, key_scr, plane_scr, cand_scr, w_scr,
                *, k_top, n_sub_total):
    s = pl.program_id(0)
    i = qi_tab[s]
    j = kj_tab[s]

    @pl.when(s == 0)
    def _():
        key_scr[...] = jnp.full_like(key_scr, INT_MIN)
        plane_scr[...] = jnp.zeros_like(plane_scr)
    nsub = IDX_KEY_BLOCK // IDX_SUB
    t_col = i * Q_BLOCK + lax.broadcasted_iota(I32, (Q_BLOCK, 1), 0)
    lane_sub = lax.broadcasted_iota(I32, (Q_BLOCK, IDX_SUB), 1)
    wi = wi_ref[...]

    qi_rows = jnp.concatenate([qidx_ref[:, h * IDX_DIM:(h + 1) * IDX_DIM] for h in range(IDX_HEADS)], axis=0)
    for sub in range(nsub):
        ki_sub = kidx_ref[sub * IDX_SUB:(sub + 1) * IDX_SUB, :]
        sc_all = lax.dot_general(qi_rows, ki_sub, _NT, preferred_element_type=F32)
        acc = jnp.zeros((Q_BLOCK, IDX_SUB), F32)
        for h in range(IDX_HEADS):
            acc = acc + jnp.maximum(sc_all[h * Q_BLOCK:(h + 1) * Q_BLOCK, :], 0.0) * wi[:, h:h + 1]
        s_idx = j * IDX_KEY_BLOCK + sub * IDX_SUB + lane_sub
        key_scr[j * nsub + sub] = jnp.where(s_idx <= t_col, _sortable_key(acc), INT_MIN)

    @pl.when(last_tab[s] == 1)
    def _():
        n_chunks = (i * Q_BLOCK + Q_BLOCK - 1) // IDX_SUB + 1

        def count(pred, ref_val):
            refb = jnp.broadcast_to(ref_val, (Q_BLOCK, LANES))

            def body(c, cnt):
                blk = key_scr[c]
                for l in range(IDX_SUB // LANES):
                    cnt = cnt + jnp.where(pred(blk[:, l * LANES:(l + 1) * LANES], refb), 1, 0)
                return cnt

            cnt = lax.fori_loop(0, n_chunks, body, jnp.zeros((Q_BLOCK, LANES), I32))
            return jnp.sum(cnt, axis=1, keepdims=True)

        tiles_per_chunk = IDX_SUB // LANES
        chunks_per_group = WORD_BITS // tiles_per_chunk
        n_tiles = n_chunks * tiles_per_chunk
        n_groups = (n_tiles + WORD_BITS - 1) // WORD_BITS

        def build_group(g, carry):
            def build_rows(r, carry_r):
                r8 = pl.multiple_of(r * 8, 8)
                a = [key_scr[g * chunks_per_group + t // tiles_per_chunk, pl.ds(r8, 8),
                             (t % tiles_per_chunk) * LANES:(t % tiles_per_chunk + 1) * LANES]
                     for t in range(WORD_BITS)]
                m, sh = 0x0000FFFF, 16
                while sh:
                    k = 0
                    while k < WORD_BITS:
                        x = (a[k] ^ lax.shift_right_logical(a[k + sh], jnp.int32(sh))) & jnp.int32(m)
                        a[k] = a[k] ^ x
                        a[k + sh] = a[k + sh] ^ jnp.left_shift(x, jnp.int32(sh))
                        k = (k + sh + 1) & ~sh
                    sh >>= 1
                    m = (m ^ (m << sh)) & 0xFFFFFFFF if sh else m
                a[0] = ~a[0]
                for p in range(WORD_BITS):
                    plane_scr[p, g, pl.ds(r8, 8), :] = a[p]
                return carry_r

            lax.fori_loop(0, Q_BLOCK // 8, build_rows, 0)
            return carry

        lax.fori_loop(0, n_groups, build_group, 0)
        n_groups_max = plane_scr.shape[1]
        for g in range(n_groups_max):
            n_valid = jnp.clip(n_tiles - g * WORD_BITS, 0, WORD_BITS)
            word = jnp.where(n_valid >= WORD_BITS, jnp.int32(-1),
                             jnp.where(n_valid <= 0, jnp.int32(0),
                                       jnp.left_shift(jnp.int32(-1), WORD_BITS - n_valid)))
            cand_scr[g] = jnp.broadcast_to(word, (Q_BLOCK, LANES))

        def plane_pass(p, thr_u, n_above, n_cand):
            cnt = jnp.zeros((Q_BLOCK, LANES), I32)
            for g in range(n_groups_max):
                w = plane_scr[p, g] & cand_scr[g]
                w_scr[g] = w
                cnt = cnt + lax.population_count(w)
            n_set = jnp.sum(cnt, axis=1, keepdims=True)
            take = n_above + n_set >= k_top
            take_b = jnp.broadcast_to(take, (Q_BLOCK, LANES))
            for g in range(n_groups_max):
                w = w_scr[g]
                cand_scr[g] = jnp.where(take_b, w, cand_scr[g] ^ w)
            bit = jnp.left_shift(jnp.int32(1), WORD_BITS - 1 - p)
            return (jnp.where(take, thr_u | bit, thr_u), jnp.where(take, n_above, n_above + n_set),
                    jnp.where(take, n_set, n_cand - n_set))

        def group_body(carry):
            grp, thr_u, n_above, n_cand, _ = carry
            for bb in range(BITS_PER_CHECK):
                thr_u, n_above, n_cand = plane_pass(grp * BITS_PER_CHECK + bb, thr_u, n_above, n_cand)
            return grp + 1, thr_u, n_above, n_cand, jnp.max(jnp.where(n_above + n_cand != k_top, 1, 0))

        _, thr_u, n_above, n_cand, n_tied_rows = lax.while_loop(
            lambda carry: (carry[0] < WORD_BITS // BITS_PER_CHECK) & (carry[4] != 0), group_body,
            (jnp.int32(0), jnp.zeros((Q_BLOCK, 1), I32), jnp.zeros((Q_BLOCK, 1), I32),
             jnp.broadcast_to(n_tiles * LANES, (Q_BLOCK, 1)).astype(I32), jnp.int32(1)))
        thr = thr_u ^ jnp.int32(INT_MIN)
        thr_b = jnp.broadcast_to(thr, (Q_BLOCK, IDX_SUB))

        def emit(c, sel):
            s_idx = c * IDX_SUB + lane_sub
            far = sel & (t_col - s_idx >= MAX_DISTANCE)
            far_ref[0, c] = jnp.where(far, 0.0, FAR_MASKED).astype(far_ref.dtype)
            key_scr[c] = jnp.where(sel, 1, 0)

        @pl.when(n_tied_rows == 0)
        def _():
            def sel_body(c, carry):
                emit(c, key_scr[c] >= thr_b)
                return carry

            lax.fori_loop(0, n_chunks, sel_body, 0)

        @pl.when(n_tied_rows != 0)
        def _():
            n_gt = count(lambda a, r: a > r, thr)
            need_eq = (k_top - n_gt).astype(F32)
            incl = (lax.broadcasted_iota(I32, (IDX_SUB, IDX_SUB), 0)
                    <= lax.broadcasted_iota(I32, (IDX_SUB, IDX_SUB), 1)).astype(BF16)

            def sel_body(c, carry):
                blk = key_scr[c]
                eq = blk == thr_b
                eq_f = jnp.where(eq, 1.0, 0.0)
                rank = carry + jnp.dot(eq_f.astype(BF16), incl, preferred_element_type=F32)
                s_idx = c * IDX_SUB + lane_sub
                emit(c, ((blk > thr_b) | (eq & (rank <= need_eq))) & (s_idx <= t_col))
                return carry + jnp.sum(eq_f, axis=1, keepdims=True)

            lax.fori_loop(0, n_chunks, sel_body, jnp.zeros((Q_BLOCK, 1), F32))

        def fill_body(c, carry):
            far_ref[0, c] = jnp.full((Q_BLOCK, IDX_SUB), FAR_MASKED, far_ref.dtype)
            return carry

        lax.fori_loop(n_chunks, n_sub_total, fill_body, 0)

        def window(blk_idx):
            per = IDX_SUB // Q_BLOCK
            chunk = key_scr[blk_idx // per]
            m = blk_idx % per
            out = chunk[:, 0:Q_BLOCK]
            for q in range(1, per):
                out = jnp.where(m == q, chunk[:, q * Q_BLOCK:(q + 1) * Q_BLOCK], out)
            return out

        r_i = lax.broadcasted_iota(I32, (Q_BLOCK, Q_BLOCK), 0)
        c_i = lax.broadcasted_iota(I32, (Q_BLOCK, Q_BLOCK), 1)
        d_lo = Q_BLOCK + r_i - c_i
        d_hi = r_i - c_i
        near_lo = (window(jnp.maximum(i - 1, 0)) != 0) & (d_lo < MAX_DISTANCE) & (i >= 1)
        near_hi = (window(i) != 0) & (d_hi >= 0) & (d_hi < MAX_DISTANCE)
        near_ref[:, 0:Q_BLOCK] = jnp.where(near_lo, 0.0, -jnp.inf).astype(near_ref.dtype)
        near_ref[:, Q_BLOCK:2 * Q_BLOCK] = jnp.where(near_hi, 0.0, -jnp.inf).astype(near_ref.dtype)


def _idx_select(proj, ki, wi, k_top):
    L = ki.shape[0]
    nq = L // Q_BLOCK
    n_sub_total = L // IDX_SUB
    qi_l, kj_l, last_l = [], [], []
    for i in range(nq):
        j_last = (i * Q_BLOCK + Q_BLOCK - 1) // IDX_KEY_BLOCK
        for j in range(j_last + 1):
            qi_l.append(i)
            kj_l.append(j)
            last_l.append(1 if j == j_last else 0)
    tabs = [jnp.asarray(np.asarray(t, np.int32)) for t in (qi_l, kj_l, last_l)]
    qcol = (DSA_HEADS * DSA_DH) // (IDX_HEADS * IDX_DIM)
    group_keys = WORD_BITS * LANES
    n_groups_max = -(-L // group_keys)
    key_chunks = n_groups_max * (group_keys // IDX_SUB)
    vmem = ((key_chunks * IDX_SUB + (WORD_BITS + 2) * n_groups_max * LANES) * Q_BLOCK * 4
            + 2 * (Q_BLOCK * L * 2) + 2 * (Q_BLOCK * IDX_HEADS * IDX_DIM * 2)
            + 2 * IDX_KEY_BLOCK * IDX_DIM * 2 + (8 << 20))
    grid_spec = pltpu.PrefetchScalarGridSpec(
        num_scalar_prefetch=3,
        grid=(len(qi_l),),
        in_specs=[pl.BlockSpec((Q_BLOCK, IDX_HEADS * IDX_DIM), lambda s, qt, kt, lt: (qt[s], qcol)),
                  pl.BlockSpec((IDX_KEY_BLOCK, IDX_DIM), lambda s, qt, kt, lt: (kt[s], 0)),
                  pl.BlockSpec((Q_BLOCK, LANES), lambda s, qt, kt, lt: (qt[s], 0))],
        out_specs=[pl.BlockSpec((1, n_sub_total, Q_BLOCK, IDX_SUB), lambda s, qt, kt, lt: (qt[s], 0, 0, 0)),
                   pl.BlockSpec((Q_BLOCK, 2 * Q_BLOCK), lambda s, qt, kt, lt: (qt[s], 0))],
        scratch_shapes=[pltpu.VMEM((key_chunks, Q_BLOCK, IDX_SUB), I32),
                        pltpu.VMEM((WORD_BITS, n_groups_max, Q_BLOCK, LANES), I32),
                        pltpu.VMEM((n_groups_max, Q_BLOCK, LANES), I32),
                        pltpu.VMEM((n_groups_max, Q_BLOCK, LANES), I32)],
    )
    return pl.pallas_call(
        functools.partial(_idx_kernel, k_top=k_top, n_sub_total=n_sub_total),
        grid_spec=grid_spec,
        out_shape=[jax.ShapeDtypeStruct((nq, n_sub_total, Q_BLOCK, IDX_SUB), BF16),
                   jax.ShapeDtypeStruct((L, 2 * Q_BLOCK), BF16)],
        compiler_params=_cparams(("arbitrary",), vmem),
        name="dsa_idx_select",
    )(*tabs, proj, ki, wi)


M_INIT = -1e30
FAR_SUBS = 2
LOG2E = math.log2(math.e)


def _attn_kernel(qi_tab, kj_tab, kind_tab, first_tab,
                 tab_ref, q_ref, kt_ref, vf_ref, klo_ref, khi_ref, vlo_ref, vhi_ref, far_ref, near_ref,
                 o_ref, m_scr, l_scr, acc_scr, b_scr, s_scr):
    s = pl.program_id(0)

    @pl.when(s == 0)
    def _():
        r_i = lax.broadcasted_iota(I32, (Q_BLOCK, 2 * Q_BLOCK), 0)
        c_i = lax.broadcasted_iota(I32, (Q_BLOCK, 2 * Q_BLOCK), 1)
        d = jnp.maximum(Q_BLOCK + r_i - c_i, 0)
        max_exact = N_BUCKETS // 2
        df = jnp.maximum(d, 1).astype(F32)
        large = max_exact + (jnp.log(df / max_exact) / math.log(MAX_DISTANCE / max_exact)
                             * (N_BUCKETS - max_exact)).astype(I32)
        large = jnp.minimum(large, N_BUCKETS - 1)
        bkt = jnp.where(d < max_exact, d, large)
        for h in range(DSA_HEADS):
            acc = jnp.zeros((Q_BLOCK, 2 * Q_BLOCK), F32)
            for b in range(N_BUCKETS):
                acc = jnp.where(bkt == b, (tab_ref[b, h] - tab_ref[N_BUCKETS - 1, h]) * LOG2E, acc)
            b_scr[h] = acc

    @pl.when(first_tab[s] == 1)
    def _():
        m_scr[...] = jnp.full_like(m_scr, M_INIT)
        l_scr[...] = jnp.zeros_like(l_scr)
        acc_scr[...] = jnp.zeros_like(acc_scr)

    def softmax_pv(v_ref, width, row0=0):
        nt = width // LANES
        rows = slice(row0, row0 + width)
        ones = jnp.ones((width, DSA_DH), BF16)
        for h in range(DSA_HEADS):
            g = h // DSA_GROUP
            tiles = [s_scr[h, :, t * LANES:(t + 1) * LANES] for t in range(nt)]
            tmax = tiles[0]
            for t in range(1, nt):
                tmax = jnp.maximum(tmax, tiles[t])
            m_prev = m_scr[h]
            m_new = jnp.maximum(m_prev, jnp.max(tmax, axis=-1, keepdims=True))
            alpha = jnp.exp2(m_prev - m_new)
            p = [jnp.exp2(tiles[t] - m_new).astype(BF16) for t in range(nt)]
            pb = jnp.concatenate(p, axis=-1) if nt > 1 else p[0]
            v_aug = jnp.concatenate([v_ref[rows, g * DSA_DH:(g + 1) * DSA_DH], ones], axis=-1)
            pv = jnp.dot(pb, v_aug, preferred_element_type=F32)
            acc_scr[h] = alpha * acc_scr[h] + pv[:, :DSA_DH]
            l_scr[h] = alpha * l_scr[h] + pv[:, DSA_DH:]
            m_scr[h] = m_new

    @pl.when(kind_tab[s] == 0)
    def _():
        eye = (lax.broadcasted_iota(I32, (Q_BLOCK, Q_BLOCK), 0)
               == lax.broadcasted_iota(I32, (Q_BLOCK, Q_BLOCK), 1)).astype(BF16)
        for sub in range(FAR_SUBS):
            cols = slice(sub * IDX_SUB, (sub + 1) * IDX_SUB)
            mask = far_ref[0, sub]
            for g in range(DSA_KV_HEADS):
                lhs = jnp.concatenate(
                    [jnp.concatenate([q_ref[:, h * DSA_DH:(h + 1) * DSA_DH], eye], axis=1)
                     for h in range(g * DSA_GROUP, (g + 1) * DSA_GROUP)], axis=0)
                rhs = jnp.concatenate([kt_ref[g * DSA_DH:(g + 1) * DSA_DH, cols], mask], axis=0)
                logits = jnp.dot(lhs, rhs, preferred_element_type=F32)
                for hh in range(DSA_GROUP):
                    s_scr[g * DSA_GROUP + hh] = logits[hh * Q_BLOCK:(hh + 1) * Q_BLOCK, :]
            softmax_pv(vf_ref, IDX_SUB, sub * IDX_SUB)

    @pl.when(kind_tab[s] == 1)
    def _():
        for half, (k_ref, v_ref) in enumerate(((klo_ref, vlo_ref), (khi_ref, vhi_ref))):
            cols = slice(half * Q_BLOCK, (half + 1) * Q_BLOCK)
            mask = near_ref[:, cols].astype(F32)
            for h in range(DSA_HEADS):
                g = h // DSA_GROUP
                logits = lax.dot_general(q_ref[:, h * DSA_DH:(h + 1) * DSA_DH],
                                         k_ref[:, g * DSA_DH:(g + 1) * DSA_DH], _NT, preferred_element_type=F32)
                s_scr[h, :, 0:Q_BLOCK] = logits + b_scr[h][:, cols] + mask
            softmax_pv(v_ref, Q_BLOCK)
        for h in range(DSA_HEADS):
            o_ref[:, h * DSA_DH:(h + 1) * DSA_DH] = (acc_scr[h] / l_scr[h]).astype(o_ref.dtype)


def _masked_attention(proj, far, near, rel_bias):
    L = proj.shape[0]
    nq = L // Q_BLOCK
    far_keys = FAR_SUBS * IDX_SUB
    per = far_keys // Q_BLOCK
    qi_l, kj_l, kind_l, first_l = [], [], [], []
    for i in range(nq):
        n_far = -(-i // per)
        for j in range(n_far):
            qi_l.append(i); kj_l.append(j); kind_l.append(0); first_l.append(1 if j == 0 else 0)
        qi_l.append(i); kj_l.append(max(n_far - 1, 0)); kind_l.append(1); first_l.append(1 if n_far == 0 else 0)
    tabs = [jnp.asarray(np.asarray(t, np.int32)) for t in (qi_l, kj_l, kind_l, first_l)]
    qw = DSA_HEADS * DSA_DH
    kvw = DSA_KV_HEADS * DSA_DH
    k_col = (2 * qw) // kvw
    v_col = k_col + 1
    hw = DSA_HEADS
    vmem = (2 * (Q_BLOCK * qw * 2 * 2 + 2 * far_keys * kvw * 2 + 4 * Q_BLOCK * kvw * 2
                 + Q_BLOCK * far_keys * 2 + Q_BLOCK * 2 * Q_BLOCK * 2)
            + hw * Q_BLOCK * (3 * LANES + 2 * Q_BLOCK) * 4 + (16 << 20))
    idx = lambda f: (lambda s, qt, kt, kd, ft: f(qt[s], kt[s]))
    k_t = proj[:, k_col * kvw:(k_col + 1) * kvw].T
    grid_spec = pltpu.PrefetchScalarGridSpec(
        num_scalar_prefetch=4,
        grid=(len(qi_l),),
        in_specs=[pl.BlockSpec(memory_space=pltpu.SMEM),
                  pl.BlockSpec((Q_BLOCK, qw), idx(lambda i, j: (i, 0))),
                  pl.BlockSpec((kvw, far_keys), idx(lambda i, j: (0, j))),
                  pl.BlockSpec((far_keys, kvw), idx(lambda i, j: (j, v_col))),
                  pl.BlockSpec((Q_BLOCK, kvw), idx(lambda i, j: (jnp.maximum(i - 1, 0), k_col))),
                  pl.BlockSpec((Q_BLOCK, kvw), idx(lambda i, j: (i, k_col))),
                  pl.BlockSpec((Q_BLOCK, kvw), idx(lambda i, j: (jnp.maximum(i - 1, 0), v_col))),
                  pl.BlockSpec((Q_BLOCK, kvw), idx(lambda i, j: (i, v_col))),
                  pl.BlockSpec((1, FAR_SUBS, Q_BLOCK, IDX_SUB), idx(lambda i, j: (i, j, 0, 0))),
                  pl.BlockSpec((Q_BLOCK, 2 * Q_BLOCK), idx(lambda i, j: (i, 0)))],
        out_specs=pl.BlockSpec((Q_BLOCK, qw), idx(lambda i, j: (i, 0))),
        scratch_shapes=[pltpu.VMEM((hw, Q_BLOCK, LANES), F32),
                        pltpu.VMEM((hw, Q_BLOCK, LANES), F32),
                        pltpu.VMEM((hw, Q_BLOCK, DSA_DH), F32),
                        pltpu.VMEM((hw, Q_BLOCK, 2 * Q_BLOCK), F32),
                        pltpu.VMEM((hw, Q_BLOCK, IDX_SUB), F32)],
    )
    return pl.pallas_call(
        _attn_kernel,
        grid_spec=grid_spec,
        out_shape=jax.ShapeDtypeStruct((L, qw), BF16),
        compiler_params=_cparams(("arbitrary",), vmem),
        name="dsa_attention",
    )(*tabs, rel_bias, proj, k_t, proj, proj, proj, proj, proj, far, near)


def _dsa_mixer(xb, w_in_all, layer, ln_g, ln_b, rel_bias):
    L = xb.shape[0]
    k_top = min(TOPK_MAX, L // 4)
    sq = DSA_HEADS * DSA_DH
    skv = DSA_KV_HEADS * DSA_DH
    si = IDX_HEADS * IDX_DIM
    w_ki = w_in_all[layer, :, sq + 2 * skv + si:sq + 2 * skv + si + IDX_DIM]
    w_wi = w_in_all[layer, :, sq + 2 * skv + si + IDX_DIM:]
    colscale = jnp.concatenate([jnp.full((1, sq), DSA_DH ** -0.5 * LOG2E, F32),
                                jnp.ones((1, si + 2 * skv), F32)], axis=1)
    w_small = jnp.concatenate([w_ki, w_wi, jnp.zeros((D_MODEL, LANES - IDX_HEADS), F32)], axis=1).astype(BF16)
    tn = 2 * skv
    nq, ni = sq // tn, si // tn
    src_block = lambda j: jnp.where(j < nq, j, jnp.where(j < nq + ni, j + 1, nq))
    proj = _matmul_scaled(xb, w_in_all, colscale, BF16, tm=TILE_M, tn=tn,
                          w_block=lambda j: (layer, 0, src_block(j)))
    ki, wi = _dsa_small(xb, w_small, ln_g, ln_b, tm=TILE_M)
    far, near = _idx_select(proj, ki, wi, k_top)
    return _masked_attention(proj, far, near, rel_bias)


def kernel(x, p, gdn_w_in, gdn_conv_w, gdn_a_log, gdn_dt_bias, gdn_norm_g, gdn_w_o, dsa_w_in, dsa_kidx_ln_g, dsa_kidx_ln_b, dsa_w_o, rel_bias, ln1_g, ln1_b, ffn_w_gate, ffn_w_up, ffn_conv_w, ffn_w_down, ln2_g, ln2_b, ple_w_proj, ple_w_gate):
    assert x.shape[0] == 1 and x.shape[2] == D_MODEL
    xf = x[0]
    xb = xf.astype(BF16)
    ple_gate_bf = ple_w_gate.astype(BF16)
    ia = ib = 0
    for i in range(DEPTH):
        if i % 2 == 0:
            mix = _gdn_mixer(xb, gdn_w_in, ia, gdn_conv_w[ia], gdn_a_log[ia], gdn_dt_bias[ia], gdn_norm_g[ia])
            w_o = gdn_w_o[ia]
            ia += 1
        else:
            mix = _dsa_mixer(xb, dsa_w_in, ib, dsa_kidx_ln_g[ib], dsa_kidx_ln_b[ib], rel_bias)
            w_o = dsa_w_o[ib]
            ib += 1
        xf, xb = _proj_res_ln(mix, w_o.astype(BF16), xf, ln1_g[i], ln1_b[i], tm=LN_TILE_M, sub=LN_SUB_M)
        hmid = _ffn_up(xb, ffn_w_gate, ffn_w_up, ffn_conv_w[i], i, tm=TILE_M, tn=FFN_TILE_N)
        xf, xb = _proj_res_ln(hmid, ffn_w_down[i].astype(BF16), xf, ln2_g[i], ln2_b[i], tm=LN_TILE_M, sub=LN_SUB_M)
        xf, xb = _ple(xb, xf, ple_gate_bf, p, ple_w_proj, i, tm=TILE_M, tn=TILE_N)
    return xf[None]
```

```python
import functools
import math

import jax
import jax.numpy as jnp
import numpy as np
from jax import lax
from jax.experimental import pallas as pl
from jax.experimental.pallas import tpu as pltpu

F32 = jnp.float32
BF16 = jnp.bfloat16
I32 = jnp.int32

D_MODEL = 2048
GDN_QK_HEADS = 16
GDN_V_HEADS = 32
GDN_DK = 128
GDN_DV = 128
GDN_CONV = 4
GDN_CHUNK = 64
GDN_QK_W = GDN_QK_HEADS * GDN_DK
GDN_V_W = GDN_V_HEADS * GDN_DV
DSA_HEADS = 16
DSA_KV_HEADS = 4
DSA_GROUP = DSA_HEADS // DSA_KV_HEADS
DSA_DH = 128
IDX_HEADS = 16
IDX_DIM = 128
TOPK_MAX = 256
N_BUCKETS = 32
MAX_DISTANCE = 128
D_FF = 5120
FFN_CONV = 3
PLE_DIM = 256
DEPTH = 2
DN_ALPHA = (2.0 * DEPTH) ** 0.25
LN_EPS = 1e-5
RMS_EPS = 1e-6

V7X_VMEM_BYTES = 64 * 1024 * 1024
V7X_VMEM_BUDGET = 56 * 1024 * 1024
LANES = 128
BF16_SUBLANES = 16

TILE_M = 1024
TILE_N = 1024
FFN_TILE_N = 512
LN_TILE_M = 512
LN_SUB_M = 256

HALO = BF16_SUBLANES
Q_BLOCK = 128
IDX_KEY_BLOCK = 2048
IDX_SUB = 512
INT_MIN = -(2 ** 31)
BITS_PER_CHECK = 4
WORD_BITS = 32
FAR_MASKED = -2e30

_NT = (((1,), (1,)), ((), ()))
_TN = (((0,), (0,)), ((), ()))


def _cparams(semantics, vmem_bytes):
    return pltpu.CompilerParams(dimension_semantics=semantics,
                                vmem_limit_bytes=int(min(V7X_VMEM_BUDGET, vmem_bytes)))


def _silu(y):
    return y * jax.nn.sigmoid(y)


def _mm_scale_kernel(x_ref, w_ref, cs_ref, o_ref):
    acc = jnp.dot(x_ref[...], w_ref[...].astype(BF16), preferred_element_type=F32)
    o_ref[...] = (acc * cs_ref[...]).astype(o_ref.dtype)


def _weight_spec(w, K, tn, w_block):
    if w_block is None:
        return pl.BlockSpec((K, tn), lambda i, j: (0, j))
    return pl.BlockSpec((None, K, tn), lambda i, j: w_block(j))


def _matmul_scaled(x, w, colscale, out_dtype, tm, tn, w_block=None):
    M, K = x.shape
    N = colscale.shape[1]
    tm, tn = min(tm, M), min(tn, N)
    osz = jnp.dtype(out_dtype).itemsize
    wsz = jnp.dtype(w.dtype).itemsize
    vmem = 2 * (tm * K * 2 + K * tn * wsz + tm * tn * osz) + K * tn * 2 + 2 * tm * tn * 4
    return pl.pallas_call(
        _mm_scale_kernel,
        grid=(M // tm, N // tn),
        in_specs=[pl.BlockSpec((tm, K), lambda i, j: (i, 0)),
                  _weight_spec(w, K, tn, w_block),
                  pl.BlockSpec((1, tn), lambda i, j: (0, j))],
        out_specs=pl.BlockSpec((tm, tn), lambda i, j: (i, j)),
        out_shape=jax.ShapeDtypeStruct((M, N), out_dtype),
        compiler_params=_cparams(("parallel", "parallel"), vmem),
        name="matmul_scaled",
    )(x, w, colscale)


CONV_SUB = 256
FFN_SUB = 512


def _causal_conv(g, gh, cw_ref, g_scr, kc, tm, cols):
    g_scr[0:HALO, cols] = gh
    g_scr[HALO:HALO + tm, cols] = g
    y = cw_ref[kc - 1:kc, cols] * g
    for j in range(kc - 1):
        off = HALO - (kc - 1) + j
        y = y + cw_ref[j:j + 1, cols] * g_scr[off:off + tm, cols]
    return y


def _mm_conv_silu_kernel(x_ref, xh_ref, w_ref, cw_ref, o_ref, g_scr, *, kc, tm):
    first = pl.program_id(0) == 0
    for c0 in range(0, o_ref.shape[1], CONV_SUB):
        cols = slice(c0, c0 + CONV_SUB)
        w = w_ref[:, cols].astype(BF16)
        gh = jnp.dot(xh_ref[...], w, preferred_element_type=F32)
        gh = jnp.where(first, 0.0, gh)
        g = jnp.dot(x_ref[...], w, preferred_element_type=F32)
        y = _causal_conv(g, gh, cw_ref, g_scr, kc, tm, cols)
        o_ref[:, cols] = _silu(y).astype(o_ref.dtype)


def _proj_conv_silu(x, w, conv_w, tm, tn, w_block=None):
    M, K = x.shape
    kc, N = conv_w.shape
    tm, tn = min(tm, M), min(tn, N)
    hb = tm // HALO
    wsz = jnp.dtype(w.dtype).itemsize
    vmem = 2 * (tm * K * 2 + HALO * K * 2 + K * tn * wsz + tm * tn * 2) + K * tn * 2 + 4 * tm * tn * 4
    return pl.pallas_call(
        functools.partial(_mm_conv_silu_kernel, kc=kc, tm=tm),
        grid=(M // tm, N // tn),
        in_specs=[pl.BlockSpec((tm, K), lambda i, j: (i, 0)),
                  pl.BlockSpec((HALO, K), lambda i, j: (jnp.maximum(i * hb - 1, 0), 0)),
                  _weight_spec(w, K, tn, w_block),
                  pl.BlockSpec((kc, tn), lambda i, j: (0, j))],
        out_specs=pl.BlockSpec((tm, tn), lambda i, j: (i, j)),
        out_shape=jax.ShapeDtypeStruct((M, N), BF16),
        scratch_shapes=[pltpu.VMEM((tm + HALO, tn), F32)],
        compiler_params=_cparams(("parallel", "parallel"), vmem),
        name="proj_conv_silu",
    )(x, x, w, conv_w)


def _ffn_up_kernel(x_ref, xh_ref, wg_ref, wu_ref, cw_ref, o_ref, g_scr, *, kc, tm):
    first = pl.program_id(0) == 0
    for c0 in range(0, o_ref.shape[1], FFN_SUB):
        cols = slice(c0, c0 + FFN_SUB)
        wg = wg_ref[:, cols].astype(BF16)
        g = jnp.dot(x_ref[...], wg, preferred_element_type=F32)
        gh = jnp.dot(xh_ref[...], wg, preferred_element_type=F32)
        gh = jnp.where(first, 0.0, gh)
        u = jnp.dot(x_ref[...], wu_ref[:, cols].astype(BF16), preferred_element_type=F32)
        y = _causal_conv(g, gh, cw_ref, g_scr, kc, tm, cols)
        o_ref[:, cols] = (_silu(y) * u).astype(o_ref.dtype)


def _ffn_up(x, w_gate, w_up, conv_w, layer, tm, tn):
    M, K = x.shape
    N = w_gate.shape[2]
    kc = conv_w.shape[0]
    tm, tn = min(tm, M), min(tn, N)
    hb = tm // HALO
    wsz = jnp.dtype(w_gate.dtype).itemsize
    vmem = 2 * (tm * K * 2 + HALO * K * 2 + 2 * K * tn * wsz + tm * tn * 2) + 2 * K * tn * 2 + 6 * tm * tn * 4
    return pl.pallas_call(
        functools.partial(_ffn_up_kernel, kc=kc, tm=tm),
        grid=(M // tm, N // tn),
        in_specs=[pl.BlockSpec((tm, K), lambda i, j: (i, 0)),
                  pl.BlockSpec((HALO, K), lambda i, j: (jnp.maximum(i * hb - 1, 0), 0)),
                  pl.BlockSpec((None, K, tn), lambda i, j: (layer, 0, j)),
                  pl.BlockSpec((None, K, tn), lambda i, j: (layer, 0, j)),
                  pl.BlockSpec((kc, tn), lambda i, j: (0, j))],
        out_specs=pl.BlockSpec((tm, tn), lambda i, j: (i, j)),
        out_shape=jax.ShapeDtypeStruct((M, N), BF16),
        scratch_shapes=[pltpu.VMEM((tm + HALO, tn), F32)],
        compiler_params=_cparams(("parallel", "parallel"), vmem),
        name="ffn_up",
    )(x, x, w_gate, w_up, conv_w)


def _mm_res_ln_kernel(a_ref, w_ref, res_ref, g_ref, b_ref, of_ref, ob_ref, *, sub):
    for r0 in range(0, a_ref.shape[0], sub):
        rows = slice(r0, r0 + sub)
        acc = jnp.dot(a_ref[rows, :], w_ref[...], preferred_element_type=F32)
        y = DN_ALPHA * res_ref[rows, :] + acc
        mu = jnp.mean(y, axis=-1, keepdims=True)
        yc = y - mu
        var = jnp.mean(yc * yc, axis=-1, keepdims=True)
        out = yc * lax.rsqrt(var + LN_EPS) * g_ref[...] + b_ref[...]
        of_ref[rows, :] = out
        ob_ref[rows, :] = out.astype(BF16)


def _proj_res_ln(a, w, res, g, b, tm, sub):
    M, K = a.shape
    N = w.shape[1]
    tm = min(tm, M)
    sub = min(sub, tm)
    vmem = K * N * 2 + 2 * (tm * K * 2 + tm * N * 4 + tm * N * 4 + tm * N * 2) + 4 * sub * N * 4
    return pl.pallas_call(
        functools.partial(_mm_res_ln_kernel, sub=sub),
        grid=(M // tm,),
        in_specs=[pl.BlockSpec((tm, K), lambda i: (i, 0)),
                  pl.BlockSpec((K, N), lambda i: (0, 0), pipeline_mode=pl.Buffered(1)),
                  pl.BlockSpec((tm, N), lambda i: (i, 0)),
                  pl.BlockSpec((1, N), lambda i: (0, 0)),
                  pl.BlockSpec((1, N), lambda i: (0, 0))],
        out_specs=[pl.BlockSpec((tm, N), lambda i: (i, 0)),
                   pl.BlockSpec((tm, N), lambda i: (i, 0))],
        out_shape=[jax.ShapeDtypeStruct((M, N), F32), jax.ShapeDtypeStruct((M, N), BF16)],
        compiler_params=_cparams(("parallel",), vmem),
        name="proj_res_ln",
    )(a, w, res, g.reshape(1, N), b.reshape(1, N))


def _ple_kernel(xb_ref, wg_ref, p_ref, wp_ref, xr_ref, of_ref, ob_ref):
    gate = jax.nn.sigmoid(jnp.dot(xb_ref[...], wg_ref[...].astype(BF16), preferred_element_type=F32))
    pe = jnp.dot(p_ref[...].astype(BF16), wp_ref[...].astype(BF16), preferred_element_type=F32)
    out = xr_ref[...] + gate * pe
    of_ref[...] = out
    ob_ref[...] = out.astype(BF16)


def _ple(xb, xf, w_gate, p, w_proj, layer, tm, tn):
    M, K = xb.shape
    N = w_gate.shape[2]
    P = p.shape[3]
    tm, tn = min(tm, M), min(tn, N)
    wsz = jnp.dtype(w_gate.dtype).itemsize
    vmem = 2 * (tm * K * 2 + K * tn * wsz + tm * P * 4 + P * tn * 4 + tm * tn * 10) + K * tn * 2 + 4 * tm * tn * 4
    return pl.pallas_call(
        _ple_kernel,
        grid=(M // tm, N // tn),
        in_specs=[pl.BlockSpec((tm, K), lambda i, j: (i, 0)),
                  pl.BlockSpec((None, K, tn), lambda i, j: (layer, 0, j)),
                  pl.BlockSpec((None, None, tm, P), lambda i, j: (layer, 0, i, 0)),
                  pl.BlockSpec((None, P, tn), lambda i, j: (layer, 0, j)),
                  pl.BlockSpec((tm, tn), lambda i, j: (i, j))],
        out_specs=[pl.BlockSpec((tm, tn), lambda i, j: (i, j)),
                   pl.BlockSpec((tm, tn), lambda i, j: (i, j))],
        out_shape=[jax.ShapeDtypeStruct((M, N), F32), jax.ShapeDtypeStruct((M, N), BF16)],
        compiler_params=_cparams(("parallel", "parallel"), vmem),
        name="ple",
    )(xb, w_gate, p, w_proj, xf)


GDN_HB = 8
GDN_NC = 4
GDN_STAGE_UNITS = 32


def _gdn_kernel(q_ref, k_ref, v_ref, z_ref, sc_ref, hp_ref, ng_ref, o_ref, s_scr):
    C = GDN_CHUNK

    @pl.when(pl.program_id(1) == 0)
    def _():
        s_scr[...] = jnp.zeros_like(s_scr)

    row = lax.broadcasted_iota(I32, (C, C), 0)
    col = lax.broadcasted_iota(I32, (C, C), 1)
    tri = row >= col
    strict = row > col
    eye = row == col
    tri_f = tri.astype(F32)
    eye_f = eye.astype(F32)

    raw = sc_ref[...]
    a_log = hp_ref[0, 0:1, :]
    dt_b = hp_ref[0, 1:2, :]
    xs = raw + dt_b
    softplus = jnp.maximum(xs, 0.0) + jnp.log1p(jnp.exp(-jnp.abs(xs)))
    g_all = -jnp.exp(a_log) * softplus
    beta_all = jax.nn.sigmoid(raw)
    ng = ng_ref[...]

    units = [(c, j) for c in range(GDN_NC) for j in range(GDN_HB)]
    kb_l, rhs_l, decay_l, qd_l, kd_l, kbf_l, qbf_l, gl_l = [], [], [], [], [], [], [], []
    for c in range(GDN_NC):
        r0 = c * C
        gc = jnp.dot(tri_f, g_all[r0:r0 + C, :], precision=lax.Precision.HIGHEST,
                     preferred_element_type=F32)
        g_last = gc[C - 1:C, :]
        e_gc = jnp.exp(gc)
        e_rest = jnp.exp(g_last - gc)
        e_last = jnp.exp(g_last)
        beta_c = beta_all[r0:r0 + C, :]
        qn, kn = [], []
        for hq in range(GDN_HB // 2):
            qf = q_ref[r0:r0 + C, hq * GDN_DK:(hq + 1) * GDN_DK].astype(F32)
            kf = k_ref[r0:r0 + C, hq * GDN_DK:(hq + 1) * GDN_DK].astype(F32)
            qn.append(qf * lax.rsqrt(jnp.sum(qf * qf, axis=-1, keepdims=True) + RMS_EPS) * (GDN_DK ** -0.5))
            kn.append(kf * lax.rsqrt(jnp.sum(kf * kf, axis=-1, keepdims=True) + RMS_EPS))
        for j in range(GDN_HB):
            q_h, k_h = qn[j // 2], kn[j // 2]
            vf = v_ref[r0:r0 + C, j * GDN_DV:(j + 1) * GDN_DV].astype(F32)
            beta = beta_c[:, GDN_HB + j:GDN_HB + j + 1]
            kb = k_h * beta
            gcb = jnp.broadcast_to(gc[:, j:j + 1], (C, C))
            gcr = jnp.sum(jnp.where(eye, gcb, 0.0), axis=0, keepdims=True)
            decay_l.append(jnp.where(tri, jnp.exp(jnp.where(tri, gcb - gcr, 0.0)), 0.0))
            kb_l.append(kb.astype(BF16))
            rhs_l.append(jnp.concatenate([vf * beta, kb * e_gc[:, j:j + 1]], axis=-1).astype(BF16))
            qd_l.append((q_h * e_gc[:, j:j + 1]).astype(BF16))
            kd_l.append((k_h * e_rest[:, j:j + 1]).astype(BF16))
            kbf_l.append(k_h.astype(BF16))
            qbf_l.append(q_h.astype(BF16))
            gl_l.append(e_last[:, j:j + 1])

    n_u = len(units)
    qk_l, sol_l = [], []
    for b0 in range(0, n_u, GDN_STAGE_UNITS):
        us = range(b0, min(b0 + GDN_STAGE_UNITS, n_u))
        kk_b = [lax.dot_general(kb_l[u], kbf_l[u], _NT, preferred_element_type=F32) for u in us]
        qk_b = [lax.dot_general(qbf_l[u], kbf_l[u], _NT, preferred_element_type=F32) for u in us]
        qk_l += [jnp.where(tri, qk * decay_l[u], 0.0).astype(BF16) for qk, u in zip(qk_b, us)]
        x_b = [(-jnp.where(strict, kk * decay_l[u], 0.0)) for kk, u in zip(kk_b, us)]
        t_b = [eye_f + x for x in x_b]
        x_b = [x.astype(BF16) for x in x_b]
        for _ in range(5):
            x_b = [jnp.dot(x, x, preferred_element_type=F32).astype(BF16) for x in x_b]
            t_b = [t + jnp.dot(t.astype(BF16), x, preferred_element_type=F32) for t, x in zip(t_b, x_b)]
        sol_l += [jnp.dot(t.astype(BF16), rhs_l[u], preferred_element_type=F32) for t, u in zip(t_b, us)]

    s_cur = [s_scr[j] for j in range(GDN_HB)]
    for c in range(GDN_NC):
        r0 = c * C
        us = [c * GDN_HB + j for j in range(GDN_HB)]
        s_bf = [s.astype(BF16) for s in s_cur]
        ws_l = [jnp.dot(sol_l[u][:, GDN_DV:].astype(BF16), s_bf[j], preferred_element_type=F32)
                for j, u in enumerate(us)]
        qs_l = [jnp.dot(qd_l[u], s_bf[j], preferred_element_type=F32) for j, u in enumerate(us)]
        vn_l = [(sol_l[u][:, :GDN_DV] - ws_l[j]).astype(BF16) for j, u in enumerate(us)]
        kv_l = [lax.dot_general(kd_l[u], vn_l[j], _TN, preferred_element_type=F32) for j, u in enumerate(us)]
        ov_l = [jnp.dot(qk_l[u], vn_l[j], preferred_element_type=F32) for j, u in enumerate(us)]
        s_cur = [s_cur[j] * gl_l[u] + kv_l[j] for j, u in enumerate(us)]
        for j in range(GDN_HB):
            o = qs_l[j] + ov_l[j]
            zf = z_ref[r0:r0 + C, j * GDN_DV:(j + 1) * GDN_DV].astype(F32)
            o = o * lax.rsqrt(jnp.mean(o * o, axis=-1, keepdims=True) + RMS_EPS) * ng * _silu(zf)
            o_ref[r0:r0 + C, j * GDN_DV:(j + 1) * GDN_DV] = o.astype(o_ref.dtype)
    for j in range(GDN_HB):
        s_scr[j] = s_cur[j]


def _gdn_core(qkv, z, scal, hparams, norm_g):
    L = qkv.shape[0]
    G = GDN_V_HEADS // GDN_HB
    R = GDN_NC * GDN_CHUNK
    qw = (GDN_HB // 2) * GDN_DK
    vw = GDN_HB * GDN_DV
    k_blk0 = GDN_QK_W // qw
    v_blk0 = 2 * GDN_QK_W // vw
    vmem = 2 * (2 * R * qw * 2 + 2 * R * vw * 2 + R * LANES * 4 + R * vw * 2) + (16 << 20)
    return pl.pallas_call(
        _gdn_kernel,
        grid=(G, L // R),
        in_specs=[pl.BlockSpec((R, qw), lambda g, s: (s, g)),
                  pl.BlockSpec((R, qw), lambda g, s: (s, k_blk0 + g)),
                  pl.BlockSpec((R, vw), lambda g, s: (s, v_blk0 + g)),
                  pl.BlockSpec((R, vw), lambda g, s: (s, g)),
                  pl.BlockSpec((R, LANES), lambda g, s: (s, g)),
                  pl.BlockSpec((1, 8, LANES), lambda g, s: (g, 0, 0)),
                  pl.BlockSpec((1, GDN_DV), lambda g, s: (0, 0))],
        out_specs=pl.BlockSpec((R, vw), lambda g, s: (s, g)),
        out_shape=jax.ShapeDtypeStruct((L, GDN_V_W), BF16),
        scratch_shapes=[pltpu.VMEM((GDN_HB, GDN_DK, GDN_DV), F32)],
        compiler_params=_cparams(("parallel", "arbitrary"), vmem),
        name="gdn_core",
    )(qkv, qkv, qkv, z, scal, hparams, norm_g.reshape(1, GDN_DV))


def _gdn_mixer(xb, w_in_all, layer, conv_w, a_log, dt_bias, norm_g):
    nqkv = 2 * GDN_QK_W + GDN_V_W
    tn = TILE_N
    w_ab = w_in_all[layer, :, nqkv + GDN_V_W:]
    G = GDN_V_HEADS // GDN_HB
    w_a = w_ab[:, :GDN_V_HEADS].reshape(D_MODEL, G, GDN_HB)
    w_b = w_ab[:, GDN_V_HEADS:].reshape(D_MODEL, G, GDN_HB)
    w_sc = jnp.concatenate([w_a, w_b, jnp.zeros((D_MODEL, G, LANES - 2 * GDN_HB), F32)], axis=-1)
    w_sc = w_sc.reshape(D_MODEL, G * LANES).astype(BF16)
    qkv = _proj_conv_silu(xb, w_in_all, conv_w, tm=TILE_M, tn=tn, w_block=lambda j: (layer, 0, j))
    ones_z = jnp.ones((1, GDN_V_W), F32)
    z = _matmul_scaled(xb, w_in_all, ones_z, BF16, tm=TILE_M, tn=tn,
                       w_block=lambda j: (layer, 0, nqkv // tn + j))
    scal = _matmul_scaled(xb, w_sc, jnp.ones((1, G * LANES), F32), F32, tm=TILE_M, tn=G * LANES)
    hp = jnp.zeros((G, 8, LANES), F32)
    hp = hp.at[:, 0, :GDN_HB].set(a_log.reshape(G, GDN_HB))
    hp = hp.at[:, 1, :GDN_HB].set(dt_bias.reshape(G, GDN_HB))
    return _gdn_core(qkv, z, scal, hp, norm_g)


def _dsa_small_kernel(x_ref, w_ref, g_ref, b_ref, ki_ref, wi_ref):
    acc = jnp.dot(x_ref[...], w_ref[...], preferred_element_type=F32)
    ki = acc[:, :IDX_DIM]
    mu = jnp.mean(ki, axis=-1, keepdims=True)
    kc = ki - mu
    var = jnp.mean(kc * kc, axis=-1, keepdims=True)
    ki_ref[...] = (kc * lax.rsqrt(var + LN_EPS) * g_ref[...] + b_ref[...]).astype(ki_ref.dtype)
    wi_ref[...] = acc[:, IDX_DIM:] * ((IDX_HEADS ** -0.5) * (IDX_DIM ** -0.5))


def _dsa_small(xb, w_small, ln_g, ln_b, tm):
    M, K = xb.shape
    tm = min(tm, M)
    N = 2 * LANES
    vmem = 2 * (tm * K * 2 + K * N * 2 + tm * LANES * 6) + 4 * tm * N * 4
    return pl.pallas_call(
        _dsa_small_kernel,
        grid=(M // tm,),
        in_specs=[pl.BlockSpec((tm, K), lambda i: (i, 0)),
                  pl.BlockSpec((K, N), lambda i: (0, 0)),
                  pl.BlockSpec((1, IDX_DIM), lambda i: (0, 0)),
                  pl.BlockSpec((1, IDX_DIM), lambda i: (0, 0))],
        out_specs=[pl.BlockSpec((tm, IDX_DIM), lambda i: (i, 0)),
                   pl.BlockSpec((tm, LANES), lambda i: (i, 0))],
        out_shape=[jax.ShapeDtypeStruct((M, IDX_DIM), BF16), jax.ShapeDtypeStruct((M, LANES), F32)],
        compiler_params=_cparams(("parallel",), vmem),
        name="dsa_idx_proj",
    )(xb, w_small, ln_g.reshape(1, IDX_DIM), ln_b.reshape(1, IDX_DIM))


def _sortable_key(score):
    bits = lax.bitcast_convert_type(score, I32)
    return jnp.where(bits >= 0, bits, bits ^ jnp.int32(0x7FFFFFFF))


def _idx_kernel(qi_tab, kj_tab, last_tab,
                qidx_ref, kidx_ref, wi_ref, far_ref, near_ref, key_scr, plane_scr, cand_scr, w_scr,
                *, k_top, n_sub_total):
    s = pl.program_id(0)
    i = qi_tab[s]
    j = kj_tab[s]

    @pl.when(s == 0)
    def _():
        key_scr[...] = jnp.full_like(key_scr, INT_MIN)
        plane_scr[...] = jnp.zeros_like(plane_scr)
    nsub = IDX_KEY_BLOCK // IDX_SUB
    t_col = i * Q_BLOCK + lax.broadcasted_iota(I32, (Q_BLOCK, 1), 0)
    lane_sub = lax.broadcasted_iota(I32, (Q_BLOCK, IDX_SUB), 1)
    wi = wi_ref[...]

    qi_rows = jnp.concatenate([qidx_ref[:, h * IDX_DIM:(h + 1) * IDX_DIM] for h in range(IDX_HEADS)], axis=0)
    for sub in range(nsub):
        ki_sub = kidx_ref[sub * IDX_SUB:(sub + 1) * IDX_SUB, :]
        sc_all = lax.dot_general(qi_rows, ki_sub, _NT, preferred_element_type=F32)
        acc = jnp.zeros((Q_BLOCK, IDX_SUB), F32)
        for h in range(IDX_HEADS):
            acc = acc + jnp.maximum(sc_all[h * Q_BLOCK:(h + 1) * Q_BLOCK, :], 0.0) * wi[:, h:h + 1]
        s_idx = j * IDX_KEY_BLOCK + sub * IDX_SUB + lane_sub
        key_scr[j * nsub + sub] = jnp.where(s_idx <= t_col, _sortable_key(acc), INT_MIN)

    @pl.when(last_tab[s] == 1)
    def _():
        n_chunks = (i * Q_BLOCK + Q_BLOCK - 1) // IDX_SUB + 1

        def count(pred, ref_val):
            refb = jnp.broadcast_to(ref_val, (Q_BLOCK, LANES))

            def body(c, cnt):
                blk = key_scr[c]
                for l in range(IDX_SUB // LANES):
                    cnt = cnt + jnp.where(pred(blk[:, l * LANES:(l + 1) * LANES], refb), 1, 0)
                return cnt

            cnt = lax.fori_loop(0, n_chunks, body, jnp.zeros((Q_BLOCK, LANES), I32))
            return jnp.sum(cnt, axis=1, keepdims=True)

        tiles_per_chunk = IDX_SUB // LANES
        chunks_per_group = WORD_BITS // tiles_per_chunk
        n_tiles = n_chunks * tiles_per_chunk
        n_groups = (n_tiles + WORD_BITS - 1) // WORD_BITS

        def build_group(g, carry):
            def build_rows(r, carry_r):
                r8 = pl.multiple_of(r * 8, 8)
                a = [key_scr[g * chunks_per_group + t // tiles_per_chunk, pl.ds(r8, 8),
                             (t % tiles_per_chunk) * LANES:(t % tiles_per_chunk + 1) * LANES]
                     for t in range(WORD_BITS)]
                m, sh = 0x0000FFFF, 16
                while sh:
                    k = 0
                    while k < WORD_BITS:
                        x = (a[k] ^ lax.shift_right_logical(a[k + sh], jnp.int32(sh))) & jnp.int32(m)
                        a[k] = a[k] ^ x
                        a[k + sh] = a[k + sh] ^ jnp.left_shift(x, jnp.int32(sh))
                        k = (k + sh + 1) & ~sh
                    sh >>= 1
                    m = (m ^ (m << sh)) & 0xFFFFFFFF if sh else m
                a[0] = ~a[0]
                for p in range(WORD_BITS):
                    plane_scr[p, g, pl.ds(r8, 8), :] = a[p]
                return carry_r

            lax.fori_loop(0, Q_BLOCK // 8, build_rows, 0)
            return carry

        lax.fori_loop(0, n_groups, build_group, 0)
        n_groups_max = plane_scr.shape[1]
        for g in range(n_groups_max):
            n_valid = jnp.clip(n_tiles - g * WORD_BITS, 0, WORD_BITS)
            word = jnp.where(n_valid >= WORD_BITS, jnp.int32(-1),
                             jnp.where(n_valid <= 0, jnp.int32(0),
                                       jnp.left_shift(jnp.int32(-1), WORD_BITS - n_valid)))
            cand_scr[g] = jnp.broadcast_to(word, (Q_BLOCK, LANES))

        def plane_pass(p, thr_u, n_above, n_cand):
            cnt = jnp.zeros((Q_BLOCK, LANES), I32)
            for g in range(n_groups_max):
                w = plane_scr[p, g] & cand_scr[g]
                w_scr[g] = w
                cnt = cnt + lax.population_count(w)
            n_set = jnp.sum(cnt, axis=1, keepdims=True)
            take = n_above + n_set >= k_top
            take_b = jnp.broadcast_to(take, (Q_BLOCK, LANES))
            for g in range(n_groups_max):
                w = w_scr[g]
                cand_scr[g] = jnp.where(take_b, w, cand_scr[g] ^ w)
            bit = jnp.left_shift(jnp.int32(1), WORD_BITS - 1 - p)
            return (jnp.where(take, thr_u | bit, thr_u), jnp.where(take, n_above, n_above + n_set),
                    jnp.where(take, n_set, n_cand - n_set))

        def group_body(carry):
            grp, thr_u, n_above, n_cand, _ = carry
            for bb in range(BITS_PER_CHECK):
                thr_u, n_above, n_cand = plane_pass(grp * BITS_PER_CHECK + bb, thr_u, n_above, n_cand)
            return grp + 1, thr_u, n_above, n_cand, jnp.max(jnp.where(n_above + n_cand != k_top, 1, 0))

        _, thr_u, n_above, n_cand, n_tied_rows = lax.while_loop(
            lambda carry: (carry[0] < WORD_BITS // BITS_PER_CHECK) & (carry[4] != 0), group_body,
            (jnp.int32(0), jnp.zeros((Q_BLOCK, 1), I32), jnp.zeros((Q_BLOCK, 1), I32),
             jnp.broadcast_to(n_tiles * LANES, (Q_BLOCK, 1)).astype(I32), jnp.int32(1)))
        thr = thr_u ^ jnp.int32(INT_MIN)
        thr_b = jnp.broadcast_to(thr, (Q_BLOCK, IDX_SUB))

        def emit(c, sel):
            s_idx = c * IDX_SUB + lane_sub
            far = sel & (t_col - s_idx >= MAX_DISTANCE)
            far_ref[0, c] = jnp.where(far, 0.0, FAR_MASKED).astype(far_ref.dtype)
            key_scr[c] = jnp.where(sel, 1, 0)

        @pl.when(n_tied_rows == 0)
        def _():
            def sel_body(c, carry):
                emit(c, key_scr[c] >= thr_b)
                return carry

            lax.fori_loop(0, n_chunks, sel_body, 0)

        @pl.when(n_tied_rows != 0)
        def _():
            n_gt = count(lambda a, r: a > r, thr)
            need_eq = (k_top - n_gt).astype(F32)
            incl = (lax.broadcasted_iota(I32, (IDX_SUB, IDX_SUB), 0)
                    <= lax.broadcasted_iota(I32, (IDX_SUB, IDX_SUB), 1)).astype(BF16)

            def sel_body(c, carry):
                blk = key_scr[c]
                eq = blk == thr_b
                eq_f = jnp.where(eq, 1.0, 0.0)
                rank = carry + jnp.dot(eq_f.astype(BF16), incl, preferred_element_type=F32)
                s_idx = c * IDX_SUB + lane_sub
                emit(c, ((blk > thr_b) | (eq & (rank <= need_eq))) & (s_idx <= t_col))
                return carry + jnp.sum(eq_f, axis=1, keepdims=True)

            lax.fori_loop(0, n_chunks, sel_body, jnp.zeros((Q_BLOCK, 1), F32))

        def fill_body(c, carry):
            far_ref[0, c] = jnp.full((Q_BLOCK, IDX_SUB), FAR_MASKED, far_ref.dtype)
            return carry

        lax.fori_loop(n_chunks, n_sub_total, fill_body, 0)

        def window(blk_idx):
            per = IDX_SUB // Q_BLOCK
            chunk = key_scr[blk_idx // per]
            m = blk_idx % per
            out = chunk[:, 0:Q_BLOCK]
            for q in range(1, per):
                out = jnp.where(m == q, chunk[:, q * Q_BLOCK:(q + 1) * Q_BLOCK], out)
            return out

        r_i = lax.broadcasted_iota(I32, (Q_BLOCK, Q_BLOCK), 0)
        c_i = lax.broadcasted_iota(I32, (Q_BLOCK, Q_BLOCK), 1)
        d_lo = Q_BLOCK + r_i - c_i
        d_hi = r_i - c_i
        near_lo = (window(jnp.maximum(i - 1, 0)) != 0) & (d_lo < MAX_DISTANCE) & (i >= 1)
        near_hi = (window(i) != 0) & (d_hi >= 0) & (d_hi < MAX_DISTANCE)
        near_ref[:, 0:Q_BLOCK] = jnp.where(near_lo, 0.0, -jnp.inf).astype(near_ref.dtype)
        near_ref[:, Q_BLOCK:2 * Q_BLOCK] = jnp.where(near_hi, 0.0, -jnp.inf).astype(near_ref.dtype)


def _idx_select(proj, ki, wi, k_top):
    L = ki.shape[0]
    nq = L // Q_BLOCK
    n_sub_total = L // IDX_SUB
    qi_l, kj_l, last_l = [], [], []
    for i in range(nq):
        j_last = (i * Q_BLOCK + Q_BLOCK - 1) // IDX_KEY_BLOCK
        for j in range(j_last + 1):
            qi_l.append(i)
            kj_l.append(j)
            last_l.append(1 if j == j_last else 0)
    tabs = [jnp.asarray(np.asarray(t, np.int32)) for t in (qi_l, kj_l, last_l)]
    qcol = (DSA_HEADS * DSA_DH) // (IDX_HEADS * IDX_DIM)
    group_keys = WORD_BITS * LANES
    n_groups_max = -(-L // group_keys)
    key_chunks = n_groups_max * (group_keys // IDX_SUB)
    vmem = ((key_chunks * IDX_SUB + (WORD_BITS + 2) * n_groups_max * LANES) * Q_BLOCK * 4
            + 2 * (Q_BLOCK * L * 2) + 2 * (Q_BLOCK * IDX_HEADS * IDX_DIM * 2)
            + 2 * IDX_KEY_BLOCK * IDX_DIM * 2 + (8 << 20))
    grid_spec = pltpu.PrefetchScalarGridSpec(
        num_scalar_prefetch=3,
        grid=(len(qi_l),),
        in_specs=[pl.BlockSpec((Q_BLOCK, IDX_HEADS * IDX_DIM), lambda s, qt, kt, lt: (qt[s], qcol)),
                  pl.BlockSpec((IDX_KEY_BLOCK, IDX_DIM), lambda s, qt, kt, lt: (kt[s], 0)),
                  pl.BlockSpec((Q_BLOCK, LANES), lambda s, qt, kt, lt: (qt[s], 0))],
        out_specs=[pl.BlockSpec((1, n_sub_total, Q_BLOCK, IDX_SUB), lambda s, qt, kt, lt: (qt[s], 0, 0, 0)),
                   pl.BlockSpec((Q_BLOCK, 2 * Q_BLOCK), lambda s, qt, kt, lt: (qt[s], 0))],
        scratch_shapes=[pltpu.VMEM((key_chunks, Q_BLOCK, IDX_SUB), I32),
                        pltpu.VMEM((WORD_BITS, n_groups_max, Q_BLOCK, LANES), I32),
                        pltpu.VMEM((n_groups_max, Q_BLOCK, LANES), I32),
                        pltpu.VMEM((n_groups_max, Q_BLOCK, LANES), I32)],
    )
    return pl.pallas_call(
        functools.partial(_idx_kernel, k_top=k_top, n_sub_total=n_sub_total),
        grid_spec=grid_spec,
        out_shape=[jax.ShapeDtypeStruct((nq, n_sub_total, Q_BLOCK, IDX_SUB), BF16),
                   jax.ShapeDtypeStruct((L, 2 * Q_BLOCK), BF16)],
        compiler_params=_cparams(("arbitrary",), vmem),
        name="dsa_idx_select",
    )(*tabs, proj, ki, wi)


M_INIT = -1e30
FAR_SUBS = 2
LOG2E = math.log2(math.e)


def _attn_kernel(qi_tab, kj_tab, kind_tab, first_tab,
                 tab_ref, q_ref, kt_ref, vf_ref, klo_ref, khi_ref, vlo_ref, vhi_ref, far_ref, near_ref,
                 o_ref, m_scr, l_scr, acc_scr, b_scr, s_scr):
    s = pl.program_id(0)

    @pl.when(s == 0)
    def _():
        r_i = lax.broadcasted_iota(I32, (Q_BLOCK, 2 * Q_BLOCK), 0)
        c_i = lax.broadcasted_iota(I32, (Q_BLOCK, 2 * Q_BLOCK), 1)
        d = jnp.maximum(Q_BLOCK + r_i - c_i, 0)
        max_exact = N_BUCKETS // 2
        df = jnp.maximum(d, 1).astype(F32)
        large = max_exact + (jnp.log(df / max_exact) / math.log(MAX_DISTANCE / max_exact)
                             * (N_BUCKETS - max_exact)).astype(I32)
        large = jnp.minimum(large, N_BUCKETS - 1)
        bkt = jnp.where(d < max_exact, d, large)
        for h in range(DSA_HEADS):
            acc = jnp.zeros((Q_BLOCK, 2 * Q_BLOCK), F32)
            for b in range(N_BUCKETS):
                acc = jnp.where(bkt == b, (tab_ref[b, h] - tab_ref[N_BUCKETS - 1, h]) * LOG2E, acc)
            b_scr[h] = acc

    @pl.when(first_tab[s] == 1)
    def _():
        m_scr[...] = jnp.full_like(m_scr, M_INIT)
        l_scr[...] = jnp.zeros_like(l_scr)
        acc_scr[...] = jnp.zeros_like(acc_scr)

    def softmax_pv(v_ref, width, row0=0):
        nt = width // LANES
        rows = slice(row0, row0 + width)
        ones = jnp.ones((width, DSA_DH), BF16)
        for g in range(DSA_KV_HEADS):
            heads = range(g * DSA_GROUP, (g + 1) * DSA_GROUP)
            alphas, pbs = [], []
            for h in heads:
                tiles = [s_scr[h, :, t * LANES:(t + 1) * LANES] for t in range(nt)]
                tmax = tiles[0]
                for t in range(1, nt):
                    tmax = jnp.maximum(tmax, tiles[t])
                m_prev = m_scr[h]
                m_new = jnp.maximum(m_prev, jnp.max(tmax, axis=-1, keepdims=True))
                alphas.append(jnp.exp2(m_prev - m_new))
                p = [jnp.exp2(tiles[t] - m_new).astype(BF16) for t in range(nt)]
                pbs.append(jnp.concatenate(p, axis=-1) if nt > 1 else p[0])
                m_scr[h] = m_new
            v_aug = jnp.concatenate([v_ref[rows, g * DSA_DH:(g + 1) * DSA_DH], ones], axis=-1)
            pv = jnp.dot(jnp.concatenate(pbs, axis=0), v_aug,
                         preferred_element_type=F32)
            for n, h in enumerate(heads):
                pv_h = pv[n * Q_BLOCK:(n + 1) * Q_BLOCK]
                acc_scr[h] = alphas[n] * acc_scr[h] + pv_h[:, :DSA_DH]
                l_scr[h] = alphas[n] * l_scr[h] + pv_h[:, DSA_DH:]

    @pl.when(kind_tab[s] == 0)
    def _():
        eye = (lax.broadcasted_iota(I32, (Q_BLOCK, Q_BLOCK), 0)
               == lax.broadcasted_iota(I32, (Q_BLOCK, Q_BLOCK), 1)).astype(BF16)
        for sub in range(FAR_SUBS):
            cols = slice(sub * IDX_SUB, (sub + 1) * IDX_SUB)
            mask = far_ref[0, sub]
            for g in range(DSA_KV_HEADS):
                lhs = jnp.concatenate(
                    [jnp.concatenate([q_ref[:, h * DSA_DH:(h + 1) * DSA_DH], eye], axis=1)
                     for h in range(g * DSA_GROUP, (g + 1) * DSA_GROUP)], axis=0)
                rhs = jnp.concatenate([kt_ref[g * DSA_DH:(g + 1) * DSA_DH, cols], mask], axis=0)
                logits = jnp.dot(lhs, rhs, preferred_element_type=F32)
                for hh in range(DSA_GROUP):
                    s_scr[g * DSA_GROUP + hh] = logits[hh * Q_BLOCK:(hh + 1) * Q_BLOCK, :]
            softmax_pv(vf_ref, IDX_SUB, sub * IDX_SUB)

    @pl.when(kind_tab[s] == 1)
    def _():
        for half, (k_ref, v_ref) in enumerate(((klo_ref, vlo_ref), (khi_ref, vhi_ref))):
            cols = slice(half * Q_BLOCK, (half + 1) * Q_BLOCK)
            mask = near_ref[:, cols].astype(F32)
            for h in range(DSA_HEADS):
                g = h // DSA_GROUP
                logits = lax.dot_general(q_ref[:, h * DSA_DH:(h + 1) * DSA_DH],
                                         k_ref[:, g * DSA_DH:(g + 1) * DSA_DH], _NT, preferred_element_type=F32)
                s_scr[h, :, 0:Q_BLOCK] = logits + b_scr[h][:, cols] + mask
            softmax_pv(v_ref, Q_BLOCK)
        for h in range(DSA_HEADS):
            o_ref[:, h * DSA_DH:(h + 1) * DSA_DH] = (acc_scr[h] / l_scr[h]).astype(o_ref.dtype)


def _masked_attention(proj, far, near, rel_bias):
    L = proj.shape[0]
    nq = L // Q_BLOCK
    far_keys = FAR_SUBS * IDX_SUB
    per = far_keys // Q_BLOCK
    qi_l, kj_l, kind_l, first_l = [], [], [], []
    for i in range(nq):
        n_far = -(-i // per)
        for j in range(n_far):
            qi_l.append(i); kj_l.append(j); kind_l.append(0); first_l.append(1 if j == 0 else 0)
        qi_l.append(i); kj_l.append(max(n_far - 1, 0)); kind_l.append(1); first_l.append(1 if n_far == 0 else 0)
    tabs = [jnp.asarray(np.asarray(t, np.int32)) for t in (qi_l, kj_l, kind_l, first_l)]
    qw = DSA_HEADS * DSA_DH
    kvw = DSA_KV_HEADS * DSA_DH
    k_col = (2 * qw) // kvw
    v_col = k_col + 1
    hw = DSA_HEADS
    vmem = (2 * (Q_BLOCK * qw * 2 * 2 + 2 * far_keys * kvw * 2 + 4 * Q_BLOCK * kvw * 2
                 + Q_BLOCK * far_keys * 2 + Q_BLOCK * 2 * Q_BLOCK * 2)
            + hw * Q_BLOCK * (3 * LANES + 2 * Q_BLOCK) * 4 + (16 << 20))
    idx = lambda f: (lambda s, qt, kt, kd, ft: f(qt[s], kt[s]))
    k_t = proj[:, k_col * kvw:(k_col + 1) * kvw].T
    grid_spec = pltpu.PrefetchScalarGridSpec(
        num_scalar_prefetch=4,
        grid=(len(qi_l),),
        in_specs=[pl.BlockSpec(memory_space=pltpu.SMEM),
                  pl.BlockSpec((Q_BLOCK, qw), idx(lambda i, j: (i, 0))),
                  pl.BlockSpec((kvw, far_keys), idx(lambda i, j: (0, j))),
                  pl.BlockSpec((far_keys, kvw), idx(lambda i, j: (j, v_col))),
                  pl.BlockSpec((Q_BLOCK, kvw), idx(lambda i, j: (jnp.maximum(i - 1, 0), k_col))),
                  pl.BlockSpec((Q_BLOCK, kvw), idx(lambda i, j: (i, k_col))),
                  pl.BlockSpec((Q_BLOCK, kvw), idx(lambda i, j: (jnp.maximum(i - 1, 0), v_col))),
                  pl.BlockSpec((Q_BLOCK, kvw), idx(lambda i, j: (i, v_col))),
                  pl.BlockSpec((1, FAR_SUBS, Q_BLOCK, IDX_SUB), idx(lambda i, j: (i, j, 0, 0))),
                  pl.BlockSpec((Q_BLOCK, 2 * Q_BLOCK), idx(lambda i, j: (i, 0)))],
        out_specs=pl.BlockSpec((Q_BLOCK, qw), idx(lambda i, j: (i, 0))),
        scratch_shapes=[pltpu.VMEM((hw, Q_BLOCK, LANES), F32),
                        pltpu.VMEM((hw, Q_BLOCK, LANES), F32),
                        pltpu.VMEM((hw, Q_BLOCK, DSA_DH), F32),
                        pltpu.VMEM((hw, Q_BLOCK, 2 * Q_BLOCK), F32),
                        pltpu.VMEM((hw, Q_BLOCK, IDX_SUB), F32)],
    )
    return pl.pallas_call(
        _attn_kernel,
        grid_spec=grid_spec,
        out_shape=jax.ShapeDtypeStruct((L, qw), BF16),
        compiler_params=_cparams(("arbitrary",), vmem),
        name="dsa_attention",
    )(*tabs, rel_bias, proj, k_t, proj, proj, proj, proj, proj, far, near)


def _dsa_mixer(xb, w_in_all, layer, ln_g, ln_b, rel_bias):
    L = xb.shape[0]
    k_top = min(TOPK_MAX, L // 4)
    sq = DSA_HEADS * DSA_DH
    skv = DSA_KV_HEADS * DSA_DH
    si = IDX_HEADS * IDX_DIM
    w_ki = w_in_all[layer, :, sq + 2 * skv + si:sq + 2 * skv + si + IDX_DIM]
    w_wi = w_in_all[layer, :, sq + 2 * skv + si + IDX_DIM:]
    colscale = jnp.concatenate([jnp.full((1, sq), DSA_DH ** -0.5 * LOG2E, F32),
                                jnp.ones((1, si + 2 * skv), F32)], axis=1)
    w_small = jnp.concatenate([w_ki, w_wi, jnp.zeros((D_MODEL, LANES - IDX_HEADS), F32)], axis=1).astype(BF16)
    tn = 2 * skv
    nq, ni = sq // tn, si // tn
    src_block = lambda j: jnp.where(j < nq, j, jnp.where(j < nq + ni, j + 1, nq))
    proj = _matmul_scaled(xb, w_in_all, colscale, BF16, tm=TILE_M, tn=tn,
                          w_block=lambda j: (layer, 0, src_block(j)))
    ki, wi = _dsa_small(xb, w_small, ln_g, ln_b, tm=TILE_M)
    far, near = _idx_select(proj, ki, wi, k_top)
    return _masked_attention(proj, far, near, rel_bias)


def kernel(x, p, gdn_w_in, gdn_conv_w, gdn_a_log, gdn_dt_bias, gdn_norm_g, gdn_w_o, dsa_w_in, dsa_kidx_ln_g, dsa_kidx_ln_b, dsa_w_o, rel_bias, ln1_g, ln1_b, ffn_w_gate, ffn_w_up, ffn_conv_w, ffn_w_down, ln2_g, ln2_b, ple_w_proj, ple_w_gate):
    assert x.shape[0] == 1 and x.shape[2] == D_MODEL
    xf = x[0]
    xb = xf.astype(BF16)
    ple_gate_bf = ple_w_gate.astype(BF16)
    ia = ib = 0
    for i in range(DEPTH):
        if i % 2 == 0:
            mix = _gdn_mixer(xb, gdn_w_in, ia, gdn_conv_w[ia], gdn_a_log[ia], gdn_dt_bias[ia], gdn_norm_g[ia])
            w_o = gdn_w_o[ia]
            ia += 1
        else:
            mix = _dsa_mixer(xb, dsa_w_in, ib, dsa_kidx_ln_g[ib], dsa_kidx_ln_b[ib], rel_bias)
            w_o = dsa_w_o[ib]
            ib += 1
        xf, xb = _proj_res_ln(mix, w_o.astype(BF16), xf, ln1_g[i], ln1_b[i], tm=LN_TILE_M, sub=LN_SUB_M)
        hmid = _ffn_up(xb, ffn_w_gate, ffn_w_up, ffn_conv_w[i], i, tm=TILE_M, tn=FFN_TILE_N)
        xf, xb = _proj_res_ln(hmid, ffn_w_down[i].astype(BF16), xf, ln2_g[i], ln2_b[i], tm=LN_TILE_M, sub=LN_SUB_M)
        xf, xb = _ple(xb, xf, ple_gate_bf, p, ple_w_proj, i, tm=TILE_M, tn=TILE_N)
    return xf[None]
```

```python
import functools
import math

import jax
import jax.numpy as jnp
import numpy as np
from jax import lax
from jax.experimental import pallas as pl
from jax.experimental.pallas import tpu as pltpu

F32 = jnp.float32
BF16 = jnp.bfloat16
I32 = jnp.int32

D_MODEL = 2048
GDN_QK_HEADS = 16
GDN_V_HEADS = 32
GDN_DK = 128
GDN_DV = 128
GDN_CONV = 4
GDN_CHUNK = 64
GDN_QK_W = GDN_QK_HEADS * GDN_DK
GDN_V_W = GDN_V_HEADS * GDN_DV
DSA_HEADS = 16
DSA_KV_HEADS = 4
DSA_GROUP = DSA_HEADS // DSA_KV_HEADS
DSA_DH = 128
IDX_HEADS = 16
IDX_DIM = 128
TOPK_MAX = 256
N_BUCKETS = 32
MAX_DISTANCE = 128
D_FF = 5120
FFN_CONV = 3
PLE_DIM = 256
DEPTH = 2
DN_ALPHA = (2.0 * DEPTH) ** 0.25
LN_EPS = 1e-5
RMS_EPS = 1e-6

V7X_VMEM_BYTES = 64 * 1024 * 1024
V7X_VMEM_BUDGET = 56 * 1024 * 1024
LANES = 128
BF16_SUBLANES = 16

TILE_M = 1024
TILE_N = 1024
FFN_TILE_N = 512
LN_TILE_M = 512
LN_SUB_M = 256

HALO = BF16_SUBLANES
Q_BLOCK = 128
IDX_KEY_BLOCK = 2048
IDX_SUB = 512
INT_MIN = -(2 ** 31)
BITS_PER_CHECK = 4
WORD_BITS = 32
FAR_MASKED = -2e30

_NT = (((1,), (1,)), ((), ()))
_TN = (((0,), (0,)), ((), ()))


def _cparams(semantics, vmem_bytes):
    return pltpu.CompilerParams(dimension_semantics=semantics,
                                vmem_limit_bytes=int(min(V7X_VMEM_BUDGET, vmem_bytes)))


def _silu(y):
    return y * jax.nn.sigmoid(y)


def _mm_scale_kernel(x_ref, w_ref, cs_ref, o_ref, *, w_t):
    w = w_ref[...].astype(BF16)
    acc = (lax.dot_general(x_ref[...], w, _NT, preferred_element_type=F32) if w_t
           else jnp.dot(x_ref[...], w, preferred_element_type=F32))
    o_ref[...] = (acc * cs_ref[...]).astype(o_ref.dtype)


def _weight_spec(w, K, tn, w_block, w_t=False):
    if w_block is None:
        return pl.BlockSpec((K, tn), lambda i, j: (0, j))
    if w_t:
        return pl.BlockSpec((None, tn, K), lambda i, j: (w_block(j)[0], w_block(j)[2], 0))
    return pl.BlockSpec((None, K, tn), lambda i, j: w_block(j))


def _matmul_scaled(x, w, colscale, out_dtype, tm, tn, w_block=None, w_t=False):
    M, K = x.shape
    N = colscale.shape[1]
    tm, tn = min(tm, M), min(tn, N)
    osz = jnp.dtype(out_dtype).itemsize
    wsz = jnp.dtype(w.dtype).itemsize
    vmem = 2 * (tm * K * 2 + K * tn * wsz + tm * tn * osz) + K * tn * 2 + 2 * tm * tn * 4
    return pl.pallas_call(
        functools.partial(_mm_scale_kernel, w_t=w_t),
        grid=(M // tm, N // tn),
        in_specs=[pl.BlockSpec((tm, K), lambda i, j: (i, 0)),
                  _weight_spec(w, K, tn, w_block, w_t),
                  pl.BlockSpec((1, tn), lambda i, j: (0, j))],
        out_specs=pl.BlockSpec((tm, tn), lambda i, j: (i, j)),
        out_shape=jax.ShapeDtypeStruct((M, N), out_dtype),
        compiler_params=_cparams(("parallel", "parallel"), vmem),
        name="matmul_scaled",
    )(x, w, colscale)


CONV_SUB = 256
FFN_SUB = 512


def _causal_conv(g, gh, cw_ref, g_scr, kc, tm, cols):
    g_scr[0:HALO, cols] = gh
    g_scr[HALO:HALO + tm, cols] = g
    y = cw_ref[kc - 1:kc, cols] * g
    for j in range(kc - 1):
        off = HALO - (kc - 1) + j
        y = y + cw_ref[j:j + 1, cols] * g_scr[off:off + tm, cols]
    return y


def _mm_conv_silu_kernel(x_ref, xh_ref, w_ref, cw_ref, o_ref, g_scr, *, kc, tm, w_t):
    first = pl.program_id(0) == 0
    for c0 in range(0, o_ref.shape[1], CONV_SUB):
        cols = slice(c0, c0 + CONV_SUB)
        if w_t:
            w = w_ref[cols, :].astype(BF16)
            mm = lambda a: lax.dot_general(a, w, _NT, preferred_element_type=F32)
        else:
            w = w_ref[:, cols].astype(BF16)
            mm = lambda a: jnp.dot(a, w, preferred_element_type=F32)
        gh = mm(xh_ref[...])
        gh = jnp.where(first, 0.0, gh)
        g = mm(x_ref[...])
        y = _causal_conv(g, gh, cw_ref, g_scr, kc, tm, cols)
        o_ref[:, cols] = _silu(y).astype(o_ref.dtype)


def _proj_conv_silu(x, w, conv_w, tm, tn, w_block=None, w_t=False):
    M, K = x.shape
    kc, N = conv_w.shape
    tm, tn = min(tm, M), min(tn, N)
    hb = tm // HALO
    wsz = jnp.dtype(w.dtype).itemsize
    vmem = 2 * (tm * K * 2 + HALO * K * 2 + K * tn * wsz + tm * tn * 2) + K * tn * 2 + 4 * tm * tn * 4
    return pl.pallas_call(
        functools.partial(_mm_conv_silu_kernel, kc=kc, tm=tm, w_t=w_t),
        grid=(M // tm, N // tn),
        in_specs=[pl.BlockSpec((tm, K), lambda i, j: (i, 0)),
                  pl.BlockSpec((HALO, K), lambda i, j: (jnp.maximum(i * hb - 1, 0), 0)),
                  _weight_spec(w, K, tn, w_block, w_t),
                  pl.BlockSpec((kc, tn), lambda i, j: (0, j))],
        out_specs=pl.BlockSpec((tm, tn), lambda i, j: (i, j)),
        out_shape=jax.ShapeDtypeStruct((M, N), BF16),
        scratch_shapes=[pltpu.VMEM((tm + HALO, tn), F32)],
        compiler_params=_cparams(("parallel", "parallel"), vmem),
        name="proj_conv_silu",
    )(x, x, w, conv_w)


def _ffn_up_kernel(x_ref, xh_ref, wg_ref, wu_ref, cw_ref, o_ref, g_scr, *, kc, tm):
    first = pl.program_id(0) == 0
    for c0 in range(0, o_ref.shape[1], FFN_SUB):
        cols = slice(c0, c0 + FFN_SUB)
        wg = wg_ref[:, cols].astype(BF16)
        g = jnp.dot(x_ref[...], wg, preferred_element_type=F32)
        gh = jnp.dot(xh_ref[...], wg, preferred_element_type=F32)
        gh = jnp.where(first, 0.0, gh)
        u = jnp.dot(x_ref[...], wu_ref[:, cols].astype(BF16), preferred_element_type=F32)
        y = _causal_conv(g, gh, cw_ref, g_scr, kc, tm, cols)
        o_ref[:, cols] = (_silu(y) * u).astype(o_ref.dtype)


def _ffn_up(x, w_gate, w_up, conv_w, layer, tm, tn):
    M, K = x.shape
    N = w_gate.shape[2]
    kc = conv_w.shape[0]
    tm, tn = min(tm, M), min(tn, N)
    hb = tm // HALO
    wsz = jnp.dtype(w_gate.dtype).itemsize
    vmem = 2 * (tm * K * 2 + HALO * K * 2 + 2 * K * tn * wsz + tm * tn * 2) + 2 * K * tn * 2 + 6 * tm * tn * 4
    return pl.pallas_call(
        functools.partial(_ffn_up_kernel, kc=kc, tm=tm),
        grid=(M // tm, N // tn),
        in_specs=[pl.BlockSpec((tm, K), lambda i, j: (i, 0)),
                  pl.BlockSpec((HALO, K), lambda i, j: (jnp.maximum(i * hb - 1, 0), 0)),
                  pl.BlockSpec((None, K, tn), lambda i, j: (layer, 0, j)),
                  pl.BlockSpec((None, K, tn), lambda i, j: (layer, 0, j)),
                  pl.BlockSpec((kc, tn), lambda i, j: (0, j))],
        out_specs=pl.BlockSpec((tm, tn), lambda i, j: (i, j)),
        out_shape=jax.ShapeDtypeStruct((M, N), BF16),
        scratch_shapes=[pltpu.VMEM((tm + HALO, tn), F32)],
        compiler_params=_cparams(("parallel", "parallel"), vmem),
        name="ffn_up",
    )(x, x, w_gate, w_up, conv_w)


def _mm_res_ln_kernel(a_ref, w_ref, res_ref, g_ref, b_ref, of_ref, ob_ref, *, sub):
    for r0 in range(0, a_ref.shape[0], sub):
        rows = slice(r0, r0 + sub)
        acc = jnp.dot(a_ref[rows, :], w_ref[...], preferred_element_type=F32)
        y = DN_ALPHA * res_ref[rows, :] + acc
        mu = jnp.mean(y, axis=-1, keepdims=True)
        yc = y - mu
        var = jnp.mean(yc * yc, axis=-1, keepdims=True)
        out = yc * lax.rsqrt(var + LN_EPS) * g_ref[...] + b_ref[...]
        of_ref[rows, :] = out
        ob_ref[rows, :] = out.astype(BF16)


def _proj_res_ln(a, w, layer, res, g, b, tm, sub):
    M, K = a.shape
    N = w.shape[2]
    tm = min(tm, M)
    sub = min(sub, tm)
    vmem = K * N * 2 + 2 * (tm * K * 2 + tm * N * 4 + tm * N * 4 + tm * N * 2) + 4 * sub * N * 4
    return pl.pallas_call(
        functools.partial(_mm_res_ln_kernel, sub=sub),
        grid=(M // tm,),
        in_specs=[pl.BlockSpec((tm, K), lambda i: (i, 0)),
                  pl.BlockSpec((None, K, N), lambda i: (layer, 0, 0), pipeline_mode=pl.Buffered(1)),
                  pl.BlockSpec((tm, N), lambda i: (i, 0)),
                  pl.BlockSpec((1, N), lambda i: (0, 0)),
                  pl.BlockSpec((1, N), lambda i: (0, 0))],
        out_specs=[pl.BlockSpec((tm, N), lambda i: (i, 0)),
                   pl.BlockSpec((tm, N), lambda i: (i, 0))],
        out_shape=[jax.ShapeDtypeStruct((M, N), F32), jax.ShapeDtypeStruct((M, N), BF16)],
        compiler_params=_cparams(("parallel",), vmem),
        name="proj_res_ln",
    )(a, w, res, g.reshape(1, N), b.reshape(1, N))


def _ple_kernel(xb_ref, wg_ref, p_ref, wp_ref, xr_ref, of_ref, ob_ref):
    gate = jax.nn.sigmoid(jnp.dot(xb_ref[...], wg_ref[...].astype(BF16), preferred_element_type=F32))
    pe = jnp.dot(p_ref[...].astype(BF16), wp_ref[...].astype(BF16), preferred_element_type=F32)
    out = xr_ref[...] + gate * pe
    of_ref[...] = out
    ob_ref[...] = out.astype(BF16)


def _ple(xb, xf, w_gate, p, w_proj, layer, tm, tn):
    M, K = xb.shape
    N = w_gate.shape[2]
    P = p.shape[3]
    tm, tn = min(tm, M), min(tn, N)
    wsz = jnp.dtype(w_gate.dtype).itemsize
    vmem = 2 * (tm * K * 2 + K * tn * wsz + tm * P * 4 + P * tn * 4 + tm * tn * 10) + K * tn * 2 + 4 * tm * tn * 4
    return pl.pallas_call(
        _ple_kernel,
        grid=(M // tm, N // tn),
        in_specs=[pl.BlockSpec((tm, K), lambda i, j: (i, 0)),
                  pl.BlockSpec((None, K, tn), lambda i, j: (layer, 0, j)),
                  pl.BlockSpec((None, None, tm, P), lambda i, j: (layer, 0, i, 0)),
                  pl.BlockSpec((None, P, tn), lambda i, j: (layer, 0, j)),
                  pl.BlockSpec((tm, tn), lambda i, j: (i, j))],
        out_specs=[pl.BlockSpec((tm, tn), lambda i, j: (i, j)),
                   pl.BlockSpec((tm, tn), lambda i, j: (i, j))],
        out_shape=[jax.ShapeDtypeStruct((M, N), F32), jax.ShapeDtypeStruct((M, N), BF16)],
        compiler_params=_cparams(("parallel", "parallel"), vmem),
        name="ple",
    )(xb, w_gate, p, w_proj, xf)


GDN_HB = 8
GDN_NC = 4
GDN_STAGE_UNITS = 32


def _gdn_kernel(q_ref, k_ref, v_ref, z_ref, sc_ref, hp_ref, ng_ref, o_ref, s_scr):
    C = GDN_CHUNK

    @pl.when(pl.program_id(1) == 0)
    def _():
        s_scr[...] = jnp.zeros_like(s_scr)

    row = lax.broadcasted_iota(I32, (C, C), 0)
    col = lax.broadcasted_iota(I32, (C, C), 1)
    tri = row >= col
    strict = row > col
    eye = row == col
    tri_f = tri.astype(F32)
    eye_f = eye.astype(F32)

    raw = sc_ref[...]
    a_log = hp_ref[0, 0:1, :]
    dt_b = hp_ref[0, 1:2, :]
    xs = raw + dt_b
    softplus = jnp.maximum(xs, 0.0) + jnp.log1p(jnp.exp(-jnp.abs(xs)))
    g_all = -jnp.exp(a_log) * softplus
    beta_all = jax.nn.sigmoid(raw)
    ng = ng_ref[...]

    units = [(c, j) for c in range(GDN_NC) for j in range(GDN_HB)]
    kb_l, rhs_l, decay_l, qd_l, kd_l, kbf_l, qbf_l, gl_l = [], [], [], [], [], [], [], []
    for c in range(GDN_NC):
        r0 = c * C
        gc = jnp.dot(tri_f, g_all[r0:r0 + C, :], precision=lax.Precision.HIGHEST,
                     preferred_element_type=F32)
        g_last = gc[C - 1:C, :]
        e_gc = jnp.exp(gc)
        e_rest = jnp.exp(g_last - gc)
        e_last = jnp.exp(g_last)
        beta_c = beta_all[r0:r0 + C, :]
        qn, kn = [], []
        for hq in range(GDN_HB // 2):
            qf = q_ref[r0:r0 + C, hq * GDN_DK:(hq + 1) * GDN_DK].astype(F32)
            kf = k_ref[r0:r0 + C, hq * GDN_DK:(hq + 1) * GDN_DK].astype(F32)
            qn.append(qf * lax.rsqrt(jnp.sum(qf * qf, axis=-1, keepdims=True) + RMS_EPS) * (GDN_DK ** -0.5))
            kn.append(kf * lax.rsqrt(jnp.sum(kf * kf, axis=-1, keepdims=True) + RMS_EPS))
        for j in range(GDN_HB):
            q_h, k_h = qn[j // 2], kn[j // 2]
            vf = v_ref[r0:r0 + C, j * GDN_DV:(j + 1) * GDN_DV].astype(F32)
            beta = beta_c[:, GDN_HB + j:GDN_HB + j + 1]
            kb = k_h * beta
            gcb = jnp.broadcast_to(gc[:, j:j + 1], (C, C))
            gcr = jnp.sum(jnp.where(eye, gcb, 0.0), axis=0, keepdims=True)
            decay_l.append(jnp.where(tri, jnp.exp(jnp.where(tri, gcb - gcr, 0.0)), 0.0))
            kb_l.append(kb.astype(BF16))
            rhs_l.append(jnp.concatenate([vf * beta, kb * e_gc[:, j:j + 1]], axis=-1).astype(BF16))
            qd_l.append((q_h * e_gc[:, j:j + 1]).astype(BF16))
            kd_l.append((k_h * e_rest[:, j:j + 1]).astype(BF16))
            kbf_l.append(k_h.astype(BF16))
            qbf_l.append(q_h.astype(BF16))
            gl_l.append(e_last[:, j:j + 1])

    n_u = len(units)
    qk_l, sol_l = [], []
    for b0 in range(0, n_u, GDN_STAGE_UNITS):
        us = range(b0, min(b0 + GDN_STAGE_UNITS, n_u))
        kk_b = [lax.dot_general(kb_l[u], kbf_l[u], _NT, preferred_element_type=F32) for u in us]
        qk_b = [lax.dot_general(qbf_l[u], kbf_l[u], _NT, preferred_element_type=F32) for u in us]
        qk_l += [jnp.where(tri, qk * decay_l[u], 0.0).astype(BF16) for qk, u in zip(qk_b, us)]
        x_b = [(-jnp.where(strict, kk * decay_l[u], 0.0)) for kk, u in zip(kk_b, us)]
        t_b = [eye_f + x for x in x_b]
        x_b = [x.astype(BF16) for x in x_b]
        for _ in range(5):
            x_b = [jnp.dot(x, x, preferred_element_type=F32).astype(BF16) for x in x_b]
            t_b = [t + jnp.dot(t.astype(BF16), x, preferred_element_type=F32) for t, x in zip(t_b, x_b)]
        sol_l += [jnp.dot(t.astype(BF16), rhs_l[u], preferred_element_type=F32) for t, u in zip(t_b, us)]

    s_cur = [s_scr[j] for j in range(GDN_HB)]
    for c in range(GDN_NC):
        r0 = c * C
        us = [c * GDN_HB + j for j in range(GDN_HB)]
        s_bf = [s.astype(BF16) for s in s_cur]
        ws_l = [jnp.dot(sol_l[u][:, GDN_DV:].astype(BF16), s_bf[j], preferred_element_type=F32)
                for j, u in enumerate(us)]
        qs_l = [jnp.dot(qd_l[u], s_bf[j], preferred_element_type=F32) for j, u in enumerate(us)]
        vn_l = [(sol_l[u][:, :GDN_DV] - ws_l[j]).astype(BF16) for j, u in enumerate(us)]
        kv_l = [lax.dot_general(kd_l[u], vn_l[j], _TN, preferred_element_type=F32) for j, u in enumerate(us)]
        ov_l = [jnp.dot(qk_l[u], vn_l[j], preferred_element_type=F32) for j, u in enumerate(us)]
        s_cur = [s_cur[j] * gl_l[u] + kv_l[j] for j, u in enumerate(us)]
        for j in range(GDN_HB):
            o = qs_l[j] + ov_l[j]
            zf = z_ref[r0:r0 + C, j * GDN_DV:(j + 1) * GDN_DV].astype(F32)
            o = o * lax.rsqrt(jnp.mean(o * o, axis=-1, keepdims=True) + RMS_EPS) * ng * _silu(zf)
            o_ref[r0:r0 + C, j * GDN_DV:(j + 1) * GDN_DV] = o.astype(o_ref.dtype)
    for j in range(GDN_HB):
        s_scr[j] = s_cur[j]


def _gdn_core(qkv, z, scal, hparams, norm_g):
    L = qkv.shape[0]
    G = GDN_V_HEADS // GDN_HB
    R = GDN_NC * GDN_CHUNK
    qw = (GDN_HB // 2) * GDN_DK
    vw = GDN_HB * GDN_DV
    k_blk0 = GDN_QK_W // qw
    v_blk0 = 2 * GDN_QK_W // vw
    vmem = 2 * (2 * R * qw * 2 + 2 * R * vw * 2 + R * LANES * 4 + R * vw * 2) + (16 << 20)
    return pl.pallas_call(
        _gdn_kernel,
        grid=(G, L // R),
        in_specs=[pl.BlockSpec((R, qw), lambda g, s: (s, g)),
                  pl.BlockSpec((R, qw), lambda g, s: (s, k_blk0 + g)),
                  pl.BlockSpec((R, vw), lambda g, s: (s, v_blk0 + g)),
                  pl.BlockSpec((R, vw), lambda g, s: (s, g)),
                  pl.BlockSpec((R, LANES), lambda g, s: (s, g)),
                  pl.BlockSpec((1, 8, LANES), lambda g, s: (g, 0, 0)),
                  pl.BlockSpec((1, GDN_DV), lambda g, s: (0, 0))],
        out_specs=pl.BlockSpec((R, vw), lambda g, s: (s, g)),
        out_shape=jax.ShapeDtypeStruct((L, GDN_V_W), BF16),
        scratch_shapes=[pltpu.VMEM((GDN_HB, GDN_DK, GDN_DV), F32)],
        compiler_params=_cparams(("parallel", "arbitrary"), vmem),
        name="gdn_core",
    )(qkv, qkv, qkv, z, scal, hparams, norm_g.reshape(1, GDN_DV))


def _gdn_mixer(xb, w_in_all, layer, conv_w, a_log, dt_bias, norm_g):
    nqkv = 2 * GDN_QK_W + GDN_V_W
    tn = TILE_N
    w_ab = w_in_all[layer, :, nqkv + GDN_V_W:]
    G = GDN_V_HEADS // GDN_HB
    w_a = w_ab[:, :GDN_V_HEADS].reshape(D_MODEL, G, GDN_HB)
    w_b = w_ab[:, GDN_V_HEADS:].reshape(D_MODEL, G, GDN_HB)
    w_sc = jnp.concatenate([w_a, w_b, jnp.zeros((D_MODEL, G, LANES - 2 * GDN_HB), F32)], axis=-1)
    w_sc = w_sc.reshape(D_MODEL, G * LANES).astype(BF16)
    w_t_all = jnp.swapaxes(w_in_all, 1, 2)
    qkv = _proj_conv_silu(xb, w_t_all, conv_w, tm=TILE_M, tn=tn, w_block=lambda j: (layer, 0, j), w_t=True)
    ones_z = jnp.ones((1, GDN_V_W), F32)
    z = _matmul_scaled(xb, w_t_all, ones_z, BF16, tm=TILE_M, tn=tn,
                       w_block=lambda j: (layer, 0, nqkv // tn + j), w_t=True)
    scal = _matmul_scaled(xb, w_sc, jnp.ones((1, G * LANES), F32), F32, tm=TILE_M, tn=G * LANES)
    hp = jnp.zeros((G, 8, LANES), F32)
    hp = hp.at[:, 0, :GDN_HB].set(a_log.reshape(G, GDN_HB))
    hp = hp.at[:, 1, :GDN_HB].set(dt_bias.reshape(G, GDN_HB))
    return _gdn_core(qkv, z, scal, hp, norm_g)


def _dsa_small_kernel(x_ref, w_ref, g_ref, b_ref, ki_ref, wi_ref):
    acc = jnp.dot(x_ref[...], w_ref[...], preferred_element_type=F32)
    ki = acc[:, :IDX_DIM]
    mu = jnp.mean(ki, axis=-1, keepdims=True)
    kc = ki - mu
    var = jnp.mean(kc * kc, axis=-1, keepdims=True)
    ki_ref[...] = (kc * lax.rsqrt(var + LN_EPS) * g_ref[...] + b_ref[...]).astype(ki_ref.dtype)
    wi_ref[...] = acc[:, IDX_DIM:] * ((IDX_HEADS ** -0.5) * (IDX_DIM ** -0.5))


def _dsa_small(xb, w_small, ln_g, ln_b, tm):
    M, K = xb.shape
    tm = min(tm, M)
    N = 2 * LANES
    vmem = 2 * (tm * K * 2 + K * N * 2 + tm * LANES * 6) + 4 * tm * N * 4
    return pl.pallas_call(
        _dsa_small_kernel,
        grid=(M // tm,),
        in_specs=[pl.BlockSpec((tm, K), lambda i: (i, 0)),
                  pl.BlockSpec((K, N), lambda i: (0, 0)),
                  pl.BlockSpec((1, IDX_DIM), lambda i: (0, 0)),
                  pl.BlockSpec((1, IDX_DIM), lambda i: (0, 0))],
        out_specs=[pl.BlockSpec((tm, IDX_DIM), lambda i: (i, 0)),
                   pl.BlockSpec((tm, LANES), lambda i: (i, 0))],
        out_shape=[jax.ShapeDtypeStruct((M, IDX_DIM), BF16), jax.ShapeDtypeStruct((M, LANES), F32)],
        compiler_params=_cparams(("parallel",), vmem),
        name="dsa_idx_proj",
    )(xb, w_small, ln_g.reshape(1, IDX_DIM), ln_b.reshape(1, IDX_DIM))


def _sortable_key(score):
    bits = lax.bitcast_convert_type(score, I32)
    return jnp.where(bits >= 0, bits, bits ^ jnp.int32(0x7FFFFFFF))


def _idx_kernel(qi_tab, kj_tab, last_tab,
                qidx_ref, kidx_ref, wi_ref, far_ref, near_ref, key_scr, plane_scr, cand_scr, w_scr,
                *, k_top, n_sub_total):
    s = pl.program_id(0)
    i = qi_tab[s]
    j = kj_tab[s]

    @pl.when(s == 0)
    def _():
        key_scr[...] = jnp.full_like(key_scr, INT_MIN)
        plane_scr[...] = jnp.zeros_like(plane_scr)
    nsub = IDX_KEY_BLOCK // IDX_SUB
    t_col = i * Q_BLOCK + lax.broadcasted_iota(I32, (Q_BLOCK, 1), 0)
    lane_sub = lax.broadcasted_iota(I32, (Q_BLOCK, IDX_SUB), 1)
    wi = wi_ref[...]

    qi_rows = jnp.concatenate([qidx_ref[:, h * IDX_DIM:(h + 1) * IDX_DIM] for h in range(IDX_HEADS)], axis=0)
    for sub in range(nsub):
        ki_sub = kidx_ref[sub * IDX_SUB:(sub + 1) * IDX_SUB, :]
        sc_all = lax.dot_general(qi_rows, ki_sub, _NT, preferred_element_type=F32)
        acc = jnp.zeros((Q_BLOCK, IDX_SUB), F32)
        for h in range(IDX_HEADS):
            acc = acc + jnp.maximum(sc_all[h * Q_BLOCK:(h + 1) * Q_BLOCK, :], 0.0) * wi[:, h:h + 1]
        s_idx = j * IDX_KEY_BLOCK + sub * IDX_SUB + lane_sub
        key_scr[j * nsub + sub] = jnp.where(s_idx <= t_col, _sortable_key(acc), INT_MIN)

    @pl.when(last_tab[s] == 1)
    def _():
        n_chunks = (i * Q_BLOCK + Q_BLOCK - 1) // IDX_SUB + 1

        def count(pred, ref_val):
            refb = jnp.broadcast_to(ref_val, (Q_BLOCK, LANES))

            def body(c, cnt):
                blk = key_scr[c]
                for l in range(IDX_SUB // LANES):
                    cnt = cnt + jnp.where(pred(blk[:, l * LANES:(l + 1) * LANES], refb), 1, 0)
                return cnt

            cnt = lax.fori_loop(0, n_chunks, body, jnp.zeros((Q_BLOCK, LANES), I32))
            return jnp.sum(cnt, axis=1, keepdims=True)

        tiles_per_chunk = IDX_SUB // LANES
        chunks_per_group = WORD_BITS // tiles_per_chunk
        n_tiles = n_chunks * tiles_per_chunk
        n_groups = (n_tiles + WORD_BITS - 1) // WORD_BITS

        def build_group(g, carry):
            def build_rows(r, carry_r):
                r8 = pl.multiple_of(r * 8, 8)
                a = [key_scr[g * chunks_per_group + t // tiles_per_chunk, pl.ds(r8, 8),
                             (t % tiles_per_chunk) * LANES:(t % tiles_per_chunk + 1) * LANES]
                     for t in range(WORD_BITS)]
                m, sh = 0x0000FFFF, 16
                while sh:
                    k = 0
                    while k < WORD_BITS:
                        x = (a[k] ^ lax.shift_right_logical(a[k + sh], jnp.int32(sh))) & jnp.int32(m)
                        a[k] = a[k] ^ x
                        a[k + sh] = a[k + sh] ^ jnp.left_shift(x, jnp.int32(sh))
                        k = (k + sh + 1) & ~sh
                    sh >>= 1
                    m = (m ^ (m << sh)) & 0xFFFFFFFF if sh else m
                a[0] = ~a[0]
                for p in range(WORD_BITS):
                    plane_scr[p, g, pl.ds(r8, 8), :] = a[p]
                return carry_r

            lax.fori_loop(0, Q_BLOCK // 8, build_rows, 0)
            return carry

        lax.fori_loop(0, n_groups, build_group, 0)
        n_groups_max = plane_scr.shape[1]
        for g in range(n_groups_max):
            n_valid = jnp.clip(n_tiles - g * WORD_BITS, 0, WORD_BITS)
            word = jnp.where(n_valid >= WORD_BITS, jnp.int32(-1),
                             jnp.where(n_valid <= 0, jnp.int32(0),
                                       jnp.left_shift(jnp.int32(-1), WORD_BITS - n_valid)))
            cand_scr[g] = jnp.broadcast_to(word, (Q_BLOCK, LANES))

        def plane_pass(p, thr_u, n_above, n_cand):
            cnt = jnp.zeros((Q_BLOCK, LANES), I32)
            for g in range(n_groups_max):
                w = plane_scr[p, g] & cand_scr[g]
                w_scr[g] = w
                cnt = cnt + lax.population_count(w)
            n_set = jnp.sum(cnt, axis=1, keepdims=True)
            take = n_above + n_set >= k_top
            take_b = jnp.broadcast_to(take, (Q_BLOCK, LANES))
            for g in range(n_groups_max):
                w = w_scr[g]
                cand_scr[g] = jnp.where(take_b, w, cand_scr[g] ^ w)
            bit = jnp.left_shift(jnp.int32(1), WORD_BITS - 1 - p)
            return (jnp.where(take, thr_u | bit, thr_u), jnp.where(take, n_above, n_above + n_set),
                    jnp.where(take, n_set, n_cand - n_set))

        def group_body(carry):
            grp, thr_u, n_above, n_cand, _ = carry
            for bb in range(BITS_PER_CHECK):
                thr_u, n_above, n_cand = plane_pass(grp * BITS_PER_CHECK + bb, thr_u, n_above, n_cand)
            return grp + 1, thr_u, n_above, n_cand, jnp.max(jnp.where(n_above + n_cand != k_top, 1, 0))

        _, thr_u, n_above, n_cand, n_tied_rows = lax.while_loop(
            lambda carry: (carry[0] < WORD_BITS // BITS_PER_CHECK) & (carry[4] != 0), group_body,
            (jnp.int32(0), jnp.zeros((Q_BLOCK, 1), I32), jnp.zeros((Q_BLOCK, 1), I32),
             jnp.broadcast_to(n_tiles * LANES, (Q_BLOCK, 1)).astype(I32), jnp.int32(1)))
        thr = thr_u ^ jnp.int32(INT_MIN)
        thr_b = jnp.broadcast_to(thr, (Q_BLOCK, IDX_SUB))

        def emit(c, sel):
            s_idx = c * IDX_SUB + lane_sub
            far = sel & (t_col - s_idx >= MAX_DISTANCE)
            far_ref[0, c] = jnp.where(far, 0.0, FAR_MASKED).astype(far_ref.dtype)
            key_scr[c] = jnp.where(sel, 1, 0)

        @pl.when(n_tied_rows == 0)
        def _():
            def sel_body(c, carry):
                emit(c, key_scr[c] >= thr_b)
                return carry

            lax.fori_loop(0, n_chunks, sel_body, 0)

        @pl.when(n_tied_rows != 0)
        def _():
            n_gt = count(lambda a, r: a > r, thr)
            need_eq = (k_top - n_gt).astype(F32)
            incl = (lax.broadcasted_iota(I32, (IDX_SUB, IDX_SUB), 0)
                    <= lax.broadcasted_iota(I32, (IDX_SUB, IDX_SUB), 1)).astype(BF16)

            def sel_body(c, carry):
                blk = key_scr[c]
                eq = blk == thr_b
                eq_f = jnp.where(eq, 1.0, 0.0)
                rank = carry + jnp.dot(eq_f.astype(BF16), incl, preferred_element_type=F32)
                s_idx = c * IDX_SUB + lane_sub
                emit(c, ((blk > thr_b) | (eq & (rank <= need_eq))) & (s_idx <= t_col))
                return carry + jnp.sum(eq_f, axis=1, keepdims=True)

            lax.fori_loop(0, n_chunks, sel_body, jnp.zeros((Q_BLOCK, 1), F32))

        def fill_body(c, carry):
            far_ref[0, c] = jnp.full((Q_BLOCK, IDX_SUB), FAR_MASKED, far_ref.dtype)
            return carry

        lax.fori_loop(n_chunks, n_sub_total, fill_body, 0)

        def window(blk_idx):
            per = IDX_SUB // Q_BLOCK
            chunk = key_scr[blk_idx // per]
            m = blk_idx % per
            out = chunk[:, 0:Q_BLOCK]
            for q in range(1, per):
                out = jnp.where(m == q, chunk[:, q * Q_BLOCK:(q + 1) * Q_BLOCK], out)
            return out

        r_i = lax.broadcasted_iota(I32, (Q_BLOCK, Q_BLOCK), 0)
        c_i = lax.broadcasted_iota(I32, (Q_BLOCK, Q_BLOCK), 1)
        d_lo = Q_BLOCK + r_i - c_i
        d_hi = r_i - c_i
        near_lo = (window(jnp.maximum(i - 1, 0)) != 0) & (d_lo < MAX_DISTANCE) & (i >= 1)
        near_hi = (window(i) != 0) & (d_hi >= 0) & (d_hi < MAX_DISTANCE)
        near_ref[:, 0:Q_BLOCK] = jnp.where(near_lo, 0.0, -jnp.inf).astype(near_ref.dtype)
        near_ref[:, Q_BLOCK:2 * Q_BLOCK] = jnp.where(near_hi, 0.0, -jnp.inf).astype(near_ref.dtype)


def _idx_select(proj, ki, wi, k_top):
    L = ki.shape[0]
    nq = L // Q_BLOCK
    n_sub_total = L // IDX_SUB
    qi_l, kj_l, last_l = [], [], []
    for i in range(nq):
        j_last = (i * Q_BLOCK + Q_BLOCK - 1) // IDX_KEY_BLOCK
        for j in range(j_last + 1):
            qi_l.append(i)
            kj_l.append(j)
            last_l.append(1 if j == j_last else 0)
    tabs = [jnp.asarray(np.asarray(t, np.int32)) for t in (qi_l, kj_l, last_l)]
    qcol = (DSA_HEADS * DSA_DH) // (IDX_HEADS * IDX_DIM)
    group_keys = WORD_BITS * LANES
    n_groups_max = -(-L // group_keys)
    key_chunks = n_groups_max * (group_keys // IDX_SUB)
    vmem = ((key_chunks * IDX_SUB + (WORD_BITS + 2) * n_groups_max * LANES) * Q_BLOCK * 4
            + 2 * (Q_BLOCK * L * 2) + 2 * (Q_BLOCK * IDX_HEADS * IDX_DIM * 2)
            + 2 * IDX_KEY_BLOCK * IDX_DIM * 2 + (8 << 20))
    grid_spec = pltpu.PrefetchScalarGridSpec(
        num_scalar_prefetch=3,
        grid=(len(qi_l),),
        in_specs=[pl.BlockSpec((Q_BLOCK, IDX_HEADS * IDX_DIM), lambda s, qt, kt, lt: (qt[s], qcol)),
                  pl.BlockSpec((IDX_KEY_BLOCK, IDX_DIM), lambda s, qt, kt, lt: (kt[s], 0)),
                  pl.BlockSpec((Q_BLOCK, LANES), lambda s, qt, kt, lt: (qt[s], 0))],
        out_specs=[pl.BlockSpec((1, n_sub_total, Q_BLOCK, IDX_SUB), lambda s, qt, kt, lt: (qt[s], 0, 0, 0)),
                   pl.BlockSpec((Q_BLOCK, 2 * Q_BLOCK), lambda s, qt, kt, lt: (qt[s], 0))],
        scratch_shapes=[pltpu.VMEM((key_chunks, Q_BLOCK, IDX_SUB), I32),
                        pltpu.VMEM((WORD_BITS, n_groups_max, Q_BLOCK, LANES), I32),
                        pltpu.VMEM((n_groups_max, Q_BLOCK, LANES), I32),
                        pltpu.VMEM((n_groups_max, Q_BLOCK, LANES), I32)],
    )
    return pl.pallas_call(
        functools.partial(_idx_kernel, k_top=k_top, n_sub_total=n_sub_total),
        grid_spec=grid_spec,
        out_shape=[jax.ShapeDtypeStruct((nq, n_sub_total, Q_BLOCK, IDX_SUB), BF16),
                   jax.ShapeDtypeStruct((L, 2 * Q_BLOCK), BF16)],
        compiler_params=_cparams(("arbitrary",), vmem),
        name="dsa_idx_select",
    )(*tabs, proj, ki, wi)


M_INIT = -1e30
FAR_SUBS = 2
LOG2E = math.log2(math.e)


def _attn_kernel(qi_tab, kj_tab, kind_tab, first_tab,
                 tab_ref, q_ref, kt_ref, vf_ref, klo_ref, khi_ref, vlo_ref, vhi_ref, far_ref, near_ref,
                 o_ref, m_scr, l_scr, acc_scr, b_scr, s_scr):
    s = pl.program_id(0)

    @pl.when(s == 0)
    def _():
        r_i = lax.broadcasted_iota(I32, (Q_BLOCK, 2 * Q_BLOCK), 0)
        c_i = lax.broadcasted_iota(I32, (Q_BLOCK, 2 * Q_BLOCK), 1)
        d = jnp.maximum(Q_BLOCK + r_i - c_i, 0)
        max_exact = N_BUCKETS // 2
        df = jnp.maximum(d, 1).astype(F32)
        large = max_exact + (jnp.log(df / max_exact) / math.log(MAX_DISTANCE / max_exact)
                             * (N_BUCKETS - max_exact)).astype(I32)
        large = jnp.minimum(large, N_BUCKETS - 1)
        bkt = jnp.where(d < max_exact, d, large)
        for h in range(DSA_HEADS):
            acc = jnp.zeros((Q_BLOCK, 2 * Q_BLOCK), F32)
            for b in range(N_BUCKETS):
                acc = jnp.where(bkt == b, (tab_ref[b, h] - tab_ref[N_BUCKETS - 1, h]) * LOG2E, acc)
            b_scr[h] = acc

    @pl.when(first_tab[s] == 1)
    def _():
        m_scr[...] = jnp.full_like(m_scr, M_INIT)
        l_scr[...] = jnp.zeros_like(l_scr)
        acc_scr[...] = jnp.zeros_like(acc_scr)

    def softmax_pv(v_ref, width, row0=0):
        nt = width // LANES
        rows = slice(row0, row0 + width)
        ones = jnp.ones((width, DSA_DH), BF16)
        for g in range(DSA_KV_HEADS):
            heads = range(g * DSA_GROUP, (g + 1) * DSA_GROUP)
            alphas, pbs = [], []
            for h in heads:
                tiles = [s_scr[h, :, t * LANES:(t + 1) * LANES] for t in range(nt)]
                tmax = tiles[0]
                for t in range(1, nt):
                    tmax = jnp.maximum(tmax, tiles[t])
                m_prev = m_scr[h]
                m_new = jnp.maximum(m_prev, jnp.max(tmax, axis=-1, keepdims=True))
                alphas.append(jnp.exp2(m_prev - m_new))
                p = [jnp.exp2(tiles[t] - m_new).astype(BF16) for t in range(nt)]
                pbs.append(jnp.concatenate(p, axis=-1) if nt > 1 else p[0])
                m_scr[h] = m_new
            v_aug = jnp.concatenate([v_ref[rows, g * DSA_DH:(g + 1) * DSA_DH], ones], axis=-1)
            pv = jnp.dot(jnp.concatenate(pbs, axis=0), v_aug,
                         preferred_element_type=F32)
            for n, h in enumerate(heads):
                pv_h = pv[n * Q_BLOCK:(n + 1) * Q_BLOCK]
                acc_scr[h] = alphas[n] * acc_scr[h] + pv_h[:, :DSA_DH]
                l_scr[h] = alphas[n] * l_scr[h] + pv_h[:, DSA_DH:]

    @pl.when(kind_tab[s] == 0)
    def _():
        eye = (lax.broadcasted_iota(I32, (Q_BLOCK, Q_BLOCK), 0)
               == lax.broadcasted_iota(I32, (Q_BLOCK, Q_BLOCK), 1)).astype(BF16)
        for sub in range(FAR_SUBS):
            cols = slice(sub * IDX_SUB, (sub + 1) * IDX_SUB)
            mask = far_ref[0, sub]
            for g in range(DSA_KV_HEADS):
                lhs = jnp.concatenate(
                    [jnp.concatenate([q_ref[:, h * DSA_DH:(h + 1) * DSA_DH], eye], axis=1)
                     for h in range(g * DSA_GROUP, (g + 1) * DSA_GROUP)], axis=0)
                rhs = jnp.concatenate([kt_ref[g * DSA_DH:(g + 1) * DSA_DH, cols], mask], axis=0)
                logits = jnp.dot(lhs, rhs, preferred_element_type=F32)
                for hh in range(DSA_GROUP):
                    s_scr[g * DSA_GROUP + hh] = logits[hh * Q_BLOCK:(hh + 1) * Q_BLOCK, :]
            softmax_pv(vf_ref, IDX_SUB, sub * IDX_SUB)

    @pl.when(kind_tab[s] == 1)
    def _():
        for half, (k_ref, v_ref) in enumerate(((klo_ref, vlo_ref), (khi_ref, vhi_ref))):
            cols = slice(half * Q_BLOCK, (half + 1) * Q_BLOCK)
            mask = near_ref[:, cols].astype(F32)
            for h in range(DSA_HEADS):
                g = h // DSA_GROUP
                logits = lax.dot_general(q_ref[:, h * DSA_DH:(h + 1) * DSA_DH],
                                         k_ref[:, g * DSA_DH:(g + 1) * DSA_DH], _NT, preferred_element_type=F32)
                s_scr[h, :, 0:Q_BLOCK] = logits + b_scr[h][:, cols] + mask
            softmax_pv(v_ref, Q_BLOCK)
        for h in range(DSA_HEADS):
            o_ref[:, h * DSA_DH:(h + 1) * DSA_DH] = (acc_scr[h] / l_scr[h]).astype(o_ref.dtype)


def _masked_attention(proj, far, near, rel_bias):
    L = proj.shape[0]
    nq = L // Q_BLOCK
    far_keys = FAR_SUBS * IDX_SUB
    per = far_keys // Q_BLOCK
    qi_l, kj_l, kind_l, first_l = [], [], [], []
    for i in range(nq):
        n_far = -(-i // per)
        for j in range(n_far):
            qi_l.append(i); kj_l.append(j); kind_l.append(0); first_l.append(1 if j == 0 else 0)
        qi_l.append(i); kj_l.append(max(n_far - 1, 0)); kind_l.append(1); first_l.append(1 if n_far == 0 else 0)
    tabs = [jnp.asarray(np.asarray(t, np.int32)) for t in (qi_l, kj_l, kind_l, first_l)]
    qw = DSA_HEADS * DSA_DH
    kvw = DSA_KV_HEADS * DSA_DH
    k_col = (2 * qw) // kvw
    v_col = k_col + 1
    hw = DSA_HEADS
    vmem = (2 * (Q_BLOCK * qw * 2 * 2 + 2 * far_keys * kvw * 2 + 4 * Q_BLOCK * kvw * 2
                 + Q_BLOCK * far_keys * 2 + Q_BLOCK * 2 * Q_BLOCK * 2)
            + hw * Q_BLOCK * (3 * LANES + 2 * Q_BLOCK) * 4 + (16 << 20))
    idx = lambda f: (lambda s, qt, kt, kd, ft: f(qt[s], kt[s]))
    k_t = proj[:, k_col * kvw:(k_col + 1) * kvw].T
    grid_spec = pltpu.PrefetchScalarGridSpec(
        num_scalar_prefetch=4,
        grid=(len(qi_l),),
        in_specs=[pl.BlockSpec(memory_space=pltpu.SMEM),
                  pl.BlockSpec((Q_BLOCK, qw), idx(lambda i, j: (i, 0))),
                  pl.BlockSpec((kvw, far_keys), idx(lambda i, j: (0, j))),
                  pl.BlockSpec((far_keys, kvw), idx(lambda i, j: (j, v_col))),
                  pl.BlockSpec((Q_BLOCK, kvw), idx(lambda i, j: (jnp.maximum(i - 1, 0), k_col))),
                  pl.BlockSpec((Q_BLOCK, kvw), idx(lambda i, j: (i, k_col))),
                  pl.BlockSpec((Q_BLOCK, kvw), idx(lambda i, j: (jnp.maximum(i - 1, 0), v_col))),
                  pl.BlockSpec((Q_BLOCK, kvw), idx(lambda i, j: (i, v_col))),
                  pl.BlockSpec((1, FAR_SUBS, Q_BLOCK, IDX_SUB), idx(lambda i, j: (i, j, 0, 0))),
                  pl.BlockSpec((Q_BLOCK, 2 * Q_BLOCK), idx(lambda i, j: (i, 0)))],
        out_specs=pl.BlockSpec((Q_BLOCK, qw), idx(lambda i, j: (i, 0))),
        scratch_shapes=[pltpu.VMEM((hw, Q_BLOCK, LANES), F32),
                        pltpu.VMEM((hw, Q_BLOCK, LANES), F32),
                        pltpu.VMEM((hw, Q_BLOCK, DSA_DH), F32),
                        pltpu.VMEM((hw, Q_BLOCK, 2 * Q_BLOCK), F32),
                        pltpu.VMEM((hw, Q_BLOCK, IDX_SUB), F32)],
    )
    return pl.pallas_call(
        _attn_kernel,
        grid_spec=grid_spec,
        out_shape=jax.ShapeDtypeStruct((L, qw), BF16),
        compiler_params=_cparams(("arbitrary",), vmem),
        name="dsa_attention",
    )(*tabs, rel_bias, proj, k_t, proj, proj, proj, proj, proj, far, near)


def _dsa_mixer(xb, w_in_all, layer, ln_g, ln_b, rel_bias):
    L = xb.shape[0]
    k_top = min(TOPK_MAX, L // 4)
    sq = DSA_HEADS * DSA_DH
    skv = DSA_KV_HEADS * DSA_DH
    si = IDX_HEADS * IDX_DIM
    w_ki = w_in_all[layer, :, sq + 2 * skv + si:sq + 2 * skv + si + IDX_DIM]
    w_wi = w_in_all[layer, :, sq + 2 * skv + si + IDX_DIM:]
    colscale = jnp.concatenate([jnp.full((1, sq), DSA_DH ** -0.5 * LOG2E, F32),
                                jnp.ones((1, si + 2 * skv), F32)], axis=1)
    w_small = jnp.concatenate([w_ki, w_wi, jnp.zeros((D_MODEL, LANES - IDX_HEADS), F32)], axis=1).astype(BF16)
    tn = 2 * skv
    nq, ni = sq // tn, si // tn
    src_block = lambda j: jnp.where(j < nq, j, jnp.where(j < nq + ni, j + 1, nq))
    proj = _matmul_scaled(xb, jnp.swapaxes(w_in_all, 1, 2), colscale, BF16, tm=TILE_M, tn=tn,
                          w_block=lambda j: (layer, 0, src_block(j)), w_t=True)
    ki, wi = _dsa_small(xb, w_small, ln_g, ln_b, tm=TILE_M)
    far, near = _idx_select(proj, ki, wi, k_top)
    return _masked_attention(proj, far, near, rel_bias)


def kernel(x, p, gdn_w_in, gdn_conv_w, gdn_a_log, gdn_dt_bias, gdn_norm_g, gdn_w_o, dsa_w_in, dsa_kidx_ln_g, dsa_kidx_ln_b, dsa_w_o, rel_bias, ln1_g, ln1_b, ffn_w_gate, ffn_w_up, ffn_conv_w, ffn_w_down, ln2_g, ln2_b, ple_w_proj, ple_w_gate):
    assert x.shape[0] == 1 and x.shape[2] == D_MODEL
    xf = x[0]
    xb = xf.astype(BF16)
    ple_gate_bf = ple_w_gate.astype(BF16)
    ffn_down_bf = ffn_w_down.astype(BF16)
    gdn_o_bf = gdn_w_o.astype(BF16)
    dsa_o_bf = dsa_w_o.astype(BF16)
    ia = ib = 0
    for i in range(DEPTH):
        if i % 2 == 0:
            mix = _gdn_mixer(xb, gdn_w_in, ia, gdn_conv_w[ia], gdn_a_log[ia], gdn_dt_bias[ia], gdn_norm_g[ia])
            w_o, lo = gdn_o_bf, ia
            ia += 1
        else:
            mix = _dsa_mixer(xb, dsa_w_in, ib, dsa_kidx_ln_g[ib], dsa_kidx_ln_b[ib], rel_bias)
            w_o, lo = dsa_o_bf, ib
            ib += 1
        xf, xb = _proj_res_ln(mix, w_o, lo, xf, ln1_g[i], ln1_b[i], tm=LN_TILE_M, sub=LN_SUB_M)
        hmid = _ffn_up(xb, ffn_w_gate, ffn_w_up, ffn_conv_w[i], i, tm=TILE_M, tn=FFN_TILE_N)
        xf, xb = _proj_res_ln(hmid, ffn_down_bf, i, xf, ln2_g[i], ln2_b[i], tm=LN_TILE_M, sub=LN_SUB_M)
        xf, xb = _ple(xb, xf, ple_gate_bf, p, ple_w_proj, i, tm=TILE_M, tn=TILE_N)
    return xf[None]
```

```python
import functools
import math

import jax
import jax.numpy as jnp
import numpy as np
from jax import lax
from jax.experimental import pallas as pl
from jax.experimental.pallas import tpu as pltpu

F32 = jnp.float32
BF16 = jnp.bfloat16
I32 = jnp.int32

D_MODEL = 2048
GDN_QK_HEADS = 16
GDN_V_HEADS = 32
GDN_DK = 128
GDN_DV = 128
GDN_CONV = 4
GDN_CHUNK = 64
GDN_QK_W = GDN_QK_HEADS * GDN_DK
GDN_V_W = GDN_V_HEADS * GDN_DV
DSA_HEADS = 16
DSA_KV_HEADS = 4
DSA_GROUP = DSA_HEADS // DSA_KV_HEADS
DSA_DH = 128
IDX_HEADS = 16
IDX_DIM = 128
TOPK_MAX = 256
N_BUCKETS = 32
MAX_DISTANCE = 128
D_FF = 5120
FFN_CONV = 3
PLE_DIM = 256
DEPTH = 2
DN_ALPHA = (2.0 * DEPTH) ** 0.25
LN_EPS = 1e-5
RMS_EPS = 1e-6

V7X_VMEM_BYTES = 64 * 1024 * 1024
V7X_VMEM_BUDGET = 56 * 1024 * 1024
LANES = 128
BF16_SUBLANES = 16

TILE_M = 1024
TILE_N = 1024
FFN_TILE_N = 512
LN_TILE_M = 512
LN_SUB_M = 256

HALO = BF16_SUBLANES
Q_BLOCK = 128
IDX_KEY_BLOCK = 2048
IDX_SUB = 512
INT_MIN = -(2 ** 31)
BITS_PER_CHECK = 4
WORD_BITS = 32
FAR_MASKED = -2e30

_NT = (((1,), (1,)), ((), ()))
_TN = (((0,), (0,)), ((), ()))


def _cparams(semantics, vmem_bytes):
    return pltpu.CompilerParams(dimension_semantics=semantics,
                                vmem_limit_bytes=int(min(V7X_VMEM_BUDGET, vmem_bytes)))


def _silu(y):
    return y * jax.nn.sigmoid(y)


def _mm_scale_kernel(x_ref, w_ref, cs_ref, o_ref, *, w_t):
    w = w_ref[...].astype(BF16)
    acc = (lax.dot_general(x_ref[...], w, _NT, preferred_element_type=F32) if w_t
           else jnp.dot(x_ref[...], w, preferred_element_type=F32))
    o_ref[...] = (acc * cs_ref[...]).astype(o_ref.dtype)


def _weight_spec(w, K, tn, w_block, w_t=False):
    if w_block is None:
        return pl.BlockSpec((K, tn), lambda i, j: (0, j))
    if w_t:
        return pl.BlockSpec((None, tn, K), lambda i, j: (w_block(j)[0], w_block(j)[2], 0))
    return pl.BlockSpec((None, K, tn), lambda i, j: w_block(j))


def _matmul_scaled(x, w, colscale, out_dtype, tm, tn, w_block=None, w_t=False):
    M, K = x.shape
    N = colscale.shape[1]
    tm, tn = min(tm, M), min(tn, N)
    osz = jnp.dtype(out_dtype).itemsize
    wsz = jnp.dtype(w.dtype).itemsize
    vmem = 2 * (tm * K * 2 + K * tn * wsz + tm * tn * osz) + K * tn * 2 + 2 * tm * tn * 4
    return pl.pallas_call(
        functools.partial(_mm_scale_kernel, w_t=w_t),
        grid=(M // tm, N // tn),
        in_specs=[pl.BlockSpec((tm, K), lambda i, j: (i, 0)),
                  _weight_spec(w, K, tn, w_block, w_t),
                  pl.BlockSpec((1, tn), lambda i, j: (0, j))],
        out_specs=pl.BlockSpec((tm, tn), lambda i, j: (i, j)),
        out_shape=jax.ShapeDtypeStruct((M, N), out_dtype),
        compiler_params=_cparams(("parallel", "parallel"), vmem),
        name="matmul_scaled",
    )(x, w, colscale)


CONV_SUB = 256
FFN_SUB = 512


def _causal_conv(g, cw_ref, g_scr, carry_scr, slot, kc, tm, cols):
    g_scr[0:HALO, cols] = carry_scr[slot]
    g_scr[HALO:HALO + tm, cols] = g
    carry_scr[slot] = g[tm - HALO:tm, :]
    y = cw_ref[kc - 1:kc, cols] * g
    for j in range(kc - 1):
        off = HALO - (kc - 1) + j
        y = y + cw_ref[j:j + 1, cols] * g_scr[off:off + tm, cols]
    return y


def _init_carry(carry_scr):
    @pl.when((pl.program_id(0) == 0) & (pl.program_id(1) == 0))
    def _():
        carry_scr[...] = jnp.zeros_like(carry_scr)


def _mm_conv_silu_kernel(x_ref, w_ref, cw_ref, o_ref, g_scr, carry_scr, *, kc, tm, w_t):
    _init_carry(carry_scr)
    nsub = o_ref.shape[1] // CONV_SUB
    for n in range(nsub):
        cols = slice(n * CONV_SUB, (n + 1) * CONV_SUB)
        if w_t:
            g = lax.dot_general(x_ref[...], w_ref[cols, :].astype(BF16), _NT, preferred_element_type=F32)
        else:
            g = jnp.dot(x_ref[...], w_ref[:, cols].astype(BF16), preferred_element_type=F32)
        y = _causal_conv(g, cw_ref, g_scr, carry_scr, pl.program_id(1) * nsub + n, kc, tm, cols)
        o_ref[:, cols] = _silu(y).astype(o_ref.dtype)


def _proj_conv_silu(x, w, conv_w, tm, tn, w_block=None, w_t=False):
    M, K = x.shape
    kc, N = conv_w.shape
    tm, tn = min(tm, M), min(tn, N)
    wsz = jnp.dtype(w.dtype).itemsize
    vmem = (2 * (tm * K * 2 + K * tn * wsz + tm * tn * 2) + K * tn * 2 + 4 * tm * tn * 4
            + HALO * N * 4)
    return pl.pallas_call(
        functools.partial(_mm_conv_silu_kernel, kc=kc, tm=tm, w_t=w_t),
        grid=(M // tm, N // tn),
        in_specs=[pl.BlockSpec((tm, K), lambda i, j: (i, 0)),
                  _weight_spec(w, K, tn, w_block, w_t),
                  pl.BlockSpec((kc, tn), lambda i, j: (0, j))],
        out_specs=pl.BlockSpec((tm, tn), lambda i, j: (i, j)),
        out_shape=jax.ShapeDtypeStruct((M, N), BF16),
        scratch_shapes=[pltpu.VMEM((tm + HALO, tn), F32),
                        pltpu.VMEM((N // CONV_SUB, HALO, CONV_SUB), F32)],
        compiler_params=_cparams(("arbitrary", "arbitrary"), vmem),
        name="proj_conv_silu",
    )(x, w, conv_w)


def _ffn_up_kernel(x_ref, wg_ref, wu_ref, cw_ref, o_ref, g_scr, carry_scr, *, kc, tm):
    _init_carry(carry_scr)
    nsub = o_ref.shape[1] // FFN_SUB
    for n in range(nsub):
        cols = slice(n * FFN_SUB, (n + 1) * FFN_SUB)
        g = jnp.dot(x_ref[...], wg_ref[:, cols].astype(BF16), preferred_element_type=F32)
        u = jnp.dot(x_ref[...], wu_ref[:, cols].astype(BF16), preferred_element_type=F32)
        y = _causal_conv(g, cw_ref, g_scr, carry_scr, pl.program_id(1) * nsub + n, kc, tm, cols)
        o_ref[:, cols] = (_silu(y) * u).astype(o_ref.dtype)


def _ffn_up(x, w_gate, w_up, conv_w, layer, tm, tn):
    M, K = x.shape
    N = w_gate.shape[2]
    kc = conv_w.shape[0]
    tm, tn = min(tm, M), min(tn, N)
    wsz = jnp.dtype(w_gate.dtype).itemsize
    vmem = (2 * (tm * K * 2 + 2 * K * tn * wsz + tm * tn * 2) + 2 * K * tn * 2 + 6 * tm * tn * 4
            + HALO * N * 4)
    return pl.pallas_call(
        functools.partial(_ffn_up_kernel, kc=kc, tm=tm),
        grid=(M // tm, N // tn),
        in_specs=[pl.BlockSpec((tm, K), lambda i, j: (i, 0)),
                  pl.BlockSpec((None, K, tn), lambda i, j: (layer, 0, j)),
                  pl.BlockSpec((None, K, tn), lambda i, j: (layer, 0, j)),
                  pl.BlockSpec((kc, tn), lambda i, j: (0, j))],
        out_specs=pl.BlockSpec((tm, tn), lambda i, j: (i, j)),
        out_shape=jax.ShapeDtypeStruct((M, N), BF16),
        scratch_shapes=[pltpu.VMEM((tm + HALO, tn), F32),
                        pltpu.VMEM((N // FFN_SUB, HALO, FFN_SUB), F32)],
        compiler_params=_cparams(("arbitrary", "arbitrary"), vmem),
        name="ffn_up",
    )(x, w_gate, w_up, conv_w)


def _mm_res_ln_kernel(a_ref, w_ref, res_ref, g_ref, b_ref, of_ref, ob_ref, *, sub):
    for r0 in range(0, a_ref.shape[0], sub):
        rows = slice(r0, r0 + sub)
        acc = jnp.dot(a_ref[rows, :], w_ref[...], preferred_element_type=F32)
        y = DN_ALPHA * res_ref[rows, :] + acc
        mu = jnp.mean(y, axis=-1, keepdims=True)
        yc = y - mu
        var = jnp.mean(yc * yc, axis=-1, keepdims=True)
        out = yc * lax.rsqrt(var + LN_EPS) * g_ref[...] + b_ref[...]
        of_ref[rows, :] = out
        ob_ref[rows, :] = out.astype(BF16)


def _proj_res_ln(a, w, layer, res, g, b, tm, sub):
    M, K = a.shape
    N = w.shape[2]
    tm = min(tm, M)
    sub = min(sub, tm)
    vmem = K * N * 2 + 2 * (tm * K * 2 + tm * N * 4 + tm * N * 4 + tm * N * 2) + 4 * sub * N * 4
    return pl.pallas_call(
        functools.partial(_mm_res_ln_kernel, sub=sub),
        grid=(M // tm,),
        in_specs=[pl.BlockSpec((tm, K), lambda i: (i, 0)),
                  pl.BlockSpec((None, K, N), lambda i: (layer, 0, 0), pipeline_mode=pl.Buffered(1)),
                  pl.BlockSpec((tm, N), lambda i: (i, 0)),
                  pl.BlockSpec((1, N), lambda i: (0, 0)),
                  pl.BlockSpec((1, N), lambda i: (0, 0))],
        out_specs=[pl.BlockSpec((tm, N), lambda i: (i, 0)),
                   pl.BlockSpec((tm, N), lambda i: (i, 0))],
        out_shape=[jax.ShapeDtypeStruct((M, N), F32), jax.ShapeDtypeStruct((M, N), BF16)],
        compiler_params=_cparams(("parallel",), vmem),
        name="proj_res_ln",
    )(a, w, res, g.reshape(1, N), b.reshape(1, N))


def _ple_kernel(xb_ref, wg_ref, p_ref, wp_ref, xr_ref, of_ref, ob_ref):
    gate = jax.nn.sigmoid(jnp.dot(xb_ref[...], wg_ref[...].astype(BF16), preferred_element_type=F32))
    pe = jnp.dot(p_ref[...].astype(BF16), wp_ref[...].astype(BF16), preferred_element_type=F32)
    out = xr_ref[...] + gate * pe
    of_ref[...] = out
    ob_ref[...] = out.astype(BF16)


def _ple(xb, xf, w_gate, p, w_proj, layer, tm, tn):
    M, K = xb.shape
    N = w_gate.shape[2]
    P = p.shape[3]
    tm, tn = min(tm, M), min(tn, N)
    wsz = jnp.dtype(w_gate.dtype).itemsize
    vmem = 2 * (tm * K * 2 + K * tn * wsz + tm * P * 4 + P * tn * 4 + tm * tn * 10) + K * tn * 2 + 4 * tm * tn * 4
    return pl.pallas_call(
        _ple_kernel,
        grid=(M // tm, N // tn),
        in_specs=[pl.BlockSpec((tm, K), lambda i, j: (i, 0)),
                  pl.BlockSpec((None, K, tn), lambda i, j: (layer, 0, j)),
                  pl.BlockSpec((None, None, tm, P), lambda i, j: (layer, 0, i, 0)),
                  pl.BlockSpec((None, P, tn), lambda i, j: (layer, 0, j)),
                  pl.BlockSpec((tm, tn), lambda i, j: (i, j))],
        out_specs=[pl.BlockSpec((tm, tn), lambda i, j: (i, j)),
                   pl.BlockSpec((tm, tn), lambda i, j: (i, j))],
        out_shape=[jax.ShapeDtypeStruct((M, N), F32), jax.ShapeDtypeStruct((M, N), BF16)],
        compiler_params=_cparams(("parallel", "parallel"), vmem),
        name="ple",
    )(xb, w_gate, p, w_proj, xf)


GDN_HB = 8
GDN_NC = 4
GDN_STAGE_UNITS = 32


def _gdn_kernel(q_ref, k_ref, v_ref, z_ref, sc_ref, hp_ref, ng_ref, o_ref, s_scr):
    C = GDN_CHUNK

    @pl.when(pl.program_id(1) == 0)
    def _():
        s_scr[...] = jnp.zeros_like(s_scr)

    row = lax.broadcasted_iota(I32, (C, C), 0)
    col = lax.broadcasted_iota(I32, (C, C), 1)
    tri = row >= col
    strict = row > col
    eye = row == col
    tri_f = tri.astype(F32)
    eye_f = eye.astype(F32)

    raw = sc_ref[...]
    a_log = hp_ref[0, 0:1, :]
    dt_b = hp_ref[0, 1:2, :]
    xs = raw + dt_b
    softplus = jnp.maximum(xs, 0.0) + jnp.log1p(jnp.exp(-jnp.abs(xs)))
    g_all = -jnp.exp(a_log) * softplus
    beta_all = jax.nn.sigmoid(raw)
    ng = ng_ref[...]

    units = [(c, j) for c in range(GDN_NC) for j in range(GDN_HB)]
    kb_l, rhs_l, decay_l, qd_l, kd_l, kbf_l, qbf_l, gl_l = [], [], [], [], [], [], [], []
    for c in range(GDN_NC):
        r0 = c * C
        gc = jnp.dot(tri_f, g_all[r0:r0 + C, :], precision=lax.Precision.HIGHEST,
                     preferred_element_type=F32)
        g_last = gc[C - 1:C, :]
        e_gc = jnp.exp(gc)
        e_rest = jnp.exp(g_last - gc)
        e_last = jnp.exp(g_last)
        beta_c = beta_all[r0:r0 + C, :]
        qn, kn = [], []
        for hq in range(GDN_HB // 2):
            qf = q_ref[r0:r0 + C, hq * GDN_DK:(hq + 1) * GDN_DK].astype(F32)
            kf = k_ref[r0:r0 + C, hq * GDN_DK:(hq + 1) * GDN_DK].astype(F32)
            qn.append(qf * lax.rsqrt(jnp.sum(qf * qf, axis=-1, keepdims=True) + RMS_EPS) * (GDN_DK ** -0.5))
            kn.append(kf * lax.rsqrt(jnp.sum(kf * kf, axis=-1, keepdims=True) + RMS_EPS))
        for j in range(GDN_HB):
            q_h, k_h = qn[j // 2], kn[j // 2]
            vf = v_ref[r0:r0 + C, j * GDN_DV:(j + 1) * GDN_DV].astype(F32)
            beta = beta_c[:, GDN_HB + j:GDN_HB + j + 1]
            kb = k_h * beta
            gcb = jnp.broadcast_to(gc[:, j:j + 1], (C, C))
            gcr = jnp.sum(jnp.where(eye, gcb, 0.0), axis=0, keepdims=True)
            decay_l.append(jnp.where(tri, jnp.exp(jnp.where(tri, gcb - gcr, 0.0)), 0.0))
            kb_l.append(kb.astype(BF16))
            rhs_l.append(jnp.concatenate([vf * beta, kb * e_gc[:, j:j + 1]], axis=-1).astype(BF16))
            qd_l.append((q_h * e_gc[:, j:j + 1]).astype(BF16))
            kd_l.append((k_h * e_rest[:, j:j + 1]).astype(BF16))
            kbf_l.append(k_h.astype(BF16))
            qbf_l.append(q_h.astype(BF16))
            gl_l.append(e_last[:, j:j + 1])

    n_u = len(units)
    qk_l, sol_l = [], []
    for b0 in range(0, n_u, GDN_STAGE_UNITS):
        us = range(b0, min(b0 + GDN_STAGE_UNITS, n_u))
        kk_b = [lax.dot_general(kb_l[u], kbf_l[u], _NT, preferred_element_type=F32) for u in us]
        qk_b = [lax.dot_general(qbf_l[u], kbf_l[u], _NT, preferred_element_type=F32) for u in us]
        qk_l += [jnp.where(tri, qk * decay_l[u], 0.0).astype(BF16) for qk, u in zip(qk_b, us)]
        x_b = [(-jnp.where(strict, kk * decay_l[u], 0.0)) for kk, u in zip(kk_b, us)]
        t_b = [eye_f + x for x in x_b]
        x_b = [x.astype(BF16) for x in x_b]
        for _ in range(5):
            x_b = [jnp.dot(x, x, preferred_element_type=F32).astype(BF16) for x in x_b]
            t_b = [t + jnp.dot(t.astype(BF16), x, preferred_element_type=F32) for t, x in zip(t_b, x_b)]
        sol_l += [jnp.dot(t.astype(BF16), rhs_l[u], preferred_element_type=F32) for t, u in zip(t_b, us)]

    s_cur = [s_scr[j] for j in range(GDN_HB)]
    for c in range(GDN_NC):
        r0 = c * C
        us = [c * GDN_HB + j for j in range(GDN_HB)]
        s_bf = [s.astype(BF16) for s in s_cur]
        ws_l = [jnp.dot(sol_l[u][:, GDN_DV:].astype(BF16), s_bf[j], preferred_element_type=F32)
                for j, u in enumerate(us)]
        qs_l = [jnp.dot(qd_l[u], s_bf[j], preferred_element_type=F32) for j, u in enumerate(us)]
        vn_l = [(sol_l[u][:, :GDN_DV] - ws_l[j]).astype(BF16) for j, u in enumerate(us)]
        kv_l = [lax.dot_general(kd_l[u], vn_l[j], _TN, preferred_element_type=F32) for j, u in enumerate(us)]
        ov_l = [jnp.dot(qk_l[u], vn_l[j], preferred_element_type=F32) for j, u in enumerate(us)]
        s_cur = [s_cur[j] * gl_l[u] + kv_l[j] for j, u in enumerate(us)]
        for j in range(GDN_HB):
            o = qs_l[j] + ov_l[j]
            zf = z_ref[r0:r0 + C, j * GDN_DV:(j + 1) * GDN_DV].astype(F32)
            o = o * lax.rsqrt(jnp.mean(o * o, axis=-1, keepdims=True) + RMS_EPS) * ng * _silu(zf)
            o_ref[r0:r0 + C, j * GDN_DV:(j + 1) * GDN_DV] = o.astype(o_ref.dtype)
    for j in range(GDN_HB):
        s_scr[j] = s_cur[j]


def _gdn_core(qkv, z, scal, hparams, norm_g):
    L = qkv.shape[0]
    G = GDN_V_HEADS // GDN_HB
    R = GDN_NC * GDN_CHUNK
    qw = (GDN_HB // 2) * GDN_DK
    vw = GDN_HB * GDN_DV
    k_blk0 = GDN_QK_W // qw
    v_blk0 = 2 * GDN_QK_W // vw
    vmem = 2 * (2 * R * qw * 2 + 2 * R * vw * 2 + R * LANES * 4 + R * vw * 2) + (16 << 20)
    return pl.pallas_call(
        _gdn_kernel,
        grid=(G, L // R),
        in_specs=[pl.BlockSpec((R, qw), lambda g, s: (s, g)),
                  pl.BlockSpec((R, qw), lambda g, s: (s, k_blk0 + g)),
                  pl.BlockSpec((R, vw), lambda g, s: (s, v_blk0 + g)),
                  pl.BlockSpec((R, vw), lambda g, s: (s, g)),
                  pl.BlockSpec((R, LANES), lambda g, s: (s, g)),
                  pl.BlockSpec((1, 8, LANES), lambda g, s: (g, 0, 0)),
                  pl.BlockSpec((1, GDN_DV), lambda g, s: (0, 0))],
        out_specs=pl.BlockSpec((R, vw), lambda g, s: (s, g)),
        out_shape=jax.ShapeDtypeStruct((L, GDN_V_W), BF16),
        scratch_shapes=[pltpu.VMEM((GDN_HB, GDN_DK, GDN_DV), F32)],
        compiler_params=_cparams(("parallel", "arbitrary"), vmem),
        name="gdn_core",
    )(qkv, qkv, qkv, z, scal, hparams, norm_g.reshape(1, GDN_DV))


def _gdn_mixer(xb, w_in_all, layer, conv_w, a_log, dt_bias, norm_g):
    nqkv = 2 * GDN_QK_W + GDN_V_W
    tn = TILE_N
    w_ab = w_in_all[layer, :, nqkv + GDN_V_W:]
    G = GDN_V_HEADS // GDN_HB
    w_a = w_ab[:, :GDN_V_HEADS].reshape(D_MODEL, G, GDN_HB)
    w_b = w_ab[:, GDN_V_HEADS:].reshape(D_MODEL, G, GDN_HB)
    w_sc = jnp.concatenate([w_a, w_b, jnp.zeros((D_MODEL, G, LANES - 2 * GDN_HB), F32)], axis=-1)
    w_sc = w_sc.reshape(D_MODEL, G * LANES).astype(BF16)
    w_t_all = jnp.swapaxes(w_in_all, 1, 2)
    qkv = _proj_conv_silu(xb, w_t_all, conv_w, tm=TILE_M, tn=tn, w_block=lambda j: (layer, 0, j), w_t=True)
    ones_z = jnp.ones((1, GDN_V_W), F32)
    z = _matmul_scaled(xb, w_t_all, ones_z, BF16, tm=TILE_M, tn=tn,
                       w_block=lambda j: (layer, 0, nqkv // tn + j), w_t=True)
    scal = _matmul_scaled(xb, w_sc, jnp.ones((1, G * LANES), F32), F32, tm=TILE_M, tn=G * LANES)
    hp = jnp.zeros((G, 8, LANES), F32)
    hp = hp.at[:, 0, :GDN_HB].set(a_log.reshape(G, GDN_HB))
    hp = hp.at[:, 1, :GDN_HB].set(dt_bias.reshape(G, GDN_HB))
    return _gdn_core(qkv, z, scal, hp, norm_g)


def _dsa_small_kernel(x_ref, w_ref, g_ref, b_ref, ki_ref, wi_ref):
    acc = jnp.dot(x_ref[...], w_ref[...], preferred_element_type=F32)
    ki = acc[:, :IDX_DIM]
    mu = jnp.mean(ki, axis=-1, keepdims=True)
    kc = ki - mu
    var = jnp.mean(kc * kc, axis=-1, keepdims=True)
    ki_ref[...] = (kc * lax.rsqrt(var + LN_EPS) * g_ref[...] + b_ref[...]).astype(ki_ref.dtype)
    wi_ref[...] = acc[:, IDX_DIM:] * ((IDX_HEADS ** -0.5) * (IDX_DIM ** -0.5))


def _dsa_small(xb, w_small, ln_g, ln_b, tm):
    M, K = xb.shape
    tm = min(tm, M)
    N = 2 * LANES
    vmem = 2 * (tm * K * 2 + K * N * 2 + tm * LANES * 6) + 4 * tm * N * 4
    return pl.pallas_call(
        _dsa_small_kernel,
        grid=(M // tm,),
        in_specs=[pl.BlockSpec((tm, K), lambda i: (i, 0)),
                  pl.BlockSpec((K, N), lambda i: (0, 0)),
                  pl.BlockSpec((1, IDX_DIM), lambda i: (0, 0)),
                  pl.BlockSpec((1, IDX_DIM), lambda i: (0, 0))],
        out_specs=[pl.BlockSpec((tm, IDX_DIM), lambda i: (i, 0)),
                   pl.BlockSpec((tm, LANES), lambda i: (i, 0))],
        out_shape=[jax.ShapeDtypeStruct((M, IDX_DIM), BF16), jax.ShapeDtypeStruct((M, LANES), F32)],
        compiler_params=_cparams(("parallel",), vmem),
        name="dsa_idx_proj",
    )(xb, w_small, ln_g.reshape(1, IDX_DIM), ln_b.reshape(1, IDX_DIM))


def _sortable_key(score):
    bits = lax.bitcast_convert_type(score, I32)
    return jnp.where(bits >= 0, bits, bits ^ jnp.int32(0x7FFFFFFF))


def _idx_kernel(qi_tab, kj_tab, last_tab,
                qidx_ref, kidx_ref, wi_ref, far_ref, near_ref, key_scr, plane_scr, cand_scr, w_scr,
                *, k_top, n_sub_total):
    s = pl.program_id(0)
    i = qi_tab[s]
    j = kj_tab[s]

    @pl.when(s == 0)
    def _():
        key_scr[...] = jnp.full_like(key_scr, INT_MIN)
        plane_scr[...] = jnp.zeros_like(plane_scr)
    nsub = IDX_KEY_BLOCK // IDX_SUB
    t_col = i * Q_BLOCK + lax.broadcasted_iota(I32, (Q_BLOCK, 1), 0)
    lane_sub = lax.broadcasted_iota(I32, (Q_BLOCK, IDX_SUB), 1)
    wi = wi_ref[...]

    qi_rows = jnp.concatenate([qidx_ref[:, h * IDX_DIM:(h + 1) * IDX_DIM] for h in range(IDX_HEADS)], axis=0)
    for sub in range(nsub):
        ki_sub = kidx_ref[sub * IDX_SUB:(sub + 1) * IDX_SUB, :]
        sc_all = lax.dot_general(qi_rows, ki_sub, _NT, preferred_element_type=F32)
        acc = jnp.zeros((Q_BLOCK, IDX_SUB), F32)
        for h in range(IDX_HEADS):
            acc = acc + jnp.maximum(sc_all[h * Q_BLOCK:(h + 1) * Q_BLOCK, :], 0.0) * wi[:, h:h + 1]
        s_idx = j * IDX_KEY_BLOCK + sub * IDX_SUB + lane_sub
        key_scr[j * nsub + sub] = jnp.where(s_idx <= t_col, _sortable_key(acc), INT_MIN)

    @pl.when(last_tab[s] == 1)
    def _():
        n_chunks = (i * Q_BLOCK + Q_BLOCK - 1) // IDX_SUB + 1

        def count(pred, ref_val):
            refb = jnp.broadcast_to(ref_val, (Q_BLOCK, LANES))

            def body(c, cnt):
                blk = key_scr[c]
                for l in range(IDX_SUB // LANES):
                    cnt = cnt + jnp.where(pred(blk[:, l * LANES:(l + 1) * LANES], refb), 1, 0)
                return cnt

            cnt = lax.fori_loop(0, n_chunks, body, jnp.zeros((Q_BLOCK, LANES), I32))
            return jnp.sum(cnt, axis=1, keepdims=True)

        tiles_per_chunk = IDX_SUB // LANES
        chunks_per_group = WORD_BITS // tiles_per_chunk
        n_tiles = n_chunks * tiles_per_chunk
        n_groups = (n_tiles + WORD_BITS - 1) // WORD_BITS

        def build_group(g, carry):
            def build_rows(r, carry_r):
                r8 = pl.multiple_of(r * 8, 8)
                a = [key_scr[g * chunks_per_group + t // tiles_per_chunk, pl.ds(r8, 8),
                             (t % tiles_per_chunk) * LANES:(t % tiles_per_chunk + 1) * LANES]
                     for t in range(WORD_BITS)]
                m, sh = 0x0000FFFF, 16
                while sh:
                    k = 0
                    while k < WORD_BITS:
                        x = (a[k] ^ lax.shift_right_logical(a[k + sh], jnp.int32(sh))) & jnp.int32(m)
                        a[k] = a[k] ^ x
                        a[k + sh] = a[k + sh] ^ jnp.left_shift(x, jnp.int32(sh))
                        k = (k + sh + 1) & ~sh
                    sh >>= 1
                    m = (m ^ (m << sh)) & 0xFFFFFFFF if sh else m
                a[0] = ~a[0]
                for p in range(WORD_BITS):
                    plane_scr[p, g, pl.ds(r8, 8), :] = a[p]
                return carry_r

            lax.fori_loop(0, Q_BLOCK // 8, build_rows, 0)
            return carry

        lax.fori_loop(0, n_groups, build_group, 0)
        n_groups_max = plane_scr.shape[1]
        for g in range(n_groups_max):
            n_valid = jnp.clip(n_tiles - g * WORD_BITS, 0, WORD_BITS)
            word = jnp.where(n_valid >= WORD_BITS, jnp.int32(-1),
                             jnp.where(n_valid <= 0, jnp.int32(0),
                                       jnp.left_shift(jnp.int32(-1), WORD_BITS - n_valid)))
            cand_scr[g] = jnp.broadcast_to(word, (Q_BLOCK, LANES))

        def plane_pass(p, thr_u, n_above, n_cand):
            cnt = jnp.zeros((Q_BLOCK, LANES), I32)
            for g in range(n_groups_max):
                w = plane_scr[p, g] & cand_scr[g]
                w_scr[g] = w
                cnt = cnt + lax.population_count(w)
            n_set = jnp.sum(cnt, axis=1, keepdims=True)
            take = n_above + n_set >= k_top
            take_b = jnp.broadcast_to(take, (Q_BLOCK, LANES))
            for g in range(n_groups_max):
                w = w_scr[g]
                cand_scr[g] = jnp.where(take_b, w, cand_scr[g] ^ w)
            bit = jnp.left_shift(jnp.int32(1), WORD_BITS - 1 - p)
            return (jnp.where(take, thr_u | bit, thr_u), jnp.where(take, n_above, n_above + n_set),
                    jnp.where(take, n_set, n_cand - n_set))

        def group_body(carry):
            grp, thr_u, n_above, n_cand, _ = carry
            for bb in range(BITS_PER_CHECK):
                thr_u, n_above, n_cand = plane_pass(grp * BITS_PER_CHECK + bb, thr_u, n_above, n_cand)
            return grp + 1, thr_u, n_above, n_cand, jnp.max(jnp.where(n_above + n_cand != k_top, 1, 0))

        _, thr_u, n_above, n_cand, n_tied_rows = lax.while_loop(
            lambda carry: (carry[0] < WORD_BITS // BITS_PER_CHECK) & (carry[4] != 0), group_body,
            (jnp.int32(0), jnp.zeros((Q_BLOCK, 1), I32), jnp.zeros((Q_BLOCK, 1), I32),
             jnp.broadcast_to(n_tiles * LANES, (Q_BLOCK, 1)).astype(I32), jnp.int32(1)))
        thr = thr_u ^ jnp.int32(INT_MIN)
        thr_b = jnp.broadcast_to(thr, (Q_BLOCK, IDX_SUB))

        def emit(c, sel):
            s_idx = c * IDX_SUB + lane_sub
            far = sel & (t_col - s_idx >= MAX_DISTANCE)
            far_ref[0, c] = jnp.where(far, 0.0, FAR_MASKED).astype(far_ref.dtype)
            key_scr[c] = jnp.where(sel, 1, 0)

        @pl.when(n_tied_rows == 0)
        def _():
            def sel_body(c, carry):
                emit(c, key_scr[c] >= thr_b)
                return carry

            lax.fori_loop(0, n_chunks, sel_body, 0)

        @pl.when(n_tied_rows != 0)
        def _():
            n_gt = count(lambda a, r: a > r, thr)
            need_eq = (k_top - n_gt).astype(F32)
            incl = (lax.broadcasted_iota(I32, (IDX_SUB, IDX_SUB), 0)
                    <= lax.broadcasted_iota(I32, (IDX_SUB, IDX_SUB), 1)).astype(BF16)

            def sel_body(c, carry):
                blk = key_scr[c]
                eq = blk == thr_b
                eq_f = jnp.where(eq, 1.0, 0.0)
                rank = carry + jnp.dot(eq_f.astype(BF16), incl, preferred_element_type=F32)
                s_idx = c * IDX_SUB + lane_sub
                emit(c, ((blk > thr_b) | (eq & (rank <= need_eq))) & (s_idx <= t_col))
                return carry + jnp.sum(eq_f, axis=1, keepdims=True)

            lax.fori_loop(0, n_chunks, sel_body, jnp.zeros((Q_BLOCK, 1), F32))

        def fill_body(c, carry):
            far_ref[0, c] = jnp.full((Q_BLOCK, IDX_SUB), FAR_MASKED, far_ref.dtype)
            return carry

        lax.fori_loop(n_chunks, n_sub_total, fill_body, 0)

        def window(blk_idx):
            per = IDX_SUB // Q_BLOCK
            chunk = key_scr[blk_idx // per]
            m = blk_idx % per
            out = chunk[:, 0:Q_BLOCK]
            for q in range(1, per):
                out = jnp.where(m == q, chunk[:, q * Q_BLOCK:(q + 1) * Q_BLOCK], out)
            return out

        r_i = lax.broadcasted_iota(I32, (Q_BLOCK, Q_BLOCK), 0)
        c_i = lax.broadcasted_iota(I32, (Q_BLOCK, Q_BLOCK), 1)
        d_lo = Q_BLOCK + r_i - c_i
        d_hi = r_i - c_i
        near_lo = (window(jnp.maximum(i - 1, 0)) != 0) & (d_lo < MAX_DISTANCE) & (i >= 1)
        near_hi = (window(i) != 0) & (d_hi >= 0) & (d_hi < MAX_DISTANCE)
        near_ref[:, 0:Q_BLOCK] = jnp.where(near_lo, 0.0, -jnp.inf).astype(near_ref.dtype)
        near_ref[:, Q_BLOCK:2 * Q_BLOCK] = jnp.where(near_hi, 0.0, -jnp.inf).astype(near_ref.dtype)


def _idx_select(proj, ki, wi, k_top):
    L = ki.shape[0]
    nq = L // Q_BLOCK
    n_sub_total = L // IDX_SUB
    qi_l, kj_l, last_l = [], [], []
    for i in range(nq):
        j_last = (i * Q_BLOCK + Q_BLOCK - 1) // IDX_KEY_BLOCK
        for j in range(j_last + 1):
            qi_l.append(i)
            kj_l.append(j)
            last_l.append(1 if j == j_last else 0)
    tabs = [jnp.asarray(np.asarray(t, np.int32)) for t in (qi_l, kj_l, last_l)]
    qcol = (DSA_HEADS * DSA_DH) // (IDX_HEADS * IDX_DIM)
    group_keys = WORD_BITS * LANES
    n_groups_max = -(-L // group_keys)
    key_chunks = n_groups_max * (group_keys // IDX_SUB)
    vmem = ((key_chunks * IDX_SUB + (WORD_BITS + 2) * n_groups_max * LANES) * Q_BLOCK * 4
            + 2 * (Q_BLOCK * L * 2) + 2 * (Q_BLOCK * IDX_HEADS * IDX_DIM * 2)
            + 2 * IDX_KEY_BLOCK * IDX_DIM * 2 + (8 << 20))
    grid_spec = pltpu.PrefetchScalarGridSpec(
        num_scalar_prefetch=3,
        grid=(len(qi_l),),
        in_specs=[pl.BlockSpec((Q_BLOCK, IDX_HEADS * IDX_DIM), lambda s, qt, kt, lt: (qt[s], qcol)),
                  pl.BlockSpec((IDX_KEY_BLOCK, IDX_DIM), lambda s, qt, kt, lt: (kt[s], 0)),
                  pl.BlockSpec((Q_BLOCK, LANES), lambda s, qt, kt, lt: (qt[s], 0))],
        out_specs=[pl.BlockSpec((1, n_sub_total, Q_BLOCK, IDX_SUB), lambda s, qt, kt, lt: (qt[s], 0, 0, 0)),
                   pl.BlockSpec((Q_BLOCK, 2 * Q_BLOCK), lambda s, qt, kt, lt: (qt[s], 0))],
        scratch_shapes=[pltpu.VMEM((key_chunks, Q_BLOCK, IDX_SUB), I32),
                        pltpu.VMEM((WORD_BITS, n_groups_max, Q_BLOCK, LANES), I32),
                        pltpu.VMEM((n_groups_max, Q_BLOCK, LANES), I32),
                        pltpu.VMEM((n_groups_max, Q_BLOCK, LANES), I32)],
    )
    return pl.pallas_call(
        functools.partial(_idx_kernel, k_top=k_top, n_sub_total=n_sub_total),
        grid_spec=grid_spec,
        out_shape=[jax.ShapeDtypeStruct((nq, n_sub_total, Q_BLOCK, IDX_SUB), BF16),
                   jax.ShapeDtypeStruct((L, 2 * Q_BLOCK), BF16)],
        compiler_params=_cparams(("arbitrary",), vmem),
        name="dsa_idx_select",
    )(*tabs, proj, ki, wi)


M_INIT = -1e30
FAR_SUBS = 2
LOG2E = math.log2(math.e)


def _attn_kernel(qi_tab, kj_tab, kind_tab, first_tab,
                 tab_ref, q_ref, kt_ref, vf_ref, klo_ref, khi_ref, vlo_ref, vhi_ref, far_ref, near_ref,
                 o_ref, m_scr, l_scr, acc_scr, b_scr, s_scr):
    s = pl.program_id(0)

    @pl.when(s == 0)
    def _():
        r_i = lax.broadcasted_iota(I32, (Q_BLOCK, 2 * Q_BLOCK), 0)
        c_i = lax.broadcasted_iota(I32, (Q_BLOCK, 2 * Q_BLOCK), 1)
        d = jnp.maximum(Q_BLOCK + r_i - c_i, 0)
        max_exact = N_BUCKETS // 2
        df = jnp.maximum(d, 1).astype(F32)
        large = max_exact + (jnp.log(df / max_exact) / math.log(MAX_DISTANCE / max_exact)
                             * (N_BUCKETS - max_exact)).astype(I32)
        large = jnp.minimum(large, N_BUCKETS - 1)
        bkt = jnp.where(d < max_exact, d, large)
        for h in range(DSA_HEADS):
            acc = jnp.zeros((Q_BLOCK, 2 * Q_BLOCK), F32)
            for b in range(N_BUCKETS):
                acc = jnp.where(bkt == b, (tab_ref[b, h] - tab_ref[N_BUCKETS - 1, h]) * LOG2E, acc)
            b_scr[h] = acc

    @pl.when(first_tab[s] == 1)
    def _():
        m_scr[...] = jnp.full_like(m_scr, M_INIT)
        l_scr[...] = jnp.zeros_like(l_scr)
        acc_scr[...] = jnp.zeros_like(acc_scr)

    def softmax_pv(v_ref, width, row0=0):
        nt = width // LANES
        rows = slice(row0, row0 + width)
        ones = jnp.ones((width, DSA_DH), BF16)
        for g in range(DSA_KV_HEADS):
            heads = range(g * DSA_GROUP, (g + 1) * DSA_GROUP)
            alphas, pbs = [], []
            for h in heads:
                tiles = [s_scr[h, :, t * LANES:(t + 1) * LANES] for t in range(nt)]
                tmax = tiles[0]
                for t in range(1, nt):
                    tmax = jnp.maximum(tmax, tiles[t])
                m_prev = m_scr[h]
                m_new = jnp.maximum(m_prev, jnp.max(tmax, axis=-1, keepdims=True))
                alphas.append(jnp.exp2(m_prev - m_new))
                p = [jnp.exp2(tiles[t] - m_new).astype(BF16) for t in range(nt)]
                pbs.append(jnp.concatenate(p, axis=-1) if nt > 1 else p[0])
                m_scr[h] = m_new
            v_aug = jnp.concatenate([v_ref[rows, g * DSA_DH:(g + 1) * DSA_DH], ones], axis=-1)
            pv = jnp.dot(jnp.concatenate(pbs, axis=0), v_aug,
                         preferred_element_type=F32)
            for n, h in enumerate(heads):
                pv_h = pv[n * Q_BLOCK:(n + 1) * Q_BLOCK]
                acc_scr[h] = alphas[n] * acc_scr[h] + pv_h[:, :DSA_DH]
                l_scr[h] = alphas[n] * l_scr[h] + pv_h[:, DSA_DH:]

    @pl.when(kind_tab[s] == 0)
    def _():
        eye = (lax.broadcasted_iota(I32, (Q_BLOCK, Q_BLOCK), 0)
               == lax.broadcasted_iota(I32, (Q_BLOCK, Q_BLOCK), 1)).astype(BF16)
        for sub in range(FAR_SUBS):
            cols = slice(sub * IDX_SUB, (sub + 1) * IDX_SUB)
            mask = far_ref[0, sub]
            for g in range(DSA_KV_HEADS):
                lhs = jnp.concatenate(
                    [jnp.concatenate([q_ref[:, h * DSA_DH:(h + 1) * DSA_DH], eye], axis=1)
                     for h in range(g * DSA_GROUP, (g + 1) * DSA_GROUP)], axis=0)
                rhs = jnp.concatenate([kt_ref[g * DSA_DH:(g + 1) * DSA_DH, cols], mask], axis=0)
                logits = jnp.dot(lhs, rhs, preferred_element_type=F32)
                for hh in range(DSA_GROUP):
                    s_scr[g * DSA_GROUP + hh] = logits[hh * Q_BLOCK:(hh + 1) * Q_BLOCK, :]
            softmax_pv(vf_ref, IDX_SUB, sub * IDX_SUB)

    @pl.when(kind_tab[s] == 1)
    def _():
        for half, (k_ref, v_ref) in enumerate(((klo_ref, vlo_ref), (khi_ref, vhi_ref))):
            cols = slice(half * Q_BLOCK, (half + 1) * Q_BLOCK)
            mask = near_ref[:, cols].astype(F32)
            for h in range(DSA_HEADS):
                g = h // DSA_GROUP
                logits = lax.dot_general(q_ref[:, h * DSA_DH:(h + 1) * DSA_DH],
                                         k_ref[:, g * DSA_DH:(g + 1) * DSA_DH], _NT, preferred_element_type=F32)
                s_scr[h, :, 0:Q_BLOCK] = logits + b_scr[h][:, cols] + mask
            softmax_pv(v_ref, Q_BLOCK)
        for h in range(DSA_HEADS):
            o_ref[:, h * DSA_DH:(h + 1) * DSA_DH] = (acc_scr[h] / l_scr[h]).astype(o_ref.dtype)


def _masked_attention(proj, far, near, rel_bias):
    L = proj.shape[0]
    nq = L // Q_BLOCK
    far_keys = FAR_SUBS * IDX_SUB
    per = far_keys // Q_BLOCK
    qi_l, kj_l, kind_l, first_l = [], [], [], []
    for i in range(nq):
        n_far = -(-i // per)
        for j in range(n_far):
            qi_l.append(i); kj_l.append(j); kind_l.append(0); first_l.append(1 if j == 0 else 0)
        qi_l.append(i); kj_l.append(max(n_far - 1, 0)); kind_l.append(1); first_l.append(1 if n_far == 0 else 0)
    tabs = [jnp.asarray(np.asarray(t, np.int32)) for t in (qi_l, kj_l, kind_l, first_l)]
    qw = DSA_HEADS * DSA_DH
    kvw = DSA_KV_HEADS * DSA_DH
    k_col = (2 * qw) // kvw
    v_col = k_col + 1
    hw = DSA_HEADS
    vmem = (2 * (Q_BLOCK * qw * 2 * 2 + 2 * far_keys * kvw * 2 + 4 * Q_BLOCK * kvw * 2
                 + Q_BLOCK * far_keys * 2 + Q_BLOCK * 2 * Q_BLOCK * 2)
            + hw * Q_BLOCK * (3 * LANES + 2 * Q_BLOCK) * 4 + (16 << 20))
    idx = lambda f: (lambda s, qt, kt, kd, ft: f(qt[s], kt[s]))
    k_t = proj[:, k_col * kvw:(k_col + 1) * kvw].T
    grid_spec = pltpu.PrefetchScalarGridSpec(
        num_scalar_prefetch=4,
        grid=(len(qi_l),),
        in_specs=[pl.BlockSpec(memory_space=pltpu.SMEM),
                  pl.BlockSpec((Q_BLOCK, qw), idx(lambda i, j: (i, 0))),
                  pl.BlockSpec((kvw, far_keys), idx(lambda i, j: (0, j))),
                  pl.BlockSpec((far_keys, kvw), idx(lambda i, j: (j, v_col))),
                  pl.BlockSpec((Q_BLOCK, kvw), idx(lambda i, j: (jnp.maximum(i - 1, 0), k_col))),
                  pl.BlockSpec((Q_BLOCK, kvw), idx(lambda i, j: (i, k_col))),
                  pl.BlockSpec((Q_BLOCK, kvw), idx(lambda i, j: (jnp.maximum(i - 1, 0), v_col))),
                  pl.BlockSpec((Q_BLOCK, kvw), idx(lambda i, j: (i, v_col))),
                  pl.BlockSpec((1, FAR_SUBS, Q_BLOCK, IDX_SUB), idx(lambda i, j: (i, j, 0, 0))),
                  pl.BlockSpec((Q_BLOCK, 2 * Q_BLOCK), idx(lambda i, j: (i, 0)))],
        out_specs=pl.BlockSpec((Q_BLOCK, qw), idx(lambda i, j: (i, 0))),
        scratch_shapes=[pltpu.VMEM((hw, Q_BLOCK, LANES), F32),
                        pltpu.VMEM((hw, Q_BLOCK, LANES), F32),
                        pltpu.VMEM((hw, Q_BLOCK, DSA_DH), F32),
                        pltpu.VMEM((hw, Q_BLOCK, 2 * Q_BLOCK), F32),
                        pltpu.VMEM((hw, Q_BLOCK, IDX_SUB), F32)],
    )
    return pl.pallas_call(
        _attn_kernel,
        grid_spec=grid_spec,
        out_shape=jax.ShapeDtypeStruct((L, qw), BF16),
        compiler_params=_cparams(("arbitrary",), vmem),
        name="dsa_attention",
    )(*tabs, rel_bias, proj, k_t, proj, proj, proj, proj, proj, far, near)


def _dsa_mixer(xb, w_in_all, layer, ln_g, ln_b, rel_bias):
    L = xb.shape[0]
    k_top = min(TOPK_MAX, L // 4)
    sq = DSA_HEADS * DSA_DH
    skv = DSA_KV_HEADS * DSA_DH
    si = IDX_HEADS * IDX_DIM
    w_ki = w_in_all[layer, :, sq + 2 * skv + si:sq + 2 * skv + si + IDX_DIM]
    w_wi = w_in_all[layer, :, sq + 2 * skv + si + IDX_DIM:]
    colscale = jnp.concatenate([jnp.full((1, sq), DSA_DH ** -0.5 * LOG2E, F32),
                                jnp.ones((1, si + 2 * skv), F32)], axis=1)
    w_small = jnp.concatenate([w_ki, w_wi, jnp.zeros((D_MODEL, LANES - IDX_HEADS), F32)], axis=1).astype(BF16)
    tn = 2 * skv
    nq, ni = sq // tn, si // tn
    src_block = lambda j: jnp.where(j < nq, j, jnp.where(j < nq + ni, j + 1, nq))
    proj = _matmul_scaled(xb, jnp.swapaxes(w_in_all, 1, 2), colscale, BF16, tm=TILE_M, tn=tn,
                          w_block=lambda j: (layer, 0, src_block(j)), w_t=True)
    ki, wi = _dsa_small(xb, w_small, ln_g, ln_b, tm=TILE_M)
    far, near = _idx_select(proj, ki, wi, k_top)
    return _masked_attention(proj, far, near, rel_bias)


def kernel(x, p, gdn_w_in, gdn_conv_w, gdn_a_log, gdn_dt_bias, gdn_norm_g, gdn_w_o, dsa_w_in, dsa_kidx_ln_g, dsa_kidx_ln_b, dsa_w_o, rel_bias, ln1_g, ln1_b, ffn_w_gate, ffn_w_up, ffn_conv_w, ffn_w_down, ln2_g, ln2_b, ple_w_proj, ple_w_gate):
    assert x.shape[0] == 1 and x.shape[2] == D_MODEL
    xf = x[0]
    xb = xf.astype(BF16)
    ple_gate_bf = ple_w_gate.astype(BF16)
    ffn_down_bf = ffn_w_down.astype(BF16)
    gdn_o_bf = gdn_w_o.astype(BF16)
    dsa_o_bf = dsa_w_o.astype(BF16)
    ia = ib = 0
    for i in range(DEPTH):
        if i % 2 == 0:
            mix = _gdn_mixer(xb, gdn_w_in, ia, gdn_conv_w[ia], gdn_a_log[ia], gdn_dt_bias[ia], gdn_norm_g[ia])
            w_o, lo = gdn_o_bf, ia
            ia += 1
        else:
            mix = _dsa_mixer(xb, dsa_w_in, ib, dsa_kidx_ln_g[ib], dsa_kidx_ln_b[ib], rel_bias)
            w_o, lo = dsa_o_bf, ib
            ib += 1
        xf, xb = _proj_res_ln(mix, w_o, lo, xf, ln1_g[i], ln1_b[i], tm=LN_TILE_M, sub=LN_SUB_M)
        hmid = _ffn_up(xb, ffn_w_gate, ffn_w_up, ffn_conv_w[i], i, tm=TILE_M, tn=FFN_TILE_N)
        xf, xb = _proj_res_ln(hmid, ffn_down_bf, i, xf, ln2_g[i], ln2_b[i], tm=LN_TILE_M, sub=LN_SUB_M)
        xf, xb = _ple(xb, xf, ple_gate_bf, p, ple_w_proj, i, tm=TILE_M, tn=TILE_N)
    return xf[None]
```

```python
import functools
import math

import jax
import jax.numpy as jnp
import numpy as np
from jax import lax
from jax.experimental import pallas as pl
from jax.experimental.pallas import tpu as pltpu

F32 = jnp.float32
BF16 = jnp.bfloat16
I32 = jnp.int32

D_MODEL = 2048
GDN_QK_HEADS = 16
GDN_V_HEADS = 32
GDN_DK = 128
GDN_DV = 128
GDN_CONV = 4
GDN_CHUNK = 64
GDN_QK_W = GDN_QK_HEADS * GDN_DK
GDN_V_W = GDN_V_HEADS * GDN_DV
DSA_HEADS = 16
DSA_KV_HEADS = 4
DSA_GROUP = DSA_HEADS // DSA_KV_HEADS
DSA_DH = 128
IDX_HEADS = 16
IDX_DIM = 128
TOPK_MAX = 256
N_BUCKETS = 32
MAX_DISTANCE = 128
D_FF = 5120
FFN_CONV = 3
PLE_DIM = 256
DEPTH = 2
DN_ALPHA = (2.0 * DEPTH) ** 0.25
LN_EPS = 1e-5
RMS_EPS = 1e-6

V7X_VMEM_BYTES = 64 * 1024 * 1024
V7X_VMEM_BUDGET = 56 * 1024 * 1024
LANES = 128
BF16_SUBLANES = 16

TILE_M = 1024
TILE_N = 1024
FFN_TILE_N = 512
LN_TILE_M = 512
LN_SUB_M = 256

HALO = BF16_SUBLANES
Q_BLOCK = 128
IDX_KEY_BLOCK = 2048
IDX_SUB = 512
INT_MIN = -(2 ** 31)
BITS_PER_CHECK = 4
WORD_BITS = 32
FAR_MASKED = -2e30

_NT = (((1,), (1,)), ((), ()))
_TN = (((0,), (0,)), ((), ()))


def _cparams(semantics, vmem_bytes):
    return pltpu.CompilerParams(dimension_semantics=semantics,
                                vmem_limit_bytes=int(min(V7X_VMEM_BUDGET, vmem_bytes)))


def _silu(y):
    return y * jax.nn.sigmoid(y)


def _mm_scale_kernel(x_ref, w_ref, cs_ref, o_ref, *, w_t):
    w = w_ref[...].astype(BF16)
    acc = (lax.dot_general(x_ref[...], w, _NT, preferred_element_type=F32) if w_t
           else jnp.dot(x_ref[...], w, preferred_element_type=F32))
    o_ref[...] = (acc * cs_ref[...]).astype(o_ref.dtype)


def _weight_spec(w, K, tn, w_block, w_t=False):
    if w_block is None:
        return pl.BlockSpec((K, tn), lambda i, j: (0, j))
    if w_t:
        return pl.BlockSpec((None, tn, K), lambda i, j: (w_block(j)[0], w_block(j)[2], 0))
    return pl.BlockSpec((None, K, tn), lambda i, j: w_block(j))


def _matmul_scaled(x, w, colscale, out_dtype, tm, tn, w_block=None, w_t=False):
    M, K = x.shape
    N = colscale.shape[1]
    tm, tn = min(tm, M), min(tn, N)
    osz = jnp.dtype(out_dtype).itemsize
    wsz = jnp.dtype(w.dtype).itemsize
    vmem = 2 * (tm * K * 2 + K * tn * wsz + tm * tn * osz) + K * tn * 2 + 2 * tm * tn * 4
    return pl.pallas_call(
        functools.partial(_mm_scale_kernel, w_t=w_t),
        grid=(M // tm, N // tn),
        in_specs=[pl.BlockSpec((tm, K), lambda i, j: (i, 0)),
                  _weight_spec(w, K, tn, w_block, w_t),
                  pl.BlockSpec((1, tn), lambda i, j: (0, j))],
        out_specs=pl.BlockSpec((tm, tn), lambda i, j: (i, j)),
        out_shape=jax.ShapeDtypeStruct((M, N), out_dtype),
        compiler_params=_cparams(("parallel", "parallel"), vmem),
        name="matmul_scaled",
    )(x, w, colscale)


CONV_SUB = 256
FFN_SUB = 512


def _causal_conv(g, cw_ref, g_scr, carry_scr, slot, kc, tm, cols):
    g_scr[0:HALO, cols] = carry_scr[slot]
    g_scr[HALO:HALO + tm, cols] = g
    carry_scr[slot] = g[tm - HALO:tm, :]
    y = cw_ref[kc - 1:kc, cols] * g
    for j in range(kc - 1):
        off = HALO - (kc - 1) + j
        y = y + cw_ref[j:j + 1, cols] * g_scr[off:off + tm, cols]
    return y


def _init_carry(carry_scr):
    @pl.when((pl.program_id(0) == 0) & (pl.program_id(1) == 0))
    def _():
        carry_scr[...] = jnp.zeros_like(carry_scr)


def _mm_conv_silu_kernel(x_ref, w_ref, cw_ref, o_ref, g_scr, carry_scr, *, kc, tm, w_t):
    _init_carry(carry_scr)
    nsub = o_ref.shape[1] // CONV_SUB
    for n in range(nsub):
        cols = slice(n * CONV_SUB, (n + 1) * CONV_SUB)
        if w_t:
            g = lax.dot_general(x_ref[...], w_ref[cols, :].astype(BF16), _NT, preferred_element_type=F32)
        else:
            g = jnp.dot(x_ref[...], w_ref[:, cols].astype(BF16), preferred_element_type=F32)
        y = _causal_conv(g, cw_ref, g_scr, carry_scr, pl.program_id(1) * nsub + n, kc, tm, cols)
        o_ref[:, cols] = _silu(y).astype(o_ref.dtype)


def _proj_conv_silu(x, w, conv_w, tm, tn, w_block=None, w_t=False):
    M, K = x.shape
    kc, N = conv_w.shape
    tm, tn = min(tm, M), min(tn, N)
    wsz = jnp.dtype(w.dtype).itemsize
    vmem = (2 * (tm * K * 2 + K * tn * wsz + tm * tn * 2) + K * tn * 2 + 4 * tm * tn * 4
            + HALO * N * 4)
    return pl.pallas_call(
        functools.partial(_mm_conv_silu_kernel, kc=kc, tm=tm, w_t=w_t),
        grid=(M // tm, N // tn),
        in_specs=[pl.BlockSpec((tm, K), lambda i, j: (i, 0)),
                  _weight_spec(w, K, tn, w_block, w_t),
                  pl.BlockSpec((kc, tn), lambda i, j: (0, j))],
        out_specs=pl.BlockSpec((tm, tn), lambda i, j: (i, j)),
        out_shape=jax.ShapeDtypeStruct((M, N), BF16),
        scratch_shapes=[pltpu.VMEM((tm + HALO, tn), F32),
                        pltpu.VMEM((N // CONV_SUB, HALO, CONV_SUB), F32)],
        compiler_params=_cparams(("arbitrary", "arbitrary"), vmem),
        name="proj_conv_silu",
    )(x, w, conv_w)


def _ffn_up_kernel(x_ref, wg_ref, wu_ref, cw_ref, o_ref, g_scr, carry_scr, *, kc, tm):
    _init_carry(carry_scr)
    nsub = o_ref.shape[1] // FFN_SUB
    for n in range(nsub):
        cols = slice(n * FFN_SUB, (n + 1) * FFN_SUB)
        g = jnp.dot(x_ref[...], wg_ref[:, cols].astype(BF16), preferred_element_type=F32)
        u = jnp.dot(x_ref[...], wu_ref[:, cols].astype(BF16), preferred_element_type=F32)
        y = _causal_conv(g, cw_ref, g_scr, carry_scr, pl.program_id(1) * nsub + n, kc, tm, cols)
        o_ref[:, cols] = (_silu(y) * u).astype(o_ref.dtype)


def _ffn_up(x, w_gate, w_up, conv_w, layer, tm, tn):
    M, K = x.shape
    N = w_gate.shape[2]
    kc = conv_w.shape[0]
    tm, tn = min(tm, M), min(tn, N)
    wsz = jnp.dtype(w_gate.dtype).itemsize
    vmem = (2 * (tm * K * 2 + 2 * K * tn * wsz + tm * tn * 2) + 2 * K * tn * 2 + 6 * tm * tn * 4
            + HALO * N * 4)
    return pl.pallas_call(
        functools.partial(_ffn_up_kernel, kc=kc, tm=tm),
        grid=(M // tm, N // tn),
        in_specs=[pl.BlockSpec((tm, K), lambda i, j: (i, 0)),
                  pl.BlockSpec((None, K, tn), lambda i, j: (layer, 0, j)),
                  pl.BlockSpec((None, K, tn), lambda i, j: (layer, 0, j)),
                  pl.BlockSpec((kc, tn), lambda i, j: (0, j))],
        out_specs=pl.BlockSpec((tm, tn), lambda i, j: (i, j)),
        out_shape=jax.ShapeDtypeStruct((M, N), BF16),
        scratch_shapes=[pltpu.VMEM((tm + HALO, tn), F32),
                        pltpu.VMEM((N // FFN_SUB, HALO, FFN_SUB), F32)],
        compiler_params=_cparams(("arbitrary", "arbitrary"), vmem),
        name="ffn_up",
    )(x, w_gate, w_up, conv_w)


def _mm_res_ln_kernel(a_ref, w_ref, res_ref, g_ref, b_ref, of_ref, ob_ref, *, sub):
    for r0 in range(0, a_ref.shape[0], sub):
        rows = slice(r0, r0 + sub)
        acc = jnp.dot(a_ref[rows, :], w_ref[...], preferred_element_type=F32)
        y = DN_ALPHA * res_ref[rows, :] + acc
        mu = jnp.mean(y, axis=-1, keepdims=True)
        yc = y - mu
        var = jnp.mean(yc * yc, axis=-1, keepdims=True)
        out = yc * lax.rsqrt(var + LN_EPS) * g_ref[...] + b_ref[...]
        of_ref[rows, :] = out
        ob_ref[rows, :] = out.astype(BF16)


def _proj_res_ln(a, w, layer, res, g, b, tm, sub):
    M, K = a.shape
    N = w.shape[2]
    tm = min(tm, M)
    sub = min(sub, tm)
    vmem = K * N * 2 + 2 * (tm * K * 2 + tm * N * 4 + tm * N * 4 + tm * N * 2) + 4 * sub * N * 4
    return pl.pallas_call(
        functools.partial(_mm_res_ln_kernel, sub=sub),
        grid=(M // tm,),
        in_specs=[pl.BlockSpec((tm, K), lambda i: (i, 0)),
                  pl.BlockSpec((None, K, N), lambda i: (layer, 0, 0), pipeline_mode=pl.Buffered(1)),
                  pl.BlockSpec((tm, N), lambda i: (i, 0)),
                  pl.BlockSpec((1, N), lambda i: (0, 0)),
                  pl.BlockSpec((1, N), lambda i: (0, 0))],
        out_specs=[pl.BlockSpec((tm, N), lambda i: (i, 0)),
                   pl.BlockSpec((tm, N), lambda i: (i, 0))],
        out_shape=[jax.ShapeDtypeStruct((M, N), F32), jax.ShapeDtypeStruct((M, N), BF16)],
        compiler_params=_cparams(("parallel",), vmem),
        name="proj_res_ln",
    )(a, w, res, g.reshape(1, N), b.reshape(1, N))


def _ple_kernel(xb_ref, wg_ref, p_ref, wp_ref, xr_ref, of_ref, ob_ref):
    gate = jax.nn.sigmoid(jnp.dot(xb_ref[...], wg_ref[...].astype(BF16), preferred_element_type=F32))
    pe = jnp.dot(p_ref[...].astype(BF16), wp_ref[...].astype(BF16), preferred_element_type=F32)
    out = xr_ref[...] + gate * pe
    of_ref[...] = out
    ob_ref[...] = out.astype(BF16)


def _ple(xb, xf, w_gate, p, w_proj, layer, tm, tn):
    M, K = xb.shape
    N = w_gate.shape[2]
    P = p.shape[3]
    tm, tn = min(tm, M), min(tn, N)
    wsz = jnp.dtype(w_gate.dtype).itemsize
    vmem = 2 * (tm * K * 2 + K * tn * wsz + tm * P * 4 + P * tn * 4 + tm * tn * 10) + K * tn * 2 + 4 * tm * tn * 4
    return pl.pallas_call(
        _ple_kernel,
        grid=(M // tm, N // tn),
        in_specs=[pl.BlockSpec((tm, K), lambda i, j: (i, 0)),
                  pl.BlockSpec((None, K, tn), lambda i, j: (layer, 0, j)),
                  pl.BlockSpec((None, None, tm, P), lambda i, j: (layer, 0, i, 0)),
                  pl.BlockSpec((None, P, tn), lambda i, j: (layer, 0, j)),
                  pl.BlockSpec((tm, tn), lambda i, j: (i, j))],
        out_specs=[pl.BlockSpec((tm, tn), lambda i, j: (i, j)),
                   pl.BlockSpec((tm, tn), lambda i, j: (i, j))],
        out_shape=[jax.ShapeDtypeStruct((M, N), F32), jax.ShapeDtypeStruct((M, N), BF16)],
        compiler_params=_cparams(("parallel", "parallel"), vmem),
        name="ple",
    )(xb, w_gate, p, w_proj, xf)


GDN_HB = 8
GDN_NC = 8
GDN_STAGE_UNITS = 64


def _gdn_kernel(q_ref, k_ref, v_ref, z_ref, sc_ref, hp_ref, ng_ref, o_ref, s_scr):
    C = GDN_CHUNK

    @pl.when(pl.program_id(1) == 0)
    def _():
        s_scr[...] = jnp.zeros_like(s_scr)

    row = lax.broadcasted_iota(I32, (C, C), 0)
    col = lax.broadcasted_iota(I32, (C, C), 1)
    tri = row >= col
    strict = row > col
    eye = row == col
    tri_f = tri.astype(F32)
    eye_f = eye.astype(F32)

    raw = sc_ref[...]
    a_log = hp_ref[0, 0:1, :]
    dt_b = hp_ref[0, 1:2, :]
    xs = raw + dt_b
    softplus = jnp.maximum(xs, 0.0) + jnp.log1p(jnp.exp(-jnp.abs(xs)))
    g_all = -jnp.exp(a_log) * softplus
    beta_all = jax.nn.sigmoid(raw)
    ng = ng_ref[...]

    units = [(c, j) for c in range(GDN_NC) for j in range(GDN_HB)]
    kb_l, rhs_l, decay_l, qd_l, kd_l, kbf_l, qbf_l, gl_l = [], [], [], [], [], [], [], []
    for c in range(GDN_NC):
        r0 = c * C
        gc = jnp.dot(tri_f, g_all[r0:r0 + C, :], precision=lax.Precision.HIGHEST,
                     preferred_element_type=F32)
        g_last = gc[C - 1:C, :]
        e_gc = jnp.exp(gc)
        e_rest = jnp.exp(g_last - gc)
        e_last = jnp.exp(g_last)
        beta_c = beta_all[r0:r0 + C, :]
        qn, kn = [], []
        for hq in range(GDN_HB // 2):
            qf = q_ref[r0:r0 + C, hq * GDN_DK:(hq + 1) * GDN_DK].astype(F32)
            kf = k_ref[r0:r0 + C, hq * GDN_DK:(hq + 1) * GDN_DK].astype(F32)
            qn.append(qf * lax.rsqrt(jnp.sum(qf * qf, axis=-1, keepdims=True) + RMS_EPS) * (GDN_DK ** -0.5))
            kn.append(kf * lax.rsqrt(jnp.sum(kf * kf, axis=-1, keepdims=True) + RMS_EPS))
        for j in range(GDN_HB):
            q_h, k_h = qn[j // 2], kn[j // 2]
            vf = v_ref[r0:r0 + C, j * GDN_DV:(j + 1) * GDN_DV].astype(F32)
            beta = beta_c[:, GDN_HB + j:GDN_HB + j + 1]
            kb = k_h * beta
            gcb = jnp.broadcast_to(gc[:, j:j + 1], (C, C))
            gcr = jnp.sum(jnp.where(eye, gcb, 0.0), axis=0, keepdims=True)
            decay_l.append(jnp.where(tri, jnp.exp(jnp.where(tri, gcb - gcr, 0.0)), 0.0))
            kb_l.append(kb.astype(BF16))
            rhs_l.append(jnp.concatenate([vf * beta, kb * e_gc[:, j:j + 1]], axis=-1).astype(BF16))
            qd_l.append((q_h * e_gc[:, j:j + 1]).astype(BF16))
            kd_l.append((k_h * e_rest[:, j:j + 1]).astype(BF16))
            kbf_l.append(k_h.astype(BF16))
            qbf_l.append(q_h.astype(BF16))
            gl_l.append(e_last[:, j:j + 1])

    n_u = len(units)
    qk_l, sol_l = [], []
    for b0 in range(0, n_u, GDN_STAGE_UNITS):
        us = range(b0, min(b0 + GDN_STAGE_UNITS, n_u))
        kk_b = [lax.dot_general(kb_l[u], kbf_l[u], _NT, preferred_element_type=F32) for u in us]
        qk_b = [lax.dot_general(qbf_l[u], kbf_l[u], _NT, preferred_element_type=F32) for u in us]
        qk_l += [jnp.where(tri, qk * decay_l[u], 0.0).astype(BF16) for qk, u in zip(qk_b, us)]
        x_b = [(-jnp.where(strict, kk * decay_l[u], 0.0)) for kk, u in zip(kk_b, us)]
        t_b = [eye_f + x for x in x_b]
        x_b = [x.astype(BF16) for x in x_b]
        for _ in range(5):
            x_b = [jnp.dot(x, x, preferred_element_type=F32).astype(BF16) for x in x_b]
            t_b = [t + jnp.dot(t.astype(BF16), x, preferred_element_type=F32) for t, x in zip(t_b, x_b)]
        sol_l += [jnp.dot(t.astype(BF16), rhs_l[u], preferred_element_type=F32) for t, u in zip(t_b, us)]

    s_cur = [s_scr[j] for j in range(GDN_HB)]
    for c in range(GDN_NC):
        r0 = c * C
        us = [c * GDN_HB + j for j in range(GDN_HB)]
        s_bf = [s.astype(BF16) for s in s_cur]
        ws_l = [jnp.dot(sol_l[u][:, GDN_DV:].astype(BF16), s_bf[j], preferred_element_type=F32)
                for j, u in enumerate(us)]
        qs_l = [jnp.dot(qd_l[u], s_bf[j], preferred_element_type=F32) for j, u in enumerate(us)]
        vn_l = [(sol_l[u][:, :GDN_DV] - ws_l[j]).astype(BF16) for j, u in enumerate(us)]
        kv_l = [lax.dot_general(kd_l[u], vn_l[j], _TN, preferred_element_type=F32) for j, u in enumerate(us)]
        ov_l = [jnp.dot(qk_l[u], vn_l[j], preferred_element_type=F32) for j, u in enumerate(us)]
        s_cur = [s_cur[j] * gl_l[u] + kv_l[j] for j, u in enumerate(us)]
        for j in range(GDN_HB):
            o = qs_l[j] + ov_l[j]
            zf = z_ref[r0:r0 + C, j * GDN_DV:(j + 1) * GDN_DV].astype(F32)
            o = o * lax.rsqrt(jnp.mean(o * o, axis=-1, keepdims=True) + RMS_EPS) * ng * _silu(zf)
            o_ref[r0:r0 + C, j * GDN_DV:(j + 1) * GDN_DV] = o.astype(o_ref.dtype)
    for j in range(GDN_HB):
        s_scr[j] = s_cur[j]


def _gdn_core(qkv, z, scal, hparams, norm_g):
    L = qkv.shape[0]
    G = GDN_V_HEADS // GDN_HB
    R = GDN_NC * GDN_CHUNK
    qw = (GDN_HB // 2) * GDN_DK
    vw = GDN_HB * GDN_DV
    k_blk0 = GDN_QK_W // qw
    v_blk0 = 2 * GDN_QK_W // vw
    vmem = (2 * (2 * R * qw * 2 + 2 * R * vw * 2 + R * LANES * 4 + R * vw * 2)
            + GDN_NC * GDN_HB * (512 << 10) + (8 << 20))
    return pl.pallas_call(
        _gdn_kernel,
        grid=(G, L // R),
        in_specs=[pl.BlockSpec((R, qw), lambda g, s: (s, g)),
                  pl.BlockSpec((R, qw), lambda g, s: (s, k_blk0 + g)),
                  pl.BlockSpec((R, vw), lambda g, s: (s, v_blk0 + g)),
                  pl.BlockSpec((R, vw), lambda g, s: (s, g)),
                  pl.BlockSpec((R, LANES), lambda g, s: (s, g)),
                  pl.BlockSpec((1, 8, LANES), lambda g, s: (g, 0, 0)),
                  pl.BlockSpec((1, GDN_DV), lambda g, s: (0, 0))],
        out_specs=pl.BlockSpec((R, vw), lambda g, s: (s, g)),
        out_shape=jax.ShapeDtypeStruct((L, GDN_V_W), BF16),
        scratch_shapes=[pltpu.VMEM((GDN_HB, GDN_DK, GDN_DV), F32)],
        compiler_params=_cparams(("parallel", "arbitrary"), vmem),
        name="gdn_core",
    )(qkv, qkv, qkv, z, scal, hparams, norm_g.reshape(1, GDN_DV))


def _gdn_mixer(xb, w_in_all, layer, conv_w, a_log, dt_bias, norm_g):
    nqkv = 2 * GDN_QK_W + GDN_V_W
    tn = TILE_N
    w_ab = w_in_all[layer, :, nqkv + GDN_V_W:]
    G = GDN_V_HEADS // GDN_HB
    w_a = w_ab[:, :GDN_V_HEADS].reshape(D_MODEL, G, GDN_HB)
    w_b = w_ab[:, GDN_V_HEADS:].reshape(D_MODEL, G, GDN_HB)
    w_sc = jnp.concatenate([w_a, w_b, jnp.zeros((D_MODEL, G, LANES - 2 * GDN_HB), F32)], axis=-1)
    w_sc = w_sc.reshape(D_MODEL, G * LANES).astype(BF16)
    w_t_all = jnp.swapaxes(w_in_all, 1, 2)
    qkv = _proj_conv_silu(xb, w_t_all, conv_w, tm=TILE_M, tn=tn, w_block=lambda j: (layer, 0, j), w_t=True)
    ones_z = jnp.ones((1, GDN_V_W), F32)
    z = _matmul_scaled(xb, w_t_all, ones_z, BF16, tm=TILE_M, tn=tn,
                       w_block=lambda j: (layer, 0, nqkv // tn + j), w_t=True)
    scal = _matmul_scaled(xb, w_sc, jnp.ones((1, G * LANES), F32), F32, tm=TILE_M, tn=G * LANES)
    hp = jnp.zeros((G, 8, LANES), F32)
    hp = hp.at[:, 0, :GDN_HB].set(a_log.reshape(G, GDN_HB))
    hp = hp.at[:, 1, :GDN_HB].set(dt_bias.reshape(G, GDN_HB))
    return _gdn_core(qkv, z, scal, hp, norm_g)


def _dsa_small_kernel(x_ref, w_ref, g_ref, b_ref, ki_ref, wi_ref):
    acc = jnp.dot(x_ref[...], w_ref[...], preferred_element_type=F32)
    ki = acc[:, :IDX_DIM]
    mu = jnp.mean(ki, axis=-1, keepdims=True)
    kc = ki - mu
    var = jnp.mean(kc * kc, axis=-1, keepdims=True)
    ki_ref[...] = (kc * lax.rsqrt(var + LN_EPS) * g_ref[...] + b_ref[...]).astype(ki_ref.dtype)
    wi_ref[...] = acc[:, IDX_DIM:] * ((IDX_HEADS ** -0.5) * (IDX_DIM ** -0.5))


def _dsa_small(xb, w_small, ln_g, ln_b, tm):
    M, K = xb.shape
    tm = min(tm, M)
    N = 2 * LANES
    vmem = 2 * (tm * K * 2 + K * N * 2 + tm * LANES * 6) + 4 * tm * N * 4
    return pl.pallas_call(
        _dsa_small_kernel,
        grid=(M // tm,),
        in_specs=[pl.BlockSpec((tm, K), lambda i: (i, 0)),
                  pl.BlockSpec((K, N), lambda i: (0, 0)),
                  pl.BlockSpec((1, IDX_DIM), lambda i: (0, 0)),
                  pl.BlockSpec((1, IDX_DIM), lambda i: (0, 0))],
        out_specs=[pl.BlockSpec((tm, IDX_DIM), lambda i: (i, 0)),
                   pl.BlockSpec((tm, LANES), lambda i: (i, 0))],
        out_shape=[jax.ShapeDtypeStruct((M, IDX_DIM), BF16), jax.ShapeDtypeStruct((M, LANES), F32)],
        compiler_params=_cparams(("parallel",), vmem),
        name="dsa_idx_proj",
    )(xb, w_small, ln_g.reshape(1, IDX_DIM), ln_b.reshape(1, IDX_DIM))


def _sortable_key(score):
    bits = lax.bitcast_convert_type(score, I32)
    return jnp.where(bits >= 0, bits, bits ^ jnp.int32(0x7FFFFFFF))


def _idx_kernel(qi_tab, kj_tab, last_tab,
                qidx_ref, kidx_ref, wi_ref, far_ref, near_ref, key_scr, plane_scr, cand_scr, w_scr,
                *, k_top, n_sub_total):
    s = pl.program_id(0)
    i = qi_tab[s]
    j = kj_tab[s]

    @pl.when(s == 0)
    def _():
        key_scr[...] = jnp.full_like(key_scr, INT_MIN)
        plane_scr[...] = jnp.zeros_like(plane_scr)
    nsub = IDX_KEY_BLOCK // IDX_SUB
    t_col = i * Q_BLOCK + lax.broadcasted_iota(I32, (Q_BLOCK, 1), 0)
    lane_sub = lax.broadcasted_iota(I32, (Q_BLOCK, IDX_SUB), 1)
    wi = wi_ref[...]

    qi_rows = jnp.concatenate([qidx_ref[:, h * IDX_DIM:(h + 1) * IDX_DIM] for h in range(IDX_HEADS)], axis=0)
    for sub in range(nsub):
        ki_sub = kidx_ref[sub * IDX_SUB:(sub + 1) * IDX_SUB, :]
        sc_all = lax.dot_general(qi_rows, ki_sub, _NT, preferred_element_type=F32)
        acc = jnp.zeros((Q_BLOCK, IDX_SUB), F32)
        for h in range(IDX_HEADS):
            acc = acc + jnp.maximum(sc_all[h * Q_BLOCK:(h + 1) * Q_BLOCK, :], 0.0) * wi[:, h:h + 1]
        s_idx = j * IDX_KEY_BLOCK + sub * IDX_SUB + lane_sub
        key_scr[j * nsub + sub] = jnp.where(s_idx <= t_col, _sortable_key(acc), INT_MIN)

    @pl.when(last_tab[s] == 1)
    def _():
        n_chunks = (i * Q_BLOCK + Q_BLOCK - 1) // IDX_SUB + 1

        def count(pred, ref_val):
            refb = jnp.broadcast_to(ref_val, (Q_BLOCK, LANES))

            def body(c, cnt):
                blk = key_scr[c]
                for l in range(IDX_SUB // LANES):
                    cnt = cnt + jnp.where(pred(blk[:, l * LANES:(l + 1) * LANES], refb), 1, 0)
                return cnt

            cnt = lax.fori_loop(0, n_chunks, body, jnp.zeros((Q_BLOCK, LANES), I32))
            return jnp.sum(cnt, axis=1, keepdims=True)

        tiles_per_chunk = IDX_SUB // LANES
        chunks_per_group = WORD_BITS // tiles_per_chunk
        n_tiles = n_chunks * tiles_per_chunk
        n_groups = (n_tiles + WORD_BITS - 1) // WORD_BITS

        def build_group(g, carry):
            def build_rows(r, carry_r):
                r8 = pl.multiple_of(r * 8, 8)
                a = [key_scr[g * chunks_per_group + t // tiles_per_chunk, pl.ds(r8, 8),
                             (t % tiles_per_chunk) * LANES:(t % tiles_per_chunk + 1) * LANES]
                     for t in range(WORD_BITS)]
                m, sh = 0x0000FFFF, 16
                while sh:
                    k = 0
                    while k < WORD_BITS:
                        x = (a[k] ^ lax.shift_right_logical(a[k + sh], jnp.int32(sh))) & jnp.int32(m)
                        a[k] = a[k] ^ x
                        a[k + sh] = a[k + sh] ^ jnp.left_shift(x, jnp.int32(sh))
                        k = (k + sh + 1) & ~sh
                    sh >>= 1
                    m = (m ^ (m << sh)) & 0xFFFFFFFF if sh else m
                a[0] = ~a[0]
                for p in range(WORD_BITS):
                    plane_scr[p, g, pl.ds(r8, 8), :] = a[p]
                return carry_r

            lax.fori_loop(0, Q_BLOCK // 8, build_rows, 0)
            return carry

        lax.fori_loop(0, n_groups, build_group, 0)
        n_groups_max = plane_scr.shape[1]
        for g in range(n_groups_max):
            n_valid = jnp.clip(n_tiles - g * WORD_BITS, 0, WORD_BITS)
            word = jnp.where(n_valid >= WORD_BITS, jnp.int32(-1),
                             jnp.where(n_valid <= 0, jnp.int32(0),
                                       jnp.left_shift(jnp.int32(-1), WORD_BITS - n_valid)))
            cand_scr[g] = jnp.broadcast_to(word, (Q_BLOCK, LANES))

        def plane_pass(p, thr_u, n_above, n_cand):
            cnt = jnp.zeros((Q_BLOCK, LANES), I32)
            for g in range(n_groups_max):
                w = plane_scr[p, g] & cand_scr[g]
                w_scr[g] = w
                cnt = cnt + lax.population_count(w)
            n_set = jnp.sum(cnt, axis=1, keepdims=True)
            take = n_above + n_set >= k_top
            take_b = jnp.broadcast_to(take, (Q_BLOCK, LANES))
            for g in range(n_groups_max):
                w = w_scr[g]
                cand_scr[g] = jnp.where(take_b, w, cand_scr[g] ^ w)
            bit = jnp.left_shift(jnp.int32(1), WORD_BITS - 1 - p)
            return (jnp.where(take, thr_u | bit, thr_u), jnp.where(take, n_above, n_above + n_set),
                    jnp.where(take, n_set, n_cand - n_set))

        def group_body(carry):
            grp, thr_u, n_above, n_cand, _ = carry
            for bb in range(BITS_PER_CHECK):
                thr_u, n_above, n_cand = plane_pass(grp * BITS_PER_CHECK + bb, thr_u, n_above, n_cand)
            return grp + 1, thr_u, n_above, n_cand, jnp.max(jnp.where(n_above + n_cand != k_top, 1, 0))

        _, thr_u, n_above, n_cand, n_tied_rows = lax.while_loop(
            lambda carry: (carry[0] < WORD_BITS // BITS_PER_CHECK) & (carry[4] != 0), group_body,
            (jnp.int32(0), jnp.zeros((Q_BLOCK, 1), I32), jnp.zeros((Q_BLOCK, 1), I32),
             jnp.broadcast_to(n_tiles * LANES, (Q_BLOCK, 1)).astype(I32), jnp.int32(1)))
        thr = thr_u ^ jnp.int32(INT_MIN)
        thr_b = jnp.broadcast_to(thr, (Q_BLOCK, IDX_SUB))

        def emit(c, sel):
            s_idx = c * IDX_SUB + lane_sub
            far = sel & (t_col - s_idx >= MAX_DISTANCE)
            far_ref[0, c] = jnp.where(far, 0.0, FAR_MASKED).astype(far_ref.dtype)
            key_scr[c] = jnp.where(sel, 1, 0)

        @pl.when(n_tied_rows == 0)
        def _():
            def sel_body(c, carry):
                emit(c, key_scr[c] >= thr_b)
                return carry

            lax.fori_loop(0, n_chunks, sel_body, 0)

        @pl.when(n_tied_rows != 0)
        def _():
            n_gt = count(lambda a, r: a > r, thr)
            need_eq = (k_top - n_gt).astype(F32)
            incl = (lax.broadcasted_iota(I32, (IDX_SUB, IDX_SUB), 0)
                    <= lax.broadcasted_iota(I32, (IDX_SUB, IDX_SUB), 1)).astype(BF16)

            def sel_body(c, carry):
                blk = key_scr[c]
                eq = blk == thr_b
                eq_f = jnp.where(eq, 1.0, 0.0)
                rank = carry + jnp.dot(eq_f.astype(BF16), incl, preferred_element_type=F32)
                s_idx = c * IDX_SUB + lane_sub
                emit(c, ((blk > thr_b) | (eq & (rank <= need_eq))) & (s_idx <= t_col))
                return carry + jnp.sum(eq_f, axis=1, keepdims=True)

            lax.fori_loop(0, n_chunks, sel_body, jnp.zeros((Q_BLOCK, 1), F32))

        def fill_body(c, carry):
            far_ref[0, c] = jnp.full((Q_BLOCK, IDX_SUB), FAR_MASKED, far_ref.dtype)
            return carry

        lax.fori_loop(n_chunks, n_sub_total, fill_body, 0)

        def window(blk_idx):
            per = IDX_SUB // Q_BLOCK
            chunk = key_scr[blk_idx // per]
            m = blk_idx % per
            out = chunk[:, 0:Q_BLOCK]
            for q in range(1, per):
                out = jnp.where(m == q, chunk[:, q * Q_BLOCK:(q + 1) * Q_BLOCK], out)
            return out

        r_i = lax.broadcasted_iota(I32, (Q_BLOCK, Q_BLOCK), 0)
        c_i = lax.broadcasted_iota(I32, (Q_BLOCK, Q_BLOCK), 1)
        d_lo = Q_BLOCK + r_i - c_i
        d_hi = r_i - c_i
        near_lo = (window(jnp.maximum(i - 1, 0)) != 0) & (d_lo < MAX_DISTANCE) & (i >= 1)
        near_hi = (window(i) != 0) & (d_hi >= 0) & (d_hi < MAX_DISTANCE)
        near_ref[:, 0:Q_BLOCK] = jnp.where(near_lo, 0.0, -jnp.inf).astype(near_ref.dtype)
        near_ref[:, Q_BLOCK:2 * Q_BLOCK] = jnp.where(near_hi, 0.0, -jnp.inf).astype(near_ref.dtype)


def _idx_select(proj, ki, wi, k_top):
    L = ki.shape[0]
    nq = L // Q_BLOCK
    n_sub_total = L // IDX_SUB
    qi_l, kj_l, last_l = [], [], []
    for i in range(nq):
        j_last = (i * Q_BLOCK + Q_BLOCK - 1) // IDX_KEY_BLOCK
        for j in range(j_last + 1):
            qi_l.append(i)
            kj_l.append(j)
            last_l.append(1 if j == j_last else 0)
    tabs = [jnp.asarray(np.asarray(t, np.int32)) for t in (qi_l, kj_l, last_l)]
    qcol = (DSA_HEADS * DSA_DH) // (IDX_HEADS * IDX_DIM)
    group_keys = WORD_BITS * LANES
    n_groups_max = -(-L // group_keys)
    key_chunks = n_groups_max * (group_keys // IDX_SUB)
    vmem = ((key_chunks * IDX_SUB + (WORD_BITS + 2) * n_groups_max * LANES) * Q_BLOCK * 4
            + 2 * (Q_BLOCK * L * 2) + 2 * (Q_BLOCK * IDX_HEADS * IDX_DIM * 2)
            + 2 * IDX_KEY_BLOCK * IDX_DIM * 2 + (8 << 20))
    grid_spec = pltpu.PrefetchScalarGridSpec(
        num_scalar_prefetch=3,
        grid=(len(qi_l),),
        in_specs=[pl.BlockSpec((Q_BLOCK, IDX_HEADS * IDX_DIM), lambda s, qt, kt, lt: (qt[s], qcol)),
                  pl.BlockSpec((IDX_KEY_BLOCK, IDX_DIM), lambda s, qt, kt, lt: (kt[s], 0)),
                  pl.BlockSpec((Q_BLOCK, LANES), lambda s, qt, kt, lt: (qt[s], 0))],
        out_specs=[pl.BlockSpec((1, n_sub_total, Q_BLOCK, IDX_SUB), lambda s, qt, kt, lt: (qt[s], 0, 0, 0)),
                   pl.BlockSpec((Q_BLOCK, 2 * Q_BLOCK), lambda s, qt, kt, lt: (qt[s], 0))],
        scratch_shapes=[pltpu.VMEM((key_chunks, Q_BLOCK, IDX_SUB), I32),
                        pltpu.VMEM((WORD_BITS, n_groups_max, Q_BLOCK, LANES), I32),
                        pltpu.VMEM((n_groups_max, Q_BLOCK, LANES), I32),
                        pltpu.VMEM((n_groups_max, Q_BLOCK, LANES), I32)],
    )
    return pl.pallas_call(
        functools.partial(_idx_kernel, k_top=k_top, n_sub_total=n_sub_total),
        grid_spec=grid_spec,
        out_shape=[jax.ShapeDtypeStruct((nq, n_sub_total, Q_BLOCK, IDX_SUB), BF16),
                   jax.ShapeDtypeStruct((L, 2 * Q_BLOCK), BF16)],
        compiler_params=_cparams(("arbitrary",), vmem),
        name="dsa_idx_select",
    )(*tabs, proj, ki, wi)


M_INIT = -1e30
FAR_SUBS = 2
LOG2E = math.log2(math.e)


def _attn_kernel(qi_tab, kj_tab, kind_tab, first_tab,
                 tab_ref, q_ref, kt_ref, vf_ref, klo_ref, khi_ref, vlo_ref, vhi_ref, far_ref, near_ref,
                 o_ref, m_scr, l_scr, acc_scr, b_scr, s_scr):
    s = pl.program_id(0)

    @pl.when(s == 0)
    def _():
        r_i = lax.broadcasted_iota(I32, (Q_BLOCK, 2 * Q_BLOCK), 0)
        c_i = lax.broadcasted_iota(I32, (Q_BLOCK, 2 * Q_BLOCK), 1)
        d = jnp.maximum(Q_BLOCK + r_i - c_i, 0)
        max_exact = N_BUCKETS // 2
        df = jnp.maximum(d, 1).astype(F32)
        large = max_exact + (jnp.log(df / max_exact) / math.log(MAX_DISTANCE / max_exact)
                             * (N_BUCKETS - max_exact)).astype(I32)
        large = jnp.minimum(large, N_BUCKETS - 1)
        bkt = jnp.where(d < max_exact, d, large)
        for h in range(DSA_HEADS):
            acc = jnp.zeros((Q_BLOCK, 2 * Q_BLOCK), F32)
            for b in range(N_BUCKETS):
                acc = jnp.where(bkt == b, (tab_ref[b, h] - tab_ref[N_BUCKETS - 1, h]) * LOG2E, acc)
            b_scr[h] = acc

    @pl.when(first_tab[s] == 1)
    def _():
        m_scr[...] = jnp.full_like(m_scr, M_INIT)
        l_scr[...] = jnp.zeros_like(l_scr)
        acc_scr[...] = jnp.zeros_like(acc_scr)

    def softmax_pv(v_ref, width, row0=0):
        nt = width // LANES
        rows = slice(row0, row0 + width)
        ones = jnp.ones((width, DSA_DH), BF16)
        for g in range(DSA_KV_HEADS):
            heads = range(g * DSA_GROUP, (g + 1) * DSA_GROUP)
            alphas, pbs = [], []
            for h in heads:
                tiles = [s_scr[h, :, t * LANES:(t + 1) * LANES] for t in range(nt)]
                tmax = tiles[0]
                for t in range(1, nt):
                    tmax = jnp.maximum(tmax, tiles[t])
                m_prev = m_scr[h]
                m_new = jnp.maximum(m_prev, jnp.max(tmax, axis=-1, keepdims=True))
                alphas.append(jnp.exp2(m_prev - m_new))
                p = [jnp.exp2(tiles[t] - m_new).astype(BF16) for t in range(nt)]
                pbs.append(jnp.concatenate(p, axis=-1) if nt > 1 else p[0])
                m_scr[h] = m_new
            v_aug = jnp.concatenate([v_ref[rows, g * DSA_DH:(g + 1) * DSA_DH], ones], axis=-1)
            pv = jnp.dot(jnp.concatenate(pbs, axis=0), v_aug,
                         preferred_element_type=F32)
            for n, h in enumerate(heads):
                pv_h = pv[n * Q_BLOCK:(n + 1) * Q_BLOCK]
                acc_scr[h] = alphas[n] * acc_scr[h] + pv_h[:, :DSA_DH]
                l_scr[h] = alphas[n] * l_scr[h] + pv_h[:, DSA_DH:]

    @pl.when(kind_tab[s] == 0)
    def _():
        eye = (lax.broadcasted_iota(I32, (Q_BLOCK, Q_BLOCK), 0)
               == lax.broadcasted_iota(I32, (Q_BLOCK, Q_BLOCK), 1)).astype(BF16)
        for sub in range(FAR_SUBS):
            cols = slice(sub * IDX_SUB, (sub + 1) * IDX_SUB)
            mask = far_ref[0, sub]
            for g in range(DSA_KV_HEADS):
                lhs = jnp.concatenate(
                    [jnp.concatenate([q_ref[:, h * DSA_DH:(h + 1) * DSA_DH], eye], axis=1)
                     for h in range(g * DSA_GROUP, (g + 1) * DSA_GROUP)], axis=0)
                rhs = jnp.concatenate([kt_ref[g * DSA_DH:(g + 1) * DSA_DH, cols], mask], axis=0)
                logits = jnp.dot(lhs, rhs, preferred_element_type=F32)
                for hh in range(DSA_GROUP):
                    s_scr[g * DSA_GROUP + hh] = logits[hh * Q_BLOCK:(hh + 1) * Q_BLOCK, :]
            softmax_pv(vf_ref, IDX_SUB, sub * IDX_SUB)

    @pl.when(kind_tab[s] == 1)
    def _():
        for half, (k_ref, v_ref) in enumerate(((klo_ref, vlo_ref), (khi_ref, vhi_ref))):
            cols = slice(half * Q_BLOCK, (half + 1) * Q_BLOCK)
            mask = near_ref[:, cols].astype(F32)
            for h in range(DSA_HEADS):
                g = h // DSA_GROUP
                logits = lax.dot_general(q_ref[:, h * DSA_DH:(h + 1) * DSA_DH],
                                         k_ref[:, g * DSA_DH:(g + 1) * DSA_DH], _NT, preferred_element_type=F32)
                s_scr[h, :, 0:Q_BLOCK] = logits + b_scr[h][:, cols] + mask
            softmax_pv(v_ref, Q_BLOCK)
        for h in range(DSA_HEADS):
            o_ref[:, h * DSA_DH:(h + 1) * DSA_DH] = (acc_scr[h] / l_scr[h]).astype(o_ref.dtype)


def _masked_attention(proj, far, near, rel_bias):
    L = proj.shape[0]
    nq = L // Q_BLOCK
    far_keys = FAR_SUBS * IDX_SUB
    per = far_keys // Q_BLOCK
    qi_l, kj_l, kind_l, first_l = [], [], [], []
    for i in range(nq):
        n_far = -(-i // per)
        for j in range(n_far):
            qi_l.append(i); kj_l.append(j); kind_l.append(0); first_l.append(1 if j == 0 else 0)
        qi_l.append(i); kj_l.append(max(n_far - 1, 0)); kind_l.append(1); first_l.append(1 if n_far == 0 else 0)
    tabs = [jnp.asarray(np.asarray(t, np.int32)) for t in (qi_l, kj_l, kind_l, first_l)]
    qw = DSA_HEADS * DSA_DH
    kvw = DSA_KV_HEADS * DSA_DH
    k_col = (2 * qw) // kvw
    v_col = k_col + 1
    hw = DSA_HEADS
    vmem = (2 * (Q_BLOCK * qw * 2 * 2 + 2 * far_keys * kvw * 2 + 4 * Q_BLOCK * kvw * 2
                 + Q_BLOCK * far_keys * 2 + Q_BLOCK * 2 * Q_BLOCK * 2)
            + hw * Q_BLOCK * (3 * LANES + 2 * Q_BLOCK) * 4 + (16 << 20))
    idx = lambda f: (lambda s, qt, kt, kd, ft: f(qt[s], kt[s]))
    k_t = proj[:, k_col * kvw:(k_col + 1) * kvw].T
    grid_spec = pltpu.PrefetchScalarGridSpec(
        num_scalar_prefetch=4,
        grid=(len(qi_l),),
        in_specs=[pl.BlockSpec(memory_space=pltpu.SMEM),
                  pl.BlockSpec((Q_BLOCK, qw), idx(lambda i, j: (i, 0))),
                  pl.BlockSpec((kvw, far_keys), idx(lambda i, j: (0, j))),
                  pl.BlockSpec((far_keys, kvw), idx(lambda i, j: (j, v_col))),
                  pl.BlockSpec((Q_BLOCK, kvw), idx(lambda i, j: (jnp.maximum(i - 1, 0), k_col))),
                  pl.BlockSpec((Q_BLOCK, kvw), idx(lambda i, j: (i, k_col))),
                  pl.BlockSpec((Q_BLOCK, kvw), idx(lambda i, j: (jnp.maximum(i - 1, 0), v_col))),
                  pl.BlockSpec((Q_BLOCK, kvw), idx(lambda i, j: (i, v_col))),
                  pl.BlockSpec((1, FAR_SUBS, Q_BLOCK, IDX_SUB), idx(lambda i, j: (i, j, 0, 0))),
                  pl.BlockSpec((Q_BLOCK, 2 * Q_BLOCK), idx(lambda i, j: (i, 0)))],
        out_specs=pl.BlockSpec((Q_BLOCK, qw), idx(lambda i, j: (i, 0))),
        scratch_shapes=[pltpu.VMEM((hw, Q_BLOCK, LANES), F32),
                        pltpu.VMEM((hw, Q_BLOCK, LANES), F32),
                        pltpu.VMEM((hw, Q_BLOCK, DSA_DH), F32),
                        pltpu.VMEM((hw, Q_BLOCK, 2 * Q_BLOCK), F32),
                        pltpu.VMEM((hw, Q_BLOCK, IDX_SUB), F32)],
    )
    return pl.pallas_call(
        _attn_kernel,
        grid_spec=grid_spec,
        out_shape=jax.ShapeDtypeStruct((L, qw), BF16),
        compiler_params=_cparams(("arbitrary",), vmem),
        name="dsa_attention",
    )(*tabs, rel_bias, proj, k_t, proj, proj, proj, proj, proj, far, near)


def _dsa_mixer(xb, w_in_all, layer, ln_g, ln_b, rel_bias):
    L = xb.shape[0]
    k_top = min(TOPK_MAX, L // 4)
    sq = DSA_HEADS * DSA_DH
    skv = DSA_KV_HEADS * DSA_DH
    si = IDX_HEADS * IDX_DIM
    w_ki = w_in_all[layer, :, sq + 2 * skv + si:sq + 2 * skv + si + IDX_DIM]
    w_wi = w_in_all[layer, :, sq + 2 * skv + si + IDX_DIM:]
    colscale = jnp.concatenate([jnp.full((1, sq), DSA_DH ** -0.5 * LOG2E, F32),
                                jnp.ones((1, si + 2 * skv), F32)], axis=1)
    w_small = jnp.concatenate([w_ki, w_wi, jnp.zeros((D_MODEL, LANES - IDX_HEADS), F32)], axis=1).astype(BF16)
    tn = 2 * skv
    nq, ni = sq // tn, si // tn
    src_block = lambda j: jnp.where(j < nq, j, jnp.where(j < nq + ni, j + 1, nq))
    proj = _matmul_scaled(xb, jnp.swapaxes(w_in_all, 1, 2), colscale, BF16, tm=TILE_M, tn=tn,
                          w_block=lambda j: (layer, 0, src_block(j)), w_t=True)
    ki, wi = _dsa_small(xb, w_small, ln_g, ln_b, tm=TILE_M)
    far, near = _idx_select(proj, ki, wi, k_top)
    return _masked_attention(proj, far, near, rel_bias)


def kernel(x, p, gdn_w_in, gdn_conv_w, gdn_a_log, gdn_dt_bias, gdn_norm_g, gdn_w_o, dsa_w_in, dsa_kidx_ln_g, dsa_kidx_ln_b, dsa_w_o, rel_bias, ln1_g, ln1_b, ffn_w_gate, ffn_w_up, ffn_conv_w, ffn_w_down, ln2_g, ln2_b, ple_w_proj, ple_w_gate):
    assert x.shape[0] == 1 and x.shape[2] == D_MODEL
    xf = x[0]
    xb = xf.astype(BF16)
    ple_gate_bf = ple_w_gate.astype(BF16)
    ffn_down_bf = ffn_w_down.astype(BF16)
    gdn_o_bf = gdn_w_o.astype(BF16)
    dsa_o_bf = dsa_w_o.astype(BF16)
    ia = ib = 0
    for i in range(DEPTH):
        if i % 2 == 0:
            mix = _gdn_mixer(xb, gdn_w_in, ia, gdn_conv_w[ia], gdn_a_log[ia], gdn_dt_bias[ia], gdn_norm_g[ia])
            w_o, lo = gdn_o_bf, ia
            ia += 1
        else:
            mix = _dsa_mixer(xb, dsa_w_in, ib, dsa_kidx_ln_g[ib], dsa_kidx_ln_b[ib], rel_bias)
            w_o, lo = dsa_o_bf, ib
            ib += 1
        xf, xb = _proj_res_ln(mix, w_o, lo, xf, ln1_g[i], ln1_b[i], tm=LN_TILE_M, sub=LN_SUB_M)
        hmid = _ffn_up(xb, ffn_w_gate, ffn_w_up, ffn_conv_w[i], i, tm=TILE_M, tn=FFN_TILE_N)
        xf, xb = _proj_res_ln(hmid, ffn_down_bf, i, xf, ln2_g[i], ln2_b[i], tm=LN_TILE_M, sub=LN_SUB_M)
        xf, xb = _ple(xb, xf, ple_gate_bf, p, ple_w_proj, i, tm=TILE_M, tn=TILE_N)
    return xf[None]
```

```python
import functools
import math

import jax
import jax.numpy as jnp
import numpy as np
from jax import lax
from jax.experimental import pallas as pl
from jax.experimental.pallas import tpu as pltpu

F32 = jnp.float32
BF16 = jnp.bfloat16
I32 = jnp.int32

D_MODEL = 2048
GDN_QK_HEADS = 16
GDN_V_HEADS = 32
GDN_DK = 128
GDN_DV = 128
GDN_CONV = 4
GDN_CHUNK = 64
GDN_QK_W = GDN_QK_HEADS * GDN_DK
GDN_V_W = GDN_V_HEADS * GDN_DV
DSA_HEADS = 16
DSA_KV_HEADS = 4
DSA_GROUP = DSA_HEADS // DSA_KV_HEADS
DSA_DH = 128
IDX_HEADS = 16
IDX_DIM = 128
TOPK_MAX = 256
N_BUCKETS = 32
MAX_DISTANCE = 128
D_FF = 5120
FFN_CONV = 3
PLE_DIM = 256
DEPTH = 2
DN_ALPHA = (2.0 * DEPTH) ** 0.25
LN_EPS = 1e-5
RMS_EPS = 1e-6

V7X_VMEM_BYTES = 64 * 1024 * 1024
V7X_VMEM_BUDGET = 56 * 1024 * 1024
LANES = 128
BF16_SUBLANES = 16

TILE_M = 1024
TILE_N = 1024
FFN_TILE_N = 512
LN_TILE_M = 512
LN_SUB_M = 256
PLE_TILE_M = 256

HALO = BF16_SUBLANES
Q_BLOCK = 128
IDX_KEY_BLOCK = 2048
IDX_SUB = 512
INT_MIN = -(2 ** 31)
BITS_PER_CHECK = 4
WORD_BITS = 32
FAR_MASKED = -2e30

_NT = (((1,), (1,)), ((), ()))
_TN = (((0,), (0,)), ((), ()))


def _cparams(semantics, vmem_bytes):
    return pltpu.CompilerParams(dimension_semantics=semantics,
                                vmem_limit_bytes=int(min(V7X_VMEM_BUDGET, vmem_bytes)))


def _silu(y):
    return y * jax.nn.sigmoid(y)


def _mm_scale_kernel(x_ref, w_ref, cs_ref, o_ref, *, w_t):
    w = w_ref[...].astype(BF16)
    acc = (lax.dot_general(x_ref[...], w, _NT, preferred_element_type=F32) if w_t
           else jnp.dot(x_ref[...], w, preferred_element_type=F32))
    o_ref[...] = (acc * cs_ref[...]).astype(o_ref.dtype)


def _weight_spec(w, K, tn, w_block, w_t=False):
    if w_block is None:
        return pl.BlockSpec((K, tn), lambda i, j: (0, j))
    if w_t:
        return pl.BlockSpec((None, tn, K), lambda i, j: (w_block(j)[0], w_block(j)[2], 0))
    return pl.BlockSpec((None, K, tn), lambda i, j: w_block(j))


def _matmul_scaled(x, w, colscale, out_dtype, tm, tn, w_block=None, w_t=False):
    M, K = x.shape
    N = colscale.shape[1]
    tm, tn = min(tm, M), min(tn, N)
    osz = jnp.dtype(out_dtype).itemsize
    wsz = jnp.dtype(w.dtype).itemsize
    vmem = 2 * (tm * K * 2 + K * tn * wsz + tm * tn * osz) + K * tn * 2 + 2 * tm * tn * 4
    return pl.pallas_call(
        functools.partial(_mm_scale_kernel, w_t=w_t),
        grid=(M // tm, N // tn),
        in_specs=[pl.BlockSpec((tm, K), lambda i, j: (i, 0)),
                  _weight_spec(w, K, tn, w_block, w_t),
                  pl.BlockSpec((1, tn), lambda i, j: (0, j))],
        out_specs=pl.BlockSpec((tm, tn), lambda i, j: (i, j)),
        out_shape=jax.ShapeDtypeStruct((M, N), out_dtype),
        compiler_params=_cparams(("parallel", "parallel"), vmem),
        name="matmul_scaled",
    )(x, w, colscale)


CONV_SUB = 256
FFN_SUB = 512


def _causal_conv(g, cw_ref, g_scr, carry_scr, slot, kc, tm, cols):
    g_scr[0:HALO, cols] = carry_scr[slot]
    g_scr[HALO:HALO + tm, cols] = g
    carry_scr[slot] = g[tm - HALO:tm, :]
    y = cw_ref[kc - 1:kc, cols] * g
    for j in range(kc - 1):
        off = HALO - (kc - 1) + j
        y = y + cw_ref[j:j + 1, cols] * g_scr[off:off + tm, cols]
    return y


def _init_carry(carry_scr):
    @pl.when((pl.program_id(0) == 0) & (pl.program_id(1) == 0))
    def _():
        carry_scr[...] = jnp.zeros_like(carry_scr)


def _mm_conv_silu_kernel(x_ref, w_ref, cw_ref, o_ref, g_scr, carry_scr, *, kc, tm, w_t):
    _init_carry(carry_scr)
    nsub = o_ref.shape[1] // CONV_SUB
    for n in range(nsub):
        cols = slice(n * CONV_SUB, (n + 1) * CONV_SUB)
        if w_t:
            g = lax.dot_general(x_ref[...], w_ref[cols, :].astype(BF16), _NT, preferred_element_type=F32)
        else:
            g = jnp.dot(x_ref[...], w_ref[:, cols].astype(BF16), preferred_element_type=F32)
        y = _causal_conv(g, cw_ref, g_scr, carry_scr, pl.program_id(1) * nsub + n, kc, tm, cols)
        o_ref[:, cols] = _silu(y).astype(o_ref.dtype)


def _proj_conv_silu(x, w, conv_w, tm, tn, w_block=None, w_t=False):
    M, K = x.shape
    kc, N = conv_w.shape
    tm, tn = min(tm, M), min(tn, N)
    wsz = jnp.dtype(w.dtype).itemsize
    vmem = (2 * (tm * K * 2 + K * tn * wsz + tm * tn * 2) + K * tn * 2 + 4 * tm * tn * 4
            + HALO * N * 4)
    return pl.pallas_call(
        functools.partial(_mm_conv_silu_kernel, kc=kc, tm=tm, w_t=w_t),
        grid=(M // tm, N // tn),
        in_specs=[pl.BlockSpec((tm, K), lambda i, j: (i, 0)),
                  _weight_spec(w, K, tn, w_block, w_t),
                  pl.BlockSpec((kc, tn), lambda i, j: (0, j))],
        out_specs=pl.BlockSpec((tm, tn), lambda i, j: (i, j)),
        out_shape=jax.ShapeDtypeStruct((M, N), BF16),
        scratch_shapes=[pltpu.VMEM((tm + HALO, tn), F32),
                        pltpu.VMEM((N // CONV_SUB, HALO, CONV_SUB), F32)],
        compiler_params=_cparams(("arbitrary", "arbitrary"), vmem),
        name="proj_conv_silu",
    )(x, w, conv_w)


def _ffn_up_kernel(x_ref, wg_ref, wu_ref, cw_ref, o_ref, g_scr, carry_scr, *, kc, tm):
    _init_carry(carry_scr)
    nsub = o_ref.shape[1] // FFN_SUB
    for n in range(nsub):
        cols = slice(n * FFN_SUB, (n + 1) * FFN_SUB)
        g = jnp.dot(x_ref[...], wg_ref[:, cols].astype(BF16), preferred_element_type=F32)
        u = jnp.dot(x_ref[...], wu_ref[:, cols].astype(BF16), preferred_element_type=F32)
        y = _causal_conv(g, cw_ref, g_scr, carry_scr, pl.program_id(1) * nsub + n, kc, tm, cols)
        o_ref[:, cols] = (_silu(y) * u).astype(o_ref.dtype)


def _ffn_up(x, w_gate, w_up, conv_w, layer, tm, tn):
    M, K = x.shape
    N = w_gate.shape[2]
    kc = conv_w.shape[0]
    tm, tn = min(tm, M), min(tn, N)
    wsz = jnp.dtype(w_gate.dtype).itemsize
    vmem = (2 * (tm * K * 2 + 2 * K * tn * wsz + tm * tn * 2) + 2 * K * tn * 2 + 6 * tm * tn * 4
            + HALO * N * 4)
    return pl.pallas_call(
        functools.partial(_ffn_up_kernel, kc=kc, tm=tm),
        grid=(M // tm, N // tn),
        in_specs=[pl.BlockSpec((tm, K), lambda i, j: (i, 0)),
                  pl.BlockSpec((None, K, tn), lambda i, j: (layer, 0, j)),
                  pl.BlockSpec((None, K, tn), lambda i, j: (layer, 0, j)),
                  pl.BlockSpec((kc, tn), lambda i, j: (0, j))],
        out_specs=pl.BlockSpec((tm, tn), lambda i, j: (i, j)),
        out_shape=jax.ShapeDtypeStruct((M, N), BF16),
        scratch_shapes=[pltpu.VMEM((tm + HALO, tn), F32),
                        pltpu.VMEM((N // FFN_SUB, HALO, FFN_SUB), F32)],
        compiler_params=_cparams(("arbitrary", "arbitrary"), vmem),
        name="ffn_up",
    )(x, w_gate, w_up, conv_w)


def _mm_res_ln_kernel(a_ref, w_ref, res_ref, g_ref, b_ref, of_ref, ob_ref, *, sub):
    for r0 in range(0, a_ref.shape[0], sub):
        rows = slice(r0, r0 + sub)
        acc = jnp.dot(a_ref[rows, :], w_ref[...], preferred_element_type=F32)
        y = DN_ALPHA * res_ref[rows, :] + acc
        mu = jnp.mean(y, axis=-1, keepdims=True)
        yc = y - mu
        var = jnp.mean(yc * yc, axis=-1, keepdims=True)
        out = yc * lax.rsqrt(var + LN_EPS) * g_ref[...] + b_ref[...]
        of_ref[rows, :] = out
        ob_ref[rows, :] = out.astype(BF16)


def _proj_res_ln(a, w, layer, res, g, b, tm, sub):
    M, K = a.shape
    N = w.shape[2]
    tm = min(tm, M)
    sub = min(sub, tm)
    vmem = K * N * 2 + 2 * (tm * K * 2 + tm * N * 4 + tm * N * 4 + tm * N * 2) + 4 * sub * N * 4
    return pl.pallas_call(
        functools.partial(_mm_res_ln_kernel, sub=sub),
        grid=(M // tm,),
        in_specs=[pl.BlockSpec((tm, K), lambda i: (i, 0)),
                  pl.BlockSpec((None, K, N), lambda i: (layer, 0, 0), pipeline_mode=pl.Buffered(1)),
                  pl.BlockSpec((tm, N), lambda i: (i, 0)),
                  pl.BlockSpec((1, N), lambda i: (0, 0)),
                  pl.BlockSpec((1, N), lambda i: (0, 0))],
        out_specs=[pl.BlockSpec((tm, N), lambda i: (i, 0)),
                   pl.BlockSpec((tm, N), lambda i: (i, 0))],
        out_shape=[jax.ShapeDtypeStruct((M, N), F32), jax.ShapeDtypeStruct((M, N), BF16)],
        compiler_params=_cparams(("parallel",), vmem),
        name="proj_res_ln",
    )(a, w, res, g.reshape(1, N), b.reshape(1, N))


def _ple_kernel(xb_ref, wg_ref, p_ref, wp_ref, xr_ref, of_ref, ob_ref):
    gate = jax.nn.sigmoid(jnp.dot(xb_ref[...], wg_ref[...].astype(BF16), preferred_element_type=F32))
    pe = jnp.dot(p_ref[...].astype(BF16), wp_ref[...].astype(BF16), preferred_element_type=F32)
    out = xr_ref[...] + gate * pe
    of_ref[...] = out
    ob_ref[...] = out.astype(BF16)


def _ple(xb, xf, w_gate, p, w_proj, layer, tm, tn):
    M, K = xb.shape
    N = w_gate.shape[2]
    P = p.shape[3]
    tm, tn = min(tm, M), min(tn, N)
    wsz = jnp.dtype(w_gate.dtype).itemsize
    vmem = 2 * (tm * K * 2 + K * tn * wsz + tm * P * 4 + P * tn * 4 + tm * tn * 10) + K * tn * 2 + 4 * tm * tn * 4
    return pl.pallas_call(
        _ple_kernel,
        grid=(M // tm, N // tn),
        in_specs=[pl.BlockSpec((tm, K), lambda i, j: (i, 0)),
                  pl.BlockSpec((None, K, tn), lambda i, j: (layer, 0, j)),
                  pl.BlockSpec((None, None, tm, P), lambda i, j: (layer, 0, i, 0)),
                  pl.BlockSpec((None, P, tn), lambda i, j: (layer, 0, j)),
                  pl.BlockSpec((tm, tn), lambda i, j: (i, j))],
        out_specs=[pl.BlockSpec((tm, tn), lambda i, j: (i, j)),
                   pl.BlockSpec((tm, tn), lambda i, j: (i, j))],
        out_shape=[jax.ShapeDtypeStruct((M, N), F32), jax.ShapeDtypeStruct((M, N), BF16)],
        compiler_params=_cparams(("parallel", "parallel"), vmem),
        name="ple",
    )(xb, w_gate, p, w_proj, xf)


GDN_HB = 8
GDN_NC = 8
GDN_STAGE_UNITS = 64


def _gdn_kernel(q_ref, k_ref, v_ref, z_ref, sc_ref, hp_ref, ng_ref, o_ref, s_scr):
    C = GDN_CHUNK

    @pl.when(pl.program_id(1) == 0)
    def _():
        s_scr[...] = jnp.zeros_like(s_scr)

    row = lax.broadcasted_iota(I32, (C, C), 0)
    col = lax.broadcasted_iota(I32, (C, C), 1)
    tri = row >= col
    strict = row > col
    eye = row == col
    tri_f = tri.astype(F32)
    eye_f = eye.astype(F32)

    raw = sc_ref[...]
    a_log = hp_ref[0, 0:1, :]
    dt_b = hp_ref[0, 1:2, :]
    xs = raw + dt_b
    softplus = jnp.maximum(xs, 0.0) + jnp.log1p(jnp.exp(-jnp.abs(xs)))
    g_all = -jnp.exp(a_log) * softplus
    beta_all = jax.nn.sigmoid(raw)
    ng = ng_ref[...]

    units = [(c, j) for c in range(GDN_NC) for j in range(GDN_HB)]
    kb_l, rhs_l, decay_l, qd_l, kd_l, kbf_l, qbf_l, gl_l = [], [], [], [], [], [], [], []
    for c in range(GDN_NC):
        r0 = c * C
        gc = jnp.dot(tri_f, g_all[r0:r0 + C, :], precision=lax.Precision.HIGHEST,
                     preferred_element_type=F32)
        g_last = gc[C - 1:C, :]
        e_gc = jnp.exp(gc)
        e_rest = jnp.exp(g_last - gc)
        e_last = jnp.exp(g_last)
        beta_c = beta_all[r0:r0 + C, :]
        qn, kn = [], []
        for hq in range(GDN_HB // 2):
            qf = q_ref[r0:r0 + C, hq * GDN_DK:(hq + 1) * GDN_DK].astype(F32)
            kf = k_ref[r0:r0 + C, hq * GDN_DK:(hq + 1) * GDN_DK].astype(F32)
            qn.append(qf * lax.rsqrt(jnp.sum(qf * qf, axis=-1, keepdims=True) + RMS_EPS) * (GDN_DK ** -0.5))
            kn.append(kf * lax.rsqrt(jnp.sum(kf * kf, axis=-1, keepdims=True) + RMS_EPS))
        for j in range(GDN_HB):
            q_h, k_h = qn[j // 2], kn[j // 2]
            vf = v_ref[r0:r0 + C, j * GDN_DV:(j + 1) * GDN_DV].astype(F32)
            beta = beta_c[:, GDN_HB + j:GDN_HB + j + 1]
            kb = k_h * beta
            gcb = jnp.broadcast_to(gc[:, j:j + 1], (C, C))
            gcr = jnp.sum(jnp.where(eye, gcb, 0.0), axis=0, keepdims=True)
            decay_l.append(jnp.where(tri, jnp.exp(jnp.where(tri, gcb - gcr, 0.0)), 0.0))
            kb_l.append(kb.astype(BF16))
            rhs_l.append(jnp.concatenate([vf * beta, kb * e_gc[:, j:j + 1]], axis=-1).astype(BF16))
            qd_l.append((q_h * e_gc[:, j:j + 1]).astype(BF16))
            kd_l.append((k_h * e_rest[:, j:j + 1]).astype(BF16))
            kbf_l.append(k_h.astype(BF16))
            qbf_l.append(q_h.astype(BF16))
            gl_l.append(e_last[:, j:j + 1])

    n_u = len(units)
    qk_l, sol_l = [], []
    for b0 in range(0, n_u, GDN_STAGE_UNITS):
        us = range(b0, min(b0 + GDN_STAGE_UNITS, n_u))
        kk_b = [lax.dot_general(kb_l[u], kbf_l[u], _NT, preferred_element_type=F32) for u in us]
        qk_b = [lax.dot_general(qbf_l[u], kbf_l[u], _NT, preferred_element_type=F32) for u in us]
        qk_l += [jnp.where(tri, qk * decay_l[u], 0.0).astype(BF16) for qk, u in zip(qk_b, us)]
        x_b = [(-jnp.where(strict, kk * decay_l[u], 0.0)) for kk, u in zip(kk_b, us)]
        t_b = [eye_f + x for x in x_b]
        x_b = [x.astype(BF16) for x in x_b]
        for _ in range(5):
            x_b = [jnp.dot(x, x, preferred_element_type=F32).astype(BF16) for x in x_b]
            t_b = [t + jnp.dot(t.astype(BF16), x, preferred_element_type=F32) for t, x in zip(t_b, x_b)]
        sol_l += [jnp.dot(t.astype(BF16), rhs_l[u], preferred_element_type=F32) for t, u in zip(t_b, us)]

    s_cur = [s_scr[j] for j in range(GDN_HB)]
    for c in range(GDN_NC):
        r0 = c * C
        us = [c * GDN_HB + j for j in range(GDN_HB)]
        s_bf = [s.astype(BF16) for s in s_cur]
        ws_l = [jnp.dot(sol_l[u][:, GDN_DV:].astype(BF16), s_bf[j], preferred_element_type=F32)
                for j, u in enumerate(us)]
        qs_l = [jnp.dot(qd_l[u], s_bf[j], preferred_element_type=F32) for j, u in enumerate(us)]
        vn_l = [(sol_l[u][:, :GDN_DV] - ws_l[j]).astype(BF16) for j, u in enumerate(us)]
        kv_l = [lax.dot_general(kd_l[u], vn_l[j], _TN, preferred_element_type=F32) for j, u in enumerate(us)]
        ov_l = [jnp.dot(qk_l[u], vn_l[j], preferred_element_type=F32) for j, u in enumerate(us)]
        s_cur = [s_cur[j] * gl_l[u] + kv_l[j] for j, u in enumerate(us)]
        for j in range(GDN_HB):
            o = qs_l[j] + ov_l[j]
            zf = z_ref[r0:r0 + C, j * GDN_DV:(j + 1) * GDN_DV].astype(F32)
            o = o * lax.rsqrt(jnp.mean(o * o, axis=-1, keepdims=True) + RMS_EPS) * ng * _silu(zf)
            o_ref[r0:r0 + C, j * GDN_DV:(j + 1) * GDN_DV] = o.astype(o_ref.dtype)
    for j in range(GDN_HB):
        s_scr[j] = s_cur[j]


def _gdn_core(qkv, z, scal, hparams, norm_g):
    L = qkv.shape[0]
    G = GDN_V_HEADS // GDN_HB
    R = GDN_NC * GDN_CHUNK
    qw = (GDN_HB // 2) * GDN_DK
    vw = GDN_HB * GDN_DV
    k_blk0 = GDN_QK_W // qw
    v_blk0 = 2 * GDN_QK_W // vw
    vmem = (2 * (2 * R * qw * 2 + 2 * R * vw * 2 + R * LANES * 4 + R * vw * 2)
            + GDN_NC * GDN_HB * (512 << 10) + (8 << 20))
    return pl.pallas_call(
        _gdn_kernel,
        grid=(G, L // R),
        in_specs=[pl.BlockSpec((R, qw), lambda g, s: (s, g)),
                  pl.BlockSpec((R, qw), lambda g, s: (s, k_blk0 + g)),
                  pl.BlockSpec((R, vw), lambda g, s: (s, v_blk0 + g)),
                  pl.BlockSpec((R, vw), lambda g, s: (s, g)),
                  pl.BlockSpec((R, LANES), lambda g, s: (s, g)),
                  pl.BlockSpec((1, 8, LANES), lambda g, s: (g, 0, 0)),
                  pl.BlockSpec((1, GDN_DV), lambda g, s: (0, 0))],
        out_specs=pl.BlockSpec((R, vw), lambda g, s: (s, g)),
        out_shape=jax.ShapeDtypeStruct((L, GDN_V_W), BF16),
        scratch_shapes=[pltpu.VMEM((GDN_HB, GDN_DK, GDN_DV), F32)],
        compiler_params=_cparams(("parallel", "arbitrary"), vmem),
        name="gdn_core",
    )(qkv, qkv, qkv, z, scal, hparams, norm_g.reshape(1, GDN_DV))


def _gdn_mixer(xb, w_in_all, layer, conv_w, a_log, dt_bias, norm_g):
    nqkv = 2 * GDN_QK_W + GDN_V_W
    tn = TILE_N
    w_ab = w_in_all[layer, :, nqkv + GDN_V_W:]
    G = GDN_V_HEADS // GDN_HB
    w_a = w_ab[:, :GDN_V_HEADS].reshape(D_MODEL, G, GDN_HB)
    w_b = w_ab[:, GDN_V_HEADS:].reshape(D_MODEL, G, GDN_HB)
    w_sc = jnp.concatenate([w_a, w_b, jnp.zeros((D_MODEL, G, LANES - 2 * GDN_HB), F32)], axis=-1)
    w_sc = w_sc.reshape(D_MODEL, G * LANES).astype(BF16)
    w_t_all = jnp.swapaxes(w_in_all, 1, 2)
    qkv = _proj_conv_silu(xb, w_t_all, conv_w, tm=TILE_M, tn=tn, w_block=lambda j: (layer, 0, j), w_t=True)
    ones_z = jnp.ones((1, GDN_V_W), F32)
    z = _matmul_scaled(xb, w_t_all, ones_z, BF16, tm=TILE_M, tn=tn,
                       w_block=lambda j: (layer, 0, nqkv // tn + j), w_t=True)
    scal = _matmul_scaled(xb, w_sc, jnp.ones((1, G * LANES), F32), F32, tm=TILE_M, tn=G * LANES)
    hp = jnp.zeros((G, 8, LANES), F32)
    hp = hp.at[:, 0, :GDN_HB].set(a_log.reshape(G, GDN_HB))
    hp = hp.at[:, 1, :GDN_HB].set(dt_bias.reshape(G, GDN_HB))
    return _gdn_core(qkv, z, scal, hp, norm_g)


def _dsa_small_kernel(x_ref, w_ref, g_ref, b_ref, ki_ref, wi_ref):
    acc = jnp.dot(x_ref[...], w_ref[...], preferred_element_type=F32)
    ki = acc[:, :IDX_DIM]
    mu = jnp.mean(ki, axis=-1, keepdims=True)
    kc = ki - mu
    var = jnp.mean(kc * kc, axis=-1, keepdims=True)
    ki_ref[...] = (kc * lax.rsqrt(var + LN_EPS) * g_ref[...] + b_ref[...]).astype(ki_ref.dtype)
    wi_ref[...] = acc[:, IDX_DIM:] * ((IDX_HEADS ** -0.5) * (IDX_DIM ** -0.5))


def _dsa_small(xb, w_small, ln_g, ln_b, tm):
    M, K = xb.shape
    tm = min(tm, M)
    N = 2 * LANES
    vmem = 2 * (tm * K * 2 + K * N * 2 + tm * LANES * 6) + 4 * tm * N * 4
    return pl.pallas_call(
        _dsa_small_kernel,
        grid=(M // tm,),
        in_specs=[pl.BlockSpec((tm, K), lambda i: (i, 0)),
                  pl.BlockSpec((K, N), lambda i: (0, 0)),
                  pl.BlockSpec((1, IDX_DIM), lambda i: (0, 0)),
                  pl.BlockSpec((1, IDX_DIM), lambda i: (0, 0))],
        out_specs=[pl.BlockSpec((tm, IDX_DIM), lambda i: (i, 0)),
                   pl.BlockSpec((tm, LANES), lambda i: (i, 0))],
        out_shape=[jax.ShapeDtypeStruct((M, IDX_DIM), BF16), jax.ShapeDtypeStruct((M, LANES), F32)],
        compiler_params=_cparams(("parallel",), vmem),
        name="dsa_idx_proj",
    )(xb, w_small, ln_g.reshape(1, IDX_DIM), ln_b.reshape(1, IDX_DIM))


def _sortable_key(score):
    bits = lax.bitcast_convert_type(score, I32)
    return jnp.where(bits >= 0, bits, bits ^ jnp.int32(0x7FFFFFFF))


def _idx_kernel(qi_tab, kj_tab, last_tab,
                qidx_ref, kidx_ref, wi_ref, far_ref, near_ref, key_scr, plane_scr, cand_scr, w_scr,
                *, k_top, n_sub_total):
    s = pl.program_id(0)
    i = qi_tab[s]
    j = kj_tab[s]

    @pl.when(s == 0)
    def _():
        key_scr[...] = jnp.full_like(key_scr, INT_MIN)
        plane_scr[...] = jnp.zeros_like(plane_scr)
    nsub = IDX_KEY_BLOCK // IDX_SUB
    t_col = i * Q_BLOCK + lax.broadcasted_iota(I32, (Q_BLOCK, 1), 0)
    lane_sub = lax.broadcasted_iota(I32, (Q_BLOCK, IDX_SUB), 1)
    wi = wi_ref[...]

    qi_rows = jnp.concatenate([qidx_ref[:, h * IDX_DIM:(h + 1) * IDX_DIM] for h in range(IDX_HEADS)], axis=0)
    for sub in range(nsub):
        ki_sub = kidx_ref[sub * IDX_SUB:(sub + 1) * IDX_SUB, :]
        sc_all = lax.dot_general(qi_rows, ki_sub, _NT, preferred_element_type=F32)
        acc = jnp.zeros((Q_BLOCK, IDX_SUB), F32)
        for h in range(IDX_HEADS):
            acc = acc + jnp.maximum(sc_all[h * Q_BLOCK:(h + 1) * Q_BLOCK, :], 0.0) * wi[:, h:h + 1]
        s_idx = j * IDX_KEY_BLOCK + sub * IDX_SUB + lane_sub
        key_scr[j * nsub + sub] = jnp.where(s_idx <= t_col, _sortable_key(acc), INT_MIN)

    @pl.when(last_tab[s] == 1)
    def _():
        n_chunks = (i * Q_BLOCK + Q_BLOCK - 1) // IDX_SUB + 1

        def count(pred, ref_val):
            refb = jnp.broadcast_to(ref_val, (Q_BLOCK, LANES))

            def body(c, cnt):
                blk = key_scr[c]
                for l in range(IDX_SUB // LANES):
                    cnt = cnt + jnp.where(pred(blk[:, l * LANES:(l + 1) * LANES], refb), 1, 0)
                return cnt

            cnt = lax.fori_loop(0, n_chunks, body, jnp.zeros((Q_BLOCK, LANES), I32))
            return jnp.sum(cnt, axis=1, keepdims=True)

        tiles_per_chunk = IDX_SUB // LANES
        chunks_per_group = WORD_BITS // tiles_per_chunk
        n_tiles = n_chunks * tiles_per_chunk
        n_groups = (n_tiles + WORD_BITS - 1) // WORD_BITS

        def build_group(g, carry):
            def build_rows(r, carry_r):
                r8 = pl.multiple_of(r * 8, 8)
                a = [key_scr[g * chunks_per_group + t // tiles_per_chunk, pl.ds(r8, 8),
                             (t % tiles_per_chunk) * LANES:(t % tiles_per_chunk + 1) * LANES]
                     for t in range(WORD_BITS)]
                m, sh = 0x0000FFFF, 16
                while sh:
                    k = 0
                    while k < WORD_BITS:
                        x = (a[k] ^ lax.shift_right_logical(a[k + sh], jnp.int32(sh))) & jnp.int32(m)
                        a[k] = a[k] ^ x
                        a[k + sh] = a[k + sh] ^ jnp.left_shift(x, jnp.int32(sh))
                        k = (k + sh + 1) & ~sh
                    sh >>= 1
                    m = (m ^ (m << sh)) & 0xFFFFFFFF if sh else m
                a[0] = ~a[0]
                for p in range(WORD_BITS):
                    plane_scr[p, g, pl.ds(r8, 8), :] = a[p]
                return carry_r

            lax.fori_loop(0, Q_BLOCK // 8, build_rows, 0)
            return carry

        lax.fori_loop(0, n_groups, build_group, 0)
        n_groups_max = plane_scr.shape[1]
        for g in range(n_groups_max):
            n_valid = jnp.clip(n_tiles - g * WORD_BITS, 0, WORD_BITS)
            word = jnp.where(n_valid >= WORD_BITS, jnp.int32(-1),
                             jnp.where(n_valid <= 0, jnp.int32(0),
                                       jnp.left_shift(jnp.int32(-1), WORD_BITS - n_valid)))
            cand_scr[g] = jnp.broadcast_to(word, (Q_BLOCK, LANES))

        def plane_pass(p, thr_u, n_above, n_cand):
            cnt = jnp.zeros((Q_BLOCK, LANES), I32)
            for g in range(n_groups_max):
                w = plane_scr[p, g] & cand_scr[g]
                w_scr[g] = w
                cnt = cnt + lax.population_count(w)
            n_set = jnp.sum(cnt, axis=1, keepdims=True)
            take = n_above + n_set >= k_top
            take_b = jnp.broadcast_to(take, (Q_BLOCK, LANES))
            for g in range(n_groups_max):
                w = w_scr[g]
                cand_scr[g] = jnp.where(take_b, w, cand_scr[g] ^ w)
            bit = jnp.left_shift(jnp.int32(1), WORD_BITS - 1 - p)
            return (jnp.where(take, thr_u | bit, thr_u), jnp.where(take, n_above, n_above + n_set),
                    jnp.where(take, n_set, n_cand - n_set))

        def group_body(carry):
            grp, thr_u, n_above, n_cand, _ = carry
            for bb in range(BITS_PER_CHECK):
                thr_u, n_above, n_cand = plane_pass(grp * BITS_PER_CHECK + bb, thr_u, n_above, n_cand)
            return grp + 1, thr_u, n_above, n_cand, jnp.max(jnp.where(n_above + n_cand != k_top, 1, 0))

        _, thr_u, n_above, n_cand, n_tied_rows = lax.while_loop(
            lambda carry: (carry[0] < WORD_BITS // BITS_PER_CHECK) & (carry[4] != 0), group_body,
            (jnp.int32(0), jnp.zeros((Q_BLOCK, 1), I32), jnp.zeros((Q_BLOCK, 1), I32),
             jnp.broadcast_to(n_tiles * LANES, (Q_BLOCK, 1)).astype(I32), jnp.int32(1)))
        thr = thr_u ^ jnp.int32(INT_MIN)
        thr_b = jnp.broadcast_to(thr, (Q_BLOCK, IDX_SUB))

        def emit(c, sel):
            s_idx = c * IDX_SUB + lane_sub
            far = sel & (t_col - s_idx >= MAX_DISTANCE)
            far_ref[0, c] = jnp.where(far, 0.0, FAR_MASKED).astype(far_ref.dtype)
            key_scr[c] = jnp.where(sel, 1, 0)

        @pl.when(n_tied_rows == 0)
        def _():
            def sel_body(c, carry):
                emit(c, key_scr[c] >= thr_b)
                return carry

            lax.fori_loop(0, n_chunks, sel_body, 0)

        @pl.when(n_tied_rows != 0)
        def _():
            n_gt = count(lambda a, r: a > r, thr)
            need_eq = (k_top - n_gt).astype(F32)
            incl = (lax.broadcasted_iota(I32, (IDX_SUB, IDX_SUB), 0)
                    <= lax.broadcasted_iota(I32, (IDX_SUB, IDX_SUB), 1)).astype(BF16)

            def sel_body(c, carry):
                blk = key_scr[c]
                eq = blk == thr_b
                eq_f = jnp.where(eq, 1.0, 0.0)
                rank = carry + jnp.dot(eq_f.astype(BF16), incl, preferred_element_type=F32)
                s_idx = c * IDX_SUB + lane_sub
                emit(c, ((blk > thr_b) | (eq & (rank <= need_eq))) & (s_idx <= t_col))
                return carry + jnp.sum(eq_f, axis=1, keepdims=True)

            lax.fori_loop(0, n_chunks, sel_body, jnp.zeros((Q_BLOCK, 1), F32))

        def fill_body(c, carry):
            far_ref[0, c] = jnp.full((Q_BLOCK, IDX_SUB), FAR_MASKED, far_ref.dtype)
            return carry

        lax.fori_loop(n_chunks, n_sub_total, fill_body, 0)

        def window(blk_idx):
            per = IDX_SUB // Q_BLOCK
            chunk = key_scr[blk_idx // per]
            m = blk_idx % per
            out = chunk[:, 0:Q_BLOCK]
            for q in range(1, per):
                out = jnp.where(m == q, chunk[:, q * Q_BLOCK:(q + 1) * Q_BLOCK], out)
            return out

        r_i = lax.broadcasted_iota(I32, (Q_BLOCK, Q_BLOCK), 0)
        c_i = lax.broadcasted_iota(I32, (Q_BLOCK, Q_BLOCK), 1)
        d_lo = Q_BLOCK + r_i - c_i
        d_hi = r_i - c_i
        near_lo = (window(jnp.maximum(i - 1, 0)) != 0) & (d_lo < MAX_DISTANCE) & (i >= 1)
        near_hi = (window(i) != 0) & (d_hi >= 0) & (d_hi < MAX_DISTANCE)
        near_ref[:, 0:Q_BLOCK] = jnp.where(near_lo, 0.0, -jnp.inf).astype(near_ref.dtype)
        near_ref[:, Q_BLOCK:2 * Q_BLOCK] = jnp.where(near_hi, 0.0, -jnp.inf).astype(near_ref.dtype)


def _idx_select(proj, ki, wi, k_top):
    L = ki.shape[0]
    nq = L // Q_BLOCK
    n_sub_total = L // IDX_SUB
    qi_l, kj_l, last_l = [], [], []
    for i in range(nq):
        j_last = (i * Q_BLOCK + Q_BLOCK - 1) // IDX_KEY_BLOCK
        for j in range(j_last + 1):
            qi_l.append(i)
            kj_l.append(j)
            last_l.append(1 if j == j_last else 0)
    tabs = [jnp.asarray(np.asarray(t, np.int32)) for t in (qi_l, kj_l, last_l)]
    qcol = (DSA_HEADS * DSA_DH) // (IDX_HEADS * IDX_DIM)
    group_keys = WORD_BITS * LANES
    n_groups_max = -(-L // group_keys)
    key_chunks = n_groups_max * (group_keys // IDX_SUB)
    vmem = ((key_chunks * IDX_SUB + (WORD_BITS + 2) * n_groups_max * LANES) * Q_BLOCK * 4
            + 2 * (Q_BLOCK * L * 2) + 2 * (Q_BLOCK * IDX_HEADS * IDX_DIM * 2)
            + 2 * IDX_KEY_BLOCK * IDX_DIM * 2 + (8 << 20))
    grid_spec = pltpu.PrefetchScalarGridSpec(
        num_scalar_prefetch=3,
        grid=(len(qi_l),),
        in_specs=[pl.BlockSpec((Q_BLOCK, IDX_HEADS * IDX_DIM), lambda s, qt, kt, lt: (qt[s], qcol)),
                  pl.BlockSpec((IDX_KEY_BLOCK, IDX_DIM), lambda s, qt, kt, lt: (kt[s], 0)),
                  pl.BlockSpec((Q_BLOCK, LANES), lambda s, qt, kt, lt: (qt[s], 0))],
        out_specs=[pl.BlockSpec((1, n_sub_total, Q_BLOCK, IDX_SUB), lambda s, qt, kt, lt: (qt[s], 0, 0, 0)),
                   pl.BlockSpec((Q_BLOCK, 2 * Q_BLOCK), lambda s, qt, kt, lt: (qt[s], 0))],
        scratch_shapes=[pltpu.VMEM((key_chunks, Q_BLOCK, IDX_SUB), I32),
                        pltpu.VMEM((WORD_BITS, n_groups_max, Q_BLOCK, LANES), I32),
                        pltpu.VMEM((n_groups_max, Q_BLOCK, LANES), I32),
                        pltpu.VMEM((n_groups_max, Q_BLOCK, LANES), I32)],
    )
    return pl.pallas_call(
        functools.partial(_idx_kernel, k_top=k_top, n_sub_total=n_sub_total),
        grid_spec=grid_spec,
        out_shape=[jax.ShapeDtypeStruct((nq, n_sub_total, Q_BLOCK, IDX_SUB), BF16),
                   jax.ShapeDtypeStruct((L, 2 * Q_BLOCK), BF16)],
        compiler_params=_cparams(("arbitrary",), vmem),
        name="dsa_idx_select",
    )(*tabs, proj, ki, wi)


M_INIT = -1e30
FAR_SUBS = 2
LOG2E = math.log2(math.e)


def _attn_kernel(qi_tab, kj_tab, kind_tab, first_tab,
                 tab_ref, q_ref, kt_ref, vf_ref, klo_ref, khi_ref, vlo_ref, vhi_ref, far_ref, near_ref,
                 o_ref, m_scr, l_scr, acc_scr, b_scr, s_scr):
    s = pl.program_id(0)

    @pl.when(s == 0)
    def _():
        r_i = lax.broadcasted_iota(I32, (Q_BLOCK, 2 * Q_BLOCK), 0)
        c_i = lax.broadcasted_iota(I32, (Q_BLOCK, 2 * Q_BLOCK), 1)
        d = jnp.maximum(Q_BLOCK + r_i - c_i, 0)
        max_exact = N_BUCKETS // 2
        df = jnp.maximum(d, 1).astype(F32)
        large = max_exact + (jnp.log(df / max_exact) / math.log(MAX_DISTANCE / max_exact)
                             * (N_BUCKETS - max_exact)).astype(I32)
        large = jnp.minimum(large, N_BUCKETS - 1)
        bkt = jnp.where(d < max_exact, d, large)
        for h in range(DSA_HEADS):
            acc = jnp.zeros((Q_BLOCK, 2 * Q_BLOCK), F32)
            for b in range(N_BUCKETS):
                acc = jnp.where(bkt == b, (tab_ref[b, h] - tab_ref[N_BUCKETS - 1, h]) * LOG2E, acc)
            b_scr[h] = acc

    @pl.when(first_tab[s] == 1)
    def _():
        m_scr[...] = jnp.full_like(m_scr, M_INIT)
        l_scr[...] = jnp.zeros_like(l_scr)
        acc_scr[...] = jnp.zeros_like(acc_scr)

    def softmax_pv(v_ref, width, row0=0):
        nt = width // LANES
        rows = slice(row0, row0 + width)
        ones = jnp.ones((width, DSA_DH), BF16)
        for g in range(DSA_KV_HEADS):
            heads = range(g * DSA_GROUP, (g + 1) * DSA_GROUP)
            alphas, pbs = [], []
            for h in heads:
                tiles = [s_scr[h, :, t * LANES:(t + 1) * LANES] for t in range(nt)]
                tmax = tiles[0]
                for t in range(1, nt):
                    tmax = jnp.maximum(tmax, tiles[t])
                m_prev = m_scr[h]
                m_new = jnp.maximum(m_prev, jnp.max(tmax, axis=-1, keepdims=True))
                alphas.append(jnp.exp2(m_prev - m_new))
                p = [jnp.exp2(tiles[t] - m_new).astype(BF16) for t in range(nt)]
                pbs.append(jnp.concatenate(p, axis=-1) if nt > 1 else p[0])
                m_scr[h] = m_new
            v_aug = jnp.concatenate([v_ref[rows, g * DSA_DH:(g + 1) * DSA_DH], ones], axis=-1)
            pv = jnp.dot(jnp.concatenate(pbs, axis=0), v_aug,
                         preferred_element_type=F32)
            for n, h in enumerate(heads):
                pv_h = pv[n * Q_BLOCK:(n + 1) * Q_BLOCK]
                acc_scr[h] = alphas[n] * acc_scr[h] + pv_h[:, :DSA_DH]
                l_scr[h] = alphas[n] * l_scr[h] + pv_h[:, DSA_DH:]

    @pl.when(kind_tab[s] == 0)
    def _():
        eye = (lax.broadcasted_iota(I32, (Q_BLOCK, Q_BLOCK), 0)
               == lax.broadcasted_iota(I32, (Q_BLOCK, Q_BLOCK), 1)).astype(BF16)
        for sub in range(FAR_SUBS):
            cols = slice(sub * IDX_SUB, (sub + 1) * IDX_SUB)
            mask = far_ref[0, sub]
            for g in range(DSA_KV_HEADS):
                lhs = jnp.concatenate(
                    [jnp.concatenate([q_ref[:, h * DSA_DH:(h + 1) * DSA_DH], eye], axis=1)
                     for h in range(g * DSA_GROUP, (g + 1) * DSA_GROUP)], axis=0)
                rhs = jnp.concatenate([kt_ref[g * DSA_DH:(g + 1) * DSA_DH, cols], mask], axis=0)
                logits = jnp.dot(lhs, rhs, preferred_element_type=F32)
                for hh in range(DSA_GROUP):
                    s_scr[g * DSA_GROUP + hh] = logits[hh * Q_BLOCK:(hh + 1) * Q_BLOCK, :]
            softmax_pv(vf_ref, IDX_SUB, sub * IDX_SUB)

    @pl.when(kind_tab[s] == 1)
    def _():
        for half, (k_ref, v_ref) in enumerate(((klo_ref, vlo_ref), (khi_ref, vhi_ref))):
            cols = slice(half * Q_BLOCK, (half + 1) * Q_BLOCK)
            mask = near_ref[:, cols].astype(F32)
            for h in range(DSA_HEADS):
                g = h // DSA_GROUP
                logits = lax.dot_general(q_ref[:, h * DSA_DH:(h + 1) * DSA_DH],
                                         k_ref[:, g * DSA_DH:(g + 1) * DSA_DH], _NT, preferred_element_type=F32)
                s_scr[h, :, 0:Q_BLOCK] = logits + b_scr[h][:, cols] + mask
            softmax_pv(v_ref, Q_BLOCK)
        for h in range(DSA_HEADS):
            o_ref[:, h * DSA_DH:(h + 1) * DSA_DH] = (acc_scr[h] / l_scr[h]).astype(o_ref.dtype)


def _masked_attention(proj, far, near, rel_bias):
    L = proj.shape[0]
    nq = L // Q_BLOCK
    far_keys = FAR_SUBS * IDX_SUB
    per = far_keys // Q_BLOCK
    qi_l, kj_l, kind_l, first_l = [], [], [], []
    for i in range(nq):
        n_far = -(-i // per)
        for j in range(n_far):
            qi_l.append(i); kj_l.append(j); kind_l.append(0); first_l.append(1 if j == 0 else 0)
        qi_l.append(i); kj_l.append(max(n_far - 1, 0)); kind_l.append(1); first_l.append(1 if n_far == 0 else 0)
    tabs = [jnp.asarray(np.asarray(t, np.int32)) for t in (qi_l, kj_l, kind_l, first_l)]
    qw = DSA_HEADS * DSA_DH
    kvw = DSA_KV_HEADS * DSA_DH
    k_col = (2 * qw) // kvw
    v_col = k_col + 1
    hw = DSA_HEADS
    vmem = (2 * (Q_BLOCK * qw * 2 * 2 + 2 * far_keys * kvw * 2 + 4 * Q_BLOCK * kvw * 2
                 + Q_BLOCK * far_keys * 2 + Q_BLOCK * 2 * Q_BLOCK * 2)
            + hw * Q_BLOCK * (3 * LANES + 2 * Q_BLOCK) * 4 + (16 << 20))
    idx = lambda f: (lambda s, qt, kt, kd, ft: f(qt[s], kt[s]))
    k_t = proj[:, k_col * kvw:(k_col + 1) * kvw].T
    grid_spec = pltpu.PrefetchScalarGridSpec(
        num_scalar_prefetch=4,
        grid=(len(qi_l),),
        in_specs=[pl.BlockSpec(memory_space=pltpu.SMEM),
                  pl.BlockSpec((Q_BLOCK, qw), idx(lambda i, j: (i, 0))),
                  pl.BlockSpec((kvw, far_keys), idx(lambda i, j: (0, j))),
                  pl.BlockSpec((far_keys, kvw), idx(lambda i, j: (j, v_col))),
                  pl.BlockSpec((Q_BLOCK, kvw), idx(lambda i, j: (jnp.maximum(i - 1, 0), k_col))),
                  pl.BlockSpec((Q_BLOCK, kvw), idx(lambda i, j: (i, k_col))),
                  pl.BlockSpec((Q_BLOCK, kvw), idx(lambda i, j: (jnp.maximum(i - 1, 0), v_col))),
                  pl.BlockSpec((Q_BLOCK, kvw), idx(lambda i, j: (i, v_col))),
                  pl.BlockSpec((1, FAR_SUBS, Q_BLOCK, IDX_SUB), idx(lambda i, j: (i, j, 0, 0))),
                  pl.BlockSpec((Q_BLOCK, 2 * Q_BLOCK), idx(lambda i, j: (i, 0)))],
        out_specs=pl.BlockSpec((Q_BLOCK, qw), idx(lambda i, j: (i, 0))),
        scratch_shapes=[pltpu.VMEM((hw, Q_BLOCK, LANES), F32),
                        pltpu.VMEM((hw, Q_BLOCK, LANES), F32),
                        pltpu.VMEM((hw, Q_BLOCK, DSA_DH), F32),
                        pltpu.VMEM((hw, Q_BLOCK, 2 * Q_BLOCK), F32),
                        pltpu.VMEM((hw, Q_BLOCK, IDX_SUB), F32)],
    )
    return pl.pallas_call(
        _attn_kernel,
        grid_spec=grid_spec,
        out_shape=jax.ShapeDtypeStruct((L, qw), BF16),
        compiler_params=_cparams(("arbitrary",), vmem),
        name="dsa_attention",
    )(*tabs, rel_bias, proj, k_t, proj, proj, proj, proj, proj, far, near)


def _dsa_mixer(xb, w_in_all, layer, ln_g, ln_b, rel_bias):
    L = xb.shape[0]
    k_top = min(TOPK_MAX, L // 4)
    sq = DSA_HEADS * DSA_DH
    skv = DSA_KV_HEADS * DSA_DH
    si = IDX_HEADS * IDX_DIM
    w_ki = w_in_all[layer, :, sq + 2 * skv + si:sq + 2 * skv + si + IDX_DIM]
    w_wi = w_in_all[layer, :, sq + 2 * skv + si + IDX_DIM:]
    colscale = jnp.concatenate([jnp.full((1, sq), DSA_DH ** -0.5 * LOG2E, F32),
                                jnp.ones((1, si + 2 * skv), F32)], axis=1)
    w_small = jnp.concatenate([w_ki, w_wi, jnp.zeros((D_MODEL, LANES - IDX_HEADS), F32)], axis=1).astype(BF16)
    tn = 2 * skv
    nq, ni = sq // tn, si // tn
    src_block = lambda j: jnp.where(j < nq, j, jnp.where(j < nq + ni, j + 1, nq))
    proj = _matmul_scaled(xb, jnp.swapaxes(w_in_all, 1, 2), colscale, BF16, tm=TILE_M, tn=tn,
                          w_block=lambda j: (layer, 0, src_block(j)), w_t=True)
    ki, wi = _dsa_small(xb, w_small, ln_g, ln_b, tm=TILE_M)
    far, near = _idx_select(proj, ki, wi, k_top)
    return _masked_attention(proj, far, near, rel_bias)


def kernel(x, p, gdn_w_in, gdn_conv_w, gdn_a_log, gdn_dt_bias, gdn_norm_g, gdn_w_o, dsa_w_in, dsa_kidx_ln_g, dsa_kidx_ln_b, dsa_w_o, rel_bias, ln1_g, ln1_b, ffn_w_gate, ffn_w_up, ffn_conv_w, ffn_w_down, ln2_g, ln2_b, ple_w_proj, ple_w_gate):
    assert x.shape[0] == 1 and x.shape[2] == D_MODEL
    xf = x[0]
    xb = xf.astype(BF16)
    ple_gate_bf = ple_w_gate.astype(BF16)
    ffn_down_bf = ffn_w_down.astype(BF16)
    gdn_o_bf = gdn_w_o.astype(BF16)
    dsa_o_bf = dsa_w_o.astype(BF16)
    ia = ib = 0
    for i in range(DEPTH):
        if i % 2 == 0:
            mix = _gdn_mixer(xb, gdn_w_in, ia, gdn_conv_w[ia], gdn_a_log[ia], gdn_dt_bias[ia], gdn_norm_g[ia])
            w_o, lo = gdn_o_bf, ia
            ia += 1
        else:
            mix = _dsa_mixer(xb, dsa_w_in, ib, dsa_kidx_ln_g[ib], dsa_kidx_ln_b[ib], rel_bias)
            w_o, lo = dsa_o_bf, ib
            ib += 1
        xf, xb = _proj_res_ln(mix, w_o, lo, xf, ln1_g[i], ln1_b[i], tm=LN_TILE_M, sub=LN_SUB_M)
        hmid = _ffn_up(xb, ffn_w_gate, ffn_w_up, ffn_conv_w[i], i, tm=TILE_M, tn=FFN_TILE_N)
        xf, xb = _proj_res_ln(hmid, ffn_down_bf, i, xf, ln2_g[i], ln2_b[i], tm=LN_TILE_M, sub=LN_SUB_M)
        xf, xb = _ple(xb, xf, ple_gate_bf, p, ple_w_proj, i, tm=PLE_TILE_M, tn=D_MODEL)
    return xf[None]
```
